```python
import math
import jax, jax.numpy as jnp
from jax import lax
import numpy as np

D_MODEL = 1024
BATCH = 16
SEQ = 2048
DEPTH = 2

N_META = 16
CHUNK = 64
Q_BLOCK = 128
FRONT_PAD = Q_BLOCK - N_META
EPS = 1e-6
NEG = -1e30
ROPE_THETA = 10000.0

GROUP_W = D_MODEL // 4
GLA_HEADS = 4
GLA_DK = GROUP_W // GLA_HEADS // 2
GLA_DV = GROUP_W // GLA_HEADS
GLA_GATE_RANK = 16
GLA_TAU = 16.0
RET_HEADS = 4
RET_DK = GROUP_W // RET_HEADS
RET_DV = GROUP_W // RET_HEADS
MLA_HEADS = 4
MLA_Q_RANK = 192
MLA_KV_RANK = 64
MLA_NOPE = 64
MLA_ROPE = 32
MLA_DV = GROUP_W // MLA_HEADS
DIFF_HEADS = 4
DIFF_DK = 32
DIFF_DV = GROUP_W // DIFF_HEADS
D_FF = 2816
N_EXPERTS = 8
TOP_K = 2
D_FF_EXPERT = 3584
N_DENSE = (DEPTH + 1) // 2
N_MOE = DEPTH // 2

SPLIT_SIZES = (
    GLA_HEADS * GLA_DK, GLA_HEADS * GLA_DK, GLA_HEADS * GLA_DV, GLA_GATE_RANK, GLA_HEADS * GLA_DV,
    RET_HEADS * RET_DK, RET_HEADS * RET_DK, RET_HEADS * RET_DV, RET_HEADS * RET_DV,
    MLA_Q_RANK, MLA_KV_RANK, MLA_ROPE,
    DIFF_HEADS * 2 * DIFF_DK, DIFF_HEADS * 2 * DIFF_DK, DIFF_HEADS * DIFF_DV,
)
D_IN = sum(SPLIT_SIZES)

kernel_name = "hybrid_parallel_heads_gla_ret_mla_diff_moe"


def rmsnorm(x, g):
    xf = x.astype(jnp.float32)
    y = xf * lax.rsqrt(jnp.mean(xf * xf, axis=-1, keepdims=True) + EPS)
    return (y * g.astype(jnp.float32)).astype(x.dtype)


def head_groupnorm(x, g):
    xf = x.astype(jnp.float32)
    mu = jnp.mean(xf, axis=-1, keepdims=True)
    xc = xf - mu
    y = xc * lax.rsqrt(jnp.mean(xc * xc, axis=-1, keepdims=True) + EPS)
    return (y * g.astype(jnp.float32)).astype(x.dtype)


def rope(x, pos):
    half = x.shape[-1] // 2
    inv = ROPE_THETA ** (-jnp.arange(half, dtype=jnp.float32) / half)
    ang = pos.astype(jnp.float32)[:, None] * inv[None, :]
    cos, sin = jnp.cos(ang), jnp.sin(ang)
    xf = x.astype(jnp.float32)
    x1, x2 = xf[..., :half], xf[..., half:]
    return jnp.concatenate([x1 * cos - x2 * sin, x1 * sin + x2 * cos], axis=-1).astype(x.dtype)


def to_heads(t, n_heads):
    b, t_len, _ = t.shape
    return t.reshape(b, t_len, n_heads, -1).transpose(0, 2, 1, 3)


def from_heads(t):
    b, h, t_len, d = t.shape
    return t.transpose(0, 2, 1, 3).reshape(b, t_len, h * d)


def chunk_state_scan(decay, update):
    def step(s, inp):
        d, u = inp
        return d[..., None] * s + u, s
    s0 = jnp.zeros(update.shape[1:], jnp.float32)
    _, s_prev = lax.scan(step, s0, (decay, update))
    return s_prev


def gla_chunked(q, k, v, g):
    b, h, t_len, dk = q.shape
    dv = v.shape[-1]
    n = t_len // CHUNK
    q = q.astype(jnp.float32).reshape(b, h, n, CHUNK, dk)
    k = k.astype(jnp.float32).reshape(b, h, n, CHUNK, dk)
    vf = v.astype(jnp.float32).reshape(b, h, n, CHUNK, dv)
    cum = jnp.cumsum(g.astype(jnp.float32).reshape(b, h, n, CHUNK, dk), axis=-2)
    cum_last = cum[..., -1:, :]
    q_t = q * jnp.exp(cum)
    k_t = k * jnp.exp(-cum)
    tri = jnp.tril(jnp.ones((CHUNK, CHUNK), bool))
    a = jnp.where(tri, jnp.einsum('bhncd,bhnsd->bhncs', q_t, k_t), 0.0)
    o_intra = jnp.einsum('bhncs,bhnse->bhnce', a, vf)
    upd = jnp.einsum('bhncd,bhnce->nbhde', k * jnp.exp(cum_last - cum), vf)
    dec = jnp.exp(cum_last[..., 0, :]).transpose(2, 0, 1, 3)
    s_prev = chunk_state_scan(dec, upd)
    o_inter = jnp.einsum('bhncd,nbhde->bhnce', q_t, s_prev)
    return (o_intra + o_inter).reshape(b, h, t_len, dv).astype(v.dtype)


def retention_chunked(q, k, v):
    b, h, t_len, dk = q.shape
    dv = v.shape[-1]
    n = t_len // CHUNK
    lg = jnp.log(1.0 - jnp.exp2(-5.0 - jnp.arange(h, dtype=jnp.float32)))
    idx = jnp.arange(CHUNK, dtype=jnp.float32)
    rel = idx[:, None] - idx[None, :]
    dmask = jnp.where(rel[None] >= 0, jnp.exp(rel[None] * lg[:, None, None]), 0.0)
    qfac = jnp.exp((idx[None, :] + 1.0) * lg[:, None])
    kfac = jnp.exp((CHUNK - 1.0 - idx[None, :]) * lg[:, None])
    q = q.astype(jnp.float32).reshape(b, h, n, CHUNK, dk)
    k = k.astype(jnp.float32).reshape(b, h, n, CHUNK, dk)
    vf = v.astype(jnp.float32).reshape(b, h, n, CHUNK, dv)
    a = jnp.einsum('bhncd,bhnsd->bhncs', q, k) * dmask[None, :, None]
    o_intra = jnp.einsum('bhncs,bhnse->bhnce', a, vf)
    upd = jnp.einsum('bhncd,hc,bhnce->nbhde', k, kfac, vf)
    dec = jnp.broadcast_to(jnp.exp(CHUNK * lg)[None, None, :, None], (n, b, h, dk))
    s_prev = chunk_state_scan(dec, upd)
    o_inter = jnp.einsum('bhncd,hc,nbhde->bhnce', q, qfac, s_prev)
    return (o_intra + o_inter).reshape(b, h, t_len, dv).astype(v.dtype)


def signed_softmax_attention(q, k, v, valid, scale, coef):
    b, h, m, t_len, dk = q.shape
    dv = v.shape[-1]
    nb = t_len // Q_BLOCK
    qb = q.reshape(b, h, m, nb, Q_BLOCK, dk).transpose(3, 0, 1, 2, 4, 5)
    kpos = jnp.arange(t_len)
    coef = coef.astype(jnp.float32)

    def block(args):
        qi, i = args
        qpos = i * Q_BLOCK + jnp.arange(Q_BLOCK)
        mask = (kpos[None, :] <= qpos[:, None]) & valid[None, :]
        s = jnp.einsum('bhmqd,bhmkd->bhmqk', qi, k).astype(jnp.float32) * scale
        p = jax.nn.softmax(jnp.where(mask, s, NEG), axis=-1)
        w = jnp.einsum('bhmqk,m->bhqk', p, coef)
        return jnp.einsum('bhqk,bhkd->bhqd', w.astype(v.dtype), v)

    out = lax.map(block, (qb, jnp.arange(nb)))
    return out.transpose(1, 2, 0, 3, 4).reshape(b, h, t_len, dv)


def hybrid_mixer(hn, li, w_in, gla_w_gate, gla_b_gate, gla_norm, ret_norm,
                 mla_q_norm, mla_w_uq, mla_kv_norm, mla_w_ukv, diff_lambda, diff_norm, w_out):
    b, t_len, _ = hn.shape
    tp = t_len + FRONT_PAD
    proj = jnp.pad(hn @ w_in, ((0, 0), (FRONT_PAD, 0), (0, 0)))
    offs = [int(o) for o in np.cumsum(SPLIT_SIZES)[:-1]]
    (a_q, a_k, a_v, a_lr, a_og, r_q, r_k, r_v, r_og,
     c_cq, c_ckv, c_kpe, d_q, d_k, d_v) = jnp.split(proj, offs, axis=-1)
    tpos = jnp.arange(tp)
    valid = tpos >= FRONT_PAD
    pos = tpos - FRONT_PAD
    kmask = valid[None, None, :, None].astype(proj.dtype)

    g = jax.nn.log_sigmoid((a_lr @ gla_w_gate + gla_b_gate).astype(jnp.float32)) / GLA_TAU
    g = jnp.where(valid[None, :, None], g, 0.0)
    o_a = gla_chunked(to_heads(a_q, GLA_HEADS) * (GLA_DK ** -0.5), to_heads(a_k, GLA_HEADS) * kmask,
                      to_heads(a_v, GLA_HEADS), to_heads(g, GLA_HEADS))
    o_a = from_heads(rmsnorm(o_a, gla_norm)) * jax.nn.silu(a_og)

    rq = rope(to_heads(r_q, RET_HEADS), pos)
    rk = rope(to_heads(r_k, RET_HEADS), pos) * (RET_DK ** -0.5) * kmask
    o_b = retention_chunked(rq, rk, to_heads(r_v, RET_HEADS))
    o_b = from_heads(head_groupnorm(o_b, ret_norm)) * jax.nn.silu(r_og)

    cq = (rmsnorm(c_cq, mla_q_norm) @ mla_w_uq).reshape(b, tp, MLA_HEADS, MLA_NOPE + MLA_ROPE).transpose(0, 2, 1, 3)
    q_c = jnp.concatenate([cq[..., :MLA_NOPE], rope(cq[..., MLA_NOPE:], pos)], axis=-1)
    kv = (rmsnorm(c_ckv, mla_kv_norm) @ mla_w_ukv).reshape(b, tp, MLA_HEADS, MLA_NOPE + MLA_DV).transpose(0, 2, 1, 3)
    k_pe = jnp.broadcast_to(rope(c_kpe[:, None], pos), (b, MLA_HEADS, tp, MLA_ROPE))
    k_c = jnp.concatenate([kv[..., :MLA_NOPE], k_pe], axis=-1)
    o_c = signed_softmax_attention(q_c[:, :, None], k_c[:, :, None], kv[..., MLA_NOPE:], valid,
                                   (MLA_NOPE + MLA_ROPE) ** -0.5, jnp.ones((1,), jnp.float32))
    o_c = from_heads(o_c)

    lam_init = 0.8 - 0.6 * math.exp(-0.3 * li)
    lv = diff_lambda.astype(jnp.float32)
    lam = jnp.exp(jnp.sum(lv[0] * lv[1])) - jnp.exp(jnp.sum(lv[2] * lv[3])) + lam_init
    qd = d_q.reshape(b, tp, DIFF_HEADS, 2, DIFF_DK).transpose(0, 2, 3, 1, 4)
    kd = d_k.reshape(b, tp, DIFF_HEADS, 2, DIFF_DK).transpose(0, 2, 3, 1, 4)
    o_d = signed_softmax_attention(qd, kd, to_heads(d_v, DIFF_HEADS), valid, DIFF_DK ** -0.5,
                                   jnp.stack([jnp.ones((), jnp.float32), -lam]))
    o_d = from_heads(rmsnorm(o_d, diff_norm)) * (1.0 - lam_init)

    o = jnp.concatenate([o_a, o_b, o_c, o_d], axis=-1)[:, FRONT_PAD:]
    return o @ w_out


def swiglu(h, wg, wu, wd):
    return (jax.nn.silu(h @ wg) * (h @ wu)) @ wd


def moe_swiglu(h, router, wg, wu, wd):
    logits = (h @ router).astype(jnp.float32)
    top_v, top_i = lax.top_k(logits, TOP_K)
    top_w = jax.nn.softmax(top_v, axis=-1)
    gates = jnp.sum(jax.nn.one_hot(top_i, N_EXPERTS, dtype=jnp.float32) * top_w[..., None], axis=-2)
    out = jnp.zeros_like(h)
    for e in range(N_EXPERTS):
        out = out + gates[..., e:e + 1].astype(h.dtype) * swiglu(h, wg[e], wu[e], wd[e])
    return out


def setup_inputs(seed: int = 0) -> dict:
    key = jax.random.key(seed)
    ks = jax.random.split(key, 32)
    f32 = jnp.float32

    def nrm(k, shape, fan_in):
        return jax.random.normal(k, shape, f32) * (fan_in ** -0.5)

    def gain(k, shape):
        return 1.0 + 0.01 * jax.random.normal(k, shape, f32)

    return {
        "x": jax.random.normal(ks[0], (BATCH, SEQ, D_MODEL), f32),
        "meta_tokens": jax.random.normal(ks[1], (N_META, D_MODEL), f32),
        "attn_norm": gain(ks[2], (DEPTH, D_MODEL)),
        "w_in": nrm(ks[3], (DEPTH, D_MODEL, D_IN), D_MODEL),
        "gla_w_gate": nrm(ks[4], (DEPTH, GLA_GATE_RANK, GLA_HEADS * GLA_DK), GLA_GATE_RANK),
        "gla_b_gate": 0.01 * jax.random.normal(ks[5], (DEPTH, GLA_HEADS * GLA_DK), f32),
        "gla_norm": gain(ks[6], (DEPTH, GLA_DV)),
        "ret_norm": gain(ks[7], (DEPTH, RET_DV)),
        "mla_q_norm": gain(ks[8], (DEPTH, MLA_Q_RANK)),
        "mla_w_uq": nrm(ks[9], (DEPTH, MLA_Q_RANK, MLA_HEADS * (MLA_NOPE + MLA_ROPE)), MLA_Q_RANK),
        "mla_kv_norm": gain(ks[10], (DEPTH, MLA_KV_RANK)),
        "mla_w_ukv": nrm(ks[11], (DEPTH, MLA_KV_RANK, MLA_HEADS * (MLA_NOPE + MLA_DV)), MLA_KV_RANK),
        "diff_lambda": 0.1 * jax.random.normal(ks[12], (DEPTH, 4, DIFF_DK), f32),
        "diff_norm": gain(ks[13], (DEPTH, DIFF_DV)),
        "w_out": nrm(ks[14], (DEPTH, D_MODEL, D_MODEL), D_MODEL),
        "ffn_norm": gain(ks[15], (DEPTH, D_MODEL)),
        "ffn_w_gate": nrm(ks[16], (N_DENSE, D_MODEL, D_FF), D_MODEL),
        "ffn_w_up": nrm(ks[17], (N_DENSE, D_MODEL, D_FF), D_MODEL),
        "ffn_w_down": nrm(ks[18], (N_DENSE, D_FF, D_MODEL), D_FF),
        "moe_router": nrm(ks[19], (N_MOE, D_MODEL, N_EXPERTS), D_MODEL),
        "moe_w_gate": nrm(ks[20], (N_MOE, N_EXPERTS, D_MODEL, D_FF_EXPERT), D_MODEL),
        "moe_w_up": nrm(ks[21], (N_MOE, N_EXPERTS, D_MODEL, D_FF_EXPERT), D_MODEL),
        "moe_w_down": nrm(ks[22], (N_MOE, N_EXPERTS, D_FF_EXPERT, D_MODEL), D_FF_EXPERT),
        "final_norm": gain(ks[23], (D_MODEL,)),
    }


def reference(x, meta_tokens, attn_norm, w_in, gla_w_gate, gla_b_gate, gla_norm, ret_norm,
              mla_q_norm, mla_w_uq, mla_kv_norm, mla_w_ukv, diff_lambda, diff_norm, w_out,
              ffn_norm, ffn_w_gate, ffn_w_up, ffn_w_down, moe_router, moe_w_gate, moe_w_up,
              moe_w_down, final_norm):
    b = x.shape[0]
    meta = jnp.broadcast_to(meta_tokens[None].astype(x.dtype), (b, N_META, x.shape[-1]))
    h = jnp.concatenate([meta, x], axis=1)
    for li in range(DEPTH):
        h = h + hybrid_mixer(rmsnorm(h, attn_norm[li]), li, w_in[li], gla_w_gate[li], gla_b_gate[li],
                             gla_norm[li], ret_norm[li], mla_q_norm[li], mla_w_uq[li], mla_kv_norm[li],
                             mla_w_ukv[li], diff_lambda[li], diff_norm[li], w_out[li])
        hn = rmsnorm(h, ffn_norm[li])
        j = li // 2
        if li % 2 == 0:
            h = h + swiglu(hn, ffn_w_gate[j], ffn_w_up[j], ffn_w_down[j])
        else:
            h = h + moe_swiglu(hn, moe_router[j], moe_w_gate[j], moe_w_up[j], moe_w_down[j])
    h = rmsnorm(h, final_norm)
    return h[:, N_META:]
```

```python
import functools
import math

import jax
import jax.numpy as jnp
from jax import lax
from jax.experimental import pallas as pl
from jax.experimental.pallas import tpu as pltpu

F32 = jnp.float32
BF16 = jnp.bfloat16

D_MODEL = 1024
DEPTH = 2
N_META = 16
CHUNK = 64
Q_BLOCK = 128
FRONT_PAD = Q_BLOCK - N_META
EPS = 1e-6
NEG = -1e30
ROPE_THETA = 10000.0
N_HEADS = 4
GLA_DK = 32
GLA_DV = 64
GLA_GATE_RANK = 16
GLA_TAU = 16.0
RET_DK = 64
MLA_Q_RANK = 192
MLA_KV_RANK = 64
MLA_NOPE = 64
MLA_ROPE = 32
DIFF_DK = 32
D_FF = 2816
N_EXPERTS = 8
D_FF_EXPERT = 3584

LANES = 128
ROW_TILE = 512
FF_CHUNK = 256
VMEM_LIMIT = 56 * 1024 * 1024

PA_W = 896
PB_W = 1024
PC_W = 384
PD_W = 1024


def _cparams(n_axes=1):
    return pltpu.CompilerParams(dimension_semantics=("arbitrary",) * n_axes,
                                vmem_limit_bytes=VMEM_LIMIT)


def _resident(shape):
    nd = len(shape)
    return pl.BlockSpec(shape, lambda *_: (0,) * nd, pipeline_mode=pl.Buffered(1))


def _sigmoid(x):
    return 1.0 / (1.0 + jnp.exp(-x))


def _split_bf16(x):
    hi = x.astype(BF16)
    lo = (x - hi.astype(F32)).astype(BF16)
    return hi, lo


def _dot(a, b):
    return jnp.dot(a, b, preferred_element_type=F32)


def _dot_nt(a, b):
    return lax.dot_general(a, b, (((1,), (1,)), ((), ())), preferred_element_type=F32)


def _inproj_kernel(h_ref, g_ref, w_ref, pa_ref, pb_ref, pc_ref, pd_ref):
    x = h_ref[...]
    ms = jnp.mean(x * x, axis=-1, keepdims=True)
    y = (x * lax.rsqrt(ms + EPS) * g_ref[...]).astype(BF16)
    off = 0
    for o_ref, width in ((pa_ref, PA_W), (pb_ref, PB_W), (pc_ref, PC_W), (pd_ref, PD_W)):
        o_ref[...] = _dot(y, w_ref[:, off:off + width]).astype(BF16)
        off += width


def _inproj(h, g, w):
    n = h.shape[0]
    wtot = PA_W + PB_W + PC_W + PD_W
    row = lambda width: pl.BlockSpec((ROW_TILE, width), lambda i: (i, 0))
    return pl.pallas_call(
        _inproj_kernel,
        grid=(n // ROW_TILE,),
        in_specs=[row(D_MODEL), _resident((1, D_MODEL)), _resident((D_MODEL, wtot))],
        out_specs=[row(PA_W), row(PB_W), row(PC_W), row(PD_W)],
        out_shape=[jax.ShapeDtypeStruct((n, w_), BF16) for w_ in (PA_W, PB_W, PC_W, PD_W)],
        compiler_params=_cparams(),
        name="inproj",
    )(h, g, w)


def _outproj_kernel(with_router, h_ref, oa_ref, ob_ref, oc_ref, od_ref, wo_ref, fn_ref, *rest):
    if with_router:
        router_ref, hmid_ref, hn_ref, gate_ref = rest
    else:
        hmid_ref, hn_ref = rest
    o = jnp.concatenate([oa_ref[...], ob_ref[...], oc_ref[...], od_ref[...]], axis=1)
    hm = h_ref[...] + _dot(o, wo_ref[...])
    hmid_ref[...] = hm
    ms = jnp.mean(hm * hm, axis=-1, keepdims=True)
    y = hm * lax.rsqrt(ms + EPS) * fn_ref[...]
    hn_ref[...] = y.astype(BF16)
    if with_router:
        y_hi, y_lo = _split_bf16(y)
        r_hi, r_lo = _split_bf16(router_ref[...])
        logits = _dot(y_hi, r_hi) + _dot(y_hi, r_lo) + _dot(y_lo, r_hi)
        lane = lax.broadcasted_iota(jnp.int32, logits.shape, 1).astype(F32)
        ninf = float("-inf")
        logits = jnp.where(lane < N_EXPERTS, logits, ninf)
        m1 = jnp.max(logits, axis=-1, keepdims=True)
        i1 = jnp.min(jnp.where(logits == m1, lane, float(LANES)), axis=-1, keepdims=True)
        rest_l = jnp.where(lane == i1, ninf, logits)
        m2 = jnp.max(rest_l, axis=-1, keepdims=True)
        i2 = jnp.min(jnp.where(rest_l == m2, lane, float(LANES)), axis=-1, keepdims=True)
        e2 = jnp.exp(m2 - m1)
        den = 1.0 + e2
        gate_ref[...] = jnp.where(lane == i1, 1.0 / den, 0.0) + jnp.where(lane == i2, e2 / den, 0.0)


def _outproj(h, oa, ob, oc, od, wo, fn, router=None):
    n = h.shape[0]
    row = lambda width: pl.BlockSpec((ROW_TILE, width), lambda i: (i, 0))
    in_specs = [row(D_MODEL), row(256), row(256), row(256), row(256),
                _resident((D_MODEL, D_MODEL)), _resident((1, D_MODEL))]
    out_specs = [row(D_MODEL), row(D_MODEL)]
    out_shape = [jax.ShapeDtypeStruct((n, D_MODEL), F32), jax.ShapeDtypeStruct((n, D_MODEL), BF16)]
    args = [h, oa, ob, oc, od, wo, fn]
    if router is not None:
        in_specs.append(_resident((D_MODEL, LANES)))
        out_specs.append(row(LANES))
        out_shape.append(jax.ShapeDtypeStruct((n, LANES), F32))
        args.append(router)
    return pl.pallas_call(
        functools.partial(_outproj_kernel, router is not None),
        grid=(n // ROW_TILE,),
        in_specs=in_specs, out_specs=out_specs, out_shape=out_shape,
        compiler_params=_cparams(),
        name="outproj_router" if router is not None else "outproj",
    )(*args)


def _swiglu_kernel(expert, n_chunks, h_ref, hn_ref, *rest):
    if expert is None:
        wg_ref, wu_ref, wd_ref, o_ref, acc_ref = rest
    else:
        gate_ref, wg_ref, wu_ref, wd_ref, o_ref, acc_ref = rest
    hn = hn_ref[...]
    acc_ref[...] = jnp.zeros_like(acc_ref)

    def body(c, carry):
        g = _dot(hn, wg_ref[c])
        u = _dot(hn, wu_ref[c])
        a = (g * _sigmoid(g) * u).astype(BF16)
        acc_ref[...] += _dot(a, wd_ref[c])
        return carry

    lax.fori_loop(0, n_chunks, body, 0)
    y = acc_ref[...]
    if expert is not None:
        y = gate_ref[:, expert:expert + 1] * y
    o_ref[...] = h_ref[...] + y


def _swiglu(h, hn, wg, wu, wd, gates=None, expert=None):
    n = h.shape[0]
    n_chunks = wg.shape[0]
    row = lambda width: pl.BlockSpec((ROW_TILE, width), lambda i: (i, 0))
    in_specs = [row(D_MODEL), row(D_MODEL)]
    args = [h, hn]
    if expert is not None:
        in_specs.append(row(LANES))
        args.append(gates)
    in_specs += [_resident(wg.shape), _resident(wu.shape), _resident(wd.shape)]
    args += [wg, wu, wd]
    return pl.pallas_call(
        functools.partial(_swiglu_kernel, expert, n_chunks),
        grid=(n // ROW_TILE,),
        in_specs=in_specs,
        out_specs=row(D_MODEL),
        out_shape=jax.ShapeDtypeStruct((n, D_MODEL), F32),
        scratch_shapes=[pltpu.VMEM((ROW_TILE, D_MODEL), F32)],
        input_output_aliases={0: 0},
        compiler_params=_cparams(),
        name="swiglu" if expert is None else f"moe_expert{expert}",
    )(*args)


def _final_norm_kernel(h_ref, g_ref, o_ref):
    x = h_ref[0]
    ms = jnp.mean(x * x, axis=-1, keepdims=True)
    o_ref[0] = x * lax.rsqrt(ms + EPS) * g_ref[...]


def _final_norm(h3, g):
    b, tp, d = h3.shape
    nb = tp // Q_BLOCK - 1
    return pl.pallas_call(
        _final_norm_kernel,
        grid=(b, nb),
        in_specs=[pl.BlockSpec((1, Q_BLOCK, d), lambda i, j: (i, j + 1, 0)), _resident((1, d))],
        out_specs=pl.BlockSpec((1, Q_BLOCK, d), lambda i, j: (i, j, 0)),
        out_shape=jax.ShapeDtypeStruct((b, nb * Q_BLOCK, d), F32),
        compiler_params=_cparams(2),
        name="final_norm",
    )(h3, g)


def _group_ones(n, group_shift):
    r = lax.broadcasted_iota(jnp.int32, (n, n), 0) >> group_shift
    c = lax.broadcasted_iota(jnp.int32, (n, n), 1) >> group_shift
    return jnp.where(r == c, 1.0, 0.0).astype(BF16)


def _gla_kernel(tp, pa_ref, wg_ref, bg_ref, gn_ref, o_ref, s_ref):
    c_len = CHUNK
    n_chunks = tp // c_len
    ri = lax.broadcasted_iota(jnp.int32, (c_len, c_len), 0)
    ci = lax.broadcasted_iota(jnp.int32, (c_len, c_len), 1)
    tri_bf = jnp.where(ri >= ci, 1.0, 0.0).astype(BF16)
    r4 = lax.broadcasted_iota(jnp.int32, (N_HEADS * c_len, c_len), 0) & (c_len - 1)
    c4 = lax.broadcasted_iota(jnp.int32, (N_HEADS * c_len, c_len), 1)
    tri4 = r4 >= c4
    qhead = lax.broadcasted_iota(jnp.int32, (1, N_HEADS * GLA_DK), 1) >> 5
    ehead = lax.broadcasted_iota(jnp.int32, (1, N_HEADS * GLA_DV), 1) >> 6
    shead = lax.broadcasted_iota(jnp.int32, (N_HEADS * GLA_DK, 1), 0) >> 5
    bd = shead == ehead
    gsum = _group_ones(N_HEADS * GLA_DV, 6)
    wg_hi, wg_lo = _split_bf16(wg_ref[...])
    bg = bg_ref[...]
    gn = gn_ref[...]
    scale = GLA_DK ** -0.5

    s_ref[...] = jnp.zeros_like(s_ref)
    o_ref[0, 0:c_len, :] = jnp.zeros((c_len, N_HEADS * GLA_DV), BF16)

    def chunk(c, carry):
        r0 = pl.multiple_of(c * c_len, c_len)
        rows = pl.ds(r0, c_len)
        q = pa_ref[0, rows, 0:128].astype(F32) * scale
        k = pa_ref[0, rows, 128:256].astype(F32)
        v = pa_ref[0, rows, 256:512]
        og = pa_ref[0, rows, 512:768].astype(F32)
        lr = pa_ref[0, rows, 768:896]
        valid = (r0 + lax.broadcasted_iota(jnp.int32, (c_len, 1), 0)) >= FRONT_PAD

        pre = _dot(lr, wg_hi) + _dot(lr, wg_lo) + bg
        logsig = jnp.minimum(pre, 0.0) - jnp.log1p(jnp.exp(-jnp.abs(pre)))
        g = jnp.where(valid, logsig * (1.0 / GLA_TAU), 0.0)
        g_hi, g_lo = _split_bf16(g)
        cum = _dot(tri_bf, g_hi) + _dot(tri_bf, g_lo)
        cum_last = cum[c_len - 1:c_len, :]
        qt = q * jnp.exp(cum)
        kt = (k * jnp.exp(-cum)).astype(BF16)
        kd = k * jnp.exp(cum_last - cum)

        qt_bf = qt.astype(BF16)
        qs = jnp.concatenate([jnp.where(qhead == h, qt_bf, jnp.zeros_like(qt_bf))
                              for h in range(N_HEADS)], axis=0)
        a = _dot_nt(qs, kt)
        a = jnp.where(tri4, a, 0.0).astype(BF16)
        r = _dot(a, v)
        o = _dot(qt_bf, s_ref[...].astype(BF16))
        for h in range(N_HEADS):
            o = o + jnp.where(ehead == h, r[h * c_len:(h + 1) * c_len, :], 0.0)

        kd_t = kd.T.astype(BF16)
        upd = _dot(kd_t, v)
        dec = jnp.exp(cum.T[:, c_len - 1:c_len])
        s_ref[...] = dec * s_ref[...] + jnp.where(bd, upd, 0.0)

        ms = _dot((o * o).astype(BF16), gsum) * (1.0 / GLA_DV)
        y = o * lax.rsqrt(ms + EPS) * gn * (og * _sigmoid(og))
        o_ref[0, rows, :] = jnp.where(valid, y, 0.0).astype(BF16)
        return carry

    lax.fori_loop(1, n_chunks, chunk, 0)


def _gla(pa, wg, bg, gn):
    b, tp, _ = pa.shape
    return pl.pallas_call(
        functools.partial(_gla_kernel, tp),
        grid=(b,),
        in_specs=[pl.BlockSpec((1, tp, PA_W), lambda i: (i, 0, 0)),
                  _resident((LANES, LANES)), _resident((1, LANES)), _resident((1, 256))],
        out_specs=pl.BlockSpec((1, tp, 256), lambda i: (i, 0, 0)),
        out_shape=jax.ShapeDtypeStruct((b, tp, 256), BF16),
        scratch_shapes=[pltpu.VMEM((N_HEADS * GLA_DK, N_HEADS * GLA_DV), F32)],
        compiler_params=_cparams(),
        name="gla",
    )(pa, wg, bg, gn)


RET_BLOCK = 128


def _ret_kernel(tp, pb_ref, cos_ref, sin_ref, dmask_ref, qfac_ref, kfac_ref, dec_ref, gn_ref,
                o_ref, s_ref):
    blk = RET_BLOCK
    n_blocks = tp // blk
    qhead = (lax.broadcasted_iota(jnp.int32, (1, 256), 1) & 127) >> 5
    ehead = lax.broadcasted_iota(jnp.int32, (1, 256), 1) >> 6
    shead = (lax.broadcasted_iota(jnp.int32, (256, 1), 0) & 127) >> 5
    bd = shead == ehead
    gsum = _group_ones(256, 6)
    gn = gn_ref[...]
    s_ref[...] = jnp.zeros_like(s_ref)

    def rope(x, cos, sin):
        x1, x2 = x[:, :128], x[:, 128:]
        return jnp.concatenate([x1 * cos - x2 * sin, x1 * sin + x2 * cos], axis=1)

    def block(j, carry):
        r0 = pl.multiple_of(j * blk, blk)
        rows = pl.ds(r0, blk)
        cos = cos_ref[rows, :]
        sin = sin_ref[rows, :]
        q = rope(pb_ref[0, rows, 0:256].astype(F32), cos, sin)
        k = rope(pb_ref[0, rows, 256:512].astype(F32), cos, sin) * (RET_DK ** -0.5)
        v = pb_ref[0, rows, 512:768]
        og = pb_ref[0, rows, 768:1024].astype(F32)
        valid = (r0 + lax.broadcasted_iota(jnp.int32, (blk, 1), 0)) >= FRONT_PAD

        q_bf = q.astype(BF16)
        qs = jnp.concatenate([jnp.where(qhead == h, q_bf, jnp.zeros_like(q_bf))
                              for h in range(N_HEADS)], axis=0)
        a = (_dot_nt(qs, k.astype(BF16)) * dmask_ref[...]).astype(BF16)
        r = _dot(a, v)
        o = _dot((q * qfac_ref[...]).astype(BF16), s_ref[...].astype(BF16))
        for h in range(N_HEADS):
            o = o + jnp.where(ehead == h, r[h * blk:(h + 1) * blk, :], 0.0)

        kd_t = (k * kfac_ref[...]).T.astype(BF16)
        upd = _dot(kd_t, v)
        s_ref[...] = dec_ref[...] * s_ref[...] + jnp.where(bd, upd, 0.0)

        mu = _dot(o.astype(BF16), gsum) * (1.0 / 64)
        xc = o - mu
        var = _dot((xc * xc).astype(BF16), gsum) * (1.0 / 64)
        y = xc * lax.rsqrt(var + EPS) * gn * (og * _sigmoid(og))
        o_ref[0, rows, :] = jnp.where(valid, y, 0.0).astype(BF16)
        return carry

    lax.fori_loop(0, n_blocks, block, 0)


def _ret(pb, cos, sin, dmask, qfac, kfac, dec, gn):
    b, tp, _ = pb.shape
    return pl.pallas_call(
        functools.partial(_ret_kernel, tp),
        grid=(b,),
        in_specs=[pl.BlockSpec((1, tp, PB_W), lambda i: (i, 0, 0)),
                  _resident(cos.shape), _resident(sin.shape), _resident(dmask.shape),
                  _resident(qfac.shape), _resident(kfac.shape), _resident(dec.shape),
                  _resident((1, 256))],
        out_specs=pl.BlockSpec((1, tp, 256), lambda i: (i, 0, 0)),
        out_shape=jax.ShapeDtypeStruct((b, tp, 256), BF16),
        scratch_shapes=[pltpu.VMEM((256, 256), F32)],
        compiler_params=_cparams(),
        name="retention",
    )(pb, cos, sin, dmask, qfac, kfac, dec, gn)


def _flash_step(s, mask, v_blocks, m_ref, l_ref, acc_ref):
    n_g = len(v_blocks)
    s = jnp.concatenate([jnp.where(mask, s[g * Q_BLOCK:(g + 1) * Q_BLOCK, :], NEG)
                         for g in range(n_g)], axis=0)
    m_prev = m_ref[...]
    m_new = jnp.maximum(m_prev, jnp.max(s, axis=-1, keepdims=True))
    p = jnp.exp(s - m_new)
    alpha = jnp.exp(m_prev - m_new)
    l_ref[...] = alpha * l_ref[...] + jnp.sum(p, axis=-1, keepdims=True)
    m_ref[...] = m_new
    p_bf = p.astype(BF16)
    pv = jnp.concatenate([_dot(p_bf[g * Q_BLOCK:(g + 1) * Q_BLOCK, :], v_blocks[g])
                          for g in range(n_g)], axis=0)
    acc_ref[...] = alpha * acc_ref[...] + pv


def _heads_to_lanes(per_head):
    lo = per_head[0] + pltpu.roll(per_head[1], 64, 1)
    hi = per_head[2] + pltpu.roll(per_head[3], 64, 1)
    return jnp.concatenate([lo, hi], axis=1)


def _mla_kernel(tp, pc_ref, qn_ref, kvn_ref, wuq_ref, wukv_ref, cos_ref, sa_ref, sb_ref,
                o_ref, q_s, k_s, v_s, m_s, l_s, acc_s):
    n_blocks = tp // Q_BLOCK
    scale = (MLA_NOPE + MLA_ROPE) ** -0.5
    is_q = lax.broadcasted_iota(jnp.int32, (1, 256), 1) < MLA_Q_RANK

    def prep(i, carry):
        r0 = pl.multiple_of(i * Q_BLOCK, Q_BLOCK)
        rows = pl.ds(r0, Q_BLOCK)
        x = pc_ref[0, rows, 0:256].astype(F32)
        x2 = x * x
        ms_q = jnp.sum(jnp.where(is_q, x2, 0.0), axis=-1, keepdims=True) * (1.0 / MLA_Q_RANK)
        ms_kv = jnp.sum(jnp.where(is_q, 0.0, x2), axis=-1, keepdims=True) * (1.0 / MLA_KV_RANK)
        yq = (x * lax.rsqrt(ms_q + EPS) * qn_ref[...]).astype(BF16)
        ykv = (x * lax.rsqrt(ms_kv + EPS) * kvn_ref[...]).astype(BF16)
        cq = _dot(yq, wuq_ref[...])
        kv = _dot(ykv, wukv_ref[...])
        cos = cos_ref[rows, :]
        sa = sa_ref[rows, :]
        sb = sb_ref[rows, :]

        def rope(t):
            return t * cos + pltpu.roll(t, 16, 1) * sa + pltpu.roll(t, LANES - 16, 1) * sb

        kpe = rope(pc_ref[0, rows, 256:384].astype(F32))
        for h in range(N_HEADS):
            q_s[h, rows, :] = (rope(cq[:, h * LANES:(h + 1) * LANES]) * scale).astype(BF16)
            k_s[h, rows, :] = (kv[:, h * LANES:(h + 1) * LANES] + kpe).astype(BF16)
            v_s[h, rows, :] = kv[:, (N_HEADS + h) * LANES:(N_HEADS + h + 1) * LANES].astype(BF16)
        return carry

    lax.fori_loop(0, n_blocks, prep, 0)

    def qblock(qi, carry):
        q0 = pl.multiple_of(qi * Q_BLOCK, Q_BLOCK)
        qrows = pl.ds(q0, Q_BLOCK)
        m_s[...] = jnp.full_like(m_s, NEG)
        l_s[...] = jnp.zeros_like(l_s)
        acc_s[...] = jnp.zeros_like(acc_s)
        qrow = q0 + lax.broadcasted_iota(jnp.int32, (Q_BLOCK, 1), 0)

        def kvblock(j, c2):
            k0 = pl.multiple_of(j * Q_BLOCK, Q_BLOCK)
            krows = pl.ds(k0, Q_BLOCK)
            kcol = k0 + lax.broadcasted_iota(jnp.int32, (1, Q_BLOCK), 1)
            mask = jnp.logical_and(kcol <= qrow, kcol >= FRONT_PAD)
            s = jnp.concatenate([_dot_nt(q_s[h, qrows, :], k_s[h, krows, :])
                                 for h in range(N_HEADS)], axis=0)
            _flash_step(s, mask, [v_s[h, krows, :] for h in range(N_HEADS)], m_s, l_s, acc_s)
            return c2

        lax.fori_loop(0, qi + 1, kvblock, 0)
        o = acc_s[...] / l_s[...]
        o = _heads_to_lanes([o[h * Q_BLOCK:(h + 1) * Q_BLOCK, :] for h in range(N_HEADS)])
        o_ref[0, qrows, :] = jnp.where(qrow >= FRONT_PAD, o, 0.0).astype(BF16)
        return carry

    lax.fori_loop(0, n_blocks, qblock, 0)


def _mla(pc, qn, kvn, wuq, wukv, cos, sa, sb):
    b, tp, _ = pc.shape
    g_rows = N_HEADS * Q_BLOCK
    return pl.pallas_call(
        functools.partial(_mla_kernel, tp),
        grid=(b,),
        in_specs=[pl.BlockSpec((1, tp, PC_W), lambda i: (i, 0, 0)),
                  _resident((1, 256)), _resident((1, 256)),
                  _resident(wuq.shape), _resident(wukv.shape),
                  _resident(cos.shape), _resident(sa.shape), _resident(sb.shape)],
        out_specs=pl.BlockSpec((1, tp, 256), lambda i: (i, 0, 0)),
        out_shape=jax.ShapeDtypeStruct((b, tp, 256), BF16),
        scratch_shapes=[pltpu.VMEM((N_HEADS, tp, LANES), BF16),
                        pltpu.VMEM((N_HEADS, tp, LANES), BF16),
                        pltpu.VMEM((N_HEADS, tp, LANES), BF16),
                        pltpu.VMEM((g_rows, 1), F32), pltpu.VMEM((g_rows, 1), F32),
                        pltpu.VMEM((g_rows, LANES), F32)],
        compiler_params=_cparams(),
        name="mla",
    )(pc, qn, kvn, wuq, wukv, cos, sa, sb)


def _diff_kernel(tp, lam_init, pd_ref, lam_ref, dn_ref, o_ref, qs_s, m_s, l_s, acc_s):
    n_blocks = tp // Q_BLOCK
    n_maps = 2 * N_HEADS
    scale = DIFF_DK ** -0.5
    group = lax.broadcasted_iota(jnp.int32, (1, 256), 1) >> 5
    lv = lam_ref[...]
    lam = (jnp.exp(jnp.sum(lv[0:1, :] * lv[1:2, :], axis=-1, keepdims=True))
           - jnp.exp(jnp.sum(lv[2:3, :] * lv[3:4, :], axis=-1, keepdims=True)) + lam_init)
    dn = dn_ref[...]

    def qblock(qi, carry):
        q0 = pl.multiple_of(qi * Q_BLOCK, Q_BLOCK)
        qrows = pl.ds(q0, Q_BLOCK)
        m_s[...] = jnp.full_like(m_s, NEG)
        l_s[...] = jnp.zeros_like(l_s)
        acc_s[...] = jnp.zeros_like(acc_s)
        qrow = q0 + lax.broadcasted_iota(jnp.int32, (Q_BLOCK, 1), 0)
        q = (pd_ref[0, qrows, 0:256].astype(F32) * scale).astype(BF16)
        for g in range(n_maps):
            qs_s[g * Q_BLOCK:(g + 1) * Q_BLOCK, :] = jnp.where(group == g, q, jnp.zeros_like(q))

        def kvblock(j, c2):
            k0 = pl.multiple_of(j * Q_BLOCK, Q_BLOCK)
            krows = pl.ds(k0, Q_BLOCK)
            kcol = k0 + lax.broadcasted_iota(jnp.int32, (1, Q_BLOCK), 1)
            mask = jnp.logical_and(kcol <= qrow, kcol >= FRONT_PAD)
            s = _dot_nt(qs_s[...], pd_ref[0, krows, 256:512])
            v_blocks = [pd_ref[0, krows, 512 + (g // 2) * LANES:512 + (g // 2 + 1) * LANES]
                        for g in range(n_maps)]
            _flash_step(s, mask, v_blocks, m_s, l_s, acc_s)
            return c2

        lax.fori_loop(0, qi + 1, kvblock, 0)
        o = acc_s[...] / l_s[...]
        per_head = []
        for h in range(N_HEADS):
            o1 = o[(2 * h) * Q_BLOCK:(2 * h + 1) * Q_BLOCK, :]
            o2 = o[(2 * h + 1) * Q_BLOCK:(2 * h + 2) * Q_BLOCK, :]
            od = o1 - lam * o2
            ms = jnp.sum(od * od, axis=-1, keepdims=True) * (1.0 / 64)
            per_head.append(od * lax.rsqrt(ms + EPS) * dn * (1.0 - lam_init))
        y = _heads_to_lanes(per_head)
        o_ref[0, qrows, :] = jnp.where(qrow >= FRONT_PAD, y, 0.0).astype(BF16)
        return carry

    lax.fori_loop(0, n_blocks, qblock, 0)


def _diff(pd, lam_rows, dn, lam_init):
    b, tp, _ = pd.shape
    g_rows = 2 * N_HEADS * Q_BLOCK
    return pl.pallas_call(
        functools.partial(_diff_kernel, tp, lam_init),
        grid=(b,),
        in_specs=[pl.BlockSpec((1, tp, PD_W), lambda i: (i, 0, 0)),
                  _resident(lam_rows.shape), _resident((1, LANES))],
        out_specs=pl.BlockSpec((1, tp, 256), lambda i: (i, 0, 0)),
        out_shape=jax.ShapeDtypeStruct((b, tp, 256), BF16),
        scratch_shapes=[pltpu.VMEM((g_rows, 256), BF16),
                        pltpu.VMEM((g_rows, 1), F32), pltpu.VMEM((g_rows, 1), F32),
                        pltpu.VMEM((g_rows, LANES), F32)],
        compiler_params=_cparams(),
        name="diffattn",
    )(pd, lam_rows, dn)


def _pad_cols(x, width):
    return jnp.pad(x, ((0, 0), (0, width - x.shape[1])))


def _rot_split(w):
    d = w.shape[0]
    return w.reshape(d, N_HEADS, 2, 32).transpose(0, 2, 1, 3).reshape(d, 256)


def _layout_w_in(w):
    sizes = (128, 128, 256, 16, 256, 256, 256, 256, 256, 192, 64, 32, 256, 256, 256)
    offs = [0]
    for s_ in sizes:
        offs.append(offs[-1] + s_)
    seg = [w[:, offs[i]:offs[i + 1]] for i in range(len(sizes))]
    (a_q, a_k, a_v, a_lr, a_og, r_q, r_k, r_v, r_og, c_cq, c_ckv, c_kpe, d_q, d_k, d_v) = seg
    d = w.shape[0]
    z = lambda n: jnp.zeros((d, n), w.dtype)
    dv_p = jnp.pad(d_v.reshape(d, N_HEADS, 64), ((0, 0), (0, 0), (0, 64))).reshape(d, 512)
    cols = [a_q, a_k, a_v, a_og, a_lr, z(112),
            _rot_split(r_q), _rot_split(r_k), r_v, r_og,
            c_cq, c_ckv, z(64), c_kpe, z(32),
            d_q, d_k, dv_p]
    return jnp.concatenate(cols, axis=1).astype(BF16)


def _tables(tp):
    pos = jnp.arange(tp, dtype=F32) - FRONT_PAD
    inv = ROPE_THETA ** (-jnp.arange(32, dtype=F32) / 32)
    ang = pos[:, None] * inv[None, :]
    ret_cos = jnp.tile(jnp.cos(ang), (1, N_HEADS))
    ret_sin = jnp.tile(jnp.sin(ang), (1, N_HEADS))
    inv16 = ROPE_THETA ** (-jnp.arange(16, dtype=F32) / 16)
    ang16 = pos[:, None] * inv16[None, :]
    c16, s16 = jnp.cos(ang16), jnp.sin(ang16)
    one = lambda n: jnp.ones((tp, n), F32)
    zero = lambda n: jnp.zeros((tp, n), F32)
    mla_cos = jnp.concatenate([one(64), c16, c16, one(32)], axis=1)
    mla_sa = jnp.concatenate([zero(80), s16, zero(32)], axis=1)
    mla_sb = jnp.concatenate([zero(64), -s16, zero(48)], axis=1)
    lg = jnp.log(1.0 - jnp.exp2(-5.0 - jnp.arange(N_HEADS, dtype=F32)))
    idx = jnp.arange(RET_BLOCK, dtype=F32)
    rel = idx[:, None] - idx[None, :]
    dmask = jnp.where(rel[None] >= 0, jnp.exp(rel[None] * lg[:, None, None]), 0.0)
    dmask = dmask.reshape(N_HEADS * RET_BLOCK, RET_BLOCK)
    lane_head = (jnp.arange(256) % 128) // 32
    qfac = jnp.exp((idx[:, None] + 1.0) * lg[lane_head][None, :])
    kfac = jnp.exp((RET_BLOCK - 1.0 - idx[:, None]) * lg[lane_head][None, :])
    dec = jnp.exp(RET_BLOCK * lg[lane_head])[:, None]
    return ret_cos, ret_sin, mla_cos, mla_sa, mla_sb, dmask, qfac, kfac, dec


def _chunk_cols(w):
    d, f = w.shape
    return w.reshape(d, f // FF_CHUNK, FF_CHUNK).transpose(1, 0, 2).astype(BF16)


def _chunk_rows(w):
    f, d = w.shape
    return w.reshape(f // FF_CHUNK, FF_CHUNK, d).astype(BF16)


def kernel(x, meta_tokens, attn_norm, w_in, gla_w_gate, gla_b_gate, gla_norm, ret_norm, mla_q_norm, mla_w_uq, mla_kv_norm, mla_w_ukv, diff_lambda, diff_norm, w_out, ffn_norm, ffn_w_gate, ffn_w_up, ffn_w_down, moe_router, moe_w_gate, moe_w_up, moe_w_down, final_norm):
    b, seq, d = x.shape
    tp = FRONT_PAD + N_META + seq
    n = b * tp
    meta = jnp.broadcast_to(meta_tokens[None].astype(x.dtype), (b, N_META, d))
    h = jnp.concatenate([jnp.zeros((b, FRONT_PAD, d), x.dtype), meta, x], axis=1).reshape(n, d)
    ret_cos, ret_sin, mla_cos, mla_sa, mla_sb, dmask, qfac, kfac, dec = _tables(tp)

    for li in range(DEPTH):
        pa, pb, pc, pd = _inproj(h, attn_norm[li][None, :], _layout_w_in(w_in[li]))
        pa, pb, pc, pd = (p.reshape(b, tp, -1) for p in (pa, pb, pc, pd))

        wgate = jnp.pad(gla_w_gate[li], ((0, LANES - GLA_GATE_RANK), (0, 0)))
        o_a = _gla(pa, wgate, gla_b_gate[li][None, :], jnp.tile(gla_norm[li], N_HEADS)[None, :])
        o_b = _ret(pb, ret_cos, ret_sin, dmask, qfac, kfac, dec, jnp.tile(ret_norm[li], N_HEADS)[None, :])

        qn = _pad_cols(mla_q_norm[li][None, :], 256)
        kvn = jnp.pad(mla_kv_norm[li][None, :], ((0, 0), (MLA_Q_RANK, 0)))
        wuq = jnp.pad(mla_w_uq[li].reshape(MLA_Q_RANK, N_HEADS, MLA_NOPE + MLA_ROPE),
                      ((0, 256 - MLA_Q_RANK), (0, 0), (0, LANES - MLA_NOPE - MLA_ROPE)))
        wuq = wuq.reshape(256, N_HEADS * LANES).astype(BF16)
        wukv = mla_w_ukv[li].reshape(MLA_KV_RANK, N_HEADS, 2, 64)
        wukv = jnp.pad(wukv, ((MLA_Q_RANK, 0), (0, 0), (0, 0), (0, 64)))
        wukv = wukv.transpose(0, 2, 1, 3).reshape(256, 2 * N_HEADS * LANES).astype(BF16)
        o_c = _mla(pc, qn, kvn, wuq, wukv, mla_cos, mla_sa, mla_sb)

        lam_init = 0.8 - 0.6 * math.exp(-0.3 * li)
        o_d = _diff(pd, diff_lambda[li], _pad_cols(diff_norm[li][None, :], LANES), lam_init)

        o_a, o_b, o_c, o_d = (o.reshape(n, 256) for o in (o_a, o_b, o_c, o_d))
        wo = w_out[li].astype(BF16)
        fn = ffn_norm[li][None, :]
        j = li // 2
        if li % 2 == 0:
            h, hn = _outproj(h, o_a, o_b, o_c, o_d, wo, fn)
            h = _swiglu(h, hn, _chunk_cols(ffn_w_gate[j]), _chunk_cols(ffn_w_up[j]),
                        _chunk_rows(ffn_w_down[j]))
        else:
            h, hn, gates = _outproj(h, o_a, o_b, o_c, o_d, wo, fn,
                                    router=_pad_cols(moe_router[j], LANES))
            for e in range(N_EXPERTS):
                h = _swiglu(h, hn, _chunk_cols(moe_w_gate[j, e]), _chunk_cols(moe_w_up[j, e]),
                            _chunk_rows(moe_w_down[j, e]), gates=gates, expert=e)

    return _final_norm(h.reshape(b, tp, d), final_norm[None, :])
```

```python
import functools
import math

import jax
import jax.numpy as jnp
from jax import lax
from jax.experimental import pallas as pl
from jax.experimental.pallas import tpu as pltpu

F32 = jnp.float32
BF16 = jnp.bfloat16

D_MODEL = 1024
DEPTH = 2
N_META = 16
CHUNK = 64
Q_BLOCK = 128
FRONT_PAD = Q_BLOCK - N_META
EPS = 1e-6
NEG = -1e30
ROPE_THETA = 10000.0
N_HEADS = 4
GLA_DK = 32
GLA_DV = 64
GLA_GATE_RANK = 16
GLA_TAU = 16.0
RET_DK = 64
MLA_Q_RANK = 192
MLA_KV_RANK = 64
MLA_NOPE = 64
MLA_ROPE = 32
DIFF_DK = 32
D_FF = 2816
N_EXPERTS = 8
D_FF_EXPERT = 3584

LANES = 128
ROW_TILE = 512
FF_CHUNK = 256
VMEM_LIMIT = 56 * 1024 * 1024

PA_W = 896
PB_W = 1024
PC_W = 384
PD_W = 1024


def _cparams(n_axes=1):
    return pltpu.CompilerParams(dimension_semantics=("arbitrary",) * n_axes,
                                vmem_limit_bytes=VMEM_LIMIT)


def _resident(shape):
    nd = len(shape)
    return pl.BlockSpec(shape, lambda *_: (0,) * nd, pipeline_mode=pl.Buffered(1))


def _sigmoid(x):
    return 1.0 / (1.0 + jnp.exp(-x))


def _split_bf16(x):
    hi = x.astype(BF16)
    lo = (x - hi.astype(F32)).astype(BF16)
    return hi, lo


def _dot(a, b):
    return jnp.dot(a, b, preferred_element_type=F32)


def _dot_nt(a, b):
    return lax.dot_general(a, b, (((1,), (1,)), ((), ())), preferred_element_type=F32)


def _inproj_kernel(h_ref, g_ref, w_ref, pa_ref, pb_ref, pc_ref, pd_ref):
    x = h_ref[...]
    ms = jnp.mean(x * x, axis=-1, keepdims=True)
    y = (x * lax.rsqrt(ms + EPS) * g_ref[...]).astype(BF16)
    off = 0
    for o_ref, width in ((pa_ref, PA_W), (pb_ref, PB_W), (pc_ref, PC_W), (pd_ref, PD_W)):
        o_ref[...] = _dot(y, w_ref[:, off:off + width]).astype(BF16)
        off += width


def _inproj(h, g, w):
    n = h.shape[0]
    wtot = PA_W + PB_W + PC_W + PD_W
    row = lambda width: pl.BlockSpec((ROW_TILE, width), lambda i: (i, 0))
    return pl.pallas_call(
        _inproj_kernel,
        grid=(n // ROW_TILE,),
        in_specs=[row(D_MODEL), _resident((1, D_MODEL)), _resident((D_MODEL, wtot))],
        out_specs=[row(PA_W), row(PB_W), row(PC_W), row(PD_W)],
        out_shape=[jax.ShapeDtypeStruct((n, w_), BF16) for w_ in (PA_W, PB_W, PC_W, PD_W)],
        compiler_params=_cparams(),
        name="inproj",
    )(h, g, w)


def _outproj_kernel(with_router, h_ref, oa_ref, ob_ref, oc_ref, od_ref, wo_ref, fn_ref, *rest):
    if with_router:
        router_ref, hmid_ref, hn_ref, gate_ref = rest
    else:
        hmid_ref, hn_ref = rest
    o = jnp.concatenate([oa_ref[...], ob_ref[...], oc_ref[...], od_ref[...]], axis=1)
    hm = h_ref[...] + _dot(o, wo_ref[...])
    hmid_ref[...] = hm
    ms = jnp.mean(hm * hm, axis=-1, keepdims=True)
    y = hm * lax.rsqrt(ms + EPS) * fn_ref[...]
    hn_ref[...] = y.astype(BF16)
    if with_router:
        y_hi, y_lo = _split_bf16(y)
        r_hi, r_lo = _split_bf16(router_ref[...])
        logits = _dot(y_hi, r_hi) + _dot(y_hi, r_lo) + _dot(y_lo, r_hi)
        lane = lax.broadcasted_iota(jnp.int32, logits.shape, 1).astype(F32)
        ninf = float("-inf")
        logits = jnp.where(lane < N_EXPERTS, logits, ninf)
        m1 = jnp.max(logits, axis=-1, keepdims=True)
        i1 = jnp.min(jnp.where(logits == m1, lane, float(LANES)), axis=-1, keepdims=True)
        rest_l = jnp.where(lane == i1, ninf, logits)
        m2 = jnp.max(rest_l, axis=-1, keepdims=True)
        i2 = jnp.min(jnp.where(rest_l == m2, lane, float(LANES)), axis=-1, keepdims=True)
        e2 = jnp.exp(m2 - m1)
        den = 1.0 + e2
        gate_ref[...] = jnp.where(lane == i1, 1.0 / den, 0.0) + jnp.where(lane == i2, e2 / den, 0.0)


def _outproj(h, oa, ob, oc, od, wo, fn, router=None):
    n = h.shape[0]
    row = lambda width: pl.BlockSpec((ROW_TILE, width), lambda i: (i, 0))
    in_specs = [row(D_MODEL), row(256), row(256), row(256), row(256),
                _resident((D_MODEL, D_MODEL)), _resident((1, D_MODEL))]
    out_specs = [row(D_MODEL), row(D_MODEL)]
    out_shape = [jax.ShapeDtypeStruct((n, D_MODEL), F32), jax.ShapeDtypeStruct((n, D_MODEL), BF16)]
    args = [h, oa, ob, oc, od, wo, fn]
    if router is not None:
        in_specs.append(_resident((D_MODEL, LANES)))
        out_specs.append(row(LANES))
        out_shape.append(jax.ShapeDtypeStruct((n, LANES), F32))
        args.append(router)
    return pl.pallas_call(
        functools.partial(_outproj_kernel, router is not None),
        grid=(n // ROW_TILE,),
        in_specs=in_specs, out_specs=out_specs, out_shape=out_shape,
        compiler_params=_cparams(),
        name="outproj_router" if router is not None else "outproj",
    )(*args)


def _swiglu_kernel(expert, n_chunks, h_ref, hn_ref, *rest):
    if expert is None:
        wg_ref, wu_ref, wd_ref, o_ref, acc_ref = rest
    else:
        gate_ref, wg_ref, wu_ref, wd_ref, o_ref, acc_ref = rest
    hn = hn_ref[...]
    acc_ref[...] = jnp.zeros_like(acc_ref)

    def body(c, carry):
        g = _dot(hn, wg_ref[c])
        u = _dot(hn, wu_ref[c])
        a = (g * _sigmoid(g) * u).astype(BF16)
        acc_ref[...] += _dot(a, wd_ref[c])
        return carry

    lax.fori_loop(0, n_chunks, body, 0)
    y = acc_ref[...]
    if expert is not None:
        y = gate_ref[:, expert:expert + 1] * y
    o_ref[...] = h_ref[...] + y


def _swiglu(h, hn, wg, wu, wd, gates=None, expert=None):
    n = h.shape[0]
    n_chunks = wg.shape[0]
    row = lambda width: pl.BlockSpec((ROW_TILE, width), lambda i: (i, 0))
    in_specs = [row(D_MODEL), row(D_MODEL)]
    args = [h, hn]
    if expert is not None:
        in_specs.append(row(LANES))
        args.append(gates)
    in_specs += [_resident(wg.shape), _resident(wu.shape), _resident(wd.shape)]
    args += [wg, wu, wd]
    return pl.pallas_call(
        functools.partial(_swiglu_kernel, expert, n_chunks),
        grid=(n // ROW_TILE,),
        in_specs=in_specs,
        out_specs=row(D_MODEL),
        out_shape=jax.ShapeDtypeStruct((n, D_MODEL), F32),
        scratch_shapes=[pltpu.VMEM((ROW_TILE, D_MODEL), F32)],
        input_output_aliases={0: 0},
        compiler_params=_cparams(),
        name="swiglu" if expert is None else f"moe_expert{expert}",
    )(*args)


def _final_norm_kernel(h_ref, g_ref, o_ref):
    x = h_ref[0]
    ms = jnp.mean(x * x, axis=-1, keepdims=True)
    o_ref[0] = x * lax.rsqrt(ms + EPS) * g_ref[...]


def _final_norm(h3, g):
    b, tp, d = h3.shape
    nb = tp // Q_BLOCK - 1
    return pl.pallas_call(
        _final_norm_kernel,
        grid=(b, nb),
        in_specs=[pl.BlockSpec((1, Q_BLOCK, d), lambda i, j: (i, j + 1, 0)), _resident((1, d))],
        out_specs=pl.BlockSpec((1, Q_BLOCK, d), lambda i, j: (i, j, 0)),
        out_shape=jax.ShapeDtypeStruct((b, nb * Q_BLOCK, d), F32),
        compiler_params=_cparams(2),
        name="final_norm",
    )(h3, g)


def _group_ones(n, group_shift):
    r = lax.broadcasted_iota(jnp.int32, (n, n), 0) >> group_shift
    c = lax.broadcasted_iota(jnp.int32, (n, n), 1) >> group_shift
    return jnp.where(r == c, 1.0, 0.0).astype(BF16)


def _gla_kernel(tp, pa_ref, wg_ref, bg_ref, gn_ref, o_ref, s_ref):
    c_len = CHUNK
    n_chunks = tp // c_len
    ri = lax.broadcasted_iota(jnp.int32, (c_len, c_len), 0)
    ci = lax.broadcasted_iota(jnp.int32, (c_len, c_len), 1)
    tri_bf = jnp.where(ri >= ci, 1.0, 0.0).astype(BF16)
    r4 = lax.broadcasted_iota(jnp.int32, (N_HEADS * c_len, c_len), 0) & (c_len - 1)
    c4 = lax.broadcasted_iota(jnp.int32, (N_HEADS * c_len, c_len), 1)
    tri4 = r4 >= c4
    qhead = lax.broadcasted_iota(jnp.int32, (1, N_HEADS * GLA_DK), 1) >> 5
    ehead = lax.broadcasted_iota(jnp.int32, (1, N_HEADS * GLA_DV), 1) >> 6
    shead = lax.broadcasted_iota(jnp.int32, (N_HEADS * GLA_DK, 1), 0) >> 5
    bd = shead == ehead
    gsum = _group_ones(N_HEADS * GLA_DV, 6)
    wg_hi, wg_lo = _split_bf16(wg_ref[...])
    bg = bg_ref[...]
    gn = gn_ref[...]
    scale = GLA_DK ** -0.5

    s_ref[...] = jnp.zeros_like(s_ref)
    o_ref[0, 0:c_len, :] = jnp.zeros((c_len, N_HEADS * GLA_DV), BF16)

    def chunk(c, carry):
        r0 = pl.multiple_of(c * c_len, c_len)
        rows = pl.ds(r0, c_len)
        q = pa_ref[0, rows, 0:128].astype(F32) * scale
        k = pa_ref[0, rows, 128:256].astype(F32)
        v = pa_ref[0, rows, 256:512]
        og = pa_ref[0, rows, 512:768].astype(F32)
        lr = pa_ref[0, rows, 768:896]
        valid = (r0 + lax.broadcasted_iota(jnp.int32, (c_len, 1), 0)) >= FRONT_PAD

        pre = _dot(lr, wg_hi) + _dot(lr, wg_lo) + bg
        logsig = jnp.minimum(pre, 0.0) - jnp.log1p(jnp.exp(-jnp.abs(pre)))
        g = jnp.where(valid, logsig * (1.0 / GLA_TAU), 0.0)
        g_hi, g_lo = _split_bf16(g)
        cum = _dot(tri_bf, g_hi) + _dot(tri_bf, g_lo)
        cum_last = cum[c_len - 1:c_len, :]
        qt = q * jnp.exp(cum)
        kt = (k * jnp.exp(-cum)).astype(BF16)
        kd = k * jnp.exp(cum_last - cum)

        qt_bf = qt.astype(BF16)
        qs = jnp.concatenate([jnp.where(qhead == h, qt_bf, jnp.zeros_like(qt_bf))
                              for h in range(N_HEADS)], axis=0)
        a = _dot_nt(qs, kt)
        a = jnp.where(tri4, a, 0.0).astype(BF16)
        r = _dot(a, v)
        o = _dot(qt_bf, s_ref[...].astype(BF16))
        for h in range(N_HEADS):
            o = o + jnp.where(ehead == h, r[h * c_len:(h + 1) * c_len, :], 0.0)

        kd_t = kd.T.astype(BF16)
        upd = _dot(kd_t, v)
        dec = jnp.exp(cum.T[:, c_len - 1:c_len])
        s_ref[...] = dec * s_ref[...] + jnp.where(bd, upd, 0.0)

        ms = _dot((o * o).astype(BF16), gsum) * (1.0 / GLA_DV)
        y = o * lax.rsqrt(ms + EPS) * gn * (og * _sigmoid(og))
        o_ref[0, rows, :] = jnp.where(valid, y, 0.0).astype(BF16)
        return carry

    lax.fori_loop(1, n_chunks, chunk, 0)


def _gla(pa, wg, bg, gn):
    b, tp, _ = pa.shape
    return pl.pallas_call(
        functools.partial(_gla_kernel, tp),
        grid=(b,),
        in_specs=[pl.BlockSpec((1, tp, PA_W), lambda i: (i, 0, 0)),
                  _resident((LANES, LANES)), _resident((1, LANES)), _resident((1, 256))],
        out_specs=pl.BlockSpec((1, tp, 256), lambda i: (i, 0, 0)),
        out_shape=jax.ShapeDtypeStruct((b, tp, 256), BF16),
        scratch_shapes=[pltpu.VMEM((N_HEADS * GLA_DK, N_HEADS * GLA_DV), F32)],
        compiler_params=_cparams(),
        name="gla",
    )(pa, wg, bg, gn)


RET_BLOCK = 128


def _ret_kernel(tp, pb_ref, cos_ref, sin_ref, dmask_ref, qfac_ref, kfac_ref, dec_ref, gn_ref,
                o_ref, s_ref):
    blk = RET_BLOCK
    n_blocks = tp // blk
    qhead = (lax.broadcasted_iota(jnp.int32, (1, 256), 1) & 127) >> 5
    ehead = lax.broadcasted_iota(jnp.int32, (1, 256), 1) >> 6
    shead = (lax.broadcasted_iota(jnp.int32, (256, 1), 0) & 127) >> 5
    bd = shead == ehead
    gsum = _group_ones(256, 6)
    gn = gn_ref[...]
    s_ref[...] = jnp.zeros_like(s_ref)

    def rope(x, cos, sin):
        x1, x2 = x[:, :128], x[:, 128:]
        return jnp.concatenate([x1 * cos - x2 * sin, x1 * sin + x2 * cos], axis=1)

    def block(j, carry):
        r0 = pl.multiple_of(j * blk, blk)
        rows = pl.ds(r0, blk)
        cos = cos_ref[rows, :]
        sin = sin_ref[rows, :]
        q = rope(pb_ref[0, rows, 0:256].astype(F32), cos, sin)
        k = rope(pb_ref[0, rows, 256:512].astype(F32), cos, sin) * (RET_DK ** -0.5)
        v = pb_ref[0, rows, 512:768]
        og = pb_ref[0, rows, 768:1024].astype(F32)
        valid = (r0 + lax.broadcasted_iota(jnp.int32, (blk, 1), 0)) >= FRONT_PAD

        q_bf = q.astype(BF16)
        qs = jnp.concatenate([jnp.where(qhead == h, q_bf, jnp.zeros_like(q_bf))
                              for h in range(N_HEADS)], axis=0)
        a = (_dot_nt(qs, k.astype(BF16)) * dmask_ref[...]).astype(BF16)
        r = _dot(a, v)
        o = _dot((q * qfac_ref[...]).astype(BF16), s_ref[...].astype(BF16))
        for h in range(N_HEADS):
            o = o + jnp.where(ehead == h, r[h * blk:(h + 1) * blk, :], 0.0)

        kd_t = (k * kfac_ref[...]).T.astype(BF16)
        upd = _dot(kd_t, v)
        s_ref[...] = dec_ref[...] * s_ref[...] + jnp.where(bd, upd, 0.0)

        mu = _dot(o.astype(BF16), gsum) * (1.0 / 64)
        xc = o - mu
        var = _dot((xc * xc).astype(BF16), gsum) * (1.0 / 64)
        y = xc * lax.rsqrt(var + EPS) * gn * (og * _sigmoid(og))
        o_ref[0, rows, :] = jnp.where(valid, y, 0.0).astype(BF16)
        return carry

    lax.fori_loop(0, n_blocks, block, 0)


def _ret(pb, cos, sin, dmask, qfac, kfac, dec, gn):
    b, tp, _ = pb.shape
    return pl.pallas_call(
        functools.partial(_ret_kernel, tp),
        grid=(b,),
        in_specs=[pl.BlockSpec((1, tp, PB_W), lambda i: (i, 0, 0)),
                  _resident(cos.shape), _resident(sin.shape), _resident(dmask.shape),
                  _resident(qfac.shape), _resident(kfac.shape), _resident(dec.shape),
                  _resident((1, 256))],
        out_specs=pl.BlockSpec((1, tp, 256), lambda i: (i, 0, 0)),
        out_shape=jax.ShapeDtypeStruct((b, tp, 256), BF16),
        scratch_shapes=[pltpu.VMEM((256, 256), F32)],
        compiler_params=_cparams(),
        name="retention",
    )(pb, cos, sin, dmask, qfac, kfac, dec, gn)


ATT_ROWS = 256
LOG2E = 1.4426950408889634


def _mask_groups(mask, s, n_g):
    rows = s.shape[0] // n_g
    return jnp.concatenate([jnp.where(mask, s[g * rows:(g + 1) * rows, :], NEG)
                            for g in range(n_g)], axis=0)


def _attn_block0(qk_fn, v_fn, n_g):
    rows = pl.ds(0, Q_BLOCK)
    qrow = lax.broadcasted_iota(jnp.int32, (Q_BLOCK, 1), 0)
    kcol = lax.broadcasted_iota(jnp.int32, (1, Q_BLOCK), 1)
    mask = jnp.logical_and(kcol <= qrow, kcol >= FRONT_PAD)
    s = _mask_groups(mask, qk_fn(0, Q_BLOCK, rows), n_g)
    p = jnp.exp2(s - jnp.max(s, axis=-1, keepdims=True))
    l = jnp.sum(p, axis=-1, keepdims=True)
    p_bf = p.astype(BF16)
    pv = jnp.concatenate([_dot(p_bf[g * Q_BLOCK:(g + 1) * Q_BLOCK, :], v_fn(g, rows))
                          for g in range(n_g)], axis=0)
    return pv / l


def _attn_block(i, qk_fn, v_fn, n_g, s_meta, s_s, mx_s, l_s, acc_s):
    rr = ATT_ROWS
    q0 = pl.multiple_of(Q_BLOCK + (i - 1) * rr, Q_BLOCK)
    meta_rows = pl.ds(0, Q_BLOCK)
    kcol = lax.broadcasted_iota(jnp.int32, (1, Q_BLOCK), 1)

    def key_rows(j):
        return pl.ds(pl.multiple_of(Q_BLOCK + j * rr, Q_BLOCK), rr)

    s = jnp.where(kcol >= FRONT_PAD, qk_fn(q0, rr, meta_rows), NEG)
    s_meta[...] = s
    mx_s[...] = s

    def pass1(j, carry):
        sj = qk_fn(q0, rr, key_rows(j))
        s_s[j] = sj
        mx_s[...] = jnp.maximum(mx_s[...], jnp.maximum(sj[:, :LANES], sj[:, LANES:]))
        return carry

    lax.fori_loop(0, i - 1, pass1, 0)
    causal = (lax.broadcasted_iota(jnp.int32, (rr, rr), 1)
              <= lax.broadcasted_iota(jnp.int32, (rr, rr), 0))
    sd = _mask_groups(causal, qk_fn(q0, rr, pl.ds(q0, rr)), n_g)
    s_s[i - 1] = sd
    m = jnp.max(jnp.maximum(mx_s[...], jnp.maximum(sd[:, :LANES], sd[:, LANES:])),
                axis=-1, keepdims=True)
    mx_s[...] = jnp.broadcast_to(m, mx_s.shape)

    p = jnp.exp2(s_meta[...] - mx_s[...])
    l_s[...] = p
    p_bf = p.astype(BF16)
    for g in range(n_g):
        acc_s[g * rr:(g + 1) * rr, :] = _dot(p_bf[g * rr:(g + 1) * rr, :], v_fn(g, meta_rows))

    def pass2(j, carry):
        sj = s_s[j]
        mrep = mx_s[...]
        p0 = jnp.exp2(sj[:, :LANES] - mrep)
        p1 = jnp.exp2(sj[:, LANES:] - mrep)
        l_s[...] += p0 + p1
        pj = jnp.concatenate([p0.astype(BF16), p1.astype(BF16)], axis=1)
        for g in range(n_g):
            acc_s[g * rr:(g + 1) * rr, :] += _dot(pj[g * rr:(g + 1) * rr, :], v_fn(g, key_rows(j)))
        return carry

    lax.fori_loop(0, i, pass2, 0)
    return acc_s[...] / jnp.sum(l_s[...], axis=-1, keepdims=True)


def _heads_to_lanes(per_head):
    lo = per_head[0] + pltpu.roll(per_head[1], 64, 1)
    hi = per_head[2] + pltpu.roll(per_head[3], 64, 1)
    return jnp.concatenate([lo, hi], axis=1)


def _mla_kernel(tp, pc_ref, qn_ref, kvn_ref, wuq_ref, wukv_ref, cos_ref, sa_ref, sb_ref,
                o_ref, q_s, k_s, v_s, s_meta, s_s, mx_s, l_s, acc_s):
    n_blocks = tp // Q_BLOCK
    scale = (MLA_NOPE + MLA_ROPE) ** -0.5 * LOG2E
    is_q = lax.broadcasted_iota(jnp.int32, (1, 256), 1) < MLA_Q_RANK

    def prep(i, carry):
        r0 = pl.multiple_of(i * Q_BLOCK, Q_BLOCK)
        rows = pl.ds(r0, Q_BLOCK)
        x = pc_ref[0, rows, 0:256].astype(F32)
        x2 = x * x
        ms_q = jnp.sum(jnp.where(is_q, x2, 0.0), axis=-1, keepdims=True) * (1.0 / MLA_Q_RANK)
        ms_kv = jnp.sum(jnp.where(is_q, 0.0, x2), axis=-1, keepdims=True) * (1.0 / MLA_KV_RANK)
        yq = (x * lax.rsqrt(ms_q + EPS) * qn_ref[...]).astype(BF16)
        ykv = (x * lax.rsqrt(ms_kv + EPS) * kvn_ref[...]).astype(BF16)
        cq = _dot(yq, wuq_ref[...])
        kv = _dot(ykv, wukv_ref[...])
        cos = cos_ref[rows, :]
        sa = sa_ref[rows, :]
        sb = sb_ref[rows, :]

        def rope(t):
            return t * cos + pltpu.roll(t, 16, 1) * sa + pltpu.roll(t, LANES - 16, 1) * sb

        kpe = rope(pc_ref[0, rows, 256:384].astype(F32))
        for h in range(N_HEADS):
            q_s[h, rows, :] = (rope(cq[:, h * LANES:(h + 1) * LANES]) * scale).astype(BF16)
            k_s[h, rows, :] = (kv[:, h * LANES:(h + 1) * LANES] + kpe).astype(BF16)
            v_s[h, rows, :] = kv[:, (N_HEADS + h) * LANES:(N_HEADS + h + 1) * LANES].astype(BF16)
        return carry

    lax.fori_loop(0, n_blocks, prep, 0)

    def qk_fn(q0, n_rows, krows):
        return jnp.concatenate([_dot_nt(q_s[h, pl.ds(q0, n_rows), :], k_s[h, krows, :])
                                for h in range(N_HEADS)], axis=0)

    def v_fn(h, krows):
        return v_s[h, krows, :]

    def emit(q0, n_rows, o, first):
        y = _heads_to_lanes([o[h * n_rows:(h + 1) * n_rows, :] for h in range(N_HEADS)])
        if first:
            qrow = lax.broadcasted_iota(jnp.int32, (n_rows, 1), 0)
            y = jnp.where(qrow >= FRONT_PAD, y, 0.0)
        o_ref[0, pl.ds(q0, n_rows), :] = y.astype(BF16)

    emit(0, Q_BLOCK, _attn_block0(qk_fn, v_fn, N_HEADS), True)

    def qblock(i, carry):
        o = _attn_block(i, qk_fn, v_fn, N_HEADS, s_meta, s_s, mx_s, l_s, acc_s)
        emit(pl.multiple_of(Q_BLOCK + (i - 1) * ATT_ROWS, Q_BLOCK), ATT_ROWS, o, False)
        return carry

    lax.fori_loop(1, (tp - Q_BLOCK) // ATT_ROWS + 1, qblock, 0)


def _attn_scratch(n_g, tp):
    g_rows = n_g * ATT_ROWS
    n_slots = (tp - Q_BLOCK) // ATT_ROWS
    return [pltpu.VMEM((g_rows, LANES), F32),
            pltpu.VMEM((n_slots, g_rows, ATT_ROWS), F32),
            pltpu.VMEM((g_rows, LANES), F32),
            pltpu.VMEM((g_rows, LANES), F32),
            pltpu.VMEM((g_rows, LANES), F32)]


def _mla(pc, qn, kvn, wuq, wukv, cos, sa, sb):
    b, tp, _ = pc.shape
    return pl.pallas_call(
        functools.partial(_mla_kernel, tp),
        grid=(b,),
        in_specs=[pl.BlockSpec((1, tp, PC_W), lambda i: (i, 0, 0)),
                  _resident((1, 256)), _resident((1, 256)),
                  _resident(wuq.shape), _resident(wukv.shape),
                  _resident(cos.shape), _resident(sa.shape), _resident(sb.shape)],
        out_specs=pl.BlockSpec((1, tp, 256), lambda i: (i, 0, 0)),
        out_shape=jax.ShapeDtypeStruct((b, tp, 256), BF16),
        scratch_shapes=[pltpu.VMEM((N_HEADS, tp, LANES), BF16),
                        pltpu.VMEM((N_HEADS, tp, LANES), BF16),
                        pltpu.VMEM((N_HEADS, tp, LANES), BF16)] + _attn_scratch(N_HEADS, tp),
        compiler_params=_cparams(),
        name="mla",
    )(pc, qn, kvn, wuq, wukv, cos, sa, sb)


def _diff_kernel(tp, lam_init, pd_ref, lam_ref, dn_ref, o_ref, qs_s, s_meta, s_s, mx_s, l_s, acc_s):
    n_maps = 2 * N_HEADS
    scale = DIFF_DK ** -0.5 * LOG2E
    group = lax.broadcasted_iota(jnp.int32, (1, 256), 1) >> 5
    lv = lam_ref[...]
    lam = (jnp.exp(jnp.sum(lv[0:1, :] * lv[1:2, :], axis=-1, keepdims=True))
           - jnp.exp(jnp.sum(lv[2:3, :] * lv[3:4, :], axis=-1, keepdims=True)) + lam_init)
    dn = dn_ref[...]

    def stack_queries(q0, n_rows):
        q = (pd_ref[0, pl.ds(q0, n_rows), 0:256].astype(F32) * scale).astype(BF16)
        for g in range(n_maps):
            qs_s[g * n_rows:(g + 1) * n_rows, :] = jnp.where(group == g, q, jnp.zeros_like(q))

    def qk_fn(q0, n_rows, krows):
        return _dot_nt(qs_s[0:n_maps * n_rows, :], pd_ref[0, krows, 256:512])

    def v_fn(g, krows):
        return pd_ref[0, krows, 512 + (g // 2) * LANES:512 + (g // 2 + 1) * LANES]

    def emit(q0, n_rows, o, first):
        per_head = []
        for h in range(N_HEADS):
            o1 = o[(2 * h) * n_rows:(2 * h + 1) * n_rows, :]
            o2 = o[(2 * h + 1) * n_rows:(2 * h + 2) * n_rows, :]
            od = o1 - lam * o2
            ms = jnp.sum(od * od, axis=-1, keepdims=True) * (1.0 / 64)
            per_head.append(od * lax.rsqrt(ms + EPS) * dn * (1.0 - lam_init))
        y = _heads_to_lanes(per_head)
        if first:
            qrow = lax.broadcasted_iota(jnp.int32, (n_rows, 1), 0)
            y = jnp.where(qrow >= FRONT_PAD, y, 0.0)
        o_ref[0, pl.ds(q0, n_rows), :] = y.astype(BF16)

    stack_queries(0, Q_BLOCK)
    emit(0, Q_BLOCK, _attn_block0(qk_fn, v_fn, n_maps), True)

    def qblock(i, carry):
        q0 = pl.multiple_of(Q_BLOCK + (i - 1) * ATT_ROWS, Q_BLOCK)
        stack_queries(q0, ATT_ROWS)
        o = _attn_block(i, qk_fn, v_fn, n_maps, s_meta, s_s, mx_s, l_s, acc_s)
        emit(q0, ATT_ROWS, o, False)
        return carry

    lax.fori_loop(1, (tp - Q_BLOCK) // ATT_ROWS + 1, qblock, 0)


def _diff(pd, lam_rows, dn, lam_init):
    b, tp, _ = pd.shape
    n_maps = 2 * N_HEADS
    return pl.pallas_call(
        functools.partial(_diff_kernel, tp, lam_init),
        grid=(b,),
        in_specs=[pl.BlockSpec((1, tp, PD_W), lambda i: (i, 0, 0)),
                  _resident(lam_rows.shape), _resident((1, LANES))],
        out_specs=pl.BlockSpec((1, tp, 256), lambda i: (i, 0, 0)),
        out_shape=jax.ShapeDtypeStruct((b, tp, 256), BF16),
        scratch_shapes=[pltpu.VMEM((n_maps * ATT_ROWS, 256), BF16)] + _attn_scratch(n_maps, tp),
        compiler_params=_cparams(),
        name="diffattn",
    )(pd, lam_rows, dn)


def _pad_cols(x, width):
    return jnp.pad(x, ((0, 0), (0, width - x.shape[1])))


def _rot_split(w):
    d = w.shape[0]
    return w.reshape(d, N_HEADS, 2, 32).transpose(0, 2, 1, 3).reshape(d, 256)


def _layout_w_in(w):
    sizes = (128, 128, 256, 16, 256, 256, 256, 256, 256, 192, 64, 32, 256, 256, 256)
    offs = [0]
    for s_ in sizes:
        offs.append(offs[-1] + s_)
    seg = [w[:, offs[i]:offs[i + 1]] for i in range(len(sizes))]
    (a_q, a_k, a_v, a_lr, a_og, r_q, r_k, r_v, r_og, c_cq, c_ckv, c_kpe, d_q, d_k, d_v) = seg
    d = w.shape[0]
    z = lambda n: jnp.zeros((d, n), w.dtype)
    dv_p = jnp.pad(d_v.reshape(d, N_HEADS, 64), ((0, 0), (0, 0), (0, 64))).reshape(d, 512)
    cols = [a_q, a_k, a_v, a_og, a_lr, z(112),
            _rot_split(r_q), _rot_split(r_k), r_v, r_og,
            c_cq, c_ckv, z(64), c_kpe, z(32),
            d_q, d_k, dv_p]
    return jnp.concatenate(cols, axis=1).astype(BF16)


def _tables(tp):
    pos = jnp.arange(tp, dtype=F32) - FRONT_PAD
    inv = ROPE_THETA ** (-jnp.arange(32, dtype=F32) / 32)
    ang = pos[:, None] * inv[None, :]
    ret_cos = jnp.tile(jnp.cos(ang), (1, N_HEADS))
    ret_sin = jnp.tile(jnp.sin(ang), (1, N_HEADS))
    inv16 = ROPE_THETA ** (-jnp.arange(16, dtype=F32) / 16)
    ang16 = pos[:, None] * inv16[None, :]
    c16, s16 = jnp.cos(ang16), jnp.sin(ang16)
    one = lambda n: jnp.ones((tp, n), F32)
    zero = lambda n: jnp.zeros((tp, n), F32)
    mla_cos = jnp.concatenate([one(64), c16, c16, one(32)], axis=1)
    mla_sa = jnp.concatenate([zero(80), s16, zero(32)], axis=1)
    mla_sb = jnp.concatenate([zero(64), -s16, zero(48)], axis=1)
    lg = jnp.log(1.0 - jnp.exp2(-5.0 - jnp.arange(N_HEADS, dtype=F32)))
    idx = jnp.arange(RET_BLOCK, dtype=F32)
    rel = idx[:, None] - idx[None, :]
    dmask = jnp.where(rel[None] >= 0, jnp.exp(rel[None] * lg[:, None, None]), 0.0)
    dmask = dmask.reshape(N_HEADS * RET_BLOCK, RET_BLOCK)
    lane_head = (jnp.arange(256) % 128) // 32
    qfac = jnp.exp((idx[:, None] + 1.0) * lg[lane_head][None, :])
    kfac = jnp.exp((RET_BLOCK - 1.0 - idx[:, None]) * lg[lane_head][None, :])
    dec = jnp.exp(RET_BLOCK * lg[lane_head])[:, None]
    return ret_cos, ret_sin, mla_cos, mla_sa, mla_sb, dmask, qfac, kfac, dec


def _chunk_cols(w):
    d, f = w.shape
    return w.reshape(d, f // FF_CHUNK, FF_CHUNK).transpose(1, 0, 2).astype(BF16)


def _chunk_rows(w):
    f, d = w.shape
    return w.reshape(f // FF_CHUNK, FF_CHUNK, d).astype(BF16)


def kernel(x, meta_tokens, attn_norm, w_in, gla_w_gate, gla_b_gate, gla_norm, ret_norm, mla_q_norm, mla_w_uq, mla_kv_norm, mla_w_ukv, diff_lambda, diff_norm, w_out, ffn_norm, ffn_w_gate, ffn_w_up, ffn_w_down, moe_router, moe_w_gate, moe_w_up, moe_w_down, final_norm):
    b, seq, d = x.shape
    tp = FRONT_PAD + N_META + seq
    n = b * tp
    meta = jnp.broadcast_to(meta_tokens[None].astype(x.dtype), (b, N_META, d))
    h = jnp.concatenate([jnp.zeros((b, FRONT_PAD, d), x.dtype), meta, x], axis=1).reshape(n, d)
    ret_cos, ret_sin, mla_cos, mla_sa, mla_sb, dmask, qfac, kfac, dec = _tables(tp)

    for li in range(DEPTH):
        pa, pb, pc, pd = _inproj(h, attn_norm[li][None, :], _layout_w_in(w_in[li]))
        pa, pb, pc, pd = (p.reshape(b, tp, -1) for p in (pa, pb, pc, pd))

        wgate = jnp.pad(gla_w_gate[li], ((0, LANES - GLA_GATE_RANK), (0, 0)))
        o_a = _gla(pa, wgate, gla_b_gate[li][None, :], jnp.tile(gla_norm[li], N_HEADS)[None, :])
        o_b = _ret(pb, ret_cos, ret_sin, dmask, qfac, kfac, dec, jnp.tile(ret_norm[li], N_HEADS)[None, :])

        qn = _pad_cols(mla_q_norm[li][None, :], 256)
        kvn = jnp.pad(mla_kv_norm[li][None, :], ((0, 0), (MLA_Q_RANK, 0)))
        wuq = jnp.pad(mla_w_uq[li].reshape(MLA_Q_RANK, N_HEADS, MLA_NOPE + MLA_ROPE),
                      ((0, 256 - MLA_Q_RANK), (0, 0), (0, LANES - MLA_NOPE - MLA_ROPE)))
        wuq = wuq.reshape(256, N_HEADS * LANES).astype(BF16)
        wukv = mla_w_ukv[li].reshape(MLA_KV_RANK, N_HEADS, 2, 64)
        wukv = jnp.pad(wukv, ((MLA_Q_RANK, 0), (0, 0), (0, 0), (0, 64)))
        wukv = wukv.transpose(0, 2, 1, 3).reshape(256, 2 * N_HEADS * LANES).astype(BF16)
        o_c = _mla(pc, qn, kvn, wuq, wukv, mla_cos, mla_sa, mla_sb)

        lam_init = 0.8 - 0.6 * math.exp(-0.3 * li)
        o_d = _diff(pd, diff_lambda[li], _pad_cols(diff_norm[li][None, :], LANES), lam_init)

        o_a, o_b, o_c, o_d = (o.reshape(n, 256) for o in (o_a, o_b, o_c, o_d))
        wo = w_out[li].astype(BF16)
        fn = ffn_norm[li][None, :]
        j = li // 2
        if li % 2 == 0:
            h, hn = _outproj(h, o_a, o_b, o_c, o_d, wo, fn)
            h = _swiglu(h, hn, _chunk_cols(ffn_w_gate[j]), _chunk_cols(ffn_w_up[j]),
                        _chunk_rows(ffn_w_down[j]))
        else:
            h, hn, gates = _outproj(h, o_a, o_b, o_c, o_d, wo, fn,
                                    router=_pad_cols(moe_router[j], LANES))
            for e in range(N_EXPERTS):
                h = _swiglu(h, hn, _chunk_cols(moe_w_gate[j, e]), _chunk_cols(moe_w_up[j, e]),
                            _chunk_rows(moe_w_down[j, e]), gates=gates, expert=e)

    return _final_norm(h.reshape(b, tp, d), final_norm[None, :])
```

```python
import functools
import math

import jax
import jax.numpy as jnp
from jax import lax
from jax.experimental import pallas as pl
from jax.experimental.pallas import tpu as pltpu

F32 = jnp.float32
BF16 = jnp.bfloat16

D_MODEL = 1024
DEPTH = 2
N_META = 16
CHUNK = 64
Q_BLOCK = 128
FRONT_PAD = Q_BLOCK - N_META
EPS = 1e-6
NEG = -1e30
ROPE_THETA = 10000.0
N_HEADS = 4
GLA_DK = 32
GLA_DV = 64
GLA_GATE_RANK = 16
GLA_TAU = 16.0
RET_DK = 64
MLA_Q_RANK = 192
MLA_KV_RANK = 64
MLA_NOPE = 64
MLA_ROPE = 32
DIFF_DK = 32
D_FF = 2816
N_EXPERTS = 8
D_FF_EXPERT = 3584

LANES = 128
ROW_TILE = 512
FF_CHUNK = 256
VMEM_LIMIT = 56 * 1024 * 1024

PA_W = 896
PB_W = 1024
PC_W = 384
PD_W = 1024


def _cparams(n_axes=1):
    return pltpu.CompilerParams(dimension_semantics=("arbitrary",) * n_axes,
                                vmem_limit_bytes=VMEM_LIMIT)


def _resident(shape):
    nd = len(shape)
    return pl.BlockSpec(shape, lambda *_: (0,) * nd, pipeline_mode=pl.Buffered(1))


def _sigmoid(x):
    return 1.0 / (1.0 + jnp.exp(-x))


def _split_bf16(x):
    hi = x.astype(BF16)
    lo = (x - hi.astype(F32)).astype(BF16)
    return hi, lo


def _dot(a, b):
    return jnp.dot(a, b, preferred_element_type=F32)


def _dot_nt(a, b):
    return lax.dot_general(a, b, (((1,), (1,)), ((), ())), preferred_element_type=F32)


def _inproj_kernel(h_ref, g_ref, w_ref, pa_ref, pb_ref, pc_ref, pd_ref):
    x = h_ref[...]
    ms = jnp.mean(x * x, axis=-1, keepdims=True)
    y = (x * lax.rsqrt(ms + EPS) * g_ref[...]).astype(BF16)
    off = 0
    for o_ref, width in ((pa_ref, PA_W), (pb_ref, PB_W), (pc_ref, PC_W), (pd_ref, PD_W)):
        o_ref[...] = _dot(y, w_ref[:, off:off + width]).astype(BF16)
        off += width


def _inproj(h, g, w):
    n = h.shape[0]
    wtot = PA_W + PB_W + PC_W + PD_W
    row = lambda width: pl.BlockSpec((ROW_TILE, width), lambda i: (i, 0))
    return pl.pallas_call(
        _inproj_kernel,
        grid=(n // ROW_TILE,),
        in_specs=[row(D_MODEL), _resident((1, D_MODEL)), _resident((D_MODEL, wtot))],
        out_specs=[row(PA_W), row(PB_W), row(PC_W), row(PD_W)],
        out_shape=[jax.ShapeDtypeStruct((n, w_), BF16) for w_ in (PA_W, PB_W, PC_W, PD_W)],
        compiler_params=_cparams(),
        name="inproj",
    )(h, g, w)


def _outproj_kernel(with_router, h_ref, oa_ref, ob_ref, oc_ref, od_ref, wo_ref, fn_ref, *rest):
    if with_router:
        router_ref, hmid_ref, hn_ref, info_ref, cnt_ref, carry_ref = rest
    else:
        hmid_ref, hn_ref = rest
    o = jnp.concatenate([oa_ref[...], ob_ref[...], oc_ref[...], od_ref[...]], axis=1)
    hm = h_ref[...] + _dot(o, wo_ref[...])
    hmid_ref[...] = hm
    ms = jnp.mean(hm * hm, axis=-1, keepdims=True)
    y = hm * lax.rsqrt(ms + EPS) * fn_ref[...]
    hn_ref[...] = y.astype(hn_ref.dtype)
    if with_router:
        y_hi, y_lo = _split_bf16(y)
        r_hi, r_lo = _split_bf16(router_ref[...])
        logits = _dot(y_hi, r_hi) + _dot(y_hi, r_lo) + _dot(y_lo, r_hi)
        lane = lax.broadcasted_iota(jnp.int32, logits.shape, 1).astype(F32)
        ninf = float("-inf")
        logits = jnp.where(lane < N_EXPERTS, logits, ninf)
        m1 = jnp.max(logits, axis=-1, keepdims=True)
        i1 = jnp.min(jnp.where(logits == m1, lane, float(LANES)), axis=-1, keepdims=True)
        rest_l = jnp.where(lane == i1, ninf, logits)
        m2 = jnp.max(rest_l, axis=-1, keepdims=True)
        i2 = jnp.min(jnp.where(rest_l == m2, lane, float(LANES)), axis=-1, keepdims=True)
        e2 = jnp.exp(m2 - m1)
        den = 1.0 + e2
        @pl.when(pl.program_id(0) == 0)
        def _():
            carry_ref[...] = jnp.zeros_like(carry_ref)
        sel = jnp.where(lane == i1, 1.0, 0.0) + jnp.where(lane == i2, 1.0, 0.0)
        rows = sel.shape[0]
        below = (lax.broadcasted_iota(jnp.int32, (rows, rows), 1)
                 < lax.broadcasted_iota(jnp.int32, (rows, rows), 0))
        count = _dot(jnp.where(below, 1.0, 0.0).astype(BF16), sel.astype(BF16)) + carry_ref[...]
        r1 = jnp.sum(jnp.where(lane == i1, count, 0.0), axis=-1, keepdims=True)
        r2 = jnp.sum(jnp.where(lane == i2, count, 0.0), axis=-1, keepdims=True)
        total = carry_ref[...] + jnp.sum(sel, axis=0, keepdims=True)
        carry_ref[...] = total
        cnt_ref[...] = jnp.broadcast_to(total, cnt_ref.shape)
        info = jnp.where(lane == 0, i1, 0.0)
        for k, val in enumerate((i2, 1.0 / den, e2 / den, r1, r2), start=1):
            info = jnp.where(lane == k, val, info)
        info_ref[...] = info


def _outproj(h, oa, ob, oc, od, wo, fn, router=None):
    n = h.shape[0]
    row = lambda width: pl.BlockSpec((ROW_TILE, width), lambda i: (i, 0))
    in_specs = [row(D_MODEL), row(256), row(256), row(256), row(256),
                _resident((D_MODEL, D_MODEL)), _resident((1, D_MODEL))]
    out_specs = [row(D_MODEL), row(D_MODEL)]
    out_shape = [jax.ShapeDtypeStruct((n, D_MODEL), F32),
                 jax.ShapeDtypeStruct((n, D_MODEL), BF16 if router is None else F32)]
    args = [h, oa, ob, oc, od, wo, fn]
    scratch = []
    if router is not None:
        in_specs.append(_resident((D_MODEL, LANES)))
        out_specs += [row(LANES), pl.BlockSpec((8, LANES), lambda i: (0, 0))]
        out_shape += [jax.ShapeDtypeStruct((n, LANES), F32), jax.ShapeDtypeStruct((8, LANES), F32)]
        args.append(router)
        scratch.append(pltpu.VMEM((1, LANES), F32))
    return pl.pallas_call(
        functools.partial(_outproj_kernel, router is not None),
        grid=(n // ROW_TILE,),
        in_specs=in_specs, out_specs=out_specs, out_shape=out_shape,
        scratch_shapes=scratch,
        compiler_params=_cparams(),
        name="outproj_router" if router is not None else "outproj",
    )(*args)


def _dispatch_kernel(pos_ref, x_ref, xs_in_ref, xs_ref, sem):
    del xs_in_ref
    n_rows = x_ref.shape[0]

    def row_copy(r, k):
        p = pos_ref[0, 0, 2 * r + k]
        return pltpu.make_async_copy(x_ref.at[pl.ds(r, 1), :], xs_ref.at[pl.ds(p, 1), :], sem)

    def start(r, carry):
        row_copy(r, 0).start()
        row_copy(r, 1).start()
        return carry

    def wait(r, carry):
        row_copy(r, 0).wait()
        row_copy(r, 1).wait()
        return carry

    lax.fori_loop(0, n_rows, start, 0, unroll=8)
    lax.fori_loop(0, n_rows, wait, 0, unroll=8)


def _dispatch(pos, x, n_sorted):
    n, d = x.shape
    xs0 = jnp.zeros((n_sorted, d), x.dtype)
    return pl.pallas_call(
        _dispatch_kernel,
        grid=(n // ROW_TILE,),
        in_specs=[pl.BlockSpec((1, 1, 2 * ROW_TILE), lambda i: (i, 0, 0), memory_space=pltpu.SMEM),
                  pl.BlockSpec((ROW_TILE, d), lambda i: (i, 0)),
                  pl.BlockSpec(memory_space=pl.ANY)],
        out_specs=pl.BlockSpec(memory_space=pl.ANY),
        out_shape=jax.ShapeDtypeStruct((n_sorted, d), x.dtype),
        scratch_shapes=[pltpu.SemaphoreType.DMA(())],
        input_output_aliases={2: 0},
        compiler_params=_cparams(),
        name="moe_dispatch",
    )(pos, x, xs0)


def _experts_kernel(n_chunks, te_ref, tv_ref, xs_ref, wg_ref, wu_ref, wd_ref, ys_ref, acc_ref):
    i = pl.program_id(0)

    @pl.when(tv_ref[i] > 0)
    def _():
        x = xs_ref[...].astype(BF16)
        acc_ref[...] = jnp.zeros_like(acc_ref)

        def body(c, carry):
            g = _dot(x, wg_ref[0, c])
            u = _dot(x, wu_ref[0, c])
            a = (g * _sigmoid(g) * u).astype(BF16)
            acc_ref[...] += _dot(a, wd_ref[0, c])
            return carry

        lax.fori_loop(0, n_chunks, body, 0)
        ys_ref[...] = acc_ref[...]

    @pl.when(tv_ref[i] == 0)
    def _():
        ys_ref[...] = jnp.zeros_like(ys_ref)


def _experts(tile_expert, tile_valid, xs, wg, wu, wd):
    n_sorted, d = xs.shape
    n_chunks = wg.shape[1]
    wspec = lambda shape: pl.BlockSpec((1,) + shape[1:], lambda i, te, tv: (te[i], 0, 0, 0),
                                       pipeline_mode=pl.Buffered(1))
    row = pl.BlockSpec((ROW_TILE, d), lambda i, te, tv: (i, 0))
    return pl.pallas_call(
        functools.partial(_experts_kernel, n_chunks),
        grid_spec=pltpu.PrefetchScalarGridSpec(
            num_scalar_prefetch=2,
            grid=(n_sorted // ROW_TILE,),
            in_specs=[row, wspec(wg.shape), wspec(wu.shape), wspec(wd.shape)],
            out_specs=row,
            scratch_shapes=[pltpu.VMEM((ROW_TILE, d), F32)]),
        out_shape=jax.ShapeDtypeStruct((n_sorted, d), F32),
        compiler_params=_cparams(),
        name="moe_experts",
    )(tile_expert, tile_valid, xs, wg, wu, wd)


def _combine_kernel(pos_ref, h_ref, info_ref, ys_ref, o_ref, y_buf, sem):
    n_rows = h_ref.shape[0]

    def row_copy(r, k):
        p = pos_ref[0, 0, 2 * r + k]
        return pltpu.make_async_copy(ys_ref.at[pl.ds(p, 1), :], y_buf.at[k, pl.ds(r, 1), :], sem)

    def start(r, carry):
        row_copy(r, 0).start()
        row_copy(r, 1).start()
        return carry

    def wait(r, carry):
        row_copy(r, 0).wait()
        row_copy(r, 1).wait()
        return carry

    lax.fori_loop(0, n_rows, start, 0, unroll=8)
    lax.fori_loop(0, n_rows, wait, 0, unroll=8)
    info = info_ref[...]
    o_ref[...] = h_ref[...] + info[:, 2:3] * y_buf[0] + info[:, 3:4] * y_buf[1]


def _combine(pos, h, info, ys):
    n, d = h.shape
    return pl.pallas_call(
        _combine_kernel,
        grid=(n // ROW_TILE,),
        in_specs=[pl.BlockSpec((1, 1, 2 * ROW_TILE), lambda i: (i, 0, 0), memory_space=pltpu.SMEM),
                  pl.BlockSpec((ROW_TILE, d), lambda i: (i, 0)),
                  pl.BlockSpec((ROW_TILE, LANES), lambda i: (i, 0)),
                  pl.BlockSpec(memory_space=pl.ANY)],
        out_specs=pl.BlockSpec((ROW_TILE, d), lambda i: (i, 0)),
        out_shape=jax.ShapeDtypeStruct((n, d), F32),
        scratch_shapes=[pltpu.VMEM((2, ROW_TILE, d), F32), pltpu.SemaphoreType.DMA(())],
        input_output_aliases={1: 0},
        compiler_params=_cparams(),
        name="moe_combine",
    )(pos, h, info, ys)


def _routed_moe(h, hn, info, cnt, wg, wu, wd):
    n = h.shape[0]
    n_sorted = 2 * n + N_EXPERTS * ROW_TILE
    n_tiles = n_sorted // ROW_TILE
    counts = cnt[0, :N_EXPERTS].astype(jnp.int32)
    tiles_e = (counts + ROW_TILE - 1) // ROW_TILE
    tile_end = jnp.cumsum(tiles_e)
    row_base = (tile_end - tiles_e) * ROW_TILE
    e_idx = info[:, 0:2].astype(jnp.int32)
    pos = row_base[e_idx] + info[:, 4:6].astype(jnp.int32)
    pos = pos.reshape(n // ROW_TILE, 1, 2 * ROW_TILE)
    tile_ids = jnp.arange(n_tiles, dtype=jnp.int32)
    tile_expert = jnp.minimum(jnp.searchsorted(tile_end, tile_ids, side="right"),
                              N_EXPERTS - 1).astype(jnp.int32)
    tile_valid = (tile_ids < tile_end[-1]).astype(jnp.int32)
    xs = _dispatch(pos, hn, n_sorted)
    ys = _experts(tile_expert, tile_valid, xs, wg, wu, wd)
    return _combine(pos, h, info, ys)


def _swiglu_kernel(expert, n_chunks, h_ref, hn_ref, *rest):
    if expert is None:
        wg_ref, wu_ref, wd_ref, o_ref, acc_ref = rest
    else:
        gate_ref, wg_ref, wu_ref, wd_ref, o_ref, acc_ref = rest
    hn = hn_ref[...]
    acc_ref[...] = jnp.zeros_like(acc_ref)

    def body(c, carry):
        g = _dot(hn, wg_ref[c])
        u = _dot(hn, wu_ref[c])
        a = (g * _sigmoid(g) * u).astype(BF16)
        acc_ref[...] += _dot(a, wd_ref[c])
        return carry

    lax.fori_loop(0, n_chunks, body, 0)
    y = acc_ref[...]
    if expert is not None:
        y = gate_ref[:, expert:expert + 1] * y
    o_ref[...] = h_ref[...] + y


def _swiglu(h, hn, wg, wu, wd, gates=None, expert=None):
    n = h.shape[0]
    n_chunks = wg.shape[0]
    row = lambda width: pl.BlockSpec((ROW_TILE, width), lambda i: (i, 0))
    in_specs = [row(D_MODEL), row(D_MODEL)]
    args = [h, hn]
    if expert is not None:
        in_specs.append(row(LANES))
        args.append(gates)
    in_specs += [_resident(wg.shape), _resident(wu.shape), _resident(wd.shape)]
    args += [wg, wu, wd]
    return pl.pallas_call(
        functools.partial(_swiglu_kernel, expert, n_chunks),
        grid=(n // ROW_TILE,),
        in_specs=in_specs,
        out_specs=row(D_MODEL),
        out_shape=jax.ShapeDtypeStruct((n, D_MODEL), F32),
        scratch_shapes=[pltpu.VMEM((ROW_TILE, D_MODEL), F32)],
        input_output_aliases={0: 0},
        compiler_params=_cparams(),
        name="swiglu" if expert is None else f"moe_expert{expert}",
    )(*args)


def _final_norm_kernel(h_ref, g_ref, o_ref):
    x = h_ref[0]
    ms = jnp.mean(x * x, axis=-1, keepdims=True)
    o_ref[0] = x * lax.rsqrt(ms + EPS) * g_ref[...]


def _final_norm(h3, g):
    b, tp, d = h3.shape
    nb = tp // Q_BLOCK - 1
    return pl.pallas_call(
        _final_norm_kernel,
        grid=(b, nb),
        in_specs=[pl.BlockSpec((1, Q_BLOCK, d), lambda i, j: (i, j + 1, 0)), _resident((1, d))],
        out_specs=pl.BlockSpec((1, Q_BLOCK, d), lambda i, j: (i, j, 0)),
        out_shape=jax.ShapeDtypeStruct((b, nb * Q_BLOCK, d), F32),
        compiler_params=_cparams(2),
        name="final_norm",
    )(h3, g)


def _group_ones(n, group_shift):
    r = lax.broadcasted_iota(jnp.int32, (n, n), 0) >> group_shift
    c = lax.broadcasted_iota(jnp.int32, (n, n), 1) >> group_shift
    return jnp.where(r == c, 1.0, 0.0).astype(BF16)


def _gla_kernel(tp, pa_ref, wg_ref, bg_ref, gn_ref, o_ref, s_ref):
    c_len = CHUNK
    n_chunks = tp // c_len
    ri = lax.broadcasted_iota(jnp.int32, (c_len, c_len), 0)
    ci = lax.broadcasted_iota(jnp.int32, (c_len, c_len), 1)
    tri_bf = jnp.where(ri >= ci, 1.0, 0.0).astype(BF16)
    r4 = lax.broadcasted_iota(jnp.int32, (N_HEADS * c_len, c_len), 0) & (c_len - 1)
    c4 = lax.broadcasted_iota(jnp.int32, (N_HEADS * c_len, c_len), 1)
    tri4 = r4 >= c4
    qhead = lax.broadcasted_iota(jnp.int32, (1, N_HEADS * GLA_DK), 1) >> 5
    ehead = lax.broadcasted_iota(jnp.int32, (1, N_HEADS * GLA_DV), 1) >> 6
    shead = lax.broadcasted_iota(jnp.int32, (N_HEADS * GLA_DK, 1), 0) >> 5
    bd = shead == ehead
    gsum = _group_ones(N_HEADS * GLA_DV, 6)
    wg_hi, wg_lo = _split_bf16(wg_ref[...])
    bg = bg_ref[...]
    gn = gn_ref[...]
    scale = GLA_DK ** -0.5

    s_ref[...] = jnp.zeros_like(s_ref)
    o_ref[0, 0:c_len, :] = jnp.zeros((c_len, N_HEADS * GLA_DV), BF16)

    def chunk(c, carry):
        r0 = pl.multiple_of(c * c_len, c_len)
        rows = pl.ds(r0, c_len)
        q = pa_ref[0, rows, 0:128].astype(F32) * scale
        k = pa_ref[0, rows, 128:256].astype(F32)
        v = pa_ref[0, rows, 256:512]
        og = pa_ref[0, rows, 512:768].astype(F32)
        lr = pa_ref[0, rows, 768:896]
        valid = (r0 + lax.broadcasted_iota(jnp.int32, (c_len, 1), 0)) >= FRONT_PAD

        pre = _dot(lr, wg_hi) + _dot(lr, wg_lo) + bg
        logsig = jnp.minimum(pre, 0.0) - jnp.log1p(jnp.exp(-jnp.abs(pre)))
        g = jnp.where(valid, logsig * (1.0 / GLA_TAU), 0.0)
        g_hi, g_lo = _split_bf16(g)
        cum = _dot(tri_bf, g_hi) + _dot(tri_bf, g_lo)
        cum_last = cum[c_len - 1:c_len, :]
        qt = q * jnp.exp(cum)
        kt = (k * jnp.exp(-cum)).astype(BF16)
        kd = k * jnp.exp(cum_last - cum)

        qt_bf = qt.astype(BF16)
        qs = jnp.concatenate([jnp.where(qhead == h, qt_bf, jnp.zeros_like(qt_bf))
                              for h in range(N_HEADS)], axis=0)
        a = _dot_nt(qs, kt)
        a = jnp.where(tri4, a, 0.0).astype(BF16)
        r = _dot(a, v)
        o = _dot(qt_bf, s_ref[...].astype(BF16))
        for h in range(N_HEADS):
            o = o + jnp.where(ehead == h, r[h * c_len:(h + 1) * c_len, :], 0.0)

        kd_t = kd.T.astype(BF16)
        upd = _dot(kd_t, v)
        dec = jnp.exp(cum.T[:, c_len - 1:c_len])
        s_ref[...] = dec * s_ref[...] + jnp.where(bd, upd, 0.0)

        ms = _dot((o * o).astype(BF16), gsum) * (1.0 / GLA_DV)
        y = o * lax.rsqrt(ms + EPS) * gn * (og * _sigmoid(og))
        o_ref[0, rows, :] = jnp.where(valid, y, 0.0).astype(BF16)
        return carry

    lax.fori_loop(1, n_chunks, chunk, 0)


def _gla(pa, wg, bg, gn):
    b, tp, _ = pa.shape
    return pl.pallas_call(
        functools.partial(_gla_kernel, tp),
        grid=(b,),
        in_specs=[pl.BlockSpec((1, tp, PA_W), lambda i: (i, 0, 0)),
                  _resident((LANES, LANES)), _resident((1, LANES)), _resident((1, 256))],
        out_specs=pl.BlockSpec((1, tp, 256), lambda i: (i, 0, 0)),
        out_shape=jax.ShapeDtypeStruct((b, tp, 256), BF16),
        scratch_shapes=[pltpu.VMEM((N_HEADS * GLA_DK, N_HEADS * GLA_DV), F32)],
        compiler_params=_cparams(),
        name="gla",
    )(pa, wg, bg, gn)


RET_BLOCK = 128


def _ret_kernel(tp, pb_ref, cos_ref, sin_ref, dmask_ref, qfac_ref, kfac_ref, dec_ref, gn_ref,
                o_ref, s_ref):
    blk = RET_BLOCK
    n_blocks = tp // blk
    qhead = (lax.broadcasted_iota(jnp.int32, (1, 256), 1) & 127) >> 5
    ehead = lax.broadcasted_iota(jnp.int32, (1, 256), 1) >> 6
    shead = (lax.broadcasted_iota(jnp.int32, (256, 1), 0) & 127) >> 5
    bd = shead == ehead
    gsum = _group_ones(256, 6)
    gn = gn_ref[...]
    s_ref[...] = jnp.zeros_like(s_ref)

    def rope(x, cos, sin):
        x1, x2 = x[:, :128], x[:, 128:]
        return jnp.concatenate([x1 * cos - x2 * sin, x1 * sin + x2 * cos], axis=1)

    def block(j, carry):
        r0 = pl.multiple_of(j * blk, blk)
        rows = pl.ds(r0, blk)
        cos = cos_ref[rows, :]
        sin = sin_ref[rows, :]
        q = rope(pb_ref[0, rows, 0:256].astype(F32), cos, sin)
        k = rope(pb_ref[0, rows, 256:512].astype(F32), cos, sin) * (RET_DK ** -0.5)
        v = pb_ref[0, rows, 512:768]
        og = pb_ref[0, rows, 768:1024].astype(F32)
        valid = (r0 + lax.broadcasted_iota(jnp.int32, (blk, 1), 0)) >= FRONT_PAD

        q_bf = q.astype(BF16)
        qs = jnp.concatenate([jnp.where(qhead == h, q_bf, jnp.zeros_like(q_bf))
                              for h in range(N_HEADS)], axis=0)
        a = (_dot_nt(qs, k.astype(BF16)) * dmask_ref[...]).astype(BF16)
        r = _dot(a, v)
        o = _dot((q * qfac_ref[...]).astype(BF16), s_ref[...].astype(BF16))
        for h in range(N_HEADS):
            o = o + jnp.where(ehead == h, r[h * blk:(h + 1) * blk, :], 0.0)

        kd_t = (k * kfac_ref[...]).T.astype(BF16)
        upd = _dot(kd_t, v)
        s_ref[...] = dec_ref[...] * s_ref[...] + jnp.where(bd, upd, 0.0)

        mu = _dot(o.astype(BF16), gsum) * (1.0 / 64)
        xc = o - mu
        var = _dot((xc * xc).astype(BF16), gsum) * (1.0 / 64)
        y = xc * lax.rsqrt(var + EPS) * gn * (og * _sigmoid(og))
        o_ref[0, rows, :] = jnp.where(valid, y, 0.0).astype(BF16)
        return carry

    lax.fori_loop(0, n_blocks, block, 0)


def _ret(pb, cos, sin, dmask, qfac, kfac, dec, gn):
    b, tp, _ = pb.shape
    return pl.pallas_call(
        functools.partial(_ret_kernel, tp),
        grid=(b,),
        in_specs=[pl.BlockSpec((1, tp, PB_W), lambda i: (i, 0, 0)),
                  _resident(cos.shape), _resident(sin.shape), _resident(dmask.shape),
                  _resident(qfac.shape), _resident(kfac.shape), _resident(dec.shape),
                  _resident((1, 256))],
        out_specs=pl.BlockSpec((1, tp, 256), lambda i: (i, 0, 0)),
        out_shape=jax.ShapeDtypeStruct((b, tp, 256), BF16),
        scratch_shapes=[pltpu.VMEM((256, 256), F32)],
        compiler_params=_cparams(),
        name="retention",
    )(pb, cos, sin, dmask, qfac, kfac, dec, gn)


ATT_ROWS = 256
LOG2E = 1.4426950408889634


def _mask_groups(mask, s, n_g):
    rows = s.shape[0] // n_g
    return jnp.concatenate([jnp.where(mask, s[g * rows:(g + 1) * rows, :], NEG)
                            for g in range(n_g)], axis=0)


def _attn_block0(qk_fn, v_fn, n_g):
    rows = pl.ds(0, Q_BLOCK)
    qrow = lax.broadcasted_iota(jnp.int32, (Q_BLOCK, 1), 0)
    kcol = lax.broadcasted_iota(jnp.int32, (1, Q_BLOCK), 1)
    mask = jnp.logical_and(kcol <= qrow, kcol >= FRONT_PAD)
    s = _mask_groups(mask, qk_fn(0, Q_BLOCK, rows), n_g)
    p = jnp.exp2(s - jnp.max(s, axis=-1, keepdims=True))
    l = jnp.sum(p, axis=-1, keepdims=True)
    p_bf = p.astype(BF16)
    pv = jnp.concatenate([_dot(p_bf[g * Q_BLOCK:(g + 1) * Q_BLOCK, :], v_fn(g, rows))
                          for g in range(n_g)], axis=0)
    return pv / l


def _attn_block(i, qk_fn, v_fn, n_g, s_meta, s_s, mx_s, l_s, acc_s):
    rr = ATT_ROWS
    q0 = pl.multiple_of(Q_BLOCK + (i - 1) * rr, Q_BLOCK)
    meta_rows = pl.ds(0, Q_BLOCK)
    kcol = lax.broadcasted_iota(jnp.int32, (1, Q_BLOCK), 1)

    def key_rows(j):
        return pl.ds(pl.multiple_of(Q_BLOCK + j * rr, Q_BLOCK), rr)

    s = jnp.where(kcol >= FRONT_PAD, qk_fn(q0, rr, meta_rows), NEG)
    s_meta[...] = s
    mx_s[...] = s

    def pass1(j, carry):
        sj = qk_fn(q0, rr, key_rows(j))
        s_s[j] = sj
        mx_s[...] = jnp.maximum(mx_s[...], jnp.maximum(sj[:, :LANES], sj[:, LANES:]))
        return carry

    lax.fori_loop(0, i - 1, pass1, 0)
    causal = (lax.broadcasted_iota(jnp.int32, (rr, rr), 1)
              <= lax.broadcasted_iota(jnp.int32, (rr, rr), 0))
    sd = _mask_groups(causal, qk_fn(q0, rr, pl.ds(q0, rr)), n_g)
    s_s[i - 1] = sd
    m = jnp.max(jnp.maximum(mx_s[...], jnp.maximum(sd[:, :LANES], sd[:, LANES:])),
                axis=-1, keepdims=True)
    mx_s[...] = jnp.broadcast_to(m, mx_s.shape)

    p = jnp.exp2(s_meta[...] - mx_s[...])
    l_s[...] = p
    p_bf = p.astype(BF16)
    for g in range(n_g):
        acc_s[g * rr:(g + 1) * rr, :] = _dot(p_bf[g * rr:(g + 1) * rr, :], v_fn(g, meta_rows))

    def pass2(j, carry):
        sj = s_s[j]
        mrep = mx_s[...]
        p0 = jnp.exp2(sj[:, :LANES] - mrep)
        p1 = jnp.exp2(sj[:, LANES:] - mrep)
        l_s[...] += p0 + p1
        pj = jnp.concatenate([p0.astype(BF16), p1.astype(BF16)], axis=1)
        for g in range(n_g):
            acc_s[g * rr:(g + 1) * rr, :] += _dot(pj[g * rr:(g + 1) * rr, :], v_fn(g, key_rows(j)))
        return carry

    lax.fori_loop(0, i, pass2, 0)
    return acc_s[...] / jnp.sum(l_s[...], axis=-1, keepdims=True)


def _heads_to_lanes(per_head):
    lo = per_head[0] + pltpu.roll(per_head[1], 64, 1)
    hi = per_head[2] + pltpu.roll(per_head[3], 64, 1)
    return jnp.concatenate([lo, hi], axis=1)


def _mla_kernel(tp, pc_ref, qn_ref, kvn_ref, wuq_ref, wukv_ref, cos_ref, sa_ref, sb_ref,
                o_ref, q_s, k_s, v_s, s_meta, s_s, mx_s, l_s, acc_s):
    n_blocks = tp // Q_BLOCK
    scale = (MLA_NOPE + MLA_ROPE) ** -0.5 * LOG2E
    is_q = lax.broadcasted_iota(jnp.int32, (1, 256), 1) < MLA_Q_RANK

    def prep(i, carry):
        r0 = pl.multiple_of(i * Q_BLOCK, Q_BLOCK)
        rows = pl.ds(r0, Q_BLOCK)
        x = pc_ref[0, rows, 0:256].astype(F32)
        x2 = x * x
        ms_q = jnp.sum(jnp.where(is_q, x2, 0.0), axis=-1, keepdims=True) * (1.0 / MLA_Q_RANK)
        ms_kv = jnp.sum(jnp.where(is_q, 0.0, x2), axis=-1, keepdims=True) * (1.0 / MLA_KV_RANK)
        yq = (x * lax.rsqrt(ms_q + EPS) * qn_ref[...]).astype(BF16)
        ykv = (x * lax.rsqrt(ms_kv + EPS) * kvn_ref[...]).astype(BF16)
        cq = _dot(yq, wuq_ref[...])
        kv = _dot(ykv, wukv_ref[...])
        cos = cos_ref[rows, :]
        sa = sa_ref[rows, :]
        sb = sb_ref[rows, :]

        def rope(t):
            return t * cos + pltpu.roll(t, 16, 1) * sa + pltpu.roll(t, LANES - 16, 1) * sb

        kpe = rope(pc_ref[0, rows, 256:384].astype(F32))
        for h in range(N_HEADS):
            q_s[h, rows, :] = (rope(cq[:, h * LANES:(h + 1) * LANES]) * scale).astype(BF16)
            k_s[h, rows, :] = (kv[:, h * LANES:(h + 1) * LANES] + kpe).astype(BF16)
            v_s[h, rows, :] = kv[:, (N_HEADS + h) * LANES:(N_HEADS + h + 1) * LANES].astype(BF16)
        return carry

    lax.fori_loop(0, n_blocks, prep, 0)

    def qk_fn(q0, n_rows, krows):
        return jnp.concatenate([_dot_nt(q_s[h, pl.ds(q0, n_rows), :], k_s[h, krows, :])
                                for h in range(N_HEADS)], axis=0)

    def v_fn(h, krows):
        return v_s[h, krows, :]

    def emit(q0, n_rows, o, first):
        y = _heads_to_lanes([o[h * n_rows:(h + 1) * n_rows, :] for h in range(N_HEADS)])
        if first:
            qrow = lax.broadcasted_iota(jnp.int32, (n_rows, 1), 0)
            y = jnp.where(qrow >= FRONT_PAD, y, 0.0)
        o_ref[0, pl.ds(q0, n_rows), :] = y.astype(BF16)

    emit(0, Q_BLOCK, _attn_block0(qk_fn, v_fn, N_HEADS), True)

    def qblock(i, carry):
        o = _attn_block(i, qk_fn, v_fn, N_HEADS, s_meta, s_s, mx_s, l_s, acc_s)
        emit(pl.multiple_of(Q_BLOCK + (i - 1) * ATT_ROWS, Q_BLOCK), ATT_ROWS, o, False)
        return carry

    lax.fori_loop(1, (tp - Q_BLOCK) // ATT_ROWS + 1, qblock, 0)


def _attn_scratch(n_g, tp):
    g_rows = n_g * ATT_ROWS
    n_slots = (tp - Q_BLOCK) // ATT_ROWS
    return [pltpu.VMEM((g_rows, LANES), F32),
            pltpu.VMEM((n_slots, g_rows, ATT_ROWS), F32),
            pltpu.VMEM((g_rows, LANES), F32),
            pltpu.VMEM((g_rows, LANES), F32),
            pltpu.VMEM((g_rows, LANES), F32)]


def _mla(pc, qn, kvn, wuq, wukv, cos, sa, sb):
    b, tp, _ = pc.shape
    return pl.pallas_call(
        functools.partial(_mla_kernel, tp),
        grid=(b,),
        in_specs=[pl.BlockSpec((1, tp, PC_W), lambda i: (i, 0, 0)),
                  _resident((1, 256)), _resident((1, 256)),
                  _resident(wuq.shape), _resident(wukv.shape),
                  _resident(cos.shape), _resident(sa.shape), _resident(sb.shape)],
        out_specs=pl.BlockSpec((1, tp, 256), lambda i: (i, 0, 0)),
        out_shape=jax.ShapeDtypeStruct((b, tp, 256), BF16),
        scratch_shapes=[pltpu.VMEM((N_HEADS, tp, LANES), BF16),
                        pltpu.VMEM((N_HEADS, tp, LANES), BF16),
                        pltpu.VMEM((N_HEADS, tp, LANES), BF16)] + _attn_scratch(N_HEADS, tp),
        compiler_params=_cparams(),
        name="mla",
    )(pc, qn, kvn, wuq, wukv, cos, sa, sb)


def _diff_kernel(tp, lam_init, pd_ref, lam_ref, dn_ref, o_ref, qs_s, s_meta, s_s, mx_s, l_s, acc_s):
    n_maps = 2 * N_HEADS
    scale = DIFF_DK ** -0.5 * LOG2E
    group = lax.broadcasted_iota(jnp.int32, (1, 256), 1) >> 5
    lv = lam_ref[...]
    lam = (jnp.exp(jnp.sum(lv[0:1, :] * lv[1:2, :], axis=-1, keepdims=True))
           - jnp.exp(jnp.sum(lv[2:3, :] * lv[3:4, :], axis=-1, keepdims=True)) + lam_init)
    dn = dn_ref[...]

    def stack_queries(q0, n_rows):
        q = (pd_ref[0, pl.ds(q0, n_rows), 0:256].astype(F32) * scale).astype(BF16)
        for g in range(n_maps):
            qs_s[g * n_rows:(g + 1) * n_rows, :] = jnp.where(group == g, q, jnp.zeros_like(q))

    def qk_fn(q0, n_rows, krows):
        return _dot_nt(qs_s[0:n_maps * n_rows, :], pd_ref[0, krows, 256:512])

    def v_fn(g, krows):
        return pd_ref[0, krows, 512 + (g // 2) * LANES:512 + (g // 2 + 1) * LANES]

    def emit(q0, n_rows, o, first):
        per_head = []
        for h in range(N_HEADS):
            o1 = o[(2 * h) * n_rows:(2 * h + 1) * n_rows, :]
            o2 = o[(2 * h + 1) * n_rows:(2 * h + 2) * n_rows, :]
            od = o1 - lam * o2
            ms = jnp.sum(od * od, axis=-1, keepdims=True) * (1.0 / 64)
            per_head.append(od * lax.rsqrt(ms + EPS) * dn * (1.0 - lam_init))
        y = _heads_to_lanes(per_head)
        if first:
            qrow = lax.broadcasted_iota(jnp.int32, (n_rows, 1), 0)
            y = jnp.where(qrow >= FRONT_PAD, y, 0.0)
        o_ref[0, pl.ds(q0, n_rows), :] = y.astype(BF16)

    stack_queries(0, Q_BLOCK)
    emit(0, Q_BLOCK, _attn_block0(qk_fn, v_fn, n_maps), True)

    def qblock(i, carry):
        q0 = pl.multiple_of(Q_BLOCK + (i - 1) * ATT_ROWS, Q_BLOCK)
        stack_queries(q0, ATT_ROWS)
        o = _attn_block(i, qk_fn, v_fn, n_maps, s_meta, s_s, mx_s, l_s, acc_s)
        emit(q0, ATT_ROWS, o, False)
        return carry

    lax.fori_loop(1, (tp - Q_BLOCK) // ATT_ROWS + 1, qblock, 0)


def _diff(pd, lam_rows, dn, lam_init):
    b, tp, _ = pd.shape
    n_maps = 2 * N_HEADS
    return pl.pallas_call(
        functools.partial(_diff_kernel, tp, lam_init),
        grid=(b,),
        in_specs=[pl.BlockSpec((1, tp, PD_W), lambda i: (i, 0, 0)),
                  _resident(lam_rows.shape), _resident((1, LANES))],
        out_specs=pl.BlockSpec((1, tp, 256), lambda i: (i, 0, 0)),
        out_shape=jax.ShapeDtypeStruct((b, tp, 256), BF16),
        scratch_shapes=[pltpu.VMEM((n_maps * ATT_ROWS, 256), BF16)] + _attn_scratch(n_maps, tp),
        compiler_params=_cparams(),
        name="diffattn",
    )(pd, lam_rows, dn)


def _pad_cols(x, width):
    return jnp.pad(x, ((0, 0), (0, width - x.shape[1])))


def _rot_split(w):
    d = w.shape[0]
    return w.reshape(d, N_HEADS, 2, 32).transpose(0, 2, 1, 3).reshape(d, 256)


def _layout_w_in(w):
    sizes = (128, 128, 256, 16, 256, 256, 256, 256, 256, 192, 64, 32, 256, 256, 256)
    offs = [0]
    for s_ in sizes:
        offs.append(offs[-1] + s_)
    seg = [w[:, offs[i]:offs[i + 1]] for i in range(len(sizes))]
    (a_q, a_k, a_v, a_lr, a_og, r_q, r_k, r_v, r_og, c_cq, c_ckv, c_kpe, d_q, d_k, d_v) = seg
    d = w.shape[0]
    z = lambda n: jnp.zeros((d, n), w.dtype)
    dv_p = jnp.pad(d_v.reshape(d, N_HEADS, 64), ((0, 0), (0, 0), (0, 64))).reshape(d, 512)
    cols = [a_q, a_k, a_v, a_og, a_lr, z(112),
            _rot_split(r_q), _rot_split(r_k), r_v, r_og,
            c_cq, c_ckv, z(64), c_kpe, z(32),
            d_q, d_k, dv_p]
    return jnp.concatenate(cols, axis=1).astype(BF16)


def _tables(tp):
    pos = jnp.arange(tp, dtype=F32) - FRONT_PAD
    inv = ROPE_THETA ** (-jnp.arange(32, dtype=F32) / 32)
    ang = pos[:, None] * inv[None, :]
    ret_cos = jnp.tile(jnp.cos(ang), (1, N_HEADS))
    ret_sin = jnp.tile(jnp.sin(ang), (1, N_HEADS))
    inv16 = ROPE_THETA ** (-jnp.arange(16, dtype=F32) / 16)
    ang16 = pos[:, None] * inv16[None, :]
    c16, s16 = jnp.cos(ang16), jnp.sin(ang16)
    one = lambda n: jnp.ones((tp, n), F32)
    zero = lambda n: jnp.zeros((tp, n), F32)
    mla_cos = jnp.concatenate([one(64), c16, c16, one(32)], axis=1)
    mla_sa = jnp.concatenate([zero(80), s16, zero(32)], axis=1)
    mla_sb = jnp.concatenate([zero(64), -s16, zero(48)], axis=1)
    lg = jnp.log(1.0 - jnp.exp2(-5.0 - jnp.arange(N_HEADS, dtype=F32)))
    idx = jnp.arange(RET_BLOCK, dtype=F32)
    rel = idx[:, None] - idx[None, :]
    dmask = jnp.where(rel[None] >= 0, jnp.exp(rel[None] * lg[:, None, None]), 0.0)
    dmask = dmask.reshape(N_HEADS * RET_BLOCK, RET_BLOCK)
    lane_head = (jnp.arange(256) % 128) // 32
    qfac = jnp.exp((idx[:, None] + 1.0) * lg[lane_head][None, :])
    kfac = jnp.exp((RET_BLOCK - 1.0 - idx[:, None]) * lg[lane_head][None, :])
    dec = jnp.exp(RET_BLOCK * lg[lane_head])[:, None]
    return ret_cos, ret_sin, mla_cos, mla_sa, mla_sb, dmask, qfac, kfac, dec


def _chunk_cols(w):
    d, f = w.shape
    return w.reshape(d, f // FF_CHUNK, FF_CHUNK).transpose(1, 0, 2).astype(BF16)


def _chunk_rows(w):
    f, d = w.shape
    return w.reshape(f // FF_CHUNK, FF_CHUNK, d).astype(BF16)


def kernel(x, meta_tokens, attn_norm, w_in, gla_w_gate, gla_b_gate, gla_norm, ret_norm, mla_q_norm, mla_w_uq, mla_kv_norm, mla_w_ukv, diff_lambda, diff_norm, w_out, ffn_norm, ffn_w_gate, ffn_w_up, ffn_w_down, moe_router, moe_w_gate, moe_w_up, moe_w_down, final_norm):
    b, seq, d = x.shape
    tp = FRONT_PAD + N_META + seq
    n = b * tp
    meta = jnp.broadcast_to(meta_tokens[None].astype(x.dtype), (b, N_META, d))
    h = jnp.concatenate([jnp.zeros((b, FRONT_PAD, d), x.dtype), meta, x], axis=1).reshape(n, d)
    ret_cos, ret_sin, mla_cos, mla_sa, mla_sb, dmask, qfac, kfac, dec = _tables(tp)

    for li in range(DEPTH):
        pa, pb, pc, pd = _inproj(h, attn_norm[li][None, :], _layout_w_in(w_in[li]))
        pa, pb, pc, pd = (p.reshape(b, tp, -1) for p in (pa, pb, pc, pd))

        wgate = jnp.pad(gla_w_gate[li], ((0, LANES - GLA_GATE_RANK), (0, 0)))
        o_a = _gla(pa, wgate, gla_b_gate[li][None, :], jnp.tile(gla_norm[li], N_HEADS)[None, :])
        o_b = _ret(pb, ret_cos, ret_sin, dmask, qfac, kfac, dec, jnp.tile(ret_norm[li], N_HEADS)[None, :])

        qn = _pad_cols(mla_q_norm[li][None, :], 256)
        kvn = jnp.pad(mla_kv_norm[li][None, :], ((0, 0), (MLA_Q_RANK, 0)))
        wuq = jnp.pad(mla_w_uq[li].reshape(MLA_Q_RANK, N_HEADS, MLA_NOPE + MLA_ROPE),
                      ((0, 256 - MLA_Q_RANK), (0, 0), (0, LANES - MLA_NOPE - MLA_ROPE)))
        wuq = wuq.reshape(256, N_HEADS * LANES).astype(BF16)
        wukv = mla_w_ukv[li].reshape(MLA_KV_RANK, N_HEADS, 2, 64)
        wukv = jnp.pad(wukv, ((MLA_Q_RANK, 0), (0, 0), (0, 0), (0, 64)))
        wukv = wukv.transpose(0, 2, 1, 3).reshape(256, 2 * N_HEADS * LANES).astype(BF16)
        o_c = _mla(pc, qn, kvn, wuq, wukv, mla_cos, mla_sa, mla_sb)

        lam_init = 0.8 - 0.6 * math.exp(-0.3 * li)
        o_d = _diff(pd, diff_lambda[li], _pad_cols(diff_norm[li][None, :], LANES), lam_init)

        o_a, o_b, o_c, o_d = (o.reshape(n, 256) for o in (o_a, o_b, o_c, o_d))
        wo = w_out[li].astype(BF16)
        fn = ffn_norm[li][None, :]
        j = li // 2
        if li % 2 == 0:
            h, hn = _outproj(h, o_a, o_b, o_c, o_d, wo, fn)
            h = _swiglu(h, hn, _chunk_cols(ffn_w_gate[j]), _chunk_cols(ffn_w_up[j]),
                        _chunk_rows(ffn_w_down[j]))
        else:
            h, hn, info, cnt = _outproj(h, o_a, o_b, o_c, o_d, wo, fn,
                                        router=_pad_cols(moe_router[j], LANES))
            h = _routed_moe(h, hn, info, cnt, jax.vmap(_chunk_cols)(moe_w_gate[j]),
                            jax.vmap(_chunk_cols)(moe_w_up[j]), jax.vmap(_chunk_rows)(moe_w_down[j]))

    return _final_norm(h.reshape(b, tp, d), final_norm[None, :])
```

```python
import functools
import math

import jax
import jax.numpy as jnp
from jax import lax
from jax.experimental import pallas as pl
from jax.experimental.pallas import tpu as pltpu

F32 = jnp.float32
BF16 = jnp.bfloat16

D_MODEL = 1024
DEPTH = 2
N_META = 16
CHUNK = 64
Q_BLOCK = 128
FRONT_PAD = Q_BLOCK - N_META
EPS = 1e-6
NEG = -1e30
ROPE_THETA = 10000.0
N_HEADS = 4
GLA_DK = 32
GLA_DV = 64
GLA_GATE_RANK = 16
GLA_TAU = 16.0
RET_DK = 64
MLA_Q_RANK = 192
MLA_KV_RANK = 64
MLA_NOPE = 64
MLA_ROPE = 32
DIFF_DK = 32
D_FF = 2816
N_EXPERTS = 8
D_FF_EXPERT = 3584

LANES = 128
ROW_TILE = 512
FF_CHUNK = 256
VMEM_LIMIT = 56 * 1024 * 1024

PA_W = 896
PB_W = 1024
PC_W = 384
PD_W = 1024


def _cparams(n_axes=1):
    return pltpu.CompilerParams(dimension_semantics=("arbitrary",) * n_axes,
                                vmem_limit_bytes=VMEM_LIMIT)


def _resident(shape):
    nd = len(shape)
    return pl.BlockSpec(shape, lambda *_: (0,) * nd, pipeline_mode=pl.Buffered(1))


def _sigmoid(x):
    return 1.0 / (1.0 + jnp.exp(-x))


def _split_bf16(x):
    hi = x.astype(BF16)
    lo = (x - hi.astype(F32)).astype(BF16)
    return hi, lo


def _dot(a, b):
    return jnp.dot(a, b, preferred_element_type=F32)


def _dot_nt(a, b):
    return lax.dot_general(a, b, (((1,), (1,)), ((), ())), preferred_element_type=F32)


def _inproj_kernel(h_ref, g_ref, w_ref, pa_ref, pb_ref, pc_ref, pd_ref):
    x = h_ref[...]
    ms = jnp.mean(x * x, axis=-1, keepdims=True)
    y = (x * lax.rsqrt(ms + EPS) * g_ref[...]).astype(BF16)
    off = 0
    for o_ref, width in ((pa_ref, PA_W), (pb_ref, PB_W), (pc_ref, PC_W), (pd_ref, PD_W)):
        o_ref[...] = _dot(y, w_ref[:, off:off + width]).astype(BF16)
        off += width


def _inproj(h, g, w):
    n = h.shape[0]
    wtot = PA_W + PB_W + PC_W + PD_W
    row = lambda width: pl.BlockSpec((ROW_TILE, width), lambda i: (i, 0))
    return pl.pallas_call(
        _inproj_kernel,
        grid=(n // ROW_TILE,),
        in_specs=[row(D_MODEL), _resident((1, D_MODEL)), _resident((D_MODEL, wtot))],
        out_specs=[row(PA_W), row(PB_W), row(PC_W), row(PD_W)],
        out_shape=[jax.ShapeDtypeStruct((n, w_), BF16) for w_ in (PA_W, PB_W, PC_W, PD_W)],
        compiler_params=_cparams(),
        name="inproj",
    )(h, g, w)


def _outproj_kernel(with_router, h_ref, oa_ref, ob_ref, oc_ref, od_ref, wo_ref, fn_ref, *rest):
    if with_router:
        router_ref, hmid_ref, hn_ref, info_ref, cnt_ref, carry_ref = rest
    else:
        hmid_ref, hn_ref = rest
    o = jnp.concatenate([oa_ref[...], ob_ref[...], oc_ref[...], od_ref[...]], axis=1)
    hm = h_ref[...] + _dot(o, wo_ref[...])
    hmid_ref[...] = hm
    ms = jnp.mean(hm * hm, axis=-1, keepdims=True)
    y = hm * lax.rsqrt(ms + EPS) * fn_ref[...]
    hn_ref[...] = y.astype(hn_ref.dtype)
    if with_router:
        y_hi, y_lo = _split_bf16(y)
        r_hi, r_lo = _split_bf16(router_ref[...])
        logits = _dot(y_hi, r_hi) + _dot(y_hi, r_lo) + _dot(y_lo, r_hi)
        lane = lax.broadcasted_iota(jnp.int32, logits.shape, 1).astype(F32)
        ninf = float("-inf")
        logits = jnp.where(lane < N_EXPERTS, logits, ninf)
        m1 = jnp.max(logits, axis=-1, keepdims=True)
        i1 = jnp.min(jnp.where(logits == m1, lane, float(LANES)), axis=-1, keepdims=True)
        rest_l = jnp.where(lane == i1, ninf, logits)
        m2 = jnp.max(rest_l, axis=-1, keepdims=True)
        i2 = jnp.min(jnp.where(rest_l == m2, lane, float(LANES)), axis=-1, keepdims=True)
        e2 = jnp.exp(m2 - m1)
        den = 1.0 + e2
        @pl.when(pl.program_id(0) == 0)
        def _():
            carry_ref[...] = jnp.zeros_like(carry_ref)
        sel = jnp.where(lane == i1, 1.0, 0.0) + jnp.where(lane == i2, 1.0, 0.0)
        rows = sel.shape[0]
        below = (lax.broadcasted_iota(jnp.int32, (rows, rows), 1)
                 < lax.broadcasted_iota(jnp.int32, (rows, rows), 0))
        count = _dot(jnp.where(below, 1.0, 0.0).astype(BF16), sel.astype(BF16)) + carry_ref[...]
        r1 = jnp.sum(jnp.where(lane == i1, count, 0.0), axis=-1, keepdims=True)
        r2 = jnp.sum(jnp.where(lane == i2, count, 0.0), axis=-1, keepdims=True)
        total = carry_ref[...] + jnp.sum(sel, axis=0, keepdims=True)
        carry_ref[...] = total
        cnt_ref[...] = jnp.broadcast_to(total, cnt_ref.shape)
        info = jnp.where(lane == 0, i1, 0.0)
        for k, val in enumerate((i2, 1.0 / den, e2 / den, r1, r2), start=1):
            info = jnp.where(lane == k, val, info)
        info_ref[...] = info


def _outproj(h, oa, ob, oc, od, wo, fn, router=None):
    n = h.shape[0]
    row = lambda width: pl.BlockSpec((ROW_TILE, width), lambda i: (i, 0))
    in_specs = [row(D_MODEL), row(256), row(256), row(256), row(256),
                _resident((D_MODEL, D_MODEL)), _resident((1, D_MODEL))]
    out_specs = [row(D_MODEL), row(D_MODEL)]
    out_shape = [jax.ShapeDtypeStruct((n, D_MODEL), F32),
                 jax.ShapeDtypeStruct((n, D_MODEL), BF16 if router is None else F32)]
    args = [h, oa, ob, oc, od, wo, fn]
    scratch = []
    if router is not None:
        in_specs.append(_resident((D_MODEL, LANES)))
        out_specs += [row(LANES), pl.BlockSpec((8, LANES), lambda i: (0, 0))]
        out_shape += [jax.ShapeDtypeStruct((n, LANES), F32), jax.ShapeDtypeStruct((8, LANES), F32)]
        args.append(router)
        scratch.append(pltpu.VMEM((1, LANES), F32))
    return pl.pallas_call(
        functools.partial(_outproj_kernel, router is not None),
        grid=(n // ROW_TILE,),
        in_specs=in_specs, out_specs=out_specs, out_shape=out_shape,
        scratch_shapes=scratch,
        compiler_params=_cparams(),
        name="outproj_router" if router is not None else "outproj",
    )(*args)


def _dispatch_kernel(pos_ref, x_ref, xs_in_ref, xs_ref, sem):
    del xs_in_ref
    n_rows = x_ref.shape[0]

    def row_copy(r, k):
        p = pos_ref[0, 0, 2 * r + k]
        return pltpu.make_async_copy(x_ref.at[pl.ds(r, 1), :], xs_ref.at[pl.ds(p, 1), :], sem)

    def start(r, carry):
        row_copy(r, 0).start()
        row_copy(r, 1).start()
        return carry

    def wait(r, carry):
        row_copy(r, 0).wait()
        row_copy(r, 1).wait()
        return carry

    lax.fori_loop(0, n_rows, start, 0, unroll=8)
    lax.fori_loop(0, n_rows, wait, 0, unroll=8)


def _dispatch(pos, x, n_sorted):
    n, d = x.shape
    xs0 = jnp.zeros((n_sorted, d), x.dtype)
    return pl.pallas_call(
        _dispatch_kernel,
        grid=(n // ROW_TILE,),
        in_specs=[pl.BlockSpec((1, 1, 2 * ROW_TILE), lambda i: (i, 0, 0), memory_space=pltpu.SMEM),
                  pl.BlockSpec((ROW_TILE, d), lambda i: (i, 0)),
                  pl.BlockSpec(memory_space=pl.ANY)],
        out_specs=pl.BlockSpec(memory_space=pl.ANY),
        out_shape=jax.ShapeDtypeStruct((n_sorted, d), x.dtype),
        scratch_shapes=[pltpu.SemaphoreType.DMA(())],
        input_output_aliases={2: 0},
        compiler_params=_cparams(),
        name="moe_dispatch",
    )(pos, x, xs0)


def _experts_kernel(n_chunks, te_ref, tv_ref, xs_ref, wg_ref, wu_ref, wd_ref, ys_ref, acc_ref):
    i = pl.program_id(0)

    @pl.when(tv_ref[i] > 0)
    def _():
        x = xs_ref[...].astype(BF16)
        acc_ref[...] = jnp.zeros_like(acc_ref)

        def body(c, carry):
            g = _dot(x, wg_ref[0, c])
            u = _dot(x, wu_ref[0, c])
            a = (g * _sigmoid(g) * u).astype(BF16)
            acc_ref[...] += _dot(a, wd_ref[0, c])
            return carry

        lax.fori_loop(0, n_chunks, body, 0)
        ys_ref[...] = acc_ref[...]

    @pl.when(tv_ref[i] == 0)
    def _():
        ys_ref[...] = jnp.zeros_like(ys_ref)


def _experts(tile_expert, tile_valid, xs, wg, wu, wd):
    n_sorted, d = xs.shape
    n_chunks = wg.shape[1]
    wspec = lambda shape: pl.BlockSpec((1,) + shape[1:], lambda i, te, tv: (te[i], 0, 0, 0),
                                       pipeline_mode=pl.Buffered(1))
    row = pl.BlockSpec((ROW_TILE, d), lambda i, te, tv: (i, 0))
    return pl.pallas_call(
        functools.partial(_experts_kernel, n_chunks),
        grid_spec=pltpu.PrefetchScalarGridSpec(
            num_scalar_prefetch=2,
            grid=(n_sorted // ROW_TILE,),
            in_specs=[row, wspec(wg.shape), wspec(wu.shape), wspec(wd.shape)],
            out_specs=row,
            scratch_shapes=[pltpu.VMEM((ROW_TILE, d), F32)]),
        out_shape=jax.ShapeDtypeStruct((n_sorted, d), F32),
        compiler_params=_cparams(),
        name="moe_experts",
    )(tile_expert, tile_valid, xs, wg, wu, wd)


def _combine_kernel(pos_ref, h_ref, info_ref, ys_ref, o_ref, y_buf, sem):
    n_rows = h_ref.shape[0]

    def row_copy(r, k):
        p = pos_ref[0, 0, 2 * r + k]
        return pltpu.make_async_copy(ys_ref.at[pl.ds(p, 1), :], y_buf.at[k, pl.ds(r, 1), :], sem)

    def start(r, carry):
        row_copy(r, 0).start()
        row_copy(r, 1).start()
        return carry

    def wait(r, carry):
        row_copy(r, 0).wait()
        row_copy(r, 1).wait()
        return carry

    lax.fori_loop(0, n_rows, start, 0, unroll=8)
    lax.fori_loop(0, n_rows, wait, 0, unroll=8)
    info = info_ref[...]
    o_ref[...] = h_ref[...] + info[:, 2:3] * y_buf[0] + info[:, 3:4] * y_buf[1]


def _combine(pos, h, info, ys):
    n, d = h.shape
    return pl.pallas_call(
        _combine_kernel,
        grid=(n // ROW_TILE,),
        in_specs=[pl.BlockSpec((1, 1, 2 * ROW_TILE), lambda i: (i, 0, 0), memory_space=pltpu.SMEM),
                  pl.BlockSpec((ROW_TILE, d), lambda i: (i, 0)),
                  pl.BlockSpec((ROW_TILE, LANES), lambda i: (i, 0)),
                  pl.BlockSpec(memory_space=pl.ANY)],
        out_specs=pl.BlockSpec((ROW_TILE, d), lambda i: (i, 0)),
        out_shape=jax.ShapeDtypeStruct((n, d), F32),
        scratch_shapes=[pltpu.VMEM((2, ROW_TILE, d), F32), pltpu.SemaphoreType.DMA(())],
        input_output_aliases={1: 0},
        compiler_params=_cparams(),
        name="moe_combine",
    )(pos, h, info, ys)


def _routed_moe(h, hn, info, cnt, wg, wu, wd):
    n = h.shape[0]
    n_sorted = 2 * n + N_EXPERTS * ROW_TILE
    n_tiles = n_sorted // ROW_TILE
    counts = cnt[0, :N_EXPERTS].astype(jnp.int32)
    tiles_e = (counts + ROW_TILE - 1) // ROW_TILE
    tile_end = jnp.cumsum(tiles_e)
    row_base = (tile_end - tiles_e) * ROW_TILE
    e_idx = info[:, 0:2].astype(jnp.int32)
    pos = row_base[e_idx] + info[:, 4:6].astype(jnp.int32)
    pos = pos.reshape(n // ROW_TILE, 1, 2 * ROW_TILE)
    tile_ids = jnp.arange(n_tiles, dtype=jnp.int32)
    tile_expert = jnp.minimum(jnp.searchsorted(tile_end, tile_ids, side="right"),
                              N_EXPERTS - 1).astype(jnp.int32)
    tile_valid = (tile_ids < tile_end[-1]).astype(jnp.int32)
    xs = _dispatch(pos, hn, n_sorted)
    ys = _experts(tile_expert, tile_valid, xs, wg, wu, wd)
    return _combine(pos, h, info, ys)


def _swiglu_kernel(expert, n_chunks, h_ref, hn_ref, *rest):
    if expert is None:
        wg_ref, wu_ref, wd_ref, o_ref, acc_ref = rest
    else:
        gate_ref, wg_ref, wu_ref, wd_ref, o_ref, acc_ref = rest
    hn = hn_ref[...]
    acc_ref[...] = jnp.zeros_like(acc_ref)

    def body(c, carry):
        g = _dot(hn, wg_ref[c])
        u = _dot(hn, wu_ref[c])
        a = (g * _sigmoid(g) * u).astype(BF16)
        acc_ref[...] += _dot(a, wd_ref[c])
        return carry

    lax.fori_loop(0, n_chunks, body, 0)
    y = acc_ref[...]
    if expert is not None:
        y = gate_ref[:, expert:expert + 1] * y
    o_ref[...] = h_ref[...] + y


def _swiglu(h, hn, wg, wu, wd, gates=None, expert=None):
    n = h.shape[0]
    n_chunks = wg.shape[0]
    row = lambda width: pl.BlockSpec((ROW_TILE, width), lambda i: (i, 0))
    in_specs = [row(D_MODEL), row(D_MODEL)]
    args = [h, hn]
    if expert is not None:
        in_specs.append(row(LANES))
        args.append(gates)
    in_specs += [_resident(wg.shape), _resident(wu.shape), _resident(wd.shape)]
    args += [wg, wu, wd]
    return pl.pallas_call(
        functools.partial(_swiglu_kernel, expert, n_chunks),
        grid=(n // ROW_TILE,),
        in_specs=in_specs,
        out_specs=row(D_MODEL),
        out_shape=jax.ShapeDtypeStruct((n, D_MODEL), F32),
        scratch_shapes=[pltpu.VMEM((ROW_TILE, D_MODEL), F32)],
        input_output_aliases={0: 0},
        compiler_params=_cparams(),
        name="swiglu" if expert is None else f"moe_expert{expert}",
    )(*args)


def _final_norm_kernel(h_ref, g_ref, o_ref):
    x = h_ref[0]
    ms = jnp.mean(x * x, axis=-1, keepdims=True)
    o_ref[0] = x * lax.rsqrt(ms + EPS) * g_ref[...]


def _final_norm(h3, g):
    b, tp, d = h3.shape
    nb = tp // Q_BLOCK - 1
    return pl.pallas_call(
        _final_norm_kernel,
        grid=(b, nb),
        in_specs=[pl.BlockSpec((1, Q_BLOCK, d), lambda i, j: (i, j + 1, 0)), _resident((1, d))],
        out_specs=pl.BlockSpec((1, Q_BLOCK, d), lambda i, j: (i, j, 0)),
        out_shape=jax.ShapeDtypeStruct((b, nb * Q_BLOCK, d), F32),
        compiler_params=_cparams(2),
        name="final_norm",
    )(h3, g)


def _group_ones(n, group_shift):
    r = lax.broadcasted_iota(jnp.int32, (n, n), 0) >> group_shift
    c = lax.broadcasted_iota(jnp.int32, (n, n), 1) >> group_shift
    return jnp.where(r == c, 1.0, 0.0).astype(BF16)


GLA_ROWS = 256


def _gla_kernel(tp, pa_ref, wg_ref, bg_ref, gn_ref, o_ref, s_ref):
    qhead = lax.broadcasted_iota(jnp.int32, (1, N_HEADS * GLA_DK), 1) >> 5
    ehead = lax.broadcasted_iota(jnp.int32, (1, N_HEADS * GLA_DV), 1) >> 6
    shead = lax.broadcasted_iota(jnp.int32, (N_HEADS * GLA_DK, 1), 0) >> 5
    bd = shead == ehead
    gsum = _group_ones(N_HEADS * GLA_DV, 6)
    wg_hi, wg_lo = _split_bf16(wg_ref[...])
    bg = bg_ref[...]
    gn = gn_ref[...]
    scale = GLA_DK ** -0.5
    s_ref[...] = jnp.zeros_like(s_ref)

    def block(r0, n_rows):
        n_c = n_rows // CHUNK
        rows = pl.ds(r0, n_rows)
        ri = lax.broadcasted_iota(jnp.int32, (n_rows, n_rows), 0)
        ci = lax.broadcasted_iota(jnp.int32, (n_rows, n_rows), 1)
        same = (ri >> 6) == (ci >> 6)
        tri_bf = jnp.where(jnp.logical_and(same, ri >= ci), 1.0, 0.0).astype(BF16)
        ones_bf = jnp.where(same, 1.0, 0.0).astype(BF16)
        r4 = lax.broadcasted_iota(jnp.int32, (N_HEADS * n_rows, n_rows), 0) & (n_rows - 1)
        c4 = lax.broadcasted_iota(jnp.int32, (N_HEADS * n_rows, n_rows), 1)
        tri4 = jnp.logical_and((r4 >> 6) == (c4 >> 6), r4 >= c4)
        col_chunk = lax.broadcasted_iota(jnp.int32, (1, n_rows), 1) >> 6

        q = pa_ref[0, rows, 0:128].astype(F32) * scale
        k = pa_ref[0, rows, 128:256].astype(F32)
        v = pa_ref[0, rows, 256:512]
        og = pa_ref[0, rows, 512:768].astype(F32)
        lr = pa_ref[0, rows, 768:896]
        valid = (r0 + lax.broadcasted_iota(jnp.int32, (n_rows, 1), 0)) >= FRONT_PAD

        pre = _dot(lr, wg_hi) + _dot(lr, wg_lo) + bg
        logsig = jnp.minimum(pre, 0.0) - jnp.log1p(jnp.exp(-jnp.abs(pre)))
        g = jnp.where(valid, logsig * (1.0 / GLA_TAU), 0.0)
        g_hi, g_lo = _split_bf16(g)
        cum = _dot(tri_bf, g_hi) + _dot(tri_bf, g_lo)
        cum_end = _dot(ones_bf, g_hi) + _dot(ones_bf, g_lo)
        qt_bf = (q * jnp.exp(cum)).astype(BF16)
        kt = (k * jnp.exp(-cum)).astype(BF16)
        kd_t = (k * jnp.exp(cum_end - cum)).T.astype(BF16)
        dec_t = jnp.exp(cum_end.T)

        qs = jnp.concatenate([jnp.where(qhead == h, qt_bf, jnp.zeros_like(qt_bf))
                              for h in range(N_HEADS)], axis=0)
        a = jnp.where(tri4, _dot_nt(qs, kt), 0.0).astype(BF16)
        r = _dot(a, v)
        o = jnp.where(ehead == 0, r[0:n_rows, :], 0.0)
        for h in range(1, N_HEADS):
            o = o + jnp.where(ehead == h, r[h * n_rows:(h + 1) * n_rows, :], 0.0)

        s = s_ref[...]
        inter = []
        for c in range(n_c):
            inter.append(_dot(qt_bf[c * CHUNK:(c + 1) * CHUNK, :], s.astype(BF16)))
            upd = _dot(jnp.where(col_chunk == c, kd_t, jnp.zeros_like(kd_t)), v)
            s = dec_t[:, c * CHUNK:c * CHUNK + 1] * s + jnp.where(bd, upd, 0.0)
        s_ref[...] = s
        o = o + jnp.concatenate(inter, axis=0)

        ms = _dot((o * o).astype(BF16), gsum) * (1.0 / GLA_DV)
        y = o * lax.rsqrt(ms + EPS) * gn * (og * _sigmoid(og))
        o_ref[0, rows, :] = jnp.where(valid, y, 0.0).astype(BF16)

    block(0, Q_BLOCK)

    def body(i, carry):
        block(pl.multiple_of(Q_BLOCK + i * GLA_ROWS, Q_BLOCK), GLA_ROWS)
        return carry

    lax.fori_loop(0, (tp - Q_BLOCK) // GLA_ROWS, body, 0)


def _gla(pa, wg, bg, gn):
    b, tp, _ = pa.shape
    return pl.pallas_call(
        functools.partial(_gla_kernel, tp),
        grid=(b,),
        in_specs=[pl.BlockSpec((1, tp, PA_W), lambda i: (i, 0, 0)),
                  _resident((LANES, LANES)), _resident((1, LANES)), _resident((1, 256))],
        out_specs=pl.BlockSpec((1, tp, 256), lambda i: (i, 0, 0)),
        out_shape=jax.ShapeDtypeStruct((b, tp, 256), BF16),
        scratch_shapes=[pltpu.VMEM((N_HEADS * GLA_DK, N_HEADS * GLA_DV), F32)],
        compiler_params=_cparams(),
        name="gla",
    )(pa, wg, bg, gn)


RET_BLOCK = 128


def _ret_kernel(tp, pb_ref, cos_ref, sin_ref, dmask_ref, qfac_ref, kfac_ref, dec_ref, gn_ref,
                o_ref, s_ref):
    blk = RET_BLOCK
    n_blocks = tp // blk
    qhead = (lax.broadcasted_iota(jnp.int32, (1, 256), 1) & 127) >> 5
    ehead = lax.broadcasted_iota(jnp.int32, (1, 256), 1) >> 6
    shead = (lax.broadcasted_iota(jnp.int32, (256, 1), 0) & 127) >> 5
    bd = shead == ehead
    gsum = _group_ones(256, 6)
    gn = gn_ref[...]
    s_ref[...] = jnp.zeros_like(s_ref)

    def rope(x, cos, sin):
        x1, x2 = x[:, :128], x[:, 128:]
        return jnp.concatenate([x1 * cos - x2 * sin, x1 * sin + x2 * cos], axis=1)

    def block(j, carry):
        r0 = pl.multiple_of(j * blk, blk)
        rows = pl.ds(r0, blk)
        cos = cos_ref[rows, :]
        sin = sin_ref[rows, :]
        q = rope(pb_ref[0, rows, 0:256].astype(F32), cos, sin)
        k = rope(pb_ref[0, rows, 256:512].astype(F32), cos, sin) * (RET_DK ** -0.5)
        v = pb_ref[0, rows, 512:768]
        og = pb_ref[0, rows, 768:1024].astype(F32)
        valid = (r0 + lax.broadcasted_iota(jnp.int32, (blk, 1), 0)) >= FRONT_PAD

        q_bf = q.astype(BF16)
        qs = jnp.concatenate([jnp.where(qhead == h, q_bf, jnp.zeros_like(q_bf))
                              for h in range(N_HEADS)], axis=0)
        a = (_dot_nt(qs, k.astype(BF16)) * dmask_ref[...]).astype(BF16)
        r = _dot(a, v)
        o = _dot((q * qfac_ref[...]).astype(BF16), s_ref[...].astype(BF16))
        for h in range(N_HEADS):
            o = o + jnp.where(ehead == h, r[h * blk:(h + 1) * blk, :], 0.0)

        kd_t = (k * kfac_ref[...]).T.astype(BF16)
        upd = _dot(kd_t, v)
        s_ref[...] = dec_ref[...] * s_ref[...] + jnp.where(bd, upd, 0.0)

        mu = _dot(o.astype(BF16), gsum) * (1.0 / 64)
        xc = o - mu
        var = _dot((xc * xc).astype(BF16), gsum) * (1.0 / 64)
        y = xc * lax.rsqrt(var + EPS) * gn * (og * _sigmoid(og))
        o_ref[0, rows, :] = jnp.where(valid, y, 0.0).astype(BF16)
        return carry

    lax.fori_loop(0, n_blocks, block, 0, unroll=2)


def _ret(pb, cos, sin, dmask, qfac, kfac, dec, gn):
    b, tp, _ = pb.shape
    return pl.pallas_call(
        functools.partial(_ret_kernel, tp),
        grid=(b,),
        in_specs=[pl.BlockSpec((1, tp, PB_W), lambda i: (i, 0, 0)),
                  _resident(cos.shape), _resident(sin.shape), _resident(dmask.shape),
                  _resident(qfac.shape), _resident(kfac.shape), _resident(dec.shape),
                  _resident((1, 256))],
        out_specs=pl.BlockSpec((1, tp, 256), lambda i: (i, 0, 0)),
        out_shape=jax.ShapeDtypeStruct((b, tp, 256), BF16),
        scratch_shapes=[pltpu.VMEM((256, 256), F32)],
        compiler_params=_cparams(),
        name="retention",
    )(pb, cos, sin, dmask, qfac, kfac, dec, gn)


ATT_ROWS = 256
LOG2E = 1.4426950408889634


def _mask_groups(mask, s, n_g):
    rows = s.shape[0] // n_g
    return jnp.concatenate([jnp.where(mask, s[g * rows:(g + 1) * rows, :], NEG)
                            for g in range(n_g)], axis=0)


def _attn_block0(qk_fn, v_fn, n_g):
    rows = pl.ds(0, Q_BLOCK)
    qrow = lax.broadcasted_iota(jnp.int32, (Q_BLOCK, 1), 0)
    kcol = lax.broadcasted_iota(jnp.int32, (1, Q_BLOCK), 1)
    mask = jnp.logical_and(kcol <= qrow, kcol >= FRONT_PAD)
    s = _mask_groups(mask, qk_fn(0, Q_BLOCK, rows), n_g)
    p = jnp.exp2(s - jnp.max(s, axis=-1, keepdims=True))
    l = jnp.sum(p, axis=-1, keepdims=True)
    p_bf = p.astype(BF16)
    pv = jnp.concatenate([_dot(p_bf[g * Q_BLOCK:(g + 1) * Q_BLOCK, :], v_fn(g, rows))
                          for g in range(n_g)], axis=0)
    return pv / l


def _attn_block(i, qk_fn, v_fn, n_g, s_meta, s_s, mx_s, l_s, acc_s):
    rr = ATT_ROWS
    q0 = pl.multiple_of(Q_BLOCK + (i - 1) * rr, Q_BLOCK)
    meta_rows = pl.ds(0, Q_BLOCK)
    kcol = lax.broadcasted_iota(jnp.int32, (1, Q_BLOCK), 1)

    def key_rows(j):
        return pl.ds(pl.multiple_of(Q_BLOCK + j * rr, Q_BLOCK), rr)

    s = jnp.where(kcol >= FRONT_PAD, qk_fn(q0, rr, meta_rows), NEG)
    s_meta[...] = s
    mx_s[...] = s

    def pass1(j, carry):
        sj = qk_fn(q0, rr, key_rows(j))
        s_s[j] = sj
        mx_s[...] = jnp.maximum(mx_s[...], jnp.maximum(sj[:, :LANES], sj[:, LANES:]))
        return carry

    lax.fori_loop(0, i - 1, pass1, 0)
    causal = (lax.broadcasted_iota(jnp.int32, (rr, rr), 1)
              <= lax.broadcasted_iota(jnp.int32, (rr, rr), 0))
    sd = _mask_groups(causal, qk_fn(q0, rr, pl.ds(q0, rr)), n_g)
    s_s[i - 1] = sd
    m = jnp.max(jnp.maximum(mx_s[...], jnp.maximum(sd[:, :LANES], sd[:, LANES:])),
                axis=-1, keepdims=True)
    mx_s[...] = jnp.broadcast_to(m, mx_s.shape)

    p = jnp.exp2(s_meta[...] - mx_s[...])
    l_s[...] = p
    p_bf = p.astype(BF16)
    for g in range(n_g):
        acc_s[g * rr:(g + 1) * rr, :] = _dot(p_bf[g * rr:(g + 1) * rr, :], v_fn(g, meta_rows))

    def pass2(j, carry):
        sj = s_s[j]
        mrep = mx_s[...]
        p0 = jnp.exp2(sj[:, :LANES] - mrep)
        p1 = jnp.exp2(sj[:, LANES:] - mrep)
        l_s[...] += p0 + p1
        pj = jnp.concatenate([p0.astype(BF16), p1.astype(BF16)], axis=1)
        for g in range(n_g):
            acc_s[g * rr:(g + 1) * rr, :] += _dot(pj[g * rr:(g + 1) * rr, :], v_fn(g, key_rows(j)))
        return carry

    lax.fori_loop(0, i, pass2, 0)
    return acc_s[...] / jnp.sum(l_s[...], axis=-1, keepdims=True)


def _heads_to_lanes(per_head):
    lo = per_head[0] + pltpu.roll(per_head[1], 64, 1)
    hi = per_head[2] + pltpu.roll(per_head[3], 64, 1)
    return jnp.concatenate([lo, hi], axis=1)


def _mla_kernel(tp, pc_ref, qn_ref, kvn_ref, wuq_ref, wukv_ref, cos_ref, sa_ref, sb_ref,
                o_ref, q_s, k_s, v_s, s_meta, s_s, mx_s, l_s, acc_s):
    n_blocks = tp // Q_BLOCK
    scale = (MLA_NOPE + MLA_ROPE) ** -0.5 * LOG2E
    is_q = lax.broadcasted_iota(jnp.int32, (1, 256), 1) < MLA_Q_RANK

    def prep(i, carry):
        r0 = pl.multiple_of(i * Q_BLOCK, Q_BLOCK)
        rows = pl.ds(r0, Q_BLOCK)
        x = pc_ref[0, rows, 0:256].astype(F32)
        x2 = x * x
        ms_q = jnp.sum(jnp.where(is_q, x2, 0.0), axis=-1, keepdims=True) * (1.0 / MLA_Q_RANK)
        ms_kv = jnp.sum(jnp.where(is_q, 0.0, x2), axis=-1, keepdims=True) * (1.0 / MLA_KV_RANK)
        yq = (x * lax.rsqrt(ms_q + EPS) * qn_ref[...]).astype(BF16)
        ykv = (x * lax.rsqrt(ms_kv + EPS) * kvn_ref[...]).astype(BF16)
        cq = _dot(yq, wuq_ref[...])
        kv = _dot(ykv, wukv_ref[...])
        cos = cos_ref[rows, :]
        sa = sa_ref[rows, :]
        sb = sb_ref[rows, :]

        def rope(t):
            return t * cos + pltpu.roll(t, 16, 1) * sa + pltpu.roll(t, LANES - 16, 1) * sb

        kpe = rope(pc_ref[0, rows, 256:384].astype(F32))
        for h in range(N_HEADS):
            q_s[h, rows, :] = (rope(cq[:, h * LANES:(h + 1) * LANES]) * scale).astype(BF16)
            k_s[h, rows, :] = (kv[:, h * LANES:(h + 1) * LANES] + kpe).astype(BF16)
            v_s[h, rows, :] = kv[:, (N_HEADS + h) * LANES:(N_HEADS + h + 1) * LANES].astype(BF16)
        return carry

    lax.fori_loop(0, n_blocks, prep, 0)

    def qk_fn(q0, n_rows, krows):
        return jnp.concatenate([_dot_nt(q_s[h, pl.ds(q0, n_rows), :], k_s[h, krows, :])
                                for h in range(N_HEADS)], axis=0)

    def v_fn(h, krows):
        return v_s[h, krows, :]

    def emit(q0, n_rows, o, first):
        y = _heads_to_lanes([o[h * n_rows:(h + 1) * n_rows, :] for h in range(N_HEADS)])
        if first:
            qrow = lax.broadcasted_iota(jnp.int32, (n_rows, 1), 0)
            y = jnp.where(qrow >= FRONT_PAD, y, 0.0)
        o_ref[0, pl.ds(q0, n_rows), :] = y.astype(BF16)

    emit(0, Q_BLOCK, _attn_block0(qk_fn, v_fn, N_HEADS), True)

    def qblock(i, carry):
        o = _attn_block(i, qk_fn, v_fn, N_HEADS, s_meta, s_s, mx_s, l_s, acc_s)
        emit(pl.multiple_of(Q_BLOCK + (i - 1) * ATT_ROWS, Q_BLOCK), ATT_ROWS, o, False)
        return carry

    lax.fori_loop(1, (tp - Q_BLOCK) // ATT_ROWS + 1, qblock, 0)


def _attn_scratch(n_g, tp):
    g_rows = n_g * ATT_ROWS
    n_slots = (tp - Q_BLOCK) // ATT_ROWS
    return [pltpu.VMEM((g_rows, LANES), F32),
            pltpu.VMEM((n_slots, g_rows, ATT_ROWS), F32),
            pltpu.VMEM((g_rows, LANES), F32),
            pltpu.VMEM((g_rows, LANES), F32),
            pltpu.VMEM((g_rows, LANES), F32)]


def _mla(pc, qn, kvn, wuq, wukv, cos, sa, sb):
    b, tp, _ = pc.shape
    return pl.pallas_call(
        functools.partial(_mla_kernel, tp),
        grid=(b,),
        in_specs=[pl.BlockSpec((1, tp, PC_W), lambda i: (i, 0, 0)),
                  _resident((1, 256)), _resident((1, 256)),
                  _resident(wuq.shape), _resident(wukv.shape),
                  _resident(cos.shape), _resident(sa.shape), _resident(sb.shape)],
        out_specs=pl.BlockSpec((1, tp, 256), lambda i: (i, 0, 0)),
        out_shape=jax.ShapeDtypeStruct((b, tp, 256), BF16),
        scratch_shapes=[pltpu.VMEM((N_HEADS, tp, LANES), BF16),
                        pltpu.VMEM((N_HEADS, tp, LANES), BF16),
                        pltpu.VMEM((N_HEADS, tp, LANES), BF16)] + _attn_scratch(N_HEADS, tp),
        compiler_params=_cparams(),
        name="mla",
    )(pc, qn, kvn, wuq, wukv, cos, sa, sb)


def _diff_kernel(tp, lam_init, pd_ref, lam_ref, dn_ref, o_ref, qs_s, s_meta, s_s, mx_s, l_s, acc_s):
    n_maps = 2 * N_HEADS
    scale = DIFF_DK ** -0.5 * LOG2E
    group = lax.broadcasted_iota(jnp.int32, (1, 256), 1) >> 5
    lv = lam_ref[...]
    lam = (jnp.exp(jnp.sum(lv[0:1, :] * lv[1:2, :], axis=-1, keepdims=True))
           - jnp.exp(jnp.sum(lv[2:3, :] * lv[3:4, :], axis=-1, keepdims=True)) + lam_init)
    dn = dn_ref[...]

    def stack_queries(q0, n_rows):
        q = (pd_ref[0, pl.ds(q0, n_rows), 0:256].astype(F32) * scale).astype(BF16)
        for g in range(n_maps):
            qs_s[g * n_rows:(g + 1) * n_rows, :] = jnp.where(group == g, q, jnp.zeros_like(q))

    def qk_fn(q0, n_rows, krows):
        return _dot_nt(qs_s[0:n_maps * n_rows, :], pd_ref[0, krows, 256:512])

    def v_fn(g, krows):
        return pd_ref[0, krows, 512 + (g // 2) * LANES:512 + (g // 2 + 1) * LANES]

    def emit(q0, n_rows, o, first):
        per_head = []
        for h in range(N_HEADS):
            o1 = o[(2 * h) * n_rows:(2 * h + 1) * n_rows, :]
            o2 = o[(2 * h + 1) * n_rows:(2 * h + 2) * n_rows, :]
            od = o1 - lam * o2
            ms = jnp.sum(od * od, axis=-1, keepdims=True) * (1.0 / 64)
            per_head.append(od * lax.rsqrt(ms + EPS) * dn * (1.0 - lam_init))
        y = _heads_to_lanes(per_head)
        if first:
            qrow = lax.broadcasted_iota(jnp.int32, (n_rows, 1), 0)
            y = jnp.where(qrow >= FRONT_PAD, y, 0.0)
        o_ref[0, pl.ds(q0, n_rows), :] = y.astype(BF16)

    stack_queries(0, Q_BLOCK)
    emit(0, Q_BLOCK, _attn_block0(qk_fn, v_fn, n_maps), True)

    def qblock(i, carry):
        q0 = pl.multiple_of(Q_BLOCK + (i - 1) * ATT_ROWS, Q_BLOCK)
        stack_queries(q0, ATT_ROWS)
        o = _attn_block(i, qk_fn, v_fn, n_maps, s_meta, s_s, mx_s, l_s, acc_s)
        emit(q0, ATT_ROWS, o, False)
        return carry

    lax.fori_loop(1, (tp - Q_BLOCK) // ATT_ROWS + 1, qblock, 0)


def _diff(pd, lam_rows, dn, lam_init):
    b, tp, _ = pd.shape
    n_maps = 2 * N_HEADS
    return pl.pallas_call(
        functools.partial(_diff_kernel, tp, lam_init),
        grid=(b,),
        in_specs=[pl.BlockSpec((1, tp, PD_W), lambda i: (i, 0, 0)),
                  _resident(lam_rows.shape), _resident((1, LANES))],
        out_specs=pl.BlockSpec((1, tp, 256), lambda i: (i, 0, 0)),
        out_shape=jax.ShapeDtypeStruct((b, tp, 256), BF16),
        scratch_shapes=[pltpu.VMEM((n_maps * ATT_ROWS, 256), BF16)] + _attn_scratch(n_maps, tp),
        compiler_params=_cparams(),
        name="diffattn",
    )(pd, lam_rows, dn)


def _pad_cols(x, width):
    return jnp.pad(x, ((0, 0), (0, width - x.shape[1])))


def _rot_split(w):
    d = w.shape[0]
    return w.reshape(d, N_HEADS, 2, 32).transpose(0, 2, 1, 3).reshape(d, 256)


def _layout_w_in(w):
    sizes = (128, 128, 256, 16, 256, 256, 256, 256, 256, 192, 64, 32, 256, 256, 256)
    offs = [0]
    for s_ in sizes:
        offs.append(offs[-1] + s_)
    seg = [w[:, offs[i]:offs[i + 1]] for i in range(len(sizes))]
    (a_q, a_k, a_v, a_lr, a_og, r_q, r_k, r_v, r_og, c_cq, c_ckv, c_kpe, d_q, d_k, d_v) = seg
    d = w.shape[0]
    z = lambda n: jnp.zeros((d, n), w.dtype)
    dv_p = jnp.pad(d_v.reshape(d, N_HEADS, 64), ((0, 0), (0, 0), (0, 64))).reshape(d, 512)
    cols = [a_q, a_k, a_v, a_og, a_lr, z(112),
            _rot_split(r_q), _rot_split(r_k), r_v, r_og,
            c_cq, c_ckv, z(64), c_kpe, z(32),
            d_q, d_k, dv_p]
    return jnp.concatenate(cols, axis=1).astype(BF16)


def _tables(tp):
    pos = jnp.arange(tp, dtype=F32) - FRONT_PAD
    inv = ROPE_THETA ** (-jnp.arange(32, dtype=F32) / 32)
    ang = pos[:, None] * inv[None, :]
    ret_cos = jnp.tile(jnp.cos(ang), (1, N_HEADS))
    ret_sin = jnp.tile(jnp.sin(ang), (1, N_HEADS))
    inv16 = ROPE_THETA ** (-jnp.arange(16, dtype=F32) / 16)
    ang16 = pos[:, None] * inv16[None, :]
    c16, s16 = jnp.cos(ang16), jnp.sin(ang16)
    one = lambda n: jnp.ones((tp, n), F32)
    zero = lambda n: jnp.zeros((tp, n), F32)
    mla_cos = jnp.concatenate([one(64), c16, c16, one(32)], axis=1)
    mla_sa = jnp.concatenate([zero(80), s16, zero(32)], axis=1)
    mla_sb = jnp.concatenate([zero(64), -s16, zero(48)], axis=1)
    lg = jnp.log(1.0 - jnp.exp2(-5.0 - jnp.arange(N_HEADS, dtype=F32)))
    idx = jnp.arange(RET_BLOCK, dtype=F32)
    rel = idx[:, None] - idx[None, :]
    dmask = jnp.where(rel[None] >= 0, jnp.exp(rel[None] * lg[:, None, None]), 0.0)
    dmask = dmask.reshape(N_HEADS * RET_BLOCK, RET_BLOCK)
    lane_head = (jnp.arange(256) % 128) // 32
    qfac = jnp.exp((idx[:, None] + 1.0) * lg[lane_head][None, :])
    kfac = jnp.exp((RET_BLOCK - 1.0 - idx[:, None]) * lg[lane_head][None, :])
    dec = jnp.exp(RET_BLOCK * lg[lane_head])[:, None]
    return ret_cos, ret_sin, mla_cos, mla_sa, mla_sb, dmask, qfac, kfac, dec


def _chunk_cols(w):
    d, f = w.shape
    return w.reshape(d, f // FF_CHUNK, FF_CHUNK).transpose(1, 0, 2).astype(BF16)


def _chunk_rows(w):
    f, d = w.shape
    return w.reshape(f // FF_CHUNK, FF_CHUNK, d).astype(BF16)


def kernel(x, meta_tokens, attn_norm, w_in, gla_w_gate, gla_b_gate, gla_norm, ret_norm, mla_q_norm, mla_w_uq, mla_kv_norm, mla_w_ukv, diff_lambda, diff_norm, w_out, ffn_norm, ffn_w_gate, ffn_w_up, ffn_w_down, moe_router, moe_w_gate, moe_w_up, moe_w_down, final_norm):
    b, seq, d = x.shape
    tp = FRONT_PAD + N_META + seq
    n = b * tp
    meta = jnp.broadcast_to(meta_tokens[None].astype(x.dtype), (b, N_META, d))
    h = jnp.concatenate([jnp.zeros((b, FRONT_PAD, d), x.dtype), meta, x], axis=1).reshape(n, d)
    ret_cos, ret_sin, mla_cos, mla_sa, mla_sb, dmask, qfac, kfac, dec = _tables(tp)

    for li in range(DEPTH):
        pa, pb, pc, pd = _inproj(h, attn_norm[li][None, :], _layout_w_in(w_in[li]))
        pa, pb, pc, pd = (p.reshape(b, tp, -1) for p in (pa, pb, pc, pd))

        wgate = jnp.pad(gla_w_gate[li], ((0, LANES - GLA_GATE_RANK), (0, 0)))
        o_a = _gla(pa, wgate, gla_b_gate[li][None, :], jnp.tile(gla_norm[li], N_HEADS)[None, :])
        o_b = _ret(pb, ret_cos, ret_sin, dmask, qfac, kfac, dec, jnp.tile(ret_norm[li], N_HEADS)[None, :])

        qn = _pad_cols(mla_q_norm[li][None, :], 256)
        kvn = jnp.pad(mla_kv_norm[li][None, :], ((0, 0), (MLA_Q_RANK, 0)))
        wuq = jnp.pad(mla_w_uq[li].reshape(MLA_Q_RANK, N_HEADS, MLA_NOPE + MLA_ROPE),
                      ((0, 256 - MLA_Q_RANK), (0, 0), (0, LANES - MLA_NOPE - MLA_ROPE)))
        wuq = wuq.reshape(256, N_HEADS * LANES).astype(BF16)
        wukv = mla_w_ukv[li].reshape(MLA_KV_RANK, N_HEADS, 2, 64)
        wukv = jnp.pad(wukv, ((MLA_Q_RANK, 0), (0, 0), (0, 0), (0, 64)))
        wukv = wukv.transpose(0, 2, 1, 3).reshape(256, 2 * N_HEADS * LANES).astype(BF16)
        o_c = _mla(pc, qn, kvn, wuq, wukv, mla_cos, mla_sa, mla_sb)

        lam_init = 0.8 - 0.6 * math.exp(-0.3 * li)
        o_d = _diff(pd, diff_lambda[li], _pad_cols(diff_norm[li][None, :], LANES), lam_init)

        o_a, o_b, o_c, o_d = (o.reshape(n, 256) for o in (o_a, o_b, o_c, o_d))
        wo = w_out[li].astype(BF16)
        fn = ffn_norm[li][None, :]
        j = li // 2
        if li % 2 == 0:
            h, hn = _outproj(h, o_a, o_b, o_c, o_d, wo, fn)
            h = _swiglu(h, hn, _chunk_cols(ffn_w_gate[j]), _chunk_cols(ffn_w_up[j]),
                        _chunk_rows(ffn_w_down[j]))
        else:
            h, hn, info, cnt = _outproj(h, o_a, o_b, o_c, o_d, wo, fn,
                                        router=_pad_cols(moe_router[j], LANES))
            h = _routed_moe(h, hn, info, cnt, jax.vmap(_chunk_cols)(moe_w_gate[j]),
                            jax.vmap(_chunk_cols)(moe_w_up[j]), jax.vmap(_chunk_rows)(moe_w_down[j]))

    return _final_norm(h.reshape(b, tp, d), final_norm[None, :])
```

```python
import functools
import math

import jax
import jax.numpy as jnp
from jax import lax
from jax.experimental import pallas as pl
from jax.experimental.pallas import tpu as pltpu

F32 = jnp.float32
BF16 = jnp.bfloat16

D_MODEL = 1024
DEPTH = 2
N_META = 16
CHUNK = 64
Q_BLOCK = 128
FRONT_PAD = Q_BLOCK - N_META
EPS = 1e-6
NEG = -1e30
ROPE_THETA = 10000.0
N_HEADS = 4
GLA_DK = 32
GLA_DV = 64
GLA_GATE_RANK = 16
GLA_TAU = 16.0
RET_DK = 64
MLA_Q_RANK = 192
MLA_KV_RANK = 64
MLA_NOPE = 64
MLA_ROPE = 32
DIFF_DK = 32
D_FF = 2816
N_EXPERTS = 8
D_FF_EXPERT = 3584

LANES = 128
ROW_TILE = 512
FF_CHUNK = 256
VMEM_LIMIT = 56 * 1024 * 1024

PA_W = 896
PB_W = 1024
PC_W = 384
PD_W = 1024


def _cparams(n_axes=1):
    return pltpu.CompilerParams(dimension_semantics=("arbitrary",) * n_axes,
                                vmem_limit_bytes=VMEM_LIMIT)


def _resident(shape):
    nd = len(shape)
    return pl.BlockSpec(shape, lambda *_: (0,) * nd, pipeline_mode=pl.Buffered(1))


def _sigmoid(x):
    return 1.0 / (1.0 + jnp.exp(-x))


def _split_bf16(x):
    hi = x.astype(BF16)
    lo = (x - hi.astype(F32)).astype(BF16)
    return hi, lo


def _dot(a, b):
    return jnp.dot(a, b, preferred_element_type=F32)


def _dot_nt(a, b):
    return lax.dot_general(a, b, (((1,), (1,)), ((), ())), preferred_element_type=F32)


def _inproj_kernel(h_ref, g_ref, w_ref, pa_ref, pb_ref, pc_ref, pd_ref):
    x = h_ref[...]
    ms = jnp.mean(x * x, axis=-1, keepdims=True)
    y = (x * lax.rsqrt(ms + EPS) * g_ref[...]).astype(BF16)
    off = 0
    for o_ref, width in ((pa_ref, PA_W), (pb_ref, PB_W), (pc_ref, PC_W), (pd_ref, PD_W)):
        o_ref[...] = _dot(y, w_ref[:, off:off + width]).astype(BF16)
        off += width


def _inproj(h, g, w):
    n = h.shape[0]
    wtot = PA_W + PB_W + PC_W + PD_W
    row = lambda width: pl.BlockSpec((ROW_TILE, width), lambda i: (i, 0))
    return pl.pallas_call(
        _inproj_kernel,
        grid=(n // ROW_TILE,),
        in_specs=[row(D_MODEL), _resident((1, D_MODEL)), _resident((D_MODEL, wtot))],
        out_specs=[row(PA_W), row(PB_W), row(PC_W), row(PD_W)],
        out_shape=[jax.ShapeDtypeStruct((n, w_), BF16) for w_ in (PA_W, PB_W, PC_W, PD_W)],
        compiler_params=_cparams(),
        name="inproj",
    )(h, g, w)


def _outproj_kernel(with_router, h_ref, oa_ref, ob_ref, oc_ref, od_ref, wo_ref, fn_ref, *rest):
    if with_router:
        router_ref, hmid_ref, hn_ref, info_ref, cnt_ref, carry_ref = rest
    else:
        hmid_ref, hn_ref = rest
    o = jnp.concatenate([oa_ref[...], ob_ref[...], oc_ref[...], od_ref[...]], axis=1)
    hm = h_ref[...] + _dot(o, wo_ref[...])
    hmid_ref[...] = hm
    ms = jnp.mean(hm * hm, axis=-1, keepdims=True)
    y = hm * lax.rsqrt(ms + EPS) * fn_ref[...]
    hn_ref[...] = y.astype(hn_ref.dtype)
    if with_router:
        y_hi, y_lo = _split_bf16(y)
        r_hi, r_lo = _split_bf16(router_ref[...])
        logits = _dot(y_hi, r_hi) + _dot(y_hi, r_lo) + _dot(y_lo, r_hi)
        lane = lax.broadcasted_iota(jnp.int32, logits.shape, 1).astype(F32)
        ninf = float("-inf")
        logits = jnp.where(lane < N_EXPERTS, logits, ninf)
        m1 = jnp.max(logits, axis=-1, keepdims=True)
        i1 = jnp.min(jnp.where(logits == m1, lane, float(LANES)), axis=-1, keepdims=True)
        rest_l = jnp.where(lane == i1, ninf, logits)
        m2 = jnp.max(rest_l, axis=-1, keepdims=True)
        i2 = jnp.min(jnp.where(rest_l == m2, lane, float(LANES)), axis=-1, keepdims=True)
        e2 = jnp.exp(m2 - m1)
        den = 1.0 + e2
        @pl.when(pl.program_id(0) == 0)
        def _():
            carry_ref[...] = jnp.zeros_like(carry_ref)
        sel = jnp.where(lane == i1, 1.0, 0.0) + jnp.where(lane == i2, 1.0, 0.0)
        rows = sel.shape[0]
        below = (lax.broadcasted_iota(jnp.int32, (rows, rows), 1)
                 < lax.broadcasted_iota(jnp.int32, (rows, rows), 0))
        count = _dot(jnp.where(below, 1.0, 0.0).astype(BF16), sel.astype(BF16)) + carry_ref[...]
        r1 = jnp.sum(jnp.where(lane == i1, count, 0.0), axis=-1, keepdims=True)
        r2 = jnp.sum(jnp.where(lane == i2, count, 0.0), axis=-1, keepdims=True)
        total = carry_ref[...] + jnp.sum(sel, axis=0, keepdims=True)
        carry_ref[...] = total
        cnt_ref[...] = jnp.broadcast_to(total, cnt_ref.shape)
        info = jnp.where(lane == 0, i1, 0.0)
        for k, val in enumerate((i2, 1.0 / den, e2 / den, r1, r2), start=1):
            info = jnp.where(lane == k, val, info)
        info_ref[...] = info


def _outproj(h, oa, ob, oc, od, wo, fn, router=None):
    n = h.shape[0]
    row = lambda width: pl.BlockSpec((ROW_TILE, width), lambda i: (i, 0))
    in_specs = [row(D_MODEL), row(256), row(256), row(256), row(256),
                _resident((D_MODEL, D_MODEL)), _resident((1, D_MODEL))]
    out_specs = [row(D_MODEL), row(D_MODEL)]
    out_shape = [jax.ShapeDtypeStruct((n, D_MODEL), F32),
                 jax.ShapeDtypeStruct((n, D_MODEL), BF16 if router is None else F32)]
    args = [h, oa, ob, oc, od, wo, fn]
    scratch = []
    if router is not None:
        in_specs.append(_resident((D_MODEL, LANES)))
        out_specs += [row(LANES), pl.BlockSpec((8, LANES), lambda i: (0, 0))]
        out_shape += [jax.ShapeDtypeStruct((n, LANES), F32), jax.ShapeDtypeStruct((8, LANES), F32)]
        args.append(router)
        scratch.append(pltpu.VMEM((1, LANES), F32))
    return pl.pallas_call(
        functools.partial(_outproj_kernel, router is not None),
        grid=(n // ROW_TILE,),
        in_specs=in_specs, out_specs=out_specs, out_shape=out_shape,
        scratch_shapes=scratch,
        compiler_params=_cparams(),
        name="outproj_router" if router is not None else "outproj",
    )(*args)


def _dispatch_kernel(pos_ref, x_ref, xs_in_ref, xs_ref, sem):
    del xs_in_ref
    n_rows = x_ref.shape[0]

    def row_copy(r, k):
        p = pos_ref[0, 0, 2 * r + k]
        return pltpu.make_async_copy(x_ref.at[pl.ds(r, 1), :], xs_ref.at[pl.ds(p, 1), :], sem)

    def start(r, carry):
        row_copy(r, 0).start()
        row_copy(r, 1).start()
        return carry

    def wait(r, carry):
        row_copy(r, 0).wait()
        row_copy(r, 1).wait()
        return carry

    lax.fori_loop(0, n_rows, start, 0, unroll=8)
    lax.fori_loop(0, n_rows, wait, 0, unroll=8)


def _dispatch(pos, x, n_sorted):
    n, d = x.shape
    xs0 = jnp.zeros((n_sorted, d), x.dtype)
    return pl.pallas_call(
        _dispatch_kernel,
        grid=(n // ROW_TILE,),
        in_specs=[pl.BlockSpec((1, 1, 2 * ROW_TILE), lambda i: (i, 0, 0), memory_space=pltpu.SMEM),
                  pl.BlockSpec((ROW_TILE, d), lambda i: (i, 0)),
                  pl.BlockSpec(memory_space=pl.ANY)],
        out_specs=pl.BlockSpec(memory_space=pl.ANY),
        out_shape=jax.ShapeDtypeStruct((n_sorted, d), x.dtype),
        scratch_shapes=[pltpu.SemaphoreType.DMA(())],
        input_output_aliases={2: 0},
        compiler_params=_cparams(),
        name="moe_dispatch",
    )(pos, x, xs0)


def _experts_kernel(n_chunks, te_ref, tv_ref, xs_ref, wg_ref, wu_ref, wd_ref, ys_ref, acc_ref):
    i = pl.program_id(0)

    @pl.when(tv_ref[i] > 0)
    def _():
        x = xs_ref[...].astype(BF16)
        acc_ref[...] = jnp.zeros_like(acc_ref)

        def body(c, carry):
            g = _dot(x, wg_ref[0, c])
            u = _dot(x, wu_ref[0, c])
            a = (g * _sigmoid(g) * u).astype(BF16)
            acc_ref[...] += _dot(a, wd_ref[0, c])
            return carry

        lax.fori_loop(0, n_chunks, body, 0)
        ys_ref[...] = acc_ref[...]

    @pl.when(tv_ref[i] == 0)
    def _():
        ys_ref[...] = jnp.zeros_like(ys_ref)


def _experts(tile_expert, tile_valid, xs, wg, wu, wd):
    n_sorted, d = xs.shape
    n_chunks = wg.shape[1]
    wspec = lambda shape: pl.BlockSpec((1,) + shape[1:], lambda i, te, tv: (te[i], 0, 0, 0),
                                       pipeline_mode=pl.Buffered(1))
    row = pl.BlockSpec((ROW_TILE, d), lambda i, te, tv: (i, 0))
    return pl.pallas_call(
        functools.partial(_experts_kernel, n_chunks),
        grid_spec=pltpu.PrefetchScalarGridSpec(
            num_scalar_prefetch=2,
            grid=(n_sorted // ROW_TILE,),
            in_specs=[row, wspec(wg.shape), wspec(wu.shape), wspec(wd.shape)],
            out_specs=row,
            scratch_shapes=[pltpu.VMEM((ROW_TILE, d), F32)]),
        out_shape=jax.ShapeDtypeStruct((n_sorted, d), F32),
        compiler_params=_cparams(),
        name="moe_experts",
    )(tile_expert, tile_valid, xs, wg, wu, wd)


def _combine_norm_kernel(tp, pos_ref, h_ref, info_ref, ys_ref, g_ref, o_ref,
                         h_buf, info_buf, y_buf, sem, row_sem):
    n_rows = h_buf.shape[0]
    row0 = pl.multiple_of(pl.program_id(0) * tp + Q_BLOCK + pl.program_id(1) * n_rows, Q_BLOCK)
    h_copy = pltpu.make_async_copy(h_ref.at[pl.ds(row0, n_rows), :], h_buf, sem.at[0])
    info_copy = pltpu.make_async_copy(info_ref.at[pl.ds(row0, n_rows), :], info_buf, sem.at[1])
    h_copy.start()
    info_copy.start()

    def row_copy(r, k):
        p = pos_ref[0, 0, 2 * r + k]
        return pltpu.make_async_copy(ys_ref.at[pl.ds(p, 1), :], y_buf.at[k, pl.ds(r, 1), :], row_sem)

    def start(r, carry):
        row_copy(r, 0).start()
        row_copy(r, 1).start()
        return carry

    def wait(r, carry):
        row_copy(r, 0).wait()
        row_copy(r, 1).wait()
        return carry

    lax.fori_loop(0, n_rows, start, 0, unroll=8)
    lax.fori_loop(0, n_rows, wait, 0, unroll=8)
    h_copy.wait()
    info_copy.wait()
    info = info_buf[...]
    x = h_buf[...] + info[:, 2:3] * y_buf[0] + info[:, 3:4] * y_buf[1]
    ms = jnp.mean(x * x, axis=-1, keepdims=True)
    o_ref[0] = x * lax.rsqrt(ms + EPS) * g_ref[...]


def _combine_norm(pos, h, info, ys, g, b, tp):
    n, d = h.shape
    seq = tp - Q_BLOCK
    tiles = seq // ROW_TILE
    any_spec = pl.BlockSpec(memory_space=pl.ANY)
    return pl.pallas_call(
        functools.partial(_combine_norm_kernel, tp),
        grid=(b, tiles),
        in_specs=[pl.BlockSpec((1, 1, 2 * ROW_TILE), lambda i, j: (i * tiles + j, 0, 0),
                               memory_space=pltpu.SMEM),
                  any_spec, any_spec, any_spec, _resident((1, d))],
        out_specs=pl.BlockSpec((1, ROW_TILE, d), lambda i, j: (i, j, 0)),
        out_shape=jax.ShapeDtypeStruct((b, seq, d), F32),
        scratch_shapes=[pltpu.VMEM((ROW_TILE, d), F32), pltpu.VMEM((ROW_TILE, LANES), F32),
                        pltpu.VMEM((2, ROW_TILE, d), F32),
                        pltpu.SemaphoreType.DMA((2,)), pltpu.SemaphoreType.DMA(())],
        compiler_params=_cparams(2),
        name="moe_combine_norm",
    )(pos, h, info, ys, g)


def _routed_moe_norm(h, hn, info, cnt, wg, wu, wd, g, b, tp):
    n = h.shape[0]
    n_sorted = 2 * n + N_EXPERTS * ROW_TILE
    n_tiles = n_sorted // ROW_TILE
    counts = cnt[0, :N_EXPERTS].astype(jnp.int32)
    tiles_e = (counts + ROW_TILE - 1) // ROW_TILE
    tile_end = jnp.cumsum(tiles_e)
    row_base = (tile_end - tiles_e) * ROW_TILE
    e_idx = info[:, 0:2].astype(jnp.int32)
    pos = row_base[e_idx] + info[:, 4:6].astype(jnp.int32)
    tile_ids = jnp.arange(n_tiles, dtype=jnp.int32)
    tile_expert = jnp.minimum(jnp.sum((tile_ids[:, None] >= tile_end[None, :]).astype(jnp.int32), axis=1),
                              N_EXPERTS - 1)
    tile_valid = (tile_ids < tile_end[-1]).astype(jnp.int32)
    xs = _dispatch(pos.reshape(n // ROW_TILE, 1, 2 * ROW_TILE), hn, n_sorted)
    ys = _experts(tile_expert, tile_valid, xs, wg, wu, wd)
    pos_seq = pos.reshape(b, tp, 2)[:, Q_BLOCK:, :].reshape(-1, 1, 2 * ROW_TILE)
    return _combine_norm(pos_seq, h, info, ys, g, b, tp)


def _swiglu_kernel(expert, n_chunks, h_ref, hn_ref, *rest):
    if expert is None:
        wg_ref, wu_ref, wd_ref, o_ref, acc_ref = rest
    else:
        gate_ref, wg_ref, wu_ref, wd_ref, o_ref, acc_ref = rest
    hn = hn_ref[...]
    acc_ref[...] = jnp.zeros_like(acc_ref)

    def body(c, carry):
        g = _dot(hn, wg_ref[c])
        u = _dot(hn, wu_ref[c])
        a = (g * _sigmoid(g) * u).astype(BF16)
        acc_ref[...] += _dot(a, wd_ref[c])
        return carry

    lax.fori_loop(0, n_chunks, body, 0)
    y = acc_ref[...]
    if expert is not None:
        y = gate_ref[:, expert:expert + 1] * y
    o_ref[...] = h_ref[...] + y


def _swiglu(h, hn, wg, wu, wd, gates=None, expert=None):
    n = h.shape[0]
    n_chunks = wg.shape[0]
    row = lambda width: pl.BlockSpec((ROW_TILE, width), lambda i: (i, 0))
    in_specs = [row(D_MODEL), row(D_MODEL)]
    args = [h, hn]
    if expert is not None:
        in_specs.append(row(LANES))
        args.append(gates)
    in_specs += [_resident(wg.shape), _resident(wu.shape), _resident(wd.shape)]
    args += [wg, wu, wd]
    return pl.pallas_call(
        functools.partial(_swiglu_kernel, expert, n_chunks),
        grid=(n // ROW_TILE,),
        in_specs=in_specs,
        out_specs=row(D_MODEL),
        out_shape=jax.ShapeDtypeStruct((n, D_MODEL), F32),
        scratch_shapes=[pltpu.VMEM((ROW_TILE, D_MODEL), F32)],
        input_output_aliases={0: 0},
        compiler_params=_cparams(),
        name="swiglu" if expert is None else f"moe_expert{expert}",
    )(*args)


def _group_ones(n, group_shift):
    r = lax.broadcasted_iota(jnp.int32, (n, n), 0) >> group_shift
    c = lax.broadcasted_iota(jnp.int32, (n, n), 1) >> group_shift
    return jnp.where(r == c, 1.0, 0.0).astype(BF16)


GLA_ROWS = 256


def _gla_kernel(tp, pa_ref, wg_ref, bg_ref, gn_ref, o_ref, s_ref):
    qhead = lax.broadcasted_iota(jnp.int32, (1, N_HEADS * GLA_DK), 1) >> 5
    ehead = lax.broadcasted_iota(jnp.int32, (1, N_HEADS * GLA_DV), 1) >> 6
    shead = lax.broadcasted_iota(jnp.int32, (N_HEADS * GLA_DK, 1), 0) >> 5
    bd = shead == ehead
    gsum = _group_ones(N_HEADS * GLA_DV, 6)
    wg_hi, wg_lo = _split_bf16(wg_ref[...])
    bg = bg_ref[...]
    gn = gn_ref[...]
    scale = GLA_DK ** -0.5
    s_ref[...] = jnp.zeros_like(s_ref)

    def block(r0, n_rows):
        n_c = n_rows // CHUNK
        rows = pl.ds(r0, n_rows)
        ri = lax.broadcasted_iota(jnp.int32, (n_rows, n_rows), 0)
        ci = lax.broadcasted_iota(jnp.int32, (n_rows, n_rows), 1)
        same = (ri >> 6) == (ci >> 6)
        tri_bf = jnp.where(jnp.logical_and(same, ri >= ci), 1.0, 0.0).astype(BF16)
        ones_bf = jnp.where(same, 1.0, 0.0).astype(BF16)
        r4 = lax.broadcasted_iota(jnp.int32, (N_HEADS * n_rows, n_rows), 0) & (n_rows - 1)
        c4 = lax.broadcasted_iota(jnp.int32, (N_HEADS * n_rows, n_rows), 1)
        tri4 = jnp.logical_and((r4 >> 6) == (c4 >> 6), r4 >= c4)
        col_chunk = lax.broadcasted_iota(jnp.int32, (1, n_rows), 1) >> 6

        q = pa_ref[0, rows, 0:128].astype(F32) * scale
        k = pa_ref[0, rows, 128:256].astype(F32)
        v = pa_ref[0, rows, 256:512]
        og = pa_ref[0, rows, 512:768].astype(F32)
        lr = pa_ref[0, rows, 768:896]
        valid = (r0 + lax.broadcasted_iota(jnp.int32, (n_rows, 1), 0)) >= FRONT_PAD

        pre = _dot(lr, wg_hi) + _dot(lr, wg_lo) + bg
        logsig = jnp.minimum(pre, 0.0) - jnp.log1p(jnp.exp(-jnp.abs(pre)))
        g = jnp.where(valid, logsig * (1.0 / GLA_TAU), 0.0)
        g_hi, g_lo = _split_bf16(g)
        cum = _dot(tri_bf, g_hi) + _dot(tri_bf, g_lo)
        cum_end = _dot(ones_bf, g_hi) + _dot(ones_bf, g_lo)
        qt_bf = (q * jnp.exp(cum)).astype(BF16)
        kt = (k * jnp.exp(-cum)).astype(BF16)
        kd_t = (k * jnp.exp(cum_end - cum)).T.astype(BF16)
        dec_t = jnp.exp(cum_end.T)

        qs = jnp.concatenate([jnp.where(qhead == h, qt_bf, jnp.zeros_like(qt_bf))
                              for h in range(N_HEADS)], axis=0)
        a = jnp.where(tri4, _dot_nt(qs, kt), 0.0).astype(BF16)
        r = _dot(a, v)
        o = jnp.where(ehead == 0, r[0:n_rows, :], 0.0)
        for h in range(1, N_HEADS):
            o = o + jnp.where(ehead == h, r[h * n_rows:(h + 1) * n_rows, :], 0.0)

        s = s_ref[...]
        inter = []
        for c in range(n_c):
            inter.append(_dot(qt_bf[c * CHUNK:(c + 1) * CHUNK, :], s.astype(BF16)))
            upd = _dot(jnp.where(col_chunk == c, kd_t, jnp.zeros_like(kd_t)), v)
            s = dec_t[:, c * CHUNK:c * CHUNK + 1] * s + jnp.where(bd, upd, 0.0)
        s_ref[...] = s
        o = o + jnp.concatenate(inter, axis=0)

        ms = _dot((o * o).astype(BF16), gsum) * (1.0 / GLA_DV)
        y = o * lax.rsqrt(ms + EPS) * gn * (og * _sigmoid(og))
        o_ref[0, rows, :] = jnp.where(valid, y, 0.0).astype(BF16)

    block(0, Q_BLOCK)

    def body(i, carry):
        block(pl.multiple_of(Q_BLOCK + i * GLA_ROWS, Q_BLOCK), GLA_ROWS)
        return carry

    lax.fori_loop(0, (tp - Q_BLOCK) // GLA_ROWS, body, 0)


def _gla(pa, wg, bg, gn):
    b, tp, _ = pa.shape
    return pl.pallas_call(
        functools.partial(_gla_kernel, tp),
        grid=(b,),
        in_specs=[pl.BlockSpec((1, tp, PA_W), lambda i: (i, 0, 0)),
                  _resident((LANES, LANES)), _resident((1, LANES)), _resident((1, 256))],
        out_specs=pl.BlockSpec((1, tp, 256), lambda i: (i, 0, 0)),
        out_shape=jax.ShapeDtypeStruct((b, tp, 256), BF16),
        scratch_shapes=[pltpu.VMEM((N_HEADS * GLA_DK, N_HEADS * GLA_DV), F32)],
        compiler_params=_cparams(),
        name="gla",
    )(pa, wg, bg, gn)


RET_BLOCK = 128


def _ret_kernel(tp, pb_ref, cos_ref, sin_ref, dmask_ref, qfac_ref, kfac_ref, dec_ref, gn_ref,
                o_ref, s_ref):
    blk = RET_BLOCK
    n_blocks = tp // blk
    qhead = (lax.broadcasted_iota(jnp.int32, (1, 256), 1) & 127) >> 5
    ehead = lax.broadcasted_iota(jnp.int32, (1, 256), 1) >> 6
    shead = (lax.broadcasted_iota(jnp.int32, (256, 1), 0) & 127) >> 5
    bd = shead == ehead
    gsum = _group_ones(256, 6)
    gn = gn_ref[...]
    s_ref[...] = jnp.zeros_like(s_ref)

    def rope(x, cos, sin):
        x1, x2 = x[:, :128], x[:, 128:]
        return jnp.concatenate([x1 * cos - x2 * sin, x1 * sin + x2 * cos], axis=1)

    def block(j, carry):
        r0 = pl.multiple_of(j * blk, blk)
        rows = pl.ds(r0, blk)
        cos = cos_ref[rows, :]
        sin = sin_ref[rows, :]
        q = rope(pb_ref[0, rows, 0:256].astype(F32), cos, sin)
        k = rope(pb_ref[0, rows, 256:512].astype(F32), cos, sin) * (RET_DK ** -0.5)
        v = pb_ref[0, rows, 512:768]
        og = pb_ref[0, rows, 768:1024].astype(F32)
        valid = (r0 + lax.broadcasted_iota(jnp.int32, (blk, 1), 0)) >= FRONT_PAD

        q_bf = q.astype(BF16)
        qs = jnp.concatenate([jnp.where(qhead == h, q_bf, jnp.zeros_like(q_bf))
                              for h in range(N_HEADS)], axis=0)
        a = (_dot_nt(qs, k.astype(BF16)) * dmask_ref[...]).astype(BF16)
        r = _dot(a, v)
        o = _dot((q * qfac_ref[...]).astype(BF16), s_ref[...].astype(BF16))
        for h in range(N_HEADS):
            o = o + jnp.where(ehead == h, r[h * blk:(h + 1) * blk, :], 0.0)

        kd_t = (k * kfac_ref[...]).T.astype(BF16)
        upd = _dot(kd_t, v)
        s_ref[...] = dec_ref[...] * s_ref[...] + jnp.where(bd, upd, 0.0)

        mu = _dot(o.astype(BF16), gsum) * (1.0 / 64)
        xc = o - mu
        var = _dot((xc * xc).astype(BF16), gsum) * (1.0 / 64)
        y = xc * lax.rsqrt(var + EPS) * gn * (og * _sigmoid(og))
        o_ref[0, rows, :] = jnp.where(valid, y, 0.0).astype(BF16)
        return carry

    lax.fori_loop(0, n_blocks, block, 0, unroll=2)


def _ret(pb, cos, sin, dmask, qfac, kfac, dec, gn):
    b, tp, _ = pb.shape
    return pl.pallas_call(
        functools.partial(_ret_kernel, tp),
        grid=(b,),
        in_specs=[pl.BlockSpec((1, tp, PB_W), lambda i: (i, 0, 0)),
                  _resident(cos.shape), _resident(sin.shape), _resident(dmask.shape),
                  _resident(qfac.shape), _resident(kfac.shape), _resident(dec.shape),
                  _resident((1, 256))],
        out_specs=pl.BlockSpec((1, tp, 256), lambda i: (i, 0, 0)),
        out_shape=jax.ShapeDtypeStruct((b, tp, 256), BF16),
        scratch_shapes=[pltpu.VMEM((256, 256), F32)],
        compiler_params=_cparams(),
        name="retention",
    )(pb, cos, sin, dmask, qfac, kfac, dec, gn)


ATT_ROWS = 256
LOG2E = 1.4426950408889634


def _for_pairs(n, body):
    def pair(t, carry):
        body(2 * t)
        body(2 * t + 1)
        return carry

    lax.fori_loop(0, n >> 1, pair, 0)

    @pl.when((n & 1) == 1)
    def _():
        body(n - 1)


def _mask_groups(mask, s, n_g):
    rows = s.shape[0] // n_g
    return jnp.concatenate([jnp.where(mask, s[g * rows:(g + 1) * rows, :], NEG)
                            for g in range(n_g)], axis=0)


def _attn_block0(qk_fn, v_fn, n_g):
    rows = pl.ds(0, Q_BLOCK)
    qrow = lax.broadcasted_iota(jnp.int32, (Q_BLOCK, 1), 0)
    kcol = lax.broadcasted_iota(jnp.int32, (1, Q_BLOCK), 1)
    mask = jnp.logical_and(kcol <= qrow, kcol >= FRONT_PAD)
    s = _mask_groups(mask, qk_fn(0, Q_BLOCK, rows), n_g)
    p = jnp.exp2(s - jnp.max(s, axis=-1, keepdims=True))
    l = jnp.sum(p, axis=-1, keepdims=True)
    p_bf = p.astype(BF16)
    pv = jnp.concatenate([_dot(p_bf[g * Q_BLOCK:(g + 1) * Q_BLOCK, :], v_fn(g, rows))
                          for g in range(n_g)], axis=0)
    return pv / l


def _attn_block(i, qk_fn, v_fn, n_g, s_meta, s_s, mx_s, l_s, acc_s):
    rr = ATT_ROWS
    q0 = pl.multiple_of(Q_BLOCK + (i - 1) * rr, Q_BLOCK)
    meta_rows = pl.ds(0, Q_BLOCK)
    kcol = lax.broadcasted_iota(jnp.int32, (1, Q_BLOCK), 1)

    def key_rows(j):
        return pl.ds(pl.multiple_of(Q_BLOCK + j * rr, Q_BLOCK), rr)

    s = jnp.where(kcol >= FRONT_PAD, qk_fn(q0, rr, meta_rows), NEG)
    s_meta[...] = s
    mx_s[...] = s

    def pass1(j):
        sj = qk_fn(q0, rr, key_rows(j))
        s_s[j] = sj
        mx_s[...] = jnp.maximum(mx_s[...], jnp.maximum(sj[:, :LANES], sj[:, LANES:]))

    _for_pairs(i - 1, pass1)
    causal = (lax.broadcasted_iota(jnp.int32, (rr, rr), 1)
              <= lax.broadcasted_iota(jnp.int32, (rr, rr), 0))
    sd = _mask_groups(causal, qk_fn(q0, rr, pl.ds(q0, rr)), n_g)
    s_s[i - 1] = sd
    m = jnp.max(jnp.maximum(mx_s[...], jnp.maximum(sd[:, :LANES], sd[:, LANES:])),
                axis=-1, keepdims=True)
    mx_s[...] = jnp.broadcast_to(m, mx_s.shape)

    p = jnp.exp2(s_meta[...] - mx_s[...])
    l_s[...] = p
    p_bf = p.astype(BF16)
    for g in range(n_g):
        acc_s[g * rr:(g + 1) * rr, :] = _dot(p_bf[g * rr:(g + 1) * rr, :], v_fn(g, meta_rows))

    def pass2(j):
        sj = s_s[j]
        mrep = mx_s[...]
        p0 = jnp.exp2(sj[:, :LANES] - mrep)
        p1 = jnp.exp2(sj[:, LANES:] - mrep)
        l_s[...] += p0 + p1
        pj = jnp.concatenate([p0.astype(BF16), p1.astype(BF16)], axis=1)
        for g in range(n_g):
            acc_s[g * rr:(g + 1) * rr, :] += _dot(pj[g * rr:(g + 1) * rr, :], v_fn(g, key_rows(j)))

    _for_pairs(i, pass2)
    return acc_s[...] / jnp.sum(l_s[...], axis=-1, keepdims=True)


def _heads_to_lanes(per_head):
    lo = per_head[0] + pltpu.roll(per_head[1], 64, 1)
    hi = per_head[2] + pltpu.roll(per_head[3], 64, 1)
    return jnp.concatenate([lo, hi], axis=1)


def _mla_kernel(tp, pc_ref, qn_ref, kvn_ref, wuq_ref, wukv_ref, cos_ref, sa_ref, sb_ref,
                o_ref, q_s, k_s, v_s, s_meta, s_s, mx_s, l_s, acc_s):
    n_blocks = tp // Q_BLOCK
    scale = (MLA_NOPE + MLA_ROPE) ** -0.5 * LOG2E
    is_q = lax.broadcasted_iota(jnp.int32, (1, 256), 1) < MLA_Q_RANK

    def prep(i, carry):
        r0 = pl.multiple_of(i * Q_BLOCK, Q_BLOCK)
        rows = pl.ds(r0, Q_BLOCK)
        x = pc_ref[0, rows, 0:256].astype(F32)
        x2 = x * x
        ms_q = jnp.sum(jnp.where(is_q, x2, 0.0), axis=-1, keepdims=True) * (1.0 / MLA_Q_RANK)
        ms_kv = jnp.sum(jnp.where(is_q, 0.0, x2), axis=-1, keepdims=True) * (1.0 / MLA_KV_RANK)
        yq = (x * lax.rsqrt(ms_q + EPS) * qn_ref[...]).astype(BF16)
        ykv = (x * lax.rsqrt(ms_kv + EPS) * kvn_ref[...]).astype(BF16)
        cq = _dot(yq, wuq_ref[...])
        kv = _dot(ykv, wukv_ref[...])
        cos = cos_ref[rows, :]
        sa = sa_ref[rows, :]
        sb = sb_ref[rows, :]

        def rope(t):
            return t * cos + pltpu.roll(t, 16, 1) * sa + pltpu.roll(t, LANES - 16, 1) * sb

        kpe = rope(pc_ref[0, rows, 256:384].astype(F32))
        for h in range(N_HEADS):
            q_s[h, rows, :] = (rope(cq[:, h * LANES:(h + 1) * LANES]) * scale).astype(BF16)
            k_s[h, rows, :] = (kv[:, h * LANES:(h + 1) * LANES] + kpe).astype(BF16)
            v_s[h, rows, :] = kv[:, (N_HEADS + h) * LANES:(N_HEADS + h + 1) * LANES].astype(BF16)
        return carry

    lax.fori_loop(0, n_blocks, prep, 0)

    def qk_fn(q0, n_rows, krows):
        return jnp.concatenate([_dot_nt(q_s[h, pl.ds(q0, n_rows), :], k_s[h, krows, :])
                                for h in range(N_HEADS)], axis=0)

    def v_fn(h, krows):
        return v_s[h, krows, :]

    def emit(q0, n_rows, o, first):
        y = _heads_to_lanes([o[h * n_rows:(h + 1) * n_rows, :] for h in range(N_HEADS)])
        if first:
            qrow = lax.broadcasted_iota(jnp.int32, (n_rows, 1), 0)
            y = jnp.where(qrow >= FRONT_PAD, y, 0.0)
        o_ref[0, pl.ds(q0, n_rows), :] = y.astype(BF16)

    emit(0, Q_BLOCK, _attn_block0(qk_fn, v_fn, N_HEADS), True)

    def qblock(i, carry):
        o = _attn_block(i, qk_fn, v_fn, N_HEADS, s_meta, s_s, mx_s, l_s, acc_s)
        emit(pl.multiple_of(Q_BLOCK + (i - 1) * ATT_ROWS, Q_BLOCK), ATT_ROWS, o, False)
        return carry

    lax.fori_loop(1, (tp - Q_BLOCK) // ATT_ROWS + 1, qblock, 0)


def _attn_scratch(n_g, tp):
    g_rows = n_g * ATT_ROWS
    n_slots = (tp - Q_BLOCK) // ATT_ROWS
    return [pltpu.VMEM((g_rows, LANES), F32),
            pltpu.VMEM((n_slots, g_rows, ATT_ROWS), F32),
            pltpu.VMEM((g_rows, LANES), F32),
            pltpu.VMEM((g_rows, LANES), F32),
            pltpu.VMEM((g_rows, LANES), F32)]


def _mla(pc, qn, kvn, wuq, wukv, cos, sa, sb):
    b, tp, _ = pc.shape
    return pl.pallas_call(
        functools.partial(_mla_kernel, tp),
        grid=(b,),
        in_specs=[pl.BlockSpec((1, tp, PC_W), lambda i: (i, 0, 0)),
                  _resident((1, 256)), _resident((1, 256)),
                  _resident(wuq.shape), _resident(wukv.shape),
                  _resident(cos.shape), _resident(sa.shape), _resident(sb.shape)],
        out_specs=pl.BlockSpec((1, tp, 256), lambda i: (i, 0, 0)),
        out_shape=jax.ShapeDtypeStruct((b, tp, 256), BF16),
        scratch_shapes=[pltpu.VMEM((N_HEADS, tp, LANES), BF16),
                        pltpu.VMEM((N_HEADS, tp, LANES), BF16),
                        pltpu.VMEM((N_HEADS, tp, LANES), BF16)] + _attn_scratch(N_HEADS, tp),
        compiler_params=_cparams(),
        name="mla",
    )(pc, qn, kvn, wuq, wukv, cos, sa, sb)


def _diff_kernel(tp, lam_init, pd_ref, lam_ref, dn_ref, o_ref, qs_s, s_meta, s_s, mx_s, l_s, acc_s):
    n_maps = 2 * N_HEADS
    scale = DIFF_DK ** -0.5 * LOG2E
    group = lax.broadcasted_iota(jnp.int32, (1, 256), 1) >> 5
    lv = lam_ref[...]
    lam = (jnp.exp(jnp.sum(lv[0:1, :] * lv[1:2, :], axis=-1, keepdims=True))
           - jnp.exp(jnp.sum(lv[2:3, :] * lv[3:4, :], axis=-1, keepdims=True)) + lam_init)
    dn = dn_ref[...]

    def stack_queries(q0, n_rows):
        q = (pd_ref[0, pl.ds(q0, n_rows), 0:256].astype(F32) * scale).astype(BF16)
        for g in range(n_maps):
            qs_s[g * n_rows:(g + 1) * n_rows, :] = jnp.where(group == g, q, jnp.zeros_like(q))

    def qk_fn(q0, n_rows, krows):
        return _dot_nt(qs_s[0:n_maps * n_rows, :], pd_ref[0, krows, 256:512])

    def v_fn(g, krows):
        return pd_ref[0, krows, 512 + (g // 2) * LANES:512 + (g // 2 + 1) * LANES]

    def emit(q0, n_rows, o, first):
        od = jnp.concatenate([o[(2 * h) * n_rows:(2 * h + 1) * n_rows, :]
                              - lam * o[(2 * h + 1) * n_rows:(2 * h + 2) * n_rows, :]
                              for h in range(N_HEADS)], axis=0)
        ms = jnp.sum(od * od, axis=-1, keepdims=True) * (1.0 / 64)
        yh = od * lax.rsqrt(ms + EPS) * dn * (1.0 - lam_init)
        y = _heads_to_lanes([yh[h * n_rows:(h + 1) * n_rows, :] for h in range(N_HEADS)])
        if first:
            qrow = lax.broadcasted_iota(jnp.int32, (n_rows, 1), 0)
            y = jnp.where(qrow >= FRONT_PAD, y, 0.0)
        o_ref[0, pl.ds(q0, n_rows), :] = y.astype(BF16)

    stack_queries(0, Q_BLOCK)
    emit(0, Q_BLOCK, _attn_block0(qk_fn, v_fn, n_maps), True)

    def qblock(i, carry):
        q0 = pl.multiple_of(Q_BLOCK + (i - 1) * ATT_ROWS, Q_BLOCK)
        stack_queries(q0, ATT_ROWS)
        o = _attn_block(i, qk_fn, v_fn, n_maps, s_meta, s_s, mx_s, l_s, acc_s)
        emit(q0, ATT_ROWS, o, False)
        return carry

    lax.fori_loop(1, (tp - Q_BLOCK) // ATT_ROWS + 1, qblock, 0)


def _diff(pd, lam_rows, dn, lam_init):
    b, tp, _ = pd.shape
    n_maps = 2 * N_HEADS
    return pl.pallas_call(
        functools.partial(_diff_kernel, tp, lam_init),
        grid=(b,),
        in_specs=[pl.BlockSpec((1, tp, PD_W), lambda i: (i, 0, 0)),
                  _resident(lam_rows.shape), _resident((1, LANES))],
        out_specs=pl.BlockSpec((1, tp, 256), lambda i: (i, 0, 0)),
        out_shape=jax.ShapeDtypeStruct((b, tp, 256), BF16),
        scratch_shapes=[pltpu.VMEM((n_maps * ATT_ROWS, 256), BF16)] + _attn_scratch(n_maps, tp),
        compiler_params=_cparams(),
        name="diffattn",
    )(pd, lam_rows, dn)


def _pad_cols(x, width):
    return jnp.pad(x, ((0, 0), (0, width - x.shape[1])))


def _rot_split(w):
    d = w.shape[0]
    return w.reshape(d, N_HEADS, 2, 32).transpose(0, 2, 1, 3).reshape(d, 256)


def _layout_w_in(w):
    sizes = (128, 128, 256, 16, 256, 256, 256, 256, 256, 192, 64, 32, 256, 256, 256)
    offs = [0]
    for s_ in sizes:
        offs.append(offs[-1] + s_)
    seg = [w[:, offs[i]:offs[i + 1]] for i in range(len(sizes))]
    (a_q, a_k, a_v, a_lr, a_og, r_q, r_k, r_v, r_og, c_cq, c_ckv, c_kpe, d_q, d_k, d_v) = seg
    d = w.shape[0]
    z = lambda n: jnp.zeros((d, n), w.dtype)
    dv_p = jnp.pad(d_v.reshape(d, N_HEADS, 64), ((0, 0), (0, 0), (0, 64))).reshape(d, 512)
    cols = [a_q, a_k, a_v, a_og, a_lr, z(112),
            _rot_split(r_q), _rot_split(r_k), r_v, r_og,
            c_cq, c_ckv, z(64), c_kpe, z(32),
            d_q, d_k, dv_p]
    return jnp.concatenate(cols, axis=1).astype(BF16)


def _tables(tp):
    pos = jnp.arange(tp, dtype=F32) - FRONT_PAD
    inv = ROPE_THETA ** (-jnp.arange(32, dtype=F32) / 32)
    ang = pos[:, None] * inv[None, :]
    ret_cos = jnp.tile(jnp.cos(ang), (1, N_HEADS))
    ret_sin = jnp.tile(jnp.sin(ang), (1, N_HEADS))
    inv16 = ROPE_THETA ** (-jnp.arange(16, dtype=F32) / 16)
    ang16 = pos[:, None] * inv16[None, :]
    c16, s16 = jnp.cos(ang16), jnp.sin(ang16)
    one = lambda n: jnp.ones((tp, n), F32)
    zero = lambda n: jnp.zeros((tp, n), F32)
    mla_cos = jnp.concatenate([one(64), c16, c16, one(32)], axis=1)
    mla_sa = jnp.concatenate([zero(80), s16, zero(32)], axis=1)
    mla_sb = jnp.concatenate([zero(64), -s16, zero(48)], axis=1)
    lg = jnp.log(1.0 - jnp.exp2(-5.0 - jnp.arange(N_HEADS, dtype=F32)))
    idx = jnp.arange(RET_BLOCK, dtype=F32)
    rel = idx[:, None] - idx[None, :]
    dmask = jnp.where(rel[None] >= 0, jnp.exp(rel[None] * lg[:, None, None]), 0.0)
    dmask = dmask.reshape(N_HEADS * RET_BLOCK, RET_BLOCK)
    lane_head = (jnp.arange(256) % 128) // 32
    qfac = jnp.exp((idx[:, None] + 1.0) * lg[lane_head][None, :])
    kfac = jnp.exp((RET_BLOCK - 1.0 - idx[:, None]) * lg[lane_head][None, :])
    dec = jnp.exp(RET_BLOCK * lg[lane_head])[:, None]
    return ret_cos, ret_sin, mla_cos, mla_sa, mla_sb, dmask, qfac, kfac, dec


def _chunk_cols(w):
    d, f = w.shape
    return w.reshape(d, f // FF_CHUNK, FF_CHUNK).transpose(1, 0, 2).astype(BF16)


def _chunk_rows(w):
    f, d = w.shape
    return w.reshape(f // FF_CHUNK, FF_CHUNK, d).astype(BF16)


def kernel(x, meta_tokens, attn_norm, w_in, gla_w_gate, gla_b_gate, gla_norm, ret_norm, mla_q_norm, mla_w_uq, mla_kv_norm, mla_w_ukv, diff_lambda, diff_norm, w_out, ffn_norm, ffn_w_gate, ffn_w_up, ffn_w_down, moe_router, moe_w_gate, moe_w_up, moe_w_down, final_norm):
    b, seq, d = x.shape
    tp = FRONT_PAD + N_META + seq
    n = b * tp
    meta = jnp.broadcast_to(meta_tokens[None].astype(x.dtype), (b, N_META, d))
    h = jnp.concatenate([jnp.zeros((b, FRONT_PAD, d), x.dtype), meta, x], axis=1).reshape(n, d)
    ret_cos, ret_sin, mla_cos, mla_sa, mla_sb, dmask, qfac, kfac, dec = _tables(tp)

    for li in range(DEPTH):
        pa, pb, pc, pd = _inproj(h, attn_norm[li][None, :], _layout_w_in(w_in[li]))
        pa, pb, pc, pd = (p.reshape(b, tp, -1) for p in (pa, pb, pc, pd))

        wgate = jnp.pad(gla_w_gate[li], ((0, LANES - GLA_GATE_RANK), (0, 0)))
        o_a = _gla(pa, wgate, gla_b_gate[li][None, :], jnp.tile(gla_norm[li], N_HEADS)[None, :])
        o_b = _ret(pb, ret_cos, ret_sin, dmask, qfac, kfac, dec, jnp.tile(ret_norm[li], N_HEADS)[None, :])

        qn = _pad_cols(mla_q_norm[li][None, :], 256)
        kvn = jnp.pad(mla_kv_norm[li][None, :], ((0, 0), (MLA_Q_RANK, 0)))
        wuq = jnp.pad(mla_w_uq[li].reshape(MLA_Q_RANK, N_HEADS, MLA_NOPE + MLA_ROPE),
                      ((0, 256 - MLA_Q_RANK), (0, 0), (0, LANES - MLA_NOPE - MLA_ROPE)))
        wuq = wuq.reshape(256, N_HEADS * LANES).astype(BF16)
        wukv = mla_w_ukv[li].reshape(MLA_KV_RANK, N_HEADS, 2, 64)
        wukv = jnp.pad(wukv, ((MLA_Q_RANK, 0), (0, 0), (0, 0), (0, 64)))
        wukv = wukv.transpose(0, 2, 1, 3).reshape(256, 2 * N_HEADS * LANES).astype(BF16)
        o_c = _mla(pc, qn, kvn, wuq, wukv, mla_cos, mla_sa, mla_sb)

        lam_init = 0.8 - 0.6 * math.exp(-0.3 * li)
        o_d = _diff(pd, diff_lambda[li], _pad_cols(diff_norm[li][None, :], LANES), lam_init)

        o_a, o_b, o_c, o_d = (o.reshape(n, 256) for o in (o_a, o_b, o_c, o_d))
        wo = w_out[li].astype(BF16)
        fn = ffn_norm[li][None, :]
        j = li // 2
        if li % 2 == 0:
            h, hn = _outproj(h, o_a, o_b, o_c, o_d, wo, fn)
            h = _swiglu(h, hn, _chunk_cols(ffn_w_gate[j]), _chunk_cols(ffn_w_up[j]),
                        _chunk_rows(ffn_w_down[j]))
        else:
            h, hn, info, cnt = _outproj(h, o_a, o_b, o_c, o_d, wo, fn,
                                        router=_pad_cols(moe_router[j], LANES))
            return _routed_moe_norm(h, hn, info, cnt, jax.vmap(_chunk_cols)(moe_w_gate[j]),
                                    jax.vmap(_chunk_cols)(moe_w_up[j]),
                                    jax.vmap(_chunk_rows)(moe_w_down[j]), final_norm[None, :], b, tp)
```

```python
import functools
import math

import jax
import jax.numpy as jnp
from jax import lax
from jax.experimental import pallas as pl
from jax.experimental.pallas import tpu as pltpu

F32 = jnp.float32
BF16 = jnp.bfloat16

D_MODEL = 1024
DEPTH = 2
N_META = 16
CHUNK = 64
Q_BLOCK = 128
FRONT_PAD = Q_BLOCK - N_META
EPS = 1e-6
NEG = -1e30
ROPE_THETA = 10000.0
N_HEADS = 4
GLA_DK = 32
GLA_DV = 64
GLA_GATE_RANK = 16
GLA_TAU = 16.0
RET_DK = 64
MLA_Q_RANK = 192
MLA_KV_RANK = 64
MLA_NOPE = 64
MLA_ROPE = 32
DIFF_DK = 32
D_FF = 2816
N_EXPERTS = 8
D_FF_EXPERT = 3584

LANES = 128
ROW_TILE = 512
FF_CHUNK = 256
VMEM_LIMIT = 56 * 1024 * 1024

PA_W = 896
PB_W = 1024
PC_W = 384
PD_W = 1024


def _cparams(n_axes=1):
    return pltpu.CompilerParams(dimension_semantics=("arbitrary",) * n_axes,
                                vmem_limit_bytes=VMEM_LIMIT)


def _resident(shape):
    nd = len(shape)
    return pl.BlockSpec(shape, lambda *_: (0,) * nd, pipeline_mode=pl.Buffered(1))


def _sigmoid(x):
    return 1.0 / (1.0 + jnp.exp(-x))


def _split_bf16(x):
    hi = x.astype(BF16)
    lo = (x - hi.astype(F32)).astype(BF16)
    return hi, lo


def _dot(a, b):
    return jnp.dot(a, b, preferred_element_type=F32)


def _dot_nt(a, b):
    return lax.dot_general(a, b, (((1,), (1,)), ((), ())), preferred_element_type=F32)


def _inproj_kernel(h_ref, g_ref, w_ref, pa_ref, pb_ref, pc_ref, pd_ref):
    x = h_ref[...]
    ms = jnp.mean(x * x, axis=-1, keepdims=True)
    y = (x * lax.rsqrt(ms + EPS) * g_ref[...]).astype(BF16)
    off = 0
    for o_ref, width in ((pa_ref, PA_W), (pb_ref, PB_W), (pc_ref, PC_W), (pd_ref, PD_W)):
        o_ref[...] = _dot(y, w_ref[:, off:off + width]).astype(BF16)
        off += width


def _inproj(h, g, w):
    n = h.shape[0]
    wtot = PA_W + PB_W + PC_W + PD_W
    row = lambda width: pl.BlockSpec((ROW_TILE, width), lambda i: (i, 0))
    return pl.pallas_call(
        _inproj_kernel,
        grid=(n // ROW_TILE,),
        in_specs=[row(D_MODEL), _resident((1, D_MODEL)), _resident((D_MODEL, wtot))],
        out_specs=[row(PA_W), row(PB_W), row(PC_W), row(PD_W)],
        out_shape=[jax.ShapeDtypeStruct((n, w_), BF16) for w_ in (PA_W, PB_W, PC_W, PD_W)],
        compiler_params=_cparams(),
        name="inproj",
    )(h, g, w)


def _outproj_kernel(with_router, h_ref, oa_ref, ob_ref, oc_ref, od_ref, wo_ref, fn_ref, *rest):
    if with_router:
        router_ref, hmid_ref, hn_ref, info_ref, cnt_ref, carry_ref = rest
    else:
        hmid_ref, hn_ref = rest
    o = jnp.concatenate([oa_ref[...], ob_ref[...], oc_ref[...], od_ref[...]], axis=1)
    hm = h_ref[...] + _dot(o, wo_ref[...])
    hmid_ref[...] = hm
    ms = jnp.mean(hm * hm, axis=-1, keepdims=True)
    y = hm * lax.rsqrt(ms + EPS) * fn_ref[...]
    hn_ref[...] = y.astype(hn_ref.dtype)
    if with_router:
        y_hi, y_lo = _split_bf16(y)
        r_hi, r_lo = _split_bf16(router_ref[...])
        logits = _dot(y_hi, r_hi) + _dot(y_hi, r_lo) + _dot(y_lo, r_hi)
        lane = lax.broadcasted_iota(jnp.int32, logits.shape, 1).astype(F32)
        ninf = float("-inf")
        logits = jnp.where(lane < N_EXPERTS, logits, ninf)
        m1 = jnp.max(logits, axis=-1, keepdims=True)
        i1 = jnp.min(jnp.where(logits == m1, lane, float(LANES)), axis=-1, keepdims=True)
        rest_l = jnp.where(lane == i1, ninf, logits)
        m2 = jnp.max(rest_l, axis=-1, keepdims=True)
        i2 = jnp.min(jnp.where(rest_l == m2, lane, float(LANES)), axis=-1, keepdims=True)
        e2 = jnp.exp(m2 - m1)
        den = 1.0 + e2
        @pl.when(pl.program_id(0) == 0)
        def _():
            carry_ref[...] = jnp.zeros_like(carry_ref)
        sel = jnp.where(lane == i1, 1.0, 0.0) + jnp.where(lane == i2, 1.0, 0.0)
        rows = sel.shape[0]
        below = (lax.broadcasted_iota(jnp.int32, (rows, rows), 1)
                 < lax.broadcasted_iota(jnp.int32, (rows, rows), 0))
        count = _dot(jnp.where(below, 1.0, 0.0).astype(BF16), sel.astype(BF16)) + carry_ref[...]
        r1 = jnp.sum(jnp.where(lane == i1, count, 0.0), axis=-1, keepdims=True)
        r2 = jnp.sum(jnp.where(lane == i2, count, 0.0), axis=-1, keepdims=True)
        total = carry_ref[...] + jnp.sum(sel, axis=0, keepdims=True)
        carry_ref[...] = total
        cnt_ref[...] = jnp.broadcast_to(total, cnt_ref.shape)
        info = jnp.where(lane == 0, i1, 0.0)
        for k, val in enumerate((i2, 1.0 / den, e2 / den, r1, r2), start=1):
            info = jnp.where(lane == k, val, info)
        info_ref[...] = info


def _outproj(h, oa, ob, oc, od, wo, fn, router=None):
    n = h.shape[0]
    row = lambda width: pl.BlockSpec((ROW_TILE, width), lambda i: (i, 0))
    in_specs = [row(D_MODEL), row(256), row(256), row(256), row(256),
                _resident((D_MODEL, D_MODEL)), _resident((1, D_MODEL))]
    out_specs = [row(D_MODEL), row(D_MODEL)]
    out_shape = [jax.ShapeDtypeStruct((n, D_MODEL), F32),
                 jax.ShapeDtypeStruct((n, D_MODEL), BF16 if router is None else F32)]
    args = [h, oa, ob, oc, od, wo, fn]
    scratch = []
    if router is not None:
        in_specs.append(_resident((D_MODEL, LANES)))
        out_specs += [row(LANES), pl.BlockSpec((8, LANES), lambda i: (0, 0))]
        out_shape += [jax.ShapeDtypeStruct((n, LANES), F32), jax.ShapeDtypeStruct((8, LANES), F32)]
        args.append(router)
        scratch.append(pltpu.VMEM((1, LANES), F32))
    return pl.pallas_call(
        functools.partial(_outproj_kernel, router is not None),
        grid=(n // ROW_TILE,),
        in_specs=in_specs, out_specs=out_specs, out_shape=out_shape,
        scratch_shapes=scratch,
        compiler_params=_cparams(),
        name="outproj_router" if router is not None else "outproj",
    )(*args)


def _dispatch_kernel(pos_ref, x_ref, xs_in_ref, xs_ref, sem):
    del xs_in_ref
    n_rows = x_ref.shape[0]

    def row_copy(r, k):
        p = pos_ref[0, 0, 2 * r + k]
        return pltpu.make_async_copy(x_ref.at[pl.ds(r, 1), :], xs_ref.at[pl.ds(p, 1), :], sem)

    def start(r, carry):
        row_copy(r, 0).start()
        row_copy(r, 1).start()
        return carry

    def wait(r, carry):
        row_copy(r, 0).wait()
        row_copy(r, 1).wait()
        return carry

    lax.fori_loop(0, n_rows, start, 0, unroll=8)
    lax.fori_loop(0, n_rows, wait, 0, unroll=8)


def _dispatch(pos, x, n_sorted):
    n, d = x.shape
    xs0 = jnp.zeros((n_sorted, d), x.dtype)
    return pl.pallas_call(
        _dispatch_kernel,
        grid=(n // ROW_TILE,),
        in_specs=[pl.BlockSpec((1, 1, 2 * ROW_TILE), lambda i: (i, 0, 0), memory_space=pltpu.SMEM),
                  pl.BlockSpec((ROW_TILE, d), lambda i: (i, 0)),
                  pl.BlockSpec(memory_space=pl.ANY)],
        out_specs=pl.BlockSpec(memory_space=pl.ANY),
        out_shape=jax.ShapeDtypeStruct((n_sorted, d), x.dtype),
        scratch_shapes=[pltpu.SemaphoreType.DMA(())],
        input_output_aliases={2: 0},
        compiler_params=_cparams(),
        name="moe_dispatch",
    )(pos, x, xs0)


def _experts_kernel(n_chunks, te_ref, tv_ref, xs_ref, wg_ref, wu_ref, wd_ref, ys_ref, acc_ref):
    i = pl.program_id(0)

    @pl.when(tv_ref[i] > 0)
    def _():
        x = xs_ref[...].astype(BF16)
        acc_ref[...] = jnp.zeros_like(acc_ref)

        def body(c, carry):
            g = _dot(x, wg_ref[0, c])
            u = _dot(x, wu_ref[0, c])
            a = (g * _sigmoid(g) * u).astype(BF16)
            acc_ref[...] += _dot(a, wd_ref[0, c])
            return carry

        lax.fori_loop(0, n_chunks, body, 0, unroll=2)
        ys_ref[...] = acc_ref[...]

    @pl.when(tv_ref[i] == 0)
    def _():
        ys_ref[...] = jnp.zeros_like(ys_ref)


def _experts(tile_expert, tile_valid, xs, wg, wu, wd):
    n_sorted, d = xs.shape
    n_chunks = wg.shape[1]
    wspec = lambda shape: pl.BlockSpec((1,) + shape[1:], lambda i, te, tv: (te[i], 0, 0, 0),
                                       pipeline_mode=pl.Buffered(1))
    row = pl.BlockSpec((ROW_TILE, d), lambda i, te, tv: (i, 0))
    return pl.pallas_call(
        functools.partial(_experts_kernel, n_chunks),
        grid_spec=pltpu.PrefetchScalarGridSpec(
            num_scalar_prefetch=2,
            grid=(n_sorted // ROW_TILE,),
            in_specs=[row, wspec(wg.shape), wspec(wu.shape), wspec(wd.shape)],
            out_specs=row,
            scratch_shapes=[pltpu.VMEM((ROW_TILE, d), F32)]),
        out_shape=jax.ShapeDtypeStruct((n_sorted, d), F32),
        compiler_params=_cparams(),
        name="moe_experts",
    )(tile_expert, tile_valid, xs, wg, wu, wd)


def _combine_norm_kernel(tp, pos_ref, h_ref, info_ref, ys_ref, g_ref, o_ref,
                         h_buf, info_buf, y_buf, sem, row_sem):
    n_rows = h_buf.shape[0]
    row0 = pl.multiple_of(pl.program_id(0) * tp + Q_BLOCK + pl.program_id(1) * n_rows, Q_BLOCK)
    h_copy = pltpu.make_async_copy(h_ref.at[pl.ds(row0, n_rows), :], h_buf, sem.at[0])
    info_copy = pltpu.make_async_copy(info_ref.at[pl.ds(row0, n_rows), :], info_buf, sem.at[1])
    h_copy.start()
    info_copy.start()

    def row_copy(r, k):
        p = pos_ref[0, 0, 2 * r + k]
        return pltpu.make_async_copy(ys_ref.at[pl.ds(p, 1), :], y_buf.at[k, pl.ds(r, 1), :], row_sem)

    def start(r, carry):
        row_copy(r, 0).start()
        row_copy(r, 1).start()
        return carry

    def wait(r, carry):
        row_copy(r, 0).wait()
        row_copy(r, 1).wait()
        return carry

    lax.fori_loop(0, n_rows, start, 0, unroll=8)
    lax.fori_loop(0, n_rows, wait, 0, unroll=8)
    h_copy.wait()
    info_copy.wait()
    info = info_buf[...]
    x = h_buf[...] + info[:, 2:3] * y_buf[0] + info[:, 3:4] * y_buf[1]
    ms = jnp.mean(x * x, axis=-1, keepdims=True)
    o_ref[0] = x * lax.rsqrt(ms + EPS) * g_ref[...]


def _combine_norm(pos, h, info, ys, g, b, tp):
    n, d = h.shape
    seq = tp - Q_BLOCK
    tiles = seq // ROW_TILE
    any_spec = pl.BlockSpec(memory_space=pl.ANY)
    return pl.pallas_call(
        functools.partial(_combine_norm_kernel, tp),
        grid=(b, tiles),
        in_specs=[pl.BlockSpec((1, 1, 2 * ROW_TILE), lambda i, j: (i * tiles + j, 0, 0),
                               memory_space=pltpu.SMEM),
                  any_spec, any_spec, any_spec, _resident((1, d))],
        out_specs=pl.BlockSpec((1, ROW_TILE, d), lambda i, j: (i, j, 0)),
        out_shape=jax.ShapeDtypeStruct((b, seq, d), F32),
        scratch_shapes=[pltpu.VMEM((ROW_TILE, d), F32), pltpu.VMEM((ROW_TILE, LANES), F32),
                        pltpu.VMEM((2, ROW_TILE, d), F32),
                        pltpu.SemaphoreType.DMA((2,)), pltpu.SemaphoreType.DMA(())],
        compiler_params=_cparams(2),
        name="moe_combine_norm",
    )(pos, h, info, ys, g)


def _routed_moe_norm(h, hn, info, cnt, wg, wu, wd, g, b, tp):
    n = h.shape[0]
    n_sorted = 2 * n + N_EXPERTS * ROW_TILE
    n_tiles = n_sorted // ROW_TILE
    counts = cnt[0, :N_EXPERTS].astype(jnp.int32)
    tiles_e = (counts + ROW_TILE - 1) // ROW_TILE
    tile_end = jnp.cumsum(tiles_e)
    row_base = (tile_end - tiles_e) * ROW_TILE
    e_idx = info[:, 0:2].astype(jnp.int32)
    pos = row_base[e_idx] + info[:, 4:6].astype(jnp.int32)
    tile_ids = jnp.arange(n_tiles, dtype=jnp.int32)
    tile_expert = jnp.minimum(jnp.sum((tile_ids[:, None] >= tile_end[None, :]).astype(jnp.int32), axis=1),
                              N_EXPERTS - 1)
    tile_valid = (tile_ids < tile_end[-1]).astype(jnp.int32)
    xs = _dispatch(pos.reshape(n // ROW_TILE, 1, 2 * ROW_TILE), hn, n_sorted)
    ys = _experts(tile_expert, tile_valid, xs, wg, wu, wd)
    pos_seq = pos.reshape(b, tp, 2)[:, Q_BLOCK:, :].reshape(-1, 1, 2 * ROW_TILE)
    return _combine_norm(pos_seq, h, info, ys, g, b, tp)


def _swiglu_kernel(expert, n_chunks, h_ref, hn_ref, *rest):
    if expert is None:
        wg_ref, wu_ref, wd_ref, o_ref, acc_ref = rest
    else:
        gate_ref, wg_ref, wu_ref, wd_ref, o_ref, acc_ref = rest
    hn = hn_ref[...]
    acc_ref[...] = jnp.zeros_like(acc_ref)

    def body(c, carry):
        g = _dot(hn, wg_ref[c])
        u = _dot(hn, wu_ref[c])
        a = (g * _sigmoid(g) * u).astype(BF16)
        acc_ref[...] += _dot(a, wd_ref[c])
        return carry

    lax.fori_loop(0, n_chunks, body, 0, unroll=2)
    y = acc_ref[...]
    if expert is not None:
        y = gate_ref[:, expert:expert + 1] * y
    o_ref[...] = h_ref[...] + y


def _swiglu(h, hn, wg, wu, wd, gates=None, expert=None):
    n = h.shape[0]
    n_chunks = wg.shape[0]
    row = lambda width: pl.BlockSpec((ROW_TILE, width), lambda i: (i, 0))
    in_specs = [row(D_MODEL), row(D_MODEL)]
    args = [h, hn]
    if expert is not None:
        in_specs.append(row(LANES))
        args.append(gates)
    in_specs += [_resident(wg.shape), _resident(wu.shape), _resident(wd.shape)]
    args += [wg, wu, wd]
    return pl.pallas_call(
        functools.partial(_swiglu_kernel, expert, n_chunks),
        grid=(n // ROW_TILE,),
        in_specs=in_specs,
        out_specs=row(D_MODEL),
        out_shape=jax.ShapeDtypeStruct((n, D_MODEL), F32),
        scratch_shapes=[pltpu.VMEM((ROW_TILE, D_MODEL), F32)],
        input_output_aliases={0: 0},
        compiler_params=_cparams(),
        name="swiglu" if expert is None else f"moe_expert{expert}",
    )(*args)


def _group_ones(n, group_shift):
    r = lax.broadcasted_iota(jnp.int32, (n, n), 0) >> group_shift
    c = lax.broadcasted_iota(jnp.int32, (n, n), 1) >> group_shift
    return jnp.where(r == c, 1.0, 0.0).astype(BF16)


GLA_ROWS = 256


def _gla_kernel(tp, pa_ref, wg_ref, bg_ref, gn_ref, o_ref, s_ref):
    qhead = lax.broadcasted_iota(jnp.int32, (1, N_HEADS * GLA_DK), 1) >> 5
    ehead = lax.broadcasted_iota(jnp.int32, (1, N_HEADS * GLA_DV), 1) >> 6
    shead = lax.broadcasted_iota(jnp.int32, (N_HEADS * GLA_DK, 1), 0) >> 5
    bd = shead == ehead
    gsum = _group_ones(N_HEADS * GLA_DV, 6)
    wg_hi, wg_lo = _split_bf16(wg_ref[...])
    bg = bg_ref[...]
    gn = gn_ref[...]
    scale = GLA_DK ** -0.5
    s_ref[...] = jnp.zeros_like(s_ref)

    def block(r0, n_rows):
        n_c = n_rows // CHUNK
        rows = pl.ds(r0, n_rows)
        ri = lax.broadcasted_iota(jnp.int32, (n_rows, n_rows), 0)
        ci = lax.broadcasted_iota(jnp.int32, (n_rows, n_rows), 1)
        same = (ri >> 6) == (ci >> 6)
        tri_bf = jnp.where(jnp.logical_and(same, ri >= ci), 1.0, 0.0).astype(BF16)
        ones_bf = jnp.where(same, 1.0, 0.0).astype(BF16)
        r4 = lax.broadcasted_iota(jnp.int32, (N_HEADS * n_rows, n_rows), 0) & (n_rows - 1)
        c4 = lax.broadcasted_iota(jnp.int32, (N_HEADS * n_rows, n_rows), 1)
        tri4 = jnp.logical_and((r4 >> 6) == (c4 >> 6), r4 >= c4)
        col_chunk = lax.broadcasted_iota(jnp.int32, (1, n_rows), 1) >> 6

        q = pa_ref[0, rows, 0:128].astype(F32) * scale
        k = pa_ref[0, rows, 128:256].astype(F32)
        v = pa_ref[0, rows, 256:512]
        og = pa_ref[0, rows, 512:768].astype(F32)
        lr = pa_ref[0, rows, 768:896]
        valid = (r0 + lax.broadcasted_iota(jnp.int32, (n_rows, 1), 0)) >= FRONT_PAD

        pre = _dot(lr, wg_hi) + _dot(lr, wg_lo) + bg
        logsig = jnp.minimum(pre, 0.0) - jnp.log1p(jnp.exp(-jnp.abs(pre)))
        g = jnp.where(valid, logsig * (1.0 / GLA_TAU), 0.0)
        g_hi, g_lo = _split_bf16(g)
        cum = _dot(tri_bf, g_hi) + _dot(tri_bf, g_lo)
        cum_end = _dot(ones_bf, g_hi) + _dot(ones_bf, g_lo)
        qt_bf = (q * jnp.exp(cum)).astype(BF16)
        kt = (k * jnp.exp(-cum)).astype(BF16)
        kd_t = (k * jnp.exp(cum_end - cum)).T.astype(BF16)
        dec_t = jnp.exp(cum_end.T)

        qs = jnp.concatenate([jnp.where(qhead == h, qt_bf, jnp.zeros_like(qt_bf))
                              for h in range(N_HEADS)], axis=0)
        a = jnp.where(tri4, _dot_nt(qs, kt), 0.0).astype(BF16)
        r = _dot(a, v)
        o = jnp.where(ehead == 0, r[0:n_rows, :], 0.0)
        for h in range(1, N_HEADS):
            o = o + jnp.where(ehead == h, r[h * n_rows:(h + 1) * n_rows, :], 0.0)

        s = s_ref[...]
        inter = []
        for c in range(n_c):
            inter.append(_dot(qt_bf[c * CHUNK:(c + 1) * CHUNK, :], s.astype(BF16)))
            upd = _dot(jnp.where(col_chunk == c, kd_t, jnp.zeros_like(kd_t)), v)
            s = dec_t[:, c * CHUNK:c * CHUNK + 1] * s + jnp.where(bd, upd, 0.0)
        s_ref[...] = s
        o = o + jnp.concatenate(inter, axis=0)

        ms = _dot((o * o).astype(BF16), gsum) * (1.0 / GLA_DV)
        y = o * lax.rsqrt(ms + EPS) * gn * (og * _sigmoid(og))
        o_ref[0, rows, :] = jnp.where(valid, y, 0.0).astype(BF16)

    block(0, Q_BLOCK)

    def body(i, carry):
        block(pl.multiple_of(Q_BLOCK + i * GLA_ROWS, Q_BLOCK), GLA_ROWS)
        return carry

    lax.fori_loop(0, (tp - Q_BLOCK) // GLA_ROWS, body, 0)


def _gla(pa, wg, bg, gn):
    b, tp, _ = pa.shape
    return pl.pallas_call(
        functools.partial(_gla_kernel, tp),
        grid=(b,),
        in_specs=[pl.BlockSpec((1, tp, PA_W), lambda i: (i, 0, 0)),
                  _resident((LANES, LANES)), _resident((1, LANES)), _resident((1, 256))],
        out_specs=pl.BlockSpec((1, tp, 256), lambda i: (i, 0, 0)),
        out_shape=jax.ShapeDtypeStruct((b, tp, 256), BF16),
        scratch_shapes=[pltpu.VMEM((N_HEADS * GLA_DK, N_HEADS * GLA_DV), F32)],
        compiler_params=_cparams(),
        name="gla",
    )(pa, wg, bg, gn)


RET_BLOCK = 128


def _ret_kernel(tp, pb_ref, cos_ref, sin_ref, dmask_ref, qfac_ref, kfac_ref, dec_ref, gn_ref,
                o_ref, s_ref):
    blk = RET_BLOCK
    n_blocks = tp // blk
    qhead = (lax.broadcasted_iota(jnp.int32, (1, 256), 1) & 127) >> 5
    ehead = lax.broadcasted_iota(jnp.int32, (1, 256), 1) >> 6
    shead = (lax.broadcasted_iota(jnp.int32, (256, 1), 0) & 127) >> 5
    bd = shead == ehead
    gsum = _group_ones(256, 6)
    gn = gn_ref[...]
    s_ref[...] = jnp.zeros_like(s_ref)

    def rope(x, cos, sin):
        x1, x2 = x[:, :128], x[:, 128:]
        return jnp.concatenate([x1 * cos - x2 * sin, x1 * sin + x2 * cos], axis=1)

    def block(j, carry):
        r0 = pl.multiple_of(j * blk, blk)
        rows = pl.ds(r0, blk)
        cos = cos_ref[rows, :]
        sin = sin_ref[rows, :]
        q = rope(pb_ref[0, rows, 0:256].astype(F32), cos, sin)
        k = rope(pb_ref[0, rows, 256:512].astype(F32), cos, sin) * (RET_DK ** -0.5)
        v = pb_ref[0, rows, 512:768]
        og = pb_ref[0, rows, 768:1024].astype(F32)
        valid = (r0 + lax.broadcasted_iota(jnp.int32, (blk, 1), 0)) >= FRONT_PAD

        q_bf = q.astype(BF16)
        qs = jnp.concatenate([jnp.where(qhead == h, q_bf, jnp.zeros_like(q_bf))
                              for h in range(N_HEADS)], axis=0)
        a = (_dot_nt(qs, k.astype(BF16)) * dmask_ref[...]).astype(BF16)
        r = _dot(a, v)
        o = _dot((q * qfac_ref[...]).astype(BF16), s_ref[...].astype(BF16))
        for h in range(N_HEADS):
            o = o + jnp.where(ehead == h, r[h * blk:(h + 1) * blk, :], 0.0)

        kd_t = (k * kfac_ref[...]).T.astype(BF16)
        upd = _dot(kd_t, v)
        s_ref[...] = dec_ref[...] * s_ref[...] + jnp.where(bd, upd, 0.0)

        mu = _dot(o.astype(BF16), gsum) * (1.0 / 64)
        xc = o - mu
        var = _dot((xc * xc).astype(BF16), gsum) * (1.0 / 64)
        y = xc * lax.rsqrt(var + EPS) * gn * (og * _sigmoid(og))
        o_ref[0, rows, :] = jnp.where(valid, y, 0.0).astype(BF16)
        return carry

    lax.fori_loop(0, n_blocks, block, 0, unroll=2)


def _ret(pb, cos, sin, dmask, qfac, kfac, dec, gn):
    b, tp, _ = pb.shape
    return pl.pallas_call(
        functools.partial(_ret_kernel, tp),
        grid=(b,),
        in_specs=[pl.BlockSpec((1, tp, PB_W), lambda i: (i, 0, 0)),
                  _resident(cos.shape), _resident(sin.shape), _resident(dmask.shape),
                  _resident(qfac.shape), _resident(kfac.shape), _resident(dec.shape),
                  _resident((1, 256))],
        out_specs=pl.BlockSpec((1, tp, 256), lambda i: (i, 0, 0)),
        out_shape=jax.ShapeDtypeStruct((b, tp, 256), BF16),
        scratch_shapes=[pltpu.VMEM((256, 256), F32)],
        compiler_params=_cparams(),
        name="retention",
    )(pb, cos, sin, dmask, qfac, kfac, dec, gn)


ATT_ROWS = 256
LOG2E = 1.4426950408889634
V_ONE = 64


def _for_pairs(n, body):
    def pair(t, carry):
        body(2 * t)
        body(2 * t + 1)
        return carry

    lax.fori_loop(0, n >> 1, pair, 0)

    @pl.when((n & 1) == 1)
    def _():
        body(n - 1)


def _mask_groups(mask, s, n_g):
    rows = s.shape[0] // n_g
    return jnp.concatenate([jnp.where(mask, s[g * rows:(g + 1) * rows, :], NEG)
                            for g in range(n_g)], axis=0)


def _attn_block0(qk_fn, v_fn, n_g):
    rows = pl.ds(0, Q_BLOCK)
    qrow = lax.broadcasted_iota(jnp.int32, (Q_BLOCK, 1), 0)
    kcol = lax.broadcasted_iota(jnp.int32, (1, Q_BLOCK), 1)
    mask = jnp.logical_and(kcol <= qrow, kcol >= FRONT_PAD)
    s = _mask_groups(mask, qk_fn(0, Q_BLOCK, rows), n_g)
    p_bf = jnp.exp2(s - jnp.max(s, axis=-1, keepdims=True)).astype(BF16)
    pv = jnp.concatenate([_dot(p_bf[g * Q_BLOCK:(g + 1) * Q_BLOCK, :], v_fn(g, rows))
                          for g in range(n_g)], axis=0)
    return pv / pv[:, V_ONE:V_ONE + 1]


def _attn_block(i, qk_fn, v_fn, n_g, s_meta, s_s, mx_s, acc_s):
    rr = ATT_ROWS
    q0 = pl.multiple_of(Q_BLOCK + (i - 1) * rr, Q_BLOCK)
    meta_rows = pl.ds(0, Q_BLOCK)
    kcol = lax.broadcasted_iota(jnp.int32, (1, Q_BLOCK), 1)

    def key_rows(j):
        return pl.ds(pl.multiple_of(Q_BLOCK + j * rr, Q_BLOCK), rr)

    s = jnp.where(kcol >= FRONT_PAD, qk_fn(q0, rr, meta_rows), NEG)
    s_meta[...] = s
    mx_s[...] = s

    def pass1(j):
        sj = qk_fn(q0, rr, key_rows(j))
        s_s[j] = sj
        mx_s[...] = jnp.maximum(mx_s[...], jnp.maximum(sj[:, :LANES], sj[:, LANES:]))

    _for_pairs(i - 1, pass1)
    causal = (lax.broadcasted_iota(jnp.int32, (rr, rr), 1)
              <= lax.broadcasted_iota(jnp.int32, (rr, rr), 0))
    sd = _mask_groups(causal, qk_fn(q0, rr, pl.ds(q0, rr)), n_g)
    s_s[i - 1] = sd
    m = jnp.max(jnp.maximum(mx_s[...], jnp.maximum(sd[:, :LANES], sd[:, LANES:])),
                axis=-1, keepdims=True)
    mx_s[...] = jnp.broadcast_to(m, mx_s.shape)

    p_bf = jnp.exp2(s_meta[...] - mx_s[...]).astype(BF16)
    for g in range(n_g):
        acc_s[g * rr:(g + 1) * rr, :] = _dot(p_bf[g * rr:(g + 1) * rr, :], v_fn(g, meta_rows))

    def pass2(j):
        sj = s_s[j]
        mrep = mx_s[...]
        p0 = jnp.exp2(sj[:, :LANES] - mrep)
        p1 = jnp.exp2(sj[:, LANES:] - mrep)
        pj = jnp.concatenate([p0.astype(BF16), p1.astype(BF16)], axis=1)
        for g in range(n_g):
            acc_s[g * rr:(g + 1) * rr, :] += _dot(pj[g * rr:(g + 1) * rr, :], v_fn(g, key_rows(j)))

    _for_pairs(i, pass2)
    acc = acc_s[...]
    return acc / acc[:, V_ONE:V_ONE + 1]


def _heads_to_lanes(per_head):
    low = lax.broadcasted_iota(jnp.int32, (1, LANES), 1) < 64
    lo = jnp.where(low, per_head[0], pltpu.roll(per_head[1], 64, 1))
    hi = jnp.where(low, per_head[2], pltpu.roll(per_head[3], 64, 1))
    return jnp.concatenate([lo, hi], axis=1)


def _with_ones_lane(v):
    lane = lax.broadcasted_iota(jnp.int32, (1, LANES), 1)
    return jnp.where(lane == V_ONE, jnp.ones_like(v), v)


def _mla_kernel(tp, pc_ref, qn_ref, kvn_ref, wuq_ref, wukv_ref, cos_ref, sa_ref, sb_ref,
                o_ref, q_s, k_s, v_s, s_meta, s_s, mx_s, acc_s):
    n_blocks = tp // Q_BLOCK
    scale = (MLA_NOPE + MLA_ROPE) ** -0.5 * LOG2E
    is_q = lax.broadcasted_iota(jnp.int32, (1, 256), 1) < MLA_Q_RANK

    def prep(i, carry):
        r0 = pl.multiple_of(i * Q_BLOCK, Q_BLOCK)
        rows = pl.ds(r0, Q_BLOCK)
        x = pc_ref[0, rows, 0:256].astype(F32)
        x2 = x * x
        ms_q = jnp.sum(jnp.where(is_q, x2, 0.0), axis=-1, keepdims=True) * (1.0 / MLA_Q_RANK)
        ms_kv = jnp.sum(jnp.where(is_q, 0.0, x2), axis=-1, keepdims=True) * (1.0 / MLA_KV_RANK)
        yq = (x * lax.rsqrt(ms_q + EPS) * qn_ref[...]).astype(BF16)
        ykv = (x * lax.rsqrt(ms_kv + EPS) * kvn_ref[...]).astype(BF16)
        cq = _dot(yq, wuq_ref[...])
        kv = _dot(ykv, wukv_ref[...])
        cos = cos_ref[rows, :]
        sa = sa_ref[rows, :]
        sb = sb_ref[rows, :]

        def rope(t):
            return t * cos + pltpu.roll(t, 16, 1) * sa + pltpu.roll(t, LANES - 16, 1) * sb

        kpe = rope(pc_ref[0, rows, 256:384].astype(F32))
        for h in range(N_HEADS):
            q_s[h, rows, :] = (rope(cq[:, h * LANES:(h + 1) * LANES]) * scale).astype(BF16)
            k_s[h, rows, :] = (kv[:, h * LANES:(h + 1) * LANES] + kpe).astype(BF16)
            v_s[h, rows, :] = _with_ones_lane(
                kv[:, (N_HEADS + h) * LANES:(N_HEADS + h + 1) * LANES]).astype(BF16)
        return carry

    lax.fori_loop(0, n_blocks, prep, 0)

    def qk_fn(q0, n_rows, krows):
        return jnp.concatenate([_dot_nt(q_s[h, pl.ds(q0, n_rows), :], k_s[h, krows, :])
                                for h in range(N_HEADS)], axis=0)

    def v_fn(h, krows):
        return v_s[h, krows, :]

    def emit(q0, n_rows, o, first):
        y = _heads_to_lanes([o[h * n_rows:(h + 1) * n_rows, :] for h in range(N_HEADS)])
        if first:
            qrow = lax.broadcasted_iota(jnp.int32, (n_rows, 1), 0)
            y = jnp.where(qrow >= FRONT_PAD, y, 0.0)
        o_ref[0, pl.ds(q0, n_rows), :] = y.astype(BF16)

    emit(0, Q_BLOCK, _attn_block0(qk_fn, v_fn, N_HEADS), True)

    def qblock(i, carry):
        o = _attn_block(i, qk_fn, v_fn, N_HEADS, s_meta, s_s, mx_s, acc_s)
        emit(pl.multiple_of(Q_BLOCK + (i - 1) * ATT_ROWS, Q_BLOCK), ATT_ROWS, o, False)
        return carry

    lax.fori_loop(1, (tp - Q_BLOCK) // ATT_ROWS + 1, qblock, 0)


def _attn_scratch(n_g, tp):
    g_rows = n_g * ATT_ROWS
    n_slots = (tp - Q_BLOCK) // ATT_ROWS
    return [pltpu.VMEM((g_rows, LANES), F32),
            pltpu.VMEM((n_slots, g_rows, ATT_ROWS), F32),
            pltpu.VMEM((g_rows, LANES), F32),
            pltpu.VMEM((g_rows, LANES), F32)]


def _mla(pc, qn, kvn, wuq, wukv, cos, sa, sb):
    b, tp, _ = pc.shape
    return pl.pallas_call(
        functools.partial(_mla_kernel, tp),
        grid=(b,),
        in_specs=[pl.BlockSpec((1, tp, PC_W), lambda i: (i, 0, 0)),
                  _resident((1, 256)), _resident((1, 256)),
                  _resident(wuq.shape), _resident(wukv.shape),
                  _resident(cos.shape), _resident(sa.shape), _resident(sb.shape)],
        out_specs=pl.BlockSpec((1, tp, 256), lambda i: (i, 0, 0)),
        out_shape=jax.ShapeDtypeStruct((b, tp, 256), BF16),
        scratch_shapes=[pltpu.VMEM((N_HEADS, tp, LANES), BF16),
                        pltpu.VMEM((N_HEADS, tp, LANES), BF16),
                        pltpu.VMEM((N_HEADS, tp, LANES), BF16)] + _attn_scratch(N_HEADS, tp),
        compiler_params=_cparams(),
        name="mla",
    )(pc, qn, kvn, wuq, wukv, cos, sa, sb)


def _diff_kernel(tp, lam_init, pd_ref, lam_ref, dn_ref, o_ref, qs_s, v_s, s_meta, s_s, mx_s, acc_s):
    n_maps = 2 * N_HEADS

    def fill_values(i, carry):
        rows = pl.ds(pl.multiple_of(i * Q_BLOCK, Q_BLOCK), Q_BLOCK)
        for h in range(N_HEADS):
            v_s[h, rows, :] = _with_ones_lane(pd_ref[0, rows, 512 + h * LANES:512 + (h + 1) * LANES])
        return carry

    lax.fori_loop(0, tp // Q_BLOCK, fill_values, 0)
    scale = DIFF_DK ** -0.5 * LOG2E
    group = lax.broadcasted_iota(jnp.int32, (1, 256), 1) >> 5
    lv = lam_ref[...]
    lam = (jnp.exp(jnp.sum(lv[0:1, :] * lv[1:2, :], axis=-1, keepdims=True))
           - jnp.exp(jnp.sum(lv[2:3, :] * lv[3:4, :], axis=-1, keepdims=True)) + lam_init)
    dn = dn_ref[...]

    def stack_queries(q0, n_rows):
        q = (pd_ref[0, pl.ds(q0, n_rows), 0:256].astype(F32) * scale).astype(BF16)
        for g in range(n_maps):
            qs_s[g * n_rows:(g + 1) * n_rows, :] = jnp.where(group == g, q, jnp.zeros_like(q))

    def qk_fn(q0, n_rows, krows):
        return _dot_nt(qs_s[0:n_maps * n_rows, :], pd_ref[0, krows, 256:512])

    def v_fn(g, krows):
        return v_s[g // 2, krows, :]

    def emit(q0, n_rows, o, first):
        od = jnp.concatenate([o[(2 * h) * n_rows:(2 * h + 1) * n_rows, :]
                              - lam * o[(2 * h + 1) * n_rows:(2 * h + 2) * n_rows, :]
                              for h in range(N_HEADS)], axis=0)
        od = jnp.where(lax.broadcasted_iota(jnp.int32, (1, LANES), 1) < V_ONE, od, 0.0)
        ms = jnp.sum(od * od, axis=-1, keepdims=True) * (1.0 / 64)
        yh = od * lax.rsqrt(ms + EPS) * dn * (1.0 - lam_init)
        y = _heads_to_lanes([yh[h * n_rows:(h + 1) * n_rows, :] for h in range(N_HEADS)])
        if first:
            qrow = lax.broadcasted_iota(jnp.int32, (n_rows, 1), 0)
            y = jnp.where(qrow >= FRONT_PAD, y, 0.0)
        o_ref[0, pl.ds(q0, n_rows), :] = y.astype(BF16)

    stack_queries(0, Q_BLOCK)
    emit(0, Q_BLOCK, _attn_block0(qk_fn, v_fn, n_maps), True)

    def qblock(i, carry):
        q0 = pl.multiple_of(Q_BLOCK + (i - 1) * ATT_ROWS, Q_BLOCK)
        stack_queries(q0, ATT_ROWS)
        o = _attn_block(i, qk_fn, v_fn, n_maps, s_meta, s_s, mx_s, acc_s)
        emit(q0, ATT_ROWS, o, False)
        return carry

    lax.fori_loop(1, (tp - Q_BLOCK) // ATT_ROWS + 1, qblock, 0)


def _diff(pd, lam_rows, dn, lam_init):
    b, tp, _ = pd.shape
    n_maps = 2 * N_HEADS
    return pl.pallas_call(
        functools.partial(_diff_kernel, tp, lam_init),
        grid=(b,),
        in_specs=[pl.BlockSpec((1, tp, PD_W), lambda i: (i, 0, 0)),
                  _resident(lam_rows.shape), _resident((1, LANES))],
        out_specs=pl.BlockSpec((1, tp, 256), lambda i: (i, 0, 0)),
        out_shape=jax.ShapeDtypeStruct((b, tp, 256), BF16),
        scratch_shapes=[pltpu.VMEM((n_maps * ATT_ROWS, 256), BF16),
                        pltpu.VMEM((N_HEADS, tp, LANES), BF16)] + _attn_scratch(n_maps, tp),
        compiler_params=_cparams(),
        name="diffattn",
    )(pd, lam_rows, dn)


def _pad_cols(x, width):
    return jnp.pad(x, ((0, 0), (0, width - x.shape[1])))


def _rot_split(w):
    d = w.shape[0]
    return w.reshape(d, N_HEADS, 2, 32).transpose(0, 2, 1, 3).reshape(d, 256)


def _layout_w_in(w):
    sizes = (128, 128, 256, 16, 256, 256, 256, 256, 256, 192, 64, 32, 256, 256, 256)
    offs = [0]
    for s_ in sizes:
        offs.append(offs[-1] + s_)
    seg = [w[:, offs[i]:offs[i + 1]] for i in range(len(sizes))]
    (a_q, a_k, a_v, a_lr, a_og, r_q, r_k, r_v, r_og, c_cq, c_ckv, c_kpe, d_q, d_k, d_v) = seg
    d = w.shape[0]
    z = lambda n: jnp.zeros((d, n), w.dtype)
    dv_p = jnp.pad(d_v.reshape(d, N_HEADS, 64), ((0, 0), (0, 0), (0, 64))).reshape(d, 512)
    cols = [a_q, a_k, a_v, a_og, a_lr, z(112),
            _rot_split(r_q), _rot_split(r_k), r_v, r_og,
            c_cq, c_ckv, z(64), c_kpe, z(32),
            d_q, d_k, dv_p]
    return jnp.concatenate(cols, axis=1).astype(BF16)


def _tables(tp):
    pos = jnp.arange(tp, dtype=F32) - FRONT_PAD
    inv = ROPE_THETA ** (-jnp.arange(32, dtype=F32) / 32)
    ang = pos[:, None] * inv[None, :]
    ret_cos = jnp.tile(jnp.cos(ang), (1, N_HEADS))
    ret_sin = jnp.tile(jnp.sin(ang), (1, N_HEADS))
    inv16 = ROPE_THETA ** (-jnp.arange(16, dtype=F32) / 16)
    ang16 = pos[:, None] * inv16[None, :]
    c16, s16 = jnp.cos(ang16), jnp.sin(ang16)
    one = lambda n: jnp.ones((tp, n), F32)
    zero = lambda n: jnp.zeros((tp, n), F32)
    mla_cos = jnp.concatenate([one(64), c16, c16, one(32)], axis=1)
    mla_sa = jnp.concatenate([zero(80), s16, zero(32)], axis=1)
    mla_sb = jnp.concatenate([zero(64), -s16, zero(48)], axis=1)
    lg = jnp.log(1.0 - jnp.exp2(-5.0 - jnp.arange(N_HEADS, dtype=F32)))
    idx = jnp.arange(RET_BLOCK, dtype=F32)
    rel = idx[:, None] - idx[None, :]
    dmask = jnp.where(rel[None] >= 0, jnp.exp(rel[None] * lg[:, None, None]), 0.0)
    dmask = dmask.reshape(N_HEADS * RET_BLOCK, RET_BLOCK)
    lane_head = (jnp.arange(256) % 128) // 32
    qfac = jnp.exp((idx[:, None] + 1.0) * lg[lane_head][None, :])
    kfac = jnp.exp((RET_BLOCK - 1.0 - idx[:, None]) * lg[lane_head][None, :])
    dec = jnp.exp(RET_BLOCK * lg[lane_head])[:, None]
    return ret_cos, ret_sin, mla_cos, mla_sa, mla_sb, dmask, qfac, kfac, dec


def _chunk_cols(w):
    d, f = w.shape
    return w.reshape(d, f // FF_CHUNK, FF_CHUNK).transpose(1, 0, 2).astype(BF16)


def _chunk_rows(w):
    f, d = w.shape
    return w.reshape(f // FF_CHUNK, FF_CHUNK, d).astype(BF16)


def kernel(x, meta_tokens, attn_norm, w_in, gla_w_gate, gla_b_gate, gla_norm, ret_norm, mla_q_norm, mla_w_uq, mla_kv_norm, mla_w_ukv, diff_lambda, diff_norm, w_out, ffn_norm, ffn_w_gate, ffn_w_up, ffn_w_down, moe_router, moe_w_gate, moe_w_up, moe_w_down, final_norm):
    b, seq, d = x.shape
    tp = FRONT_PAD + N_META + seq
    n = b * tp
    meta = jnp.broadcast_to(meta_tokens[None].astype(x.dtype), (b, N_META, d))
    h = jnp.concatenate([jnp.zeros((b, FRONT_PAD, d), x.dtype), meta, x], axis=1).reshape(n, d)
    ret_cos, ret_sin, mla_cos, mla_sa, mla_sb, dmask, qfac, kfac, dec = _tables(tp)

    for li in range(DEPTH):
        pa, pb, pc, pd = _inproj(h, attn_norm[li][None, :], _layout_w_in(w_in[li]))
        pa, pb, pc, pd = (p.reshape(b, tp, -1) for p in (pa, pb, pc, pd))

        wgate = jnp.pad(gla_w_gate[li], ((0, LANES - GLA_GATE_RANK), (0, 0)))
        o_a = _gla(pa, wgate, gla_b_gate[li][None, :], jnp.tile(gla_norm[li], N_HEADS)[None, :])
        o_b = _ret(pb, ret_cos, ret_sin, dmask, qfac, kfac, dec, jnp.tile(ret_norm[li], N_HEADS)[None, :])

        qn = _pad_cols(mla_q_norm[li][None, :], 256)
        kvn = jnp.pad(mla_kv_norm[li][None, :], ((0, 0), (MLA_Q_RANK, 0)))
        wuq = jnp.pad(mla_w_uq[li].reshape(MLA_Q_RANK, N_HEADS, MLA_NOPE + MLA_ROPE),
                      ((0, 256 - MLA_Q_RANK), (0, 0), (0, LANES - MLA_NOPE - MLA_ROPE)))
        wuq = wuq.reshape(256, N_HEADS * LANES).astype(BF16)
        wukv = mla_w_ukv[li].reshape(MLA_KV_RANK, N_HEADS, 2, 64)
        wukv = jnp.pad(wukv, ((MLA_Q_RANK, 0), (0, 0), (0, 0), (0, 64)))
        wukv = wukv.transpose(0, 2, 1, 3).reshape(256, 2 * N_HEADS * LANES).astype(BF16)
        o_c = _mla(pc, qn, kvn, wuq, wukv, mla_cos, mla_sa, mla_sb)

        lam_init = 0.8 - 0.6 * math.exp(-0.3 * li)
        o_d = _diff(pd, diff_lambda[li], _pad_cols(diff_norm[li][None, :], LANES), lam_init)

        o_a, o_b, o_c, o_d = (o.reshape(n, 256) for o in (o_a, o_b, o_c, o_d))
        wo = w_out[li].astype(BF16)
        fn = ffn_norm[li][None, :]
        j = li // 2
        if li % 2 == 0:
            h, hn = _outproj(h, o_a, o_b, o_c, o_d, wo, fn)
            h = _swiglu(h, hn, _chunk_cols(ffn_w_gate[j]), _chunk_cols(ffn_w_up[j]),
                        _chunk_rows(ffn_w_down[j]))
        else:
            h, hn, info, cnt = _outproj(h, o_a, o_b, o_c, o_d, wo, fn,
                                        router=_pad_cols(moe_router[j], LANES))
            return _routed_moe_norm(h, hn, info, cnt, jax.vmap(_chunk_cols)(moe_w_gate[j]),
                                    jax.vmap(_chunk_cols)(moe_w_up[j]),
                                    jax.vmap(_chunk_rows)(moe_w_down[j]), final_norm[None, :], b, tp)
```

```python
import functools
import math

import jax
import jax.numpy as jnp
from jax import lax
from jax.experimental import pallas as pl
from jax.experimental.pallas import tpu as pltpu

F32 = jnp.float32
BF16 = jnp.bfloat16

D_MODEL = 1024
DEPTH = 2
N_META = 16
CHUNK = 64
Q_BLOCK = 128
FRONT_PAD = Q_BLOCK - N_META
EPS = 1e-6
NEG = -1e30
ROPE_THETA = 10000.0
N_HEADS = 4
GLA_DK = 32
GLA_DV = 64
GLA_GATE_RANK = 16
GLA_TAU = 16.0
RET_DK = 64
MLA_Q_RANK = 192
MLA_KV_RANK = 64
MLA_NOPE = 64
MLA_ROPE = 32
DIFF_DK = 32
D_FF = 2816
N_EXPERTS = 8
D_FF_EXPERT = 3584

LANES = 128
ROW_TILE = 512
FF_CHUNK = 256
VMEM_LIMIT = 56 * 1024 * 1024

PA_W = 896
PB_W = 1024
PC_W = 384
PD_W = 1024


def _cparams(n_axes=1):
    return pltpu.CompilerParams(dimension_semantics=("arbitrary",) * n_axes,
                                vmem_limit_bytes=VMEM_LIMIT)


def _resident(shape):
    nd = len(shape)
    return pl.BlockSpec(shape, lambda *_: (0,) * nd, pipeline_mode=pl.Buffered(1))


def _sigmoid(x):
    return 1.0 / (1.0 + jnp.exp(-x))


def _split_bf16(x):
    hi = x.astype(BF16)
    lo = (x - hi.astype(F32)).astype(BF16)
    return hi, lo


def _dot(a, b):
    return jnp.dot(a, b, preferred_element_type=F32)


def _dot_nt(a, b):
    return lax.dot_general(a, b, (((1,), (1,)), ((), ())), preferred_element_type=F32)


def _inproj_kernel(h_ref, g_ref, w_ref, pa_ref, pb_ref, pc_ref, pd_ref):
    x = h_ref[...]
    ms = jnp.mean(x * x, axis=-1, keepdims=True)
    y = (x * lax.rsqrt(ms + EPS) * g_ref[...]).astype(BF16)
    off = 0
    for o_ref, width in ((pa_ref, PA_W), (pb_ref, PB_W), (pc_ref, PC_W), (pd_ref, PD_W)):
        o_ref[...] = _dot(y, w_ref[:, off:off + width]).astype(BF16)
        off += width


def _inproj(h, g, w):
    n = h.shape[0]
    wtot = PA_W + PB_W + PC_W + PD_W
    row = lambda width: pl.BlockSpec((ROW_TILE, width), lambda i: (i, 0))
    return pl.pallas_call(
        _inproj_kernel,
        grid=(n // ROW_TILE,),
        in_specs=[row(D_MODEL), _resident((1, D_MODEL)), _resident((D_MODEL, wtot))],
        out_specs=[row(PA_W), row(PB_W), row(PC_W), row(PD_W)],
        out_shape=[jax.ShapeDtypeStruct((n, w_), BF16) for w_ in (PA_W, PB_W, PC_W, PD_W)],
        compiler_params=_cparams(),
        name="inproj",
    )(h, g, w)


def _outproj_kernel(with_router, h_ref, oa_ref, ob_ref, oc_ref, od_ref, wo_ref, fn_ref, *rest):
    if with_router:
        router_ref, hmid_ref, hn_ref, info_ref, cnt_ref, carry_ref = rest
    else:
        hmid_ref, hn_ref = rest
    o = jnp.concatenate([oa_ref[...], ob_ref[...], oc_ref[...], od_ref[...]], axis=1)
    hm = h_ref[...] + _dot(o, wo_ref[...])
    hmid_ref[...] = hm
    ms = jnp.mean(hm * hm, axis=-1, keepdims=True)
    y = hm * lax.rsqrt(ms + EPS) * fn_ref[...]
    hn_ref[...] = y.astype(hn_ref.dtype)
    if with_router:
        y_hi, y_lo = _split_bf16(y)
        r_hi, r_lo = _split_bf16(router_ref[...])
        logits = _dot(y_hi, r_hi) + _dot(y_hi, r_lo) + _dot(y_lo, r_hi)
        lane = lax.broadcasted_iota(jnp.int32, logits.shape, 1).astype(F32)
        ninf = float("-inf")
        logits = jnp.where(lane < N_EXPERTS, logits, ninf)
        m1 = jnp.max(logits, axis=-1, keepdims=True)
        i1 = jnp.min(jnp.where(logits == m1, lane, float(LANES)), axis=-1, keepdims=True)
        rest_l = jnp.where(lane == i1, ninf, logits)
        m2 = jnp.max(rest_l, axis=-1, keepdims=True)
        i2 = jnp.min(jnp.where(rest_l == m2, lane, float(LANES)), axis=-1, keepdims=True)
        e2 = jnp.exp(m2 - m1)
        den = 1.0 + e2
        @pl.when(pl.program_id(0) == 0)
        def _():
            carry_ref[...] = jnp.zeros_like(carry_ref)
        sel = jnp.where(lane == i1, 1.0, 0.0) + jnp.where(lane == i2, 1.0, 0.0)
        rows = sel.shape[0]
        below = (lax.broadcasted_iota(jnp.int32, (rows, rows), 1)
                 < lax.broadcasted_iota(jnp.int32, (rows, rows), 0))
        count = _dot(jnp.where(below, 1.0, 0.0).astype(BF16), sel.astype(BF16)) + carry_ref[...]
        r1 = jnp.sum(jnp.where(lane == i1, count, 0.0), axis=-1, keepdims=True)
        r2 = jnp.sum(jnp.where(lane == i2, count, 0.0), axis=-1, keepdims=True)
        total = carry_ref[...] + jnp.sum(sel, axis=0, keepdims=True)
        carry_ref[...] = total
        cnt_ref[...] = jnp.broadcast_to(total, cnt_ref.shape)
        info = jnp.where(lane == 0, i1, 0.0)
        for k, val in enumerate((i2, 1.0 / den, e2 / den, r1, r2), start=1):
            info = jnp.where(lane == k, val, info)
        info_ref[...] = info


def _outproj(h, oa, ob, oc, od, wo, fn, router=None):
    n = h.shape[0]
    row = lambda width: pl.BlockSpec((ROW_TILE, width), lambda i: (i, 0))
    in_specs = [row(D_MODEL), row(256), row(256), row(256), row(256),
                _resident((D_MODEL, D_MODEL)), _resident((1, D_MODEL))]
    out_specs = [row(D_MODEL), row(D_MODEL)]
    out_shape = [jax.ShapeDtypeStruct((n, D_MODEL), F32),
                 jax.ShapeDtypeStruct((n, D_MODEL), BF16 if router is None else F32)]
    args = [h, oa, ob, oc, od, wo, fn]
    scratch = []
    if router is not None:
        in_specs.append(_resident((D_MODEL, LANES)))
        out_specs += [row(LANES), pl.BlockSpec((8, LANES), lambda i: (0, 0))]
        out_shape += [jax.ShapeDtypeStruct((n, LANES), F32), jax.ShapeDtypeStruct((8, LANES), F32)]
        args.append(router)
        scratch.append(pltpu.VMEM((1, LANES), F32))
    return pl.pallas_call(
        functools.partial(_outproj_kernel, router is not None),
        grid=(n // ROW_TILE,),
        in_specs=in_specs, out_specs=out_specs, out_shape=out_shape,
        scratch_shapes=scratch,
        compiler_params=_cparams(),
        name="outproj_router" if router is not None else "outproj",
    )(*args)


def _dispatch_kernel(n_fill, zs_ref, ze_ref, pos_ref, x_ref, xs_ref, zero_buf, sem, zsem):
    n_rows = x_ref.shape[0]

    def row_copy(r, k):
        p = pos_ref[0, 0, 2 * r + k]
        return pltpu.make_async_copy(x_ref.at[pl.ds(r, 1), :], xs_ref.at[pl.ds(p, 1), :], sem)

    def start(r, carry):
        row_copy(r, 0).start()
        row_copy(r, 1).start()
        return carry

    def wait(r, carry):
        row_copy(r, 0).wait()
        row_copy(r, 1).wait()
        return carry

    lax.fori_loop(0, n_rows, start, 0, unroll=8)
    lax.fori_loop(0, n_rows, wait, 0, unroll=8)

    @pl.when(pl.program_id(0) == pl.num_programs(0) - 1)
    def _():
        zero_buf[...] = jnp.zeros_like(zero_buf)

        def zero_copy(r):
            return pltpu.make_async_copy(zero_buf.at[pl.ds(0, 1), :], xs_ref.at[pl.ds(r, 1), :], zsem)

        def fill(r, carry):
            zero_copy(r).start()
            return carry

        for k in range(zs_ref.shape[0]):
            lax.fori_loop(zs_ref[k], ze_ref[k], fill, 0)

        def drain(r, carry):
            zero_copy(0).wait()
            return carry

        lax.fori_loop(0, n_fill, drain, 0)


def _dispatch(zero_start, zero_end, pos, x, n_sorted):
    n, d = x.shape
    return pl.pallas_call(
        functools.partial(_dispatch_kernel, n_sorted - 2 * n),
        grid_spec=pltpu.PrefetchScalarGridSpec(
            num_scalar_prefetch=2,
            grid=(n // ROW_TILE,),
            in_specs=[pl.BlockSpec((1, 1, 2 * ROW_TILE), lambda i, zs, ze: (i, 0, 0),
                                   memory_space=pltpu.SMEM),
                      pl.BlockSpec((ROW_TILE, d), lambda i, zs, ze: (i, 0))],
            out_specs=pl.BlockSpec(memory_space=pl.ANY),
            scratch_shapes=[pltpu.VMEM((8, d), x.dtype), pltpu.SemaphoreType.DMA(()),
                            pltpu.SemaphoreType.DMA(())]),
        out_shape=jax.ShapeDtypeStruct((n_sorted, d), x.dtype),
        compiler_params=_cparams(),
        name="moe_dispatch",
    )(zero_start, zero_end, pos, x)


def _swiglu_tile(x, wg, wu, wd, acc_ref):
    n_chunks = wd.shape[0] // FF_CHUNK
    for c in range(n_chunks):
        cols = slice(c * FF_CHUNK, (c + 1) * FF_CHUNK)
        g = _dot(x, wg[:, cols])
        u = _dot(x, wu[:, cols])
        part = _dot((g * _sigmoid(g) * u).astype(BF16), wd[cols, :])
        if c == 0:
            acc_ref[...] = part
        elif c < n_chunks - 1:
            acc_ref[...] += part
    return acc_ref[...] + part


def _experts_kernel(te_ref, tv_ref, xs_ref, wg_ref, wu_ref, wd_ref, ys_ref, acc_ref):
    i = pl.program_id(0)

    @pl.when(tv_ref[i] > 0)
    def _():
        ys_ref[...] = _swiglu_tile(xs_ref[...].astype(BF16), wg_ref.at[0], wu_ref.at[0],
                                   wd_ref.at[0], acc_ref)

    @pl.when(tv_ref[i] == 0)
    def _():
        ys_ref[...] = jnp.zeros_like(ys_ref)


def _experts(tile_expert, tile_valid, xs, wg, wu, wd):
    n_sorted, d = xs.shape
    wspec = lambda shape: pl.BlockSpec((1,) + shape[1:], lambda i, te, tv: (te[i], 0, 0),
                                       pipeline_mode=pl.Buffered(1))
    row = pl.BlockSpec((ROW_TILE, d), lambda i, te, tv: (i, 0))
    return pl.pallas_call(
        _experts_kernel,
        grid_spec=pltpu.PrefetchScalarGridSpec(
            num_scalar_prefetch=2,
            grid=(n_sorted // ROW_TILE,),
            in_specs=[row, wspec(wg.shape), wspec(wu.shape), wspec(wd.shape)],
            out_specs=row,
            scratch_shapes=[pltpu.VMEM((ROW_TILE, d), F32)]),
        out_shape=jax.ShapeDtypeStruct((n_sorted, d), F32),
        compiler_params=_cparams(),
        name="moe_experts",
    )(tile_expert, tile_valid, xs, wg, wu, wd)


def _combine_norm_kernel(tp, pos_ref, h_ref, info_ref, ys_ref, g_ref, o_ref,
                         h_buf, info_buf, y_buf, sem, row_sem):
    n_rows = h_buf.shape[0]
    row0 = pl.multiple_of(pl.program_id(0) * tp + Q_BLOCK + pl.program_id(1) * n_rows, Q_BLOCK)
    h_copy = pltpu.make_async_copy(h_ref.at[pl.ds(row0, n_rows), :], h_buf, sem.at[0])
    info_copy = pltpu.make_async_copy(info_ref.at[pl.ds(row0, n_rows), :], info_buf, sem.at[1])
    h_copy.start()
    info_copy.start()

    def row_copy(r, k):
        p = pos_ref[0, 0, 2 * r + k]
        return pltpu.make_async_copy(ys_ref.at[pl.ds(p, 1), :], y_buf.at[k, pl.ds(r, 1), :], row_sem)

    def start(r, carry):
        row_copy(r, 0).start()
        row_copy(r, 1).start()
        return carry

    def wait(r, carry):
        row_copy(r, 0).wait()
        row_copy(r, 1).wait()
        return carry

    lax.fori_loop(0, n_rows, start, 0, unroll=8)
    lax.fori_loop(0, n_rows, wait, 0, unroll=8)
    h_copy.wait()
    info_copy.wait()
    info = info_buf[...]
    x = h_buf[...] + info[:, 2:3] * y_buf[0] + info[:, 3:4] * y_buf[1]
    ms = jnp.mean(x * x, axis=-1, keepdims=True)
    o_ref[0] = x * lax.rsqrt(ms + EPS) * g_ref[...]


def _combine_norm(pos, h, info, ys, g, b, tp):
    n, d = h.shape
    seq = tp - Q_BLOCK
    tiles = seq // ROW_TILE
    any_spec = pl.BlockSpec(memory_space=pl.ANY)
    return pl.pallas_call(
        functools.partial(_combine_norm_kernel, tp),
        grid=(b, tiles),
        in_specs=[pl.BlockSpec((1, 1, 2 * ROW_TILE), lambda i, j: (i * tiles + j, 0, 0),
                               memory_space=pltpu.SMEM),
                  any_spec, any_spec, any_spec, _resident((1, d))],
        out_specs=pl.BlockSpec((1, ROW_TILE, d), lambda i, j: (i, j, 0)),
        out_shape=jax.ShapeDtypeStruct((b, seq, d), F32),
        scratch_shapes=[pltpu.VMEM((ROW_TILE, d), F32), pltpu.VMEM((ROW_TILE, LANES), F32),
                        pltpu.VMEM((2, ROW_TILE, d), F32),
                        pltpu.SemaphoreType.DMA((2,)), pltpu.SemaphoreType.DMA(())],
        compiler_params=_cparams(2),
        name="moe_combine_norm",
    )(pos, h, info, ys, g)


def _routed_moe_norm(h, hn, info, cnt, wg, wu, wd, g, b, tp):
    n = h.shape[0]
    n_sorted = 2 * n + N_EXPERTS * ROW_TILE
    n_tiles = n_sorted // ROW_TILE
    counts = cnt[0, :N_EXPERTS].astype(jnp.int32)
    tiles_e = (counts + ROW_TILE - 1) // ROW_TILE
    tile_end = jnp.cumsum(tiles_e)
    row_base = (tile_end - tiles_e) * ROW_TILE
    e_idx = info[:, 0:2].astype(jnp.int32)
    pos = row_base[e_idx] + info[:, 4:6].astype(jnp.int32)
    tile_ids = jnp.arange(n_tiles, dtype=jnp.int32)
    tile_expert = jnp.minimum(jnp.sum((tile_ids[:, None] >= tile_end[None, :]).astype(jnp.int32), axis=1),
                              N_EXPERTS - 1)
    tile_valid = (tile_ids < tile_end[-1]).astype(jnp.int32)
    zero_start = jnp.concatenate([row_base + counts, tile_end[-1:] * ROW_TILE])
    zero_end = jnp.concatenate([tile_end * ROW_TILE, jnp.full((1,), n_sorted, jnp.int32)])
    xs = _dispatch(zero_start, zero_end, pos.reshape(n // ROW_TILE, 1, 2 * ROW_TILE), hn, n_sorted)
    ys = _experts(tile_expert, tile_valid, xs, wg, wu, wd)
    pos_seq = pos.reshape(b, tp, 2)[:, Q_BLOCK:, :].reshape(-1, 1, 2 * ROW_TILE)
    return _combine_norm(pos_seq, h, info, ys, g, b, tp)


def _swiglu_kernel(h_ref, hn_ref, wg_ref, wu_ref, wd_ref, o_ref, acc_ref):
    o_ref[...] = h_ref[...] + _swiglu_tile(hn_ref[...], wg_ref, wu_ref, wd_ref, acc_ref)


def _swiglu(h, hn, wg, wu, wd):
    n = h.shape[0]
    row = pl.BlockSpec((ROW_TILE, D_MODEL), lambda i: (i, 0))
    return pl.pallas_call(
        _swiglu_kernel,
        grid=(n // ROW_TILE,),
        in_specs=[row, row, _resident(wg.shape), _resident(wu.shape), _resident(wd.shape)],
        out_specs=row,
        out_shape=jax.ShapeDtypeStruct((n, D_MODEL), F32),
        scratch_shapes=[pltpu.VMEM((ROW_TILE, D_MODEL), F32)],
        input_output_aliases={0: 0},
        compiler_params=_cparams(),
        name="swiglu",
    )(h, hn, wg, wu, wd)


def _group_ones(n, group_shift):
    r = lax.broadcasted_iota(jnp.int32, (n, n), 0) >> group_shift
    c = lax.broadcasted_iota(jnp.int32, (n, n), 1) >> group_shift
    return jnp.where(r == c, 1.0, 0.0).astype(BF16)


GLA_ROWS = 256


def _gla_kernel(tp, pa_ref, wg_ref, bg_ref, gn_ref, o_ref, s_ref):
    qhead = lax.broadcasted_iota(jnp.int32, (1, N_HEADS * GLA_DK), 1) >> 5
    ehead = lax.broadcasted_iota(jnp.int32, (1, N_HEADS * GLA_DV), 1) >> 6
    shead = lax.broadcasted_iota(jnp.int32, (N_HEADS * GLA_DK, 1), 0) >> 5
    bd = shead == ehead
    gsum = _group_ones(N_HEADS * GLA_DV, 6)
    wg_hi, wg_lo = _split_bf16(wg_ref[...])
    bg = bg_ref[...]
    gn = gn_ref[...]
    scale = GLA_DK ** -0.5
    s_ref[...] = jnp.zeros_like(s_ref)

    def block(r0, n_rows):
        n_c = n_rows // CHUNK
        rows = pl.ds(r0, n_rows)
        ri = lax.broadcasted_iota(jnp.int32, (n_rows, n_rows), 0)
        ci = lax.broadcasted_iota(jnp.int32, (n_rows, n_rows), 1)
        same = (ri >> 6) == (ci >> 6)
        tri_bf = jnp.where(jnp.logical_and(same, ri >= ci), 1.0, 0.0).astype(BF16)
        ones_bf = jnp.where(same, 1.0, 0.0).astype(BF16)
        r4 = lax.broadcasted_iota(jnp.int32, (N_HEADS * n_rows, n_rows), 0) & (n_rows - 1)
        c4 = lax.broadcasted_iota(jnp.int32, (N_HEADS * n_rows, n_rows), 1)
        tri4 = jnp.logical_and((r4 >> 6) == (c4 >> 6), r4 >= c4)
        col_chunk = lax.broadcasted_iota(jnp.int32, (1, n_rows), 1) >> 6

        q = pa_ref[0, rows, 0:128].astype(F32) * scale
        k = pa_ref[0, rows, 128:256].astype(F32)
        v = pa_ref[0, rows, 256:512]
        og = pa_ref[0, rows, 512:768].astype(F32)
        lr = pa_ref[0, rows, 768:896]
        valid = (r0 + lax.broadcasted_iota(jnp.int32, (n_rows, 1), 0)) >= FRONT_PAD

        pre = _dot(lr, wg_hi) + _dot(lr, wg_lo) + bg
        logsig = jnp.minimum(pre, 0.0) - jnp.log1p(jnp.exp(-jnp.abs(pre)))
        g = jnp.where(valid, logsig * (1.0 / GLA_TAU), 0.0)
        g_hi, g_lo = _split_bf16(g)
        cum = _dot(tri_bf, g_hi) + _dot(tri_bf, g_lo)
        cum_end = _dot(ones_bf, g_hi) + _dot(ones_bf, g_lo)
        qt_bf = (q * jnp.exp(cum)).astype(BF16)
        kt = (k * jnp.exp(-cum)).astype(BF16)
        kd_t = (k * jnp.exp(cum_end - cum)).T.astype(BF16)
        dec_t = jnp.exp(cum_end.T)

        qs = jnp.concatenate([jnp.where(qhead == h, qt_bf, jnp.zeros_like(qt_bf))
                              for h in range(N_HEADS)], axis=0)
        a = jnp.where(tri4, _dot_nt(qs, kt), 0.0).astype(BF16)
        r = _dot(a, v)
        o = jnp.where(ehead == 0, r[0:n_rows, :], 0.0)
        for h in range(1, N_HEADS):
            o = o + jnp.where(ehead == h, r[h * n_rows:(h + 1) * n_rows, :], 0.0)

        s = s_ref[...]
        inter = []
        for c in range(n_c):
            inter.append(_dot(qt_bf[c * CHUNK:(c + 1) * CHUNK, :], s.astype(BF16)))
            upd = _dot(jnp.where(col_chunk == c, kd_t, jnp.zeros_like(kd_t)), v)
            s = dec_t[:, c * CHUNK:c * CHUNK + 1] * s + jnp.where(bd, upd, 0.0)
        s_ref[...] = s
        o = o + jnp.concatenate(inter, axis=0)

        ms = _dot((o * o).astype(BF16), gsum) * (1.0 / GLA_DV)
        y = o * lax.rsqrt(ms + EPS) * gn * (og * _sigmoid(og))
        o_ref[0, rows, :] = jnp.where(valid, y, 0.0).astype(BF16)

    block(0, Q_BLOCK)

    def body(i, carry):
        block(pl.multiple_of(Q_BLOCK + i * GLA_ROWS, Q_BLOCK), GLA_ROWS)
        return carry

    lax.fori_loop(0, (tp - Q_BLOCK) // GLA_ROWS, body, 0)


def _gla(pa, wg, bg, gn):
    b, tp, _ = pa.shape
    return pl.pallas_call(
        functools.partial(_gla_kernel, tp),
        grid=(b,),
        in_specs=[pl.BlockSpec((1, tp, PA_W), lambda i: (i, 0, 0)),
                  _resident((LANES, LANES)), _resident((1, LANES)), _resident((1, 256))],
        out_specs=pl.BlockSpec((1, tp, 256), lambda i: (i, 0, 0)),
        out_shape=jax.ShapeDtypeStruct((b, tp, 256), BF16),
        scratch_shapes=[pltpu.VMEM((N_HEADS * GLA_DK, N_HEADS * GLA_DV), F32)],
        compiler_params=_cparams(),
        name="gla",
    )(pa, wg, bg, gn)


RET_BLOCK = 128


def _ret_kernel(tp, pb_ref, cos_ref, sin_ref, dmask_ref, qfac_ref, kfac_ref, dec_ref, gn_ref,
                o_ref, s_ref):
    blk = RET_BLOCK
    n_blocks = tp // blk
    qhead = (lax.broadcasted_iota(jnp.int32, (1, 256), 1) & 127) >> 5
    ehead = lax.broadcasted_iota(jnp.int32, (1, 256), 1) >> 6
    shead = (lax.broadcasted_iota(jnp.int32, (256, 1), 0) & 127) >> 5
    bd = shead == ehead
    gsum = _group_ones(256, 6)
    gn = gn_ref[...]
    s_ref[...] = jnp.zeros_like(s_ref)

    def rope(x, cos, sin):
        x1, x2 = x[:, :128], x[:, 128:]
        return jnp.concatenate([x1 * cos - x2 * sin, x1 * sin + x2 * cos], axis=1)

    def block(j, carry):
        r0 = pl.multiple_of(j * blk, blk)
        rows = pl.ds(r0, blk)
        cos = cos_ref[rows, :]
        sin = sin_ref[rows, :]
        q = rope(pb_ref[0, rows, 0:256].astype(F32), cos, sin)
        k = rope(pb_ref[0, rows, 256:512].astype(F32), cos, sin) * (RET_DK ** -0.5)
        v = pb_ref[0, rows, 512:768]
        og = pb_ref[0, rows, 768:1024].astype(F32)
        valid = (r0 + lax.broadcasted_iota(jnp.int32, (blk, 1), 0)) >= FRONT_PAD

        q_bf = q.astype(BF16)
        qs = jnp.concatenate([jnp.where(qhead == h, q_bf, jnp.zeros_like(q_bf))
                              for h in range(N_HEADS)], axis=0)
        a = (_dot_nt(qs, k.astype(BF16)) * dmask_ref[...]).astype(BF16)
        r = _dot(a, v)
        o = _dot((q * qfac_ref[...]).astype(BF16), s_ref[...].astype(BF16))
        for h in range(N_HEADS):
            o = o + jnp.where(ehead == h, r[h * blk:(h + 1) * blk, :], 0.0)

        kd_t = (k * kfac_ref[...]).T.astype(BF16)
        upd = _dot(kd_t, v)
        s_ref[...] = dec_ref[...] * s_ref[...] + jnp.where(bd, upd, 0.0)

        mu = _dot(o.astype(BF16), gsum) * (1.0 / 64)
        xc = o - mu
        var = _dot((xc * xc).astype(BF16), gsum) * (1.0 / 64)
        y = xc * lax.rsqrt(var + EPS) * gn * (og * _sigmoid(og))
        o_ref[0, rows, :] = jnp.where(valid, y, 0.0).astype(BF16)
        return carry

    lax.fori_loop(0, n_blocks, block, 0, unroll=2)


def _ret(pb, cos, sin, dmask, qfac, kfac, dec, gn):
    b, tp, _ = pb.shape
    return pl.pallas_call(
        functools.partial(_ret_kernel, tp),
        grid=(b,),
        in_specs=[pl.BlockSpec((1, tp, PB_W), lambda i: (i, 0, 0)),
                  _resident(cos.shape), _resident(sin.shape), _resident(dmask.shape),
                  _resident(qfac.shape), _resident(kfac.shape), _resident(dec.shape),
                  _resident((1, 256))],
        out_specs=pl.BlockSpec((1, tp, 256), lambda i: (i, 0, 0)),
        out_shape=jax.ShapeDtypeStruct((b, tp, 256), BF16),
        scratch_shapes=[pltpu.VMEM((256, 256), F32)],
        compiler_params=_cparams(),
        name="retention",
    )(pb, cos, sin, dmask, qfac, kfac, dec, gn)


ATT_ROWS = 256
LOG2E = 1.4426950408889634
V_ONE = 64


def _for_pairs(n, body):
    def pair(t, carry):
        body(2 * t)
        body(2 * t + 1)
        return carry

    lax.fori_loop(0, n >> 1, pair, 0)

    @pl.when((n & 1) == 1)
    def _():
        body(n - 1)


def _mask_groups(mask, s, n_g):
    rows = s.shape[0] // n_g
    return jnp.concatenate([jnp.where(mask, s[g * rows:(g + 1) * rows, :], NEG)
                            for g in range(n_g)], axis=0)


def _attn_block0(qk_fn, v_fn, n_g):
    rows = pl.ds(0, Q_BLOCK)
    qrow = lax.broadcasted_iota(jnp.int32, (Q_BLOCK, 1), 0)
    kcol = lax.broadcasted_iota(jnp.int32, (1, Q_BLOCK), 1)
    mask = jnp.logical_and(kcol <= qrow, kcol >= FRONT_PAD)
    s = _mask_groups(mask, qk_fn(0, Q_BLOCK, rows), n_g)
    p_bf = jnp.exp2(s - jnp.max(s, axis=-1, keepdims=True)).astype(BF16)
    pv = jnp.concatenate([_dot(p_bf[g * Q_BLOCK:(g + 1) * Q_BLOCK, :], v_fn(g, rows))
                          for g in range(n_g)], axis=0)
    return pv / pv[:, V_ONE:V_ONE + 1]


def _attn_block(i, qk_fn, v_fn, n_g, s_meta, s_s, mx_s, acc_s):
    rr = ATT_ROWS
    q0 = pl.multiple_of(Q_BLOCK + (i - 1) * rr, Q_BLOCK)
    meta_rows = pl.ds(0, Q_BLOCK)
    kcol = lax.broadcasted_iota(jnp.int32, (1, Q_BLOCK), 1)

    def key_rows(j):
        return pl.ds(pl.multiple_of(Q_BLOCK + j * rr, Q_BLOCK), rr)

    s = jnp.where(kcol >= FRONT_PAD, qk_fn(q0, rr, meta_rows), NEG)
    s_meta[...] = s
    mx_s[...] = s

    def pass1(j):
        sj = qk_fn(q0, rr, key_rows(j))
        s_s[j] = sj
        mx_s[...] = jnp.maximum(mx_s[...], jnp.maximum(sj[:, :LANES], sj[:, LANES:]))

    _for_pairs(i - 1, pass1)
    causal = (lax.broadcasted_iota(jnp.int32, (rr, rr), 1)
              <= lax.broadcasted_iota(jnp.int32, (rr, rr), 0))
    sd = _mask_groups(causal, qk_fn(q0, rr, pl.ds(q0, rr)), n_g)
    s_s[i - 1] = sd
    m = jnp.max(jnp.maximum(mx_s[...], jnp.maximum(sd[:, :LANES], sd[:, LANES:])),
                axis=-1, keepdims=True)
    mx_s[...] = jnp.broadcast_to(m, mx_s.shape)

    p_bf = jnp.exp2(s_meta[...] - mx_s[...]).astype(BF16)
    for g in range(n_g):
        acc_s[g * rr:(g + 1) * rr, :] = _dot(p_bf[g * rr:(g + 1) * rr, :], v_fn(g, meta_rows))

    def pass2(j):
        sj = s_s[j]
        mrep = mx_s[...]
        p0 = jnp.exp2(sj[:, :LANES] - mrep)
        p1 = jnp.exp2(sj[:, LANES:] - mrep)
        pj = jnp.concatenate([p0.astype(BF16), p1.astype(BF16)], axis=1)
        for g in range(n_g):
            acc_s[g * rr:(g + 1) * rr, :] += _dot(pj[g * rr:(g + 1) * rr, :], v_fn(g, key_rows(j)))

    _for_pairs(i, pass2)
    acc = acc_s[...]
    return acc / acc[:, V_ONE:V_ONE + 1]


def _heads_to_lanes(per_head):
    low = lax.broadcasted_iota(jnp.int32, (1, LANES), 1) < 64
    lo = jnp.where(low, per_head[0], pltpu.roll(per_head[1], 64, 1))
    hi = jnp.where(low, per_head[2], pltpu.roll(per_head[3], 64, 1))
    return jnp.concatenate([lo, hi], axis=1)


def _with_ones_lane(v):
    lane = lax.broadcasted_iota(jnp.int32, (1, LANES), 1)
    return jnp.where(lane == V_ONE, jnp.ones_like(v), v)


def _mla_kernel(tp, pc_ref, qn_ref, kvn_ref, wuq_ref, wukv_ref, cos_ref, sa_ref, sb_ref,
                o_ref, q_s, k_s, v_s, s_meta, s_s, mx_s, acc_s):
    n_blocks = tp // Q_BLOCK
    scale = (MLA_NOPE + MLA_ROPE) ** -0.5 * LOG2E
    is_q = lax.broadcasted_iota(jnp.int32, (1, 256), 1) < MLA_Q_RANK

    def prep(i, carry):
        r0 = pl.multiple_of(i * Q_BLOCK, Q_BLOCK)
        rows = pl.ds(r0, Q_BLOCK)
        x = pc_ref[0, rows, 0:256].astype(F32)
        x2 = x * x
        ms_q = jnp.sum(jnp.where(is_q, x2, 0.0), axis=-1, keepdims=True) * (1.0 / MLA_Q_RANK)
        ms_kv = jnp.sum(jnp.where(is_q, 0.0, x2), axis=-1, keepdims=True) * (1.0 / MLA_KV_RANK)
        yq = (x * lax.rsqrt(ms_q + EPS) * qn_ref[...]).astype(BF16)
        ykv = (x * lax.rsqrt(ms_kv + EPS) * kvn_ref[...]).astype(BF16)
        cq = _dot(yq, wuq_ref[...])
        kv = _dot(ykv, wukv_ref[...])
        cos = cos_ref[rows, :]
        sa = sa_ref[rows, :]
        sb = sb_ref[rows, :]

        def rope(t):
            return t * cos + pltpu.roll(t, 16, 1) * sa + pltpu.roll(t, LANES - 16, 1) * sb

        kpe = rope(pc_ref[0, rows, 256:384].astype(F32))
        for h in range(N_HEADS):
            q_s[h, rows, :] = (rope(cq[:, h * LANES:(h + 1) * LANES]) * scale).astype(BF16)
            k_s[h, rows, :] = (kv[:, h * LANES:(h + 1) * LANES] + kpe).astype(BF16)
            v_s[h, rows, :] = _with_ones_lane(
                kv[:, (N_HEADS + h) * LANES:(N_HEADS + h + 1) * LANES]).astype(BF16)
        return carry

    lax.fori_loop(0, n_blocks, prep, 0)

    def qk_fn(q0, n_rows, krows):
        return jnp.concatenate([_dot_nt(q_s[h, pl.ds(q0, n_rows), :], k_s[h, krows, :])
                                for h in range(N_HEADS)], axis=0)

    def v_fn(h, krows):
        return v_s[h, krows, :]

    def emit(q0, n_rows, o, first):
        y = _heads_to_lanes([o[h * n_rows:(h + 1) * n_rows, :] for h in range(N_HEADS)])
        if first:
            qrow = lax.broadcasted_iota(jnp.int32, (n_rows, 1), 0)
            y = jnp.where(qrow >= FRONT_PAD, y, 0.0)
        o_ref[0, pl.ds(q0, n_rows), :] = y.astype(BF16)

    emit(0, Q_BLOCK, _attn_block0(qk_fn, v_fn, N_HEADS), True)

    def qblock(i, carry):
        o = _attn_block(i, qk_fn, v_fn, N_HEADS, s_meta, s_s, mx_s, acc_s)
        emit(pl.multiple_of(Q_BLOCK + (i - 1) * ATT_ROWS, Q_BLOCK), ATT_ROWS, o, False)
        return carry

    lax.fori_loop(1, (tp - Q_BLOCK) // ATT_ROWS + 1, qblock, 0)


def _attn_scratch(n_g, tp):
    g_rows = n_g * ATT_ROWS
    n_slots = (tp - Q_BLOCK) // ATT_ROWS
    return [pltpu.VMEM((g_rows, LANES), F32),
            pltpu.VMEM((n_slots, g_rows, ATT_ROWS), F32),
            pltpu.VMEM((g_rows, LANES), F32),
            pltpu.VMEM((g_rows, LANES), F32)]


def _mla(pc, qn, kvn, wuq, wukv, cos, sa, sb):
    b, tp, _ = pc.shape
    return pl.pallas_call(
        functools.partial(_mla_kernel, tp),
        grid=(b,),
        in_specs=[pl.BlockSpec((1, tp, PC_W), lambda i: (i, 0, 0)),
                  _resident((1, 256)), _resident((1, 256)),
                  _resident(wuq.shape), _resident(wukv.shape),
                  _resident(cos.shape), _resident(sa.shape), _resident(sb.shape)],
        out_specs=pl.BlockSpec((1, tp, 256), lambda i: (i, 0, 0)),
        out_shape=jax.ShapeDtypeStruct((b, tp, 256), BF16),
        scratch_shapes=[pltpu.VMEM((N_HEADS, tp, LANES), BF16),
                        pltpu.VMEM((N_HEADS, tp, LANES), BF16),
                        pltpu.VMEM((N_HEADS, tp, LANES), BF16)] + _attn_scratch(N_HEADS, tp),
        compiler_params=_cparams(),
        name="mla",
    )(pc, qn, kvn, wuq, wukv, cos, sa, sb)


def _diff_kernel(tp, lam_init, pd_ref, lam_ref, dn_ref, o_ref, qs_s, v_s, s_meta, s_s, mx_s, acc_s):
    n_maps = 2 * N_HEADS

    def fill_values(i, carry):
        rows = pl.ds(pl.multiple_of(i * Q_BLOCK, Q_BLOCK), Q_BLOCK)
        for h in range(N_HEADS):
            v_s[h, rows, :] = _with_ones_lane(pd_ref[0, rows, 512 + h * LANES:512 + (h + 1) * LANES])
        return carry

    lax.fori_loop(0, tp // Q_BLOCK, fill_values, 0)
    scale = DIFF_DK ** -0.5 * LOG2E
    group = lax.broadcasted_iota(jnp.int32, (1, 256), 1) >> 5
    lv = lam_ref[...]
    lam = (jnp.exp(jnp.sum(lv[0:1, :] * lv[1:2, :], axis=-1, keepdims=True))
           - jnp.exp(jnp.sum(lv[2:3, :] * lv[3:4, :], axis=-1, keepdims=True)) + lam_init)
    dn = dn_ref[...]

    def stack_queries(q0, n_rows):
        q = (pd_ref[0, pl.ds(q0, n_rows), 0:256].astype(F32) * scale).astype(BF16)
        for g in range(n_maps):
            qs_s[g * n_rows:(g + 1) * n_rows, :] = jnp.where(group == g, q, jnp.zeros_like(q))

    def qk_fn(q0, n_rows, krows):
        return _dot_nt(qs_s[0:n_maps * n_rows, :], pd_ref[0, krows, 256:512])

    def v_fn(g, krows):
        return v_s[g // 2, krows, :]

    def emit(q0, n_rows, o, first):
        od = jnp.concatenate([o[(2 * h) * n_rows:(2 * h + 1) * n_rows, :]
                              - lam * o[(2 * h + 1) * n_rows:(2 * h + 2) * n_rows, :]
                              for h in range(N_HEADS)], axis=0)
        od = jnp.where(lax.broadcasted_iota(jnp.int32, (1, LANES), 1) < V_ONE, od, 0.0)
        ms = jnp.sum(od * od, axis=-1, keepdims=True) * (1.0 / 64)
        yh = od * lax.rsqrt(ms + EPS) * dn * (1.0 - lam_init)
        y = _heads_to_lanes([yh[h * n_rows:(h + 1) * n_rows, :] for h in range(N_HEADS)])
        if first:
            qrow = lax.broadcasted_iota(jnp.int32, (n_rows, 1), 0)
            y = jnp.where(qrow >= FRONT_PAD, y, 0.0)
        o_ref[0, pl.ds(q0, n_rows), :] = y.astype(BF16)

    stack_queries(0, Q_BLOCK)
    emit(0, Q_BLOCK, _attn_block0(qk_fn, v_fn, n_maps), True)

    def qblock(i, carry):
        q0 = pl.multiple_of(Q_BLOCK + (i - 1) * ATT_ROWS, Q_BLOCK)
        stack_queries(q0, ATT_ROWS)
        o = _attn_block(i, qk_fn, v_fn, n_maps, s_meta, s_s, mx_s, acc_s)
        emit(q0, ATT_ROWS, o, False)
        return carry

    lax.fori_loop(1, (tp - Q_BLOCK) // ATT_ROWS + 1, qblock, 0)


def _diff(pd, lam_rows, dn, lam_init):
    b, tp, _ = pd.shape
    n_maps = 2 * N_HEADS
    return pl.pallas_call(
        functools.partial(_diff_kernel, tp, lam_init),
        grid=(b,),
        in_specs=[pl.BlockSpec((1, tp, PD_W), lambda i: (i, 0, 0)),
                  _resident(lam_rows.shape), _resident((1, LANES))],
        out_specs=pl.BlockSpec((1, tp, 256), lambda i: (i, 0, 0)),
        out_shape=jax.ShapeDtypeStruct((b, tp, 256), BF16),
        scratch_shapes=[pltpu.VMEM((n_maps * ATT_ROWS, 256), BF16),
                        pltpu.VMEM((N_HEADS, tp, LANES), BF16)] + _attn_scratch(n_maps, tp),
        compiler_params=_cparams(),
        name="diffattn",
    )(pd, lam_rows, dn)


def _pad_cols(x, width):
    return jnp.pad(x, ((0, 0), (0, width - x.shape[1])))


def _rot_split(w):
    d = w.shape[0]
    return w.reshape(d, N_HEADS, 2, 32).transpose(0, 2, 1, 3).reshape(d, 256)


def _layout_w_in(w):
    sizes = (128, 128, 256, 16, 256, 256, 256, 256, 256, 192, 64, 32, 256, 256, 256)
    offs = [0]
    for s_ in sizes:
        offs.append(offs[-1] + s_)
    seg = [w[:, offs[i]:offs[i + 1]] for i in range(len(sizes))]
    (a_q, a_k, a_v, a_lr, a_og, r_q, r_k, r_v, r_og, c_cq, c_ckv, c_kpe, d_q, d_k, d_v) = seg
    d = w.shape[0]
    z = lambda n: jnp.zeros((d, n), w.dtype)
    dv_p = jnp.pad(d_v.reshape(d, N_HEADS, 64), ((0, 0), (0, 0), (0, 64))).reshape(d, 512)
    cols = [a_q, a_k, a_v, a_og, a_lr, z(112),
            _rot_split(r_q), _rot_split(r_k), r_v, r_og,
            c_cq, c_ckv, z(64), c_kpe, z(32),
            d_q, d_k, dv_p]
    return jnp.concatenate(cols, axis=1).astype(BF16)


def _tables(tp):
    pos = jnp.arange(tp, dtype=F32) - FRONT_PAD
    inv = ROPE_THETA ** (-jnp.arange(32, dtype=F32) / 32)
    ang = pos[:, None] * inv[None, :]
    ret_cos = jnp.tile(jnp.cos(ang), (1, N_HEADS))
    ret_sin = jnp.tile(jnp.sin(ang), (1, N_HEADS))
    inv16 = ROPE_THETA ** (-jnp.arange(16, dtype=F32) / 16)
    ang16 = pos[:, None] * inv16[None, :]
    c16, s16 = jnp.cos(ang16), jnp.sin(ang16)
    one = lambda n: jnp.ones((tp, n), F32)
    zero = lambda n: jnp.zeros((tp, n), F32)
    mla_cos = jnp.concatenate([one(64), c16, c16, one(32)], axis=1)
    mla_sa = jnp.concatenate([zero(80), s16, zero(32)], axis=1)
    mla_sb = jnp.concatenate([zero(64), -s16, zero(48)], axis=1)
    lg = jnp.log(1.0 - jnp.exp2(-5.0 - jnp.arange(N_HEADS, dtype=F32)))
    idx = jnp.arange(RET_BLOCK, dtype=F32)
    rel = idx[:, None] - idx[None, :]
    dmask = jnp.where(rel[None] >= 0, jnp.exp(rel[None] * lg[:, None, None]), 0.0)
    dmask = dmask.reshape(N_HEADS * RET_BLOCK, RET_BLOCK)
    lane_head = (jnp.arange(256) % 128) // 32
    qfac = jnp.exp((idx[:, None] + 1.0) * lg[lane_head][None, :])
    kfac = jnp.exp((RET_BLOCK - 1.0 - idx[:, None]) * lg[lane_head][None, :])
    dec = jnp.exp(RET_BLOCK * lg[lane_head])[:, None]
    return ret_cos, ret_sin, mla_cos, mla_sa, mla_sb, dmask, qfac, kfac, dec


def kernel(x, meta_tokens, attn_norm, w_in, gla_w_gate, gla_b_gate, gla_norm, ret_norm, mla_q_norm, mla_w_uq, mla_kv_norm, mla_w_ukv, diff_lambda, diff_norm, w_out, ffn_norm, ffn_w_gate, ffn_w_up, ffn_w_down, moe_router, moe_w_gate, moe_w_up, moe_w_down, final_norm):
    b, seq, d = x.shape
    tp = FRONT_PAD + N_META + seq
    n = b * tp
    meta = jnp.broadcast_to(meta_tokens[None].astype(x.dtype), (b, N_META, d))
    h = jnp.concatenate([jnp.zeros((b, FRONT_PAD, d), x.dtype), meta, x], axis=1).reshape(n, d)
    ret_cos, ret_sin, mla_cos, mla_sa, mla_sb, dmask, qfac, kfac, dec = _tables(tp)

    for li in range(DEPTH):
        pa, pb, pc, pd = _inproj(h, attn_norm[li][None, :], _layout_w_in(w_in[li]))
        pa, pb, pc, pd = (p.reshape(b, tp, -1) for p in (pa, pb, pc, pd))

        wgate = jnp.pad(gla_w_gate[li], ((0, LANES - GLA_GATE_RANK), (0, 0)))
        o_a = _gla(pa, wgate, gla_b_gate[li][None, :], jnp.tile(gla_norm[li], N_HEADS)[None, :])
        o_b = _ret(pb, ret_cos, ret_sin, dmask, qfac, kfac, dec, jnp.tile(ret_norm[li], N_HEADS)[None, :])

        qn = _pad_cols(mla_q_norm[li][None, :], 256)
        kvn = jnp.pad(mla_kv_norm[li][None, :], ((0, 0), (MLA_Q_RANK, 0)))
        wuq = jnp.pad(mla_w_uq[li].reshape(MLA_Q_RANK, N_HEADS, MLA_NOPE + MLA_ROPE),
                      ((0, 256 - MLA_Q_RANK), (0, 0), (0, LANES - MLA_NOPE - MLA_ROPE)))
        wuq = wuq.reshape(256, N_HEADS * LANES).astype(BF16)
        wukv = mla_w_ukv[li].reshape(MLA_KV_RANK, N_HEADS, 2, 64)
        wukv = jnp.pad(wukv, ((MLA_Q_RANK, 0), (0, 0), (0, 0), (0, 64)))
        wukv = wukv.transpose(0, 2, 1, 3).reshape(256, 2 * N_HEADS * LANES).astype(BF16)
        o_c = _mla(pc, qn, kvn, wuq, wukv, mla_cos, mla_sa, mla_sb)

        lam_init = 0.8 - 0.6 * math.exp(-0.3 * li)
        o_d = _diff(pd, diff_lambda[li], _pad_cols(diff_norm[li][None, :], LANES), lam_init)

        o_a, o_b, o_c, o_d = (o.reshape(n, 256) for o in (o_a, o_b, o_c, o_d))
        wo = w_out[li].astype(BF16)
        fn = ffn_norm[li][None, :]
        j = li // 2
        if li % 2 == 0:
            h, hn = _outproj(h, o_a, o_b, o_c, o_d, wo, fn)
            h = _swiglu(h, hn, ffn_w_gate[j].astype(BF16), ffn_w_up[j].astype(BF16),
                        ffn_w_down[j].astype(BF16))
        else:
            h, hn, info, cnt = _outproj(h, o_a, o_b, o_c, o_d, wo, fn,
                                        router=_pad_cols(moe_router[j], LANES))
            return _routed_moe_norm(h, hn, info, cnt, moe_w_gate[j].astype(BF16),
                                    moe_w_up[j].astype(BF16), moe_w_down[j].astype(BF16),
                                    final_norm[None, :], b, tp)
```

```python
import functools
import math

import jax
import jax.numpy as jnp
from jax import lax
from jax.experimental import pallas as pl
from jax.experimental.pallas import tpu as pltpu

F32 = jnp.float32
BF16 = jnp.bfloat16

D_MODEL = 1024
DEPTH = 2
N_META = 16
CHUNK = 64
Q_BLOCK = 128
FRONT_PAD = Q_BLOCK - N_META
EPS = 1e-6
NEG = -1e30
ROPE_THETA = 10000.0
N_HEADS = 4
GLA_DK = 32
GLA_DV = 64
GLA_GATE_RANK = 16
GLA_TAU = 16.0
RET_DK = 64
MLA_Q_RANK = 192
MLA_KV_RANK = 64
MLA_NOPE = 64
MLA_ROPE = 32
DIFF_DK = 32
D_FF = 2816
N_EXPERTS = 8
D_FF_EXPERT = 3584

LANES = 128
SUBLANES = 8
ROW_TILE = 512
FF_CHUNK = 256
VMEM_LIMIT = 56 * 1024 * 1024

PA_W = 896
PB_W = 1024
PC_W = 384
PD_W = 1024


def _cparams(n_axes=1):
    return pltpu.CompilerParams(dimension_semantics=("arbitrary",) * n_axes,
                                vmem_limit_bytes=VMEM_LIMIT)


def _resident(shape):
    nd = len(shape)
    return pl.BlockSpec(shape, lambda *_: (0,) * nd, pipeline_mode=pl.Buffered(1))


def _sigmoid(x):
    return 1.0 / (1.0 + jnp.exp(-x))


def _split_bf16(x):
    hi = x.astype(BF16)
    lo = (x - hi.astype(F32)).astype(BF16)
    return hi, lo


def _dot(a, b):
    return jnp.dot(a, b, preferred_element_type=F32)


def _dot_nt(a, b):
    return lax.dot_general(a, b, (((1,), (1,)), ((), ())), preferred_element_type=F32)


def _inproj_kernel(h_ref, g_ref, w_ref, pa_ref, pb_ref, pc_ref, pd_ref):
    x = h_ref[...]
    ms = jnp.mean(x * x, axis=-1, keepdims=True)
    y = (x * lax.rsqrt(ms + EPS) * g_ref[...]).astype(BF16)
    off = 0
    for o_ref, width in ((pa_ref, PA_W), (pb_ref, PB_W), (pc_ref, PC_W), (pd_ref, PD_W)):
        o_ref[...] = _dot(y, w_ref[:, off:off + width]).astype(BF16)
        off += width


def _inproj(h, g, w):
    n = h.shape[0]
    wtot = PA_W + PB_W + PC_W + PD_W
    row = lambda width: pl.BlockSpec((ROW_TILE, width), lambda i: (i, 0))
    return pl.pallas_call(
        _inproj_kernel,
        grid=(n // ROW_TILE,),
        in_specs=[row(D_MODEL), _resident((1, D_MODEL)), _resident((D_MODEL, wtot))],
        out_specs=[row(PA_W), row(PB_W), row(PC_W), row(PD_W)],
        out_shape=[jax.ShapeDtypeStruct((n, w_), BF16) for w_ in (PA_W, PB_W, PC_W, PD_W)],
        compiler_params=_cparams(),
        name="inproj",
    )(h, g, w)


def _outproj_kernel(with_router, h_ref, oa_ref, ob_ref, oc_ref, od_ref, wo_ref, fn_ref, *rest):
    if with_router:
        router_ref, hmid_ref, hn_ref, info_ref, cnt_ref, carry_ref = rest
    else:
        hmid_ref, hn_ref = rest
    o = jnp.concatenate([oa_ref[...], ob_ref[...], oc_ref[...], od_ref[...]], axis=1)
    hm = h_ref[...] + _dot(o, wo_ref[...])
    hmid_ref[...] = hm
    ms = jnp.mean(hm * hm, axis=-1, keepdims=True)
    y = hm * lax.rsqrt(ms + EPS) * fn_ref[...]
    hn_ref[...] = y.astype(hn_ref.dtype)
    if with_router:
        y_hi, y_lo = _split_bf16(y)
        r_hi, r_lo = _split_bf16(router_ref[...])
        logits = _dot(y_hi, r_hi) + _dot(y_hi, r_lo) + _dot(y_lo, r_hi)
        lane = lax.broadcasted_iota(jnp.int32, logits.shape, 1).astype(F32)
        ninf = float("-inf")
        logits = jnp.where(lane < N_EXPERTS, logits, ninf)
        m1 = jnp.max(logits, axis=-1, keepdims=True)
        i1 = jnp.min(jnp.where(logits == m1, lane, float(LANES)), axis=-1, keepdims=True)
        rest_l = jnp.where(lane == i1, ninf, logits)
        m2 = jnp.max(rest_l, axis=-1, keepdims=True)
        i2 = jnp.min(jnp.where(rest_l == m2, lane, float(LANES)), axis=-1, keepdims=True)
        e2 = jnp.exp(m2 - m1)
        den = 1.0 + e2
        @pl.when(pl.program_id(0) == 0)
        def _():
            carry_ref[...] = jnp.zeros_like(carry_ref)
        sel = jnp.where(lane == i1, 1.0, 0.0) + jnp.where(lane == i2, 1.0, 0.0)
        rows = sel.shape[0]
        below = (lax.broadcasted_iota(jnp.int32, (rows, rows), 1)
                 < lax.broadcasted_iota(jnp.int32, (rows, rows), 0))
        count = _dot(jnp.where(below, 1.0, 0.0).astype(BF16), sel.astype(BF16)) + carry_ref[...]
        r1 = jnp.sum(jnp.where(lane == i1, count, 0.0), axis=-1, keepdims=True)
        r2 = jnp.sum(jnp.where(lane == i2, count, 0.0), axis=-1, keepdims=True)
        total = carry_ref[...] + jnp.sum(sel, axis=0, keepdims=True)
        carry_ref[...] = total
        cnt_ref[...] = jnp.broadcast_to(total, cnt_ref.shape)
        info = jnp.where(lane == 0, i1, 0.0)
        for k, val in enumerate((i2, 1.0 / den, e2 / den, r1, r2), start=1):
            info = jnp.where(lane == k, val, info)
        info_ref[...] = info


def _outproj(h, oa, ob, oc, od, wo, fn, router=None):
    n = h.shape[0]
    row = lambda width: pl.BlockSpec((ROW_TILE, width), lambda i: (i, 0))
    in_specs = [row(D_MODEL), row(256), row(256), row(256), row(256),
                _resident((D_MODEL, D_MODEL)), _resident((1, D_MODEL))]
    out_specs = [row(D_MODEL), row(D_MODEL)]
    out_shape = [jax.ShapeDtypeStruct((n, D_MODEL), F32),
                 jax.ShapeDtypeStruct((n, D_MODEL), BF16 if router is None else F32)]
    args = [h, oa, ob, oc, od, wo, fn]
    scratch = []
    if router is not None:
        in_specs.append(_resident((D_MODEL, LANES)))
        out_specs += [row(LANES), pl.BlockSpec((8, LANES), lambda i: (0, 0))]
        out_shape += [jax.ShapeDtypeStruct((n, LANES), F32), jax.ShapeDtypeStruct((8, LANES), F32)]
        args.append(router)
        scratch.append(pltpu.VMEM((1, LANES), F32))
    return pl.pallas_call(
        functools.partial(_outproj_kernel, router is not None),
        grid=(n // ROW_TILE,),
        in_specs=in_specs, out_specs=out_specs, out_shape=out_shape,
        scratch_shapes=scratch,
        compiler_params=_cparams(),
        name="outproj_router" if router is not None else "outproj",
    )(*args)


def _dispatch_kernel(n_fill, zs_ref, ze_ref, pos_ref, x_ref, xs_ref, zero_buf, sem, zsem):
    def row_copy(rb, u, k):
        p = pos_ref[0, 0, rb * (2 * SUBLANES) + 2 * u + k]
        return pltpu.make_async_copy(x_ref.at[rb, pl.ds(u, 1), :], xs_ref.at[pl.ds(p, 1), :], sem)

    def start(rb, carry):
        for u in range(SUBLANES):
            row_copy(rb, u, 0).start()
            row_copy(rb, u, 1).start()
        return carry

    def wait(rb, carry):
        for u in range(SUBLANES):
            row_copy(rb, u, 0).wait()
            row_copy(rb, u, 1).wait()
        return carry

    lax.fori_loop(0, x_ref.shape[0], start, 0)
    lax.fori_loop(0, x_ref.shape[0], wait, 0)

    @pl.when(pl.program_id(0) == pl.num_programs(0) - 1)
    def _():
        zero_buf[...] = jnp.zeros_like(zero_buf)

        def zero_copy(r):
            return pltpu.make_async_copy(zero_buf.at[pl.ds(0, 1), :], xs_ref.at[pl.ds(r, 1), :], zsem)

        def fill(r, carry):
            zero_copy(r).start()
            return carry

        for k in range(zs_ref.shape[0]):
            lax.fori_loop(zs_ref[k], ze_ref[k], fill, 0)

        def drain(r, carry):
            zero_copy(0).wait()
            return carry

        lax.fori_loop(0, n_fill, drain, 0)


def _dispatch(zero_start, zero_end, pos, x, n_sorted):
    n, d = x.shape
    return pl.pallas_call(
        functools.partial(_dispatch_kernel, n_sorted - 2 * n),
        grid_spec=pltpu.PrefetchScalarGridSpec(
            num_scalar_prefetch=2,
            grid=(n // ROW_TILE,),
            in_specs=[pl.BlockSpec((1, 1, 2 * ROW_TILE), lambda i, zs, ze: (i, 0, 0),
                                   memory_space=pltpu.SMEM),
                      pl.BlockSpec((ROW_TILE // SUBLANES, SUBLANES, d), lambda i, zs, ze: (i, 0, 0))],
            out_specs=pl.BlockSpec(memory_space=pl.ANY),
            scratch_shapes=[pltpu.VMEM((SUBLANES, d), x.dtype), pltpu.SemaphoreType.DMA(()),
                            pltpu.SemaphoreType.DMA(())]),
        out_shape=jax.ShapeDtypeStruct((n_sorted, d), x.dtype),
        compiler_params=_cparams(),
        name="moe_dispatch",
    )(zero_start, zero_end, pos, x.reshape(n // SUBLANES, SUBLANES, d))


def _swiglu_tile(x, wg, wu, wd, acc_ref):
    n_chunks = wd.shape[0] // FF_CHUNK
    for c in range(n_chunks):
        cols = slice(c * FF_CHUNK, (c + 1) * FF_CHUNK)
        g = _dot(x, wg[:, cols])
        u = _dot(x, wu[:, cols])
        part = _dot((g * _sigmoid(g) * u).astype(BF16), wd[cols, :])
        if c == 0:
            acc_ref[...] = part
        elif c < n_chunks - 1:
            acc_ref[...] += part
    return acc_ref[...] + part


def _experts_kernel(te_ref, tv_ref, xs_ref, wg_ref, wu_ref, wd_ref, ys_ref, acc_ref):
    i = pl.program_id(0)

    @pl.when(tv_ref[i] > 0)
    def _():
        ys_ref[...] = _swiglu_tile(xs_ref[...].astype(BF16), wg_ref.at[0], wu_ref.at[0],
                                   wd_ref.at[0], acc_ref)

    @pl.when(tv_ref[i] == 0)
    def _():
        ys_ref[...] = jnp.zeros_like(ys_ref)


def _experts(tile_expert, tile_valid, xs, wg, wu, wd):
    n_sorted, d = xs.shape
    wspec = lambda shape: pl.BlockSpec((1,) + shape[1:], lambda i, te, tv: (te[i], 0, 0),
                                       pipeline_mode=pl.Buffered(1))
    row = pl.BlockSpec((ROW_TILE, d), lambda i, te, tv: (i, 0))
    return pl.pallas_call(
        _experts_kernel,
        grid_spec=pltpu.PrefetchScalarGridSpec(
            num_scalar_prefetch=2,
            grid=(n_sorted // ROW_TILE,),
            in_specs=[row, wspec(wg.shape), wspec(wu.shape), wspec(wd.shape)],
            out_specs=row,
            scratch_shapes=[pltpu.VMEM((ROW_TILE, d), F32)]),
        out_shape=jax.ShapeDtypeStruct((n_sorted, d), F32),
        compiler_params=_cparams(),
        name="moe_experts",
    )(tile_expert, tile_valid, xs, wg, wu, wd)


def _combine_norm_kernel(tp, pos_ref, h_ref, info_ref, ys_ref, g_ref, o_ref,
                         h_buf, info_buf, y_buf, sem, row_sem):
    n_rows = h_buf.shape[0]
    row0 = pl.multiple_of(pl.program_id(0) * tp + Q_BLOCK + pl.program_id(1) * n_rows, Q_BLOCK)
    h_copy = pltpu.make_async_copy(h_ref.at[pl.ds(row0, n_rows), :], h_buf, sem.at[0])
    info_copy = pltpu.make_async_copy(info_ref.at[pl.ds(row0, n_rows), :], info_buf, sem.at[1])
    h_copy.start()
    info_copy.start()

    def row_copy(rb, u, k):
        p = pos_ref[0, 0, rb * (2 * SUBLANES) + 2 * u + k]
        return pltpu.make_async_copy(ys_ref.at[pl.ds(p, 1), :], y_buf.at[k, rb, pl.ds(u, 1), :], row_sem)

    def start(rb, carry):
        for u in range(SUBLANES):
            row_copy(rb, u, 0).start()
            row_copy(rb, u, 1).start()
        return carry

    def wait(rb, carry):
        for u in range(SUBLANES):
            row_copy(rb, u, 0).wait()
            row_copy(rb, u, 1).wait()
        return carry

    lax.fori_loop(0, n_rows // SUBLANES, start, 0)
    lax.fori_loop(0, n_rows // SUBLANES, wait, 0)
    h_copy.wait()
    info_copy.wait()
    info = info_buf[...]
    y0 = y_buf[0].reshape(n_rows, -1)
    y1 = y_buf[1].reshape(n_rows, -1)
    x = h_buf[...] + info[:, 2:3] * y0 + info[:, 3:4] * y1
    ms = jnp.mean(x * x, axis=-1, keepdims=True)
    o_ref[0] = x * lax.rsqrt(ms + EPS) * g_ref[...]


def _combine_norm(pos, h, info, ys, g, b, tp):
    n, d = h.shape
    seq = tp - Q_BLOCK
    tiles = seq // ROW_TILE
    any_spec = pl.BlockSpec(memory_space=pl.ANY)
    return pl.pallas_call(
        functools.partial(_combine_norm_kernel, tp),
        grid=(b, tiles),
        in_specs=[pl.BlockSpec((1, 1, 2 * ROW_TILE), lambda i, j: (i * tiles + j, 0, 0),
                               memory_space=pltpu.SMEM),
                  any_spec, any_spec, any_spec, _resident((1, d))],
        out_specs=pl.BlockSpec((1, ROW_TILE, d), lambda i, j: (i, j, 0)),
        out_shape=jax.ShapeDtypeStruct((b, seq, d), F32),
        scratch_shapes=[pltpu.VMEM((ROW_TILE, d), F32), pltpu.VMEM((ROW_TILE, LANES), F32),
                        pltpu.VMEM((2, ROW_TILE // SUBLANES, SUBLANES, d), F32),
                        pltpu.SemaphoreType.DMA((2,)), pltpu.SemaphoreType.DMA(())],
        compiler_params=_cparams(2),
        name="moe_combine_norm",
    )(pos, h, info, ys, g)


def _routed_moe_norm(h, hn, info, cnt, wg, wu, wd, g, b, tp):
    n = h.shape[0]
    n_sorted = 2 * n + N_EXPERTS * ROW_TILE
    n_tiles = n_sorted // ROW_TILE
    counts = cnt[0, :N_EXPERTS].astype(jnp.int32)
    tiles_e = (counts + ROW_TILE - 1) // ROW_TILE
    tile_end = jnp.cumsum(tiles_e)
    row_base = (tile_end - tiles_e) * ROW_TILE
    e_idx = info[:, 0:2].astype(jnp.int32)
    pos = row_base[e_idx] + info[:, 4:6].astype(jnp.int32)
    tile_ids = jnp.arange(n_tiles, dtype=jnp.int32)
    tile_expert = jnp.minimum(jnp.sum((tile_ids[:, None] >= tile_end[None, :]).astype(jnp.int32), axis=1),
                              N_EXPERTS - 1)
    tile_valid = (tile_ids < tile_end[-1]).astype(jnp.int32)
    zero_start = jnp.concatenate([row_base + counts, tile_end[-1:] * ROW_TILE])
    zero_end = jnp.concatenate([tile_end * ROW_TILE, jnp.full((1,), n_sorted, jnp.int32)])
    xs = _dispatch(zero_start, zero_end, pos.reshape(n // ROW_TILE, 1, 2 * ROW_TILE), hn, n_sorted)
    ys = _experts(tile_expert, tile_valid, xs, wg, wu, wd)
    pos_seq = pos.reshape(b, tp, 2)[:, Q_BLOCK:, :].reshape(-1, 1, 2 * ROW_TILE)
    return _combine_norm(pos_seq, h, info, ys, g, b, tp)


def _swiglu_kernel(h_ref, hn_ref, wg_ref, wu_ref, wd_ref, o_ref, acc_ref):
    o_ref[...] = h_ref[...] + _swiglu_tile(hn_ref[...], wg_ref, wu_ref, wd_ref, acc_ref)


def _swiglu(h, hn, wg, wu, wd):
    n = h.shape[0]
    row = pl.BlockSpec((ROW_TILE, D_MODEL), lambda i: (i, 0))
    return pl.pallas_call(
        _swiglu_kernel,
        grid=(n // ROW_TILE,),
        in_specs=[row, row, _resident(wg.shape), _resident(wu.shape), _resident(wd.shape)],
        out_specs=row,
        out_shape=jax.ShapeDtypeStruct((n, D_MODEL), F32),
        scratch_shapes=[pltpu.VMEM((ROW_TILE, D_MODEL), F32)],
        input_output_aliases={0: 0},
        compiler_params=_cparams(),
        name="swiglu",
    )(h, hn, wg, wu, wd)


def _group_ones(n, group_shift):
    r = lax.broadcasted_iota(jnp.int32, (n, n), 0) >> group_shift
    c = lax.broadcasted_iota(jnp.int32, (n, n), 1) >> group_shift
    return jnp.where(r == c, 1.0, 0.0).astype(BF16)


GLA_ROWS = 256


def _gla_kernel(tp, pa_ref, wg_ref, bg_ref, gn_ref, o_ref, s_ref):
    qhead = lax.broadcasted_iota(jnp.int32, (1, N_HEADS * GLA_DK), 1) >> 5
    ehead = lax.broadcasted_iota(jnp.int32, (1, N_HEADS * GLA_DV), 1) >> 6
    shead = lax.broadcasted_iota(jnp.int32, (N_HEADS * GLA_DK, 1), 0) >> 5
    bd = shead == ehead
    gsum = _group_ones(N_HEADS * GLA_DV, 6)
    wg_hi, wg_lo = _split_bf16(wg_ref[...])
    bg = bg_ref[...]
    gn = gn_ref[...]
    scale = GLA_DK ** -0.5
    s_ref[...] = jnp.zeros_like(s_ref)

    def block(r0, n_rows):
        n_c = n_rows // CHUNK
        rows = pl.ds(r0, n_rows)
        ri = lax.broadcasted_iota(jnp.int32, (n_rows, n_rows), 0)
        ci = lax.broadcasted_iota(jnp.int32, (n_rows, n_rows), 1)
        same = (ri >> 6) == (ci >> 6)
        tri_bf = jnp.where(jnp.logical_and(same, ri >= ci), 1.0, 0.0).astype(BF16)
        ones_bf = jnp.where(same, 1.0, 0.0).astype(BF16)
        r4 = lax.broadcasted_iota(jnp.int32, (N_HEADS * n_rows, n_rows), 0) & (n_rows - 1)
        c4 = lax.broadcasted_iota(jnp.int32, (N_HEADS * n_rows, n_rows), 1)
        tri4 = jnp.logical_and((r4 >> 6) == (c4 >> 6), r4 >= c4)
        col_chunk = lax.broadcasted_iota(jnp.int32, (1, n_rows), 1) >> 6

        q = pa_ref[0, rows, 0:128].astype(F32) * scale
        k = pa_ref[0, rows, 128:256].astype(F32)
        v = pa_ref[0, rows, 256:512]
        og = pa_ref[0, rows, 512:768].astype(F32)
        lr = pa_ref[0, rows, 768:896]
        valid = (r0 + lax.broadcasted_iota(jnp.int32, (n_rows, 1), 0)) >= FRONT_PAD

        pre = _dot(lr, wg_hi) + _dot(lr, wg_lo) + bg
        logsig = jnp.minimum(pre, 0.0) - jnp.log1p(jnp.exp(-jnp.abs(pre)))
        g = jnp.where(valid, logsig * (1.0 / GLA_TAU), 0.0)
        g_hi, g_lo = _split_bf16(g)
        cum = _dot(tri_bf, g_hi) + _dot(tri_bf, g_lo)
        cum_end = _dot(ones_bf, g_hi) + _dot(ones_bf, g_lo)
        qt_bf = (q * jnp.exp(cum)).astype(BF16)
        kt = (k * jnp.exp(-cum)).astype(BF16)
        kd_t = (k * jnp.exp(cum_end - cum)).T.astype(BF16)
        dec_t = jnp.exp(cum_end.T)

        qs = jnp.concatenate([jnp.where(qhead == h, qt_bf, jnp.zeros_like(qt_bf))
                              for h in range(N_HEADS)], axis=0)
        a = jnp.where(tri4, _dot_nt(qs, kt), 0.0).astype(BF16)
        r = _dot(a, v)
        o = jnp.where(ehead == 0, r[0:n_rows, :], 0.0)
        for h in range(1, N_HEADS):
            o = o + jnp.where(ehead == h, r[h * n_rows:(h + 1) * n_rows, :], 0.0)

        s = s_ref[...]
        inter = []
        for c in range(n_c):
            inter.append(_dot(qt_bf[c * CHUNK:(c + 1) * CHUNK, :], s.astype(BF16)))
            upd = _dot(jnp.where(col_chunk == c, kd_t, jnp.zeros_like(kd_t)), v)
            s = dec_t[:, c * CHUNK:c * CHUNK + 1] * s + jnp.where(bd, upd, 0.0)
        s_ref[...] = s
        o = o + jnp.concatenate(inter, axis=0)

        ms = _dot((o * o).astype(BF16), gsum) * (1.0 / GLA_DV)
        y = o * lax.rsqrt(ms + EPS) * gn * (og * _sigmoid(og))
        o_ref[0, rows, :] = jnp.where(valid, y, 0.0).astype(BF16)

    block(0, Q_BLOCK)

    def body(i, carry):
        block(pl.multiple_of(Q_BLOCK + i * GLA_ROWS, Q_BLOCK), GLA_ROWS)
        return carry

    lax.fori_loop(0, (tp - Q_BLOCK) // GLA_ROWS, body, 0)


def _gla(pa, wg, bg, gn):
    b, tp, _ = pa.shape
    return pl.pallas_call(
        functools.partial(_gla_kernel, tp),
        grid=(b,),
        in_specs=[pl.BlockSpec((1, tp, PA_W), lambda i: (i, 0, 0)),
                  _resident((LANES, LANES)), _resident((1, LANES)), _resident((1, 256))],
        out_specs=pl.BlockSpec((1, tp, 256), lambda i: (i, 0, 0)),
        out_shape=jax.ShapeDtypeStruct((b, tp, 256), BF16),
        scratch_shapes=[pltpu.VMEM((N_HEADS * GLA_DK, N_HEADS * GLA_DV), F32)],
        compiler_params=_cparams(),
        name="gla",
    )(pa, wg, bg, gn)


RET_BLOCK = 128


def _ret_kernel(tp, pb_ref, cos_ref, sin_ref, dmask_ref, qfac_ref, kfac_ref, dec_ref, gn_ref,
                o_ref, s_ref):
    blk = RET_BLOCK
    n_blocks = tp // blk
    qhead = (lax.broadcasted_iota(jnp.int32, (1, 256), 1) & 127) >> 5
    ehead = lax.broadcasted_iota(jnp.int32, (1, 256), 1) >> 6
    shead = (lax.broadcasted_iota(jnp.int32, (256, 1), 0) & 127) >> 5
    bd = shead == ehead
    gsum = _group_ones(256, 6)
    gn = gn_ref[...]
    s_ref[...] = jnp.zeros_like(s_ref)

    def rope(x, cos, sin):
        x1, x2 = x[:, :128], x[:, 128:]
        return jnp.concatenate([x1 * cos - x2 * sin, x1 * sin + x2 * cos], axis=1)

    def block(j, carry):
        r0 = pl.multiple_of(j * blk, blk)
        rows = pl.ds(r0, blk)
        cos = cos_ref[rows, :]
        sin = sin_ref[rows, :]
        q = rope(pb_ref[0, rows, 0:256].astype(F32), cos, sin)
        k = rope(pb_ref[0, rows, 256:512].astype(F32), cos, sin) * (RET_DK ** -0.5)
        v = pb_ref[0, rows, 512:768]
        og = pb_ref[0, rows, 768:1024].astype(F32)
        valid = (r0 + lax.broadcasted_iota(jnp.int32, (blk, 1), 0)) >= FRONT_PAD

        q_bf = q.astype(BF16)
        qs = jnp.concatenate([jnp.where(qhead == h, q_bf, jnp.zeros_like(q_bf))
                              for h in range(N_HEADS)], axis=0)
        a = (_dot_nt(qs, k.astype(BF16)) * dmask_ref[...]).astype(BF16)
        r = _dot(a, v)
        o = _dot((q * qfac_ref[...]).astype(BF16), s_ref[...].astype(BF16))
        for h in range(N_HEADS):
            o = o + jnp.where(ehead == h, r[h * blk:(h + 1) * blk, :], 0.0)

        kd_t = (k * kfac_ref[...]).T.astype(BF16)
        upd = _dot(kd_t, v)
        s_ref[...] = dec_ref[...] * s_ref[...] + jnp.where(bd, upd, 0.0)

        mu = _dot(o.astype(BF16), gsum) * (1.0 / 64)
        xc = o - mu
        var = _dot((xc * xc).astype(BF16), gsum) * (1.0 / 64)
        y = xc * lax.rsqrt(var + EPS) * gn * (og * _sigmoid(og))
        o_ref[0, rows, :] = jnp.where(valid, y, 0.0).astype(BF16)
        return carry

    lax.fori_loop(0, n_blocks, block, 0, unroll=2)


def _ret(pb, cos, sin, dmask, qfac, kfac, dec, gn):
    b, tp, _ = pb.shape
    return pl.pallas_call(
        functools.partial(_ret_kernel, tp),
        grid=(b,),
        in_specs=[pl.BlockSpec((1, tp, PB_W), lambda i: (i, 0, 0)),
                  _resident(cos.shape), _resident(sin.shape), _resident(dmask.shape),
                  _resident(qfac.shape), _resident(kfac.shape), _resident(dec.shape),
                  _resident((1, 256))],
        out_specs=pl.BlockSpec((1, tp, 256), lambda i: (i, 0, 0)),
        out_shape=jax.ShapeDtypeStruct((b, tp, 256), BF16),
        scratch_shapes=[pltpu.VMEM((256, 256), F32)],
        compiler_params=_cparams(),
        name="retention",
    )(pb, cos, sin, dmask, qfac, kfac, dec, gn)


ATT_ROWS = 256
LOG2E = 1.4426950408889634
V_ONE = 64


def _for_pairs(n, body):
    def pair(t, carry):
        body(2 * t)
        body(2 * t + 1)
        return carry

    lax.fori_loop(0, n >> 1, pair, 0)

    @pl.when((n & 1) == 1)
    def _():
        body(n - 1)


def _mask_groups(mask, s, n_g):
    rows = s.shape[0] // n_g
    return jnp.concatenate([jnp.where(mask, s[g * rows:(g + 1) * rows, :], NEG)
                            for g in range(n_g)], axis=0)


def _attn_block0(qk_fn, v_fn, n_g):
    rows = pl.ds(0, Q_BLOCK)
    qrow = lax.broadcasted_iota(jnp.int32, (Q_BLOCK, 1), 0)
    kcol = lax.broadcasted_iota(jnp.int32, (1, Q_BLOCK), 1)
    mask = jnp.logical_and(kcol <= qrow, kcol >= FRONT_PAD)
    s = _mask_groups(mask, qk_fn(0, Q_BLOCK, rows), n_g)
    p_bf = jnp.exp2(s - jnp.max(s, axis=-1, keepdims=True)).astype(BF16)
    pv = jnp.concatenate([_dot(p_bf[g * Q_BLOCK:(g + 1) * Q_BLOCK, :], v_fn(g, rows))
                          for g in range(n_g)], axis=0)
    return pv / pv[:, V_ONE:V_ONE + 1]


def _attn_block(i, qk_fn, v_fn, n_g, s_meta, s_s, mx_s, acc_s):
    rr = ATT_ROWS
    q0 = pl.multiple_of(Q_BLOCK + (i - 1) * rr, Q_BLOCK)
    meta_rows = pl.ds(0, Q_BLOCK)
    kcol = lax.broadcasted_iota(jnp.int32, (1, Q_BLOCK), 1)

    def key_rows(j):
        return pl.ds(pl.multiple_of(Q_BLOCK + j * rr, Q_BLOCK), rr)

    s = jnp.where(kcol >= FRONT_PAD, qk_fn(q0, rr, meta_rows), NEG)
    s_meta[...] = s
    mx_s[...] = s

    def pass1(j):
        sj = qk_fn(q0, rr, key_rows(j))
        s_s[j] = sj
        mx_s[...] = jnp.maximum(mx_s[...], jnp.maximum(sj[:, :LANES], sj[:, LANES:]))

    _for_pairs(i - 1, pass1)
    causal = (lax.broadcasted_iota(jnp.int32, (rr, rr), 1)
              <= lax.broadcasted_iota(jnp.int32, (rr, rr), 0))
    sd = _mask_groups(causal, qk_fn(q0, rr, pl.ds(q0, rr)), n_g)
    s_s[i - 1] = sd
    m = jnp.max(jnp.maximum(mx_s[...], jnp.maximum(sd[:, :LANES], sd[:, LANES:])),
                axis=-1, keepdims=True)
    mx_s[...] = jnp.broadcast_to(m, mx_s.shape)

    p_bf = jnp.exp2(s_meta[...] - mx_s[...]).astype(BF16)
    for g in range(n_g):
        acc_s[g * rr:(g + 1) * rr, :] = _dot(p_bf[g * rr:(g + 1) * rr, :], v_fn(g, meta_rows))

    def pass2(j):
        sj = s_s[j]
        mrep = mx_s[...]
        p0 = jnp.exp2(sj[:, :LANES] - mrep)
        p1 = jnp.exp2(sj[:, LANES:] - mrep)
        pj = jnp.concatenate([p0.astype(BF16), p1.astype(BF16)], axis=1)
        for g in range(n_g):
            acc_s[g * rr:(g + 1) * rr, :] += _dot(pj[g * rr:(g + 1) * rr, :], v_fn(g, key_rows(j)))

    _for_pairs(i, pass2)
    acc = acc_s[...]
    return acc / acc[:, V_ONE:V_ONE + 1]


def _heads_to_lanes(per_head):
    low = lax.broadcasted_iota(jnp.int32, (1, LANES), 1) < 64
    lo = jnp.where(low, per_head[0], pltpu.roll(per_head[1], 64, 1))
    hi = jnp.where(low, per_head[2], pltpu.roll(per_head[3], 64, 1))
    return jnp.concatenate([lo, hi], axis=1)


def _with_ones_lane(v):
    lane = lax.broadcasted_iota(jnp.int32, (1, LANES), 1)
    return jnp.where(lane == V_ONE, jnp.ones_like(v), v)


def _mla_kernel(tp, pc_ref, qn_ref, kvn_ref, wuq_ref, wukv_ref, cos_ref, sa_ref, sb_ref,
                o_ref, q_s, k_s, v_s, s_meta, s_s, mx_s, acc_s):
    n_blocks = tp // Q_BLOCK
    scale = (MLA_NOPE + MLA_ROPE) ** -0.5 * LOG2E
    is_q = lax.broadcasted_iota(jnp.int32, (1, 256), 1) < MLA_Q_RANK

    def prep(i, carry):
        r0 = pl.multiple_of(i * Q_BLOCK, Q_BLOCK)
        rows = pl.ds(r0, Q_BLOCK)
        x = pc_ref[0, rows, 0:256].astype(F32)
        x2 = x * x
        ms_q = jnp.sum(jnp.where(is_q, x2, 0.0), axis=-1, keepdims=True) * (1.0 / MLA_Q_RANK)
        ms_kv = jnp.sum(jnp.where(is_q, 0.0, x2), axis=-1, keepdims=True) * (1.0 / MLA_KV_RANK)
        yq = (x * lax.rsqrt(ms_q + EPS) * qn_ref[...]).astype(BF16)
        ykv = (x * lax.rsqrt(ms_kv + EPS) * kvn_ref[...]).astype(BF16)
        cq = _dot(yq, wuq_ref[...])
        kv = _dot(ykv, wukv_ref[...])
        cos = cos_ref[rows, :]
        sa = sa_ref[rows, :]
        sb = sb_ref[rows, :]

        def rope(t):
            return t * cos + pltpu.roll(t, 16, 1) * sa + pltpu.roll(t, LANES - 16, 1) * sb

        kpe = rope(pc_ref[0, rows, 256:384].astype(F32))
        for h in range(N_HEADS):
            q_s[h, rows, :] = (rope(cq[:, h * LANES:(h + 1) * LANES]) * scale).astype(BF16)
            k_s[h, rows, :] = (kv[:, h * LANES:(h + 1) * LANES] + kpe).astype(BF16)
            v_s[h, rows, :] = _with_ones_lane(
                kv[:, (N_HEADS + h) * LANES:(N_HEADS + h + 1) * LANES]).astype(BF16)
        return carry

    lax.fori_loop(0, n_blocks, prep, 0)

    def qk_fn(q0, n_rows, krows):
        return jnp.concatenate([_dot_nt(q_s[h, pl.ds(q0, n_rows), :], k_s[h, krows, :])
                                for h in range(N_HEADS)], axis=0)

    def v_fn(h, krows):
        return v_s[h, krows, :]

    def emit(q0, n_rows, o, first):
        y = _heads_to_lanes([o[h * n_rows:(h + 1) * n_rows, :] for h in range(N_HEADS)])
        if first:
            qrow = lax.broadcasted_iota(jnp.int32, (n_rows, 1), 0)
            y = jnp.where(qrow >= FRONT_PAD, y, 0.0)
        o_ref[0, pl.ds(q0, n_rows), :] = y.astype(BF16)

    emit(0, Q_BLOCK, _attn_block0(qk_fn, v_fn, N_HEADS), True)

    def qblock(i, carry):
        o = _attn_block(i, qk_fn, v_fn, N_HEADS, s_meta, s_s, mx_s, acc_s)
        emit(pl.multiple_of(Q_BLOCK + (i - 1) * ATT_ROWS, Q_BLOCK), ATT_ROWS, o, False)
        return carry

    lax.fori_loop(1, (tp - Q_BLOCK) // ATT_ROWS + 1, qblock, 0)


def _attn_scratch(n_g, tp):
    g_rows = n_g * ATT_ROWS
    n_slots = (tp - Q_BLOCK) // ATT_ROWS
    return [pltpu.VMEM((g_rows, LANES), F32),
            pltpu.VMEM((n_slots, g_rows, ATT_ROWS), F32),
            pltpu.VMEM((g_rows, LANES), F32),
            pltpu.VMEM((g_rows, LANES), F32)]


def _mla(pc, qn, kvn, wuq, wukv, cos, sa, sb):
    b, tp, _ = pc.shape
    return pl.pallas_call(
        functools.partial(_mla_kernel, tp),
        grid=(b,),
        in_specs=[pl.BlockSpec((1, tp, PC_W), lambda i: (i, 0, 0)),
                  _resident((1, 256)), _resident((1, 256)),
                  _resident(wuq.shape), _resident(wukv.shape),
                  _resident(cos.shape), _resident(sa.shape), _resident(sb.shape)],
        out_specs=pl.BlockSpec((1, tp, 256), lambda i: (i, 0, 0)),
        out_shape=jax.ShapeDtypeStruct((b, tp, 256), BF16),
        scratch_shapes=[pltpu.VMEM((N_HEADS, tp, LANES), BF16),
                        pltpu.VMEM((N_HEADS, tp, LANES), BF16),
                        pltpu.VMEM((N_HEADS, tp, LANES), BF16)] + _attn_scratch(N_HEADS, tp),
        compiler_params=_cparams(),
        name="mla",
    )(pc, qn, kvn, wuq, wukv, cos, sa, sb)


def _diff_kernel(tp, lam_init, pd_ref, lam_ref, dn_ref, o_ref, qs_s, v_s, s_meta, s_s, mx_s, acc_s):
    n_maps = 2 * N_HEADS

    def fill_values(i, carry):
        rows = pl.ds(pl.multiple_of(i * Q_BLOCK, Q_BLOCK), Q_BLOCK)
        for h in range(N_HEADS):
            v_s[h, rows, :] = _with_ones_lane(pd_ref[0, rows, 512 + h * LANES:512 + (h + 1) * LANES])
        return carry

    lax.fori_loop(0, tp // Q_BLOCK, fill_values, 0)
    scale = DIFF_DK ** -0.5 * LOG2E
    group = lax.broadcasted_iota(jnp.int32, (1, 256), 1) >> 5
    lv = lam_ref[...]
    lam = (jnp.exp(jnp.sum(lv[0:1, :] * lv[1:2, :], axis=-1, keepdims=True))
           - jnp.exp(jnp.sum(lv[2:3, :] * lv[3:4, :], axis=-1, keepdims=True)) + lam_init)
    dn = dn_ref[...]

    def stack_queries(q0, n_rows):
        q = (pd_ref[0, pl.ds(q0, n_rows), 0:256].astype(F32) * scale).astype(BF16)
        for g in range(n_maps):
            qs_s[g * n_rows:(g + 1) * n_rows, :] = jnp.where(group == g, q, jnp.zeros_like(q))

    def qk_fn(q0, n_rows, krows):
        return _dot_nt(qs_s[0:n_maps * n_rows, :], pd_ref[0, krows, 256:512])

    def v_fn(g, krows):
        return v_s[g // 2, krows, :]

    def emit(q0, n_rows, o, first):
        od = jnp.concatenate([o[(2 * h) * n_rows:(2 * h + 1) * n_rows, :]
                              - lam * o[(2 * h + 1) * n_rows:(2 * h + 2) * n_rows, :]
                              for h in range(N_HEADS)], axis=0)
        od = jnp.where(lax.broadcasted_iota(jnp.int32, (1, LANES), 1) < V_ONE, od, 0.0)
        ms = jnp.sum(od * od, axis=-1, keepdims=True) * (1.0 / 64)
        yh = od * lax.rsqrt(ms + EPS) * dn * (1.0 - lam_init)
        y = _heads_to_lanes([yh[h * n_rows:(h + 1) * n_rows, :] for h in range(N_HEADS)])
        if first:
            qrow = lax.broadcasted_iota(jnp.int32, (n_rows, 1), 0)
            y = jnp.where(qrow >= FRONT_PAD, y, 0.0)
        o_ref[0, pl.ds(q0, n_rows), :] = y.astype(BF16)

    stack_queries(0, Q_BLOCK)
    emit(0, Q_BLOCK, _attn_block0(qk_fn, v_fn, n_maps), True)

    def qblock(i, carry):
        q0 = pl.multiple_of(Q_BLOCK + (i - 1) * ATT_ROWS, Q_BLOCK)
        stack_queries(q0, ATT_ROWS)
        o = _attn_block(i, qk_fn, v_fn, n_maps, s_meta, s_s, mx_s, acc_s)
        emit(q0, ATT_ROWS, o, False)
        return carry

    lax.fori_loop(1, (tp - Q_BLOCK) // ATT_ROWS + 1, qblock, 0)


def _diff(pd, lam_rows, dn, lam_init):
    b, tp, _ = pd.shape
    n_maps = 2 * N_HEADS
    return pl.pallas_call(
        functools.partial(_diff_kernel, tp, lam_init),
        grid=(b,),
        in_specs=[pl.BlockSpec((1, tp, PD_W), lambda i: (i, 0, 0)),
                  _resident(lam_rows.shape), _resident((1, LANES))],
        out_specs=pl.BlockSpec((1, tp, 256), lambda i: (i, 0, 0)),
        out_shape=jax.ShapeDtypeStruct((b, tp, 256), BF16),
        scratch_shapes=[pltpu.VMEM((n_maps * ATT_ROWS, 256), BF16),
                        pltpu.VMEM((N_HEADS, tp, LANES), BF16)] + _attn_scratch(n_maps, tp),
        compiler_params=_cparams(),
        name="diffattn",
    )(pd, lam_rows, dn)


def _pad_cols(x, width):
    return jnp.pad(x, ((0, 0), (0, width - x.shape[1])))


def _rot_split(w):
    d = w.shape[0]
    return w.reshape(d, N_HEADS, 2, 32).transpose(0, 2, 1, 3).reshape(d, 256)


def _layout_w_in(w):
    sizes = (128, 128, 256, 16, 256, 256, 256, 256, 256, 192, 64, 32, 256, 256, 256)
    offs = [0]
    for s_ in sizes:
        offs.append(offs[-1] + s_)
    seg = [w[:, offs[i]:offs[i + 1]] for i in range(len(sizes))]
    (a_q, a_k, a_v, a_lr, a_og, r_q, r_k, r_v, r_og, c_cq, c_ckv, c_kpe, d_q, d_k, d_v) = seg
    d = w.shape[0]
    z = lambda n: jnp.zeros((d, n), w.dtype)
    dv_p = jnp.pad(d_v.reshape(d, N_HEADS, 64), ((0, 0), (0, 0), (0, 64))).reshape(d, 512)
    cols = [a_q, a_k, a_v, a_og, a_lr, z(112),
            _rot_split(r_q), _rot_split(r_k), r_v, r_og,
            c_cq, c_ckv, z(64), c_kpe, z(32),
            d_q, d_k, dv_p]
    return jnp.concatenate(cols, axis=1).astype(BF16)


def _tables(tp):
    pos = jnp.arange(tp, dtype=F32) - FRONT_PAD
    inv = ROPE_THETA ** (-jnp.arange(32, dtype=F32) / 32)
    ang = pos[:, None] * inv[None, :]
    ret_cos = jnp.tile(jnp.cos(ang), (1, N_HEADS))
    ret_sin = jnp.tile(jnp.sin(ang), (1, N_HEADS))
    inv16 = ROPE_THETA ** (-jnp.arange(16, dtype=F32) / 16)
    ang16 = pos[:, None] * inv16[None, :]
    c16, s16 = jnp.cos(ang16), jnp.sin(ang16)
    one = lambda n: jnp.ones((tp, n), F32)
    zero = lambda n: jnp.zeros((tp, n), F32)
    mla_cos = jnp.concatenate([one(64), c16, c16, one(32)], axis=1)
    mla_sa = jnp.concatenate([zero(80), s16, zero(32)], axis=1)
    mla_sb = jnp.concatenate([zero(64), -s16, zero(48)], axis=1)
    lg = jnp.log(1.0 - jnp.exp2(-5.0 - jnp.arange(N_HEADS, dtype=F32)))
    idx = jnp.arange(RET_BLOCK, dtype=F32)
    rel = idx[:, None] - idx[None, :]
    dmask = jnp.where(rel[None] >= 0, jnp.exp(rel[None] * lg[:, None, None]), 0.0)
    dmask = dmask.reshape(N_HEADS * RET_BLOCK, RET_BLOCK)
    lane_head = (jnp.arange(256) % 128) // 32
    qfac = jnp.exp((idx[:, None] + 1.0) * lg[lane_head][None, :])
    kfac = jnp.exp((RET_BLOCK - 1.0 - idx[:, None]) * lg[lane_head][None, :])
    dec = jnp.exp(RET_BLOCK * lg[lane_head])[:, None]
    return ret_cos, ret_sin, mla_cos, mla_sa, mla_sb, dmask, qfac, kfac, dec


def kernel(x, meta_tokens, attn_norm, w_in, gla_w_gate, gla_b_gate, gla_norm, ret_norm, mla_q_norm, mla_w_uq, mla_kv_norm, mla_w_ukv, diff_lambda, diff_norm, w_out, ffn_norm, ffn_w_gate, ffn_w_up, ffn_w_down, moe_router, moe_w_gate, moe_w_up, moe_w_down, final_norm):
    b, seq, d = x.shape
    tp = FRONT_PAD + N_META + seq
    n = b * tp
    meta = jnp.broadcast_to(meta_tokens[None].astype(x.dtype), (b, N_META, d))
    h = jnp.concatenate([jnp.zeros((b, FRONT_PAD, d), x.dtype), meta, x], axis=1).reshape(n, d)
    ret_cos, ret_sin, mla_cos, mla_sa, mla_sb, dmask, qfac, kfac, dec = _tables(tp)

    for li in range(DEPTH):
        pa, pb, pc, pd = _inproj(h, attn_norm[li][None, :], _layout_w_in(w_in[li]))
        pa, pb, pc, pd = (p.reshape(b, tp, -1) for p in (pa, pb, pc, pd))

        wgate = jnp.pad(gla_w_gate[li], ((0, LANES - GLA_GATE_RANK), (0, 0)))
        o_a = _gla(pa, wgate, gla_b_gate[li][None, :], jnp.tile(gla_norm[li], N_HEADS)[None, :])
        o_b = _ret(pb, ret_cos, ret_sin, dmask, qfac, kfac, dec, jnp.tile(ret_norm[li], N_HEADS)[None, :])

        qn = _pad_cols(mla_q_norm[li][None, :], 256)
        kvn = jnp.pad(mla_kv_norm[li][None, :], ((0, 0), (MLA_Q_RANK, 0)))
        wuq = jnp.pad(mla_w_uq[li].reshape(MLA_Q_RANK, N_HEADS, MLA_NOPE + MLA_ROPE),
                      ((0, 256 - MLA_Q_RANK), (0, 0), (0, LANES - MLA_NOPE - MLA_ROPE)))
        wuq = wuq.reshape(256, N_HEADS * LANES).astype(BF16)
        wukv = mla_w_ukv[li].reshape(MLA_KV_RANK, N_HEADS, 2, 64)
        wukv = jnp.pad(wukv, ((MLA_Q_RANK, 0), (0, 0), (0, 0), (0, 64)))
        wukv = wukv.transpose(0, 2, 1, 3).reshape(256, 2 * N_HEADS * LANES).astype(BF16)
        o_c = _mla(pc, qn, kvn, wuq, wukv, mla_cos, mla_sa, mla_sb)

        lam_init = 0.8 - 0.6 * math.exp(-0.3 * li)
        o_d = _diff(pd, diff_lambda[li], _pad_cols(diff_norm[li][None, :], LANES), lam_init)

        o_a, o_b, o_c, o_d = (o.reshape(n, 256) for o in (o_a, o_b, o_c, o_d))
        wo = w_out[li].astype(BF16)
        fn = ffn_norm[li][None, :]
        j = li // 2
        if li % 2 == 0:
            h, hn = _outproj(h, o_a, o_b, o_c, o_d, wo, fn)
            h = _swiglu(h, hn, ffn_w_gate[j].astype(BF16), ffn_w_up[j].astype(BF16),
                        ffn_w_down[j].astype(BF16))
        else:
            h, hn, info, cnt = _outproj(h, o_a, o_b, o_c, o_d, wo, fn,
                                        router=_pad_cols(moe_router[j], LANES))
            return _routed_moe_norm(h, hn, info, cnt, moe_w_gate[j].astype(BF16),
                                    moe_w_up[j].astype(BF16), moe_w_down[j].astype(BF16),
                                    final_norm[None, :], b, tp)
```

```python
import functools
import math

import jax
import jax.numpy as jnp
from jax import lax
from jax.experimental import pallas as pl
from jax.experimental.pallas import tpu as pltpu

F32 = jnp.float32
BF16 = jnp.bfloat16

D_MODEL = 1024
DEPTH = 2
N_META = 16
CHUNK = 64
Q_BLOCK = 128
FRONT_PAD = Q_BLOCK - N_META
EPS = 1e-6
NEG = -1e30
ROPE_THETA = 10000.0
N_HEADS = 4
GLA_DK = 32
GLA_DV = 64
GLA_GATE_RANK = 16
GLA_TAU = 16.0
RET_DK = 64
MLA_Q_RANK = 192
MLA_KV_RANK = 64
MLA_NOPE = 64
MLA_ROPE = 32
DIFF_DK = 32
D_FF = 2816
N_EXPERTS = 8
D_FF_EXPERT = 3584

LANES = 128
SUBLANES = 8
ROW_TILE = 512
FF_CHUNK = 256
VMEM_LIMIT = 56 * 1024 * 1024

PA_W = 896
PB_W = 1024
PC_W = 384
PD_W = 1024


def _cparams(n_axes=1):
    return pltpu.CompilerParams(dimension_semantics=("arbitrary",) * n_axes,
                                vmem_limit_bytes=VMEM_LIMIT)


def _resident(shape):
    nd = len(shape)
    return pl.BlockSpec(shape, lambda *_: (0,) * nd, pipeline_mode=pl.Buffered(1))


def _sigmoid(x):
    return 1.0 / (1.0 + jnp.exp(-x))


def _split_bf16(x):
    hi = x.astype(BF16)
    lo = (x - hi.astype(F32)).astype(BF16)
    return hi, lo


def _dot(a, b):
    return jnp.dot(a, b, preferred_element_type=F32)


def _dot_nt(a, b):
    return lax.dot_general(a, b, (((1,), (1,)), ((), ())), preferred_element_type=F32)


def _inproj_kernel(h_ref, g_ref, w_ref, pa_ref, pb_ref, pc_ref, pd_ref):
    x = h_ref[...]
    ms = jnp.mean(x * x, axis=-1, keepdims=True)
    y = (x * lax.rsqrt(ms + EPS) * g_ref[...]).astype(BF16)
    off = 0
    for o_ref, width in ((pa_ref, PA_W), (pb_ref, PB_W), (pc_ref, PC_W), (pd_ref, PD_W)):
        o_ref[...] = _dot(y, w_ref[:, off:off + width]).astype(BF16)
        off += width


def _inproj(h, g, w):
    n = h.shape[0]
    wtot = PA_W + PB_W + PC_W + PD_W
    row = lambda width: pl.BlockSpec((ROW_TILE, width), lambda i: (i, 0))
    return pl.pallas_call(
        _inproj_kernel,
        grid=(n // ROW_TILE,),
        in_specs=[row(D_MODEL), _resident((1, D_MODEL)), _resident((D_MODEL, wtot))],
        out_specs=[row(PA_W), row(PB_W), row(PC_W), row(PD_W)],
        out_shape=[jax.ShapeDtypeStruct((n, w_), BF16) for w_ in (PA_W, PB_W, PC_W, PD_W)],
        compiler_params=_cparams(),
        name="inproj",
    )(h, g, w)


def _outproj_kernel(with_router, h_ref, oa_ref, ob_ref, oc_ref, od_ref, wo_ref, fn_ref, *rest):
    if with_router:
        router_ref, hmid_ref, hn_ref, info_ref, cnt_ref, carry_ref = rest
    else:
        hmid_ref, hn_ref = rest
    o = jnp.concatenate([oa_ref[...], ob_ref[...], oc_ref[...], od_ref[...]], axis=1)
    hm = h_ref[...] + _dot(o, wo_ref[...])
    hmid_ref[...] = hm
    ms = jnp.mean(hm * hm, axis=-1, keepdims=True)
    y = hm * lax.rsqrt(ms + EPS) * fn_ref[...]
    hn_ref[...] = y.astype(hn_ref.dtype)
    if with_router:
        y_hi, y_lo = _split_bf16(y)
        r_hi, r_lo = _split_bf16(router_ref[...])
        logits = _dot(y_hi, r_hi) + _dot(y_hi, r_lo) + _dot(y_lo, r_hi)
        lane = lax.broadcasted_iota(jnp.int32, logits.shape, 1).astype(F32)
        ninf = float("-inf")
        logits = jnp.where(lane < N_EXPERTS, logits, ninf)
        m1 = jnp.max(logits, axis=-1, keepdims=True)
        i1 = jnp.min(jnp.where(logits == m1, lane, float(LANES)), axis=-1, keepdims=True)
        rest_l = jnp.where(lane == i1, ninf, logits)
        m2 = jnp.max(rest_l, axis=-1, keepdims=True)
        i2 = jnp.min(jnp.where(rest_l == m2, lane, float(LANES)), axis=-1, keepdims=True)
        e2 = jnp.exp(m2 - m1)
        den = 1.0 + e2
        @pl.when(pl.program_id(0) == 0)
        def _():
            carry_ref[...] = jnp.zeros_like(carry_ref)
        sel = jnp.where(lane == i1, 1.0, 0.0) + jnp.where(lane == i2, 1.0, 0.0)
        rows = sel.shape[0]
        below = (lax.broadcasted_iota(jnp.int32, (rows, rows), 1)
                 < lax.broadcasted_iota(jnp.int32, (rows, rows), 0))
        count = _dot(jnp.where(below, 1.0, 0.0).astype(BF16), sel.astype(BF16)) + carry_ref[...]
        r1 = jnp.sum(jnp.where(lane == i1, count, 0.0), axis=-1, keepdims=True)
        r2 = jnp.sum(jnp.where(lane == i2, count, 0.0), axis=-1, keepdims=True)
        total = carry_ref[...] + jnp.sum(sel, axis=0, keepdims=True)
        carry_ref[...] = total
        cnt_ref[...] = jnp.broadcast_to(total, cnt_ref.shape)
        info = jnp.where(lane == 0, i1, 0.0)
        for k, val in enumerate((i2, 1.0 / den, e2 / den, r1, r2), start=1):
            info = jnp.where(lane == k, val, info)
        info_ref[...] = info


def _outproj(h, oa, ob, oc, od, wo, fn, router=None):
    n = h.shape[0]
    row = lambda width: pl.BlockSpec((ROW_TILE, width), lambda i: (i, 0))
    in_specs = [row(D_MODEL), row(256), row(256), row(256), row(256),
                _resident((D_MODEL, D_MODEL)), _resident((1, D_MODEL))]
    out_specs = [row(D_MODEL), row(D_MODEL)]
    out_shape = [jax.ShapeDtypeStruct((n, D_MODEL), F32),
                 jax.ShapeDtypeStruct((n, D_MODEL), BF16 if router is None else F32)]
    args = [h, oa, ob, oc, od, wo, fn]
    scratch = []
    if router is not None:
        in_specs.append(_resident((D_MODEL, LANES)))
        out_specs += [row(LANES), pl.BlockSpec((8, LANES), lambda i: (0, 0))]
        out_shape += [jax.ShapeDtypeStruct((n, LANES), F32), jax.ShapeDtypeStruct((8, LANES), F32)]
        args.append(router)
        scratch.append(pltpu.VMEM((1, LANES), F32))
    return pl.pallas_call(
        functools.partial(_outproj_kernel, router is not None),
        grid=(n // ROW_TILE,),
        in_specs=in_specs, out_specs=out_specs, out_shape=out_shape,
        scratch_shapes=scratch,
        compiler_params=_cparams(),
        name="outproj_router" if router is not None else "outproj",
    )(*args)


def _dispatch_kernel(n_fill, zs_ref, ze_ref, pos_ref, x_ref, xs_ref, zero_buf, sem, zsem):
    def row_copy(rb, u, k):
        p = pos_ref[0, 0, rb * (2 * SUBLANES) + 2 * u + k]
        return pltpu.make_async_copy(x_ref.at[rb, pl.ds(u, 1), :], xs_ref.at[pl.ds(p, 1), :], sem)

    def start(rb, carry):
        for u in range(SUBLANES):
            row_copy(rb, u, 0).start(priority=0)
            row_copy(rb, u, 1).start(priority=1)
        return carry

    def wait(rb, carry):
        for u in range(SUBLANES):
            row_copy(rb, u, 0).wait()
            row_copy(rb, u, 1).wait()
        return carry

    lax.fori_loop(0, x_ref.shape[0], start, 0)
    lax.fori_loop(0, x_ref.shape[0], wait, 0)

    @pl.when(pl.program_id(0) == pl.num_programs(0) - 1)
    def _():
        zero_buf[...] = jnp.zeros_like(zero_buf)

        def zero_copy(r):
            return pltpu.make_async_copy(zero_buf.at[pl.ds(0, 1), :], xs_ref.at[pl.ds(r, 1), :], zsem)

        def fill(r, carry):
            zero_copy(r).start()
            return carry

        for k in range(zs_ref.shape[0]):
            lax.fori_loop(zs_ref[k], ze_ref[k], fill, 0)

        def drain(r, carry):
            zero_copy(0).wait()
            return carry

        lax.fori_loop(0, n_fill, drain, 0)


def _dispatch(zero_start, zero_end, pos, x, n_sorted):
    n, d = x.shape
    return pl.pallas_call(
        functools.partial(_dispatch_kernel, n_sorted - 2 * n),
        grid_spec=pltpu.PrefetchScalarGridSpec(
            num_scalar_prefetch=2,
            grid=(n // ROW_TILE,),
            in_specs=[pl.BlockSpec((1, 1, 2 * ROW_TILE), lambda i, zs, ze: (i, 0, 0),
                                   memory_space=pltpu.SMEM),
                      pl.BlockSpec((ROW_TILE // SUBLANES, SUBLANES, d), lambda i, zs, ze: (i, 0, 0))],
            out_specs=pl.BlockSpec(memory_space=pl.ANY),
            scratch_shapes=[pltpu.VMEM((SUBLANES, d), x.dtype), pltpu.SemaphoreType.DMA(()),
                            pltpu.SemaphoreType.DMA(())]),
        out_shape=jax.ShapeDtypeStruct((n_sorted, d), x.dtype),
        compiler_params=_cparams(),
        name="moe_dispatch",
    )(zero_start, zero_end, pos, x.reshape(n // SUBLANES, SUBLANES, d))


def _swiglu_tile(x, wg, wu, wd, acc_ref):
    n_chunks = wd.shape[0] // FF_CHUNK
    for c in range(n_chunks):
        cols = slice(c * FF_CHUNK, (c + 1) * FF_CHUNK)
        g = _dot(x, wg[:, cols])
        u = _dot(x, wu[:, cols])
        part = _dot((g * _sigmoid(g) * u).astype(BF16), wd[cols, :])
        if c == 0:
            acc_ref[...] = part
        elif c < n_chunks - 1:
            acc_ref[...] += part
    return acc_ref[...] + part


def _experts_kernel(te_ref, tv_ref, xs_ref, wg_ref, wu_ref, wd_ref, ys_ref, acc_ref):
    i = pl.program_id(0)

    @pl.when(tv_ref[i] > 0)
    def _():
        ys_ref[...] = _swiglu_tile(xs_ref[...].astype(BF16), wg_ref.at[0], wu_ref.at[0],
                                   wd_ref.at[0], acc_ref)

    @pl.when(tv_ref[i] == 0)
    def _():
        ys_ref[...] = jnp.zeros_like(ys_ref)


def _experts(tile_expert, tile_valid, xs, wg, wu, wd):
    n_sorted, d = xs.shape
    wspec = lambda shape: pl.BlockSpec((1,) + shape[1:], lambda i, te, tv: (te[i], 0, 0),
                                       pipeline_mode=pl.Buffered(1))
    row = pl.BlockSpec((ROW_TILE, d), lambda i, te, tv: (i, 0))
    return pl.pallas_call(
        _experts_kernel,
        grid_spec=pltpu.PrefetchScalarGridSpec(
            num_scalar_prefetch=2,
            grid=(n_sorted // ROW_TILE,),
            in_specs=[row, wspec(wg.shape), wspec(wu.shape), wspec(wd.shape)],
            out_specs=row,
            scratch_shapes=[pltpu.VMEM((ROW_TILE, d), F32)]),
        out_shape=jax.ShapeDtypeStruct((n_sorted, d), F32),
        compiler_params=_cparams(),
        name="moe_experts",
    )(tile_expert, tile_valid, xs, wg, wu, wd)


def _combine_norm_kernel(tp, pos_ref, h_ref, info_ref, ys_ref, g_ref, o_ref,
                         h_buf, info_buf, y_buf, sem, row_sem):
    n_rows = h_buf.shape[0]
    row0 = pl.multiple_of(pl.program_id(0) * tp + Q_BLOCK + pl.program_id(1) * n_rows, Q_BLOCK)
    h_copy = pltpu.make_async_copy(h_ref.at[pl.ds(row0, n_rows), :], h_buf, sem.at[0])
    info_copy = pltpu.make_async_copy(info_ref.at[pl.ds(row0, n_rows), :], info_buf, sem.at[1])
    h_copy.start()
    info_copy.start()

    def row_copy(rb, u, k):
        p = pos_ref[0, 0, rb * (2 * SUBLANES) + 2 * u + k]
        return pltpu.make_async_copy(ys_ref.at[pl.ds(p, 1), :], y_buf.at[k, rb, pl.ds(u, 1), :], row_sem)

    def start(rb, carry):
        for u in range(SUBLANES):
            row_copy(rb, u, 0).start(priority=0)
            row_copy(rb, u, 1).start(priority=1)
        return carry

    def wait(rb, carry):
        for u in range(SUBLANES):
            row_copy(rb, u, 0).wait()
            row_copy(rb, u, 1).wait()
        return carry

    lax.fori_loop(0, n_rows // SUBLANES, start, 0)
    lax.fori_loop(0, n_rows // SUBLANES, wait, 0)
    h_copy.wait()
    info_copy.wait()
    info = info_buf[...]
    y0 = y_buf[0].reshape(n_rows, -1)
    y1 = y_buf[1].reshape(n_rows, -1)
    x = h_buf[...] + info[:, 2:3] * y0 + info[:, 3:4] * y1
    ms = jnp.mean(x * x, axis=-1, keepdims=True)
    o_ref[0] = x * lax.rsqrt(ms + EPS) * g_ref[...]


def _combine_norm(pos, h, info, ys, g, b, tp):
    n, d = h.shape
    seq = tp - Q_BLOCK
    tiles = seq // ROW_TILE
    any_spec = pl.BlockSpec(memory_space=pl.ANY)
    return pl.pallas_call(
        functools.partial(_combine_norm_kernel, tp),
        grid=(b, tiles),
        in_specs=[pl.BlockSpec((1, 1, 2 * ROW_TILE), lambda i, j: (i * tiles + j, 0, 0),
                               memory_space=pltpu.SMEM),
                  any_spec, any_spec, any_spec, _resident((1, d))],
        out_specs=pl.BlockSpec((1, ROW_TILE, d), lambda i, j: (i, j, 0)),
        out_shape=jax.ShapeDtypeStruct((b, seq, d), F32),
        scratch_shapes=[pltpu.VMEM((ROW_TILE, d), F32), pltpu.VMEM((ROW_TILE, LANES), F32),
                        pltpu.VMEM((2, ROW_TILE // SUBLANES, SUBLANES, d), F32),
                        pltpu.SemaphoreType.DMA((2,)), pltpu.SemaphoreType.DMA(())],
        compiler_params=_cparams(2),
        name="moe_combine_norm",
    )(pos, h, info, ys, g)


def _routed_moe_norm(h, hn, info, cnt, wg, wu, wd, g, b, tp):
    n = h.shape[0]
    n_sorted = 2 * n + N_EXPERTS * ROW_TILE
    n_tiles = n_sorted // ROW_TILE
    counts = cnt[0, :N_EXPERTS].astype(jnp.int32)
    tiles_e = (counts + ROW_TILE - 1) // ROW_TILE
    tile_end = jnp.cumsum(tiles_e)
    row_base = (tile_end - tiles_e) * ROW_TILE
    e_idx = info[:, 0:2].astype(jnp.int32)
    pos = row_base[e_idx] + info[:, 4:6].astype(jnp.int32)
    tile_ids = jnp.arange(n_tiles, dtype=jnp.int32)
    tile_expert = jnp.minimum(jnp.sum((tile_ids[:, None] >= tile_end[None, :]).astype(jnp.int32), axis=1),
                              N_EXPERTS - 1)
    tile_valid = (tile_ids < tile_end[-1]).astype(jnp.int32)
    zero_start = jnp.concatenate([row_base + counts, tile_end[-1:] * ROW_TILE])
    zero_end = jnp.concatenate([tile_end * ROW_TILE, jnp.full((1,), n_sorted, jnp.int32)])
    xs = _dispatch(zero_start, zero_end, pos.reshape(n // ROW_TILE, 1, 2 * ROW_TILE), hn, n_sorted)
    ys = _experts(tile_expert, tile_valid, xs, wg, wu, wd)
    pos_seq = pos.reshape(b, tp, 2)[:, Q_BLOCK:, :].reshape(-1, 1, 2 * ROW_TILE)
    return _combine_norm(pos_seq, h, info, ys, g, b, tp)


def _swiglu_kernel(h_ref, hn_ref, wg_ref, wu_ref, wd_ref, o_ref, acc_ref):
    o_ref[...] = h_ref[...] + _swiglu_tile(hn_ref[...], wg_ref, wu_ref, wd_ref, acc_ref)


def _swiglu(h, hn, wg, wu, wd):
    n = h.shape[0]
    row = pl.BlockSpec((ROW_TILE, D_MODEL), lambda i: (i, 0))
    return pl.pallas_call(
        _swiglu_kernel,
        grid=(n // ROW_TILE,),
        in_specs=[row, row, _resident(wg.shape), _resident(wu.shape), _resident(wd.shape)],
        out_specs=row,
        out_shape=jax.ShapeDtypeStruct((n, D_MODEL), F32),
        scratch_shapes=[pltpu.VMEM((ROW_TILE, D_MODEL), F32)],
        input_output_aliases={0: 0},
        compiler_params=_cparams(),
        name="swiglu",
    )(h, hn, wg, wu, wd)


def _group_ones(n, group_shift):
    r = lax.broadcasted_iota(jnp.int32, (n, n), 0) >> group_shift
    c = lax.broadcasted_iota(jnp.int32, (n, n), 1) >> group_shift
    return jnp.where(r == c, 1.0, 0.0).astype(BF16)


GLA_ROWS = 256


def _gla_kernel(tp, pa_ref, wg_ref, bg_ref, gn_ref, o_ref, s_ref):
    qhead = lax.broadcasted_iota(jnp.int32, (1, N_HEADS * GLA_DK), 1) >> 5
    ehead = lax.broadcasted_iota(jnp.int32, (1, N_HEADS * GLA_DV), 1) >> 6
    shead = lax.broadcasted_iota(jnp.int32, (N_HEADS * GLA_DK, 1), 0) >> 5
    bd = shead == ehead
    gsum = _group_ones(N_HEADS * GLA_DV, 6)
    wg_hi, wg_lo = _split_bf16(wg_ref[...])
    bg = bg_ref[...]
    gn = gn_ref[...]
    scale = GLA_DK ** -0.5
    s_ref[...] = jnp.zeros_like(s_ref)

    def block(r0, n_rows):
        n_c = n_rows // CHUNK
        rows = pl.ds(r0, n_rows)
        ri = lax.broadcasted_iota(jnp.int32, (n_rows, n_rows), 0)
        ci = lax.broadcasted_iota(jnp.int32, (n_rows, n_rows), 1)
        same = (ri >> 6) == (ci >> 6)
        tri_bf = jnp.where(jnp.logical_and(same, ri >= ci), 1.0, 0.0).astype(BF16)
        ones_bf = jnp.where(same, 1.0, 0.0).astype(BF16)
        r4 = lax.broadcasted_iota(jnp.int32, (N_HEADS * n_rows, n_rows), 0) & (n_rows - 1)
        c4 = lax.broadcasted_iota(jnp.int32, (N_HEADS * n_rows, n_rows), 1)
        tri4 = jnp.logical_and((r4 >> 6) == (c4 >> 6), r4 >= c4)
        col_chunk = lax.broadcasted_iota(jnp.int32, (1, n_rows), 1) >> 6

        q = pa_ref[0, rows, 0:128].astype(F32) * scale
        k = pa_ref[0, rows, 128:256].astype(F32)
        v = pa_ref[0, rows, 256:512]
        og = pa_ref[0, rows, 512:768].astype(F32)
        lr = pa_ref[0, rows, 768:896]
        valid = (r0 + lax.broadcasted_iota(jnp.int32, (n_rows, 1), 0)) >= FRONT_PAD

        pre = _dot(lr, wg_hi) + _dot(lr, wg_lo) + bg
        logsig = jnp.minimum(pre, 0.0) - jnp.log1p(jnp.exp(-jnp.abs(pre)))
        g = jnp.where(valid, logsig * (1.0 / GLA_TAU), 0.0)
        g_hi, g_lo = _split_bf16(g)
        cum = _dot(tri_bf, g_hi) + _dot(tri_bf, g_lo)
        cum_end = _dot(ones_bf, g_hi) + _dot(ones_bf, g_lo)
        qt_bf = (q * jnp.exp(cum)).astype(BF16)
        kt = (k * jnp.exp(-cum)).astype(BF16)
        kd_t = (k * jnp.exp(cum_end - cum)).T.astype(BF16)
        dec_t = jnp.exp(cum_end.T)

        qs = jnp.concatenate([jnp.where(qhead == h, qt_bf, jnp.zeros_like(qt_bf))
                              for h in range(N_HEADS)], axis=0)
        a = jnp.where(tri4, _dot_nt(qs, kt), 0.0).astype(BF16)
        r = _dot(a, v)
        o = jnp.where(ehead == 0, r[0:n_rows, :], 0.0)
        for h in range(1, N_HEADS):
            o = o + jnp.where(ehead == h, r[h * n_rows:(h + 1) * n_rows, :], 0.0)

        s = s_ref[...]
        inter = []
        for c in range(n_c):
            inter.append(_dot(qt_bf[c * CHUNK:(c + 1) * CHUNK, :], s.astype(BF16)))
            upd = _dot(jnp.where(col_chunk == c, kd_t, jnp.zeros_like(kd_t)), v)
            s = dec_t[:, c * CHUNK:c * CHUNK + 1] * s + jnp.where(bd, upd, 0.0)
        s_ref[...] = s
        o = o + jnp.concatenate(inter, axis=0)

        ms = _dot((o * o).astype(BF16), gsum) * (1.0 / GLA_DV)
        y = o * lax.rsqrt(ms + EPS) * gn * (og * _sigmoid(og))
        o_ref[0, rows, :] = jnp.where(valid, y, 0.0).astype(BF16)

    block(0, Q_BLOCK)

    def body(i, carry):
        block(pl.multiple_of(Q_BLOCK + i * GLA_ROWS, Q_BLOCK), GLA_ROWS)
        return carry

    lax.fori_loop(0, (tp - Q_BLOCK) // GLA_ROWS, body, 0)


def _gla(pa, wg, bg, gn):
    b, tp, _ = pa.shape
    return pl.pallas_call(
        functools.partial(_gla_kernel, tp),
        grid=(b,),
        in_specs=[pl.BlockSpec((1, tp, PA_W), lambda i: (i, 0, 0)),
                  _resident((LANES, LANES)), _resident((1, LANES)), _resident((1, 256))],
        out_specs=pl.BlockSpec((1, tp, 256), lambda i: (i, 0, 0)),
        out_shape=jax.ShapeDtypeStruct((b, tp, 256), BF16),
        scratch_shapes=[pltpu.VMEM((N_HEADS * GLA_DK, N_HEADS * GLA_DV), F32)],
        compiler_params=_cparams(),
        name="gla",
    )(pa, wg, bg, gn)


RET_BLOCK = 128


def _ret_kernel(tp, pb_ref, cos_ref, sin_ref, dmask_ref, qfac_ref, kfac_ref, dec_ref, gn_ref,
                o_ref, s_ref):
    blk = RET_BLOCK
    n_blocks = tp // blk
    qhead = (lax.broadcasted_iota(jnp.int32, (1, 256), 1) & 127) >> 5
    ehead = lax.broadcasted_iota(jnp.int32, (1, 256), 1) >> 6
    shead = (lax.broadcasted_iota(jnp.int32, (256, 1), 0) & 127) >> 5
    bd = shead == ehead
    gsum = _group_ones(256, 6)
    gn = gn_ref[...]
    s_ref[...] = jnp.zeros_like(s_ref)

    def rope(x, cos, sin):
        x1, x2 = x[:, :128], x[:, 128:]
        return jnp.concatenate([x1 * cos - x2 * sin, x1 * sin + x2 * cos], axis=1)

    def block(j, carry):
        r0 = pl.multiple_of(j * blk, blk)
        rows = pl.ds(r0, blk)
        cos = cos_ref[rows, :]
        sin = sin_ref[rows, :]
        q = rope(pb_ref[0, rows, 0:256].astype(F32), cos, sin)
        k = rope(pb_ref[0, rows, 256:512].astype(F32), cos, sin) * (RET_DK ** -0.5)
        v = pb_ref[0, rows, 512:768]
        og = pb_ref[0, rows, 768:1024].astype(F32)
        valid = (r0 + lax.broadcasted_iota(jnp.int32, (blk, 1), 0)) >= FRONT_PAD

        q_bf = q.astype(BF16)
        qs = jnp.concatenate([jnp.where(qhead == h, q_bf, jnp.zeros_like(q_bf))
                              for h in range(N_HEADS)], axis=0)
        a = (_dot_nt(qs, k.astype(BF16)) * dmask_ref[...]).astype(BF16)
        r = _dot(a, v)
        o = _dot((q * qfac_ref[...]).astype(BF16), s_ref[...].astype(BF16))
        for h in range(N_HEADS):
            o = o + jnp.where(ehead == h, r[h * blk:(h + 1) * blk, :], 0.0)

        kd_t = (k * kfac_ref[...]).T.astype(BF16)
        upd = _dot(kd_t, v)
        s_ref[...] = dec_ref[...] * s_ref[...] + jnp.where(bd, upd, 0.0)

        mu = _dot(o.astype(BF16), gsum) * (1.0 / 64)
        xc = o - mu
        var = _dot((xc * xc).astype(BF16), gsum) * (1.0 / 64)
        y = xc * lax.rsqrt(var + EPS) * gn * (og * _sigmoid(og))
        o_ref[0, rows, :] = jnp.where(valid, y, 0.0).astype(BF16)
        return carry

    lax.fori_loop(0, n_blocks, block, 0, unroll=2)


def _ret(pb, cos, sin, dmask, qfac, kfac, dec, gn):
    b, tp, _ = pb.shape
    return pl.pallas_call(
        functools.partial(_ret_kernel, tp),
        grid=(b,),
        in_specs=[pl.BlockSpec((1, tp, PB_W), lambda i: (i, 0, 0)),
                  _resident(cos.shape), _resident(sin.shape), _resident(dmask.shape),
                  _resident(qfac.shape), _resident(kfac.shape), _resident(dec.shape),
                  _resident((1, 256))],
        out_specs=pl.BlockSpec((1, tp, 256), lambda i: (i, 0, 0)),
        out_shape=jax.ShapeDtypeStruct((b, tp, 256), BF16),
        scratch_shapes=[pltpu.VMEM((256, 256), F32)],
        compiler_params=_cparams(),
        name="retention",
    )(pb, cos, sin, dmask, qfac, kfac, dec, gn)


ATT_ROWS = 256
LOG2E = 1.4426950408889634
V_ONE = 64


def _aligned(x, m):
    return x if isinstance(x, int) else pl.multiple_of(x, m)


def _for_pairs(n, body):
    if isinstance(n, int):
        for j in range(n):
            body(j)
        return

    def pair(t, carry):
        body(2 * t)
        body(2 * t + 1)
        return carry

    lax.fori_loop(0, n >> 1, pair, 0)

    @pl.when((n & 1) == 1)
    def _():
        body(n - 1)


def _mask_groups(mask, s, n_g):
    rows = s.shape[0] // n_g
    return jnp.concatenate([jnp.where(mask, s[g * rows:(g + 1) * rows, :], NEG)
                            for g in range(n_g)], axis=0)


def _attn_block0(qk_fn, v_fn, n_g):
    rows = pl.ds(0, Q_BLOCK)
    qrow = lax.broadcasted_iota(jnp.int32, (Q_BLOCK, 1), 0)
    kcol = lax.broadcasted_iota(jnp.int32, (1, Q_BLOCK), 1)
    mask = jnp.logical_and(kcol <= qrow, kcol >= FRONT_PAD)
    s = _mask_groups(mask, qk_fn(0, Q_BLOCK, rows), n_g)
    p_bf = jnp.exp2(s - jnp.max(s, axis=-1, keepdims=True)).astype(BF16)
    pv = jnp.concatenate([_dot(p_bf[g * Q_BLOCK:(g + 1) * Q_BLOCK, :], v_fn(g, rows))
                          for g in range(n_g)], axis=0)
    return pv / pv[:, V_ONE:V_ONE + 1]


def _attn_row0(i):
    return _aligned(Q_BLOCK + (i - 1) * ATT_ROWS, Q_BLOCK)


def _attn_result(acc_s):
    acc = acc_s[...]
    return acc / acc[:, V_ONE:V_ONE + 1]


def _attn_block(i, qk_fn, v_fn, n_g, s_meta, s_s, mx_s, acc_s, mid_fn=None):
    rr = ATT_ROWS
    q0 = _attn_row0(i)
    meta_rows = pl.ds(0, Q_BLOCK)
    kcol = lax.broadcasted_iota(jnp.int32, (1, Q_BLOCK), 1)

    def key_rows(j):
        return pl.ds(_aligned(Q_BLOCK + j * rr, Q_BLOCK), rr)

    s = jnp.where(kcol >= FRONT_PAD, qk_fn(q0, rr, meta_rows), NEG)
    s_meta[...] = s
    mx_s[...] = s

    def pass1(j):
        sj = qk_fn(q0, rr, key_rows(j))
        s_s[j] = sj
        mx_s[...] = jnp.maximum(mx_s[...], jnp.maximum(sj[:, :LANES], sj[:, LANES:]))

    _for_pairs(i - 1, pass1)
    if mid_fn is not None:
        mid_fn()
    causal = (lax.broadcasted_iota(jnp.int32, (rr, rr), 1)
              <= lax.broadcasted_iota(jnp.int32, (rr, rr), 0))
    sd = _mask_groups(causal, qk_fn(q0, rr, pl.ds(q0, rr)), n_g)
    s_s[i - 1] = sd
    m = jnp.max(jnp.maximum(mx_s[...], jnp.maximum(sd[:, :LANES], sd[:, LANES:])),
                axis=-1, keepdims=True)
    mx_s[...] = jnp.broadcast_to(m, mx_s.shape)

    p_bf = jnp.exp2(s_meta[...] - mx_s[...]).astype(BF16)
    for g in range(n_g):
        acc_s[g * rr:(g + 1) * rr, :] = _dot(p_bf[g * rr:(g + 1) * rr, :], v_fn(g, meta_rows))

    def pass2(j):
        sj = s_s[j]
        mrep = mx_s[...]
        p0 = jnp.exp2(sj[:, :LANES] - mrep)
        p1 = jnp.exp2(sj[:, LANES:] - mrep)
        pj = jnp.concatenate([p0.astype(BF16), p1.astype(BF16)], axis=1)
        for g in range(n_g):
            acc_s[g * rr:(g + 1) * rr, :] += _dot(pj[g * rr:(g + 1) * rr, :], v_fn(g, key_rows(j)))

    _for_pairs(i, pass2)


def _attn_all_blocks(n_q, block_fn, finish_fn):
    block_fn(1, None)

    def body(i, carry):
        block_fn(i, lambda: finish_fn(i - 1))
        return carry

    lax.fori_loop(2, n_q + 1, body, 0)
    finish_fn(n_q)


def _heads_to_lanes(per_head):
    low = lax.broadcasted_iota(jnp.int32, (1, LANES), 1) < 64
    lo = jnp.where(low, per_head[0], pltpu.roll(per_head[1], 64, 1))
    hi = jnp.where(low, per_head[2], pltpu.roll(per_head[3], 64, 1))
    return jnp.concatenate([lo, hi], axis=1)


def _with_ones_lane(v):
    lane = lax.broadcasted_iota(jnp.int32, (1, LANES), 1)
    return jnp.where(lane == V_ONE, jnp.ones_like(v), v)


def _mla_kernel(tp, pc_ref, qn_ref, kvn_ref, wuq_ref, wukv_ref, cos_ref, sa_ref, sb_ref,
                o_ref, q_s, k_s, v_s, s_meta, s_s, mx_s, acc_s):
    n_blocks = tp // Q_BLOCK
    scale = (MLA_NOPE + MLA_ROPE) ** -0.5 * LOG2E
    is_q = lax.broadcasted_iota(jnp.int32, (1, 256), 1) < MLA_Q_RANK

    def prep(i, carry):
        r0 = pl.multiple_of(i * Q_BLOCK, Q_BLOCK)
        rows = pl.ds(r0, Q_BLOCK)
        x = pc_ref[0, rows, 0:256].astype(F32)
        x2 = x * x
        ms_q = jnp.sum(jnp.where(is_q, x2, 0.0), axis=-1, keepdims=True) * (1.0 / MLA_Q_RANK)
        ms_kv = jnp.sum(jnp.where(is_q, 0.0, x2), axis=-1, keepdims=True) * (1.0 / MLA_KV_RANK)
        yq = (x * lax.rsqrt(ms_q + EPS) * qn_ref[...]).astype(BF16)
        ykv = (x * lax.rsqrt(ms_kv + EPS) * kvn_ref[...]).astype(BF16)
        cq = _dot(yq, wuq_ref[...])
        kv = _dot(ykv, wukv_ref[...])
        cos = cos_ref[rows, :]
        sa = sa_ref[rows, :]
        sb = sb_ref[rows, :]

        def rope(t):
            return t * cos + pltpu.roll(t, 16, 1) * sa + pltpu.roll(t, LANES - 16, 1) * sb

        kpe = rope(pc_ref[0, rows, 256:384].astype(F32))
        for h in range(N_HEADS):
            q_s[h, rows, :] = (rope(cq[:, h * LANES:(h + 1) * LANES]) * scale).astype(BF16)
            k_s[h, rows, :] = (kv[:, h * LANES:(h + 1) * LANES] + kpe).astype(BF16)
            v_s[h, rows, :] = _with_ones_lane(
                kv[:, (N_HEADS + h) * LANES:(N_HEADS + h + 1) * LANES]).astype(BF16)
        return carry

    lax.fori_loop(0, n_blocks, prep, 0)

    def qk_fn(q0, n_rows, krows):
        return jnp.concatenate([_dot_nt(q_s[h, pl.ds(q0, n_rows), :], k_s[h, krows, :])
                                for h in range(N_HEADS)], axis=0)

    def v_fn(h, krows):
        return v_s[h, krows, :]

    def emit(q0, n_rows, o, first):
        y = _heads_to_lanes([o[h * n_rows:(h + 1) * n_rows, :] for h in range(N_HEADS)])
        if first:
            qrow = lax.broadcasted_iota(jnp.int32, (n_rows, 1), 0)
            y = jnp.where(qrow >= FRONT_PAD, y, 0.0)
        o_ref[0, pl.ds(q0, n_rows), :] = y.astype(BF16)

    emit(0, Q_BLOCK, _attn_block0(qk_fn, v_fn, N_HEADS), True)

    def block(i, mid_fn):
        _attn_block(i, qk_fn, v_fn, N_HEADS, s_meta, s_s, mx_s, acc_s, mid_fn)

    def finish(i):
        emit(_attn_row0(i), ATT_ROWS, _attn_result(acc_s), False)

    _attn_all_blocks((tp - Q_BLOCK) // ATT_ROWS, block, finish)


def _attn_scratch(n_g, tp):
    g_rows = n_g * ATT_ROWS
    n_slots = (tp - Q_BLOCK) // ATT_ROWS
    return [pltpu.VMEM((g_rows, LANES), F32),
            pltpu.VMEM((n_slots, g_rows, ATT_ROWS), F32),
            pltpu.VMEM((g_rows, LANES), F32),
            pltpu.VMEM((g_rows, LANES), F32)]


def _mla(pc, qn, kvn, wuq, wukv, cos, sa, sb):
    b, tp, _ = pc.shape
    return pl.pallas_call(
        functools.partial(_mla_kernel, tp),
        grid=(b,),
        in_specs=[pl.BlockSpec((1, tp, PC_W), lambda i: (i, 0, 0)),
                  _resident((1, 256)), _resident((1, 256)),
                  _resident(wuq.shape), _resident(wukv.shape),
                  _resident(cos.shape), _resident(sa.shape), _resident(sb.shape)],
        out_specs=pl.BlockSpec((1, tp, 256), lambda i: (i, 0, 0)),
        out_shape=jax.ShapeDtypeStruct((b, tp, 256), BF16),
        scratch_shapes=[pltpu.VMEM((N_HEADS, tp, LANES), BF16),
                        pltpu.VMEM((N_HEADS, tp, LANES), BF16),
                        pltpu.VMEM((N_HEADS, tp, LANES), BF16)] + _attn_scratch(N_HEADS, tp),
        compiler_params=_cparams(),
        name="mla",
    )(pc, qn, kvn, wuq, wukv, cos, sa, sb)


def _diff_kernel(tp, lam_init, pd_ref, lam_ref, dn_ref, o_ref, qs_s, v_s, s_meta, s_s, mx_s, acc_s):
    n_maps = 2 * N_HEADS

    def fill_values(i, carry):
        rows = pl.ds(pl.multiple_of(i * Q_BLOCK, Q_BLOCK), Q_BLOCK)
        for h in range(N_HEADS):
            v_s[h, rows, :] = _with_ones_lane(pd_ref[0, rows, 512 + h * LANES:512 + (h + 1) * LANES])
        return carry

    lax.fori_loop(0, tp // Q_BLOCK, fill_values, 0)
    scale = DIFF_DK ** -0.5 * LOG2E
    group = lax.broadcasted_iota(jnp.int32, (1, 256), 1) >> 5
    lv = lam_ref[...]
    lam = (jnp.exp(jnp.sum(lv[0:1, :] * lv[1:2, :], axis=-1, keepdims=True))
           - jnp.exp(jnp.sum(lv[2:3, :] * lv[3:4, :], axis=-1, keepdims=True)) + lam_init)
    dn = dn_ref[...]

    def stack_queries(q0, n_rows):
        q = (pd_ref[0, pl.ds(q0, n_rows), 0:256].astype(F32) * scale).astype(BF16)
        for g in range(n_maps):
            qs_s[g * n_rows:(g + 1) * n_rows, :] = jnp.where(group == g, q, jnp.zeros_like(q))

    def qk_fn(q0, n_rows, krows):
        return _dot_nt(qs_s[0:n_maps * n_rows, :], pd_ref[0, krows, 256:512])

    def v_fn(g, krows):
        return v_s[g // 2, krows, :]

    def emit(q0, n_rows, o, first):
        od = jnp.concatenate([o[(2 * h) * n_rows:(2 * h + 1) * n_rows, :]
                              - lam * o[(2 * h + 1) * n_rows:(2 * h + 2) * n_rows, :]
                              for h in range(N_HEADS)], axis=0)
        od = jnp.where(lax.broadcasted_iota(jnp.int32, (1, LANES), 1) < V_ONE, od, 0.0)
        ms = jnp.sum(od * od, axis=-1, keepdims=True) * (1.0 / 64)
        yh = od * lax.rsqrt(ms + EPS) * dn * (1.0 - lam_init)
        y = _heads_to_lanes([yh[h * n_rows:(h + 1) * n_rows, :] for h in range(N_HEADS)])
        if first:
            qrow = lax.broadcasted_iota(jnp.int32, (n_rows, 1), 0)
            y = jnp.where(qrow >= FRONT_PAD, y, 0.0)
        o_ref[0, pl.ds(q0, n_rows), :] = y.astype(BF16)

    stack_queries(0, Q_BLOCK)
    emit(0, Q_BLOCK, _attn_block0(qk_fn, v_fn, n_maps), True)

    def block(i, mid_fn):
        stack_queries(_attn_row0(i), ATT_ROWS)
        _attn_block(i, qk_fn, v_fn, n_maps, s_meta, s_s, mx_s, acc_s, mid_fn)

    def finish(i):
        emit(_attn_row0(i), ATT_ROWS, _attn_result(acc_s), False)

    _attn_all_blocks((tp - Q_BLOCK) // ATT_ROWS, block, finish)


def _diff(pd, lam_rows, dn, lam_init):
    b, tp, _ = pd.shape
    n_maps = 2 * N_HEADS
    return pl.pallas_call(
        functools.partial(_diff_kernel, tp, lam_init),
        grid=(b,),
        in_specs=[pl.BlockSpec((1, tp, PD_W), lambda i: (i, 0, 0)),
                  _resident(lam_rows.shape), _resident((1, LANES))],
        out_specs=pl.BlockSpec((1, tp, 256), lambda i: (i, 0, 0)),
        out_shape=jax.ShapeDtypeStruct((b, tp, 256), BF16),
        scratch_shapes=[pltpu.VMEM((n_maps * ATT_ROWS, 256), BF16),
                        pltpu.VMEM((N_HEADS, tp, LANES), BF16)] + _attn_scratch(n_maps, tp),
        compiler_params=_cparams(),
        name="diffattn",
    )(pd, lam_rows, dn)


def _pad_cols(x, width):
    return jnp.pad(x, ((0, 0), (0, width - x.shape[1])))


def _rot_split(w):
    d = w.shape[0]
    return w.reshape(d, N_HEADS, 2, 32).transpose(0, 2, 1, 3).reshape(d, 256)


def _layout_w_in(w):
    sizes = (128, 128, 256, 16, 256, 256, 256, 256, 256, 192, 64, 32, 256, 256, 256)
    offs = [0]
    for s_ in sizes:
        offs.append(offs[-1] + s_)
    seg = [w[:, offs[i]:offs[i + 1]] for i in range(len(sizes))]
    (a_q, a_k, a_v, a_lr, a_og, r_q, r_k, r_v, r_og, c_cq, c_ckv, c_kpe, d_q, d_k, d_v) = seg
    d = w.shape[0]
    z = lambda n: jnp.zeros((d, n), w.dtype)
    dv_p = jnp.pad(d_v.reshape(d, N_HEADS, 64), ((0, 0), (0, 0), (0, 64))).reshape(d, 512)
    cols = [a_q, a_k, a_v, a_og, a_lr, z(112),
            _rot_split(r_q), _rot_split(r_k), r_v, r_og,
            c_cq, c_ckv, z(64), c_kpe, z(32),
            d_q, d_k, dv_p]
    return jnp.concatenate(cols, axis=1).astype(BF16)


def _tables(tp):
    pos = jnp.arange(tp, dtype=F32) - FRONT_PAD
    inv = ROPE_THETA ** (-jnp.arange(32, dtype=F32) / 32)
    ang = pos[:, None] * inv[None, :]
    ret_cos = jnp.tile(jnp.cos(ang), (1, N_HEADS))
    ret_sin = jnp.tile(jnp.sin(ang), (1, N_HEADS))
    inv16 = ROPE_THETA ** (-jnp.arange(16, dtype=F32) / 16)
    ang16 = pos[:, None] * inv16[None, :]
    c16, s16 = jnp.cos(ang16), jnp.sin(ang16)
    one = lambda n: jnp.ones((tp, n), F32)
    zero = lambda n: jnp.zeros((tp, n), F32)
    mla_cos = jnp.concatenate([one(64), c16, c16, one(32)], axis=1)
    mla_sa = jnp.concatenate([zero(80), s16, zero(32)], axis=1)
    mla_sb = jnp.concatenate([zero(64), -s16, zero(48)], axis=1)
    lg = jnp.log(1.0 - jnp.exp2(-5.0 - jnp.arange(N_HEADS, dtype=F32)))
    idx = jnp.arange(RET_BLOCK, dtype=F32)
    rel = idx[:, None] - idx[None, :]
    dmask = jnp.where(rel[None] >= 0, jnp.exp(rel[None] * lg[:, None, None]), 0.0)
    dmask = dmask.reshape(N_HEADS * RET_BLOCK, RET_BLOCK)
    lane_head = (jnp.arange(256) % 128) // 32
    qfac = jnp.exp((idx[:, None] + 1.0) * lg[lane_head][None, :])
    kfac = jnp.exp((RET_BLOCK - 1.0 - idx[:, None]) * lg[lane_head][None, :])
    dec = jnp.exp(RET_BLOCK * lg[lane_head])[:, None]
    return ret_cos, ret_sin, mla_cos, mla_sa, mla_sb, dmask, qfac, kfac, dec


def kernel(x, meta_tokens, attn_norm, w_in, gla_w_gate, gla_b_gate, gla_norm, ret_norm, mla_q_norm, mla_w_uq, mla_kv_norm, mla_w_ukv, diff_lambda, diff_norm, w_out, ffn_norm, ffn_w_gate, ffn_w_up, ffn_w_down, moe_router, moe_w_gate, moe_w_up, moe_w_down, final_norm):
    b, seq, d = x.shape
    tp = FRONT_PAD + N_META + seq
    n = b * tp
    meta = jnp.broadcast_to(meta_tokens[None].astype(x.dtype), (b, N_META, d))
    h = jnp.concatenate([jnp.zeros((b, FRONT_PAD, d), x.dtype), meta, x], axis=1).reshape(n, d)
    ret_cos, ret_sin, mla_cos, mla_sa, mla_sb, dmask, qfac, kfac, dec = _tables(tp)

    for li in range(DEPTH):
        pa, pb, pc, pd = _inproj(h, attn_norm[li][None, :], _layout_w_in(w_in[li]))
        pa, pb, pc, pd = (p.reshape(b, tp, -1) for p in (pa, pb, pc, pd))

        wgate = jnp.pad(gla_w_gate[li], ((0, LANES - GLA_GATE_RANK), (0, 0)))
        o_a = _gla(pa, wgate, gla_b_gate[li][None, :], jnp.tile(gla_norm[li], N_HEADS)[None, :])
        o_b = _ret(pb, ret_cos, ret_sin, dmask, qfac, kfac, dec, jnp.tile(ret_norm[li], N_HEADS)[None, :])

        qn = _pad_cols(mla_q_norm[li][None, :], 256)
        kvn = jnp.pad(mla_kv_norm[li][None, :], ((0, 0), (MLA_Q_RANK, 0)))
        wuq = jnp.pad(mla_w_uq[li].reshape(MLA_Q_RANK, N_HEADS, MLA_NOPE + MLA_ROPE),
                      ((0, 256 - MLA_Q_RANK), (0, 0), (0, LANES - MLA_NOPE - MLA_ROPE)))
        wuq = wuq.reshape(256, N_HEADS * LANES).astype(BF16)
        wukv = mla_w_ukv[li].reshape(MLA_KV_RANK, N_HEADS, 2, 64)
        wukv = jnp.pad(wukv, ((MLA_Q_RANK, 0), (0, 0), (0, 0), (0, 64)))
        wukv = wukv.transpose(0, 2, 1, 3).reshape(256, 2 * N_HEADS * LANES).astype(BF16)
        o_c = _mla(pc, qn, kvn, wuq, wukv, mla_cos, mla_sa, mla_sb)

        lam_init = 0.8 - 0.6 * math.exp(-0.3 * li)
        o_d = _diff(pd, diff_lambda[li], _pad_cols(diff_norm[li][None, :], LANES), lam_init)

        o_a, o_b, o_c, o_d = (o.reshape(n, 256) for o in (o_a, o_b, o_c, o_d))
        wo = w_out[li].astype(BF16)
        fn = ffn_norm[li][None, :]
        j = li // 2
        if li % 2 == 0:
            h, hn = _outproj(h, o_a, o_b, o_c, o_d, wo, fn)
            h = _swiglu(h, hn, ffn_w_gate[j].astype(BF16), ffn_w_up[j].astype(BF16),
                        ffn_w_down[j].astype(BF16))
        else:
            h, hn, info, cnt = _outproj(h, o_a, o_b, o_c, o_d, wo, fn,
                                        router=_pad_cols(moe_router[j], LANES))
            return _routed_moe_norm(h, hn, info, cnt, moe_w_gate[j].astype(BF16),
                                    moe_w_up[j].astype(BF16), moe_w_down[j].astype(BF16),
                                    final_norm[None, :], b, tp)
```

```python
import functools
import math

import jax
import jax.numpy as jnp
from jax import lax
from jax.experimental import pallas as pl
from jax.experimental.pallas import tpu as pltpu

F32 = jnp.float32
BF16 = jnp.bfloat16

D_MODEL = 1024
DEPTH = 2
N_META = 16
CHUNK = 64
Q_BLOCK = 128
FRONT_PAD = Q_BLOCK - N_META
EPS = 1e-6
NEG = -1e30
ROPE_THETA = 10000.0
N_HEADS = 4
GLA_DK = 32
GLA_DV = 64
GLA_GATE_RANK = 16
GLA_TAU = 16.0
RET_DK = 64
MLA_Q_RANK = 192
MLA_KV_RANK = 64
MLA_NOPE = 64
MLA_ROPE = 32
DIFF_DK = 32
D_FF = 2816
N_EXPERTS = 8
D_FF_EXPERT = 3584

LANES = 128
SUBLANES = 8
ROW_TILE = 512
FF_CHUNK = 256
VMEM_LIMIT = 56 * 1024 * 1024

PA_W = 768
PB_W = 1024
PC_W = 384
PD_W = 768


def _cparams(n_axes=1):
    return pltpu.CompilerParams(dimension_semantics=("arbitrary",) * n_axes,
                                vmem_limit_bytes=VMEM_LIMIT)


def _resident(shape):
    nd = len(shape)
    return pl.BlockSpec(shape, lambda *_: (0,) * nd, pipeline_mode=pl.Buffered(1))


def _sigmoid(x):
    return 1.0 / (1.0 + jnp.exp(-x))


def _split_bf16(x):
    hi = x.astype(BF16)
    lo = (x - hi.astype(F32)).astype(BF16)
    return hi, lo


def _dot(a, b):
    return jnp.dot(a, b, preferred_element_type=F32)


def _dot_nt(a, b):
    return lax.dot_general(a, b, (((1,), (1,)), ((), ())), preferred_element_type=F32)


def _inproj_kernel(h_ref, g_ref, w_ref, pa_ref, pb_ref, pc_ref, pd_ref):
    x = h_ref[...]
    ms = jnp.mean(x * x, axis=-1, keepdims=True)
    y = (x * lax.rsqrt(ms + EPS) * g_ref[...]).astype(BF16)
    off = 0
    for o_ref, width in ((pa_ref, PA_W), (pb_ref, PB_W), (pc_ref, PC_W), (pd_ref, PD_W)):
        o_ref[...] = _dot(y, w_ref[:, off:off + width]).astype(BF16)
        off += width


def _inproj(h, g, w):
    n = h.shape[0]
    wtot = PA_W + PB_W + PC_W + PD_W
    row = lambda width: pl.BlockSpec((ROW_TILE, width), lambda i: (i, 0))
    return pl.pallas_call(
        _inproj_kernel,
        grid=(n // ROW_TILE,),
        in_specs=[row(D_MODEL), _resident((1, D_MODEL)), _resident((D_MODEL, wtot))],
        out_specs=[row(PA_W), row(PB_W), row(PC_W), row(PD_W)],
        out_shape=[jax.ShapeDtypeStruct((n, w_), BF16) for w_ in (PA_W, PB_W, PC_W, PD_W)],
        compiler_params=_cparams(),
        name="inproj",
    )(h, g, w)


def _outproj_kernel(router_seq_rows, h_ref, oa_ref, ob_ref, oc_ref, od_ref, wo_ref, fn_ref, *rest):
    with_router = router_seq_rows > 0
    if with_router:
        router_ref, hmid_ref, hn_ref, info_ref, cnt_ref, carry_ref = rest
    else:
        hmid_ref, hn_ref = rest
    o = jnp.concatenate([oa_ref[...], ob_ref[...], oc_ref[...], od_ref[...]], axis=1)
    hm = h_ref[...] + _dot(o, wo_ref[...])
    hmid_ref[...] = hm
    ms = jnp.mean(hm * hm, axis=-1, keepdims=True)
    y = hm * lax.rsqrt(ms + EPS) * fn_ref[...]
    hn_ref[...] = y.astype(hn_ref.dtype)
    if with_router:
        y_hi, y_lo = _split_bf16(y)
        r_hi, r_lo = _split_bf16(router_ref[...])
        logits = _dot(y_hi, r_hi) + _dot(y_hi, r_lo) + _dot(y_lo, r_hi)
        lane = lax.broadcasted_iota(jnp.int32, logits.shape, 1).astype(F32)
        ninf = float("-inf")
        logits = jnp.where(lane < N_EXPERTS, logits, ninf)
        m1 = jnp.max(logits, axis=-1, keepdims=True)
        i1 = jnp.min(jnp.where(logits == m1, lane, float(LANES)), axis=-1, keepdims=True)
        rest_l = jnp.where(lane == i1, ninf, logits)
        m2 = jnp.max(rest_l, axis=-1, keepdims=True)
        i2 = jnp.min(jnp.where(rest_l == m2, lane, float(LANES)), axis=-1, keepdims=True)
        e2 = jnp.exp(m2 - m1)
        den = 1.0 + e2
        @pl.when(pl.program_id(0) == 0)
        def _():
            carry_ref[...] = jnp.zeros_like(carry_ref)
        sel = jnp.where(lane == i1, 1.0, 0.0) + jnp.where(lane == i2, 1.0, 0.0)
        rows = sel.shape[0]
        seq_pos = (lax.rem(pl.program_id(0) * rows, router_seq_rows)
                   + lax.broadcasted_iota(jnp.int32, (rows, 1), 0))
        seq_pos = jnp.where(seq_pos >= router_seq_rows, seq_pos - router_seq_rows, seq_pos)
        sel = jnp.where(seq_pos >= Q_BLOCK, sel, 0.0)
        below = (lax.broadcasted_iota(jnp.int32, (rows, rows), 1)
                 < lax.broadcasted_iota(jnp.int32, (rows, rows), 0))
        count = _dot(jnp.where(below, 1.0, 0.0).astype(BF16), sel.astype(BF16)) + carry_ref[...]
        r1 = jnp.sum(jnp.where(lane == i1, count, 0.0), axis=-1, keepdims=True)
        r2 = jnp.sum(jnp.where(lane == i2, count, 0.0), axis=-1, keepdims=True)
        total = carry_ref[...] + jnp.sum(sel, axis=0, keepdims=True)
        carry_ref[...] = total
        cnt_ref[...] = jnp.broadcast_to(total, cnt_ref.shape)
        info = jnp.where(lane == 0, i1, 0.0)
        for k, val in enumerate((i2, 1.0 / den, e2 / den, r1, r2), start=1):
            info = jnp.where(lane == k, val, info)
        info_ref[...] = info


def _outproj(h, oa, ob, oc, od, wo, fn, router=None, seq_rows=0):
    n = h.shape[0]
    row = lambda width: pl.BlockSpec((ROW_TILE, width), lambda i: (i, 0))
    in_specs = [row(D_MODEL), row(256), row(256), row(256), row(256),
                _resident((D_MODEL, D_MODEL)), _resident((1, D_MODEL))]
    out_specs = [row(D_MODEL), row(D_MODEL)]
    out_shape = [jax.ShapeDtypeStruct((n, D_MODEL), F32),
                 jax.ShapeDtypeStruct((n, D_MODEL), BF16 if router is None else F32)]
    args = [h, oa, ob, oc, od, wo, fn]
    scratch = []
    if router is not None:
        in_specs.append(_resident((D_MODEL, LANES)))
        out_specs += [row(LANES), pl.BlockSpec((8, LANES), lambda i: (0, 0))]
        out_shape += [jax.ShapeDtypeStruct((n, LANES), F32), jax.ShapeDtypeStruct((8, LANES), F32)]
        args.append(router)
        scratch.append(pltpu.VMEM((1, LANES), F32))
    return pl.pallas_call(
        functools.partial(_outproj_kernel, seq_rows if router is not None else 0),
        grid=(n // ROW_TILE,),
        in_specs=in_specs, out_specs=out_specs, out_shape=out_shape,
        scratch_shapes=scratch,
        compiler_params=_cparams(),
        name="outproj_router" if router is not None else "outproj",
    )(*args)


def _dispatch_kernel(n_fill, zs_ref, ze_ref, pos_ref, x_ref, xs_ref, zero_buf, sem, zsem):
    def row_copy(rb, u, k):
        p = pos_ref[0, 0, rb * (2 * SUBLANES) + 2 * u + k]
        return pltpu.make_async_copy(x_ref.at[rb, pl.ds(u, 1), :], xs_ref.at[pl.ds(p, 1), :], sem)

    def start(rb, carry):
        for u in range(SUBLANES):
            row_copy(rb, u, 0).start(priority=0)
            row_copy(rb, u, 1).start(priority=1)
        return carry

    def wait(rb, carry):
        for u in range(SUBLANES):
            row_copy(rb, u, 0).wait()
            row_copy(rb, u, 1).wait()
        return carry

    lax.fori_loop(0, x_ref.shape[0], start, 0)
    lax.fori_loop(0, x_ref.shape[0], wait, 0)

    @pl.when(pl.program_id(0) == pl.num_programs(0) - 1)
    def _():
        zero_buf[...] = jnp.zeros_like(zero_buf)

        def zero_copy(r):
            return pltpu.make_async_copy(zero_buf.at[pl.ds(0, 1), :], xs_ref.at[pl.ds(r, 1), :], zsem)

        def fill(r, carry):
            zero_copy(r).start()
            return carry

        for k in range(zs_ref.shape[0]):
            lax.fori_loop(zs_ref[k], ze_ref[k], fill, 0)

        def drain(r, carry):
            zero_copy(0).wait()
            return carry

        lax.fori_loop(0, n_fill, drain, 0)


def _dispatch(zero_start, zero_end, pos, x, n_sorted, n_fill):
    n, d = x.shape
    return pl.pallas_call(
        functools.partial(_dispatch_kernel, n_fill),
        grid_spec=pltpu.PrefetchScalarGridSpec(
            num_scalar_prefetch=2,
            grid=(n // ROW_TILE,),
            in_specs=[pl.BlockSpec((1, 1, 2 * ROW_TILE), lambda i, zs, ze: (i, 0, 0),
                                   memory_space=pltpu.SMEM),
                      pl.BlockSpec((ROW_TILE // SUBLANES, SUBLANES, d), lambda i, zs, ze: (i, 0, 0))],
            out_specs=pl.BlockSpec(memory_space=pl.ANY),
            scratch_shapes=[pltpu.VMEM((SUBLANES, d), x.dtype), pltpu.SemaphoreType.DMA(()),
                            pltpu.SemaphoreType.DMA(())]),
        out_shape=jax.ShapeDtypeStruct((n_sorted, d), x.dtype),
        compiler_params=_cparams(),
        name="moe_dispatch",
    )(zero_start, zero_end, pos, x.reshape(n // SUBLANES, SUBLANES, d))


def _swiglu_tile(x, wg, wu, wd, acc_ref):
    n_chunks = wd.shape[0] // FF_CHUNK
    for c in range(n_chunks):
        cols = slice(c * FF_CHUNK, (c + 1) * FF_CHUNK)
        g = _dot(x, wg[:, cols])
        u = _dot(x, wu[:, cols])
        part = _dot((g * _sigmoid(g) * u).astype(BF16), wd[cols, :])
        if c == 0:
            acc_ref[...] = part
        elif c < n_chunks - 1:
            acc_ref[...] += part
    return acc_ref[...] + part


def _experts_kernel(te_ref, tv_ref, xs_ref, wg_ref, wu_ref, wd_ref, ys_ref, acc_ref):
    i = pl.program_id(0)

    @pl.when(tv_ref[i] > 0)
    def _():
        ys_ref[...] = _swiglu_tile(xs_ref[...].astype(BF16), wg_ref.at[0], wu_ref.at[0],
                                   wd_ref.at[0], acc_ref)

    @pl.when(tv_ref[i] == 0)
    def _():
        ys_ref[...] = jnp.zeros_like(ys_ref)


def _experts(tile_expert, tile_valid, xs, wg, wu, wd, n_sorted):
    d = xs.shape[1]
    wspec = lambda shape: pl.BlockSpec((1,) + shape[1:], lambda i, te, tv: (te[i], 0, 0),
                                       pipeline_mode=pl.Buffered(1))
    row = pl.BlockSpec((ROW_TILE, d), lambda i, te, tv: (i, 0))
    return pl.pallas_call(
        _experts_kernel,
        grid_spec=pltpu.PrefetchScalarGridSpec(
            num_scalar_prefetch=2,
            grid=(n_sorted // ROW_TILE,),
            in_specs=[row, wspec(wg.shape), wspec(wu.shape), wspec(wd.shape)],
            out_specs=row,
            scratch_shapes=[pltpu.VMEM((ROW_TILE, d), F32)]),
        out_shape=jax.ShapeDtypeStruct((n_sorted, d), F32),
        compiler_params=_cparams(),
        name="moe_experts",
    )(tile_expert, tile_valid, xs, wg, wu, wd)


def _combine_norm_kernel(tp, pos_ref, h_ref, info_ref, ys_ref, g_ref, o_ref,
                         h_buf, info_buf, y_buf, sem, row_sem):
    n_rows = h_buf.shape[0]
    row0 = pl.multiple_of(pl.program_id(0) * tp + Q_BLOCK + pl.program_id(1) * n_rows, Q_BLOCK)
    h_copy = pltpu.make_async_copy(h_ref.at[pl.ds(row0, n_rows), :], h_buf, sem.at[0])
    info_copy = pltpu.make_async_copy(info_ref.at[pl.ds(row0, n_rows), :], info_buf, sem.at[1])
    h_copy.start()
    info_copy.start()

    def row_copy(rb, u, k):
        p = pos_ref[0, 0, rb * (2 * SUBLANES) + 2 * u + k]
        return pltpu.make_async_copy(ys_ref.at[pl.ds(p, 1), :], y_buf.at[k, rb, pl.ds(u, 1), :], row_sem)

    def start(rb, carry):
        for u in range(SUBLANES):
            row_copy(rb, u, 0).start(priority=0)
            row_copy(rb, u, 1).start(priority=1)
        return carry

    def wait(rb, carry):
        for u in range(SUBLANES):
            row_copy(rb, u, 0).wait()
            row_copy(rb, u, 1).wait()
        return carry

    lax.fori_loop(0, n_rows // SUBLANES, start, 0)
    lax.fori_loop(0, n_rows // SUBLANES, wait, 0)
    h_copy.wait()
    info_copy.wait()
    info = info_buf[...]
    y0 = y_buf[0].reshape(n_rows, -1)
    y1 = y_buf[1].reshape(n_rows, -1)
    x = h_buf[...] + info[:, 2:3] * y0 + info[:, 3:4] * y1
    ms = jnp.mean(x * x, axis=-1, keepdims=True)
    o_ref[0] = x * lax.rsqrt(ms + EPS) * g_ref[...]


def _combine_norm(pos, h, info, ys, g, b, tp):
    n, d = h.shape
    seq = tp - Q_BLOCK
    tiles = seq // ROW_TILE
    any_spec = pl.BlockSpec(memory_space=pl.ANY)
    return pl.pallas_call(
        functools.partial(_combine_norm_kernel, tp),
        grid=(b, tiles),
        in_specs=[pl.BlockSpec((1, 1, 2 * ROW_TILE), lambda i, j: (i * tiles + j, 0, 0),
                               memory_space=pltpu.SMEM),
                  any_spec, any_spec, any_spec, _resident((1, d))],
        out_specs=pl.BlockSpec((1, ROW_TILE, d), lambda i, j: (i, j, 0)),
        out_shape=jax.ShapeDtypeStruct((b, seq, d), F32),
        scratch_shapes=[pltpu.VMEM((ROW_TILE, d), F32), pltpu.VMEM((ROW_TILE, LANES), F32),
                        pltpu.VMEM((2, ROW_TILE // SUBLANES, SUBLANES, d), F32),
                        pltpu.SemaphoreType.DMA((2,)), pltpu.SemaphoreType.DMA(())],
        compiler_params=_cparams(2),
        name="moe_combine_norm",
    )(pos, h, info, ys, g)


def _routed_moe_norm(h, hn, info, cnt, wg, wu, wd, g, b, tp):
    n = h.shape[0]
    n_routed = b * (tp - Q_BLOCK)
    n_fill = N_EXPERTS * ROW_TILE
    n_main = 2 * n_routed + n_fill
    n_tiles = n_main // ROW_TILE
    counts = cnt[0, :N_EXPERTS].astype(jnp.int32)
    tiles_e = (counts + ROW_TILE - 1) // ROW_TILE
    tile_end = jnp.cumsum(tiles_e)
    row_base = (tile_end - tiles_e) * ROW_TILE
    e_idx = info[:, 0:2].astype(jnp.int32)
    pos = row_base[e_idx] + info[:, 4:6].astype(jnp.int32)
    rows = jnp.arange(n, dtype=jnp.int32)
    seq_pos = rows % tp
    parked = n_main + 2 * ((rows // tp) * Q_BLOCK + seq_pos)[:, None] + jnp.arange(2, dtype=jnp.int32)[None, :]
    pos = jnp.where((seq_pos >= Q_BLOCK)[:, None], pos, parked)
    tile_ids = jnp.arange(n_tiles, dtype=jnp.int32)
    tile_expert = jnp.minimum(jnp.sum((tile_ids[:, None] >= tile_end[None, :]).astype(jnp.int32), axis=1),
                              N_EXPERTS - 1)
    tile_valid = (tile_ids < tile_end[-1]).astype(jnp.int32)
    zero_start = jnp.concatenate([row_base + counts, tile_end[-1:] * ROW_TILE])
    zero_end = jnp.concatenate([tile_end * ROW_TILE, jnp.full((1,), n_main, jnp.int32)])
    xs = _dispatch(zero_start, zero_end, pos.reshape(n // ROW_TILE, 1, 2 * ROW_TILE), hn,
                   n_main + 2 * (n - n_routed), n_fill)
    ys = _experts(tile_expert, tile_valid, xs, wg, wu, wd, n_main)
    pos_seq = pos.reshape(b, tp, 2)[:, Q_BLOCK:, :].reshape(-1, 1, 2 * ROW_TILE)
    return _combine_norm(pos_seq, h, info, ys, g, b, tp)


def _swiglu_kernel(h_ref, hn_ref, wg_ref, wu_ref, wd_ref, o_ref, acc_ref):
    o_ref[...] = h_ref[...] + _swiglu_tile(hn_ref[...], wg_ref, wu_ref, wd_ref, acc_ref)


def _swiglu(h, hn, wg, wu, wd):
    n = h.shape[0]
    row = pl.BlockSpec((ROW_TILE, D_MODEL), lambda i: (i, 0))
    return pl.pallas_call(
        _swiglu_kernel,
        grid=(n // ROW_TILE,),
        in_specs=[row, row, _resident(wg.shape), _resident(wu.shape), _resident(wd.shape)],
        out_specs=row,
        out_shape=jax.ShapeDtypeStruct((n, D_MODEL), F32),
        scratch_shapes=[pltpu.VMEM((ROW_TILE, D_MODEL), F32)],
        input_output_aliases={0: 0},
        compiler_params=_cparams(),
        name="swiglu",
    )(h, hn, wg, wu, wd)


def _group_ones(n, group_shift):
    r = lax.broadcasted_iota(jnp.int32, (n, n), 0) >> group_shift
    c = lax.broadcasted_iota(jnp.int32, (n, n), 1) >> group_shift
    return jnp.where(r == c, 1.0, 0.0).astype(BF16)


GLA_ROWS = 256


def _gla_kernel(tp, pa_ref, lr_ref, wg_ref, bg_ref, gn_ref, o_ref, s_ref):
    qhead = lax.broadcasted_iota(jnp.int32, (1, N_HEADS * GLA_DK), 1) >> 5
    ehead = lax.broadcasted_iota(jnp.int32, (1, N_HEADS * GLA_DV), 1) >> 6
    shead = lax.broadcasted_iota(jnp.int32, (N_HEADS * GLA_DK, 1), 0) >> 5
    bd = shead == ehead
    gsum = _group_ones(N_HEADS * GLA_DV, 6)
    wg_hi, wg_lo = _split_bf16(wg_ref[...])
    bg = bg_ref[...]
    gn = gn_ref[...]
    scale = GLA_DK ** -0.5
    s_ref[...] = jnp.zeros_like(s_ref)

    def block(r0, n_rows):
        n_c = n_rows // CHUNK
        rows = pl.ds(r0, n_rows)
        ri = lax.broadcasted_iota(jnp.int32, (n_rows, n_rows), 0)
        ci = lax.broadcasted_iota(jnp.int32, (n_rows, n_rows), 1)
        same = (ri >> 6) == (ci >> 6)
        tri_bf = jnp.where(jnp.logical_and(same, ri >= ci), 1.0, 0.0).astype(BF16)
        ones_bf = jnp.where(same, 1.0, 0.0).astype(BF16)
        r4 = lax.broadcasted_iota(jnp.int32, (N_HEADS * n_rows, n_rows), 0) & (n_rows - 1)
        c4 = lax.broadcasted_iota(jnp.int32, (N_HEADS * n_rows, n_rows), 1)
        tri4 = jnp.logical_and((r4 >> 6) == (c4 >> 6), r4 >= c4)
        col_chunk = lax.broadcasted_iota(jnp.int32, (1, n_rows), 1) >> 6

        q = pa_ref[0, rows, 0:128].astype(F32) * scale
        k = pa_ref[0, rows, 128:256].astype(F32)
        v = pa_ref[0, rows, 256:512]
        og = pa_ref[0, rows, 512:768].astype(F32)
        lr = lr_ref[0, rows, :]
        valid = (r0 + lax.broadcasted_iota(jnp.int32, (n_rows, 1), 0)) >= FRONT_PAD

        pre = _dot(lr, wg_hi) + _dot(lr, wg_lo) + bg
        logsig = jnp.minimum(pre, 0.0) - jnp.log1p(jnp.exp(-jnp.abs(pre)))
        g = jnp.where(valid, logsig * (1.0 / GLA_TAU), 0.0)
        g_hi, g_lo = _split_bf16(g)
        cum = _dot(tri_bf, g_hi) + _dot(tri_bf, g_lo)
        cum_end = _dot(ones_bf, g_hi) + _dot(ones_bf, g_lo)
        qt_bf = (q * jnp.exp(cum)).astype(BF16)
        kt = (k * jnp.exp(-cum)).astype(BF16)
        kd_t = (k * jnp.exp(cum_end - cum)).T.astype(BF16)
        dec_t = jnp.exp(cum_end.T)

        qs = jnp.concatenate([jnp.where(qhead == h, qt_bf, jnp.zeros_like(qt_bf))
                              for h in range(N_HEADS)], axis=0)
        a = jnp.where(tri4, _dot_nt(qs, kt), 0.0).astype(BF16)
        r = _dot(a, v)
        o = jnp.where(ehead == 0, r[0:n_rows, :], 0.0)
        for h in range(1, N_HEADS):
            o = o + jnp.where(ehead == h, r[h * n_rows:(h + 1) * n_rows, :], 0.0)

        s = s_ref[...]
        inter = []
        for c in range(n_c):
            inter.append(_dot(qt_bf[c * CHUNK:(c + 1) * CHUNK, :], s.astype(BF16)))
            upd = _dot(jnp.where(col_chunk == c, kd_t, jnp.zeros_like(kd_t)), v)
            s = dec_t[:, c * CHUNK:c * CHUNK + 1] * s + jnp.where(bd, upd, 0.0)
        s_ref[...] = s
        o = o + jnp.concatenate(inter, axis=0)

        ms = _dot((o * o).astype(BF16), gsum) * (1.0 / GLA_DV)
        y = o * lax.rsqrt(ms + EPS) * gn * (og * _sigmoid(og))
        o_ref[0, rows, :] = jnp.where(valid, y, 0.0).astype(BF16)

    block(0, Q_BLOCK)

    def body(i, carry):
        block(pl.multiple_of(Q_BLOCK + i * GLA_ROWS, Q_BLOCK), GLA_ROWS)
        return carry

    lax.fori_loop(0, (tp - Q_BLOCK) // GLA_ROWS, body, 0)


def _gla(pa, pc, wg, bg, gn):
    b, tp, _ = pa.shape
    return pl.pallas_call(
        functools.partial(_gla_kernel, tp),
        grid=(b,),
        in_specs=[pl.BlockSpec((1, tp, PA_W), lambda i: (i, 0, 0)),
                  pl.BlockSpec((1, tp, LANES), lambda i: (i, 0, 2)),
                  _resident((LANES, LANES)), _resident((1, LANES)), _resident((1, 256))],
        out_specs=pl.BlockSpec((1, tp, 256), lambda i: (i, 0, 0)),
        out_shape=jax.ShapeDtypeStruct((b, tp, 256), BF16),
        scratch_shapes=[pltpu.VMEM((N_HEADS * GLA_DK, N_HEADS * GLA_DV), F32)],
        compiler_params=_cparams(),
        name="gla",
    )(pa, pc, wg, bg, gn)


RET_BLOCK = 128


def _ret_kernel(tp, pb_ref, cos_ref, sin_ref, dmask_ref, qfac_ref, kfac_ref, dec_ref, gn_ref,
                o_ref, s_ref):
    blk = RET_BLOCK
    n_blocks = tp // blk
    qhead = (lax.broadcasted_iota(jnp.int32, (1, 256), 1) & 127) >> 5
    ehead = lax.broadcasted_iota(jnp.int32, (1, 256), 1) >> 6
    shead = (lax.broadcasted_iota(jnp.int32, (256, 1), 0) & 127) >> 5
    bd = shead == ehead
    gsum = _group_ones(256, 6)
    gn = gn_ref[...]
    s_ref[...] = jnp.zeros_like(s_ref)

    def rope(x, cos, sin):
        x1, x2 = x[:, :128], x[:, 128:]
        return jnp.concatenate([x1 * cos - x2 * sin, x1 * sin + x2 * cos], axis=1)

    def block(j, carry):
        r0 = pl.multiple_of(j * blk, blk)
        rows = pl.ds(r0, blk)
        cos = cos_ref[rows, :]
        sin = sin_ref[rows, :]
        q = rope(pb_ref[0, rows, 0:256].astype(F32), cos, sin)
        k = rope(pb_ref[0, rows, 256:512].astype(F32), cos, sin) * (RET_DK ** -0.5)
        v = pb_ref[0, rows, 512:768]
        og = pb_ref[0, rows, 768:1024].astype(F32)
        valid = (r0 + lax.broadcasted_iota(jnp.int32, (blk, 1), 0)) >= FRONT_PAD

        q_bf = q.astype(BF16)
        qs = jnp.concatenate([jnp.where(qhead == h, q_bf, jnp.zeros_like(q_bf))
                              for h in range(N_HEADS)], axis=0)
        a = (_dot_nt(qs, k.astype(BF16)) * dmask_ref[...]).astype(BF16)
        r = _dot(a, v)
        o = _dot((q * qfac_ref[...]).astype(BF16), s_ref[...].astype(BF16))
        for h in range(N_HEADS):
            o = o + jnp.where(ehead == h, r[h * blk:(h + 1) * blk, :], 0.0)

        kd_t = (k * kfac_ref[...]).T.astype(BF16)
        upd = _dot(kd_t, v)
        s_ref[...] = dec_ref[...] * s_ref[...] + jnp.where(bd, upd, 0.0)

        mu = _dot(o.astype(BF16), gsum) * (1.0 / 64)
        xc = o - mu
        var = _dot((xc * xc).astype(BF16), gsum) * (1.0 / 64)
        y = xc * lax.rsqrt(var + EPS) * gn * (og * _sigmoid(og))
        o_ref[0, rows, :] = jnp.where(valid, y, 0.0).astype(BF16)
        return carry

    lax.fori_loop(0, n_blocks, block, 0, unroll=2)


def _ret(pb, cos, sin, dmask, qfac, kfac, dec, gn):
    b, tp, _ = pb.shape
    return pl.pallas_call(
        functools.partial(_ret_kernel, tp),
        grid=(b,),
        in_specs=[pl.BlockSpec((1, tp, PB_W), lambda i: (i, 0, 0)),
                  _resident(cos.shape), _resident(sin.shape), _resident(dmask.shape),
                  _resident(qfac.shape), _resident(kfac.shape), _resident(dec.shape),
                  _resident((1, 256))],
        out_specs=pl.BlockSpec((1, tp, 256), lambda i: (i, 0, 0)),
        out_shape=jax.ShapeDtypeStruct((b, tp, 256), BF16),
        scratch_shapes=[pltpu.VMEM((256, 256), F32)],
        compiler_params=_cparams(),
        name="retention",
    )(pb, cos, sin, dmask, qfac, kfac, dec, gn)


ATT_ROWS = 256
LOG2E = 1.4426950408889634
V_ONE = 64


def _aligned(x, m):
    return x if isinstance(x, int) else pl.multiple_of(x, m)


def _for_pairs(n, body):
    if isinstance(n, int):
        for j in range(n):
            body(j)
        return

    def pair(t, carry):
        body(2 * t)
        body(2 * t + 1)
        return carry

    lax.fori_loop(0, n >> 1, pair, 0)

    @pl.when((n & 1) == 1)
    def _():
        body(n - 1)


def _mask_groups(mask, s, n_g):
    rows = s.shape[0] // n_g
    return jnp.concatenate([jnp.where(mask, s[g * rows:(g + 1) * rows, :], NEG)
                            for g in range(n_g)], axis=0)


def _attn_block0(qk_fn, v_fn, n_g):
    rows = pl.ds(0, Q_BLOCK)
    qrow = lax.broadcasted_iota(jnp.int32, (Q_BLOCK, 1), 0)
    kcol = lax.broadcasted_iota(jnp.int32, (1, Q_BLOCK), 1)
    mask = jnp.logical_and(kcol <= qrow, kcol >= FRONT_PAD)
    s = _mask_groups(mask, qk_fn(0, Q_BLOCK, rows), n_g)
    p_bf = jnp.exp2(s - jnp.max(s, axis=-1, keepdims=True)).astype(BF16)
    pv = jnp.concatenate([_dot(p_bf[g * Q_BLOCK:(g + 1) * Q_BLOCK, :], v_fn(g, rows))
                          for g in range(n_g)], axis=0)
    return pv / pv[:, V_ONE:V_ONE + 1]


def _attn_row0(i):
    return _aligned(Q_BLOCK + (i - 1) * ATT_ROWS, Q_BLOCK)


def _attn_result(acc_s):
    acc = acc_s[...]
    return acc / acc[:, V_ONE:V_ONE + 1]


def _attn_block(i, qk_fn, v_fn, n_g, s_meta, s_s, mx_s, acc_s, mid_fn=None):
    rr = ATT_ROWS
    q0 = _attn_row0(i)
    meta_rows = pl.ds(0, Q_BLOCK)
    kcol = lax.broadcasted_iota(jnp.int32, (1, Q_BLOCK), 1)

    def key_rows(j):
        return pl.ds(_aligned(Q_BLOCK + j * rr, Q_BLOCK), rr)

    s = jnp.where(kcol >= FRONT_PAD, qk_fn(q0, rr, meta_rows), NEG)
    s_meta[...] = s
    mx_s[...] = s

    def pass1(j):
        sj = qk_fn(q0, rr, key_rows(j))
        s_s[j] = sj
        mx_s[...] = jnp.maximum(mx_s[...], jnp.maximum(sj[:, :LANES], sj[:, LANES:]))

    _for_pairs(i - 1, pass1)
    if mid_fn is not None:
        mid_fn()
    causal = (lax.broadcasted_iota(jnp.int32, (rr, rr), 1)
              <= lax.broadcasted_iota(jnp.int32, (rr, rr), 0))
    sd = _mask_groups(causal, qk_fn(q0, rr, pl.ds(q0, rr)), n_g)
    s_s[i - 1] = sd
    m = jnp.max(jnp.maximum(mx_s[...], jnp.maximum(sd[:, :LANES], sd[:, LANES:])),
                axis=-1, keepdims=True)
    mx_s[...] = jnp.broadcast_to(m, mx_s.shape)

    p_bf = jnp.exp2(s_meta[...] - mx_s[...]).astype(BF16)
    for g in range(n_g):
        acc_s[g * rr:(g + 1) * rr, :] = _dot(p_bf[g * rr:(g + 1) * rr, :], v_fn(g, meta_rows))

    def pass2(j):
        sj = s_s[j]
        mrep = mx_s[...]
        p0 = jnp.exp2(sj[:, :LANES] - mrep)
        p1 = jnp.exp2(sj[:, LANES:] - mrep)
        pj = jnp.concatenate([p0.astype(BF16), p1.astype(BF16)], axis=1)
        for g in range(n_g):
            acc_s[g * rr:(g + 1) * rr, :] += _dot(pj[g * rr:(g + 1) * rr, :], v_fn(g, key_rows(j)))

    _for_pairs(i, pass2)


def _attn_all_blocks(n_q, block_fn, finish_fn):
    block_fn(1, None)

    def body(i, carry):
        block_fn(i, lambda: finish_fn(i - 1))
        return carry

    lax.fori_loop(2, n_q + 1, body, 0)
    finish_fn(n_q)


def _heads_to_lanes(per_head):
    low = lax.broadcasted_iota(jnp.int32, (1, LANES), 1) < 64
    lo = jnp.where(low, per_head[0], pltpu.roll(per_head[1], 64, 1))
    hi = jnp.where(low, per_head[2], pltpu.roll(per_head[3], 64, 1))
    return jnp.concatenate([lo, hi], axis=1)


def _with_ones_lane(v):
    lane = lax.broadcasted_iota(jnp.int32, (1, LANES), 1)
    return jnp.where(lane == V_ONE, jnp.ones_like(v), v)


def _mla_kernel(tp, pc_ref, qn_ref, kvn_ref, wuq_ref, wukv_ref, cos_ref, sa_ref, sb_ref,
                o_ref, q_s, k_s, v_s, s_meta, s_s, mx_s, acc_s):
    n_blocks = tp // Q_BLOCK
    scale = (MLA_NOPE + MLA_ROPE) ** -0.5 * LOG2E
    is_q = lax.broadcasted_iota(jnp.int32, (1, 256), 1) < MLA_Q_RANK

    def prep(i, carry):
        r0 = pl.multiple_of(i * Q_BLOCK, Q_BLOCK)
        rows = pl.ds(r0, Q_BLOCK)
        x = pc_ref[0, rows, 0:256].astype(F32)
        x2 = x * x
        ms_q = jnp.sum(jnp.where(is_q, x2, 0.0), axis=-1, keepdims=True) * (1.0 / MLA_Q_RANK)
        ms_kv = jnp.sum(jnp.where(is_q, 0.0, x2), axis=-1, keepdims=True) * (1.0 / MLA_KV_RANK)
        yq = (x * lax.rsqrt(ms_q + EPS) * qn_ref[...]).astype(BF16)
        ykv = (x * lax.rsqrt(ms_kv + EPS) * kvn_ref[...]).astype(BF16)
        cq = _dot(yq, wuq_ref[...])
        kv = _dot(ykv, wukv_ref[...])
        cos = cos_ref[rows, :]
        sa = sa_ref[rows, :]
        sb = sb_ref[rows, :]

        def rope(t):
            return t * cos + pltpu.roll(t, 16, 1) * sa + pltpu.roll(t, LANES - 16, 1) * sb

        kpe_in = pc_ref[0, rows, 256:384].astype(F32)
        kpe = rope(jnp.where(lax.broadcasted_iota(jnp.int32, (1, LANES), 1) >= 64, kpe_in, 0.0))
        for h in range(N_HEADS):
            q_s[h, rows, :] = (rope(cq[:, h * LANES:(h + 1) * LANES]) * scale).astype(BF16)
            k_s[h, rows, :] = (kv[:, h * LANES:(h + 1) * LANES] + kpe).astype(BF16)
            v_s[h, rows, :] = _with_ones_lane(
                kv[:, (N_HEADS + h) * LANES:(N_HEADS + h + 1) * LANES]).astype(BF16)
        return carry

    lax.fori_loop(0, n_blocks, prep, 0)

    def qk_fn(q0, n_rows, krows):
        return jnp.concatenate([_dot_nt(q_s[h, pl.ds(q0, n_rows), :], k_s[h, krows, :])
                                for h in range(N_HEADS)], axis=0)

    def v_fn(h, krows):
        return v_s[h, krows, :]

    def emit(q0, n_rows, o, first):
        y = _heads_to_lanes([o[h * n_rows:(h + 1) * n_rows, :] for h in range(N_HEADS)])
        if first:
            qrow = lax.broadcasted_iota(jnp.int32, (n_rows, 1), 0)
            y = jnp.where(qrow >= FRONT_PAD, y, 0.0)
        o_ref[0, pl.ds(q0, n_rows), :] = y.astype(BF16)

    emit(0, Q_BLOCK, _attn_block0(qk_fn, v_fn, N_HEADS), True)

    def block(i, mid_fn):
        _attn_block(i, qk_fn, v_fn, N_HEADS, s_meta, s_s, mx_s, acc_s, mid_fn)

    def finish(i):
        emit(_attn_row0(i), ATT_ROWS, _attn_result(acc_s), False)

    _attn_all_blocks((tp - Q_BLOCK) // ATT_ROWS, block, finish)


def _attn_scratch(n_g, tp):
    g_rows = n_g * ATT_ROWS
    n_slots = (tp - Q_BLOCK) // ATT_ROWS
    return [pltpu.VMEM((g_rows, LANES), F32),
            pltpu.VMEM((n_slots, g_rows, ATT_ROWS), F32),
            pltpu.VMEM((g_rows, LANES), F32),
            pltpu.VMEM((g_rows, LANES), F32)]


def _mla(pc, qn, kvn, wuq, wukv, cos, sa, sb):
    b, tp, _ = pc.shape
    return pl.pallas_call(
        functools.partial(_mla_kernel, tp),
        grid=(b,),
        in_specs=[pl.BlockSpec((1, tp, PC_W), lambda i: (i, 0, 0)),
                  _resident((1, 256)), _resident((1, 256)),
                  _resident(wuq.shape), _resident(wukv.shape),
                  _resident(cos.shape), _resident(sa.shape), _resident(sb.shape)],
        out_specs=pl.BlockSpec((1, tp, 256), lambda i: (i, 0, 0)),
        out_shape=jax.ShapeDtypeStruct((b, tp, 256), BF16),
        scratch_shapes=[pltpu.VMEM((N_HEADS, tp, LANES), BF16),
                        pltpu.VMEM((N_HEADS, tp, LANES), BF16),
                        pltpu.VMEM((N_HEADS, tp, LANES), BF16)] + _attn_scratch(N_HEADS, tp),
        compiler_params=_cparams(),
        name="mla",
    )(pc, qn, kvn, wuq, wukv, cos, sa, sb)


def _diff_kernel(tp, lam_init, pd_ref, lam_ref, dn_ref, o_ref, qs_s, v_s, s_meta, s_s, mx_s, acc_s):
    n_maps = 2 * N_HEADS

    def fill_values(i, carry):
        rows = pl.ds(pl.multiple_of(i * Q_BLOCK, Q_BLOCK), Q_BLOCK)
        low = lax.broadcasted_iota(jnp.int32, (1, LANES), 1) < V_ONE
        for pair in range(N_HEADS // 2):
            two = pd_ref[0, rows, 512 + pair * LANES:512 + (pair + 1) * LANES].astype(F32)
            for h, vals in ((2 * pair, two), (2 * pair + 1, pltpu.roll(two, V_ONE, 1))):
                v_s[h, rows, :] = _with_ones_lane(jnp.where(low, vals, 0.0)).astype(BF16)
        return carry

    lax.fori_loop(0, tp // Q_BLOCK, fill_values, 0)
    scale = DIFF_DK ** -0.5 * LOG2E
    group = lax.broadcasted_iota(jnp.int32, (1, 256), 1) >> 5
    lv = lam_ref[...]
    lam = (jnp.exp(jnp.sum(lv[0:1, :] * lv[1:2, :], axis=-1, keepdims=True))
           - jnp.exp(jnp.sum(lv[2:3, :] * lv[3:4, :], axis=-1, keepdims=True)) + lam_init)
    dn = dn_ref[...]

    def stack_queries(q0, n_rows):
        q = (pd_ref[0, pl.ds(q0, n_rows), 0:256].astype(F32) * scale).astype(BF16)
        for g in range(n_maps):
            qs_s[g * n_rows:(g + 1) * n_rows, :] = jnp.where(group == g, q, jnp.zeros_like(q))

    def qk_fn(q0, n_rows, krows):
        return _dot_nt(qs_s[0:n_maps * n_rows, :], pd_ref[0, krows, 256:512])

    def v_fn(g, krows):
        return v_s[g // 2, krows, :]

    def emit(q0, n_rows, o, first):
        od = jnp.concatenate([o[(2 * h) * n_rows:(2 * h + 1) * n_rows, :]
                              - lam * o[(2 * h + 1) * n_rows:(2 * h + 2) * n_rows, :]
                              for h in range(N_HEADS)], axis=0)
        od = jnp.where(lax.broadcasted_iota(jnp.int32, (1, LANES), 1) < V_ONE, od, 0.0)
        ms = jnp.sum(od * od, axis=-1, keepdims=True) * (1.0 / 64)
        yh = od * lax.rsqrt(ms + EPS) * dn * (1.0 - lam_init)
        y = _heads_to_lanes([yh[h * n_rows:(h + 1) * n_rows, :] for h in range(N_HEADS)])
        if first:
            qrow = lax.broadcasted_iota(jnp.int32, (n_rows, 1), 0)
            y = jnp.where(qrow >= FRONT_PAD, y, 0.0)
        o_ref[0, pl.ds(q0, n_rows), :] = y.astype(BF16)

    stack_queries(0, Q_BLOCK)
    emit(0, Q_BLOCK, _attn_block0(qk_fn, v_fn, n_maps), True)

    def block(i, mid_fn):
        stack_queries(_attn_row0(i), ATT_ROWS)
        _attn_block(i, qk_fn, v_fn, n_maps, s_meta, s_s, mx_s, acc_s, mid_fn)

    def finish(i):
        emit(_attn_row0(i), ATT_ROWS, _attn_result(acc_s), False)

    _attn_all_blocks((tp - Q_BLOCK) // ATT_ROWS, block, finish)


def _diff(pd, lam_rows, dn, lam_init):
    b, tp, _ = pd.shape
    n_maps = 2 * N_HEADS
    return pl.pallas_call(
        functools.partial(_diff_kernel, tp, lam_init),
        grid=(b,),
        in_specs=[pl.BlockSpec((1, tp, PD_W), lambda i: (i, 0, 0)),
                  _resident(lam_rows.shape), _resident((1, LANES))],
        out_specs=pl.BlockSpec((1, tp, 256), lambda i: (i, 0, 0)),
        out_shape=jax.ShapeDtypeStruct((b, tp, 256), BF16),
        scratch_shapes=[pltpu.VMEM((n_maps * ATT_ROWS, 256), BF16),
                        pltpu.VMEM((N_HEADS, tp, LANES), BF16)] + _attn_scratch(n_maps, tp),
        compiler_params=_cparams(),
        name="diffattn",
    )(pd, lam_rows, dn)


def _pad_cols(x, width):
    return jnp.pad(x, ((0, 0), (0, width - x.shape[1])))


def _rot_split(w):
    d = w.shape[0]
    return w.reshape(d, N_HEADS, 2, 32).transpose(0, 2, 1, 3).reshape(d, 256)


def _layout_w_in(w):
    sizes = (128, 128, 256, 16, 256, 256, 256, 256, 256, 192, 64, 32, 256, 256, 256)
    offs = [0]
    for s_ in sizes:
        offs.append(offs[-1] + s_)
    seg = [w[:, offs[i]:offs[i + 1]] for i in range(len(sizes))]
    (a_q, a_k, a_v, a_lr, a_og, r_q, r_k, r_v, r_og, c_cq, c_ckv, c_kpe, d_q, d_k, d_v) = seg
    d = w.shape[0]
    z = lambda n: jnp.zeros((d, n), w.dtype)
    cols = [a_q, a_k, a_v, a_og,
            _rot_split(r_q), _rot_split(r_k), r_v, r_og,
            c_cq, c_ckv, a_lr, z(48), c_kpe, z(32),
            d_q, d_k, d_v]
    return jnp.concatenate(cols, axis=1).astype(BF16)


def _tables(tp):
    pos = jnp.arange(tp, dtype=F32) - FRONT_PAD
    inv = ROPE_THETA ** (-jnp.arange(32, dtype=F32) / 32)
    ang = pos[:, None] * inv[None, :]
    ret_cos = jnp.tile(jnp.cos(ang), (1, N_HEADS))
    ret_sin = jnp.tile(jnp.sin(ang), (1, N_HEADS))
    inv16 = ROPE_THETA ** (-jnp.arange(16, dtype=F32) / 16)
    ang16 = pos[:, None] * inv16[None, :]
    c16, s16 = jnp.cos(ang16), jnp.sin(ang16)
    one = lambda n: jnp.ones((tp, n), F32)
    zero = lambda n: jnp.zeros((tp, n), F32)
    mla_cos = jnp.concatenate([one(64), c16, c16, one(32)], axis=1)
    mla_sa = jnp.concatenate([zero(80), s16, zero(32)], axis=1)
    mla_sb = jnp.concatenate([zero(64), -s16, zero(48)], axis=1)
    lg = jnp.log(1.0 - jnp.exp2(-5.0 - jnp.arange(N_HEADS, dtype=F32)))
    idx = jnp.arange(RET_BLOCK, dtype=F32)
    rel = idx[:, None] - idx[None, :]
    dmask = jnp.where(rel[None] >= 0, jnp.exp(rel[None] * lg[:, None, None]), 0.0)
    dmask = dmask.reshape(N_HEADS * RET_BLOCK, RET_BLOCK)
    lane_head = (jnp.arange(256) % 128) // 32
    qfac = jnp.exp((idx[:, None] + 1.0) * lg[lane_head][None, :])
    kfac = jnp.exp((RET_BLOCK - 1.0 - idx[:, None]) * lg[lane_head][None, :])
    dec = jnp.exp(RET_BLOCK * lg[lane_head])[:, None]
    return ret_cos, ret_sin, mla_cos, mla_sa, mla_sb, dmask, qfac, kfac, dec


def kernel(x, meta_tokens, attn_norm, w_in, gla_w_gate, gla_b_gate, gla_norm, ret_norm, mla_q_norm, mla_w_uq, mla_kv_norm, mla_w_ukv, diff_lambda, diff_norm, w_out, ffn_norm, ffn_w_gate, ffn_w_up, ffn_w_down, moe_router, moe_w_gate, moe_w_up, moe_w_down, final_norm):
    b, seq, d = x.shape
    tp = FRONT_PAD + N_META + seq
    n = b * tp
    meta = jnp.broadcast_to(meta_tokens[None].astype(x.dtype), (b, N_META, d))
    h = jnp.concatenate([jnp.zeros((b, FRONT_PAD, d), x.dtype), meta, x], axis=1).reshape(n, d)
    ret_cos, ret_sin, mla_cos, mla_sa, mla_sb, dmask, qfac, kfac, dec = _tables(tp)

    for li in range(DEPTH):
        pa, pb, pc, pd = _inproj(h, attn_norm[li][None, :], _layout_w_in(w_in[li]))
        pa, pb, pc, pd = (p.reshape(b, tp, -1) for p in (pa, pb, pc, pd))

        wgate = jnp.pad(gla_w_gate[li], ((0, LANES - GLA_GATE_RANK), (0, 0)))
        o_a = _gla(pa, pc, wgate, gla_b_gate[li][None, :], jnp.tile(gla_norm[li], N_HEADS)[None, :])
        o_b = _ret(pb, ret_cos, ret_sin, dmask, qfac, kfac, dec, jnp.tile(ret_norm[li], N_HEADS)[None, :])

        qn = _pad_cols(mla_q_norm[li][None, :], 256)
        kvn = jnp.pad(mla_kv_norm[li][None, :], ((0, 0), (MLA_Q_RANK, 0)))
        wuq = jnp.pad(mla_w_uq[li].reshape(MLA_Q_RANK, N_HEADS, MLA_NOPE + MLA_ROPE),
                      ((0, 256 - MLA_Q_RANK), (0, 0), (0, LANES - MLA_NOPE - MLA_ROPE)))
        wuq = wuq.reshape(256, N_HEADS * LANES).astype(BF16)
        wukv = mla_w_ukv[li].reshape(MLA_KV_RANK, N_HEADS, 2, 64)
        wukv = jnp.pad(wukv, ((MLA_Q_RANK, 0), (0, 0), (0, 0), (0, 64)))
        wukv = wukv.transpose(0, 2, 1, 3).reshape(256, 2 * N_HEADS * LANES).astype(BF16)
        o_c = _mla(pc, qn, kvn, wuq, wukv, mla_cos, mla_sa, mla_sb)

        lam_init = 0.8 - 0.6 * math.exp(-0.3 * li)
        o_d = _diff(pd, diff_lambda[li], _pad_cols(diff_norm[li][None, :], LANES), lam_init)

        o_a, o_b, o_c, o_d = (o.reshape(n, 256) for o in (o_a, o_b, o_c, o_d))
        wo = w_out[li].astype(BF16)
        fn = ffn_norm[li][None, :]
        j = li // 2
        if li % 2 == 0:
            h, hn = _outproj(h, o_a, o_b, o_c, o_d, wo, fn)
            h = _swiglu(h, hn, ffn_w_gate[j].astype(BF16), ffn_w_up[j].astype(BF16),
                        ffn_w_down[j].astype(BF16))
        else:
            h, hn, info, cnt = _outproj(h, o_a, o_b, o_c, o_d, wo, fn,
                                        router=_pad_cols(moe_router[j], LANES), seq_rows=tp)
            return _routed_moe_norm(h, hn, info, cnt, moe_w_gate[j].astype(BF16),
                                    moe_w_up[j].astype(BF16), moe_w_down[j].astype(BF16),
                                    final_norm[None, :], b, tp)
```

```python
import functools
import math

import jax
import jax.numpy as jnp
from jax import lax
from jax.experimental import pallas as pl
from jax.experimental.pallas import tpu as pltpu

F32 = jnp.float32
BF16 = jnp.bfloat16

D_MODEL = 1024
DEPTH = 2
N_META = 16
CHUNK = 64
Q_BLOCK = 128
FRONT_PAD = Q_BLOCK - N_META
EPS = 1e-6
NEG = -1e30
ROPE_THETA = 10000.0
N_HEADS = 4
GLA_DK = 32
GLA_DV = 64
GLA_GATE_RANK = 16
GLA_TAU = 16.0
RET_DK = 64
MLA_Q_RANK = 192
MLA_KV_RANK = 64
MLA_NOPE = 64
MLA_ROPE = 32
DIFF_DK = 32
D_FF = 2816
N_EXPERTS = 8
D_FF_EXPERT = 3584

LANES = 128
SUBLANES = 8
ROW_TILE = 512
FF_CHUNK = 256
VMEM_LIMIT = 56 * 1024 * 1024

PA_W = 768
PB_W = 1024
PC_W = 384
PD_W = 768


def _cparams(n_axes=1):
    return pltpu.CompilerParams(dimension_semantics=("arbitrary",) * n_axes,
                                vmem_limit_bytes=VMEM_LIMIT)


def _resident(shape):
    nd = len(shape)
    return pl.BlockSpec(shape, lambda *_: (0,) * nd, pipeline_mode=pl.Buffered(1))


def _sigmoid(x):
    return 1.0 / (1.0 + jnp.exp(-x))


def _split_bf16(x):
    hi = x.astype(BF16)
    lo = (x - hi.astype(F32)).astype(BF16)
    return hi, lo


def _dot(a, b):
    return jnp.dot(a, b, preferred_element_type=F32)


def _dot_nt(a, b):
    return lax.dot_general(a, b, (((1,), (1,)), ((), ())), preferred_element_type=F32)


def _inproj_kernel(h_ref, g_ref, w_ref, pa_ref, pb_ref, pc_ref, pd_ref):
    x = h_ref[...]
    ms = jnp.mean(x * x, axis=-1, keepdims=True)
    y = (x * lax.rsqrt(ms + EPS) * g_ref[...]).astype(BF16)
    off = 0
    for o_ref, width in ((pa_ref, PA_W), (pb_ref, PB_W), (pc_ref, PC_W), (pd_ref, PD_W)):
        o_ref[...] = _dot(y, w_ref[:, off:off + width]).astype(BF16)
        off += width


def _inproj(h, g, w):
    n = h.shape[0]
    wtot = PA_W + PB_W + PC_W + PD_W
    row = lambda width: pl.BlockSpec((ROW_TILE, width), lambda i: (i, 0))
    return pl.pallas_call(
        _inproj_kernel,
        grid=(n // ROW_TILE,),
        in_specs=[row(D_MODEL), _resident((1, D_MODEL)), _resident((D_MODEL, wtot))],
        out_specs=[row(PA_W), row(PB_W), row(PC_W), row(PD_W)],
        out_shape=[jax.ShapeDtypeStruct((n, w_), BF16) for w_ in (PA_W, PB_W, PC_W, PD_W)],
        compiler_params=_cparams(),
        name="inproj",
    )(h, g, w)


def _outproj_kernel(router_seq_rows, h_ref, oa_ref, ob_ref, oc_ref, od_ref, wo_ref, fn_ref, *rest):
    with_router = router_seq_rows > 0
    if with_router:
        router_ref, below_ref, hmid_ref, hn_ref, info_ref, cnt_ref, carry_ref = rest
    else:
        hmid_ref, hn_ref = rest
    o = jnp.concatenate([oa_ref[...], ob_ref[...], oc_ref[...], od_ref[...]], axis=1)
    hm = h_ref[...] + _dot(o, wo_ref[...])
    hmid_ref[...] = hm
    ms = jnp.mean(hm * hm, axis=-1, keepdims=True)
    y = hm * lax.rsqrt(ms + EPS) * fn_ref[...]
    hn_ref[...] = y.astype(hn_ref.dtype)
    if with_router:
        y_hi, y_lo = _split_bf16(y)
        r_hi, r_lo = _split_bf16(router_ref[...])
        logits = _dot(y_hi, r_hi) + _dot(y_hi, r_lo) + _dot(y_lo, r_hi)
        lane = lax.broadcasted_iota(jnp.int32, logits.shape, 1).astype(F32)
        ninf = float("-inf")
        logits = jnp.where(lane < N_EXPERTS, logits, ninf)
        m1 = jnp.max(logits, axis=-1, keepdims=True)
        i1 = jnp.min(jnp.where(logits == m1, lane, float(LANES)), axis=-1, keepdims=True)
        rest_l = jnp.where(lane == i1, ninf, logits)
        m2 = jnp.max(rest_l, axis=-1, keepdims=True)
        i2 = jnp.min(jnp.where(rest_l == m2, lane, float(LANES)), axis=-1, keepdims=True)
        e2 = jnp.exp(m2 - m1)
        den = 1.0 + e2
        @pl.when(pl.program_id(0) == 0)
        def _():
            carry_ref[...] = jnp.zeros_like(carry_ref)
        sel = jnp.where(lane == i1, 1.0, 0.0) + jnp.where(lane == i2, 1.0, 0.0)
        rows = sel.shape[0]
        seq_pos = (lax.rem(pl.program_id(0) * rows, router_seq_rows)
                   + lax.broadcasted_iota(jnp.int32, (rows, 1), 0))
        seq_pos = jnp.where(seq_pos >= router_seq_rows, seq_pos - router_seq_rows, seq_pos)
        sel = jnp.where(seq_pos >= Q_BLOCK, sel, 0.0)
        count = _dot(below_ref[...], sel.astype(BF16)) + carry_ref[...]
        r1 = jnp.sum(jnp.where(lane == i1, count, 0.0), axis=-1, keepdims=True)
        r2 = jnp.sum(jnp.where(lane == i2, count, 0.0), axis=-1, keepdims=True)
        total = carry_ref[...] + jnp.sum(sel, axis=0, keepdims=True)
        carry_ref[...] = total
        cnt_ref[...] = jnp.broadcast_to(total, cnt_ref.shape)
        info = jnp.where(lane == 0, i1, 0.0)
        for k, val in enumerate((i2, 1.0 / den, e2 / den, r1, r2), start=1):
            info = jnp.where(lane == k, val, info)
        info_ref[...] = info


def _outproj(h, oa, ob, oc, od, wo, fn, router=None, seq_rows=0):
    n = h.shape[0]
    row = lambda width: pl.BlockSpec((ROW_TILE, width), lambda i: (i, 0))
    in_specs = [row(D_MODEL), row(256), row(256), row(256), row(256),
                _resident((D_MODEL, D_MODEL)), _resident((1, D_MODEL))]
    out_specs = [row(D_MODEL), row(D_MODEL)]
    out_shape = [jax.ShapeDtypeStruct((n, D_MODEL), F32),
                 jax.ShapeDtypeStruct((n, D_MODEL), BF16 if router is None else F32)]
    args = [h, oa, ob, oc, od, wo, fn]
    scratch = []
    if router is not None:
        in_specs += [_resident((D_MODEL, LANES)), _resident((ROW_TILE, ROW_TILE))]
        out_specs += [row(LANES), pl.BlockSpec((8, LANES), lambda i: (0, 0))]
        out_shape += [jax.ShapeDtypeStruct((n, LANES), F32), jax.ShapeDtypeStruct((8, LANES), F32)]
        args += [router, jnp.tri(ROW_TILE, k=-1, dtype=BF16)]
        scratch.append(pltpu.VMEM((1, LANES), F32))
    return pl.pallas_call(
        functools.partial(_outproj_kernel, seq_rows if router is not None else 0),
        grid=(n // ROW_TILE,),
        in_specs=in_specs, out_specs=out_specs, out_shape=out_shape,
        scratch_shapes=scratch,
        compiler_params=_cparams(),
        name="outproj_router" if router is not None else "outproj",
    )(*args)


def _dispatch_kernel(n_fill, zs_ref, ze_ref, pos_ref, x_ref, xs_ref, zero_buf, sem, zsem):
    def row_copy(rb, u, k):
        p = pos_ref[0, 0, rb * (2 * SUBLANES) + 2 * u + k]
        return pltpu.make_async_copy(x_ref.at[rb, pl.ds(u, 1), :], xs_ref.at[pl.ds(p, 1), :], sem)

    def start(rb, carry):
        for u in range(SUBLANES):
            row_copy(rb, u, 0).start(priority=0)
            row_copy(rb, u, 1).start(priority=1)
        return carry

    def wait(rb, carry):
        for u in range(SUBLANES):
            row_copy(rb, u, 0).wait()
            row_copy(rb, u, 1).wait()
        return carry

    lax.fori_loop(0, x_ref.shape[0], start, 0)
    lax.fori_loop(0, x_ref.shape[0], wait, 0)

    @pl.when(pl.program_id(0) == pl.num_programs(0) - 1)
    def _():
        zero_buf[...] = jnp.zeros_like(zero_buf)

        def zero_copy(r):
            return pltpu.make_async_copy(zero_buf.at[pl.ds(0, 1), :], xs_ref.at[pl.ds(r, 1), :], zsem)

        def fill(r, carry):
            zero_copy(r).start()
            return carry

        for k in range(zs_ref.shape[0]):
            lax.fori_loop(zs_ref[k], ze_ref[k], fill, 0)

        def drain(r, carry):
            zero_copy(0).wait()
            return carry

        lax.fori_loop(0, n_fill, drain, 0)


def _dispatch(zero_start, zero_end, pos, x, n_sorted, n_fill):
    n, d = x.shape
    return pl.pallas_call(
        functools.partial(_dispatch_kernel, n_fill),
        grid_spec=pltpu.PrefetchScalarGridSpec(
            num_scalar_prefetch=2,
            grid=(n // ROW_TILE,),
            in_specs=[pl.BlockSpec((1, 1, 2 * ROW_TILE), lambda i, zs, ze: (i, 0, 0),
                                   memory_space=pltpu.SMEM),
                      pl.BlockSpec((ROW_TILE // SUBLANES, SUBLANES, d), lambda i, zs, ze: (i, 0, 0))],
            out_specs=pl.BlockSpec(memory_space=pl.ANY),
            scratch_shapes=[pltpu.VMEM((SUBLANES, d), x.dtype), pltpu.SemaphoreType.DMA(()),
                            pltpu.SemaphoreType.DMA(())]),
        out_shape=jax.ShapeDtypeStruct((n_sorted, d), x.dtype),
        compiler_params=_cparams(),
        name="moe_dispatch",
    )(zero_start, zero_end, pos, x.reshape(n // SUBLANES, SUBLANES, d))


def _swiglu_tile(x, wg, wu, wd, acc_ref):
    n_chunks = wd.shape[0] // FF_CHUNK
    for c in range(n_chunks):
        cols = slice(c * FF_CHUNK, (c + 1) * FF_CHUNK)
        g = _dot(x, wg[:, cols])
        u = _dot(x, wu[:, cols])
        part = _dot((g * _sigmoid(g) * u).astype(BF16), wd[cols, :])
        if c == 0:
            acc_ref[...] = part
        elif c < n_chunks - 1:
            acc_ref[...] += part
    return acc_ref[...] + part


def _experts_kernel(te_ref, tv_ref, xs_ref, wg_ref, wu_ref, wd_ref, ys_ref, acc_ref):
    i = pl.program_id(0)

    @pl.when(tv_ref[i] > 0)
    def _():
        ys_ref[...] = _swiglu_tile(xs_ref[...].astype(BF16), wg_ref.at[0], wu_ref.at[0],
                                   wd_ref.at[0], acc_ref)

    @pl.when(tv_ref[i] == 0)
    def _():
        ys_ref[...] = jnp.zeros_like(ys_ref)


def _experts(tile_expert, tile_valid, xs, wg, wu, wd, n_sorted):
    d = xs.shape[1]
    wspec = lambda shape: pl.BlockSpec((1,) + shape[1:], lambda i, te, tv: (te[i], 0, 0),
                                       pipeline_mode=pl.Buffered(1))
    row = pl.BlockSpec((ROW_TILE, d), lambda i, te, tv: (i, 0))
    return pl.pallas_call(
        _experts_kernel,
        grid_spec=pltpu.PrefetchScalarGridSpec(
            num_scalar_prefetch=2,
            grid=(n_sorted // ROW_TILE,),
            in_specs=[row, wspec(wg.shape), wspec(wu.shape), wspec(wd.shape)],
            out_specs=row,
            scratch_shapes=[pltpu.VMEM((ROW_TILE, d), F32)]),
        out_shape=jax.ShapeDtypeStruct((n_sorted, d), F32),
        compiler_params=_cparams(),
        name="moe_experts",
    )(tile_expert, tile_valid, xs, wg, wu, wd)


def _combine_norm_kernel(tp, pos_ref, h_ref, info_ref, ys_ref, g_ref, o_ref,
                         h_buf, info_buf, y_buf, sem, row_sem):
    n_rows = h_buf.shape[0]
    row0 = pl.multiple_of(pl.program_id(0) * tp + Q_BLOCK + pl.program_id(1) * n_rows, Q_BLOCK)
    h_copy = pltpu.make_async_copy(h_ref.at[pl.ds(row0, n_rows), :], h_buf, sem.at[0])
    info_copy = pltpu.make_async_copy(info_ref.at[pl.ds(row0, n_rows), :], info_buf, sem.at[1])
    h_copy.start()
    info_copy.start()

    def row_copy(rb, u, k):
        p = pos_ref[0, 0, rb * (2 * SUBLANES) + 2 * u + k]
        return pltpu.make_async_copy(ys_ref.at[pl.ds(p, 1), :], y_buf.at[k, rb, pl.ds(u, 1), :], row_sem)

    def start(rb, carry):
        for u in range(SUBLANES):
            row_copy(rb, u, 0).start(priority=0)
            row_copy(rb, u, 1).start(priority=1)
        return carry

    def wait(rb, carry):
        for u in range(SUBLANES):
            row_copy(rb, u, 0).wait()
            row_copy(rb, u, 1).wait()
        return carry

    lax.fori_loop(0, n_rows // SUBLANES, start, 0)
    lax.fori_loop(0, n_rows // SUBLANES, wait, 0)
    h_copy.wait()
    info_copy.wait()
    info = info_buf[...]
    y0 = y_buf[0].reshape(n_rows, -1)
    y1 = y_buf[1].reshape(n_rows, -1)
    x = h_buf[...] + info[:, 2:3] * y0 + info[:, 3:4] * y1
    ms = jnp.mean(x * x, axis=-1, keepdims=True)
    o_ref[0] = x * lax.rsqrt(ms + EPS) * g_ref[...]


def _combine_norm(pos, h, info, ys, g, b, tp):
    n, d = h.shape
    seq = tp - Q_BLOCK
    tiles = seq // ROW_TILE
    any_spec = pl.BlockSpec(memory_space=pl.ANY)
    return pl.pallas_call(
        functools.partial(_combine_norm_kernel, tp),
        grid=(b, tiles),
        in_specs=[pl.BlockSpec((1, 1, 2 * ROW_TILE), lambda i, j: (i * tiles + j, 0, 0),
                               memory_space=pltpu.SMEM),
                  any_spec, any_spec, any_spec, _resident((1, d))],
        out_specs=pl.BlockSpec((1, ROW_TILE, d), lambda i, j: (i, j, 0)),
        out_shape=jax.ShapeDtypeStruct((b, seq, d), F32),
        scratch_shapes=[pltpu.VMEM((ROW_TILE, d), F32), pltpu.VMEM((ROW_TILE, LANES), F32),
                        pltpu.VMEM((2, ROW_TILE // SUBLANES, SUBLANES, d), F32),
                        pltpu.SemaphoreType.DMA((2,)), pltpu.SemaphoreType.DMA(())],
        compiler_params=_cparams(2),
        name="moe_combine_norm",
    )(pos, h, info, ys, g)


def _routed_moe_norm(h, hn, info, cnt, wg, wu, wd, g, b, tp):
    n = h.shape[0]
    n_routed = b * (tp - Q_BLOCK)
    n_fill = N_EXPERTS * ROW_TILE
    n_main = 2 * n_routed + n_fill
    n_tiles = n_main // ROW_TILE
    counts = cnt[0, :N_EXPERTS].astype(jnp.int32)
    tiles_e = (counts + ROW_TILE - 1) // ROW_TILE
    tile_end = jnp.cumsum(tiles_e)
    row_base = (tile_end - tiles_e) * ROW_TILE
    e_idx = info[:, 0:2].astype(jnp.int32)
    pos = row_base[e_idx] + info[:, 4:6].astype(jnp.int32)
    rows = jnp.arange(n, dtype=jnp.int32)
    seq_pos = rows % tp
    parked = n_main + 2 * ((rows // tp) * Q_BLOCK + seq_pos)[:, None] + jnp.arange(2, dtype=jnp.int32)[None, :]
    pos = jnp.where((seq_pos >= Q_BLOCK)[:, None], pos, parked)
    tile_ids = jnp.arange(n_tiles, dtype=jnp.int32)
    tile_expert = jnp.minimum(jnp.sum((tile_ids[:, None] >= tile_end[None, :]).astype(jnp.int32), axis=1),
                              N_EXPERTS - 1)
    tile_valid = (tile_ids < tile_end[-1]).astype(jnp.int32)
    zero_start = jnp.concatenate([row_base + counts, tile_end[-1:] * ROW_TILE])
    zero_end = jnp.concatenate([tile_end * ROW_TILE, jnp.full((1,), n_main, jnp.int32)])
    xs = _dispatch(zero_start, zero_end, pos.reshape(n // ROW_TILE, 1, 2 * ROW_TILE), hn,
                   n_main + 2 * (n - n_routed), n_fill)
    ys = _experts(tile_expert, tile_valid, xs, wg, wu, wd, n_main)
    pos_seq = pos.reshape(b, tp, 2)[:, Q_BLOCK:, :].reshape(-1, 1, 2 * ROW_TILE)
    return _combine_norm(pos_seq, h, info, ys, g, b, tp)


def _swiglu_kernel(h_ref, hn_ref, wg_ref, wu_ref, wd_ref, o_ref, acc_ref):
    o_ref[...] = h_ref[...] + _swiglu_tile(hn_ref[...], wg_ref, wu_ref, wd_ref, acc_ref)


def _swiglu(h, hn, wg, wu, wd):
    n = h.shape[0]
    row = pl.BlockSpec((ROW_TILE, D_MODEL), lambda i: (i, 0))
    return pl.pallas_call(
        _swiglu_kernel,
        grid=(n // ROW_TILE,),
        in_specs=[row, row, _resident(wg.shape), _resident(wu.shape), _resident(wd.shape)],
        out_specs=row,
        out_shape=jax.ShapeDtypeStruct((n, D_MODEL), F32),
        scratch_shapes=[pltpu.VMEM((ROW_TILE, D_MODEL), F32)],
        input_output_aliases={0: 0},
        compiler_params=_cparams(),
        name="swiglu",
    )(h, hn, wg, wu, wd)


def _group_ones(n, group_shift):
    r = lax.broadcasted_iota(jnp.int32, (n, n), 0) >> group_shift
    c = lax.broadcasted_iota(jnp.int32, (n, n), 1) >> group_shift
    return jnp.where(r == c, 1.0, 0.0).astype(BF16)


GLA_ROWS = 256


def _gla_kernel(tp, pa_ref, lr_ref, wg_ref, bg_ref, gn_ref, o_ref, s_ref):
    qhead = lax.broadcasted_iota(jnp.int32, (1, N_HEADS * GLA_DK), 1) >> 5
    ehead = lax.broadcasted_iota(jnp.int32, (1, N_HEADS * GLA_DV), 1) >> 6
    shead = lax.broadcasted_iota(jnp.int32, (N_HEADS * GLA_DK, 1), 0) >> 5
    bd = shead == ehead
    gsum = _group_ones(N_HEADS * GLA_DV, 6)
    wg_hi, wg_lo = _split_bf16(wg_ref[...])
    bg = bg_ref[...]
    gn = gn_ref[...]
    scale = GLA_DK ** -0.5
    s_ref[...] = jnp.zeros_like(s_ref)

    def block(r0, n_rows):
        n_c = n_rows // CHUNK
        rows = pl.ds(r0, n_rows)
        ri = lax.broadcasted_iota(jnp.int32, (n_rows, n_rows), 0)
        ci = lax.broadcasted_iota(jnp.int32, (n_rows, n_rows), 1)
        same = (ri >> 6) == (ci >> 6)
        tri_bf = jnp.where(jnp.logical_and(same, ri >= ci), 1.0, 0.0).astype(BF16)
        ones_bf = jnp.where(same, 1.0, 0.0).astype(BF16)
        r4 = lax.broadcasted_iota(jnp.int32, (N_HEADS * n_rows, n_rows), 0) & (n_rows - 1)
        c4 = lax.broadcasted_iota(jnp.int32, (N_HEADS * n_rows, n_rows), 1)
        tri4 = jnp.logical_and((r4 >> 6) == (c4 >> 6), r4 >= c4)
        col_chunk = lax.broadcasted_iota(jnp.int32, (1, n_rows), 1) >> 6

        q = pa_ref[0, rows, 0:128].astype(F32) * scale
        k = pa_ref[0, rows, 128:256].astype(F32)
        v = pa_ref[0, rows, 256:512]
        og = pa_ref[0, rows, 512:768].astype(F32)
        lr = lr_ref[0, rows, :]
        valid = (r0 + lax.broadcasted_iota(jnp.int32, (n_rows, 1), 0)) >= FRONT_PAD

        pre = _dot(lr, wg_hi) + _dot(lr, wg_lo) + bg
        logsig = jnp.minimum(pre, 0.0) - jnp.log1p(jnp.exp(-jnp.abs(pre)))
        g = jnp.where(valid, logsig * (1.0 / GLA_TAU), 0.0)
        g_hi, g_lo = _split_bf16(g)
        cum = _dot(tri_bf, g_hi) + _dot(tri_bf, g_lo)
        cum_end = _dot(ones_bf, g_hi) + _dot(ones_bf, g_lo)
        qt_bf = (q * jnp.exp(cum)).astype(BF16)
        kt = (k * jnp.exp(-cum)).astype(BF16)
        kd_t = (k * jnp.exp(cum_end - cum)).T.astype(BF16)
        dec_t = jnp.exp(cum_end.T)

        qs = jnp.concatenate([jnp.where(qhead == h, qt_bf, jnp.zeros_like(qt_bf))
                              for h in range(N_HEADS)], axis=0)
        a = jnp.where(tri4, _dot_nt(qs, kt), 0.0).astype(BF16)
        r = _dot(a, v)
        o = jnp.where(ehead == 0, r[0:n_rows, :], 0.0)
        for h in range(1, N_HEADS):
            o = o + jnp.where(ehead == h, r[h * n_rows:(h + 1) * n_rows, :], 0.0)

        s = s_ref[...]
        inter = []
        for c in range(n_c):
            inter.append(_dot(qt_bf[c * CHUNK:(c + 1) * CHUNK, :], s.astype(BF16)))
            upd = _dot(jnp.where(col_chunk == c, kd_t, jnp.zeros_like(kd_t)), v)
            s = dec_t[:, c * CHUNK:c * CHUNK + 1] * s + jnp.where(bd, upd, 0.0)
        s_ref[...] = s
        o = o + jnp.concatenate(inter, axis=0)

        ms = _dot((o * o).astype(BF16), gsum) * (1.0 / GLA_DV)
        y = o * lax.rsqrt(ms + EPS) * gn * (og * _sigmoid(og))
        o_ref[0, rows, :] = jnp.where(valid, y, 0.0).astype(BF16)

    block(0, Q_BLOCK)

    def body(i, carry):
        block(pl.multiple_of(Q_BLOCK + i * GLA_ROWS, Q_BLOCK), GLA_ROWS)
        return carry

    lax.fori_loop(0, (tp - Q_BLOCK) // GLA_ROWS, body, 0)


def _gla(pa, pc, wg, bg, gn):
    b, tp, _ = pa.shape
    return pl.pallas_call(
        functools.partial(_gla_kernel, tp),
        grid=(b,),
        in_specs=[pl.BlockSpec((1, tp, PA_W), lambda i: (i, 0, 0)),
                  pl.BlockSpec((1, tp, LANES), lambda i: (i, 0, 2)),
                  _resident((LANES, LANES)), _resident((1, LANES)), _resident((1, 256))],
        out_specs=pl.BlockSpec((1, tp, 256), lambda i: (i, 0, 0)),
        out_shape=jax.ShapeDtypeStruct((b, tp, 256), BF16),
        scratch_shapes=[pltpu.VMEM((N_HEADS * GLA_DK, N_HEADS * GLA_DV), F32)],
        compiler_params=_cparams(),
        name="gla",
    )(pa, pc, wg, bg, gn)


RET_BLOCK = 128


def _ret_kernel(tp, pb_ref, cos_ref, sin_ref, dmask_ref, qfac_ref, kfac_ref, dec_ref, gn_ref,
                o_ref, s_ref):
    blk = RET_BLOCK
    n_blocks = tp // blk
    qhead = (lax.broadcasted_iota(jnp.int32, (1, 256), 1) & 127) >> 5
    ehead = lax.broadcasted_iota(jnp.int32, (1, 256), 1) >> 6
    shead = (lax.broadcasted_iota(jnp.int32, (256, 1), 0) & 127) >> 5
    bd = shead == ehead
    gsum = _group_ones(256, 6)
    gn = gn_ref[...]
    s_ref[...] = jnp.zeros_like(s_ref)

    def rope(x, cos, sin):
        x1, x2 = x[:, :128], x[:, 128:]
        return jnp.concatenate([x1 * cos - x2 * sin, x1 * sin + x2 * cos], axis=1)

    def block(j, carry):
        r0 = pl.multiple_of(j * blk, blk)
        rows = pl.ds(r0, blk)
        cos = cos_ref[rows, :]
        sin = sin_ref[rows, :]
        q = rope(pb_ref[0, rows, 0:256].astype(F32), cos, sin)
        k = rope(pb_ref[0, rows, 256:512].astype(F32), cos, sin) * (RET_DK ** -0.5)
        v = pb_ref[0, rows, 512:768]
        og = pb_ref[0, rows, 768:1024].astype(F32)
        valid = (r0 + lax.broadcasted_iota(jnp.int32, (blk, 1), 0)) >= FRONT_PAD

        q_bf = q.astype(BF16)
        qs = jnp.concatenate([jnp.where(qhead == h, q_bf, jnp.zeros_like(q_bf))
                              for h in range(N_HEADS)], axis=0)
        a = (_dot_nt(qs, k.astype(BF16)) * dmask_ref[...]).astype(BF16)
        r = _dot(a, v)
        o = _dot((q * qfac_ref[...]).astype(BF16), s_ref[...].astype(BF16))
        for h in range(N_HEADS):
            o = o + jnp.where(ehead == h, r[h * blk:(h + 1) * blk, :], 0.0)

        kd_t = (k * kfac_ref[...]).T.astype(BF16)
        upd = _dot(kd_t, v)
        s_ref[...] = dec_ref[...] * s_ref[...] + jnp.where(bd, upd, 0.0)

        mu = _dot(o.astype(BF16), gsum) * (1.0 / 64)
        xc = o - mu
        var = _dot((xc * xc).astype(BF16), gsum) * (1.0 / 64)
        y = xc * lax.rsqrt(var + EPS) * gn * (og * _sigmoid(og))
        o_ref[0, rows, :] = jnp.where(valid, y, 0.0).astype(BF16)
        return carry

    lax.fori_loop(0, n_blocks, block, 0, unroll=2)


def _ret(pb, cos, sin, dmask, qfac, kfac, dec, gn):
    b, tp, _ = pb.shape
    return pl.pallas_call(
        functools.partial(_ret_kernel, tp),
        grid=(b,),
        in_specs=[pl.BlockSpec((1, tp, PB_W), lambda i: (i, 0, 0)),
                  _resident(cos.shape), _resident(sin.shape), _resident(dmask.shape),
                  _resident(qfac.shape), _resident(kfac.shape), _resident(dec.shape),
                  _resident((1, 256))],
        out_specs=pl.BlockSpec((1, tp, 256), lambda i: (i, 0, 0)),
        out_shape=jax.ShapeDtypeStruct((b, tp, 256), BF16),
        scratch_shapes=[pltpu.VMEM((256, 256), F32)],
        compiler_params=_cparams(),
        name="retention",
    )(pb, cos, sin, dmask, qfac, kfac, dec, gn)


ATT_ROWS = 256
LOG2E = 1.4426950408889634
V_ONE = 64


def _aligned(x, m):
    return x if isinstance(x, int) else pl.multiple_of(x, m)


def _for_chunks(n, body):
    if isinstance(n, int):
        for j in range(n):
            body(j)
        return

    def quad(t, carry):
        for u in range(4):
            body(4 * t + u)
        return carry

    lax.fori_loop(0, n >> 2, quad, 0)
    done = (n >> 2) << 2

    @pl.when((n & 2) != 0)
    def _():
        body(done)
        body(done + 1)

    @pl.when((n & 1) != 0)
    def _():
        body(done + (n & 2))


def _mask_groups(mask, s, n_g):
    rows = s.shape[0] // n_g
    return jnp.concatenate([jnp.where(mask, s[g * rows:(g + 1) * rows, :], NEG)
                            for g in range(n_g)], axis=0)


def _attn_block0(qk_fn, v_fn, n_g):
    rows = pl.ds(0, Q_BLOCK)
    qrow = lax.broadcasted_iota(jnp.int32, (Q_BLOCK, 1), 0)
    kcol = lax.broadcasted_iota(jnp.int32, (1, Q_BLOCK), 1)
    mask = jnp.logical_and(kcol <= qrow, kcol >= FRONT_PAD)
    s = _mask_groups(mask, qk_fn(0, Q_BLOCK, rows), n_g)
    p_bf = jnp.exp2(s - jnp.max(s, axis=-1, keepdims=True)).astype(BF16)
    pv = jnp.concatenate([_dot(p_bf[g * Q_BLOCK:(g + 1) * Q_BLOCK, :], v_fn(g, rows))
                          for g in range(n_g)], axis=0)
    return pv / pv[:, V_ONE:V_ONE + 1]


def _attn_row0(i):
    return _aligned(Q_BLOCK + (i - 1) * ATT_ROWS, Q_BLOCK)


def _attn_result(acc_s):
    acc = acc_s[...]
    return acc / acc[:, V_ONE:V_ONE + 1]


def _attn_block(i, qk_fn, v_fn, n_g, s_meta, s_s, mx_s, acc_s, mid_fn=None):
    rr = ATT_ROWS
    q0 = _attn_row0(i)
    meta_rows = pl.ds(0, Q_BLOCK)
    kcol = lax.broadcasted_iota(jnp.int32, (1, Q_BLOCK), 1)

    def key_rows(j):
        return pl.ds(_aligned(Q_BLOCK + j * rr, Q_BLOCK), rr)

    s = jnp.where(kcol >= FRONT_PAD, qk_fn(q0, rr, meta_rows), NEG)
    s_meta[...] = s
    mx_s[...] = s

    def pass1(j):
        sj = qk_fn(q0, rr, key_rows(j))
        s_s[j] = sj
        mx_s[...] = jnp.maximum(mx_s[...], jnp.maximum(sj[:, :LANES], sj[:, LANES:]))

    _for_chunks(i - 1, pass1)
    if mid_fn is not None:
        mid_fn()
    causal = (lax.broadcasted_iota(jnp.int32, (rr, rr), 1)
              <= lax.broadcasted_iota(jnp.int32, (rr, rr), 0))
    sd = _mask_groups(causal, qk_fn(q0, rr, pl.ds(q0, rr)), n_g)
    s_s[i - 1] = sd
    m = jnp.max(jnp.maximum(mx_s[...], jnp.maximum(sd[:, :LANES], sd[:, LANES:])),
                axis=-1, keepdims=True)
    mx_s[...] = jnp.broadcast_to(m, mx_s.shape)

    p_bf = jnp.exp2(s_meta[...] - mx_s[...]).astype(BF16)
    for g in range(n_g):
        acc_s[g * rr:(g + 1) * rr, :] = _dot(p_bf[g * rr:(g + 1) * rr, :], v_fn(g, meta_rows))

    def pass2(j):
        sj = s_s[j]
        mrep = mx_s[...]
        p0 = jnp.exp2(sj[:, :LANES] - mrep)
        p1 = jnp.exp2(sj[:, LANES:] - mrep)
        pj = jnp.concatenate([p0.astype(BF16), p1.astype(BF16)], axis=1)
        for g in range(n_g):
            acc_s[g * rr:(g + 1) * rr, :] += _dot(pj[g * rr:(g + 1) * rr, :], v_fn(g, key_rows(j)))

    _for_chunks(i, pass2)


def _attn_all_blocks(n_q, block_fn, finish_fn):
    block_fn(1, None)

    def body(i, carry):
        block_fn(i, lambda: finish_fn(i - 1))
        return carry

    lax.fori_loop(2, n_q + 1, body, 0)
    finish_fn(n_q)


def _heads_to_lanes(per_head):
    low = lax.broadcasted_iota(jnp.int32, (1, LANES), 1) < 64
    lo = jnp.where(low, per_head[0], pltpu.roll(per_head[1], 64, 1))
    hi = jnp.where(low, per_head[2], pltpu.roll(per_head[3], 64, 1))
    return jnp.concatenate([lo, hi], axis=1)


def _with_ones_lane(v):
    lane = lax.broadcasted_iota(jnp.int32, (1, LANES), 1)
    return jnp.where(lane == V_ONE, jnp.ones_like(v), v)


def _mla_kernel(tp, pc_ref, qn_ref, kvn_ref, wuq_ref, wukv_ref, cos_ref, sa_ref, sb_ref,
                o_ref, q_s, k_s, v_s, s_meta, s_s, mx_s, acc_s):
    n_blocks = tp // Q_BLOCK
    scale = (MLA_NOPE + MLA_ROPE) ** -0.5 * LOG2E
    is_q = lax.broadcasted_iota(jnp.int32, (1, 256), 1) < MLA_Q_RANK

    def prep(i, carry):
        r0 = pl.multiple_of(i * Q_BLOCK, Q_BLOCK)
        rows = pl.ds(r0, Q_BLOCK)
        x = pc_ref[0, rows, 0:256].astype(F32)
        x2 = x * x
        ms_q = jnp.sum(jnp.where(is_q, x2, 0.0), axis=-1, keepdims=True) * (1.0 / MLA_Q_RANK)
        ms_kv = jnp.sum(jnp.where(is_q, 0.0, x2), axis=-1, keepdims=True) * (1.0 / MLA_KV_RANK)
        yq = (x * lax.rsqrt(ms_q + EPS) * qn_ref[...]).astype(BF16)
        ykv = (x * lax.rsqrt(ms_kv + EPS) * kvn_ref[...]).astype(BF16)
        cq = _dot(yq, wuq_ref[...])
        kv = _dot(ykv, wukv_ref[...])
        cos = cos_ref[rows, :]
        sa = sa_ref[rows, :]
        sb = sb_ref[rows, :]

        def rope(t):
            return t * cos + pltpu.roll(t, 16, 1) * sa + pltpu.roll(t, LANES - 16, 1) * sb

        kpe_in = pc_ref[0, rows, 256:384].astype(F32)
        kpe = rope(jnp.where(lax.broadcasted_iota(jnp.int32, (1, LANES), 1) >= 64, kpe_in, 0.0))
        for h in range(N_HEADS):
            q_s[h, rows, :] = (rope(cq[:, h * LANES:(h + 1) * LANES]) * scale).astype(BF16)
            k_s[h, rows, :] = (kv[:, h * LANES:(h + 1) * LANES] + kpe).astype(BF16)
            v_s[h, rows, :] = _with_ones_lane(
                kv[:, (N_HEADS + h) * LANES:(N_HEADS + h + 1) * LANES]).astype(BF16)
        return carry

    lax.fori_loop(0, n_blocks, prep, 0)

    def qk_fn(q0, n_rows, krows):
        return jnp.concatenate([_dot_nt(q_s[h, pl.ds(q0, n_rows), :], k_s[h, krows, :])
                                for h in range(N_HEADS)], axis=0)

    def v_fn(h, krows):
        return v_s[h, krows, :]

    def emit(q0, n_rows, o, first):
        y = _heads_to_lanes([o[h * n_rows:(h + 1) * n_rows, :] for h in range(N_HEADS)])
        if first:
            qrow = lax.broadcasted_iota(jnp.int32, (n_rows, 1), 0)
            y = jnp.where(qrow >= FRONT_PAD, y, 0.0)
        o_ref[0, pl.ds(q0, n_rows), :] = y.astype(BF16)

    emit(0, Q_BLOCK, _attn_block0(qk_fn, v_fn, N_HEADS), True)

    def block(i, mid_fn):
        _attn_block(i, qk_fn, v_fn, N_HEADS, s_meta, s_s, mx_s, acc_s, mid_fn)

    def finish(i):
        emit(_attn_row0(i), ATT_ROWS, _attn_result(acc_s), False)

    _attn_all_blocks((tp - Q_BLOCK) // ATT_ROWS, block, finish)


def _attn_scratch(n_g, tp):
    g_rows = n_g * ATT_ROWS
    n_slots = (tp - Q_BLOCK) // ATT_ROWS
    return [pltpu.VMEM((g_rows, LANES), F32),
            pltpu.VMEM((n_slots, g_rows, ATT_ROWS), F32),
            pltpu.VMEM((g_rows, LANES), F32),
            pltpu.VMEM((g_rows, LANES), F32)]


def _mla(pc, qn, kvn, wuq, wukv, cos, sa, sb):
    b, tp, _ = pc.shape
    return pl.pallas_call(
        functools.partial(_mla_kernel, tp),
        grid=(b,),
        in_specs=[pl.BlockSpec((1, tp, PC_W), lambda i: (i, 0, 0)),
                  _resident((1, 256)), _resident((1, 256)),
                  _resident(wuq.shape), _resident(wukv.shape),
                  _resident(cos.shape), _resident(sa.shape), _resident(sb.shape)],
        out_specs=pl.BlockSpec((1, tp, 256), lambda i: (i, 0, 0)),
        out_shape=jax.ShapeDtypeStruct((b, tp, 256), BF16),
        scratch_shapes=[pltpu.VMEM((N_HEADS, tp, LANES), BF16),
                        pltpu.VMEM((N_HEADS, tp, LANES), BF16),
                        pltpu.VMEM((N_HEADS, tp, LANES), BF16)] + _attn_scratch(N_HEADS, tp),
        compiler_params=_cparams(),
        name="mla",
    )(pc, qn, kvn, wuq, wukv, cos, sa, sb)


def _diff_kernel(tp, lam_init, pd_ref, lam_ref, dn_ref, o_ref, qs_s, v_s, s_meta, s_s, mx_s, acc_s):
    n_maps = 2 * N_HEADS

    def fill_values(i, carry):
        rows = pl.ds(pl.multiple_of(i * Q_BLOCK, Q_BLOCK), Q_BLOCK)
        low = lax.broadcasted_iota(jnp.int32, (1, LANES), 1) < V_ONE
        for pair in range(N_HEADS // 2):
            two = pd_ref[0, rows, 512 + pair * LANES:512 + (pair + 1) * LANES].astype(F32)
            for h, vals in ((2 * pair, two), (2 * pair + 1, pltpu.roll(two, V_ONE, 1))):
                v_s[h, rows, :] = _with_ones_lane(jnp.where(low, vals, 0.0)).astype(BF16)
        return carry

    lax.fori_loop(0, tp // Q_BLOCK, fill_values, 0)
    scale = DIFF_DK ** -0.5 * LOG2E
    group = lax.broadcasted_iota(jnp.int32, (1, 256), 1) >> 5
    lv = lam_ref[...]
    lam = (jnp.exp(jnp.sum(lv[0:1, :] * lv[1:2, :], axis=-1, keepdims=True))
           - jnp.exp(jnp.sum(lv[2:3, :] * lv[3:4, :], axis=-1, keepdims=True)) + lam_init)
    dn = dn_ref[...]

    def stack_queries(q0, n_rows):
        q = (pd_ref[0, pl.ds(q0, n_rows), 0:256].astype(F32) * scale).astype(BF16)
        for g in range(n_maps):
            qs_s[g * n_rows:(g + 1) * n_rows, :] = jnp.where(group == g, q, jnp.zeros_like(q))

    def qk_fn(q0, n_rows, krows):
        return _dot_nt(qs_s[0:n_maps * n_rows, :], pd_ref[0, krows, 256:512])

    def v_fn(g, krows):
        return v_s[g // 2, krows, :]

    def emit(q0, n_rows, o, first):
        od = jnp.concatenate([o[(2 * h) * n_rows:(2 * h + 1) * n_rows, :]
                              - lam * o[(2 * h + 1) * n_rows:(2 * h + 2) * n_rows, :]
                              for h in range(N_HEADS)], axis=0)
        od = jnp.where(lax.broadcasted_iota(jnp.int32, (1, LANES), 1) < V_ONE, od, 0.0)
        ms = jnp.sum(od * od, axis=-1, keepdims=True) * (1.0 / 64)
        yh = od * lax.rsqrt(ms + EPS) * dn * (1.0 - lam_init)
        y = _heads_to_lanes([yh[h * n_rows:(h + 1) * n_rows, :] for h in range(N_HEADS)])
        if first:
            qrow = lax.broadcasted_iota(jnp.int32, (n_rows, 1), 0)
            y = jnp.where(qrow >= FRONT_PAD, y, 0.0)
        o_ref[0, pl.ds(q0, n_rows), :] = y.astype(BF16)

    stack_queries(0, Q_BLOCK)
    emit(0, Q_BLOCK, _attn_block0(qk_fn, v_fn, n_maps), True)

    def block(i, mid_fn):
        stack_queries(_attn_row0(i), ATT_ROWS)
        _attn_block(i, qk_fn, v_fn, n_maps, s_meta, s_s, mx_s, acc_s, mid_fn)

    def finish(i):
        emit(_attn_row0(i), ATT_ROWS, _attn_result(acc_s), False)

    _attn_all_blocks((tp - Q_BLOCK) // ATT_ROWS, block, finish)


def _diff(pd, lam_rows, dn, lam_init):
    b, tp, _ = pd.shape
    n_maps = 2 * N_HEADS
    return pl.pallas_call(
        functools.partial(_diff_kernel, tp, lam_init),
        grid=(b,),
        in_specs=[pl.BlockSpec((1, tp, PD_W), lambda i: (i, 0, 0)),
                  _resident(lam_rows.shape), _resident((1, LANES))],
        out_specs=pl.BlockSpec((1, tp, 256), lambda i: (i, 0, 0)),
        out_shape=jax.ShapeDtypeStruct((b, tp, 256), BF16),
        scratch_shapes=[pltpu.VMEM((n_maps * ATT_ROWS, 256), BF16),
                        pltpu.VMEM((N_HEADS, tp, LANES), BF16)] + _attn_scratch(n_maps, tp),
        compiler_params=_cparams(),
        name="diffattn",
    )(pd, lam_rows, dn)


def _pad_cols(x, width):
    return jnp.pad(x, ((0, 0), (0, width - x.shape[1])))


def _rot_split(w):
    d = w.shape[0]
    return w.reshape(d, N_HEADS, 2, 32).transpose(0, 2, 1, 3).reshape(d, 256)


def _layout_w_in(w):
    sizes = (128, 128, 256, 16, 256, 256, 256, 256, 256, 192, 64, 32, 256, 256, 256)
    offs = [0]
    for s_ in sizes:
        offs.append(offs[-1] + s_)
    seg = [w[:, offs[i]:offs[i + 1]] for i in range(len(sizes))]
    (a_q, a_k, a_v, a_lr, a_og, r_q, r_k, r_v, r_og, c_cq, c_ckv, c_kpe, d_q, d_k, d_v) = seg
    d = w.shape[0]
    z = lambda n: jnp.zeros((d, n), w.dtype)
    cols = [a_q, a_k, a_v, a_og,
            _rot_split(r_q), _rot_split(r_k), r_v, r_og,
            c_cq, c_ckv, a_lr, z(48), c_kpe, z(32),
            d_q, d_k, d_v]
    return jnp.concatenate(cols, axis=1).astype(BF16)


def _tables(tp):
    pos = jnp.arange(tp, dtype=F32) - FRONT_PAD
    inv = ROPE_THETA ** (-jnp.arange(32, dtype=F32) / 32)
    ang = pos[:, None] * inv[None, :]
    ret_cos = jnp.tile(jnp.cos(ang), (1, N_HEADS))
    ret_sin = jnp.tile(jnp.sin(ang), (1, N_HEADS))
    inv16 = ROPE_THETA ** (-jnp.arange(16, dtype=F32) / 16)
    ang16 = pos[:, None] * inv16[None, :]
    c16, s16 = jnp.cos(ang16), jnp.sin(ang16)
    one = lambda n: jnp.ones((tp, n), F32)
    zero = lambda n: jnp.zeros((tp, n), F32)
    mla_cos = jnp.concatenate([one(64), c16, c16, one(32)], axis=1)
    mla_sa = jnp.concatenate([zero(80), s16, zero(32)], axis=1)
    mla_sb = jnp.concatenate([zero(64), -s16, zero(48)], axis=1)
    lg = jnp.log(1.0 - jnp.exp2(-5.0 - jnp.arange(N_HEADS, dtype=F32)))
    idx = jnp.arange(RET_BLOCK, dtype=F32)
    rel = idx[:, None] - idx[None, :]
    dmask = jnp.where(rel[None] >= 0, jnp.exp(rel[None] * lg[:, None, None]), 0.0)
    dmask = dmask.reshape(N_HEADS * RET_BLOCK, RET_BLOCK)
    lane_head = (jnp.arange(256) % 128) // 32
    qfac = jnp.exp((idx[:, None] + 1.0) * lg[lane_head][None, :])
    kfac = jnp.exp((RET_BLOCK - 1.0 - idx[:, None]) * lg[lane_head][None, :])
    dec = jnp.exp(RET_BLOCK * lg[lane_head])[:, None]
    return ret_cos, ret_sin, mla_cos, mla_sa, mla_sb, dmask, qfac, kfac, dec


def kernel(x, meta_tokens, attn_norm, w_in, gla_w_gate, gla_b_gate, gla_norm, ret_norm, mla_q_norm, mla_w_uq, mla_kv_norm, mla_w_ukv, diff_lambda, diff_norm, w_out, ffn_norm, ffn_w_gate, ffn_w_up, ffn_w_down, moe_router, moe_w_gate, moe_w_up, moe_w_down, final_norm):
    b, seq, d = x.shape
    tp = FRONT_PAD + N_META + seq
    n = b * tp
    meta = jnp.broadcast_to(meta_tokens[None].astype(x.dtype), (b, N_META, d))
    h = jnp.concatenate([jnp.zeros((b, FRONT_PAD, d), x.dtype), meta, x], axis=1).reshape(n, d)
    ret_cos, ret_sin, mla_cos, mla_sa, mla_sb, dmask, qfac, kfac, dec = _tables(tp)

    for li in range(DEPTH):
        pa, pb, pc, pd = _inproj(h, attn_norm[li][None, :], _layout_w_in(w_in[li]))
        pa, pb, pc, pd = (p.reshape(b, tp, -1) for p in (pa, pb, pc, pd))

        wgate = jnp.pad(gla_w_gate[li], ((0, LANES - GLA_GATE_RANK), (0, 0)))
        o_a = _gla(pa, pc, wgate, gla_b_gate[li][None, :], jnp.tile(gla_norm[li], N_HEADS)[None, :])
        o_b = _ret(pb, ret_cos, ret_sin, dmask, qfac, kfac, dec, jnp.tile(ret_norm[li], N_HEADS)[None, :])

        qn = _pad_cols(mla_q_norm[li][None, :], 256)
        kvn = jnp.pad(mla_kv_norm[li][None, :], ((0, 0), (MLA_Q_RANK, 0)))
        wuq = jnp.pad(mla_w_uq[li].reshape(MLA_Q_RANK, N_HEADS, MLA_NOPE + MLA_ROPE),
                      ((0, 256 - MLA_Q_RANK), (0, 0), (0, LANES - MLA_NOPE - MLA_ROPE)))
        wuq = wuq.reshape(256, N_HEADS * LANES).astype(BF16)
        wukv = mla_w_ukv[li].reshape(MLA_KV_RANK, N_HEADS, 2, 64)
        wukv = jnp.pad(wukv, ((MLA_Q_RANK, 0), (0, 0), (0, 0), (0, 64)))
        wukv = wukv.transpose(0, 2, 1, 3).reshape(256, 2 * N_HEADS * LANES).astype(BF16)
        o_c = _mla(pc, qn, kvn, wuq, wukv, mla_cos, mla_sa, mla_sb)

        lam_init = 0.8 - 0.6 * math.exp(-0.3 * li)
        o_d = _diff(pd, diff_lambda[li], _pad_cols(diff_norm[li][None, :], LANES), lam_init)

        o_a, o_b, o_c, o_d = (o.reshape(n, 256) for o in (o_a, o_b, o_c, o_d))
        wo = w_out[li].astype(BF16)
        fn = ffn_norm[li][None, :]
        j = li // 2
        if li % 2 == 0:
            h, hn = _outproj(h, o_a, o_b, o_c, o_d, wo, fn)
            h = _swiglu(h, hn, ffn_w_gate[j].astype(BF16), ffn_w_up[j].astype(BF16),
                        ffn_w_down[j].astype(BF16))
        else:
            h, hn, info, cnt = _outproj(h, o_a, o_b, o_c, o_d, wo, fn,
                                        router=_pad_cols(moe_router[j], LANES), seq_rows=tp)
            return _routed_moe_norm(h, hn, info, cnt, moe_w_gate[j].astype(BF16),
                                    moe_w_up[j].astype(BF16), moe_w_down[j].astype(BF16),
                                    final_norm[None, :], b, tp)
```

```python
import functools
import math

import jax
import jax.numpy as jnp
from jax import lax
from jax.experimental import pallas as pl
from jax.experimental.pallas import tpu as pltpu

F32 = jnp.float32
BF16 = jnp.bfloat16

D_MODEL = 1024
DEPTH = 2
N_META = 16
CHUNK = 64
Q_BLOCK = 128
FRONT_PAD = Q_BLOCK - N_META
EPS = 1e-6
NEG = -1e30
ROPE_THETA = 10000.0
N_HEADS = 4
GLA_DK = 32
GLA_DV = 64
GLA_GATE_RANK = 16
GLA_TAU = 16.0
RET_DK = 64
MLA_Q_RANK = 192
MLA_KV_RANK = 64
MLA_NOPE = 64
MLA_ROPE = 32
DIFF_DK = 32
D_FF = 2816
N_EXPERTS = 8
D_FF_EXPERT = 3584

LANES = 128
SUBLANES = 8
ROW_TILE = 512
FF_CHUNK = 256
VMEM_LIMIT = 56 * 1024 * 1024

PA_W = 768
PB_W = 1024
PC_W = 384
PD_W = 768


def _cparams(n_axes=1):
    return pltpu.CompilerParams(dimension_semantics=("arbitrary",) * n_axes,
                                vmem_limit_bytes=VMEM_LIMIT)


def _resident(shape):
    nd = len(shape)
    return pl.BlockSpec(shape, lambda *_: (0,) * nd, pipeline_mode=pl.Buffered(1))


def _sigmoid(x):
    return 1.0 / (1.0 + jnp.exp(-x))


def _split_bf16(x):
    hi = x.astype(BF16)
    lo = (x - hi.astype(F32)).astype(BF16)
    return hi, lo


def _dot(a, b):
    return jnp.dot(a, b, preferred_element_type=F32)


def _dot_nt(a, b):
    return lax.dot_general(a, b, (((1,), (1,)), ((), ())), preferred_element_type=F32)


def _inproj_kernel(h_ref, g_ref, w_ref, pa_ref, pb_ref, pc_ref, pd_ref):
    x = h_ref[...]
    ms = jnp.mean(x * x, axis=-1, keepdims=True)
    y = (x * lax.rsqrt(ms + EPS) * g_ref[...]).astype(BF16)
    off = 0
    for o_ref, width in ((pa_ref, PA_W), (pb_ref, PB_W), (pc_ref, PC_W), (pd_ref, PD_W)):
        o_ref[...] = _dot(y, w_ref[:, off:off + width]).astype(BF16)
        off += width


def _inproj(h, g, w):
    n = h.shape[0]
    wtot = PA_W + PB_W + PC_W + PD_W
    row = lambda width: pl.BlockSpec((ROW_TILE, width), lambda i: (i, 0))
    return pl.pallas_call(
        _inproj_kernel,
        grid=(n // ROW_TILE,),
        in_specs=[row(D_MODEL), _resident((1, D_MODEL)), _resident((D_MODEL, wtot))],
        out_specs=[row(PA_W), row(PB_W), row(PC_W), row(PD_W)],
        out_shape=[jax.ShapeDtypeStruct((n, w_), BF16) for w_ in (PA_W, PB_W, PC_W, PD_W)],
        compiler_params=_cparams(),
        name="inproj",
    )(h, g, w)


def _outproj_kernel(router_seq_rows, h_ref, oa_ref, ob_ref, oc_ref, od_ref, wo_ref, fn_ref, *rest):
    with_router = router_seq_rows > 0
    if with_router:
        router_ref, below_ref, hmid_ref, hn_ref, info_ref, cnt_ref, carry_ref = rest
    else:
        hmid_ref, hn_ref = rest
    o = jnp.concatenate([oa_ref[...], ob_ref[...], oc_ref[...], od_ref[...]], axis=1)
    hm = h_ref[...] + _dot(o, wo_ref[...])
    hmid_ref[...] = hm
    ms = jnp.mean(hm * hm, axis=-1, keepdims=True)
    y = hm * lax.rsqrt(ms + EPS) * fn_ref[...]
    hn_ref[...] = y.astype(hn_ref.dtype)
    if with_router:
        y_hi, y_lo = _split_bf16(y)
        r_hi, r_lo = _split_bf16(router_ref[...])
        logits = _dot(y_hi, r_hi) + _dot(y_hi, r_lo) + _dot(y_lo, r_hi)
        lane = lax.broadcasted_iota(jnp.int32, logits.shape, 1).astype(F32)
        ninf = float("-inf")
        logits = jnp.where(lane < N_EXPERTS, logits, ninf)
        m1 = jnp.max(logits, axis=-1, keepdims=True)
        i1 = jnp.min(jnp.where(logits == m1, lane, float(LANES)), axis=-1, keepdims=True)
        rest_l = jnp.where(lane == i1, ninf, logits)
        m2 = jnp.max(rest_l, axis=-1, keepdims=True)
        i2 = jnp.min(jnp.where(rest_l == m2, lane, float(LANES)), axis=-1, keepdims=True)
        e2 = jnp.exp(m2 - m1)
        den = 1.0 + e2
        @pl.when(pl.program_id(0) == 0)
        def _():
            carry_ref[...] = jnp.zeros_like(carry_ref)
        sel = jnp.where(lane == i1, 1.0, 0.0) + jnp.where(lane == i2, 1.0, 0.0)
        rows = sel.shape[0]
        seq_pos = (lax.rem(pl.program_id(0) * rows, router_seq_rows)
                   + lax.broadcasted_iota(jnp.int32, (rows, 1), 0))
        seq_pos = jnp.where(seq_pos >= router_seq_rows, seq_pos - router_seq_rows, seq_pos)
        sel = jnp.where(seq_pos >= Q_BLOCK, sel, 0.0)
        count = _dot(below_ref[...], sel.astype(BF16)) + carry_ref[...]
        r1 = jnp.sum(jnp.where(lane == i1, count, 0.0), axis=-1, keepdims=True)
        r2 = jnp.sum(jnp.where(lane == i2, count, 0.0), axis=-1, keepdims=True)
        total = carry_ref[...] + jnp.sum(sel, axis=0, keepdims=True)
        carry_ref[...] = total
        cnt_ref[...] = jnp.broadcast_to(total, cnt_ref.shape)
        info = jnp.where(lane == 0, i1, 0.0)
        for k, val in enumerate((i2, 1.0 / den, e2 / den, r1, r2), start=1):
            info = jnp.where(lane == k, val, info)
        info_ref[...] = info


def _outproj(h, oa, ob, oc, od, wo, fn, router=None, seq_rows=0):
    n = h.shape[0]
    row = lambda width: pl.BlockSpec((ROW_TILE, width), lambda i: (i, 0))
    in_specs = [row(D_MODEL), row(256), row(256), row(256), row(256),
                _resident((D_MODEL, D_MODEL)), _resident((1, D_MODEL))]
    out_specs = [row(D_MODEL), row(D_MODEL)]
    out_shape = [jax.ShapeDtypeStruct((n, D_MODEL), F32),
                 jax.ShapeDtypeStruct((n, D_MODEL), BF16 if router is None else F32)]
    args = [h, oa, ob, oc, od, wo, fn]
    scratch = []
    if router is not None:
        in_specs += [_resident((D_MODEL, LANES)), _resident((ROW_TILE, ROW_TILE))]
        out_specs += [row(LANES), pl.BlockSpec((8, LANES), lambda i: (0, 0))]
        out_shape += [jax.ShapeDtypeStruct((n, LANES), F32), jax.ShapeDtypeStruct((8, LANES), F32)]
        args += [router, jnp.tri(ROW_TILE, k=-1, dtype=BF16)]
        scratch.append(pltpu.VMEM((1, LANES), F32))
    return pl.pallas_call(
        functools.partial(_outproj_kernel, seq_rows if router is not None else 0),
        grid=(n // ROW_TILE,),
        in_specs=in_specs, out_specs=out_specs, out_shape=out_shape,
        scratch_shapes=scratch,
        compiler_params=_cparams(),
        name="outproj_router" if router is not None else "outproj",
    )(*args)


def _dispatch_kernel(n_fill, zs_ref, ze_ref, pos_ref, x_ref, xs_ref, zero_buf, sem, zsem):
    def row_copy(rb, u, k):
        p = pos_ref[0, 0, rb * (2 * SUBLANES) + 2 * u + k]
        return pltpu.make_async_copy(x_ref.at[rb, pl.ds(u, 1), :], xs_ref.at[pl.ds(p, 1), :], sem)

    def start(rb, carry):
        for u in range(SUBLANES):
            row_copy(rb, u, 0).start(priority=0)
            row_copy(rb, u, 1).start(priority=1)
        return carry

    def wait(rb, carry):
        for u in range(SUBLANES):
            row_copy(rb, u, 0).wait()
            row_copy(rb, u, 1).wait()
        return carry

    lax.fori_loop(0, x_ref.shape[0], start, 0)
    lax.fori_loop(0, x_ref.shape[0], wait, 0)

    @pl.when(pl.program_id(0) == pl.num_programs(0) - 1)
    def _():
        zero_buf[...] = jnp.zeros_like(zero_buf)

        def zero_row(r):
            return pltpu.make_async_copy(zero_buf.at[pl.ds(0, 1), :], xs_ref.at[pl.ds(r, 1), :], zsem)

        def zero_tile(t):
            return pltpu.make_async_copy(
                zero_buf, xs_ref.at[pl.ds(pl.multiple_of(t * SUBLANES, SUBLANES), SUBLANES), :], zsem)

        def fill_row(r, carry):
            zero_row(r).start()
            return carry

        def fill_tile(t, carry):
            zero_tile(t).start()
            return carry

        n_single = 0
        for k in range(zs_ref.shape[0]):
            aligned = jnp.minimum((zs_ref[k] + SUBLANES - 1) & -SUBLANES, ze_ref[k])
            lax.fori_loop(zs_ref[k], aligned, fill_row, 0)
            lax.fori_loop(aligned >> 3, ze_ref[k] >> 3, fill_tile, 0)
            n_single = n_single + (aligned - zs_ref[k])

        def drain_row(r, carry):
            zero_row(0).wait()
            return carry

        def drain_tile(t, carry):
            zero_tile(0).wait()
            return carry

        lax.fori_loop(0, n_single, drain_row, 0)
        lax.fori_loop(0, (n_fill - n_single) >> 3, drain_tile, 0)


def _dispatch(zero_start, zero_end, pos, x, n_sorted, n_fill):
    n, d = x.shape
    return pl.pallas_call(
        functools.partial(_dispatch_kernel, n_fill),
        grid_spec=pltpu.PrefetchScalarGridSpec(
            num_scalar_prefetch=2,
            grid=(n // ROW_TILE,),
            in_specs=[pl.BlockSpec((1, 1, 2 * ROW_TILE), lambda i, zs, ze: (i, 0, 0),
                                   memory_space=pltpu.SMEM),
                      pl.BlockSpec((ROW_TILE // SUBLANES, SUBLANES, d), lambda i, zs, ze: (i, 0, 0))],
            out_specs=pl.BlockSpec(memory_space=pl.ANY),
            scratch_shapes=[pltpu.VMEM((SUBLANES, d), x.dtype), pltpu.SemaphoreType.DMA(()),
                            pltpu.SemaphoreType.DMA(())]),
        out_shape=jax.ShapeDtypeStruct((n_sorted, d), x.dtype),
        compiler_params=_cparams(),
        name="moe_dispatch",
    )(zero_start, zero_end, pos, x.reshape(n // SUBLANES, SUBLANES, d))


def _swiglu_tile(x, wg, wu, wd, acc_ref):
    n_chunks = wd.shape[0] // FF_CHUNK
    for c in range(n_chunks):
        cols = slice(c * FF_CHUNK, (c + 1) * FF_CHUNK)
        g = _dot(x, wg[:, cols])
        u = _dot(x, wu[:, cols])
        part = _dot((g * _sigmoid(g) * u).astype(BF16), wd[cols, :])
        if c == 0:
            acc_ref[...] = part
        elif c < n_chunks - 1:
            acc_ref[...] += part
    return acc_ref[...] + part


def _experts_kernel(te_ref, tv_ref, xs_ref, wg_ref, wu_ref, wd_ref, ys_ref, acc_ref):
    i = pl.program_id(0)

    @pl.when(tv_ref[i] > 0)
    def _():
        ys_ref[...] = _swiglu_tile(xs_ref[...].astype(BF16), wg_ref.at[0], wu_ref.at[0],
                                   wd_ref.at[0], acc_ref)

    @pl.when(tv_ref[i] == 0)
    def _():
        ys_ref[...] = jnp.zeros_like(ys_ref)


def _experts(tile_expert, tile_valid, xs, wg, wu, wd, n_sorted):
    d = xs.shape[1]
    wspec = lambda shape: pl.BlockSpec((1,) + shape[1:], lambda i, te, tv: (te[i], 0, 0),
                                       pipeline_mode=pl.Buffered(1))
    row = pl.BlockSpec((ROW_TILE, d), lambda i, te, tv: (i, 0))
    return pl.pallas_call(
        _experts_kernel,
        grid_spec=pltpu.PrefetchScalarGridSpec(
            num_scalar_prefetch=2,
            grid=(n_sorted // ROW_TILE,),
            in_specs=[row, wspec(wg.shape), wspec(wu.shape), wspec(wd.shape)],
            out_specs=row,
            scratch_shapes=[pltpu.VMEM((ROW_TILE, d), F32)]),
        out_shape=jax.ShapeDtypeStruct((n_sorted, d), F32),
        compiler_params=_cparams(),
        name="moe_experts",
    )(tile_expert, tile_valid, xs, wg, wu, wd)


def _combine_norm_kernel(tp, pos_ref, h_ref, info_ref, ys_ref, g_ref, o_ref,
                         h_buf, info_buf, y_buf, sem, row_sem):
    n_rows = h_buf.shape[0]
    row0 = pl.multiple_of(pl.program_id(0) * tp + Q_BLOCK + pl.program_id(1) * n_rows, Q_BLOCK)
    h_copy = pltpu.make_async_copy(h_ref.at[pl.ds(row0, n_rows), :], h_buf, sem.at[0])
    info_copy = pltpu.make_async_copy(info_ref.at[pl.ds(row0, n_rows), :], info_buf, sem.at[1])
    h_copy.start()
    info_copy.start()

    def row_copy(rb, u, k):
        p = pos_ref[0, 0, rb * (2 * SUBLANES) + 2 * u + k]
        return pltpu.make_async_copy(ys_ref.at[pl.ds(p, 1), :], y_buf.at[k, rb, pl.ds(u, 1), :], row_sem)

    def start(rb, carry):
        for u in range(SUBLANES):
            row_copy(rb, u, 0).start(priority=0)
            row_copy(rb, u, 1).start(priority=1)
        return carry

    def wait(rb, carry):
        for u in range(SUBLANES):
            row_copy(rb, u, 0).wait()
            row_copy(rb, u, 1).wait()
        return carry

    lax.fori_loop(0, n_rows // SUBLANES, start, 0)
    lax.fori_loop(0, n_rows // SUBLANES, wait, 0)
    h_copy.wait()
    info_copy.wait()
    info = info_buf[...]
    y0 = y_buf[0].reshape(n_rows, -1)
    y1 = y_buf[1].reshape(n_rows, -1)
    x = h_buf[...] + info[:, 2:3] * y0 + info[:, 3:4] * y1
    ms = jnp.mean(x * x, axis=-1, keepdims=True)
    o_ref[0] = x * lax.rsqrt(ms + EPS) * g_ref[...]


def _combine_norm(pos, h, info, ys, g, b, tp):
    n, d = h.shape
    seq = tp - Q_BLOCK
    tiles = seq // ROW_TILE
    any_spec = pl.BlockSpec(memory_space=pl.ANY)
    return pl.pallas_call(
        functools.partial(_combine_norm_kernel, tp),
        grid=(b, tiles),
        in_specs=[pl.BlockSpec((1, 1, 2 * ROW_TILE), lambda i, j: (i * tiles + j, 0, 0),
                               memory_space=pltpu.SMEM),
                  any_spec, any_spec, any_spec, _resident((1, d))],
        out_specs=pl.BlockSpec((1, ROW_TILE, d), lambda i, j: (i, j, 0)),
        out_shape=jax.ShapeDtypeStruct((b, seq, d), F32),
        scratch_shapes=[pltpu.VMEM((ROW_TILE, d), F32), pltpu.VMEM((ROW_TILE, LANES), F32),
                        pltpu.VMEM((2, ROW_TILE // SUBLANES, SUBLANES, d), F32),
                        pltpu.SemaphoreType.DMA((2,)), pltpu.SemaphoreType.DMA(())],
        compiler_params=_cparams(2),
        name="moe_combine_norm",
    )(pos, h, info, ys, g)


def _routed_moe_norm(h, hn, info, cnt, wg, wu, wd, g, b, tp):
    n = h.shape[0]
    n_routed = b * (tp - Q_BLOCK)
    n_fill = N_EXPERTS * ROW_TILE
    n_main = 2 * n_routed + n_fill
    n_tiles = n_main // ROW_TILE
    counts = cnt[0, :N_EXPERTS].astype(jnp.int32)
    tiles_e = (counts + ROW_TILE - 1) // ROW_TILE
    tile_end = jnp.cumsum(tiles_e)
    row_base = (tile_end - tiles_e) * ROW_TILE
    e_idx = info[:, 0:2].astype(jnp.int32)
    pos = row_base[e_idx] + info[:, 4:6].astype(jnp.int32)
    rows = jnp.arange(n, dtype=jnp.int32)
    seq_pos = rows % tp
    parked = n_main + 2 * ((rows // tp) * Q_BLOCK + seq_pos)[:, None] + jnp.arange(2, dtype=jnp.int32)[None, :]
    pos = jnp.where((seq_pos >= Q_BLOCK)[:, None], pos, parked)
    tile_ids = jnp.arange(n_tiles, dtype=jnp.int32)
    tile_expert = jnp.minimum(jnp.sum((tile_ids[:, None] >= tile_end[None, :]).astype(jnp.int32), axis=1),
                              N_EXPERTS - 1)
    tile_valid = (tile_ids < tile_end[-1]).astype(jnp.int32)
    zero_start = jnp.concatenate([row_base + counts, tile_end[-1:] * ROW_TILE])
    zero_end = jnp.concatenate([tile_end * ROW_TILE, jnp.full((1,), n_main, jnp.int32)])
    xs = _dispatch(zero_start, zero_end, pos.reshape(n // ROW_TILE, 1, 2 * ROW_TILE), hn,
                   n_main + 2 * (n - n_routed), n_fill)
    ys = _experts(tile_expert, tile_valid, xs, wg, wu, wd, n_main)
    pos_seq = pos.reshape(b, tp, 2)[:, Q_BLOCK:, :].reshape(-1, 1, 2 * ROW_TILE)
    return _combine_norm(pos_seq, h, info, ys, g, b, tp)


def _swiglu_kernel(h_ref, hn_ref, wg_ref, wu_ref, wd_ref, o_ref, acc_ref):
    o_ref[...] = h_ref[...] + _swiglu_tile(hn_ref[...], wg_ref, wu_ref, wd_ref, acc_ref)


def _swiglu(h, hn, wg, wu, wd):
    n = h.shape[0]
    row = pl.BlockSpec((ROW_TILE, D_MODEL), lambda i: (i, 0))
    return pl.pallas_call(
        _swiglu_kernel,
        grid=(n // ROW_TILE,),
        in_specs=[row, row, _resident(wg.shape), _resident(wu.shape), _resident(wd.shape)],
        out_specs=row,
        out_shape=jax.ShapeDtypeStruct((n, D_MODEL), F32),
        scratch_shapes=[pltpu.VMEM((ROW_TILE, D_MODEL), F32)],
        input_output_aliases={0: 0},
        compiler_params=_cparams(),
        name="swiglu",
    )(h, hn, wg, wu, wd)


def _group_ones(n, group_shift):
    r = lax.broadcasted_iota(jnp.int32, (n, n), 0) >> group_shift
    c = lax.broadcasted_iota(jnp.int32, (n, n), 1) >> group_shift
    return jnp.where(r == c, 1.0, 0.0).astype(BF16)


GLA_ROWS = 256


def _gla_kernel(tp, pa_ref, lr_ref, wg_ref, bg_ref, gn_ref, o_ref, s_ref):
    qhead = lax.broadcasted_iota(jnp.int32, (1, N_HEADS * GLA_DK), 1) >> 5
    ehead = lax.broadcasted_iota(jnp.int32, (1, N_HEADS * GLA_DV), 1) >> 6
    shead = lax.broadcasted_iota(jnp.int32, (N_HEADS * GLA_DK, 1), 0) >> 5
    bd = shead == ehead
    gsum = _group_ones(N_HEADS * GLA_DV, 6)
    wg_hi, wg_lo = _split_bf16(wg_ref[...])
    bg = bg_ref[...]
    gn = gn_ref[...]
    scale = GLA_DK ** -0.5
    s_ref[...] = jnp.zeros_like(s_ref)

    def block(r0, n_rows):
        n_c = n_rows // CHUNK
        rows = pl.ds(r0, n_rows)
        ri = lax.broadcasted_iota(jnp.int32, (n_rows, n_rows), 0)
        ci = lax.broadcasted_iota(jnp.int32, (n_rows, n_rows), 1)
        same = (ri >> 6) == (ci >> 6)
        tri_bf = jnp.where(jnp.logical_and(same, ri >= ci), 1.0, 0.0).astype(BF16)
        ones_bf = jnp.where(same, 1.0, 0.0).astype(BF16)
        r4 = lax.broadcasted_iota(jnp.int32, (N_HEADS * n_rows, n_rows), 0) & (n_rows - 1)
        c4 = lax.broadcasted_iota(jnp.int32, (N_HEADS * n_rows, n_rows), 1)
        tri4 = jnp.logical_and((r4 >> 6) == (c4 >> 6), r4 >= c4)
        col_chunk = lax.broadcasted_iota(jnp.int32, (1, n_rows), 1) >> 6

        q = pa_ref[0, rows, 0:128].astype(F32) * scale
        k = pa_ref[0, rows, 128:256].astype(F32)
        v = pa_ref[0, rows, 256:512]
        og = pa_ref[0, rows, 512:768].astype(F32)
        lr = lr_ref[0, rows, :]
        valid = (r0 + lax.broadcasted_iota(jnp.int32, (n_rows, 1), 0)) >= FRONT_PAD

        pre = _dot(lr, wg_hi) + _dot(lr, wg_lo) + bg
        logsig = jnp.minimum(pre, 0.0) - jnp.log1p(jnp.exp(-jnp.abs(pre)))
        g = jnp.where(valid, logsig * (1.0 / GLA_TAU), 0.0)
        g_hi, g_lo = _split_bf16(g)
        cum = _dot(tri_bf, g_hi) + _dot(tri_bf, g_lo)
        cum_end = _dot(ones_bf, g_hi) + _dot(ones_bf, g_lo)
        qt_bf = (q * jnp.exp(cum)).astype(BF16)
        kt = (k * jnp.exp(-cum)).astype(BF16)
        kd_t = (k * jnp.exp(cum_end - cum)).T.astype(BF16)
        dec_t = jnp.exp(cum_end.T)

        qs = jnp.concatenate([jnp.where(qhead == h, qt_bf, jnp.zeros_like(qt_bf))
                              for h in range(N_HEADS)], axis=0)
        a = jnp.where(tri4, _dot_nt(qs, kt), 0.0).astype(BF16)
        r = _dot(a, v)
        o = jnp.where(ehead == 0, r[0:n_rows, :], 0.0)
        for h in range(1, N_HEADS):
            o = o + jnp.where(ehead == h, r[h * n_rows:(h + 1) * n_rows, :], 0.0)

        s = s_ref[...]
        inter = []
        for c in range(n_c):
            inter.append(_dot(qt_bf[c * CHUNK:(c + 1) * CHUNK, :], s.astype(BF16)))
            upd = _dot(jnp.where(col_chunk == c, kd_t, jnp.zeros_like(kd_t)), v)
            s = dec_t[:, c * CHUNK:c * CHUNK + 1] * s + jnp.where(bd, upd, 0.0)
        s_ref[...] = s
        o = o + jnp.concatenate(inter, axis=0)

        ms = _dot((o * o).astype(BF16), gsum) * (1.0 / GLA_DV)
        y = o * lax.rsqrt(ms + EPS) * gn * (og * _sigmoid(og))
        o_ref[0, rows, :] = jnp.where(valid, y, 0.0).astype(BF16)

    block(0, Q_BLOCK)

    def body(i, carry):
        block(pl.multiple_of(Q_BLOCK + i * GLA_ROWS, Q_BLOCK), GLA_ROWS)
        return carry

    lax.fori_loop(0, (tp - Q_BLOCK) // GLA_ROWS, body, 0)


def _gla(pa, pc, wg, bg, gn):
    b, tp, _ = pa.shape
    return pl.pallas_call(
        functools.partial(_gla_kernel, tp),
        grid=(b,),
        in_specs=[pl.BlockSpec((1, tp, PA_W), lambda i: (i, 0, 0)),
                  pl.BlockSpec((1, tp, LANES), lambda i: (i, 0, 2)),
                  _resident((LANES, LANES)), _resident((1, LANES)), _resident((1, 256))],
        out_specs=pl.BlockSpec((1, tp, 256), lambda i: (i, 0, 0)),
        out_shape=jax.ShapeDtypeStruct((b, tp, 256), BF16),
        scratch_shapes=[pltpu.VMEM((N_HEADS * GLA_DK, N_HEADS * GLA_DV), F32)],
        compiler_params=_cparams(),
        name="gla",
    )(pa, pc, wg, bg, gn)


RET_BLOCK = 128


def _ret_kernel(tp, pb_ref, cos_ref, sin_ref, dmask_ref, qfac_ref, kfac_ref, dec_ref, gn_ref,
                o_ref, s_ref):
    blk = RET_BLOCK
    n_blocks = tp // blk
    qhead = (lax.broadcasted_iota(jnp.int32, (1, 256), 1) & 127) >> 5
    ehead = lax.broadcasted_iota(jnp.int32, (1, 256), 1) >> 6
    shead = (lax.broadcasted_iota(jnp.int32, (256, 1), 0) & 127) >> 5
    bd = shead == ehead
    gsum = _group_ones(256, 6)
    gn = gn_ref[...]
    s_ref[...] = jnp.zeros_like(s_ref)

    def rope(x, cos, sin):
        x1, x2 = x[:, :128], x[:, 128:]
        return jnp.concatenate([x1 * cos - x2 * sin, x1 * sin + x2 * cos], axis=1)

    def block(j, carry):
        r0 = pl.multiple_of(j * blk, blk)
        rows = pl.ds(r0, blk)
        cos = cos_ref[rows, :]
        sin = sin_ref[rows, :]
        q = rope(pb_ref[0, rows, 0:256].astype(F32), cos, sin)
        k = rope(pb_ref[0, rows, 256:512].astype(F32), cos, sin) * (RET_DK ** -0.5)
        v = pb_ref[0, rows, 512:768]
        og = pb_ref[0, rows, 768:1024].astype(F32)
        valid = (r0 + lax.broadcasted_iota(jnp.int32, (blk, 1), 0)) >= FRONT_PAD

        q_bf = q.astype(BF16)
        qs = jnp.concatenate([jnp.where(qhead == h, q_bf, jnp.zeros_like(q_bf))
                              for h in range(N_HEADS)], axis=0)
        a = (_dot_nt(qs, k.astype(BF16)) * dmask_ref[...]).astype(BF16)
        r = _dot(a, v)
        o = _dot((q * qfac_ref[...]).astype(BF16), s_ref[...].astype(BF16))
        for h in range(N_HEADS):
            o = o + jnp.where(ehead == h, r[h * blk:(h + 1) * blk, :], 0.0)

        kd_t = (k * kfac_ref[...]).T.astype(BF16)
        upd = _dot(kd_t, v)
        s_ref[...] = dec_ref[...] * s_ref[...] + jnp.where(bd, upd, 0.0)

        mu = _dot(o.astype(BF16), gsum) * (1.0 / 64)
        xc = o - mu
        var = _dot((xc * xc).astype(BF16), gsum) * (1.0 / 64)
        y = xc * lax.rsqrt(var + EPS) * gn * (og * _sigmoid(og))
        o_ref[0, rows, :] = jnp.where(valid, y, 0.0).astype(BF16)
        return carry

    lax.fori_loop(0, n_blocks, block, 0, unroll=4)


def _ret(pb, cos, sin, dmask, qfac, kfac, dec, gn):
    b, tp, _ = pb.shape
    return pl.pallas_call(
        functools.partial(_ret_kernel, tp),
        grid=(b,),
        in_specs=[pl.BlockSpec((1, tp, PB_W), lambda i: (i, 0, 0)),
                  _resident(cos.shape), _resident(sin.shape), _resident(dmask.shape),
                  _resident(qfac.shape), _resident(kfac.shape), _resident(dec.shape),
                  _resident((1, 256))],
        out_specs=pl.BlockSpec((1, tp, 256), lambda i: (i, 0, 0)),
        out_shape=jax.ShapeDtypeStruct((b, tp, 256), BF16),
        scratch_shapes=[pltpu.VMEM((256, 256), F32)],
        compiler_params=_cparams(),
        name="retention",
    )(pb, cos, sin, dmask, qfac, kfac, dec, gn)


ATT_ROWS = 256
LOG2E = 1.4426950408889634
V_ONE = 64


def _aligned(x, m):
    return x if isinstance(x, int) else pl.multiple_of(x, m)


def _for_chunks(n, body):
    if isinstance(n, int):
        for j in range(n):
            body(j)
        return

    def quad(t, carry):
        for u in range(4):
            body(4 * t + u)
        return carry

    lax.fori_loop(0, n >> 2, quad, 0)
    done = (n >> 2) << 2

    @pl.when((n & 2) != 0)
    def _():
        body(done)
        body(done + 1)

    @pl.when((n & 1) != 0)
    def _():
        body(done + (n & 2))


def _mask_groups(mask, s, n_g):
    rows = s.shape[0] // n_g
    return jnp.concatenate([jnp.where(mask, s[g * rows:(g + 1) * rows, :], NEG)
                            for g in range(n_g)], axis=0)


def _attn_block0(qk_fn, v_fn, n_g):
    rows = pl.ds(0, Q_BLOCK)
    qrow = lax.broadcasted_iota(jnp.int32, (Q_BLOCK, 1), 0)
    kcol = lax.broadcasted_iota(jnp.int32, (1, Q_BLOCK), 1)
    mask = jnp.logical_and(kcol <= qrow, kcol >= FRONT_PAD)
    s = _mask_groups(mask, qk_fn(0, Q_BLOCK, rows), n_g)
    p_bf = jnp.exp2(s - jnp.max(s, axis=-1, keepdims=True)).astype(BF16)
    pv = jnp.concatenate([_dot(p_bf[g * Q_BLOCK:(g + 1) * Q_BLOCK, :], v_fn(g, rows))
                          for g in range(n_g)], axis=0)
    return pv / pv[:, V_ONE:V_ONE + 1]


def _attn_row0(i):
    return _aligned(Q_BLOCK + (i - 1) * ATT_ROWS, Q_BLOCK)


def _attn_result(acc_s):
    acc = acc_s[...]
    return acc / acc[:, V_ONE:V_ONE + 1]


def _attn_block(i, qk_fn, v_fn, n_g, s_meta, s_s, mx_s, acc_s, mid_fn=None):
    rr = ATT_ROWS
    q0 = _attn_row0(i)
    meta_rows = pl.ds(0, Q_BLOCK)
    kcol = lax.broadcasted_iota(jnp.int32, (1, Q_BLOCK), 1)

    def key_rows(j):
        return pl.ds(_aligned(Q_BLOCK + j * rr, Q_BLOCK), rr)

    s = jnp.where(kcol >= FRONT_PAD, qk_fn(q0, rr, meta_rows), NEG)
    s_meta[...] = s
    mx_s[...] = s

    def pass1(j):
        sj = qk_fn(q0, rr, key_rows(j))
        s_s[j] = sj
        mx_s[...] = jnp.maximum(mx_s[...], jnp.maximum(sj[:, :LANES], sj[:, LANES:]))

    _for_chunks(i - 1, pass1)
    if mid_fn is not None:
        mid_fn()
    causal = (lax.broadcasted_iota(jnp.int32, (rr, rr), 1)
              <= lax.broadcasted_iota(jnp.int32, (rr, rr), 0))
    sd = _mask_groups(causal, qk_fn(q0, rr, pl.ds(q0, rr)), n_g)
    s_s[i - 1] = sd
    m = jnp.max(jnp.maximum(mx_s[...], jnp.maximum(sd[:, :LANES], sd[:, LANES:])),
                axis=-1, keepdims=True)
    mx_s[...] = jnp.broadcast_to(m, mx_s.shape)

    p_bf = jnp.exp2(s_meta[...] - mx_s[...]).astype(BF16)
    for g in range(n_g):
        acc_s[g * rr:(g + 1) * rr, :] = _dot(p_bf[g * rr:(g + 1) * rr, :], v_fn(g, meta_rows))

    def pass2(j):
        sj = s_s[j]
        mrep = mx_s[...]
        p0 = jnp.exp2(sj[:, :LANES] - mrep)
        p1 = jnp.exp2(sj[:, LANES:] - mrep)
        pj = jnp.concatenate([p0.astype(BF16), p1.astype(BF16)], axis=1)
        for g in range(n_g):
            acc_s[g * rr:(g + 1) * rr, :] += _dot(pj[g * rr:(g + 1) * rr, :], v_fn(g, key_rows(j)))

    _for_chunks(i, pass2)


def _attn_all_blocks(n_q, block_fn, finish_fn):
    block_fn(1, None)

    def body(i, carry):
        block_fn(i, lambda: finish_fn(i - 1))
        return carry

    lax.fori_loop(2, n_q + 1, body, 0)
    finish_fn(n_q)


def _heads_to_lanes(per_head):
    low = lax.broadcasted_iota(jnp.int32, (1, LANES), 1) < 64
    lo = jnp.where(low, per_head[0], pltpu.roll(per_head[1], 64, 1))
    hi = jnp.where(low, per_head[2], pltpu.roll(per_head[3], 64, 1))
    return jnp.concatenate([lo, hi], axis=1)


def _with_ones_lane(v):
    lane = lax.broadcasted_iota(jnp.int32, (1, LANES), 1)
    return jnp.where(lane == V_ONE, jnp.ones_like(v), v)


def _mla_kernel(tp, pc_ref, qn_ref, kvn_ref, wuq_ref, wukv_ref, cos_ref, sa_ref, sb_ref,
                o_ref, q_s, k_s, v_s, s_meta, s_s, mx_s, acc_s):
    n_blocks = tp // Q_BLOCK
    scale = (MLA_NOPE + MLA_ROPE) ** -0.5 * LOG2E
    is_q = lax.broadcasted_iota(jnp.int32, (1, 256), 1) < MLA_Q_RANK

    def prep(r0, n_rows):
        rows = pl.ds(r0, n_rows)
        x = pc_ref[0, rows, 0:256].astype(F32)
        x2 = x * x
        ms_q = jnp.sum(jnp.where(is_q, x2, 0.0), axis=-1, keepdims=True) * (1.0 / MLA_Q_RANK)
        ms_kv = jnp.sum(jnp.where(is_q, 0.0, x2), axis=-1, keepdims=True) * (1.0 / MLA_KV_RANK)
        yq = (x * lax.rsqrt(ms_q + EPS) * qn_ref[...]).astype(BF16)
        ykv = (x * lax.rsqrt(ms_kv + EPS) * kvn_ref[...]).astype(BF16)
        cq = _dot(yq, wuq_ref[...])
        kv = _dot(ykv, wukv_ref[...])
        cos = cos_ref[rows, :]
        sa = sa_ref[rows, :]
        sb = sb_ref[rows, :]

        def rope(t):
            return t * cos + pltpu.roll(t, 16, 1) * sa + pltpu.roll(t, LANES - 16, 1) * sb

        kpe_in = pc_ref[0, rows, 256:384].astype(F32)
        kpe = rope(jnp.where(lax.broadcasted_iota(jnp.int32, (1, LANES), 1) >= 64, kpe_in, 0.0))
        for h in range(N_HEADS):
            q_s[h, rows, :] = (rope(cq[:, h * LANES:(h + 1) * LANES]) * scale).astype(BF16)
            k_s[h, rows, :] = (kv[:, h * LANES:(h + 1) * LANES] + kpe).astype(BF16)
            v_s[h, rows, :] = _with_ones_lane(
                kv[:, (N_HEADS + h) * LANES:(N_HEADS + h + 1) * LANES]).astype(BF16)

    prep(0, Q_BLOCK)

    def prep_body(i, carry):
        prep(pl.multiple_of(Q_BLOCK + i * ATT_ROWS, Q_BLOCK), ATT_ROWS)
        return carry

    lax.fori_loop(0, (tp - Q_BLOCK) // ATT_ROWS, prep_body, 0)

    def qk_fn(q0, n_rows, krows):
        return jnp.concatenate([_dot_nt(q_s[h, pl.ds(q0, n_rows), :], k_s[h, krows, :])
                                for h in range(N_HEADS)], axis=0)

    def v_fn(h, krows):
        return v_s[h, krows, :]

    def emit(q0, n_rows, o, first):
        y = _heads_to_lanes([o[h * n_rows:(h + 1) * n_rows, :] for h in range(N_HEADS)])
        if first:
            qrow = lax.broadcasted_iota(jnp.int32, (n_rows, 1), 0)
            y = jnp.where(qrow >= FRONT_PAD, y, 0.0)
        o_ref[0, pl.ds(q0, n_rows), :] = y.astype(BF16)

    emit(0, Q_BLOCK, _attn_block0(qk_fn, v_fn, N_HEADS), True)

    def block(i, mid_fn):
        _attn_block(i, qk_fn, v_fn, N_HEADS, s_meta, s_s, mx_s, acc_s, mid_fn)

    def finish(i):
        emit(_attn_row0(i), ATT_ROWS, _attn_result(acc_s), False)

    _attn_all_blocks((tp - Q_BLOCK) // ATT_ROWS, block, finish)


def _attn_scratch(n_g, tp):
    g_rows = n_g * ATT_ROWS
    n_slots = (tp - Q_BLOCK) // ATT_ROWS
    return [pltpu.VMEM((g_rows, LANES), F32),
            pltpu.VMEM((n_slots, g_rows, ATT_ROWS), F32),
            pltpu.VMEM((g_rows, LANES), F32),
            pltpu.VMEM((g_rows, LANES), F32)]


def _mla(pc, qn, kvn, wuq, wukv, cos, sa, sb):
    b, tp, _ = pc.shape
    return pl.pallas_call(
        functools.partial(_mla_kernel, tp),
        grid=(b,),
        in_specs=[pl.BlockSpec((1, tp, PC_W), lambda i: (i, 0, 0)),
                  _resident((1, 256)), _resident((1, 256)),
                  _resident(wuq.shape), _resident(wukv.shape),
                  _resident(cos.shape), _resident(sa.shape), _resident(sb.shape)],
        out_specs=pl.BlockSpec((1, tp, 256), lambda i: (i, 0, 0)),
        out_shape=jax.ShapeDtypeStruct((b, tp, 256), BF16),
        scratch_shapes=[pltpu.VMEM((N_HEADS, tp, LANES), BF16),
                        pltpu.VMEM((N_HEADS, tp, LANES), BF16),
                        pltpu.VMEM((N_HEADS, tp, LANES), BF16)] + _attn_scratch(N_HEADS, tp),
        compiler_params=_cparams(),
        name="mla",
    )(pc, qn, kvn, wuq, wukv, cos, sa, sb)


def _diff_kernel(tp, lam_init, pd_ref, lam_ref, dn_ref, o_ref, qs_s, v_s, s_meta, s_s, mx_s, acc_s):
    n_maps = 2 * N_HEADS

    def fill_values(i, carry):
        rows = pl.ds(pl.multiple_of(i * Q_BLOCK, Q_BLOCK), Q_BLOCK)
        low = lax.broadcasted_iota(jnp.int32, (1, LANES), 1) < V_ONE
        for pair in range(N_HEADS // 2):
            two = pd_ref[0, rows, 512 + pair * LANES:512 + (pair + 1) * LANES].astype(F32)
            for h, vals in ((2 * pair, two), (2 * pair + 1, pltpu.roll(two, V_ONE, 1))):
                v_s[h, rows, :] = _with_ones_lane(jnp.where(low, vals, 0.0)).astype(BF16)
        return carry

    lax.fori_loop(0, tp // Q_BLOCK, fill_values, 0)
    scale = DIFF_DK ** -0.5 * LOG2E
    group = lax.broadcasted_iota(jnp.int32, (1, 256), 1) >> 5
    lv = lam_ref[...]
    lam = (jnp.exp(jnp.sum(lv[0:1, :] * lv[1:2, :], axis=-1, keepdims=True))
           - jnp.exp(jnp.sum(lv[2:3, :] * lv[3:4, :], axis=-1, keepdims=True)) + lam_init)
    dn = dn_ref[...]

    def stack_queries(q0, n_rows):
        q = (pd_ref[0, pl.ds(q0, n_rows), 0:256].astype(F32) * scale).astype(BF16)
        for g in range(n_maps):
            qs_s[g * n_rows:(g + 1) * n_rows, :] = jnp.where(group == g, q, jnp.zeros_like(q))

    def qk_fn(q0, n_rows, krows):
        return _dot_nt(qs_s[0:n_maps * n_rows, :], pd_ref[0, krows, 256:512])

    def v_fn(g, krows):
        return v_s[g // 2, krows, :]

    def emit(q0, n_rows, o, first):
        od = jnp.concatenate([o[(2 * h) * n_rows:(2 * h + 1) * n_rows, :]
                              - lam * o[(2 * h + 1) * n_rows:(2 * h + 2) * n_rows, :]
                              for h in range(N_HEADS)], axis=0)
        od = jnp.where(lax.broadcasted_iota(jnp.int32, (1, LANES), 1) < V_ONE, od, 0.0)
        ms = jnp.sum(od * od, axis=-1, keepdims=True) * (1.0 / 64)
        yh = od * lax.rsqrt(ms + EPS) * dn * (1.0 - lam_init)
        y = _heads_to_lanes([yh[h * n_rows:(h + 1) * n_rows, :] for h in range(N_HEADS)])
        if first:
            qrow = lax.broadcasted_iota(jnp.int32, (n_rows, 1), 0)
            y = jnp.where(qrow >= FRONT_PAD, y, 0.0)
        o_ref[0, pl.ds(q0, n_rows), :] = y.astype(BF16)

    stack_queries(0, Q_BLOCK)
    emit(0, Q_BLOCK, _attn_block0(qk_fn, v_fn, n_maps), True)

    def block(i, mid_fn):
        stack_queries(_attn_row0(i), ATT_ROWS)
        _attn_block(i, qk_fn, v_fn, n_maps, s_meta, s_s, mx_s, acc_s, mid_fn)

    def finish(i):
        emit(_attn_row0(i), ATT_ROWS, _attn_result(acc_s), False)

    _attn_all_blocks((tp - Q_BLOCK) // ATT_ROWS, block, finish)


def _diff(pd, lam_rows, dn, lam_init):
    b, tp, _ = pd.shape
    n_maps = 2 * N_HEADS
    return pl.pallas_call(
        functools.partial(_diff_kernel, tp, lam_init),
        grid=(b,),
        in_specs=[pl.BlockSpec((1, tp, PD_W), lambda i: (i, 0, 0)),
                  _resident(lam_rows.shape), _resident((1, LANES))],
        out_specs=pl.BlockSpec((1, tp, 256), lambda i: (i, 0, 0)),
        out_shape=jax.ShapeDtypeStruct((b, tp, 256), BF16),
        scratch_shapes=[pltpu.VMEM((n_maps * ATT_ROWS, 256), BF16),
                        pltpu.VMEM((N_HEADS, tp, LANES), BF16)] + _attn_scratch(n_maps, tp),
        compiler_params=_cparams(),
        name="diffattn",
    )(pd, lam_rows, dn)


def _pad_cols(x, width):
    return jnp.pad(x, ((0, 0), (0, width - x.shape[1])))


def _rot_split(w):
    d = w.shape[0]
    return w.reshape(d, N_HEADS, 2, 32).transpose(0, 2, 1, 3).reshape(d, 256)


def _layout_w_in(w):
    sizes = (128, 128, 256, 16, 256, 256, 256, 256, 256, 192, 64, 32, 256, 256, 256)
    offs = [0]
    for s_ in sizes:
        offs.append(offs[-1] + s_)
    seg = [w[:, offs[i]:offs[i + 1]] for i in range(len(sizes))]
    (a_q, a_k, a_v, a_lr, a_og, r_q, r_k, r_v, r_og, c_cq, c_ckv, c_kpe, d_q, d_k, d_v) = seg
    d = w.shape[0]
    z = lambda n: jnp.zeros((d, n), w.dtype)
    cols = [a_q, a_k, a_v, a_og,
            _rot_split(r_q), _rot_split(r_k), r_v, r_og,
            c_cq, c_ckv, a_lr, z(48), c_kpe, z(32),
            d_q, d_k, d_v]
    return jnp.concatenate(cols, axis=1).astype(BF16)


def _tables(tp):
    pos = jnp.arange(tp, dtype=F32) - FRONT_PAD
    inv = ROPE_THETA ** (-jnp.arange(32, dtype=F32) / 32)
    ang = pos[:, None] * inv[None, :]
    ret_cos = jnp.tile(jnp.cos(ang), (1, N_HEADS))
    ret_sin = jnp.tile(jnp.sin(ang), (1, N_HEADS))
    inv16 = ROPE_THETA ** (-jnp.arange(16, dtype=F32) / 16)
    ang16 = pos[:, None] * inv16[None, :]
    c16, s16 = jnp.cos(ang16), jnp.sin(ang16)
    one = lambda n: jnp.ones((tp, n), F32)
    zero = lambda n: jnp.zeros((tp, n), F32)
    mla_cos = jnp.concatenate([one(64), c16, c16, one(32)], axis=1)
    mla_sa = jnp.concatenate([zero(80), s16, zero(32)], axis=1)
    mla_sb = jnp.concatenate([zero(64), -s16, zero(48)], axis=1)
    lg = jnp.log(1.0 - jnp.exp2(-5.0 - jnp.arange(N_HEADS, dtype=F32)))
    idx = jnp.arange(RET_BLOCK, dtype=F32)
    rel = idx[:, None] - idx[None, :]
    dmask = jnp.where(rel[None] >= 0, jnp.exp(rel[None] * lg[:, None, None]), 0.0)
    dmask = dmask.reshape(N_HEADS * RET_BLOCK, RET_BLOCK)
    lane_head = (jnp.arange(256) % 128) // 32
    qfac = jnp.exp((idx[:, None] + 1.0) * lg[lane_head][None, :])
    kfac = jnp.exp((RET_BLOCK - 1.0 - idx[:, None]) * lg[lane_head][None, :])
    dec = jnp.exp(RET_BLOCK * lg[lane_head])[:, None]
    return ret_cos, ret_sin, mla_cos, mla_sa, mla_sb, dmask, qfac, kfac, dec


def kernel(x, meta_tokens, attn_norm, w_in, gla_w_gate, gla_b_gate, gla_norm, ret_norm, mla_q_norm, mla_w_uq, mla_kv_norm, mla_w_ukv, diff_lambda, diff_norm, w_out, ffn_norm, ffn_w_gate, ffn_w_up, ffn_w_down, moe_router, moe_w_gate, moe_w_up, moe_w_down, final_norm):
    b, seq, d = x.shape
    tp = FRONT_PAD + N_META + seq
    n = b * tp
    meta = jnp.broadcast_to(meta_tokens[None].astype(x.dtype), (b, N_META, d))
    h = jnp.concatenate([jnp.zeros((b, FRONT_PAD, d), x.dtype), meta, x], axis=1).reshape(n, d)
    ret_cos, ret_sin, mla_cos, mla_sa, mla_sb, dmask, qfac, kfac, dec = _tables(tp)

    for li in range(DEPTH):
        pa, pb, pc, pd = _inproj(h, attn_norm[li][None, :], _layout_w_in(w_in[li]))
        pa, pb, pc, pd = (p.reshape(b, tp, -1) for p in (pa, pb, pc, pd))

        wgate = jnp.pad(gla_w_gate[li], ((0, LANES - GLA_GATE_RANK), (0, 0)))
        o_a = _gla(pa, pc, wgate, gla_b_gate[li][None, :], jnp.tile(gla_norm[li], N_HEADS)[None, :])
        o_b = _ret(pb, ret_cos, ret_sin, dmask, qfac, kfac, dec, jnp.tile(ret_norm[li], N_HEADS)[None, :])

        qn = _pad_cols(mla_q_norm[li][None, :], 256)
        kvn = jnp.pad(mla_kv_norm[li][None, :], ((0, 0), (MLA_Q_RANK, 0)))
        wuq = jnp.pad(mla_w_uq[li].reshape(MLA_Q_RANK, N_HEADS, MLA_NOPE + MLA_ROPE),
                      ((0, 256 - MLA_Q_RANK), (0, 0), (0, LANES - MLA_NOPE - MLA_ROPE)))
        wuq = wuq.reshape(256, N_HEADS * LANES).astype(BF16)
        wukv = mla_w_ukv[li].reshape(MLA_KV_RANK, N_HEADS, 2, 64)
        wukv = jnp.pad(wukv, ((MLA_Q_RANK, 0), (0, 0), (0, 0), (0, 64)))
        wukv = wukv.transpose(0, 2, 1, 3).reshape(256, 2 * N_HEADS * LANES).astype(BF16)
        o_c = _mla(pc, qn, kvn, wuq, wukv, mla_cos, mla_sa, mla_sb)

        lam_init = 0.8 - 0.6 * math.exp(-0.3 * li)
        o_d = _diff(pd, diff_lambda[li], _pad_cols(diff_norm[li][None, :], LANES), lam_init)

        o_a, o_b, o_c, o_d = (o.reshape(n, 256) for o in (o_a, o_b, o_c, o_d))
        wo = w_out[li].astype(BF16)
        fn = ffn_norm[li][None, :]
        j = li // 2
        if li % 2 == 0:
            h, hn = _outproj(h, o_a, o_b, o_c, o_d, wo, fn)
            h = _swiglu(h, hn, ffn_w_gate[j].astype(BF16), ffn_w_up[j].astype(BF16),
                        ffn_w_down[j].astype(BF16))
        else:
            h, hn, info, cnt = _outproj(h, o_a, o_b, o_c, o_d, wo, fn,
                                        router=_pad_cols(moe_router[j], LANES), seq_rows=tp)
            return _routed_moe_norm(h, hn, info, cnt, moe_w_gate[j].astype(BF16),
                                    moe_w_up[j].astype(BF16), moe_w_down[j].astype(BF16),
                                    final_norm[None, :], b, tp)
```

```python
import functools
import math

import jax
import jax.numpy as jnp
from jax import lax
from jax.experimental import pallas as pl
from jax.experimental.pallas import tpu as pltpu

F32 = jnp.float32
BF16 = jnp.bfloat16

D_MODEL = 1024
DEPTH = 2
N_META = 16
CHUNK = 64
Q_BLOCK = 128
FRONT_PAD = Q_BLOCK - N_META
EPS = 1e-6
NEG = -1e30
ROPE_THETA = 10000.0
N_HEADS = 4
GLA_DK = 32
GLA_DV = 64
GLA_GATE_RANK = 16
GLA_TAU = 16.0
RET_DK = 64
MLA_Q_RANK = 192
MLA_KV_RANK = 64
MLA_NOPE = 64
MLA_ROPE = 32
DIFF_DK = 32
N_EXPERTS = 8

LANES = 128
SUBLANES = 8
ROW_TILE = 512
FF_CHUNK = 256
VMEM_LIMIT = 56 * 1024 * 1024

PA_W = 768
PB_W = 1024
PC_W = 384
PD_W = 768


def _cparams(n_axes=1):
    return pltpu.CompilerParams(dimension_semantics=("arbitrary",) * n_axes,
                                vmem_limit_bytes=VMEM_LIMIT)


def _resident(shape):
    nd = len(shape)
    return pl.BlockSpec(shape, lambda *_: (0,) * nd, pipeline_mode=pl.Buffered(1))


def _sigmoid(x):
    return 1.0 / (1.0 + jnp.exp(-x))


def _split_bf16(x):
    hi = x.astype(BF16)
    lo = (x - hi.astype(F32)).astype(BF16)
    return hi, lo


def _dot(a, b):
    return jnp.dot(a, b, preferred_element_type=F32)


def _dot_nt(a, b):
    return lax.dot_general(a, b, (((1,), (1,)), ((), ())), preferred_element_type=F32)


def _inproj_kernel(h_ref, g_ref, w_ref, pa_ref, pb_ref, pc_ref, pd_ref):
    x = h_ref[...]
    ms = jnp.mean(x * x, axis=-1, keepdims=True)
    y = (x * lax.rsqrt(ms + EPS) * g_ref[...]).astype(BF16)
    off = 0
    for o_ref, width in ((pa_ref, PA_W), (pb_ref, PB_W), (pc_ref, PC_W), (pd_ref, PD_W)):
        o_ref[...] = _dot(y, w_ref[:, off:off + width]).astype(BF16)
        off += width


def _inproj(h, g, w):
    n = h.shape[0]
    wtot = PA_W + PB_W + PC_W + PD_W
    row = lambda width: pl.BlockSpec((ROW_TILE, width), lambda i: (i, 0))
    return pl.pallas_call(
        _inproj_kernel,
        grid=(n // ROW_TILE,),
        in_specs=[row(D_MODEL), _resident((1, D_MODEL)), _resident((D_MODEL, wtot))],
        out_specs=[row(PA_W), row(PB_W), row(PC_W), row(PD_W)],
        out_shape=[jax.ShapeDtypeStruct((n, w_), BF16) for w_ in (PA_W, PB_W, PC_W, PD_W)],
        compiler_params=_cparams(),
        name="inproj",
    )(h, g, w)


def _outproj_kernel(router_seq_rows, h_ref, oa_ref, ob_ref, oc_ref, od_ref, wo_ref, fn_ref, *rest):
    with_router = router_seq_rows > 0
    if with_router:
        router_ref, below_ref, hmid_ref, hn_ref, info_ref, cnt_ref, carry_ref = rest
    else:
        hmid_ref, hn_ref = rest
    o = jnp.concatenate([oa_ref[...], ob_ref[...], oc_ref[...], od_ref[...]], axis=1)
    hm = h_ref[...] + _dot(o, wo_ref[...])
    hmid_ref[...] = hm
    ms = jnp.mean(hm * hm, axis=-1, keepdims=True)
    y = hm * lax.rsqrt(ms + EPS) * fn_ref[...]
    hn_ref[...] = y.astype(hn_ref.dtype)
    if with_router:
        y_hi, y_lo = _split_bf16(y)
        r_hi, r_lo = _split_bf16(router_ref[...])
        logits = _dot(y_hi, r_hi) + _dot(y_hi, r_lo) + _dot(y_lo, r_hi)
        lane = lax.broadcasted_iota(jnp.int32, logits.shape, 1).astype(F32)
        ninf = float("-inf")
        logits = jnp.where(lane < N_EXPERTS, logits, ninf)
        m1 = jnp.max(logits, axis=-1, keepdims=True)
        i1 = jnp.min(jnp.where(logits == m1, lane, float(LANES)), axis=-1, keepdims=True)
        rest_l = jnp.where(lane == i1, ninf, logits)
        m2 = jnp.max(rest_l, axis=-1, keepdims=True)
        i2 = jnp.min(jnp.where(rest_l == m2, lane, float(LANES)), axis=-1, keepdims=True)
        e2 = jnp.exp(m2 - m1)
        den = 1.0 + e2
        @pl.when(pl.program_id(0) == 0)
        def _():
            carry_ref[...] = jnp.zeros_like(carry_ref)
        sel = jnp.where(lane == i1, 1.0, 0.0) + jnp.where(lane == i2, 1.0, 0.0)
        rows = sel.shape[0]
        seq_pos = (lax.rem(pl.program_id(0) * rows, router_seq_rows)
                   + lax.broadcasted_iota(jnp.int32, (rows, 1), 0))
        seq_pos = jnp.where(seq_pos >= router_seq_rows, seq_pos - router_seq_rows, seq_pos)
        sel = jnp.where(seq_pos >= Q_BLOCK, sel, 0.0)
        count = _dot(below_ref[...], sel.astype(BF16)) + carry_ref[...]
        r1 = jnp.sum(jnp.where(lane == i1, count, 0.0), axis=-1, keepdims=True)
        r2 = jnp.sum(jnp.where(lane == i2, count, 0.0), axis=-1, keepdims=True)
        total = carry_ref[...] + jnp.sum(sel, axis=0, keepdims=True)
        carry_ref[...] = total
        cnt_ref[...] = jnp.broadcast_to(total, cnt_ref.shape)
        info = jnp.where(lane == 0, i1, 0.0)
        for k, val in enumerate((i2, 1.0 / den, e2 / den, r1, r2), start=1):
            info = jnp.where(lane == k, val, info)
        info_ref[...] = info


def _outproj(h, oa, ob, oc, od, wo, fn, router=None, seq_rows=0):
    n = h.shape[0]
    row = lambda width: pl.BlockSpec((ROW_TILE, width), lambda i: (i, 0))
    in_specs = [row(D_MODEL), row(256), row(256), row(256), row(256),
                _resident((D_MODEL, D_MODEL)), _resident((1, D_MODEL))]
    out_specs = [row(D_MODEL), row(D_MODEL)]
    out_shape = [jax.ShapeDtypeStruct((n, D_MODEL), F32),
                 jax.ShapeDtypeStruct((n, D_MODEL), BF16 if router is None else F32)]
    args = [h, oa, ob, oc, od, wo, fn]
    scratch = []
    if router is not None:
        in_specs += [_resident((D_MODEL, LANES)), _resident((ROW_TILE, ROW_TILE))]
        out_specs += [row(LANES), pl.BlockSpec((8, LANES), lambda i: (0, 0))]
        out_shape += [jax.ShapeDtypeStruct((n, LANES), F32), jax.ShapeDtypeStruct((8, LANES), F32)]
        args += [router, jnp.tri(ROW_TILE, k=-1, dtype=BF16)]
        scratch.append(pltpu.VMEM((1, LANES), F32))
    return pl.pallas_call(
        functools.partial(_outproj_kernel, seq_rows if router is not None else 0),
        grid=(n // ROW_TILE,),
        in_specs=in_specs, out_specs=out_specs, out_shape=out_shape,
        scratch_shapes=scratch,
        compiler_params=_cparams(),
        name="outproj_router" if router is not None else "outproj",
    )(*args)


def _swiglu_tile(x, wg, wu, wd, acc_ref, per_chunk=None):
    n_chunks = wd.shape[0] // FF_CHUNK
    for c in range(n_chunks):
        cols = slice(c * FF_CHUNK, (c + 1) * FF_CHUNK)
        g = _dot(x, wg[:, cols])
        u = _dot(x, wu[:, cols])
        part = _dot((g * _sigmoid(g) * u).astype(BF16), wd[cols, :])
        if c == 0:
            acc_ref[...] = part
        elif c < n_chunks - 1:
            acc_ref[...] += part
        if per_chunk is not None:
            per_chunk(c, n_chunks)
    return acc_ref[...] + part


def _experts_kernel(te_ref, tv_ref, tok_ref, tok_next_ref, hn_ref, wg_ref, wu_ref, wd_ref, ys_ref,
                    x_buf, acc_ref, sem):
    i = pl.program_id(0)
    last = pl.num_programs(0) - 1
    slot = i & 1
    n_blocks = x_buf.shape[1]

    def row_copy(t_ref, s, rb, u):
        tok = t_ref[0, 0, rb * SUBLANES + u]
        return pltpu.make_async_copy(hn_ref.at[pl.ds(tok, 1), :], x_buf.at[s, rb, pl.ds(u, 1), :],
                                     sem.at[s])

    def wait_tile(s):
        def wait_block(rb, carry):
            for u in range(SUBLANES):
                row_copy(tok_ref, s, rb, u).wait()
            return carry

        lax.fori_loop(0, n_blocks, wait_block, 0)

    @pl.when(i == 0)
    def _():
        def start_block(rb, carry):
            for u in range(SUBLANES):
                row_copy(tok_ref, 0, rb, u).start(priority=u % 2)
            return carry

        lax.fori_loop(0, n_blocks, start_block, 0)

    @pl.when(jnp.logical_or(i == 0, tv_ref[jnp.maximum(i - 1, 0)] > 0))
    def _():
        wait_tile(slot)

    @pl.when(tv_ref[i] > 0)
    def _():
        def prefetch(c, n_chunks):
            per = -(-ROW_TILE // n_chunks)
            for r in range(c * per, min((c + 1) * per, ROW_TILE)):
                row_copy(tok_next_ref, 1 - slot, r // SUBLANES, r % SUBLANES).start(priority=r % 2)

        x = x_buf[slot].reshape(ROW_TILE, -1).astype(BF16)
        ys_ref[...] = _swiglu_tile(x, wg_ref.at[0], wu_ref.at[0], wd_ref.at[0], acc_ref, prefetch)

    @pl.when(tv_ref[i] == 0)
    def _():
        ys_ref[...] = jnp.zeros_like(ys_ref)

    @pl.when(jnp.logical_and(i == last, tv_ref[i] > 0))
    def _():
        wait_tile(1 - slot)


def _experts(tile_expert, tile_valid, tok, hn, wg, wu, wd):
    n_tiles = tok.shape[0]
    d = hn.shape[1]
    wspec = lambda shape: pl.BlockSpec((1,) + shape[1:], lambda i, te, tv: (te[i], 0, 0),
                                       pipeline_mode=pl.Buffered(1))
    tok_spec = lambda index: pl.BlockSpec((1, 1, ROW_TILE), index, memory_space=pltpu.SMEM)
    return pl.pallas_call(
        _experts_kernel,
        grid_spec=pltpu.PrefetchScalarGridSpec(
            num_scalar_prefetch=2,
            grid=(n_tiles,),
            in_specs=[tok_spec(lambda i, te, tv: (i, 0, 0)),
                      tok_spec(lambda i, te, tv: (jnp.minimum(i + 1, n_tiles - 1), 0, 0)),
                      pl.BlockSpec(memory_space=pl.ANY),
                      wspec(wg.shape), wspec(wu.shape), wspec(wd.shape)],
            out_specs=pl.BlockSpec((ROW_TILE, d), lambda i, te, tv: (i, 0)),
            scratch_shapes=[pltpu.VMEM((2, ROW_TILE // SUBLANES, SUBLANES, d), F32),
                            pltpu.VMEM((ROW_TILE, d), F32),
                            pltpu.SemaphoreType.DMA((2,))]),
        out_shape=jax.ShapeDtypeStruct((n_tiles * ROW_TILE, d), F32),
        compiler_params=_cparams(),
        name="moe_experts",
    )(tile_expert, tile_valid, tok, tok, hn, wg, wu, wd)


def _combine_norm_kernel(tp, pos_ref, h_ref, info_ref, ys_ref, g_ref, o_ref,
                         h_buf, info_buf, y_buf, sem, row_sem):
    n_rows = h_buf.shape[0]
    row0 = pl.multiple_of(pl.program_id(0) * tp + Q_BLOCK + pl.program_id(1) * n_rows, Q_BLOCK)
    h_copy = pltpu.make_async_copy(h_ref.at[pl.ds(row0, n_rows), :], h_buf, sem.at[0])
    info_copy = pltpu.make_async_copy(info_ref.at[pl.ds(row0, n_rows), :], info_buf, sem.at[1])
    h_copy.start()
    info_copy.start()

    def row_copy(rb, u, k):
        p = pos_ref[0, 0, rb * (2 * SUBLANES) + 2 * u + k]
        return pltpu.make_async_copy(ys_ref.at[pl.ds(p, 1), :], y_buf.at[k, rb, pl.ds(u, 1), :], row_sem)

    def start(rb, carry):
        for u in range(SUBLANES):
            row_copy(rb, u, 0).start(priority=0)
            row_copy(rb, u, 1).start(priority=1)
        return carry

    def wait(rb, carry):
        for u in range(SUBLANES):
            row_copy(rb, u, 0).wait()
            row_copy(rb, u, 1).wait()
        return carry

    lax.fori_loop(0, n_rows // SUBLANES, start, 0)
    lax.fori_loop(0, n_rows // SUBLANES, wait, 0)
    h_copy.wait()
    info_copy.wait()
    info = info_buf[...]
    y0 = y_buf[0].reshape(n_rows, -1)
    y1 = y_buf[1].reshape(n_rows, -1)
    x = h_buf[...] + info[:, 2:3] * y0 + info[:, 3:4] * y1
    ms = jnp.mean(x * x, axis=-1, keepdims=True)
    o_ref[0] = x * lax.rsqrt(ms + EPS) * g_ref[...]


def _combine_norm(pos, h, info, ys, g, b, tp):
    n, d = h.shape
    seq = tp - Q_BLOCK
    tiles = seq // ROW_TILE
    any_spec = pl.BlockSpec(memory_space=pl.ANY)
    return pl.pallas_call(
        functools.partial(_combine_norm_kernel, tp),
        grid=(b, tiles),
        in_specs=[pl.BlockSpec((1, 1, 2 * ROW_TILE), lambda i, j: (i * tiles + j, 0, 0),
                               memory_space=pltpu.SMEM),
                  any_spec, any_spec, any_spec, _resident((1, d))],
        out_specs=pl.BlockSpec((1, ROW_TILE, d), lambda i, j: (i, j, 0)),
        out_shape=jax.ShapeDtypeStruct((b, seq, d), F32),
        scratch_shapes=[pltpu.VMEM((ROW_TILE, d), F32), pltpu.VMEM((ROW_TILE, LANES), F32),
                        pltpu.VMEM((2, ROW_TILE // SUBLANES, SUBLANES, d), F32),
                        pltpu.SemaphoreType.DMA((2,)), pltpu.SemaphoreType.DMA(())],
        compiler_params=_cparams(2),
        name="moe_combine_norm",
    )(pos, h, info, ys, g)


def _routed_moe_norm(h, hn, info, cnt, wg, wu, wd, g, b, tp):
    n = h.shape[0]
    n_routed = b * (tp - Q_BLOCK)
    n_sorted = 2 * n_routed + N_EXPERTS * ROW_TILE
    n_tiles = n_sorted // ROW_TILE
    counts = cnt[0, :N_EXPERTS].astype(jnp.int32)
    tiles_e = (counts + ROW_TILE - 1) // ROW_TILE
    tile_end = jnp.cumsum(tiles_e)
    row_base = (tile_end - tiles_e) * ROW_TILE
    e_idx = info[:, 0:2].astype(jnp.int32)
    pos = row_base[e_idx] + info[:, 4:6].astype(jnp.int32)
    rows = jnp.arange(n, dtype=jnp.int32)
    routed = (rows % tp >= Q_BLOCK)[:, None]
    dropped = n_sorted + 2 * rows[:, None] + jnp.arange(2, dtype=jnp.int32)[None, :]
    tok = jnp.zeros((n_sorted,), jnp.int32).at[jnp.where(routed, pos, dropped).reshape(-1)].set(
        jnp.repeat(rows, 2), mode="drop", unique_indices=True)
    tile_ids = jnp.arange(n_tiles, dtype=jnp.int32)
    tile_expert = jnp.minimum(jnp.sum((tile_ids[:, None] >= tile_end[None, :]).astype(jnp.int32), axis=1),
                              N_EXPERTS - 1)
    tile_valid = (tile_ids < tile_end[-1]).astype(jnp.int32)
    ys = _experts(tile_expert, tile_valid, tok.reshape(n_tiles, 1, ROW_TILE), hn, wg, wu, wd)
    pos_seq = pos.reshape(b, tp, 2)[:, Q_BLOCK:, :].reshape(-1, 1, 2 * ROW_TILE)
    return _combine_norm(pos_seq, h, info, ys, g, b, tp)


def _swiglu_kernel(h_ref, hn_ref, wg_ref, wu_ref, wd_ref, o_ref, acc_ref):
    o_ref[...] = h_ref[...] + _swiglu_tile(hn_ref[...], wg_ref, wu_ref, wd_ref, acc_ref)


def _swiglu(h, hn, wg, wu, wd):
    n = h.shape[0]
    row = pl.BlockSpec((ROW_TILE, D_MODEL), lambda i: (i, 0))
    return pl.pallas_call(
        _swiglu_kernel,
        grid=(n // ROW_TILE,),
        in_specs=[row, row, _resident(wg.shape), _resident(wu.shape), _resident(wd.shape)],
        out_specs=row,
        out_shape=jax.ShapeDtypeStruct((n, D_MODEL), F32),
        scratch_shapes=[pltpu.VMEM((ROW_TILE, D_MODEL), F32)],
        input_output_aliases={0: 0},
        compiler_params=_cparams(),
        name="swiglu",
    )(h, hn, wg, wu, wd)


def _group_ones(n, group_shift):
    r = lax.broadcasted_iota(jnp.int32, (n, n), 0) >> group_shift
    c = lax.broadcasted_iota(jnp.int32, (n, n), 1) >> group_shift
    return jnp.where(r == c, 1.0, 0.0).astype(BF16)


GLA_ROWS = 256


def _gla_kernel(tp, pa_ref, lr_ref, wg_ref, bg_ref, gn_ref, o_ref, s_ref):
    qhead = lax.broadcasted_iota(jnp.int32, (1, N_HEADS * GLA_DK), 1) >> 5
    ehead = lax.broadcasted_iota(jnp.int32, (1, N_HEADS * GLA_DV), 1) >> 6
    shead = lax.broadcasted_iota(jnp.int32, (N_HEADS * GLA_DK, 1), 0) >> 5
    bd = shead == ehead
    gsum = _group_ones(N_HEADS * GLA_DV, 6)
    wg_hi, wg_lo = _split_bf16(wg_ref[...])
    bg = bg_ref[...]
    gn = gn_ref[...]
    scale = GLA_DK ** -0.5
    s_ref[...] = jnp.zeros_like(s_ref)

    def block(r0, n_rows):
        n_c = n_rows // CHUNK
        rows = pl.ds(r0, n_rows)
        ri = lax.broadcasted_iota(jnp.int32, (n_rows, n_rows), 0)
        ci = lax.broadcasted_iota(jnp.int32, (n_rows, n_rows), 1)
        same = (ri >> 6) == (ci >> 6)
        tri_bf = jnp.where(jnp.logical_and(same, ri >= ci), 1.0, 0.0).astype(BF16)
        ones_bf = jnp.where(same, 1.0, 0.0).astype(BF16)
        r4 = lax.broadcasted_iota(jnp.int32, (N_HEADS * n_rows, n_rows), 0) & (n_rows - 1)
        c4 = lax.broadcasted_iota(jnp.int32, (N_HEADS * n_rows, n_rows), 1)
        tri4 = jnp.logical_and((r4 >> 6) == (c4 >> 6), r4 >= c4)
        col_chunk = lax.broadcasted_iota(jnp.int32, (1, n_rows), 1) >> 6

        q = pa_ref[0, rows, 0:128].astype(F32) * scale
        k = pa_ref[0, rows, 128:256].astype(F32)
        v = pa_ref[0, rows, 256:512]
        og = pa_ref[0, rows, 512:768].astype(F32)
        lr = lr_ref[0, rows, :]
        valid = (r0 + lax.broadcasted_iota(jnp.int32, (n_rows, 1), 0)) >= FRONT_PAD

        pre = _dot(lr, wg_hi) + _dot(lr, wg_lo) + bg
        logsig = jnp.minimum(pre, 0.0) - jnp.log1p(jnp.exp(-jnp.abs(pre)))
        g = jnp.where(valid, logsig * (1.0 / GLA_TAU), 0.0)
        g_hi, g_lo = _split_bf16(g)
        cum = _dot(tri_bf, g_hi) + _dot(tri_bf, g_lo)
        cum_end = _dot(ones_bf, g_hi) + _dot(ones_bf, g_lo)
        qt_bf = (q * jnp.exp(cum)).astype(BF16)
        kt = (k * jnp.exp(-cum)).astype(BF16)
        kd_t = (k * jnp.exp(cum_end - cum)).T.astype(BF16)
        dec_t = jnp.exp(cum_end.T)

        qs = jnp.concatenate([jnp.where(qhead == h, qt_bf, jnp.zeros_like(qt_bf))
                              for h in range(N_HEADS)], axis=0)
        a = jnp.where(tri4, _dot_nt(qs, kt), 0.0).astype(BF16)
        r = _dot(a, v)
        o = jnp.where(ehead == 0, r[0:n_rows, :], 0.0)
        for h in range(1, N_HEADS):
            o = o + jnp.where(ehead == h, r[h * n_rows:(h + 1) * n_rows, :], 0.0)

        s = s_ref[...]
        inter = []
        for c in range(n_c):
            inter.append(_dot(qt_bf[c * CHUNK:(c + 1) * CHUNK, :], s.astype(BF16)))
            upd = _dot(jnp.where(col_chunk == c, kd_t, jnp.zeros_like(kd_t)), v)
            s = dec_t[:, c * CHUNK:c * CHUNK + 1] * s + jnp.where(bd, upd, 0.0)
        s_ref[...] = s
        o = o + jnp.concatenate(inter, axis=0)

        ms = _dot((o * o).astype(BF16), gsum) * (1.0 / GLA_DV)
        y = o * lax.rsqrt(ms + EPS) * gn * (og * _sigmoid(og))
        o_ref[0, rows, :] = jnp.where(valid, y, 0.0).astype(BF16)

    block(0, Q_BLOCK)

    def body(i, carry):
        block(pl.multiple_of(Q_BLOCK + i * GLA_ROWS, Q_BLOCK), GLA_ROWS)
        return carry

    lax.fori_loop(0, (tp - Q_BLOCK) // GLA_ROWS, body, 0)


def _gla(pa, pc, wg, bg, gn):
    b, tp, _ = pa.shape
    return pl.pallas_call(
        functools.partial(_gla_kernel, tp),
        grid=(b,),
        in_specs=[pl.BlockSpec((1, tp, PA_W), lambda i: (i, 0, 0)),
                  pl.BlockSpec((1, tp, LANES), lambda i: (i, 0, 2)),
                  _resident((LANES, LANES)), _resident((1, LANES)), _resident((1, 256))],
        out_specs=pl.BlockSpec((1, tp, 256), lambda i: (i, 0, 0)),
        out_shape=jax.ShapeDtypeStruct((b, tp, 256), BF16),
        scratch_shapes=[pltpu.VMEM((N_HEADS * GLA_DK, N_HEADS * GLA_DV), F32)],
        compiler_params=_cparams(),
        name="gla",
    )(pa, pc, wg, bg, gn)


RET_BLOCK = 128


def _ret_kernel(tp, pb_ref, cos_ref, sin_ref, dmask_ref, qfac_ref, kfac_ref, dec_ref, gn_ref,
                o_ref, s_ref):
    blk = RET_BLOCK
    n_blocks = tp // blk
    qhead = (lax.broadcasted_iota(jnp.int32, (1, 256), 1) & 127) >> 5
    ehead = lax.broadcasted_iota(jnp.int32, (1, 256), 1) >> 6
    shead = (lax.broadcasted_iota(jnp.int32, (256, 1), 0) & 127) >> 5
    bd = shead == ehead
    gsum = _group_ones(256, 6)
    gn = gn_ref[...]
    s_ref[...] = jnp.zeros_like(s_ref)

    def rope(x, cos, sin):
        x1, x2 = x[:, :128], x[:, 128:]
        return jnp.concatenate([x1 * cos - x2 * sin, x1 * sin + x2 * cos], axis=1)

    def block(j, carry):
        r0 = pl.multiple_of(j * blk, blk)
        rows = pl.ds(r0, blk)
        cos = cos_ref[rows, :]
        sin = sin_ref[rows, :]
        q = rope(pb_ref[0, rows, 0:256].astype(F32), cos, sin)
        k = rope(pb_ref[0, rows, 256:512].astype(F32), cos, sin) * (RET_DK ** -0.5)
        v = pb_ref[0, rows, 512:768]
        og = pb_ref[0, rows, 768:1024].astype(F32)
        valid = (r0 + lax.broadcasted_iota(jnp.int32, (blk, 1), 0)) >= FRONT_PAD

        q_bf = q.astype(BF16)
        qs = jnp.concatenate([jnp.where(qhead == h, q_bf, jnp.zeros_like(q_bf))
                              for h in range(N_HEADS)], axis=0)
        a = (_dot_nt(qs, k.astype(BF16)) * dmask_ref[...]).astype(BF16)
        r = _dot(a, v)
        o = _dot((q * qfac_ref[...]).astype(BF16), s_ref[...].astype(BF16))
        for h in range(N_HEADS):
            o = o + jnp.where(ehead == h, r[h * blk:(h + 1) * blk, :], 0.0)

        kd_t = (k * kfac_ref[...]).T.astype(BF16)
        upd = _dot(kd_t, v)
        s_ref[...] = dec_ref[...] * s_ref[...] + jnp.where(bd, upd, 0.0)

        mu = _dot(o.astype(BF16), gsum) * (1.0 / 64)
        xc = o - mu
        var = _dot((xc * xc).astype(BF16), gsum) * (1.0 / 64)
        y = xc * lax.rsqrt(var + EPS) * gn * (og * _sigmoid(og))
        o_ref[0, rows, :] = jnp.where(valid, y, 0.0).astype(BF16)
        return carry

    lax.fori_loop(0, n_blocks, block, 0, unroll=4)


def _ret(pb, cos, sin, dmask, qfac, kfac, dec, gn):
    b, tp, _ = pb.shape
    return pl.pallas_call(
        functools.partial(_ret_kernel, tp),
        grid=(b,),
        in_specs=[pl.BlockSpec((1, tp, PB_W), lambda i: (i, 0, 0)),
                  _resident(cos.shape), _resident(sin.shape), _resident(dmask.shape),
                  _resident(qfac.shape), _resident(kfac.shape), _resident(dec.shape),
                  _resident((1, 256))],
        out_specs=pl.BlockSpec((1, tp, 256), lambda i: (i, 0, 0)),
        out_shape=jax.ShapeDtypeStruct((b, tp, 256), BF16),
        scratch_shapes=[pltpu.VMEM((256, 256), F32)],
        compiler_params=_cparams(),
        name="retention",
    )(pb, cos, sin, dmask, qfac, kfac, dec, gn)


ATT_ROWS = 256
LOG2E = 1.4426950408889634
V_ONE = 64


def _aligned(x, m):
    return x if isinstance(x, int) else pl.multiple_of(x, m)


def _for_chunks(n, body):
    if isinstance(n, int):
        for j in range(n):
            body(j)
        return

    def quad(t, carry):
        for u in range(4):
            body(4 * t + u)
        return carry

    lax.fori_loop(0, n >> 2, quad, 0)
    done = (n >> 2) << 2

    @pl.when((n & 2) != 0)
    def _():
        body(done)
        body(done + 1)

    @pl.when((n & 1) != 0)
    def _():
        body(done + (n & 2))


def _mask_groups(mask, s, n_g):
    rows = s.shape[0] // n_g
    return jnp.concatenate([jnp.where(mask, s[g * rows:(g + 1) * rows, :], NEG)
                            for g in range(n_g)], axis=0)


def _attn_block0(qk_fn, v_fn, n_g):
    rows = pl.ds(0, Q_BLOCK)
    qrow = lax.broadcasted_iota(jnp.int32, (Q_BLOCK, 1), 0)
    kcol = lax.broadcasted_iota(jnp.int32, (1, Q_BLOCK), 1)
    mask = jnp.logical_and(kcol <= qrow, kcol >= FRONT_PAD)
    s = _mask_groups(mask, qk_fn(0, Q_BLOCK, rows), n_g)
    p_bf = jnp.exp2(s - jnp.max(s, axis=-1, keepdims=True)).astype(BF16)
    pv = jnp.concatenate([_dot(p_bf[g * Q_BLOCK:(g + 1) * Q_BLOCK, :], v_fn(g, rows))
                          for g in range(n_g)], axis=0)
    return pv / pv[:, V_ONE:V_ONE + 1]


def _attn_row0(i):
    return _aligned(Q_BLOCK + (i - 1) * ATT_ROWS, Q_BLOCK)


def _attn_result(acc_s):
    acc = acc_s[...]
    return acc / acc[:, V_ONE:V_ONE + 1]


def _attn_block(i, qk_fn, v_fn, n_g, s_meta, s_s, mx_s, acc_s, mid_fn=None):
    rr = ATT_ROWS
    q0 = _attn_row0(i)
    meta_rows = pl.ds(0, Q_BLOCK)
    kcol = lax.broadcasted_iota(jnp.int32, (1, Q_BLOCK), 1)

    def key_rows(j):
        return pl.ds(_aligned(Q_BLOCK + j * rr, Q_BLOCK), rr)

    s = jnp.where(kcol >= FRONT_PAD, qk_fn(q0, rr, meta_rows), NEG)
    s_meta[...] = s
    mx_s[...] = s

    def pass1(j):
        sj = qk_fn(q0, rr, key_rows(j))
        s_s[j] = sj
        mx_s[...] = jnp.maximum(mx_s[...], jnp.maximum(sj[:, :LANES], sj[:, LANES:]))

    _for_chunks(i - 1, pass1)
    if mid_fn is not None:
        mid_fn()
    causal = (lax.broadcasted_iota(jnp.int32, (rr, rr), 1)
              <= lax.broadcasted_iota(jnp.int32, (rr, rr), 0))
    sd = _mask_groups(causal, qk_fn(q0, rr, pl.ds(q0, rr)), n_g)
    s_s[i - 1] = sd
    m = jnp.max(jnp.maximum(mx_s[...], jnp.maximum(sd[:, :LANES], sd[:, LANES:])),
                axis=-1, keepdims=True)
    mx_s[...] = jnp.broadcast_to(m, mx_s.shape)

    p_bf = jnp.exp2(s_meta[...] - mx_s[...]).astype(BF16)
    for g in range(n_g):
        acc_s[g * rr:(g + 1) * rr, :] = _dot(p_bf[g * rr:(g + 1) * rr, :], v_fn(g, meta_rows))

    def pass2(j):
        sj = s_s[j]
        mrep = mx_s[...]
        p0 = jnp.exp2(sj[:, :LANES] - mrep)
        p1 = jnp.exp2(sj[:, LANES:] - mrep)
        pj = jnp.concatenate([p0.astype(BF16), p1.astype(BF16)], axis=1)
        for g in range(n_g):
            acc_s[g * rr:(g + 1) * rr, :] += _dot(pj[g * rr:(g + 1) * rr, :], v_fn(g, key_rows(j)))

    _for_chunks(i, pass2)


def _attn_all_blocks(n_q, block_fn, finish_fn):
    block_fn(1, None)

    def body(i, carry):
        block_fn(i, lambda: finish_fn(i - 1))
        return carry

    lax.fori_loop(2, n_q + 1, body, 0)
    finish_fn(n_q)


def _heads_to_lanes(per_head):
    low = lax.broadcasted_iota(jnp.int32, (1, LANES), 1) < 64
    lo = jnp.where(low, per_head[0], pltpu.roll(per_head[1], 64, 1))
    hi = jnp.where(low, per_head[2], pltpu.roll(per_head[3], 64, 1))
    return jnp.concatenate([lo, hi], axis=1)


def _with_ones_lane(v):
    lane = lax.broadcasted_iota(jnp.int32, (1, LANES), 1)
    return jnp.where(lane == V_ONE, jnp.ones_like(v), v)


def _mla_kernel(tp, pc_ref, qn_ref, kvn_ref, wuq_ref, wukv_ref, cos_ref, sa_ref, sb_ref,
                o_ref, q_s, k_s, v_s, s_meta, s_s, mx_s, acc_s):
    scale = (MLA_NOPE + MLA_ROPE) ** -0.5 * LOG2E
    is_q = lax.broadcasted_iota(jnp.int32, (1, 256), 1) < MLA_Q_RANK

    def prep(r0, n_rows):
        rows = pl.ds(r0, n_rows)
        x = pc_ref[0, rows, 0:256].astype(F32)
        x2 = x * x
        ms_q = jnp.sum(jnp.where(is_q, x2, 0.0), axis=-1, keepdims=True) * (1.0 / MLA_Q_RANK)
        ms_kv = jnp.sum(jnp.where(is_q, 0.0, x2), axis=-1, keepdims=True) * (1.0 / MLA_KV_RANK)
        yq = (x * lax.rsqrt(ms_q + EPS) * qn_ref[...]).astype(BF16)
        ykv = (x * lax.rsqrt(ms_kv + EPS) * kvn_ref[...]).astype(BF16)
        cq = _dot(yq, wuq_ref[...])
        kv = _dot(ykv, wukv_ref[...])
        cos = cos_ref[rows, :]
        sa = sa_ref[rows, :]
        sb = sb_ref[rows, :]

        def rope(t):
            return t * cos + pltpu.roll(t, 16, 1) * sa + pltpu.roll(t, LANES - 16, 1) * sb

        kpe_in = pc_ref[0, rows, 256:384].astype(F32)
        kpe = rope(jnp.where(lax.broadcasted_iota(jnp.int32, (1, LANES), 1) >= 64, kpe_in, 0.0))
        for h in range(N_HEADS):
            q_s[h, rows, :] = (rope(cq[:, h * LANES:(h + 1) * LANES]) * scale).astype(BF16)
            k_s[h, rows, :] = (kv[:, h * LANES:(h + 1) * LANES] + kpe).astype(BF16)
            v_s[h, rows, :] = _with_ones_lane(
                kv[:, (N_HEADS + h) * LANES:(N_HEADS + h + 1) * LANES]).astype(BF16)

    prep(0, Q_BLOCK)

    def prep_body(i, carry):
        prep(pl.multiple_of(Q_BLOCK + i * ATT_ROWS, Q_BLOCK), ATT_ROWS)
        return carry

    lax.fori_loop(0, (tp - Q_BLOCK) // ATT_ROWS, prep_body, 0)

    def qk_fn(q0, n_rows, krows):
        return jnp.concatenate([_dot_nt(q_s[h, pl.ds(q0, n_rows), :], k_s[h, krows, :])
                                for h in range(N_HEADS)], axis=0)

    def v_fn(h, krows):
        return v_s[h, krows, :]

    def emit(q0, n_rows, o, first):
        y = _heads_to_lanes([o[h * n_rows:(h + 1) * n_rows, :] for h in range(N_HEADS)])
        if first:
            qrow = lax.broadcasted_iota(jnp.int32, (n_rows, 1), 0)
            y = jnp.where(qrow >= FRONT_PAD, y, 0.0)
        o_ref[0, pl.ds(q0, n_rows), :] = y.astype(BF16)

    emit(0, Q_BLOCK, _attn_block0(qk_fn, v_fn, N_HEADS), True)

    def block(i, mid_fn):
        _attn_block(i, qk_fn, v_fn, N_HEADS, s_meta, s_s, mx_s, acc_s, mid_fn)

    def finish(i):
        emit(_attn_row0(i), ATT_ROWS, _attn_result(acc_s), False)

    _attn_all_blocks((tp - Q_BLOCK) // ATT_ROWS, block, finish)


def _attn_scratch(n_g, tp):
    g_rows = n_g * ATT_ROWS
    n_slots = (tp - Q_BLOCK) // ATT_ROWS
    return [pltpu.VMEM((g_rows, LANES), F32),
            pltpu.VMEM((n_slots, g_rows, ATT_ROWS), F32),
            pltpu.VMEM((g_rows, LANES), F32),
            pltpu.VMEM((g_rows, LANES), F32)]


def _mla(pc, qn, kvn, wuq, wukv, cos, sa, sb):
    b, tp, _ = pc.shape
    return pl.pallas_call(
        functools.partial(_mla_kernel, tp),
        grid=(b,),
        in_specs=[pl.BlockSpec((1, tp, PC_W), lambda i: (i, 0, 0)),
                  _resident((1, 256)), _resident((1, 256)),
                  _resident(wuq.shape), _resident(wukv.shape),
                  _resident(cos.shape), _resident(sa.shape), _resident(sb.shape)],
        out_specs=pl.BlockSpec((1, tp, 256), lambda i: (i, 0, 0)),
        out_shape=jax.ShapeDtypeStruct((b, tp, 256), BF16),
        scratch_shapes=[pltpu.VMEM((N_HEADS, tp, LANES), BF16),
                        pltpu.VMEM((N_HEADS, tp, LANES), BF16),
                        pltpu.VMEM((N_HEADS, tp, LANES), BF16)] + _attn_scratch(N_HEADS, tp),
        compiler_params=_cparams(),
        name="mla",
    )(pc, qn, kvn, wuq, wukv, cos, sa, sb)


def _diff_kernel(tp, lam_init, pd_ref, lam_ref, dn_ref, o_ref, qs_s, v_s, s_meta, s_s, mx_s, acc_s):
    n_maps = 2 * N_HEADS

    def fill_values(i, carry):
        rows = pl.ds(pl.multiple_of(i * Q_BLOCK, Q_BLOCK), Q_BLOCK)
        low = lax.broadcasted_iota(jnp.int32, (1, LANES), 1) < V_ONE
        for pair in range(N_HEADS // 2):
            two = pd_ref[0, rows, 512 + pair * LANES:512 + (pair + 1) * LANES].astype(F32)
            for h, vals in ((2 * pair, two), (2 * pair + 1, pltpu.roll(two, V_ONE, 1))):
                v_s[h, rows, :] = _with_ones_lane(jnp.where(low, vals, 0.0)).astype(BF16)
        return carry

    lax.fori_loop(0, tp // Q_BLOCK, fill_values, 0)
    scale = DIFF_DK ** -0.5 * LOG2E
    group = lax.broadcasted_iota(jnp.int32, (1, 256), 1) >> 5
    lv = lam_ref[...]
    lam = (jnp.exp(jnp.sum(lv[0:1, :] * lv[1:2, :], axis=-1, keepdims=True))
           - jnp.exp(jnp.sum(lv[2:3, :] * lv[3:4, :], axis=-1, keepdims=True)) + lam_init)
    dn = dn_ref[...]

    def stack_queries(q0, n_rows):
        q = (pd_ref[0, pl.ds(q0, n_rows), 0:256].astype(F32) * scale).astype(BF16)
        for g in range(n_maps):
            qs_s[g * n_rows:(g + 1) * n_rows, :] = jnp.where(group == g, q, jnp.zeros_like(q))

    def qk_fn(q0, n_rows, krows):
        return _dot_nt(qs_s[0:n_maps * n_rows, :], pd_ref[0, krows, 256:512])

    def v_fn(g, krows):
        return v_s[g // 2, krows, :]

    def emit(q0, n_rows, o, first):
        od = jnp.concatenate([o[(2 * h) * n_rows:(2 * h + 1) * n_rows, :]
                              - lam * o[(2 * h + 1) * n_rows:(2 * h + 2) * n_rows, :]
                              for h in range(N_HEADS)], axis=0)
        od = jnp.where(lax.broadcasted_iota(jnp.int32, (1, LANES), 1) < V_ONE, od, 0.0)
        ms = jnp.sum(od * od, axis=-1, keepdims=True) * (1.0 / 64)
        yh = od * lax.rsqrt(ms + EPS) * dn * (1.0 - lam_init)
        y = _heads_to_lanes([yh[h * n_rows:(h + 1) * n_rows, :] for h in range(N_HEADS)])
        if first:
            qrow = lax.broadcasted_iota(jnp.int32, (n_rows, 1), 0)
            y = jnp.where(qrow >= FRONT_PAD, y, 0.0)
        o_ref[0, pl.ds(q0, n_rows), :] = y.astype(BF16)

    stack_queries(0, Q_BLOCK)
    emit(0, Q_BLOCK, _attn_block0(qk_fn, v_fn, n_maps), True)

    def block(i, mid_fn):
        stack_queries(_attn_row0(i), ATT_ROWS)
        _attn_block(i, qk_fn, v_fn, n_maps, s_meta, s_s, mx_s, acc_s, mid_fn)

    def finish(i):
        emit(_attn_row0(i), ATT_ROWS, _attn_result(acc_s), False)

    _attn_all_blocks((tp - Q_BLOCK) // ATT_ROWS, block, finish)


def _diff(pd, lam_rows, dn, lam_init):
    b, tp, _ = pd.shape
    n_maps = 2 * N_HEADS
    return pl.pallas_call(
        functools.partial(_diff_kernel, tp, lam_init),
        grid=(b,),
        in_specs=[pl.BlockSpec((1, tp, PD_W), lambda i: (i, 0, 0)),
                  _resident(lam_rows.shape), _resident((1, LANES))],
        out_specs=pl.BlockSpec((1, tp, 256), lambda i: (i, 0, 0)),
        out_shape=jax.ShapeDtypeStruct((b, tp, 256), BF16),
        scratch_shapes=[pltpu.VMEM((n_maps * ATT_ROWS, 256), BF16),
                        pltpu.VMEM((N_HEADS, tp, LANES), BF16)] + _attn_scratch(n_maps, tp),
        compiler_params=_cparams(),
        name="diffattn",
    )(pd, lam_rows, dn)


def _pad_cols(x, width):
    return jnp.pad(x, ((0, 0), (0, width - x.shape[1])))


def _rot_split(w):
    d = w.shape[0]
    return w.reshape(d, N_HEADS, 2, 32).transpose(0, 2, 1, 3).reshape(d, 256)


def _layout_w_in(w):
    sizes = (128, 128, 256, 16, 256, 256, 256, 256, 256, 192, 64, 32, 256, 256, 256)
    offs = [0]
    for s_ in sizes:
        offs.append(offs[-1] + s_)
    seg = [w[:, offs[i]:offs[i + 1]] for i in range(len(sizes))]
    (a_q, a_k, a_v, a_lr, a_og, r_q, r_k, r_v, r_og, c_cq, c_ckv, c_kpe, d_q, d_k, d_v) = seg
    d = w.shape[0]
    z = lambda n: jnp.zeros((d, n), w.dtype)
    cols = [a_q, a_k, a_v, a_og,
            _rot_split(r_q), _rot_split(r_k), r_v, r_og,
            c_cq, c_ckv, a_lr, z(48), c_kpe, z(32),
            d_q, d_k, d_v]
    return jnp.concatenate(cols, axis=1).astype(BF16)


def _tables(tp):
    pos = jnp.arange(tp, dtype=F32) - FRONT_PAD
    inv = ROPE_THETA ** (-jnp.arange(32, dtype=F32) / 32)
    ang = pos[:, None] * inv[None, :]
    ret_cos = jnp.tile(jnp.cos(ang), (1, N_HEADS))
    ret_sin = jnp.tile(jnp.sin(ang), (1, N_HEADS))
    inv16 = ROPE_THETA ** (-jnp.arange(16, dtype=F32) / 16)
    ang16 = pos[:, None] * inv16[None, :]
    c16, s16 = jnp.cos(ang16), jnp.sin(ang16)
    one = lambda n: jnp.ones((tp, n), F32)
    zero = lambda n: jnp.zeros((tp, n), F32)
    mla_cos = jnp.concatenate([one(64), c16, c16, one(32)], axis=1)
    mla_sa = jnp.concatenate([zero(80), s16, zero(32)], axis=1)
    mla_sb = jnp.concatenate([zero(64), -s16, zero(48)], axis=1)
    lg = jnp.log(1.0 - jnp.exp2(-5.0 - jnp.arange(N_HEADS, dtype=F32)))
    idx = jnp.arange(RET_BLOCK, dtype=F32)
    rel = idx[:, None] - idx[None, :]
    dmask = jnp.where(rel[None] >= 0, jnp.exp(rel[None] * lg[:, None, None]), 0.0)
    dmask = dmask.reshape(N_HEADS * RET_BLOCK, RET_BLOCK)
    lane_head = (jnp.arange(256) % 128) // 32
    qfac = jnp.exp((idx[:, None] + 1.0) * lg[lane_head][None, :])
    kfac = jnp.exp((RET_BLOCK - 1.0 - idx[:, None]) * lg[lane_head][None, :])
    dec = jnp.exp(RET_BLOCK * lg[lane_head])[:, None]
    return ret_cos, ret_sin, mla_cos, mla_sa, mla_sb, dmask, qfac, kfac, dec


def kernel(x, meta_tokens, attn_norm, w_in, gla_w_gate, gla_b_gate, gla_norm, ret_norm, mla_q_norm, mla_w_uq, mla_kv_norm, mla_w_ukv, diff_lambda, diff_norm, w_out, ffn_norm, ffn_w_gate, ffn_w_up, ffn_w_down, moe_router, moe_w_gate, moe_w_up, moe_w_down, final_norm):
    b, seq, d = x.shape
    tp = FRONT_PAD + N_META + seq
    n = b * tp
    meta = jnp.broadcast_to(meta_tokens[None].astype(x.dtype), (b, N_META, d))
    h = jnp.concatenate([jnp.zeros((b, FRONT_PAD, d), x.dtype), meta, x], axis=1).reshape(n, d)
    ret_cos, ret_sin, mla_cos, mla_sa, mla_sb, dmask, qfac, kfac, dec = _tables(tp)

    for li in range(DEPTH):
        pa, pb, pc, pd = _inproj(h, attn_norm[li][None, :], _layout_w_in(w_in[li]))
        pa, pb, pc, pd = (p.reshape(b, tp, -1) for p in (pa, pb, pc, pd))

        wgate = jnp.pad(gla_w_gate[li], ((0, LANES - GLA_GATE_RANK), (0, 0)))
        o_a = _gla(pa, pc, wgate, gla_b_gate[li][None, :], jnp.tile(gla_norm[li], N_HEADS)[None, :])
        o_b = _ret(pb, ret_cos, ret_sin, dmask, qfac, kfac, dec, jnp.tile(ret_norm[li], N_HEADS)[None, :])

        qn = _pad_cols(mla_q_norm[li][None, :], 256)
        kvn = jnp.pad(mla_kv_norm[li][None, :], ((0, 0), (MLA_Q_RANK, 0)))
        wuq = jnp.pad(mla_w_uq[li].reshape(MLA_Q_RANK, N_HEADS, MLA_NOPE + MLA_ROPE),
                      ((0, 256 - MLA_Q_RANK), (0, 0), (0, LANES - MLA_NOPE - MLA_ROPE)))
        wuq = wuq.reshape(256, N_HEADS * LANES).astype(BF16)
        wukv = mla_w_ukv[li].reshape(MLA_KV_RANK, N_HEADS, 2, 64)
        wukv = jnp.pad(wukv, ((MLA_Q_RANK, 0), (0, 0), (0, 0), (0, 64)))
        wukv = wukv.transpose(0, 2, 1, 3).reshape(256, 2 * N_HEADS * LANES).astype(BF16)
        o_c = _mla(pc, qn, kvn, wuq, wukv, mla_cos, mla_sa, mla_sb)

        lam_init = 0.8 - 0.6 * math.exp(-0.3 * li)
        o_d = _diff(pd, diff_lambda[li], _pad_cols(diff_norm[li][None, :], LANES), lam_init)

        o_a, o_b, o_c, o_d = (o.reshape(n, 256) for o in (o_a, o_b, o_c, o_d))
        wo = w_out[li].astype(BF16)
        fn = ffn_norm[li][None, :]
        j = li // 2
        if li % 2 == 0:
            h, hn = _outproj(h, o_a, o_b, o_c, o_d, wo, fn)
            h = _swiglu(h, hn, ffn_w_gate[j].astype(BF16), ffn_w_up[j].astype(BF16),
                        ffn_w_down[j].astype(BF16))
        else:
            h, hn, info, cnt = _outproj(h, o_a, o_b, o_c, o_d, wo, fn,
                                        router=_pad_cols(moe_router[j], LANES), seq_rows=tp)
            return _routed_moe_norm(h, hn, info, cnt, moe_w_gate[j].astype(BF16),
                                    moe_w_up[j].astype(BF16), moe_w_down[j].astype(BF16),
                                    final_norm[None, :], b, tp)
```

```python
import functools
import math

import jax
import jax.numpy as jnp
from jax import lax
from jax.experimental import pallas as pl
from jax.experimental.pallas import tpu as pltpu

F32 = jnp.float32
BF16 = jnp.bfloat16

D_MODEL = 1024
DEPTH = 2
N_META = 16
CHUNK = 64
Q_BLOCK = 128
FRONT_PAD = Q_BLOCK - N_META
EPS = 1e-6
NEG = -1e30
ROPE_THETA = 10000.0
N_HEADS = 4
GLA_DK = 32
GLA_DV = 64
GLA_GATE_RANK = 16
GLA_TAU = 16.0
RET_DK = 64
MLA_Q_RANK = 192
MLA_KV_RANK = 64
MLA_NOPE = 64
MLA_ROPE = 32
DIFF_DK = 32
N_EXPERTS = 8

LANES = 128
SUBLANES = 8
ROW_TILE = 512
FF_CHUNK = 256
VMEM_LIMIT = 56 * 1024 * 1024

PA_W = 768
PB_W = 1024
PC_W = 384
PD_W = 768


def _cparams(n_axes=1):
    return pltpu.CompilerParams(dimension_semantics=("arbitrary",) * n_axes,
                                vmem_limit_bytes=VMEM_LIMIT)


def _resident(shape):
    nd = len(shape)
    return pl.BlockSpec(shape, lambda *_: (0,) * nd, pipeline_mode=pl.Buffered(1))


def _sigmoid(x):
    return 1.0 / (1.0 + jnp.exp(-x))


def _split_bf16(x):
    hi = x.astype(BF16)
    lo = (x - hi.astype(F32)).astype(BF16)
    return hi, lo


def _dot(a, b):
    return jnp.dot(a, b, preferred_element_type=F32)


def _dot_nt(a, b):
    return lax.dot_general(a, b, (((1,), (1,)), ((), ())), preferred_element_type=F32)


def _inproj_kernel(h_ref, g_ref, w_ref, pa_ref, pb_ref, pc_ref, pd_ref):
    x = h_ref[...]
    ms = jnp.mean(x * x, axis=-1, keepdims=True)
    y = (x * lax.rsqrt(ms + EPS) * g_ref[...]).astype(BF16)
    off = 0
    for o_ref, width in ((pa_ref, PA_W), (pb_ref, PB_W), (pc_ref, PC_W), (pd_ref, PD_W)):
        o_ref[...] = _dot(y, w_ref[:, off:off + width]).astype(BF16)
        off += width


def _inproj(h, g, w):
    n = h.shape[0]
    wtot = PA_W + PB_W + PC_W + PD_W
    row = lambda width: pl.BlockSpec((ROW_TILE, width), lambda i: (i, 0))
    return pl.pallas_call(
        _inproj_kernel,
        grid=(n // ROW_TILE,),
        in_specs=[row(D_MODEL), _resident((1, D_MODEL)), _resident((D_MODEL, wtot))],
        out_specs=[row(PA_W), row(PB_W), row(PC_W), row(PD_W)],
        out_shape=[jax.ShapeDtypeStruct((n, w_), BF16) for w_ in (PA_W, PB_W, PC_W, PD_W)],
        compiler_params=_cparams(),
        name="inproj",
    )(h, g, w)


def _outproj_kernel(router_seq_rows, h_ref, oa_ref, ob_ref, oc_ref, od_ref, wo_ref, fn_ref, *rest):
    with_router = router_seq_rows > 0
    if with_router:
        router_ref, below_ref, hmid_ref, hn_ref, info_ref, cnt_ref, carry_ref = rest
    else:
        hmid_ref, hn_ref = rest
    o = jnp.concatenate([oa_ref[...], ob_ref[...], oc_ref[...], od_ref[...]], axis=1)
    hm = h_ref[...] + _dot(o, wo_ref[...])
    hmid_ref[...] = hm
    ms = jnp.mean(hm * hm, axis=-1, keepdims=True)
    y = hm * lax.rsqrt(ms + EPS) * fn_ref[...]
    hn_ref[...] = y.astype(hn_ref.dtype)
    if with_router:
        y_hi, y_lo = _split_bf16(y)
        r_hi, r_lo = _split_bf16(router_ref[...])
        logits = _dot(y_hi, r_hi) + _dot(y_hi, r_lo) + _dot(y_lo, r_hi)
        lane = lax.broadcasted_iota(jnp.int32, logits.shape, 1).astype(F32)
        ninf = float("-inf")
        logits = jnp.where(lane < N_EXPERTS, logits, ninf)
        m1 = jnp.max(logits, axis=-1, keepdims=True)
        i1 = jnp.min(jnp.where(logits == m1, lane, float(LANES)), axis=-1, keepdims=True)
        rest_l = jnp.where(lane == i1, ninf, logits)
        m2 = jnp.max(rest_l, axis=-1, keepdims=True)
        i2 = jnp.min(jnp.where(rest_l == m2, lane, float(LANES)), axis=-1, keepdims=True)
        e2 = jnp.exp(m2 - m1)
        den = 1.0 + e2
        @pl.when(pl.program_id(0) == 0)
        def _():
            carry_ref[...] = jnp.zeros_like(carry_ref)
        sel = jnp.where(lane == i1, 1.0, 0.0) + jnp.where(lane == i2, 1.0, 0.0)
        rows = sel.shape[0]
        seq_pos = (lax.rem(pl.program_id(0) * rows, router_seq_rows)
                   + lax.broadcasted_iota(jnp.int32, (rows, 1), 0))
        seq_pos = jnp.where(seq_pos >= router_seq_rows, seq_pos - router_seq_rows, seq_pos)
        sel = jnp.where(seq_pos >= Q_BLOCK, sel, 0.0)
        count = _dot(below_ref[...], sel.astype(BF16)) + carry_ref[...]
        r1 = jnp.sum(jnp.where(lane == i1, count, 0.0), axis=-1, keepdims=True)
        r2 = jnp.sum(jnp.where(lane == i2, count, 0.0), axis=-1, keepdims=True)
        total = carry_ref[...] + jnp.sum(sel, axis=0, keepdims=True)
        carry_ref[...] = total
        cnt_ref[...] = jnp.broadcast_to(total, cnt_ref.shape)
        info = jnp.where(lane == 0, i1, 0.0)
        for k, val in enumerate((i2, 1.0 / den, e2 / den, r1, r2), start=1):
            info = jnp.where(lane == k, val, info)
        info_ref[...] = info


def _outproj(h, oa, ob, oc, od, wo, fn, router=None, seq_rows=0):
    n = h.shape[0]
    row = lambda width: pl.BlockSpec((ROW_TILE, width), lambda i: (i, 0))
    in_specs = [row(D_MODEL), row(256), row(256), row(256), row(256),
                _resident((D_MODEL, D_MODEL)), _resident((1, D_MODEL))]
    out_specs = [row(D_MODEL), row(D_MODEL)]
    out_shape = [jax.ShapeDtypeStruct((n, D_MODEL), F32),
                 jax.ShapeDtypeStruct((n, D_MODEL), BF16 if router is None else F32)]
    args = [h, oa, ob, oc, od, wo, fn]
    scratch = []
    if router is not None:
        in_specs += [_resident((D_MODEL, LANES)), _resident((ROW_TILE, ROW_TILE))]
        out_specs += [row(LANES), pl.BlockSpec((8, LANES), lambda i: (0, 0))]
        out_shape += [jax.ShapeDtypeStruct((n, LANES), F32), jax.ShapeDtypeStruct((8, LANES), F32)]
        args += [router, jnp.tri(ROW_TILE, k=-1, dtype=BF16)]
        scratch.append(pltpu.VMEM((1, LANES), F32))
    return pl.pallas_call(
        functools.partial(_outproj_kernel, seq_rows if router is not None else 0),
        grid=(n // ROW_TILE,),
        in_specs=in_specs, out_specs=out_specs, out_shape=out_shape,
        scratch_shapes=scratch,
        compiler_params=_cparams(),
        name="outproj_router" if router is not None else "outproj",
    )(*args)


def _inverse_map_kernel(zs_ref, ze_ref, pos_ref, tok_ref):
    i = pl.program_id(0)

    @pl.when(i == 0)
    def _():
        def zero(j, carry):
            tok_ref[j] = 0
            return carry

        for k in range(zs_ref.shape[0]):
            lax.fori_loop(zs_ref[k], ze_ref[k], zero, 0)

    base = i * ROW_TILE

    def body(r, carry):
        tok_ref[pos_ref[0, 0, 2 * r]] = base + r
        tok_ref[pos_ref[0, 0, 2 * r + 1]] = base + r
        return carry

    lax.fori_loop(0, ROW_TILE, body, 0, unroll=8)


def _inverse_map(zero_start, zero_end, pos, n_sorted):
    return pl.pallas_call(
        _inverse_map_kernel,
        grid_spec=pltpu.PrefetchScalarGridSpec(
            num_scalar_prefetch=2,
            grid=(pos.shape[0],),
            in_specs=[pl.BlockSpec((1, 1, 2 * ROW_TILE), lambda i, zs, ze: (i, 0, 0),
                                   memory_space=pltpu.SMEM)],
            out_specs=pl.BlockSpec(memory_space=pltpu.SMEM)),
        out_shape=jax.ShapeDtypeStruct((n_sorted + ROW_TILE,), jnp.int32),
        compiler_params=_cparams(),
        name="moe_inverse_map",
    )(zero_start, zero_end, pos)


def _swiglu_tile(x, wg, wu, wd, acc_ref, per_chunk=None):
    n_chunks = wd.shape[0] // FF_CHUNK
    for c in range(n_chunks):
        cols = slice(c * FF_CHUNK, (c + 1) * FF_CHUNK)
        g = _dot(x, wg[:, cols])
        u = _dot(x, wu[:, cols])
        part = _dot((g * _sigmoid(g) * u).astype(BF16), wd[cols, :])
        if c == 0:
            acc_ref[...] = part
        elif c < n_chunks - 1:
            acc_ref[...] += part
        if per_chunk is not None:
            per_chunk(c, n_chunks)
    return acc_ref[...] + part


def _experts_kernel(te_ref, tv_ref, tok_ref, tok_next_ref, hn_ref, wg_ref, wu_ref, wd_ref, ys_ref,
                    x_buf, acc_ref, sem):
    i = pl.program_id(0)
    last = pl.num_programs(0) - 1
    slot = i & 1
    n_blocks = x_buf.shape[1]

    def row_copy(t_ref, s, rb, u):
        tok = t_ref[0, 0, rb * SUBLANES + u]
        return pltpu.make_async_copy(hn_ref.at[pl.ds(tok, 1), :], x_buf.at[s, rb, pl.ds(u, 1), :],
                                     sem.at[s])

    def wait_tile(s):
        pltpu.make_async_copy(x_buf.at[s], x_buf.at[s], sem.at[s]).wait()

    @pl.when(i == 0)
    def _():
        def start_block(rb, carry):
            for u in range(SUBLANES):
                row_copy(tok_ref, 0, rb, u).start(priority=u % 2)
            return carry

        lax.fori_loop(0, n_blocks, start_block, 0)

    @pl.when(jnp.logical_or(i == 0, tv_ref[jnp.maximum(i - 1, 0)] > 0))
    def _():
        wait_tile(slot)

    @pl.when(tv_ref[i] > 0)
    def _():
        def prefetch(c, n_chunks):
            per = -(-ROW_TILE // n_chunks)
            for r in range(c * per, min((c + 1) * per, ROW_TILE)):
                row_copy(tok_next_ref, 1 - slot, r // SUBLANES, r % SUBLANES).start(priority=r % 2)

        x = x_buf[slot].reshape(ROW_TILE, -1).astype(BF16)
        ys_ref[...] = _swiglu_tile(x, wg_ref.at[0], wu_ref.at[0], wd_ref.at[0], acc_ref, prefetch)

    @pl.when(tv_ref[i] == 0)
    def _():
        ys_ref[...] = jnp.zeros_like(ys_ref)

    @pl.when(jnp.logical_and(i == last, tv_ref[i] > 0))
    def _():
        wait_tile(1 - slot)


def _experts(tile_expert, tile_valid, tok, hn, wg, wu, wd):
    n_tiles = tok.shape[0]
    d = hn.shape[1]
    wspec = lambda shape: pl.BlockSpec((1,) + shape[1:], lambda i, te, tv: (te[i], 0, 0),
                                       pipeline_mode=pl.Buffered(1))
    tok_spec = lambda index: pl.BlockSpec((1, 1, ROW_TILE), index, memory_space=pltpu.SMEM)
    return pl.pallas_call(
        _experts_kernel,
        grid_spec=pltpu.PrefetchScalarGridSpec(
            num_scalar_prefetch=2,
            grid=(n_tiles,),
            in_specs=[tok_spec(lambda i, te, tv: (i, 0, 0)),
                      tok_spec(lambda i, te, tv: (jnp.minimum(i + 1, n_tiles - 1), 0, 0)),
                      pl.BlockSpec(memory_space=pl.ANY),
                      wspec(wg.shape), wspec(wu.shape), wspec(wd.shape)],
            out_specs=pl.BlockSpec((ROW_TILE, d), lambda i, te, tv: (i, 0)),
            scratch_shapes=[pltpu.VMEM((2, ROW_TILE // SUBLANES, SUBLANES, d), F32),
                            pltpu.VMEM((ROW_TILE, d), F32),
                            pltpu.SemaphoreType.DMA((2,))]),
        out_shape=jax.ShapeDtypeStruct((n_tiles * ROW_TILE, d), F32),
        compiler_params=_cparams(),
        name="moe_experts",
    )(tile_expert, tile_valid, tok, tok, hn, wg, wu, wd)


def _combine_norm_kernel(tp, pos_ref, h_ref, info_ref, ys_ref, g_ref, o_ref,
                         h_buf, info_buf, y_buf, sem, row_sem):
    n_rows = h_buf.shape[0]
    row0 = pl.multiple_of(pl.program_id(0) * tp + Q_BLOCK + pl.program_id(1) * n_rows, Q_BLOCK)
    h_copy = pltpu.make_async_copy(h_ref.at[pl.ds(row0, n_rows), :], h_buf, sem.at[0])
    info_copy = pltpu.make_async_copy(info_ref.at[pl.ds(row0, n_rows), :], info_buf, sem.at[1])
    h_copy.start()
    info_copy.start()

    def row_copy(rb, u, k):
        p = pos_ref[0, 0, rb * (2 * SUBLANES) + 2 * u + k]
        return pltpu.make_async_copy(ys_ref.at[pl.ds(p, 1), :], y_buf.at[k, rb, pl.ds(u, 1), :], row_sem)

    def start(rb, carry):
        for u in range(SUBLANES):
            row_copy(rb, u, 0).start(priority=0)
            row_copy(rb, u, 1).start(priority=1)
        return carry

    lax.fori_loop(0, n_rows // SUBLANES, start, 0)
    pltpu.make_async_copy(y_buf, y_buf, row_sem).wait()
    h_copy.wait()
    info_copy.wait()
    info = info_buf[...]
    y0 = y_buf[0].reshape(n_rows, -1)
    y1 = y_buf[1].reshape(n_rows, -1)
    x = h_buf[...] + info[:, 2:3] * y0 + info[:, 3:4] * y1
    ms = jnp.mean(x * x, axis=-1, keepdims=True)
    o_ref[0] = x * lax.rsqrt(ms + EPS) * g_ref[...]


def _combine_norm(pos, h, info, ys, g, b, tp):
    n, d = h.shape
    seq = tp - Q_BLOCK
    tiles = seq // ROW_TILE
    any_spec = pl.BlockSpec(memory_space=pl.ANY)
    return pl.pallas_call(
        functools.partial(_combine_norm_kernel, tp),
        grid=(b, tiles),
        in_specs=[pl.BlockSpec((1, 1, 2 * ROW_TILE), lambda i, j: (i * tiles + j, 0, 0),
                               memory_space=pltpu.SMEM),
                  any_spec, any_spec, any_spec, _resident((1, d))],
        out_specs=pl.BlockSpec((1, ROW_TILE, d), lambda i, j: (i, j, 0)),
        out_shape=jax.ShapeDtypeStruct((b, seq, d), F32),
        scratch_shapes=[pltpu.VMEM((ROW_TILE, d), F32), pltpu.VMEM((ROW_TILE, LANES), F32),
                        pltpu.VMEM((2, ROW_TILE // SUBLANES, SUBLANES, d), F32),
                        pltpu.SemaphoreType.DMA((2,)), pltpu.SemaphoreType.DMA(())],
        compiler_params=_cparams(2),
        name="moe_combine_norm",
    )(pos, h, info, ys, g)


def _routed_moe_norm(h, hn, info, cnt, wg, wu, wd, g, b, tp):
    n = h.shape[0]
    n_routed = b * (tp - Q_BLOCK)
    n_sorted = 2 * n_routed + N_EXPERTS * ROW_TILE
    n_tiles = n_sorted // ROW_TILE
    counts = cnt[0, :N_EXPERTS].astype(jnp.int32)
    tiles_e = (counts + ROW_TILE - 1) // ROW_TILE
    tile_end = jnp.cumsum(tiles_e)
    row_base = (tile_end - tiles_e) * ROW_TILE
    e_idx = info[:, 0:2].astype(jnp.int32)
    pos = row_base[e_idx] + info[:, 4:6].astype(jnp.int32)
    routed = (jnp.arange(n, dtype=jnp.int32) % tp >= Q_BLOCK)[:, None]
    zero_start = jnp.concatenate([row_base + counts, tile_end[-1:] * ROW_TILE])
    zero_end = jnp.concatenate([tile_end * ROW_TILE, jnp.full((1,), n_sorted + ROW_TILE, jnp.int32)])
    tok = _inverse_map(zero_start, zero_end,
                       jnp.where(routed, pos, n_sorted).reshape(n // ROW_TILE, 1, 2 * ROW_TILE),
                       n_sorted)[:n_sorted]
    tile_ids = jnp.arange(n_tiles, dtype=jnp.int32)
    tile_expert = jnp.minimum(jnp.sum((tile_ids[:, None] >= tile_end[None, :]).astype(jnp.int32), axis=1),
                              N_EXPERTS - 1)
    tile_valid = (tile_ids < tile_end[-1]).astype(jnp.int32)
    ys = _experts(tile_expert, tile_valid, tok.reshape(n_tiles, 1, ROW_TILE), hn, wg, wu, wd)
    pos_seq = pos.reshape(b, tp, 2)[:, Q_BLOCK:, :].reshape(-1, 1, 2 * ROW_TILE)
    return _combine_norm(pos_seq, h, info, ys, g, b, tp)


def _swiglu_kernel(h_ref, hn_ref, wg_ref, wu_ref, wd_ref, o_ref, acc_ref):
    o_ref[...] = h_ref[...] + _swiglu_tile(hn_ref[...], wg_ref, wu_ref, wd_ref, acc_ref)


def _swiglu(h, hn, wg, wu, wd):
    n = h.shape[0]
    row = pl.BlockSpec((ROW_TILE, D_MODEL), lambda i: (i, 0))
    return pl.pallas_call(
        _swiglu_kernel,
        grid=(n // ROW_TILE,),
        in_specs=[row, row, _resident(wg.shape), _resident(wu.shape), _resident(wd.shape)],
        out_specs=row,
        out_shape=jax.ShapeDtypeStruct((n, D_MODEL), F32),
        scratch_shapes=[pltpu.VMEM((ROW_TILE, D_MODEL), F32)],
        input_output_aliases={0: 0},
        compiler_params=_cparams(),
        name="swiglu",
    )(h, hn, wg, wu, wd)


def _group_ones(n, group_shift):
    r = lax.broadcasted_iota(jnp.int32, (n, n), 0) >> group_shift
    c = lax.broadcasted_iota(jnp.int32, (n, n), 1) >> group_shift
    return jnp.where(r == c, 1.0, 0.0).astype(BF16)


GLA_ROWS = 256


def _gla_kernel(tp, pa_ref, lr_ref, wg_ref, bg_ref, gn_ref, o_ref, s_ref):
    qhead = lax.broadcasted_iota(jnp.int32, (1, N_HEADS * GLA_DK), 1) >> 5
    ehead = lax.broadcasted_iota(jnp.int32, (1, N_HEADS * GLA_DV), 1) >> 6
    shead = lax.broadcasted_iota(jnp.int32, (N_HEADS * GLA_DK, 1), 0) >> 5
    bd = shead == ehead
    gsum = _group_ones(N_HEADS * GLA_DV, 6)
    wg_hi, wg_lo = _split_bf16(wg_ref[...])
    bg = bg_ref[...]
    gn = gn_ref[...]
    scale = GLA_DK ** -0.5
    s_ref[...] = jnp.zeros_like(s_ref)

    def block(r0, n_rows):
        n_c = n_rows // CHUNK
        rows = pl.ds(r0, n_rows)
        ri = lax.broadcasted_iota(jnp.int32, (n_rows, n_rows), 0)
        ci = lax.broadcasted_iota(jnp.int32, (n_rows, n_rows), 1)
        same = (ri >> 6) == (ci >> 6)
        tri_bf = jnp.where(jnp.logical_and(same, ri >= ci), 1.0, 0.0).astype(BF16)
        ones_bf = jnp.where(same, 1.0, 0.0).astype(BF16)
        r4 = lax.broadcasted_iota(jnp.int32, (N_HEADS * n_rows, n_rows), 0) & (n_rows - 1)
        c4 = lax.broadcasted_iota(jnp.int32, (N_HEADS * n_rows, n_rows), 1)
        tri4 = jnp.logical_and((r4 >> 6) == (c4 >> 6), r4 >= c4)
        col_chunk = lax.broadcasted_iota(jnp.int32, (1, n_rows), 1) >> 6

        q = pa_ref[0, rows, 0:128].astype(F32) * scale
        k = pa_ref[0, rows, 128:256].astype(F32)
        v = pa_ref[0, rows, 256:512]
        og = pa_ref[0, rows, 512:768].astype(F32)
        lr = lr_ref[0, rows, :]
        valid = (r0 + lax.broadcasted_iota(jnp.int32, (n_rows, 1), 0)) >= FRONT_PAD

        pre = _dot(lr, wg_hi) + _dot(lr, wg_lo) + bg
        logsig = jnp.minimum(pre, 0.0) - jnp.log1p(jnp.exp(-jnp.abs(pre)))
        g = jnp.where(valid, logsig * (1.0 / GLA_TAU), 0.0)
        g_hi, g_lo = _split_bf16(g)
        cum = _dot(tri_bf, g_hi) + _dot(tri_bf, g_lo)
        cum_end = _dot(ones_bf, g_hi) + _dot(ones_bf, g_lo)
        qt_bf = (q * jnp.exp(cum)).astype(BF16)
        kt = (k * jnp.exp(-cum)).astype(BF16)
        kd_t = (k * jnp.exp(cum_end - cum)).T.astype(BF16)
        dec_t = jnp.exp(cum_end.T)

        qs = jnp.concatenate([jnp.where(qhead == h, qt_bf, jnp.zeros_like(qt_bf))
                              for h in range(N_HEADS)], axis=0)
        a = jnp.where(tri4, _dot_nt(qs, kt), 0.0).astype(BF16)
        r = _dot(a, v)
        o = jnp.where(ehead == 0, r[0:n_rows, :], 0.0)
        for h in range(1, N_HEADS):
            o = o + jnp.where(ehead == h, r[h * n_rows:(h + 1) * n_rows, :], 0.0)

        s = s_ref[...]
        inter = []
        for c in range(n_c):
            inter.append(_dot(qt_bf[c * CHUNK:(c + 1) * CHUNK, :], s.astype(BF16)))
            upd = _dot(jnp.where(col_chunk == c, kd_t, jnp.zeros_like(kd_t)), v)
            s = dec_t[:, c * CHUNK:c * CHUNK + 1] * s + jnp.where(bd, upd, 0.0)
        s_ref[...] = s
        o = o + jnp.concatenate(inter, axis=0)

        ms = _dot((o * o).astype(BF16), gsum) * (1.0 / GLA_DV)
        y = o * lax.rsqrt(ms + EPS) * gn * (og * _sigmoid(og))
        o_ref[0, rows, :] = jnp.where(valid, y, 0.0).astype(BF16)

    block(0, Q_BLOCK)

    def body(i, carry):
        block(pl.multiple_of(Q_BLOCK + i * GLA_ROWS, Q_BLOCK), GLA_ROWS)
        return carry

    lax.fori_loop(0, (tp - Q_BLOCK) // GLA_ROWS, body, 0)


def _gla(pa, pc, wg, bg, gn):
    b, tp, _ = pa.shape
    return pl.pallas_call(
        functools.partial(_gla_kernel, tp),
        grid=(b,),
        in_specs=[pl.BlockSpec((1, tp, PA_W), lambda i: (i, 0, 0)),
                  pl.BlockSpec((1, tp, LANES), lambda i: (i, 0, 2)),
                  _resident((LANES, LANES)), _resident((1, LANES)), _resident((1, 256))],
        out_specs=pl.BlockSpec((1, tp, 256), lambda i: (i, 0, 0)),
        out_shape=jax.ShapeDtypeStruct((b, tp, 256), BF16),
        scratch_shapes=[pltpu.VMEM((N_HEADS * GLA_DK, N_HEADS * GLA_DV), F32)],
        compiler_params=_cparams(),
        name="gla",
    )(pa, pc, wg, bg, gn)


RET_BLOCK = 128


def _ret_kernel(tp, pb_ref, cos_ref, sin_ref, dmask_ref, qfac_ref, kfac_ref, dec_ref, gn_ref,
                o_ref, s_ref):
    blk = RET_BLOCK
    n_blocks = tp // blk
    qhead = (lax.broadcasted_iota(jnp.int32, (1, 256), 1) & 127) >> 5
    ehead = lax.broadcasted_iota(jnp.int32, (1, 256), 1) >> 6
    shead = (lax.broadcasted_iota(jnp.int32, (256, 1), 0) & 127) >> 5
    bd = shead == ehead
    gsum = _group_ones(256, 6)
    gn = gn_ref[...]
    s_ref[...] = jnp.zeros_like(s_ref)

    def rope(x, cos, sin):
        x1, x2 = x[:, :128], x[:, 128:]
        return jnp.concatenate([x1 * cos - x2 * sin, x1 * sin + x2 * cos], axis=1)

    def block(j, carry):
        r0 = pl.multiple_of(j * blk, blk)
        rows = pl.ds(r0, blk)
        cos = cos_ref[rows, :]
        sin = sin_ref[rows, :]
        q = rope(pb_ref[0, rows, 0:256].astype(F32), cos, sin)
        k = rope(pb_ref[0, rows, 256:512].astype(F32), cos, sin) * (RET_DK ** -0.5)
        v = pb_ref[0, rows, 512:768]
        og = pb_ref[0, rows, 768:1024].astype(F32)
        valid = (r0 + lax.broadcasted_iota(jnp.int32, (blk, 1), 0)) >= FRONT_PAD

        q_bf = q.astype(BF16)
        qs = jnp.concatenate([jnp.where(qhead == h, q_bf, jnp.zeros_like(q_bf))
                              for h in range(N_HEADS)], axis=0)
        a = (_dot_nt(qs, k.astype(BF16)) * dmask_ref[...]).astype(BF16)
        r = _dot(a, v)
        o = _dot((q * qfac_ref[...]).astype(BF16), s_ref[...].astype(BF16))
        for h in range(N_HEADS):
            o = o + jnp.where(ehead == h, r[h * blk:(h + 1) * blk, :], 0.0)

        kd_t = (k * kfac_ref[...]).T.astype(BF16)
        upd = _dot(kd_t, v)
        s_ref[...] = dec_ref[...] * s_ref[...] + jnp.where(bd, upd, 0.0)

        mu = _dot(o.astype(BF16), gsum) * (1.0 / 64)
        xc = o - mu
        var = _dot((xc * xc).astype(BF16), gsum) * (1.0 / 64)
        y = xc * lax.rsqrt(var + EPS) * gn * (og * _sigmoid(og))
        o_ref[0, rows, :] = jnp.where(valid, y, 0.0).astype(BF16)
        return carry

    lax.fori_loop(0, n_blocks, block, 0, unroll=4)


def _ret(pb, cos, sin, dmask, qfac, kfac, dec, gn):
    b, tp, _ = pb.shape
    return pl.pallas_call(
        functools.partial(_ret_kernel, tp),
        grid=(b,),
        in_specs=[pl.BlockSpec((1, tp, PB_W), lambda i: (i, 0, 0)),
                  _resident(cos.shape), _resident(sin.shape), _resident(dmask.shape),
                  _resident(qfac.shape), _resident(kfac.shape), _resident(dec.shape),
                  _resident((1, 256))],
        out_specs=pl.BlockSpec((1, tp, 256), lambda i: (i, 0, 0)),
        out_shape=jax.ShapeDtypeStruct((b, tp, 256), BF16),
        scratch_shapes=[pltpu.VMEM((256, 256), F32)],
        compiler_params=_cparams(),
        name="retention",
    )(pb, cos, sin, dmask, qfac, kfac, dec, gn)


ATT_ROWS = 256
LOG2E = 1.4426950408889634
V_ONE = 64


def _aligned(x, m):
    return x if isinstance(x, int) else pl.multiple_of(x, m)


def _for_chunks(n, body):
    if isinstance(n, int):
        for j in range(n):
            body(j)
        return

    def quad(t, carry):
        for u in range(4):
            body(4 * t + u)
        return carry

    lax.fori_loop(0, n >> 2, quad, 0)
    done = (n >> 2) << 2

    @pl.when((n & 2) != 0)
    def _():
        body(done)
        body(done + 1)

    @pl.when((n & 1) != 0)
    def _():
        body(done + (n & 2))


def _mask_groups(mask, s, n_g):
    rows = s.shape[0] // n_g
    return jnp.concatenate([jnp.where(mask, s[g * rows:(g + 1) * rows, :], NEG)
                            for g in range(n_g)], axis=0)


def _attn_block0(qk_fn, v_fn, n_g):
    rows = pl.ds(0, Q_BLOCK)
    qrow = lax.broadcasted_iota(jnp.int32, (Q_BLOCK, 1), 0)
    kcol = lax.broadcasted_iota(jnp.int32, (1, Q_BLOCK), 1)
    mask = jnp.logical_and(kcol <= qrow, kcol >= FRONT_PAD)
    s = _mask_groups(mask, qk_fn(0, Q_BLOCK, rows), n_g)
    p_bf = jnp.exp2(s - jnp.max(s, axis=-1, keepdims=True)).astype(BF16)
    pv = jnp.concatenate([_dot(p_bf[g * Q_BLOCK:(g + 1) * Q_BLOCK, :], v_fn(g, rows))
                          for g in range(n_g)], axis=0)
    return pv / pv[:, V_ONE:V_ONE + 1]


def _attn_row0(i):
    return _aligned(Q_BLOCK + (i - 1) * ATT_ROWS, Q_BLOCK)


def _attn_result(acc_s):
    acc = acc_s[...]
    return acc / acc[:, V_ONE:V_ONE + 1]


def _attn_block(i, qk_fn, v_fn, n_g, s_meta, s_s, mx_s, acc_s, mid_fn=None):
    rr = ATT_ROWS
    q0 = _attn_row0(i)
    meta_rows = pl.ds(0, Q_BLOCK)
    kcol = lax.broadcasted_iota(jnp.int32, (1, Q_BLOCK), 1)

    def key_rows(j):
        return pl.ds(_aligned(Q_BLOCK + j * rr, Q_BLOCK), rr)

    s = jnp.where(kcol >= FRONT_PAD, qk_fn(q0, rr, meta_rows), NEG)
    s_meta[...] = s
    mx_s[...] = s

    def pass1(j):
        sj = qk_fn(q0, rr, key_rows(j))
        s_s[j] = sj
        mx_s[...] = jnp.maximum(mx_s[...], jnp.maximum(sj[:, :LANES], sj[:, LANES:]))

    _for_chunks(i - 1, pass1)
    if mid_fn is not None:
        mid_fn()
    causal = (lax.broadcasted_iota(jnp.int32, (rr, rr), 1)
              <= lax.broadcasted_iota(jnp.int32, (rr, rr), 0))
    sd = _mask_groups(causal, qk_fn(q0, rr, pl.ds(q0, rr)), n_g)
    s_s[i - 1] = sd
    m = jnp.max(jnp.maximum(mx_s[...], jnp.maximum(sd[:, :LANES], sd[:, LANES:])),
                axis=-1, keepdims=True)
    mx_s[...] = jnp.broadcast_to(m, mx_s.shape)

    p_bf = jnp.exp2(s_meta[...] - mx_s[...]).astype(BF16)
    for g in range(n_g):
        acc_s[g * rr:(g + 1) * rr, :] = _dot(p_bf[g * rr:(g + 1) * rr, :], v_fn(g, meta_rows))

    def pass2(j):
        sj = s_s[j]
        mrep = mx_s[...]
        p0 = jnp.exp2(sj[:, :LANES] - mrep)
        p1 = jnp.exp2(sj[:, LANES:] - mrep)
        pj = jnp.concatenate([p0.astype(BF16), p1.astype(BF16)], axis=1)
        for g in range(n_g):
            acc_s[g * rr:(g + 1) * rr, :] += _dot(pj[g * rr:(g + 1) * rr, :], v_fn(g, key_rows(j)))

    _for_chunks(i, pass2)


def _attn_all_blocks(n_q, block_fn, finish_fn):
    block_fn(1, None)

    def body(i, carry):
        block_fn(i, lambda: finish_fn(i - 1))
        return carry

    lax.fori_loop(2, n_q + 1, body, 0)
    finish_fn(n_q)


def _heads_to_lanes(per_head):
    low = lax.broadcasted_iota(jnp.int32, (1, LANES), 1) < 64
    lo = jnp.where(low, per_head[0], pltpu.roll(per_head[1], 64, 1))
    hi = jnp.where(low, per_head[2], pltpu.roll(per_head[3], 64, 1))
    return jnp.concatenate([lo, hi], axis=1)


def _with_ones_lane(v):
    lane = lax.broadcasted_iota(jnp.int32, (1, LANES), 1)
    return jnp.where(lane == V_ONE, jnp.ones_like(v), v)


def _mla_kernel(tp, pc_ref, qn_ref, kvn_ref, wuq_ref, wukv_ref, cos_ref, sa_ref, sb_ref,
                o_ref, q_s, k_s, v_s, s_meta, s_s, mx_s, acc_s):
    scale = (MLA_NOPE + MLA_ROPE) ** -0.5 * LOG2E
    is_q = lax.broadcasted_iota(jnp.int32, (1, 256), 1) < MLA_Q_RANK

    def prep(r0, n_rows):
        rows = pl.ds(r0, n_rows)
        x = pc_ref[0, rows, 0:256].astype(F32)
        x2 = x * x
        ms_q = jnp.sum(jnp.where(is_q, x2, 0.0), axis=-1, keepdims=True) * (1.0 / MLA_Q_RANK)
        ms_kv = jnp.sum(jnp.where(is_q, 0.0, x2), axis=-1, keepdims=True) * (1.0 / MLA_KV_RANK)
        yq = (x * lax.rsqrt(ms_q + EPS) * qn_ref[...]).astype(BF16)
        ykv = (x * lax.rsqrt(ms_kv + EPS) * kvn_ref[...]).astype(BF16)
        cq = _dot(yq, wuq_ref[...])
        kv = _dot(ykv, wukv_ref[...])
        cos = cos_ref[rows, :]
        sa = sa_ref[rows, :]
        sb = sb_ref[rows, :]

        def rope(t):
            return t * cos + pltpu.roll(t, 16, 1) * sa + pltpu.roll(t, LANES - 16, 1) * sb

        kpe_in = pc_ref[0, rows, 256:384].astype(F32)
        kpe = rope(jnp.where(lax.broadcasted_iota(jnp.int32, (1, LANES), 1) >= 64, kpe_in, 0.0))
        for h in range(N_HEADS):
            q_s[h, rows, :] = (rope(cq[:, h * LANES:(h + 1) * LANES]) * scale).astype(BF16)
            k_s[h, rows, :] = (kv[:, h * LANES:(h + 1) * LANES] + kpe).astype(BF16)
            v_s[h, rows, :] = _with_ones_lane(
                kv[:, (N_HEADS + h) * LANES:(N_HEADS + h + 1) * LANES]).astype(BF16)

    prep(0, Q_BLOCK)

    def prep_body(i, carry):
        prep(pl.multiple_of(Q_BLOCK + i * ATT_ROWS, Q_BLOCK), ATT_ROWS)
        return carry

    lax.fori_loop(0, (tp - Q_BLOCK) // ATT_ROWS, prep_body, 0)

    def qk_fn(q0, n_rows, krows):
        return jnp.concatenate([_dot_nt(q_s[h, pl.ds(q0, n_rows), :], k_s[h, krows, :])
                                for h in range(N_HEADS)], axis=0)

    def v_fn(h, krows):
        return v_s[h, krows, :]

    def emit(q0, n_rows, o, first):
        y = _heads_to_lanes([o[h * n_rows:(h + 1) * n_rows, :] for h in range(N_HEADS)])
        if first:
            qrow = lax.broadcasted_iota(jnp.int32, (n_rows, 1), 0)
            y = jnp.where(qrow >= FRONT_PAD, y, 0.0)
        o_ref[0, pl.ds(q0, n_rows), :] = y.astype(BF16)

    emit(0, Q_BLOCK, _attn_block0(qk_fn, v_fn, N_HEADS), True)

    def block(i, mid_fn):
        _attn_block(i, qk_fn, v_fn, N_HEADS, s_meta, s_s, mx_s, acc_s, mid_fn)

    def finish(i):
        emit(_attn_row0(i), ATT_ROWS, _attn_result(acc_s), False)

    _attn_all_blocks((tp - Q_BLOCK) // ATT_ROWS, block, finish)


def _attn_scratch(n_g, tp):
    g_rows = n_g * ATT_ROWS
    n_slots = (tp - Q_BLOCK) // ATT_ROWS
    return [pltpu.VMEM((g_rows, LANES), F32),
            pltpu.VMEM((n_slots, g_rows, ATT_ROWS), F32),
            pltpu.VMEM((g_rows, LANES), F32),
            pltpu.VMEM((g_rows, LANES), F32)]


def _mla(pc, qn, kvn, wuq, wukv, cos, sa, sb):
    b, tp, _ = pc.shape
    return pl.pallas_call(
        functools.partial(_mla_kernel, tp),
        grid=(b,),
        in_specs=[pl.BlockSpec((1, tp, PC_W), lambda i: (i, 0, 0)),
                  _resident((1, 256)), _resident((1, 256)),
                  _resident(wuq.shape), _resident(wukv.shape),
                  _resident(cos.shape), _resident(sa.shape), _resident(sb.shape)],
        out_specs=pl.BlockSpec((1, tp, 256), lambda i: (i, 0, 0)),
        out_shape=jax.ShapeDtypeStruct((b, tp, 256), BF16),
        scratch_shapes=[pltpu.VMEM((N_HEADS, tp, LANES), BF16),
                        pltpu.VMEM((N_HEADS, tp, LANES), BF16),
                        pltpu.VMEM((N_HEADS, tp, LANES), BF16)] + _attn_scratch(N_HEADS, tp),
        compiler_params=_cparams(),
        name="mla",
    )(pc, qn, kvn, wuq, wukv, cos, sa, sb)


def _diff_kernel(tp, lam_init, pd_ref, lam_ref, dn_ref, o_ref, qs_s, v_s, s_meta, s_s, mx_s, acc_s):
    n_maps = 2 * N_HEADS

    def fill_values(i, carry):
        rows = pl.ds(pl.multiple_of(i * Q_BLOCK, Q_BLOCK), Q_BLOCK)
        low = lax.broadcasted_iota(jnp.int32, (1, LANES), 1) < V_ONE
        for pair in range(N_HEADS // 2):
            two = pd_ref[0, rows, 512 + pair * LANES:512 + (pair + 1) * LANES].astype(F32)
            for h, vals in ((2 * pair, two), (2 * pair + 1, pltpu.roll(two, V_ONE, 1))):
                v_s[h, rows, :] = _with_ones_lane(jnp.where(low, vals, 0.0)).astype(BF16)
        return carry

    lax.fori_loop(0, tp // Q_BLOCK, fill_values, 0)
    scale = DIFF_DK ** -0.5 * LOG2E
    group = lax.broadcasted_iota(jnp.int32, (1, 256), 1) >> 5
    lv = lam_ref[...]
    lam = (jnp.exp(jnp.sum(lv[0:1, :] * lv[1:2, :], axis=-1, keepdims=True))
           - jnp.exp(jnp.sum(lv[2:3, :] * lv[3:4, :], axis=-1, keepdims=True)) + lam_init)
    dn = dn_ref[...]

    def stack_queries(q0, n_rows):
        q = (pd_ref[0, pl.ds(q0, n_rows), 0:256].astype(F32) * scale).astype(BF16)
        for g in range(n_maps):
            qs_s[g * n_rows:(g + 1) * n_rows, :] = jnp.where(group == g, q, jnp.zeros_like(q))

    def qk_fn(q0, n_rows, krows):
        return _dot_nt(qs_s[0:n_maps * n_rows, :], pd_ref[0, krows, 256:512])

    def v_fn(g, krows):
        return v_s[g // 2, krows, :]

    def emit(q0, n_rows, o, first):
        od = jnp.concatenate([o[(2 * h) * n_rows:(2 * h + 1) * n_rows, :]
                              - lam * o[(2 * h + 1) * n_rows:(2 * h + 2) * n_rows, :]
                              for h in range(N_HEADS)], axis=0)
        od = jnp.where(lax.broadcasted_iota(jnp.int32, (1, LANES), 1) < V_ONE, od, 0.0)
        ms = jnp.sum(od * od, axis=-1, keepdims=True) * (1.0 / 64)
        yh = od * lax.rsqrt(ms + EPS) * dn * (1.0 - lam_init)
        y = _heads_to_lanes([yh[h * n_rows:(h + 1) * n_rows, :] for h in range(N_HEADS)])
        if first:
            qrow = lax.broadcasted_iota(jnp.int32, (n_rows, 1), 0)
            y = jnp.where(qrow >= FRONT_PAD, y, 0.0)
        o_ref[0, pl.ds(q0, n_rows), :] = y.astype(BF16)

    stack_queries(0, Q_BLOCK)
    emit(0, Q_BLOCK, _attn_block0(qk_fn, v_fn, n_maps), True)

    def block(i, mid_fn):
        stack_queries(_attn_row0(i), ATT_ROWS)
        _attn_block(i, qk_fn, v_fn, n_maps, s_meta, s_s, mx_s, acc_s, mid_fn)

    def finish(i):
        emit(_attn_row0(i), ATT_ROWS, _attn_result(acc_s), False)

    _attn_all_blocks((tp - Q_BLOCK) // ATT_ROWS, block, finish)


def _diff(pd, lam_rows, dn, lam_init):
    b, tp, _ = pd.shape
    n_maps = 2 * N_HEADS
    return pl.pallas_call(
        functools.partial(_diff_kernel, tp, lam_init),
        grid=(b,),
        in_specs=[pl.BlockSpec((1, tp, PD_W), lambda i: (i, 0, 0)),
                  _resident(lam_rows.shape), _resident((1, LANES))],
        out_specs=pl.BlockSpec((1, tp, 256), lambda i: (i, 0, 0)),
        out_shape=jax.ShapeDtypeStruct((b, tp, 256), BF16),
        scratch_shapes=[pltpu.VMEM((n_maps * ATT_ROWS, 256), BF16),
                        pltpu.VMEM((N_HEADS, tp, LANES), BF16)] + _attn_scratch(n_maps, tp),
        compiler_params=_cparams(),
        name="diffattn",
    )(pd, lam_rows, dn)


def _pad_cols(x, width):
    return jnp.pad(x, ((0, 0), (0, width - x.shape[1])))


def _rot_split(w):
    d = w.shape[0]
    return w.reshape(d, N_HEADS, 2, 32).transpose(0, 2, 1, 3).reshape(d, 256)


def _layout_w_in(w):
    sizes = (128, 128, 256, 16, 256, 256, 256, 256, 256, 192, 64, 32, 256, 256, 256)
    offs = [0]
    for s_ in sizes:
        offs.append(offs[-1] + s_)
    seg = [w[:, offs[i]:offs[i + 1]] for i in range(len(sizes))]
    (a_q, a_k, a_v, a_lr, a_og, r_q, r_k, r_v, r_og, c_cq, c_ckv, c_kpe, d_q, d_k, d_v) = seg
    d = w.shape[0]
    z = lambda n: jnp.zeros((d, n), w.dtype)
    cols = [a_q, a_k, a_v, a_og,
            _rot_split(r_q), _rot_split(r_k), r_v, r_og,
            c_cq, c_ckv, a_lr, z(48), c_kpe, z(32),
            d_q, d_k, d_v]
    return jnp.concatenate(cols, axis=1).astype(BF16)


def _tables(tp):
    pos = jnp.arange(tp, dtype=F32) - FRONT_PAD
    inv = ROPE_THETA ** (-jnp.arange(32, dtype=F32) / 32)
    ang = pos[:, None] * inv[None, :]
    ret_cos = jnp.tile(jnp.cos(ang), (1, N_HEADS))
    ret_sin = jnp.tile(jnp.sin(ang), (1, N_HEADS))
    inv16 = ROPE_THETA ** (-jnp.arange(16, dtype=F32) / 16)
    ang16 = pos[:, None] * inv16[None, :]
    c16, s16 = jnp.cos(ang16), jnp.sin(ang16)
    one = lambda n: jnp.ones((tp, n), F32)
    zero = lambda n: jnp.zeros((tp, n), F32)
    mla_cos = jnp.concatenate([one(64), c16, c16, one(32)], axis=1)
    mla_sa = jnp.concatenate([zero(80), s16, zero(32)], axis=1)
    mla_sb = jnp.concatenate([zero(64), -s16, zero(48)], axis=1)
    lg = jnp.log(1.0 - jnp.exp2(-5.0 - jnp.arange(N_HEADS, dtype=F32)))
    idx = jnp.arange(RET_BLOCK, dtype=F32)
    rel = idx[:, None] - idx[None, :]
    dmask = jnp.where(rel[None] >= 0, jnp.exp(rel[None] * lg[:, None, None]), 0.0)
    dmask = dmask.reshape(N_HEADS * RET_BLOCK, RET_BLOCK)
    lane_head = (jnp.arange(256) % 128) // 32
    qfac = jnp.exp((idx[:, None] + 1.0) * lg[lane_head][None, :])
    kfac = jnp.exp((RET_BLOCK - 1.0 - idx[:, None]) * lg[lane_head][None, :])
    dec = jnp.exp(RET_BLOCK * lg[lane_head])[:, None]
    return ret_cos, ret_sin, mla_cos, mla_sa, mla_sb, dmask, qfac, kfac, dec


def kernel(x, meta_tokens, attn_norm, w_in, gla_w_gate, gla_b_gate, gla_norm, ret_norm, mla_q_norm, mla_w_uq, mla_kv_norm, mla_w_ukv, diff_lambda, diff_norm, w_out, ffn_norm, ffn_w_gate, ffn_w_up, ffn_w_down, moe_router, moe_w_gate, moe_w_up, moe_w_down, final_norm):
    b, seq, d = x.shape
    tp = FRONT_PAD + N_META + seq
    n = b * tp
    meta = jnp.broadcast_to(meta_tokens[None].astype(x.dtype), (b, N_META, d))
    h = jnp.concatenate([jnp.zeros((b, FRONT_PAD, d), x.dtype), meta, x], axis=1).reshape(n, d)
    ret_cos, ret_sin, mla_cos, mla_sa, mla_sb, dmask, qfac, kfac, dec = _tables(tp)

    for li in range(DEPTH):
        pa, pb, pc, pd = _inproj(h, attn_norm[li][None, :], _layout_w_in(w_in[li]))
        pa, pb, pc, pd = (p.reshape(b, tp, -1) for p in (pa, pb, pc, pd))

        wgate = jnp.pad(gla_w_gate[li], ((0, LANES - GLA_GATE_RANK), (0, 0)))
        o_a = _gla(pa, pc, wgate, gla_b_gate[li][None, :], jnp.tile(gla_norm[li], N_HEADS)[None, :])
        o_b = _ret(pb, ret_cos, ret_sin, dmask, qfac, kfac, dec, jnp.tile(ret_norm[li], N_HEADS)[None, :])

        qn = _pad_cols(mla_q_norm[li][None, :], 256)
        kvn = jnp.pad(mla_kv_norm[li][None, :], ((0, 0), (MLA_Q_RANK, 0)))
        wuq = jnp.pad(mla_w_uq[li].reshape(MLA_Q_RANK, N_HEADS, MLA_NOPE + MLA_ROPE),
                      ((0, 256 - MLA_Q_RANK), (0, 0), (0, LANES - MLA_NOPE - MLA_ROPE)))
        wuq = wuq.reshape(256, N_HEADS * LANES).astype(BF16)
        wukv = mla_w_ukv[li].reshape(MLA_KV_RANK, N_HEADS, 2, 64)
        wukv = jnp.pad(wukv, ((MLA_Q_RANK, 0), (0, 0), (0, 0), (0, 64)))
        wukv = wukv.transpose(0, 2, 1, 3).reshape(256, 2 * N_HEADS * LANES).astype(BF16)
        o_c = _mla(pc, qn, kvn, wuq, wukv, mla_cos, mla_sa, mla_sb)

        lam_init = 0.8 - 0.6 * math.exp(-0.3 * li)
        o_d = _diff(pd, diff_lambda[li], _pad_cols(diff_norm[li][None, :], LANES), lam_init)

        o_a, o_b, o_c, o_d = (o.reshape(n, 256) for o in (o_a, o_b, o_c, o_d))
        wo = w_out[li].astype(BF16)
        fn = ffn_norm[li][None, :]
        j = li // 2
        if li % 2 == 0:
            h, hn = _outproj(h, o_a, o_b, o_c, o_d, wo, fn)
            h = _swiglu(h, hn, ffn_w_gate[j].astype(BF16), ffn_w_up[j].astype(BF16),
                        ffn_w_down[j].astype(BF16))
        else:
            h, hn, info, cnt = _outproj(h, o_a, o_b, o_c, o_d, wo, fn,
                                        router=_pad_cols(moe_router[j], LANES), seq_rows=tp)
            return _routed_moe_norm(h, hn, info, cnt, moe_w_gate[j].astype(BF16),
                                    moe_w_up[j].astype(BF16), moe_w_down[j].astype(BF16),
                                    final_norm[None, :], b, tp)
```

```python
import functools
import math

import jax
import jax.numpy as jnp
from jax import lax
from jax.experimental import pallas as pl
from jax.experimental.pallas import tpu as pltpu

F32 = jnp.float32
BF16 = jnp.bfloat16

D_MODEL = 1024
DEPTH = 2
N_META = 16
CHUNK = 64
Q_BLOCK = 128
FRONT_PAD = Q_BLOCK - N_META
EPS = 1e-6
NEG = -1e30
ROPE_THETA = 10000.0
N_HEADS = 4
GLA_DK = 32
GLA_DV = 64
GLA_GATE_RANK = 16
GLA_TAU = 16.0
RET_DK = 64
MLA_Q_RANK = 192
MLA_KV_RANK = 64
MLA_NOPE = 64
MLA_ROPE = 32
DIFF_DK = 32
N_EXPERTS = 8

LANES = 128
SUBLANES = 8
ROW_TILE = 512
FF_CHUNK = 256
PREFETCH_TAIL_CHUNKS = 4
VMEM_LIMIT = 56 * 1024 * 1024

PA_W = 768
PB_W = 1024
PC_W = 384
PD_W = 768


def _cparams(n_axes=1):
    return pltpu.CompilerParams(dimension_semantics=("arbitrary",) * n_axes,
                                vmem_limit_bytes=VMEM_LIMIT)


def _resident(shape):
    nd = len(shape)
    return pl.BlockSpec(shape, lambda *_: (0,) * nd, pipeline_mode=pl.Buffered(1))


def _sigmoid(x):
    return 1.0 / (1.0 + jnp.exp(-x))


def _split_bf16(x):
    hi = x.astype(BF16)
    lo = (x - hi.astype(F32)).astype(BF16)
    return hi, lo


def _dot(a, b):
    return jnp.dot(a, b, preferred_element_type=F32)


def _dot_nt(a, b):
    return lax.dot_general(a, b, (((1,), (1,)), ((), ())), preferred_element_type=F32)


def _inproj_kernel(h_ref, g_ref, w_ref, pa_ref, pb_ref, pc_ref, pd_ref):
    x = h_ref[...]
    ms = jnp.mean(x * x, axis=-1, keepdims=True)
    y = (x * lax.rsqrt(ms + EPS) * g_ref[...]).astype(BF16)
    off = 0
    for o_ref, width in ((pa_ref, PA_W), (pb_ref, PB_W), (pc_ref, PC_W), (pd_ref, PD_W)):
        o_ref[...] = _dot(y, w_ref[:, off:off + width]).astype(BF16)
        off += width


def _inproj(h, g, w):
    n = h.shape[0]
    wtot = PA_W + PB_W + PC_W + PD_W
    row = lambda width: pl.BlockSpec((ROW_TILE, width), lambda i: (i, 0))
    return pl.pallas_call(
        _inproj_kernel,
        grid=(n // ROW_TILE,),
        in_specs=[row(D_MODEL), _resident((1, D_MODEL)), _resident((D_MODEL, wtot))],
        out_specs=[row(PA_W), row(PB_W), row(PC_W), row(PD_W)],
        out_shape=[jax.ShapeDtypeStruct((n, w_), BF16) for w_ in (PA_W, PB_W, PC_W, PD_W)],
        compiler_params=_cparams(),
        name="inproj",
    )(h, g, w)


def _outproj_kernel(router_seq_rows, h_ref, oa_ref, ob_ref, oc_ref, od_ref, wo_ref, fn_ref, *rest):
    with_router = router_seq_rows > 0
    if with_router:
        router_ref, below_ref, hmid_ref, hn_ref, info_ref, cnt_ref, carry_ref = rest
    else:
        hmid_ref, hn_ref = rest
    o = jnp.concatenate([oa_ref[...], ob_ref[...], oc_ref[...], od_ref[...]], axis=1)
    hm = h_ref[...] + _dot(o, wo_ref[...])
    hmid_ref[...] = hm
    ms = jnp.mean(hm * hm, axis=-1, keepdims=True)
    y = hm * lax.rsqrt(ms + EPS) * fn_ref[...]
    hn_ref[...] = y.astype(hn_ref.dtype)
    if with_router:
        y_hi, y_lo = _split_bf16(y)
        r_hi, r_lo = _split_bf16(router_ref[...])
        logits = _dot(y_hi, r_hi) + _dot(y_hi, r_lo) + _dot(y_lo, r_hi)
        lane = lax.broadcasted_iota(jnp.int32, logits.shape, 1).astype(F32)
        ninf = float("-inf")
        logits = jnp.where(lane < N_EXPERTS, logits, ninf)
        m1 = jnp.max(logits, axis=-1, keepdims=True)
        i1 = jnp.min(jnp.where(logits == m1, lane, float(LANES)), axis=-1, keepdims=True)
        rest_l = jnp.where(lane == i1, ninf, logits)
        m2 = jnp.max(rest_l, axis=-1, keepdims=True)
        i2 = jnp.min(jnp.where(rest_l == m2, lane, float(LANES)), axis=-1, keepdims=True)
        e2 = jnp.exp(m2 - m1)
        den = 1.0 + e2
        @pl.when(pl.program_id(0) == 0)
        def _():
            carry_ref[...] = jnp.zeros_like(carry_ref)
        sel = jnp.where(lane == i1, 1.0, 0.0) + jnp.where(lane == i2, 1.0, 0.0)
        rows = sel.shape[0]
        seq_pos = (lax.rem(pl.program_id(0) * rows, router_seq_rows)
                   + lax.broadcasted_iota(jnp.int32, (rows, 1), 0))
        seq_pos = jnp.where(seq_pos >= router_seq_rows, seq_pos - router_seq_rows, seq_pos)
        sel = jnp.where(seq_pos >= Q_BLOCK, sel, 0.0)
        count = _dot(below_ref[...], sel.astype(BF16)) + carry_ref[...]
        r1 = jnp.sum(jnp.where(lane == i1, count, 0.0), axis=-1, keepdims=True)
        r2 = jnp.sum(jnp.where(lane == i2, count, 0.0), axis=-1, keepdims=True)
        total = carry_ref[...] + jnp.sum(sel, axis=0, keepdims=True)
        carry_ref[...] = total
        cnt_ref[...] = jnp.broadcast_to(total, cnt_ref.shape)
        info = jnp.where(lane == 0, i1, 0.0)
        for k, val in enumerate((i2, 1.0 / den, e2 / den, r1, r2), start=1):
            info = jnp.where(lane == k, val, info)
        info_ref[...] = info


def _outproj(h, oa, ob, oc, od, wo, fn, router=None, seq_rows=0):
    n = h.shape[0]
    row = lambda width: pl.BlockSpec((ROW_TILE, width), lambda i: (i, 0))
    in_specs = [row(D_MODEL), row(256), row(256), row(256), row(256),
                _resident((D_MODEL, D_MODEL)), _resident((1, D_MODEL))]
    out_specs = [row(D_MODEL), row(D_MODEL)]
    out_shape = [jax.ShapeDtypeStruct((n, D_MODEL), F32),
                 jax.ShapeDtypeStruct((n, D_MODEL), BF16 if router is None else F32)]
    args = [h, oa, ob, oc, od, wo, fn]
    scratch = []
    if router is not None:
        in_specs += [_resident((D_MODEL, LANES)), _resident((ROW_TILE, ROW_TILE))]
        out_specs += [row(LANES), pl.BlockSpec((8, LANES), lambda i: (0, 0))]
        out_shape += [jax.ShapeDtypeStruct((n, LANES), F32), jax.ShapeDtypeStruct((8, LANES), F32)]
        args += [router, jnp.tri(ROW_TILE, k=-1, dtype=BF16)]
        scratch.append(pltpu.VMEM((1, LANES), F32))
    return pl.pallas_call(
        functools.partial(_outproj_kernel, seq_rows if router is not None else 0),
        grid=(n // ROW_TILE,),
        in_specs=in_specs, out_specs=out_specs, out_shape=out_shape,
        scratch_shapes=scratch,
        compiler_params=_cparams(),
        name="outproj_router" if router is not None else "outproj",
    )(*args)


def _inverse_map_kernel(zs_ref, ze_ref, pos_ref, tok_ref):
    i = pl.program_id(0)

    @pl.when(i == 0)
    def _():
        def zero(j, carry):
            tok_ref[j] = 0
            return carry

        for k in range(zs_ref.shape[0]):
            lax.fori_loop(zs_ref[k], ze_ref[k], zero, 0)

    base = i * ROW_TILE

    def body(r, carry):
        tok_ref[pos_ref[0, 0, 2 * r]] = base + r
        tok_ref[pos_ref[0, 0, 2 * r + 1]] = base + r
        return carry

    lax.fori_loop(0, ROW_TILE, body, 0, unroll=8)


def _inverse_map(zero_start, zero_end, pos, n_sorted):
    return pl.pallas_call(
        _inverse_map_kernel,
        grid_spec=pltpu.PrefetchScalarGridSpec(
            num_scalar_prefetch=2,
            grid=(pos.shape[0],),
            in_specs=[pl.BlockSpec((1, 1, 2 * ROW_TILE), lambda i, zs, ze: (i, 0, 0),
                                   memory_space=pltpu.SMEM)],
            out_specs=pl.BlockSpec(memory_space=pltpu.SMEM)),
        out_shape=jax.ShapeDtypeStruct((n_sorted + ROW_TILE,), jnp.int32),
        compiler_params=_cparams(),
        name="moe_inverse_map",
    )(zero_start, zero_end, pos)


def _swiglu_tile(x, wg, wu, wd, acc_ref, per_chunk=None):
    n_chunks = wd.shape[0] // FF_CHUNK
    for c in range(n_chunks):
        cols = slice(c * FF_CHUNK, (c + 1) * FF_CHUNK)
        g = _dot(x, wg[:, cols])
        u = _dot(x, wu[:, cols])
        part = _dot((g * _sigmoid(g) * u).astype(BF16), wd[cols, :])
        if c == 0:
            acc_ref[...] = part
        elif c < n_chunks - 1:
            acc_ref[...] += part
        if per_chunk is not None:
            per_chunk(c, n_chunks)
    return acc_ref[...] + part


def _experts_kernel(te_ref, tv_ref, tok_ref, tok_next_ref, hn_ref, wg_ref, wu_ref, wd_ref, ys_ref,
                    x_buf, acc_ref, sem):
    i = pl.program_id(0)
    last = pl.num_programs(0) - 1
    slot = i & 1
    n_blocks = x_buf.shape[1]

    def row_copy(t_ref, s, rb, u):
        tok = t_ref[0, 0, rb * SUBLANES + u]
        return pltpu.make_async_copy(hn_ref.at[pl.ds(tok, 1), :], x_buf.at[s, rb, pl.ds(u, 1), :],
                                     sem.at[s])

    def wait_tile(s):
        pltpu.make_async_copy(x_buf.at[s], x_buf.at[s], sem.at[s]).wait()

    @pl.when(i == 0)
    def _():
        def start_block(rb, carry):
            for u in range(SUBLANES):
                row_copy(tok_ref, 0, rb, u).start(priority=u % 2)
            return carry

        lax.fori_loop(0, n_blocks, start_block, 0)

    @pl.when(jnp.logical_or(i == 0, tv_ref[jnp.maximum(i - 1, 0)] > 0))
    def _():
        wait_tile(slot)

    @pl.when(tv_ref[i] > 0)
    def _():
        def prefetch(c, n_chunks):
            per = -(-ROW_TILE // max(n_chunks - PREFETCH_TAIL_CHUNKS, 1))
            for r in range(c * per, min((c + 1) * per, ROW_TILE)):
                row_copy(tok_next_ref, 1 - slot, r // SUBLANES, r % SUBLANES).start(priority=r % 2)

        x = x_buf[slot].reshape(ROW_TILE, -1).astype(BF16)
        ys_ref[...] = _swiglu_tile(x, wg_ref.at[0], wu_ref.at[0], wd_ref.at[0], acc_ref, prefetch)

    @pl.when(tv_ref[i] == 0)
    def _():
        ys_ref[...] = jnp.zeros_like(ys_ref)

    @pl.when(jnp.logical_and(i == last, tv_ref[i] > 0))
    def _():
        wait_tile(1 - slot)


def _experts(tile_expert, tile_valid, tok, hn, wg, wu, wd):
    n_tiles = tok.shape[0]
    d = hn.shape[1]
    wspec = lambda shape: pl.BlockSpec((1,) + shape[1:], lambda i, te, tv: (te[i], 0, 0),
                                       pipeline_mode=pl.Buffered(1))
    tok_spec = lambda index: pl.BlockSpec((1, 1, ROW_TILE), index, memory_space=pltpu.SMEM)
    return pl.pallas_call(
        _experts_kernel,
        grid_spec=pltpu.PrefetchScalarGridSpec(
            num_scalar_prefetch=2,
            grid=(n_tiles,),
            in_specs=[tok_spec(lambda i, te, tv: (i, 0, 0)),
                      tok_spec(lambda i, te, tv: (jnp.minimum(i + 1, n_tiles - 1), 0, 0)),
                      pl.BlockSpec(memory_space=pl.ANY),
                      wspec(wg.shape), wspec(wu.shape), wspec(wd.shape)],
            out_specs=pl.BlockSpec((ROW_TILE, d), lambda i, te, tv: (i, 0)),
            scratch_shapes=[pltpu.VMEM((2, ROW_TILE // SUBLANES, SUBLANES, d), F32),
                            pltpu.VMEM((ROW_TILE, d), F32),
                            pltpu.SemaphoreType.DMA((2,))]),
        out_shape=jax.ShapeDtypeStruct((n_tiles * ROW_TILE, d), F32),
        compiler_params=_cparams(),
        name="moe_experts",
    )(tile_expert, tile_valid, tok, tok, hn, wg, wu, wd)


def _combine_norm_kernel(tp, pos_ref, h_ref, info_ref, ys_ref, g_ref, o_ref,
                         h_buf, info_buf, y_buf, sem, row_sem):
    n_rows = h_buf.shape[0]
    row0 = pl.multiple_of(pl.program_id(0) * tp + Q_BLOCK + pl.program_id(1) * n_rows, Q_BLOCK)
    h_copy = pltpu.make_async_copy(h_ref.at[pl.ds(row0, n_rows), :], h_buf, sem.at[0])
    info_copy = pltpu.make_async_copy(info_ref.at[pl.ds(row0, n_rows), :], info_buf, sem.at[1])
    h_copy.start()
    info_copy.start()

    def row_copy(rb, u, k):
        p = pos_ref[0, 0, rb * (2 * SUBLANES) + 2 * u + k]
        return pltpu.make_async_copy(ys_ref.at[pl.ds(p, 1), :], y_buf.at[k, rb, pl.ds(u, 1), :], row_sem)

    def start(rb, carry):
        for u in range(SUBLANES):
            row_copy(rb, u, 0).start(priority=0)
            row_copy(rb, u, 1).start(priority=1)
        return carry

    lax.fori_loop(0, n_rows // SUBLANES, start, 0)
    pltpu.make_async_copy(y_buf, y_buf, row_sem).wait()
    h_copy.wait()
    info_copy.wait()
    info = info_buf[...]
    y0 = y_buf[0].reshape(n_rows, -1)
    y1 = y_buf[1].reshape(n_rows, -1)
    x = h_buf[...] + info[:, 2:3] * y0 + info[:, 3:4] * y1
    ms = jnp.mean(x * x, axis=-1, keepdims=True)
    o_ref[0] = x * lax.rsqrt(ms + EPS) * g_ref[...]


def _combine_norm(pos, h, info, ys, g, b, tp):
    n, d = h.shape
    seq = tp - Q_BLOCK
    tiles = seq // ROW_TILE
    any_spec = pl.BlockSpec(memory_space=pl.ANY)
    return pl.pallas_call(
        functools.partial(_combine_norm_kernel, tp),
        grid=(b, tiles),
        in_specs=[pl.BlockSpec((1, 1, 2 * ROW_TILE), lambda i, j: (i * tiles + j, 0, 0),
                               memory_space=pltpu.SMEM),
                  any_spec, any_spec, any_spec, _resident((1, d))],
        out_specs=pl.BlockSpec((1, ROW_TILE, d), lambda i, j: (i, j, 0)),
        out_shape=jax.ShapeDtypeStruct((b, seq, d), F32),
        scratch_shapes=[pltpu.VMEM((ROW_TILE, d), F32), pltpu.VMEM((ROW_TILE, LANES), F32),
                        pltpu.VMEM((2, ROW_TILE // SUBLANES, SUBLANES, d), F32),
                        pltpu.SemaphoreType.DMA((2,)), pltpu.SemaphoreType.DMA(())],
        compiler_params=_cparams(2),
        name="moe_combine_norm",
    )(pos, h, info, ys, g)


def _routed_moe_norm(h, hn, info, cnt, wg, wu, wd, g, b, tp):
    n = h.shape[0]
    n_routed = b * (tp - Q_BLOCK)
    n_sorted = 2 * n_routed + N_EXPERTS * ROW_TILE
    n_tiles = n_sorted // ROW_TILE
    counts = cnt[0, :N_EXPERTS].astype(jnp.int32)
    tiles_e = (counts + ROW_TILE - 1) // ROW_TILE
    tile_end = jnp.cumsum(tiles_e)
    row_base = (tile_end - tiles_e) * ROW_TILE
    e_idx = info[:, 0:2].astype(jnp.int32)
    pos = row_base[e_idx] + info[:, 4:6].astype(jnp.int32)
    routed = (jnp.arange(n, dtype=jnp.int32) % tp >= Q_BLOCK)[:, None]
    zero_start = jnp.concatenate([row_base + counts, tile_end[-1:] * ROW_TILE])
    zero_end = jnp.concatenate([tile_end * ROW_TILE, jnp.full((1,), n_sorted + ROW_TILE, jnp.int32)])
    tok = _inverse_map(zero_start, zero_end,
                       jnp.where(routed, pos, n_sorted).reshape(n // ROW_TILE, 1, 2 * ROW_TILE),
                       n_sorted)[:n_sorted]
    tile_ids = jnp.arange(n_tiles, dtype=jnp.int32)
    tile_expert = jnp.minimum(jnp.sum((tile_ids[:, None] >= tile_end[None, :]).astype(jnp.int32), axis=1),
                              N_EXPERTS - 1)
    tile_valid = (tile_ids < tile_end[-1]).astype(jnp.int32)
    ys = _experts(tile_expert, tile_valid, tok.reshape(n_tiles, 1, ROW_TILE), hn, wg, wu, wd)
    pos_seq = pos.reshape(b, tp, 2)[:, Q_BLOCK:, :].reshape(-1, 1, 2 * ROW_TILE)
    return _combine_norm(pos_seq, h, info, ys, g, b, tp)


def _swiglu_kernel(h_ref, hn_ref, wg_ref, wu_ref, wd_ref, o_ref, acc_ref):
    o_ref[...] = h_ref[...] + _swiglu_tile(hn_ref[...], wg_ref, wu_ref, wd_ref, acc_ref)


def _swiglu(h, hn, wg, wu, wd):
    n = h.shape[0]
    row = pl.BlockSpec((ROW_TILE, D_MODEL), lambda i: (i, 0))
    return pl.pallas_call(
        _swiglu_kernel,
        grid=(n // ROW_TILE,),
        in_specs=[row, row, _resident(wg.shape), _resident(wu.shape), _resident(wd.shape)],
        out_specs=row,
        out_shape=jax.ShapeDtypeStruct((n, D_MODEL), F32),
        scratch_shapes=[pltpu.VMEM((ROW_TILE, D_MODEL), F32)],
        input_output_aliases={0: 0},
        compiler_params=_cparams(),
        name="swiglu",
    )(h, hn, wg, wu, wd)


def _group_ones(n, group_shift):
    r = lax.broadcasted_iota(jnp.int32, (n, n), 0) >> group_shift
    c = lax.broadcasted_iota(jnp.int32, (n, n), 1) >> group_shift
    return jnp.where(r == c, 1.0, 0.0).astype(BF16)


GLA_ROWS = 256


def _gla_kernel(tp, pa_ref, lr_ref, wg_ref, bg_ref, gn_ref, o_ref, s_ref):
    qhead = lax.broadcasted_iota(jnp.int32, (1, N_HEADS * GLA_DK), 1) >> 5
    ehead = lax.broadcasted_iota(jnp.int32, (1, N_HEADS * GLA_DV), 1) >> 6
    shead = lax.broadcasted_iota(jnp.int32, (N_HEADS * GLA_DK, 1), 0) >> 5
    bd = shead == ehead
    gsum = _group_ones(N_HEADS * GLA_DV, 6)
    wg_hi, wg_lo = _split_bf16(wg_ref[...])
    bg = bg_ref[...]
    gn = gn_ref[...]
    scale = GLA_DK ** -0.5
    s_ref[...] = jnp.zeros_like(s_ref)

    def block(r0, n_rows):
        n_c = n_rows // CHUNK
        rows = pl.ds(r0, n_rows)
        ri = lax.broadcasted_iota(jnp.int32, (n_rows, n_rows), 0)
        ci = lax.broadcasted_iota(jnp.int32, (n_rows, n_rows), 1)
        same = (ri >> 6) == (ci >> 6)
        tri_bf = jnp.where(jnp.logical_and(same, ri >= ci), 1.0, 0.0).astype(BF16)
        ones_bf = jnp.where(same, 1.0, 0.0).astype(BF16)
        r4 = lax.broadcasted_iota(jnp.int32, (N_HEADS * n_rows, n_rows), 0) & (n_rows - 1)
        c4 = lax.broadcasted_iota(jnp.int32, (N_HEADS * n_rows, n_rows), 1)
        tri4 = jnp.logical_and((r4 >> 6) == (c4 >> 6), r4 >= c4)
        col_chunk = lax.broadcasted_iota(jnp.int32, (1, n_rows), 1) >> 6

        q = pa_ref[0, rows, 0:128].astype(F32) * scale
        k = pa_ref[0, rows, 128:256].astype(F32)
        v = pa_ref[0, rows, 256:512]
        og = pa_ref[0, rows, 512:768].astype(F32)
        lr = lr_ref[0, rows, :]
        valid = (r0 + lax.broadcasted_iota(jnp.int32, (n_rows, 1), 0)) >= FRONT_PAD

        pre = _dot(lr, wg_hi) + _dot(lr, wg_lo) + bg
        logsig = jnp.minimum(pre, 0.0) - jnp.log1p(jnp.exp(-jnp.abs(pre)))
        g = jnp.where(valid, logsig * (1.0 / GLA_TAU), 0.0)
        g_hi, g_lo = _split_bf16(g)
        cum = _dot(tri_bf, g_hi) + _dot(tri_bf, g_lo)
        cum_end = _dot(ones_bf, g_hi) + _dot(ones_bf, g_lo)
        qt_bf = (q * jnp.exp(cum)).astype(BF16)
        kt = (k * jnp.exp(-cum)).astype(BF16)
        kd_t = (k * jnp.exp(cum_end - cum)).T.astype(BF16)
        dec_t = jnp.exp(cum_end.T)

        qs = jnp.concatenate([jnp.where(qhead == h, qt_bf, jnp.zeros_like(qt_bf))
                              for h in range(N_HEADS)], axis=0)
        a = jnp.where(tri4, _dot_nt(qs, kt), 0.0).astype(BF16)
        r = _dot(a, v)
        o = jnp.where(ehead == 0, r[0:n_rows, :], 0.0)
        for h in range(1, N_HEADS):
            o = o + jnp.where(ehead == h, r[h * n_rows:(h + 1) * n_rows, :], 0.0)

        s = s_ref[...]
        inter = []
        for c in range(n_c):
            inter.append(_dot(qt_bf[c * CHUNK:(c + 1) * CHUNK, :], s.astype(BF16)))
            upd = _dot(jnp.where(col_chunk == c, kd_t, jnp.zeros_like(kd_t)), v)
            s = dec_t[:, c * CHUNK:c * CHUNK + 1] * s + jnp.where(bd, upd, 0.0)
        s_ref[...] = s
        o = o + jnp.concatenate(inter, axis=0)

        ms = _dot((o * o).astype(BF16), gsum) * (1.0 / GLA_DV)
        y = o * lax.rsqrt(ms + EPS) * gn * (og * _sigmoid(og))
        o_ref[0, rows, :] = jnp.where(valid, y, 0.0).astype(BF16)

    block(0, Q_BLOCK)

    def body(i, carry):
        block(pl.multiple_of(Q_BLOCK + i * GLA_ROWS, Q_BLOCK), GLA_ROWS)
        return carry

    lax.fori_loop(0, (tp - Q_BLOCK) // GLA_ROWS, body, 0)


def _gla(pa, pc, wg, bg, gn):
    b, tp, _ = pa.shape
    return pl.pallas_call(
        functools.partial(_gla_kernel, tp),
        grid=(b,),
        in_specs=[pl.BlockSpec((1, tp, PA_W), lambda i: (i, 0, 0)),
                  pl.BlockSpec((1, tp, LANES), lambda i: (i, 0, 2)),
                  _resident((LANES, LANES)), _resident((1, LANES)), _resident((1, 256))],
        out_specs=pl.BlockSpec((1, tp, 256), lambda i: (i, 0, 0)),
        out_shape=jax.ShapeDtypeStruct((b, tp, 256), BF16),
        scratch_shapes=[pltpu.VMEM((N_HEADS * GLA_DK, N_HEADS * GLA_DV), F32)],
        compiler_params=_cparams(),
        name="gla",
    )(pa, pc, wg, bg, gn)


RET_BLOCK = 128


def _ret_kernel(tp, pb_ref, cos_ref, sin_ref, dmask_ref, qfac_ref, kfac_ref, dec_ref, gn_ref,
                o_ref, s_ref):
    blk = RET_BLOCK
    n_blocks = tp // blk
    qhead = (lax.broadcasted_iota(jnp.int32, (1, 256), 1) & 127) >> 5
    ehead = lax.broadcasted_iota(jnp.int32, (1, 256), 1) >> 6
    shead = (lax.broadcasted_iota(jnp.int32, (256, 1), 0) & 127) >> 5
    bd = shead == ehead
    gsum = _group_ones(256, 6)
    gn = gn_ref[...]
    s_ref[...] = jnp.zeros_like(s_ref)

    def rope(x, cos, sin):
        x1, x2 = x[:, :128], x[:, 128:]
        return jnp.concatenate([x1 * cos - x2 * sin, x1 * sin + x2 * cos], axis=1)

    def block(j, carry):
        r0 = pl.multiple_of(j * blk, blk)
        rows = pl.ds(r0, blk)
        cos = cos_ref[rows, :]
        sin = sin_ref[rows, :]
        q = rope(pb_ref[0, rows, 0:256].astype(F32), cos, sin)
        k = rope(pb_ref[0, rows, 256:512].astype(F32), cos, sin) * (RET_DK ** -0.5)
        v = pb_ref[0, rows, 512:768]
        og = pb_ref[0, rows, 768:1024].astype(F32)
        valid = (r0 + lax.broadcasted_iota(jnp.int32, (blk, 1), 0)) >= FRONT_PAD

        q_bf = q.astype(BF16)
        qs = jnp.concatenate([jnp.where(qhead == h, q_bf, jnp.zeros_like(q_bf))
                              for h in range(N_HEADS)], axis=0)
        a = (_dot_nt(qs, k.astype(BF16)) * dmask_ref[...]).astype(BF16)
        r = _dot(a, v)
        o = _dot((q * qfac_ref[...]).astype(BF16), s_ref[...].astype(BF16))
        for h in range(N_HEADS):
            o = o + jnp.where(ehead == h, r[h * blk:(h + 1) * blk, :], 0.0)

        kd_t = (k * kfac_ref[...]).T.astype(BF16)
        upd = _dot(kd_t, v)
        s_ref[...] = dec_ref[...] * s_ref[...] + jnp.where(bd, upd, 0.0)

        mu = _dot(o.astype(BF16), gsum) * (1.0 / 64)
        xc = o - mu
        var = _dot((xc * xc).astype(BF16), gsum) * (1.0 / 64)
        y = xc * lax.rsqrt(var + EPS) * gn * (og * _sigmoid(og))
        o_ref[0, rows, :] = jnp.where(valid, y, 0.0).astype(BF16)
        return carry

    lax.fori_loop(0, n_blocks, block, 0, unroll=4)


def _ret(pb, cos, sin, dmask, qfac, kfac, dec, gn):
    b, tp, _ = pb.shape
    return pl.pallas_call(
        functools.partial(_ret_kernel, tp),
        grid=(b,),
        in_specs=[pl.BlockSpec((1, tp, PB_W), lambda i: (i, 0, 0)),
                  _resident(cos.shape), _resident(sin.shape), _resident(dmask.shape),
                  _resident(qfac.shape), _resident(kfac.shape), _resident(dec.shape),
                  _resident((1, 256))],
        out_specs=pl.BlockSpec((1, tp, 256), lambda i: (i, 0, 0)),
        out_shape=jax.ShapeDtypeStruct((b, tp, 256), BF16),
        scratch_shapes=[pltpu.VMEM((256, 256), F32)],
        compiler_params=_cparams(),
        name="retention",
    )(pb, cos, sin, dmask, qfac, kfac, dec, gn)


ATT_ROWS = 256
LOG2E = 1.4426950408889634
V_ONE = 64


def _aligned(x, m):
    return x if isinstance(x, int) else pl.multiple_of(x, m)


def _for_chunks(n, body):
    if isinstance(n, int):
        for j in range(n):
            body(j)
        return

    def quad(t, carry):
        for u in range(4):
            body(4 * t + u)
        return carry

    lax.fori_loop(0, n >> 2, quad, 0)
    done = (n >> 2) << 2

    @pl.when((n & 2) != 0)
    def _():
        body(done)
        body(done + 1)

    @pl.when((n & 1) != 0)
    def _():
        body(done + (n & 2))


def _mask_groups(mask, s, n_g):
    rows = s.shape[0] // n_g
    return jnp.concatenate([jnp.where(mask, s[g * rows:(g + 1) * rows, :], NEG)
                            for g in range(n_g)], axis=0)


def _attn_block0(qk_fn, v_fn, n_g):
    rows = pl.ds(0, Q_BLOCK)
    qrow = lax.broadcasted_iota(jnp.int32, (Q_BLOCK, 1), 0)
    kcol = lax.broadcasted_iota(jnp.int32, (1, Q_BLOCK), 1)
    mask = jnp.logical_and(kcol <= qrow, kcol >= FRONT_PAD)
    s = _mask_groups(mask, qk_fn(0, Q_BLOCK, rows), n_g)
    p_bf = jnp.exp2(s - jnp.max(s, axis=-1, keepdims=True)).astype(BF16)
    pv = jnp.concatenate([_dot(p_bf[g * Q_BLOCK:(g + 1) * Q_BLOCK, :], v_fn(g, rows))
                          for g in range(n_g)], axis=0)
    return pv / pv[:, V_ONE:V_ONE + 1]


def _attn_row0(i):
    return _aligned(Q_BLOCK + (i - 1) * ATT_ROWS, Q_BLOCK)


def _attn_result(acc_s):
    acc = acc_s[...]
    return acc / acc[:, V_ONE:V_ONE + 1]


def _attn_block(i, qk_fn, v_fn, n_g, s_meta, s_s, mx_s, acc_s, mid_fn=None):
    rr = ATT_ROWS
    q0 = _attn_row0(i)
    meta_rows = pl.ds(0, Q_BLOCK)
    kcol = lax.broadcasted_iota(jnp.int32, (1, Q_BLOCK), 1)

    def key_rows(j):
        return pl.ds(_aligned(Q_BLOCK + j * rr, Q_BLOCK), rr)

    s = jnp.where(kcol >= FRONT_PAD, qk_fn(q0, rr, meta_rows), NEG)
    s_meta[...] = s
    mx_s[...] = s

    def pass1(j):
        sj = qk_fn(q0, rr, key_rows(j))
        s_s[j] = sj
        mx_s[...] = jnp.maximum(mx_s[...], jnp.maximum(sj[:, :LANES], sj[:, LANES:]))

    _for_chunks(i - 1, pass1)
    if mid_fn is not None:
        mid_fn()
    causal = (lax.broadcasted_iota(jnp.int32, (rr, rr), 1)
              <= lax.broadcasted_iota(jnp.int32, (rr, rr), 0))
    sd = _mask_groups(causal, qk_fn(q0, rr, pl.ds(q0, rr)), n_g)
    s_s[i - 1] = sd
    m = jnp.max(jnp.maximum(mx_s[...], jnp.maximum(sd[:, :LANES], sd[:, LANES:])),
                axis=-1, keepdims=True)
    mx_s[...] = jnp.broadcast_to(m, mx_s.shape)

    p_bf = jnp.exp2(s_meta[...] - mx_s[...]).astype(BF16)
    for g in range(n_g):
        acc_s[g * rr:(g + 1) * rr, :] = _dot(p_bf[g * rr:(g + 1) * rr, :], v_fn(g, meta_rows))

    def pass2(j):
        sj = s_s[j]
        mrep = mx_s[...]
        p0 = jnp.exp2(sj[:, :LANES] - mrep)
        p1 = jnp.exp2(sj[:, LANES:] - mrep)
        pj = jnp.concatenate([p0.astype(BF16), p1.astype(BF16)], axis=1)
        for g in range(n_g):
            acc_s[g * rr:(g + 1) * rr, :] += _dot(pj[g * rr:(g + 1) * rr, :], v_fn(g, key_rows(j)))

    _for_chunks(i, pass2)


def _attn_all_blocks(n_q, block_fn, finish_fn):
    block_fn(1, None)

    def body(i, carry):
        block_fn(i, lambda: finish_fn(i - 1))
        return carry

    lax.fori_loop(2, n_q + 1, body, 0)
    finish_fn(n_q)


def _heads_to_lanes(per_head):
    low = lax.broadcasted_iota(jnp.int32, (1, LANES), 1) < 64
    lo = jnp.where(low, per_head[0], pltpu.roll(per_head[1], 64, 1))
    hi = jnp.where(low, per_head[2], pltpu.roll(per_head[3], 64, 1))
    return jnp.concatenate([lo, hi], axis=1)


def _with_ones_lane(v):
    lane = lax.broadcasted_iota(jnp.int32, (1, LANES), 1)
    return jnp.where(lane == V_ONE, jnp.ones_like(v), v)


def _mla_kernel(tp, pc_ref, qn_ref, kvn_ref, wuq_ref, wukv_ref, cos_ref, sa_ref, sb_ref,
                o_ref, q_s, k_s, v_s, s_meta, s_s, mx_s, acc_s):
    scale = (MLA_NOPE + MLA_ROPE) ** -0.5 * LOG2E
    is_q = lax.broadcasted_iota(jnp.int32, (1, 256), 1) < MLA_Q_RANK

    def prep(r0, n_rows):
        rows = pl.ds(r0, n_rows)
        x = pc_ref[0, rows, 0:256].astype(F32)
        x2 = x * x
        ms_q = jnp.sum(jnp.where(is_q, x2, 0.0), axis=-1, keepdims=True) * (1.0 / MLA_Q_RANK)
        ms_kv = jnp.sum(jnp.where(is_q, 0.0, x2), axis=-1, keepdims=True) * (1.0 / MLA_KV_RANK)
        yq = (x * lax.rsqrt(ms_q + EPS) * qn_ref[...]).astype(BF16)
        ykv = (x * lax.rsqrt(ms_kv + EPS) * kvn_ref[...]).astype(BF16)
        cq = _dot(yq, wuq_ref[...])
        kv = _dot(ykv, wukv_ref[...])
        cos = cos_ref[rows, :]
        sa = sa_ref[rows, :]
        sb = sb_ref[rows, :]

        def rope(t):
            return t * cos + pltpu.roll(t, 16, 1) * sa + pltpu.roll(t, LANES - 16, 1) * sb

        kpe_in = pc_ref[0, rows, 256:384].astype(F32)
        kpe = rope(jnp.where(lax.broadcasted_iota(jnp.int32, (1, LANES), 1) >= 64, kpe_in, 0.0))
        for h in range(N_HEADS):
            q_s[h, rows, :] = (rope(cq[:, h * LANES:(h + 1) * LANES]) * scale).astype(BF16)
            k_s[h, rows, :] = (kv[:, h * LANES:(h + 1) * LANES] + kpe).astype(BF16)
            v_s[h, rows, :] = _with_ones_lane(
                kv[:, (N_HEADS + h) * LANES:(N_HEADS + h + 1) * LANES]).astype(BF16)

    prep(0, Q_BLOCK)

    def prep_body(i, carry):
        prep(pl.multiple_of(Q_BLOCK + i * ATT_ROWS, Q_BLOCK), ATT_ROWS)
        return carry

    lax.fori_loop(0, (tp - Q_BLOCK) // ATT_ROWS, prep_body, 0)

    def qk_fn(q0, n_rows, krows):
        return jnp.concatenate([_dot_nt(q_s[h, pl.ds(q0, n_rows), :], k_s[h, krows, :])
                                for h in range(N_HEADS)], axis=0)

    def v_fn(h, krows):
        return v_s[h, krows, :]

    def emit(q0, n_rows, o, first):
        y = _heads_to_lanes([o[h * n_rows:(h + 1) * n_rows, :] for h in range(N_HEADS)])
        if first:
            qrow = lax.broadcasted_iota(jnp.int32, (n_rows, 1), 0)
            y = jnp.where(qrow >= FRONT_PAD, y, 0.0)
        o_ref[0, pl.ds(q0, n_rows), :] = y.astype(BF16)

    emit(0, Q_BLOCK, _attn_block0(qk_fn, v_fn, N_HEADS), True)

    def block(i, mid_fn):
        _attn_block(i, qk_fn, v_fn, N_HEADS, s_meta, s_s, mx_s, acc_s, mid_fn)

    def finish(i):
        emit(_attn_row0(i), ATT_ROWS, _attn_result(acc_s), False)

    _attn_all_blocks((tp - Q_BLOCK) // ATT_ROWS, block, finish)


def _attn_scratch(n_g, tp):
    g_rows = n_g * ATT_ROWS
    n_slots = (tp - Q_BLOCK) // ATT_ROWS
    return [pltpu.VMEM((g_rows, LANES), F32),
            pltpu.VMEM((n_slots, g_rows, ATT_ROWS), F32),
            pltpu.VMEM((g_rows, LANES), F32),
            pltpu.VMEM((g_rows, LANES), F32)]


def _mla(pc, qn, kvn, wuq, wukv, cos, sa, sb):
    b, tp, _ = pc.shape
    return pl.pallas_call(
        functools.partial(_mla_kernel, tp),
        grid=(b,),
        in_specs=[pl.BlockSpec((1, tp, PC_W), lambda i: (i, 0, 0)),
                  _resident((1, 256)), _resident((1, 256)),
                  _resident(wuq.shape), _resident(wukv.shape),
                  _resident(cos.shape), _resident(sa.shape), _resident(sb.shape)],
        out_specs=pl.BlockSpec((1, tp, 256), lambda i: (i, 0, 0)),
        out_shape=jax.ShapeDtypeStruct((b, tp, 256), BF16),
        scratch_shapes=[pltpu.VMEM((N_HEADS, tp, LANES), BF16),
                        pltpu.VMEM((N_HEADS, tp, LANES), BF16),
                        pltpu.VMEM((N_HEADS, tp, LANES), BF16)] + _attn_scratch(N_HEADS, tp),
        compiler_params=_cparams(),
        name="mla",
    )(pc, qn, kvn, wuq, wukv, cos, sa, sb)


def _diff_kernel(tp, lam_init, pd_ref, lam_ref, dn_ref, o_ref, qs_s, v_s, s_meta, s_s, mx_s, acc_s):
    n_maps = 2 * N_HEADS

    def fill_values(i, carry):
        rows = pl.ds(pl.multiple_of(i * Q_BLOCK, Q_BLOCK), Q_BLOCK)
        low = lax.broadcasted_iota(jnp.int32, (1, LANES), 1) < V_ONE
        for pair in range(N_HEADS // 2):
            two = pd_ref[0, rows, 512 + pair * LANES:512 + (pair + 1) * LANES].astype(F32)
            for h, vals in ((2 * pair, two), (2 * pair + 1, pltpu.roll(two, V_ONE, 1))):
                v_s[h, rows, :] = _with_ones_lane(jnp.where(low, vals, 0.0)).astype(BF16)
        return carry

    lax.fori_loop(0, tp // Q_BLOCK, fill_values, 0)
    scale = DIFF_DK ** -0.5 * LOG2E
    group = lax.broadcasted_iota(jnp.int32, (1, 256), 1) >> 5
    lv = lam_ref[...]
    lam = (jnp.exp(jnp.sum(lv[0:1, :] * lv[1:2, :], axis=-1, keepdims=True))
           - jnp.exp(jnp.sum(lv[2:3, :] * lv[3:4, :], axis=-1, keepdims=True)) + lam_init)
    dn = dn_ref[...]

    def stack_queries(q0, n_rows):
        q = (pd_ref[0, pl.ds(q0, n_rows), 0:256].astype(F32) * scale).astype(BF16)
        for g in range(n_maps):
            qs_s[g * n_rows:(g + 1) * n_rows, :] = jnp.where(group == g, q, jnp.zeros_like(q))

    def qk_fn(q0, n_rows, krows):
        return _dot_nt(qs_s[0:n_maps * n_rows, :], pd_ref[0, krows, 256:512])

    def v_fn(g, krows):
        return v_s[g // 2, krows, :]

    def emit(q0, n_rows, o, first):
        od = jnp.concatenate([o[(2 * h) * n_rows:(2 * h + 1) * n_rows, :]
                              - lam * o[(2 * h + 1) * n_rows:(2 * h + 2) * n_rows, :]
                              for h in range(N_HEADS)], axis=0)
        od = jnp.where(lax.broadcasted_iota(jnp.int32, (1, LANES), 1) < V_ONE, od, 0.0)
        ms = jnp.sum(od * od, axis=-1, keepdims=True) * (1.0 / 64)
        yh = od * lax.rsqrt(ms + EPS) * dn * (1.0 - lam_init)
        y = _heads_to_lanes([yh[h * n_rows:(h + 1) * n_rows, :] for h in range(N_HEADS)])
        if first:
            qrow = lax.broadcasted_iota(jnp.int32, (n_rows, 1), 0)
            y = jnp.where(qrow >= FRONT_PAD, y, 0.0)
        o_ref[0, pl.ds(q0, n_rows), :] = y.astype(BF16)

    stack_queries(0, Q_BLOCK)
    emit(0, Q_BLOCK, _attn_block0(qk_fn, v_fn, n_maps), True)

    def block(i, mid_fn):
        stack_queries(_attn_row0(i), ATT_ROWS)
        _attn_block(i, qk_fn, v_fn, n_maps, s_meta, s_s, mx_s, acc_s, mid_fn)

    def finish(i):
        emit(_attn_row0(i), ATT_ROWS, _attn_result(acc_s), False)

    _attn_all_blocks((tp - Q_BLOCK) // ATT_ROWS, block, finish)


def _diff(pd, lam_rows, dn, lam_init):
    b, tp, _ = pd.shape
    n_maps = 2 * N_HEADS
    return pl.pallas_call(
        functools.partial(_diff_kernel, tp, lam_init),
        grid=(b,),
        in_specs=[pl.BlockSpec((1, tp, PD_W), lambda i: (i, 0, 0)),
                  _resident(lam_rows.shape), _resident((1, LANES))],
        out_specs=pl.BlockSpec((1, tp, 256), lambda i: (i, 0, 0)),
        out_shape=jax.ShapeDtypeStruct((b, tp, 256), BF16),
        scratch_shapes=[pltpu.VMEM((n_maps * ATT_ROWS, 256), BF16),
                        pltpu.VMEM((N_HEADS, tp, LANES), BF16)] + _attn_scratch(n_maps, tp),
        compiler_params=_cparams(),
        name="diffattn",
    )(pd, lam_rows, dn)


def _pad_cols(x, width):
    return jnp.pad(x, ((0, 0), (0, width - x.shape[1])))


def _rot_split(w):
    d = w.shape[0]
    return w.reshape(d, N_HEADS, 2, 32).transpose(0, 2, 1, 3).reshape(d, 256)


def _layout_w_in(w):
    sizes = (128, 128, 256, 16, 256, 256, 256, 256, 256, 192, 64, 32, 256, 256, 256)
    offs = [0]
    for s_ in sizes:
        offs.append(offs[-1] + s_)
    seg = [w[:, offs[i]:offs[i + 1]] for i in range(len(sizes))]
    (a_q, a_k, a_v, a_lr, a_og, r_q, r_k, r_v, r_og, c_cq, c_ckv, c_kpe, d_q, d_k, d_v) = seg
    d = w.shape[0]
    z = lambda n: jnp.zeros((d, n), w.dtype)
    cols = [a_q, a_k, a_v, a_og,
            _rot_split(r_q), _rot_split(r_k), r_v, r_og,
            c_cq, c_ckv, a_lr, z(48), c_kpe, z(32),
            d_q, d_k, d_v]
    return jnp.concatenate(cols, axis=1).astype(BF16)


def _tables(tp):
    pos = jnp.arange(tp, dtype=F32) - FRONT_PAD
    inv = ROPE_THETA ** (-jnp.arange(32, dtype=F32) / 32)
    ang = pos[:, None] * inv[None, :]
    ret_cos = jnp.tile(jnp.cos(ang), (1, N_HEADS))
    ret_sin = jnp.tile(jnp.sin(ang), (1, N_HEADS))
    inv16 = ROPE_THETA ** (-jnp.arange(16, dtype=F32) / 16)
    ang16 = pos[:, None] * inv16[None, :]
    c16, s16 = jnp.cos(ang16), jnp.sin(ang16)
    one = lambda n: jnp.ones((tp, n), F32)
    zero = lambda n: jnp.zeros((tp, n), F32)
    mla_cos = jnp.concatenate([one(64), c16, c16, one(32)], axis=1)
    mla_sa = jnp.concatenate([zero(80), s16, zero(32)], axis=1)
    mla_sb = jnp.concatenate([zero(64), -s16, zero(48)], axis=1)
    lg = jnp.log(1.0 - jnp.exp2(-5.0 - jnp.arange(N_HEADS, dtype=F32)))
    idx = jnp.arange(RET_BLOCK, dtype=F32)
    rel = idx[:, None] - idx[None, :]
    dmask = jnp.where(rel[None] >= 0, jnp.exp(rel[None] * lg[:, None, None]), 0.0)
    dmask = dmask.reshape(N_HEADS * RET_BLOCK, RET_BLOCK)
    lane_head = (jnp.arange(256) % 128) // 32
    qfac = jnp.exp((idx[:, None] + 1.0) * lg[lane_head][None, :])
    kfac = jnp.exp((RET_BLOCK - 1.0 - idx[:, None]) * lg[lane_head][None, :])
    dec = jnp.exp(RET_BLOCK * lg[lane_head])[:, None]
    return ret_cos, ret_sin, mla_cos, mla_sa, mla_sb, dmask, qfac, kfac, dec


def kernel(x, meta_tokens, attn_norm, w_in, gla_w_gate, gla_b_gate, gla_norm, ret_norm, mla_q_norm, mla_w_uq, mla_kv_norm, mla_w_ukv, diff_lambda, diff_norm, w_out, ffn_norm, ffn_w_gate, ffn_w_up, ffn_w_down, moe_router, moe_w_gate, moe_w_up, moe_w_down, final_norm):
    b, seq, d = x.shape
    tp = FRONT_PAD + N_META + seq
    n = b * tp
    meta = jnp.broadcast_to(meta_tokens[None].astype(x.dtype), (b, N_META, d))
    h = jnp.concatenate([jnp.zeros((b, FRONT_PAD, d), x.dtype), meta, x], axis=1).reshape(n, d)
    ret_cos, ret_sin, mla_cos, mla_sa, mla_sb, dmask, qfac, kfac, dec = _tables(tp)

    for li in range(DEPTH):
        pa, pb, pc, pd = _inproj(h, attn_norm[li][None, :], _layout_w_in(w_in[li]))
        pa, pb, pc, pd = (p.reshape(b, tp, -1) for p in (pa, pb, pc, pd))

        wgate = jnp.pad(gla_w_gate[li], ((0, LANES - GLA_GATE_RANK), (0, 0)))
        o_a = _gla(pa, pc, wgate, gla_b_gate[li][None, :], jnp.tile(gla_norm[li], N_HEADS)[None, :])
        o_b = _ret(pb, ret_cos, ret_sin, dmask, qfac, kfac, dec, jnp.tile(ret_norm[li], N_HEADS)[None, :])

        qn = _pad_cols(mla_q_norm[li][None, :], 256)
        kvn = jnp.pad(mla_kv_norm[li][None, :], ((0, 0), (MLA_Q_RANK, 0)))
        wuq = jnp.pad(mla_w_uq[li].reshape(MLA_Q_RANK, N_HEADS, MLA_NOPE + MLA_ROPE),
                      ((0, 256 - MLA_Q_RANK), (0, 0), (0, LANES - MLA_NOPE - MLA_ROPE)))
        wuq = wuq.reshape(256, N_HEADS * LANES).astype(BF16)
        wukv = mla_w_ukv[li].reshape(MLA_KV_RANK, N_HEADS, 2, 64)
        wukv = jnp.pad(wukv, ((MLA_Q_RANK, 0), (0, 0), (0, 0), (0, 64)))
        wukv = wukv.transpose(0, 2, 1, 3).reshape(256, 2 * N_HEADS * LANES).astype(BF16)
        o_c = _mla(pc, qn, kvn, wuq, wukv, mla_cos, mla_sa, mla_sb)

        lam_init = 0.8 - 0.6 * math.exp(-0.3 * li)
        o_d = _diff(pd, diff_lambda[li], _pad_cols(diff_norm[li][None, :], LANES), lam_init)

        o_a, o_b, o_c, o_d = (o.reshape(n, 256) for o in (o_a, o_b, o_c, o_d))
        wo = w_out[li].astype(BF16)
        fn = ffn_norm[li][None, :]
        j = li // 2
        if li % 2 == 0:
            h, hn = _outproj(h, o_a, o_b, o_c, o_d, wo, fn)
            h = _swiglu(h, hn, ffn_w_gate[j].astype(BF16), ffn_w_up[j].astype(BF16),
                        ffn_w_down[j].astype(BF16))
        else:
            h, hn, info, cnt = _outproj(h, o_a, o_b, o_c, o_d, wo, fn,
                                        router=_pad_cols(moe_router[j], LANES), seq_rows=tp)
            return _routed_moe_norm(h, hn, info, cnt, moe_w_gate[j].astype(BF16),
                                    moe_w_up[j].astype(BF16), moe_w_down[j].astype(BF16),
                                    final_norm[None, :], b, tp)
```

```python
import functools
import math

import jax
import jax.numpy as jnp
from jax import lax
from jax.experimental import pallas as pl
from jax.experimental.pallas import tpu as pltpu

F32 = jnp.float32
BF16 = jnp.bfloat16

D_MODEL = 1024
DEPTH = 2
N_META = 16
CHUNK = 64
Q_BLOCK = 128
FRONT_PAD = Q_BLOCK - N_META
EPS = 1e-6
NEG = -1e30
ROPE_THETA = 10000.0
N_HEADS = 4
GLA_DK = 32
GLA_DV = 64
GLA_GATE_RANK = 16
GLA_TAU = 16.0
RET_DK = 64
MLA_Q_RANK = 192
MLA_KV_RANK = 64
MLA_NOPE = 64
MLA_ROPE = 32
DIFF_DK = 32
N_EXPERTS = 8

LANES = 128
SUBLANES = 8
ROW_TILE = 512
FF_CHUNK = 256
PREFETCH_TAIL_CHUNKS = 4
VMEM_LIMIT = 56 * 1024 * 1024

PA_W = 768
PB_W = 1024
PC_W = 384
PD_W = 768


def _cparams(n_axes=1):
    return pltpu.CompilerParams(dimension_semantics=("arbitrary",) * n_axes,
                                vmem_limit_bytes=VMEM_LIMIT)


def _resident(shape):
    nd = len(shape)
    return pl.BlockSpec(shape, lambda *_: (0,) * nd, pipeline_mode=pl.Buffered(1))


def _sigmoid(x):
    return 1.0 / (1.0 + jnp.exp(-x))


def _split_bf16(x):
    hi = x.astype(BF16)
    lo = (x - hi.astype(F32)).astype(BF16)
    return hi, lo


def _dot(a, b):
    return jnp.dot(a, b, preferred_element_type=F32)


def _dot_nt(a, b):
    return lax.dot_general(a, b, (((1,), (1,)), ((), ())), preferred_element_type=F32)


def _inproj_kernel(h_ref, g_ref, w_ref, pa_ref, pb_ref, pc_ref, pd_ref):
    x = h_ref[...]
    ms = jnp.mean(x * x, axis=-1, keepdims=True)
    y = (x * lax.rsqrt(ms + EPS) * g_ref[...]).astype(BF16)
    off = 0
    for o_ref, width in ((pa_ref, PA_W), (pb_ref, PB_W), (pc_ref, PC_W), (pd_ref, PD_W)):
        o_ref[...] = _dot(y, w_ref[:, off:off + width]).astype(BF16)
        off += width


def _inproj(h, g, w):
    n = h.shape[0]
    wtot = PA_W + PB_W + PC_W + PD_W
    row = lambda width: pl.BlockSpec((ROW_TILE, width), lambda i: (i, 0))
    return pl.pallas_call(
        _inproj_kernel,
        grid=(n // ROW_TILE,),
        in_specs=[row(D_MODEL), _resident((1, D_MODEL)), _resident((D_MODEL, wtot))],
        out_specs=[row(PA_W), row(PB_W), row(PC_W), row(PD_W)],
        out_shape=[jax.ShapeDtypeStruct((n, w_), BF16) for w_ in (PA_W, PB_W, PC_W, PD_W)],
        compiler_params=_cparams(),
        name="inproj",
    )(h, g, w)


def _outproj_kernel(router_seq_rows, h_ref, oa_ref, ob_ref, oc_ref, od_ref, wo_ref, fn_ref, *rest):
    with_router = router_seq_rows > 0
    if with_router:
        router_ref, below_ref, hmid_ref, hn_ref, info_ref, cnt_ref, carry_ref = rest
    else:
        hmid_ref, hn_ref = rest
    o = jnp.concatenate([oa_ref[...], ob_ref[...], oc_ref[...], od_ref[...]], axis=1)
    hm = h_ref[...] + _dot(o, wo_ref[...])
    hmid_ref[...] = hm
    ms = jnp.mean(hm * hm, axis=-1, keepdims=True)
    y = hm * lax.rsqrt(ms + EPS) * fn_ref[...]
    hn_ref[...] = y.astype(hn_ref.dtype)
    if with_router:
        y_hi, y_lo = _split_bf16(y)
        r_hi, r_lo = _split_bf16(router_ref[...])
        logits = _dot(y_hi, r_hi) + _dot(y_hi, r_lo) + _dot(y_lo, r_hi)
        lane = lax.broadcasted_iota(jnp.int32, logits.shape, 1).astype(F32)
        ninf = float("-inf")
        logits = jnp.where(lane < N_EXPERTS, logits, ninf)
        m1 = jnp.max(logits, axis=-1, keepdims=True)
        i1 = jnp.min(jnp.where(logits == m1, lane, float(LANES)), axis=-1, keepdims=True)
        rest_l = jnp.where(lane == i1, ninf, logits)
        m2 = jnp.max(rest_l, axis=-1, keepdims=True)
        i2 = jnp.min(jnp.where(rest_l == m2, lane, float(LANES)), axis=-1, keepdims=True)
        e2 = jnp.exp(m2 - m1)
        den = 1.0 + e2
        @pl.when(pl.program_id(0) == 0)
        def _():
            carry_ref[...] = jnp.zeros_like(carry_ref)
        sel = jnp.where(lane == i1, 1.0, 0.0) + jnp.where(lane == i2, 1.0, 0.0)
        rows = sel.shape[0]
        seq_pos = (lax.rem(pl.program_id(0) * rows, router_seq_rows)
                   + lax.broadcasted_iota(jnp.int32, (rows, 1), 0))
        seq_pos = jnp.where(seq_pos >= router_seq_rows, seq_pos - router_seq_rows, seq_pos)
        sel = jnp.where(seq_pos >= Q_BLOCK, sel, 0.0)
        count = _dot(below_ref[...], sel.astype(BF16)) + carry_ref[...]
        r1 = jnp.sum(jnp.where(lane == i1, count, 0.0), axis=-1, keepdims=True)
        r2 = jnp.sum(jnp.where(lane == i2, count, 0.0), axis=-1, keepdims=True)
        total = carry_ref[...] + jnp.sum(sel, axis=0, keepdims=True)
        carry_ref[...] = total
        cnt_ref[...] = jnp.broadcast_to(total, cnt_ref.shape)
        info = jnp.where(lane == 0, i1, 0.0)
        for k, val in enumerate((i2, 1.0 / den, e2 / den, r1, r2), start=1):
            info = jnp.where(lane == k, val, info)
        info_ref[...] = info


def _outproj(h, oa, ob, oc, od, wo, fn, router=None, seq_rows=0):
    n = h.shape[0]
    row = lambda width: pl.BlockSpec((ROW_TILE, width), lambda i: (i, 0))
    in_specs = [row(D_MODEL), row(256), row(256), row(256), row(256),
                _resident((D_MODEL, D_MODEL)), _resident((1, D_MODEL))]
    out_specs = [row(D_MODEL), row(D_MODEL)]
    out_shape = [jax.ShapeDtypeStruct((n, D_MODEL), F32),
                 jax.ShapeDtypeStruct((n, D_MODEL), BF16 if router is None else F32)]
    args = [h, oa, ob, oc, od, wo, fn]
    scratch = []
    if router is not None:
        in_specs += [_resident((D_MODEL, LANES)), _resident((ROW_TILE, ROW_TILE))]
        out_specs += [row(LANES), pl.BlockSpec((8, LANES), lambda i: (0, 0))]
        out_shape += [jax.ShapeDtypeStruct((n, LANES), F32), jax.ShapeDtypeStruct((8, LANES), F32)]
        args += [router, jnp.tri(ROW_TILE, k=-1, dtype=BF16)]
        scratch.append(pltpu.VMEM((1, LANES), F32))
    return pl.pallas_call(
        functools.partial(_outproj_kernel, seq_rows if router is not None else 0),
        grid=(n // ROW_TILE,),
        in_specs=in_specs, out_specs=out_specs, out_shape=out_shape,
        scratch_shapes=scratch,
        compiler_params=_cparams(),
        name="outproj_router" if router is not None else "outproj",
    )(*args)


def _inverse_map_kernel(zs_ref, ze_ref, pos_ref, tok_ref):
    i = pl.program_id(0)

    @pl.when(i == 0)
    def _():
        def zero(j, carry):
            tok_ref[j] = 0
            return carry

        for k in range(zs_ref.shape[0]):
            lax.fori_loop(zs_ref[k], ze_ref[k], zero, 0)

    base = i * ROW_TILE

    for r in range(ROW_TILE):
        tok_ref[pos_ref[0, 0, 2 * r]] = base + r
        tok_ref[pos_ref[0, 0, 2 * r + 1]] = base + r


def _inverse_map(zero_start, zero_end, pos, n_sorted):
    return pl.pallas_call(
        _inverse_map_kernel,
        grid_spec=pltpu.PrefetchScalarGridSpec(
            num_scalar_prefetch=2,
            grid=(pos.shape[0],),
            in_specs=[pl.BlockSpec((1, 1, 2 * ROW_TILE), lambda i, zs, ze: (i, 0, 0),
                                   memory_space=pltpu.SMEM)],
            out_specs=pl.BlockSpec(memory_space=pltpu.SMEM)),
        out_shape=jax.ShapeDtypeStruct((n_sorted + ROW_TILE,), jnp.int32),
        compiler_params=_cparams(),
        name="moe_inverse_map",
    )(zero_start, zero_end, pos)


def _swiglu_tile(x, wg, wu, wd, acc_ref, per_chunk=None):
    n_chunks = wd.shape[0] // FF_CHUNK
    for c in range(n_chunks):
        cols = slice(c * FF_CHUNK, (c + 1) * FF_CHUNK)
        g = _dot(x, wg[:, cols])
        u = _dot(x, wu[:, cols])
        part = _dot((g * _sigmoid(g) * u).astype(BF16), wd[cols, :])
        if c == 0:
            acc_ref[...] = part
        elif c < n_chunks - 1:
            acc_ref[...] += part
        if per_chunk is not None:
            per_chunk(c, n_chunks)
    return acc_ref[...] + part


def _experts_kernel(te_ref, tv_ref, tok_ref, tok_next_ref, hn_ref, wg_ref, wu_ref, wd_ref, ys_ref,
                    x_buf, acc_ref, sem):
    i = pl.program_id(0)
    last = pl.num_programs(0) - 1
    slot = i & 1
    n_blocks = x_buf.shape[1]

    def row_copy(t_ref, s, rb, u):
        tok = t_ref[0, 0, rb * SUBLANES + u]
        return pltpu.make_async_copy(hn_ref.at[pl.ds(tok, 1), :], x_buf.at[s, rb, pl.ds(u, 1), :],
                                     sem.at[s])

    def wait_tile(s):
        pltpu.make_async_copy(x_buf.at[s], x_buf.at[s], sem.at[s]).wait()

    @pl.when(i == 0)
    def _():
        def start_block(rb, carry):
            for u in range(SUBLANES):
                row_copy(tok_ref, 0, rb, u).start(priority=u % 2)
            return carry

        lax.fori_loop(0, n_blocks, start_block, 0)

    @pl.when(jnp.logical_or(i == 0, tv_ref[jnp.maximum(i - 1, 0)] > 0))
    def _():
        wait_tile(slot)

    @pl.when(tv_ref[i] > 0)
    def _():
        def prefetch(c, n_chunks):
            per = -(-ROW_TILE // max(n_chunks - PREFETCH_TAIL_CHUNKS, 1))
            for r in range(c * per, min((c + 1) * per, ROW_TILE)):
                row_copy(tok_next_ref, 1 - slot, r // SUBLANES, r % SUBLANES).start(priority=r % 2)

        x = x_buf[slot].reshape(ROW_TILE, -1).astype(BF16)
        ys_ref[...] = _swiglu_tile(x, wg_ref.at[0], wu_ref.at[0], wd_ref.at[0], acc_ref, prefetch)

    @pl.when(tv_ref[i] == 0)
    def _():
        ys_ref[...] = jnp.zeros_like(ys_ref)

    @pl.when(jnp.logical_and(i == last, tv_ref[i] > 0))
    def _():
        wait_tile(1 - slot)


def _experts(tile_expert, tile_valid, tok, hn, wg, wu, wd):
    n_tiles = tok.shape[0]
    d = hn.shape[1]
    wspec = lambda shape: pl.BlockSpec((1,) + shape[1:], lambda i, te, tv: (te[i], 0, 0),
                                       pipeline_mode=pl.Buffered(1))
    tok_spec = lambda index: pl.BlockSpec((1, 1, ROW_TILE), index, memory_space=pltpu.SMEM)
    return pl.pallas_call(
        _experts_kernel,
        grid_spec=pltpu.PrefetchScalarGridSpec(
            num_scalar_prefetch=2,
            grid=(n_tiles,),
            in_specs=[tok_spec(lambda i, te, tv: (i, 0, 0)),
                      tok_spec(lambda i, te, tv: (jnp.minimum(i + 1, n_tiles - 1), 0, 0)),
                      pl.BlockSpec(memory_space=pl.ANY),
                      wspec(wg.shape), wspec(wu.shape), wspec(wd.shape)],
            out_specs=pl.BlockSpec((ROW_TILE, d), lambda i, te, tv: (i, 0)),
            scratch_shapes=[pltpu.VMEM((2, ROW_TILE // SUBLANES, SUBLANES, d), F32),
                            pltpu.VMEM((ROW_TILE, d), F32),
                            pltpu.SemaphoreType.DMA((2,))]),
        out_shape=jax.ShapeDtypeStruct((n_tiles * ROW_TILE, d), F32),
        compiler_params=_cparams(),
        name="moe_experts",
    )(tile_expert, tile_valid, tok, tok, hn, wg, wu, wd)


def _combine_norm_kernel(tp, pos_ref, h_ref, info_ref, ys_ref, g_ref, o_ref,
                         h_buf, info_buf, y_buf, sem, row_sem):
    n_rows = h_buf.shape[0]
    row0 = pl.multiple_of(pl.program_id(0) * tp + Q_BLOCK + pl.program_id(1) * n_rows, Q_BLOCK)
    h_copy = pltpu.make_async_copy(h_ref.at[pl.ds(row0, n_rows), :], h_buf, sem.at[0])
    info_copy = pltpu.make_async_copy(info_ref.at[pl.ds(row0, n_rows), :], info_buf, sem.at[1])
    h_copy.start()
    info_copy.start()

    def row_copy(rb, u, k):
        p = pos_ref[0, 0, rb * (2 * SUBLANES) + 2 * u + k]
        return pltpu.make_async_copy(ys_ref.at[pl.ds(p, 1), :], y_buf.at[k, rb, pl.ds(u, 1), :], row_sem)

    for rb in range(n_rows // SUBLANES):
        for u in range(SUBLANES):
            row_copy(rb, u, 0).start(priority=0)
            row_copy(rb, u, 1).start(priority=1)
    pltpu.make_async_copy(y_buf, y_buf, row_sem).wait()
    h_copy.wait()
    info_copy.wait()
    info = info_buf[...]
    y0 = y_buf[0].reshape(n_rows, -1)
    y1 = y_buf[1].reshape(n_rows, -1)
    x = h_buf[...] + info[:, 2:3] * y0 + info[:, 3:4] * y1
    ms = jnp.mean(x * x, axis=-1, keepdims=True)
    o_ref[0] = x * lax.rsqrt(ms + EPS) * g_ref[...]


def _combine_norm(pos, h, info, ys, g, b, tp):
    n, d = h.shape
    seq = tp - Q_BLOCK
    tiles = seq // ROW_TILE
    any_spec = pl.BlockSpec(memory_space=pl.ANY)
    return pl.pallas_call(
        functools.partial(_combine_norm_kernel, tp),
        grid=(b, tiles),
        in_specs=[pl.BlockSpec((1, 1, 2 * ROW_TILE), lambda i, j: (i * tiles + j, 0, 0),
                               memory_space=pltpu.SMEM),
                  any_spec, any_spec, any_spec, _resident((1, d))],
        out_specs=pl.BlockSpec((1, ROW_TILE, d), lambda i, j: (i, j, 0)),
        out_shape=jax.ShapeDtypeStruct((b, seq, d), F32),
        scratch_shapes=[pltpu.VMEM((ROW_TILE, d), F32), pltpu.VMEM((ROW_TILE, LANES), F32),
                        pltpu.VMEM((2, ROW_TILE // SUBLANES, SUBLANES, d), F32),
                        pltpu.SemaphoreType.DMA((2,)), pltpu.SemaphoreType.DMA(())],
        compiler_params=_cparams(2),
        name="moe_combine_norm",
    )(pos, h, info, ys, g)


def _routed_moe_norm(h, hn, info, cnt, wg, wu, wd, g, b, tp):
    n = h.shape[0]
    n_routed = b * (tp - Q_BLOCK)
    n_sorted = 2 * n_routed + N_EXPERTS * ROW_TILE
    n_tiles = n_sorted // ROW_TILE
    counts = cnt[0, :N_EXPERTS].astype(jnp.int32)
    tiles_e = (counts + ROW_TILE - 1) // ROW_TILE
    tile_end = jnp.cumsum(tiles_e)
    row_base = (tile_end - tiles_e) * ROW_TILE
    e_idx = info[:, 0:2].astype(jnp.int32)
    pos = row_base[e_idx] + info[:, 4:6].astype(jnp.int32)
    routed = (jnp.arange(n, dtype=jnp.int32) % tp >= Q_BLOCK)[:, None]
    zero_start = jnp.concatenate([row_base + counts, tile_end[-1:] * ROW_TILE])
    zero_end = jnp.concatenate([tile_end * ROW_TILE, jnp.full((1,), n_sorted + ROW_TILE, jnp.int32)])
    tok = _inverse_map(zero_start, zero_end,
                       jnp.where(routed, pos, n_sorted).reshape(n // ROW_TILE, 1, 2 * ROW_TILE),
                       n_sorted)[:n_sorted]
    tile_ids = jnp.arange(n_tiles, dtype=jnp.int32)
    tile_expert = jnp.minimum(jnp.sum((tile_ids[:, None] >= tile_end[None, :]).astype(jnp.int32), axis=1),
                              N_EXPERTS - 1)
    tile_valid = (tile_ids < tile_end[-1]).astype(jnp.int32)
    ys = _experts(tile_expert, tile_valid, tok.reshape(n_tiles, 1, ROW_TILE), hn, wg, wu, wd)
    pos_seq = pos.reshape(b, tp, 2)[:, Q_BLOCK:, :].reshape(-1, 1, 2 * ROW_TILE)
    return _combine_norm(pos_seq, h, info, ys, g, b, tp)


def _swiglu_kernel(h_ref, hn_ref, wg_ref, wu_ref, wd_ref, o_ref, acc_ref):
    o_ref[...] = h_ref[...] + _swiglu_tile(hn_ref[...], wg_ref, wu_ref, wd_ref, acc_ref)


def _swiglu(h, hn, wg, wu, wd):
    n = h.shape[0]
    row = pl.BlockSpec((ROW_TILE, D_MODEL), lambda i: (i, 0))
    return pl.pallas_call(
        _swiglu_kernel,
        grid=(n // ROW_TILE,),
        in_specs=[row, row, _resident(wg.shape), _resident(wu.shape), _resident(wd.shape)],
        out_specs=row,
        out_shape=jax.ShapeDtypeStruct((n, D_MODEL), F32),
        scratch_shapes=[pltpu.VMEM((ROW_TILE, D_MODEL), F32)],
        input_output_aliases={0: 0},
        compiler_params=_cparams(),
        name="swiglu",
    )(h, hn, wg, wu, wd)


def _group_ones(n, group_shift):
    r = lax.broadcasted_iota(jnp.int32, (n, n), 0) >> group_shift
    c = lax.broadcasted_iota(jnp.int32, (n, n), 1) >> group_shift
    return jnp.where(r == c, 1.0, 0.0).astype(BF16)


GLA_ROWS = 256


def _gla_kernel(tp, pa_ref, lr_ref, wg_ref, bg_ref, gn_ref, o_ref, s_ref):
    qhead = lax.broadcasted_iota(jnp.int32, (1, N_HEADS * GLA_DK), 1) >> 5
    ehead = lax.broadcasted_iota(jnp.int32, (1, N_HEADS * GLA_DV), 1) >> 6
    shead = lax.broadcasted_iota(jnp.int32, (N_HEADS * GLA_DK, 1), 0) >> 5
    bd = shead == ehead
    gsum = _group_ones(N_HEADS * GLA_DV, 6)
    wg_hi, wg_lo = _split_bf16(wg_ref[...])
    bg = bg_ref[...]
    gn = gn_ref[...]
    scale = GLA_DK ** -0.5
    s_ref[...] = jnp.zeros_like(s_ref)

    def block(r0, n_rows):
        n_c = n_rows // CHUNK
        rows = pl.ds(r0, n_rows)
        ri = lax.broadcasted_iota(jnp.int32, (n_rows, n_rows), 0)
        ci = lax.broadcasted_iota(jnp.int32, (n_rows, n_rows), 1)
        same = (ri >> 6) == (ci >> 6)
        tri_bf = jnp.where(jnp.logical_and(same, ri >= ci), 1.0, 0.0).astype(BF16)
        ones_bf = jnp.where(same, 1.0, 0.0).astype(BF16)
        r4 = lax.broadcasted_iota(jnp.int32, (N_HEADS * n_rows, n_rows), 0) & (n_rows - 1)
        c4 = lax.broadcasted_iota(jnp.int32, (N_HEADS * n_rows, n_rows), 1)
        tri4 = jnp.logical_and((r4 >> 6) == (c4 >> 6), r4 >= c4)
        col_chunk = lax.broadcasted_iota(jnp.int32, (1, n_rows), 1) >> 6

        q = pa_ref[0, rows, 0:128].astype(F32) * scale
        k = pa_ref[0, rows, 128:256].astype(F32)
        v = pa_ref[0, rows, 256:512]
        og = pa_ref[0, rows, 512:768].astype(F32)
        lr = lr_ref[0, rows, :]
        valid = (r0 + lax.broadcasted_iota(jnp.int32, (n_rows, 1), 0)) >= FRONT_PAD

        pre = _dot(lr, wg_hi) + _dot(lr, wg_lo) + bg
        logsig = jnp.minimum(pre, 0.0) - jnp.log1p(jnp.exp(-jnp.abs(pre)))
        g = jnp.where(valid, logsig * (1.0 / GLA_TAU), 0.0)
        g_hi, g_lo = _split_bf16(g)
        cum = _dot(tri_bf, g_hi) + _dot(tri_bf, g_lo)
        cum_end = _dot(ones_bf, g_hi) + _dot(ones_bf, g_lo)
        qt_bf = (q * jnp.exp(cum)).astype(BF16)
        kt = (k * jnp.exp(-cum)).astype(BF16)
        kd_t = (k * jnp.exp(cum_end - cum)).T.astype(BF16)
        dec_t = jnp.exp(cum_end.T)

        qs = jnp.concatenate([jnp.where(qhead == h, qt_bf, jnp.zeros_like(qt_bf))
                              for h in range(N_HEADS)], axis=0)
        a = jnp.where(tri4, _dot_nt(qs, kt), 0.0).astype(BF16)
        r = _dot(a, v)
        o = jnp.where(ehead == 0, r[0:n_rows, :], 0.0)
        for h in range(1, N_HEADS):
            o = o + jnp.where(ehead == h, r[h * n_rows:(h + 1) * n_rows, :], 0.0)

        s = s_ref[...]
        inter = []
        for c in range(n_c):
            inter.append(_dot(qt_bf[c * CHUNK:(c + 1) * CHUNK, :], s.astype(BF16)))
            upd = _dot(jnp.where(col_chunk == c, kd_t, jnp.zeros_like(kd_t)), v)
            s = dec_t[:, c * CHUNK:c * CHUNK + 1] * s + jnp.where(bd, upd, 0.0)
        s_ref[...] = s
        o = o + jnp.concatenate(inter, axis=0)

        ms = _dot((o * o).astype(BF16), gsum) * (1.0 / GLA_DV)
        y = o * lax.rsqrt(ms + EPS) * gn * (og * _sigmoid(og))
        o_ref[0, rows, :] = jnp.where(valid, y, 0.0).astype(BF16)

    block(0, Q_BLOCK)

    def body(i, carry):
        block(pl.multiple_of(Q_BLOCK + i * GLA_ROWS, Q_BLOCK), GLA_ROWS)
        return carry

    lax.fori_loop(0, (tp - Q_BLOCK) // GLA_ROWS, body, 0)


def _gla(pa, pc, wg, bg, gn):
    b, tp, _ = pa.shape
    return pl.pallas_call(
        functools.partial(_gla_kernel, tp),
        grid=(b,),
        in_specs=[pl.BlockSpec((1, tp, PA_W), lambda i: (i, 0, 0)),
                  pl.BlockSpec((1, tp, LANES), lambda i: (i, 0, 2)),
                  _resident((LANES, LANES)), _resident((1, LANES)), _resident((1, 256))],
        out_specs=pl.BlockSpec((1, tp, 256), lambda i: (i, 0, 0)),
        out_shape=jax.ShapeDtypeStruct((b, tp, 256), BF16),
        scratch_shapes=[pltpu.VMEM((N_HEADS * GLA_DK, N_HEADS * GLA_DV), F32)],
        compiler_params=_cparams(),
        name="gla",
    )(pa, pc, wg, bg, gn)


RET_BLOCK = 128


def _ret_kernel(tp, pb_ref, cos_ref, sin_ref, dmask_ref, qfac_ref, kfac_ref, dec_ref, gn_ref,
                o_ref, s_ref):
    blk = RET_BLOCK
    n_blocks = tp // blk
    qhead = (lax.broadcasted_iota(jnp.int32, (1, 256), 1) & 127) >> 5
    ehead = lax.broadcasted_iota(jnp.int32, (1, 256), 1) >> 6
    shead = (lax.broadcasted_iota(jnp.int32, (256, 1), 0) & 127) >> 5
    bd = shead == ehead
    gsum = _group_ones(256, 6)
    gn = gn_ref[...]
    s_ref[...] = jnp.zeros_like(s_ref)

    def rope(x, cos, sin):
        x1, x2 = x[:, :128], x[:, 128:]
        return jnp.concatenate([x1 * cos - x2 * sin, x1 * sin + x2 * cos], axis=1)

    def block(j, carry):
        r0 = pl.multiple_of(j * blk, blk)
        rows = pl.ds(r0, blk)
        cos = cos_ref[rows, :]
        sin = sin_ref[rows, :]
        q = rope(pb_ref[0, rows, 0:256].astype(F32), cos, sin)
        k = rope(pb_ref[0, rows, 256:512].astype(F32), cos, sin) * (RET_DK ** -0.5)
        v = pb_ref[0, rows, 512:768]
        og = pb_ref[0, rows, 768:1024].astype(F32)
        valid = (r0 + lax.broadcasted_iota(jnp.int32, (blk, 1), 0)) >= FRONT_PAD

        q_bf = q.astype(BF16)
        qs = jnp.concatenate([jnp.where(qhead == h, q_bf, jnp.zeros_like(q_bf))
                              for h in range(N_HEADS)], axis=0)
        a = (_dot_nt(qs, k.astype(BF16)) * dmask_ref[...]).astype(BF16)
        r = _dot(a, v)
        o = _dot((q * qfac_ref[...]).astype(BF16), s_ref[...].astype(BF16))
        for h in range(N_HEADS):
            o = o + jnp.where(ehead == h, r[h * blk:(h + 1) * blk, :], 0.0)

        kd_t = (k * kfac_ref[...]).T.astype(BF16)
        upd = _dot(kd_t, v)
        s_ref[...] = dec_ref[...] * s_ref[...] + jnp.where(bd, upd, 0.0)

        mu = _dot(o.astype(BF16), gsum) * (1.0 / 64)
        xc = o - mu
        var = _dot((xc * xc).astype(BF16), gsum) * (1.0 / 64)
        y = xc * lax.rsqrt(var + EPS) * gn * (og * _sigmoid(og))
        o_ref[0, rows, :] = jnp.where(valid, y, 0.0).astype(BF16)
        return carry

    lax.fori_loop(0, n_blocks, block, 0, unroll=4)


def _ret(pb, cos, sin, dmask, qfac, kfac, dec, gn):
    b, tp, _ = pb.shape
    return pl.pallas_call(
        functools.partial(_ret_kernel, tp),
        grid=(b,),
        in_specs=[pl.BlockSpec((1, tp, PB_W), lambda i: (i, 0, 0)),
                  _resident(cos.shape), _resident(sin.shape), _resident(dmask.shape),
                  _resident(qfac.shape), _resident(kfac.shape), _resident(dec.shape),
                  _resident((1, 256))],
        out_specs=pl.BlockSpec((1, tp, 256), lambda i: (i, 0, 0)),
        out_shape=jax.ShapeDtypeStruct((b, tp, 256), BF16),
        scratch_shapes=[pltpu.VMEM((256, 256), F32)],
        compiler_params=_cparams(),
        name="retention",
    )(pb, cos, sin, dmask, qfac, kfac, dec, gn)


ATT_ROWS = 256
LOG2E = 1.4426950408889634
V_ONE = 64


def _aligned(x, m):
    return x if isinstance(x, int) else pl.multiple_of(x, m)


def _for_chunks(n, body):
    if isinstance(n, int):
        for j in range(n):
            body(j)
        return

    def quad(t, carry):
        for u in range(4):
            body(4 * t + u)
        return carry

    lax.fori_loop(0, n >> 2, quad, 0)
    done = (n >> 2) << 2

    @pl.when((n & 2) != 0)
    def _():
        body(done)
        body(done + 1)

    @pl.when((n & 1) != 0)
    def _():
        body(done + (n & 2))


def _mask_groups(mask, s, n_g):
    rows = s.shape[0] // n_g
    return jnp.concatenate([jnp.where(mask, s[g * rows:(g + 1) * rows, :], NEG)
                            for g in range(n_g)], axis=0)


def _attn_block0(qk_fn, v_fn, n_g):
    rows = pl.ds(0, Q_BLOCK)
    qrow = lax.broadcasted_iota(jnp.int32, (Q_BLOCK, 1), 0)
    kcol = lax.broadcasted_iota(jnp.int32, (1, Q_BLOCK), 1)
    mask = jnp.logical_and(kcol <= qrow, kcol >= FRONT_PAD)
    s = _mask_groups(mask, qk_fn(0, Q_BLOCK, rows), n_g)
    p_bf = jnp.exp2(s - jnp.max(s, axis=-1, keepdims=True)).astype(BF16)
    pv = jnp.concatenate([_dot(p_bf[g * Q_BLOCK:(g + 1) * Q_BLOCK, :], v_fn(g, rows))
                          for g in range(n_g)], axis=0)
    return pv / pv[:, V_ONE:V_ONE + 1]


def _attn_row0(i):
    return _aligned(Q_BLOCK + (i - 1) * ATT_ROWS, Q_BLOCK)


def _attn_result(acc_s):
    acc = acc_s[...]
    return acc / acc[:, V_ONE:V_ONE + 1]


def _attn_block(i, qk_fn, v_fn, n_g, s_meta, s_s, mx_s, acc_s, mid_fn=None):
    rr = ATT_ROWS
    q0 = _attn_row0(i)
    meta_rows = pl.ds(0, Q_BLOCK)
    kcol = lax.broadcasted_iota(jnp.int32, (1, Q_BLOCK), 1)

    def key_rows(j):
        return pl.ds(_aligned(Q_BLOCK + j * rr, Q_BLOCK), rr)

    s = jnp.where(kcol >= FRONT_PAD, qk_fn(q0, rr, meta_rows), NEG)
    s_meta[...] = s
    mx_s[...] = s

    def pass1(j):
        sj = qk_fn(q0, rr, key_rows(j))
        s_s[j] = sj
        mx_s[...] = jnp.maximum(mx_s[...], jnp.maximum(sj[:, :LANES], sj[:, LANES:]))

    _for_chunks(i - 1, pass1)
    if mid_fn is not None:
        mid_fn()
    causal = (lax.broadcasted_iota(jnp.int32, (rr, rr), 1)
              <= lax.broadcasted_iota(jnp.int32, (rr, rr), 0))
    sd = _mask_groups(causal, qk_fn(q0, rr, pl.ds(q0, rr)), n_g)
    s_s[i - 1] = sd
    m = jnp.max(jnp.maximum(mx_s[...], jnp.maximum(sd[:, :LANES], sd[:, LANES:])),
                axis=-1, keepdims=True)
    mx_s[...] = jnp.broadcast_to(m, mx_s.shape)

    p_bf = jnp.exp2(s_meta[...] - mx_s[...]).astype(BF16)
    for g in range(n_g):
        acc_s[g * rr:(g + 1) * rr, :] = _dot(p_bf[g * rr:(g + 1) * rr, :], v_fn(g, meta_rows))

    def pass2(j):
        sj = s_s[j]
        mrep = mx_s[...]
        p0 = jnp.exp2(sj[:, :LANES] - mrep)
        p1 = jnp.exp2(sj[:, LANES:] - mrep)
        pj = jnp.concatenate([p0.astype(BF16), p1.astype(BF16)], axis=1)
        for g in range(n_g):
            acc_s[g * rr:(g + 1) * rr, :] += _dot(pj[g * rr:(g + 1) * rr, :], v_fn(g, key_rows(j)))

    _for_chunks(i, pass2)


def _attn_all_blocks(n_q, block_fn, finish_fn):
    block_fn(1, None)

    def body(i, carry):
        block_fn(i, lambda: finish_fn(i - 1))
        return carry

    lax.fori_loop(2, n_q + 1, body, 0)
    finish_fn(n_q)


def _heads_to_lanes(per_head):
    low = lax.broadcasted_iota(jnp.int32, (1, LANES), 1) < 64
    lo = jnp.where(low, per_head[0], pltpu.roll(per_head[1], 64, 1))
    hi = jnp.where(low, per_head[2], pltpu.roll(per_head[3], 64, 1))
    return jnp.concatenate([lo, hi], axis=1)


def _with_ones_lane(v):
    lane = lax.broadcasted_iota(jnp.int32, (1, LANES), 1)
    return jnp.where(lane == V_ONE, jnp.ones_like(v), v)


def _mla_kernel(tp, pc_ref, qn_ref, kvn_ref, wuq_ref, wukv_ref, cos_ref, sa_ref, sb_ref,
                o_ref, q_s, k_s, v_s, s_meta, s_s, mx_s, acc_s):
    scale = (MLA_NOPE + MLA_ROPE) ** -0.5 * LOG2E
    is_q = lax.broadcasted_iota(jnp.int32, (1, 256), 1) < MLA_Q_RANK

    def prep(r0, n_rows):
        rows = pl.ds(r0, n_rows)
        x = pc_ref[0, rows, 0:256].astype(F32)
        x2 = x * x
        ms_q = jnp.sum(jnp.where(is_q, x2, 0.0), axis=-1, keepdims=True) * (1.0 / MLA_Q_RANK)
        ms_kv = jnp.sum(jnp.where(is_q, 0.0, x2), axis=-1, keepdims=True) * (1.0 / MLA_KV_RANK)
        yq = (x * lax.rsqrt(ms_q + EPS) * qn_ref[...]).astype(BF16)
        ykv = (x * lax.rsqrt(ms_kv + EPS) * kvn_ref[...]).astype(BF16)
        cq = _dot(yq, wuq_ref[...])
        kv = _dot(ykv, wukv_ref[...])
        cos = cos_ref[rows, :]
        sa = sa_ref[rows, :]
        sb = sb_ref[rows, :]

        def rope(t):
            return t * cos + pltpu.roll(t, 16, 1) * sa + pltpu.roll(t, LANES - 16, 1) * sb

        kpe_in = pc_ref[0, rows, 256:384].astype(F32)
        kpe = rope(jnp.where(lax.broadcasted_iota(jnp.int32, (1, LANES), 1) >= 64, kpe_in, 0.0))
        for h in range(N_HEADS):
            q_s[h, rows, :] = (rope(cq[:, h * LANES:(h + 1) * LANES]) * scale).astype(BF16)
            k_s[h, rows, :] = (kv[:, h * LANES:(h + 1) * LANES] + kpe).astype(BF16)
            v_s[h, rows, :] = _with_ones_lane(
                kv[:, (N_HEADS + h) * LANES:(N_HEADS + h + 1) * LANES]).astype(BF16)

    prep(0, Q_BLOCK)

    def prep_body(i, carry):
        prep(pl.multiple_of(Q_BLOCK + i * ATT_ROWS, Q_BLOCK), ATT_ROWS)
        return carry

    lax.fori_loop(0, (tp - Q_BLOCK) // ATT_ROWS, prep_body, 0)

    def qk_fn(q0, n_rows, krows):
        return jnp.concatenate([_dot_nt(q_s[h, pl.ds(q0, n_rows), :], k_s[h, krows, :])
                                for h in range(N_HEADS)], axis=0)

    def v_fn(h, krows):
        return v_s[h, krows, :]

    def emit(q0, n_rows, o, first):
        y = _heads_to_lanes([o[h * n_rows:(h + 1) * n_rows, :] for h in range(N_HEADS)])
        if first:
            qrow = lax.broadcasted_iota(jnp.int32, (n_rows, 1), 0)
            y = jnp.where(qrow >= FRONT_PAD, y, 0.0)
        o_ref[0, pl.ds(q0, n_rows), :] = y.astype(BF16)

    emit(0, Q_BLOCK, _attn_block0(qk_fn, v_fn, N_HEADS), True)

    def block(i, mid_fn):
        _attn_block(i, qk_fn, v_fn, N_HEADS, s_meta, s_s, mx_s, acc_s, mid_fn)

    def finish(i):
        emit(_attn_row0(i), ATT_ROWS, _attn_result(acc_s), False)

    _attn_all_blocks((tp - Q_BLOCK) // ATT_ROWS, block, finish)


def _attn_scratch(n_g, tp):
    g_rows = n_g * ATT_ROWS
    n_slots = (tp - Q_BLOCK) // ATT_ROWS
    return [pltpu.VMEM((g_rows, LANES), F32),
            pltpu.VMEM((n_slots, g_rows, ATT_ROWS), F32),
            pltpu.VMEM((g_rows, LANES), F32),
            pltpu.VMEM((g_rows, LANES), F32)]


def _mla(pc, qn, kvn, wuq, wukv, cos, sa, sb):
    b, tp, _ = pc.shape
    return pl.pallas_call(
        functools.partial(_mla_kernel, tp),
        grid=(b,),
        in_specs=[pl.BlockSpec((1, tp, PC_W), lambda i: (i, 0, 0)),
                  _resident((1, 256)), _resident((1, 256)),
                  _resident(wuq.shape), _resident(wukv.shape),
                  _resident(cos.shape), _resident(sa.shape), _resident(sb.shape)],
        out_specs=pl.BlockSpec((1, tp, 256), lambda i: (i, 0, 0)),
        out_shape=jax.ShapeDtypeStruct((b, tp, 256), BF16),
        scratch_shapes=[pltpu.VMEM((N_HEADS, tp, LANES), BF16),
                        pltpu.VMEM((N_HEADS, tp, LANES), BF16),
                        pltpu.VMEM((N_HEADS, tp, LANES), BF16)] + _attn_scratch(N_HEADS, tp),
        compiler_params=_cparams(),
        name="mla",
    )(pc, qn, kvn, wuq, wukv, cos, sa, sb)


def _diff_kernel(tp, lam_init, pd_ref, lam_ref, dn_ref, o_ref, qs_s, v_s, s_meta, s_s, mx_s, acc_s):
    n_maps = 2 * N_HEADS

    def fill_values(i, carry):
        rows = pl.ds(pl.multiple_of(i * Q_BLOCK, Q_BLOCK), Q_BLOCK)
        low = lax.broadcasted_iota(jnp.int32, (1, LANES), 1) < V_ONE
        for pair in range(N_HEADS // 2):
            two = pd_ref[0, rows, 512 + pair * LANES:512 + (pair + 1) * LANES].astype(F32)
            for h, vals in ((2 * pair, two), (2 * pair + 1, pltpu.roll(two, V_ONE, 1))):
                v_s[h, rows, :] = _with_ones_lane(jnp.where(low, vals, 0.0)).astype(BF16)
        return carry

    lax.fori_loop(0, tp // Q_BLOCK, fill_values, 0)
    scale = DIFF_DK ** -0.5 * LOG2E
    group = lax.broadcasted_iota(jnp.int32, (1, 256), 1) >> 5
    lv = lam_ref[...]
    lam = (jnp.exp(jnp.sum(lv[0:1, :] * lv[1:2, :], axis=-1, keepdims=True))
           - jnp.exp(jnp.sum(lv[2:3, :] * lv[3:4, :], axis=-1, keepdims=True)) + lam_init)
    dn = dn_ref[...]

    def stack_queries(q0, n_rows):
        q = (pd_ref[0, pl.ds(q0, n_rows), 0:256].astype(F32) * scale).astype(BF16)
        for g in range(n_maps):
            qs_s[g * n_rows:(g + 1) * n_rows, :] = jnp.where(group == g, q, jnp.zeros_like(q))

    def qk_fn(q0, n_rows, krows):
        return _dot_nt(qs_s[0:n_maps * n_rows, :], pd_ref[0, krows, 256:512])

    def v_fn(g, krows):
        return v_s[g // 2, krows, :]

    def emit(q0, n_rows, o, first):
        od = jnp.concatenate([o[(2 * h) * n_rows:(2 * h + 1) * n_rows, :]
                              - lam * o[(2 * h + 1) * n_rows:(2 * h + 2) * n_rows, :]
                              for h in range(N_HEADS)], axis=0)
        od = jnp.where(lax.broadcasted_iota(jnp.int32, (1, LANES), 1) < V_ONE, od, 0.0)
        ms = jnp.sum(od * od, axis=-1, keepdims=True) * (1.0 / 64)
        yh = od * lax.rsqrt(ms + EPS) * dn * (1.0 - lam_init)
        y = _heads_to_lanes([yh[h * n_rows:(h + 1) * n_rows, :] for h in range(N_HEADS)])
        if first:
            qrow = lax.broadcasted_iota(jnp.int32, (n_rows, 1), 0)
            y = jnp.where(qrow >= FRONT_PAD, y, 0.0)
        o_ref[0, pl.ds(q0, n_rows), :] = y.astype(BF16)

    stack_queries(0, Q_BLOCK)
    emit(0, Q_BLOCK, _attn_block0(qk_fn, v_fn, n_maps), True)

    def block(i, mid_fn):
        stack_queries(_attn_row0(i), ATT_ROWS)
        _attn_block(i, qk_fn, v_fn, n_maps, s_meta, s_s, mx_s, acc_s, mid_fn)

    def finish(i):
        emit(_attn_row0(i), ATT_ROWS, _attn_result(acc_s), False)

    _attn_all_blocks((tp - Q_BLOCK) // ATT_ROWS, block, finish)


def _diff(pd, lam_rows, dn, lam_init):
    b, tp, _ = pd.shape
    n_maps = 2 * N_HEADS
    return pl.pallas_call(
        functools.partial(_diff_kernel, tp, lam_init),
        grid=(b,),
        in_specs=[pl.BlockSpec((1, tp, PD_W), lambda i: (i, 0, 0)),
                  _resident(lam_rows.shape), _resident((1, LANES))],
        out_specs=pl.BlockSpec((1, tp, 256), lambda i: (i, 0, 0)),
        out_shape=jax.ShapeDtypeStruct((b, tp, 256), BF16),
        scratch_shapes=[pltpu.VMEM((n_maps * ATT_ROWS, 256), BF16),
                        pltpu.VMEM((N_HEADS, tp, LANES), BF16)] + _attn_scratch(n_maps, tp),
        compiler_params=_cparams(),
        name="diffattn",
    )(pd, lam_rows, dn)


def _pad_cols(x, width):
    return jnp.pad(x, ((0, 0), (0, width - x.shape[1])))


def _rot_split(w):
    d = w.shape[0]
    return w.reshape(d, N_HEADS, 2, 32).transpose(0, 2, 1, 3).reshape(d, 256)


def _layout_w_in(w):
    sizes = (128, 128, 256, 16, 256, 256, 256, 256, 256, 192, 64, 32, 256, 256, 256)
    offs = [0]
    for s_ in sizes:
        offs.append(offs[-1] + s_)
    seg = [w[:, offs[i]:offs[i + 1]] for i in range(len(sizes))]
    (a_q, a_k, a_v, a_lr, a_og, r_q, r_k, r_v, r_og, c_cq, c_ckv, c_kpe, d_q, d_k, d_v) = seg
    d = w.shape[0]
    z = lambda n: jnp.zeros((d, n), w.dtype)
    cols = [a_q, a_k, a_v, a_og,
            _rot_split(r_q), _rot_split(r_k), r_v, r_og,
            c_cq, c_ckv, a_lr, z(48), c_kpe, z(32),
            d_q, d_k, d_v]
    return jnp.concatenate(cols, axis=1).astype(BF16)


def _tables(tp):
    pos = jnp.arange(tp, dtype=F32) - FRONT_PAD
    inv = ROPE_THETA ** (-jnp.arange(32, dtype=F32) / 32)
    ang = pos[:, None] * inv[None, :]
    ret_cos = jnp.tile(jnp.cos(ang), (1, N_HEADS))
    ret_sin = jnp.tile(jnp.sin(ang), (1, N_HEADS))
    inv16 = ROPE_THETA ** (-jnp.arange(16, dtype=F32) / 16)
    ang16 = pos[:, None] * inv16[None, :]
    c16, s16 = jnp.cos(ang16), jnp.sin(ang16)
    one = lambda n: jnp.ones((tp, n), F32)
    zero = lambda n: jnp.zeros((tp, n), F32)
    mla_cos = jnp.concatenate([one(64), c16, c16, one(32)], axis=1)
    mla_sa = jnp.concatenate([zero(80), s16, zero(32)], axis=1)
    mla_sb = jnp.concatenate([zero(64), -s16, zero(48)], axis=1)
    lg = jnp.log(1.0 - jnp.exp2(-5.0 - jnp.arange(N_HEADS, dtype=F32)))
    idx = jnp.arange(RET_BLOCK, dtype=F32)
    rel = idx[:, None] - idx[None, :]
    dmask = jnp.where(rel[None] >= 0, jnp.exp(rel[None] * lg[:, None, None]), 0.0)
    dmask = dmask.reshape(N_HEADS * RET_BLOCK, RET_BLOCK)
    lane_head = (jnp.arange(256) % 128) // 32
    qfac = jnp.exp((idx[:, None] + 1.0) * lg[lane_head][None, :])
    kfac = jnp.exp((RET_BLOCK - 1.0 - idx[:, None]) * lg[lane_head][None, :])
    dec = jnp.exp(RET_BLOCK * lg[lane_head])[:, None]
    return ret_cos, ret_sin, mla_cos, mla_sa, mla_sb, dmask, qfac, kfac, dec


def kernel(x, meta_tokens, attn_norm, w_in, gla_w_gate, gla_b_gate, gla_norm, ret_norm, mla_q_norm, mla_w_uq, mla_kv_norm, mla_w_ukv, diff_lambda, diff_norm, w_out, ffn_norm, ffn_w_gate, ffn_w_up, ffn_w_down, moe_router, moe_w_gate, moe_w_up, moe_w_down, final_norm):
    b, seq, d = x.shape
    assert DEPTH % 2 == 0, "the expert layer (odd index) must be last: its combine step applies the final norm"
    tp = FRONT_PAD + N_META + seq
    n = b * tp
    meta = jnp.broadcast_to(meta_tokens[None].astype(x.dtype), (b, N_META, d))
    h = jnp.concatenate([jnp.zeros((b, FRONT_PAD, d), x.dtype), meta, x], axis=1).reshape(n, d)
    ret_cos, ret_sin, mla_cos, mla_sa, mla_sb, dmask, qfac, kfac, dec = _tables(tp)

    for li in range(DEPTH):
        pa, pb, pc, pd = _inproj(h, attn_norm[li][None, :], _layout_w_in(w_in[li]))
        pa, pb, pc, pd = (p.reshape(b, tp, -1) for p in (pa, pb, pc, pd))

        wgate = jnp.pad(gla_w_gate[li], ((0, LANES - GLA_GATE_RANK), (0, 0)))
        o_a = _gla(pa, pc, wgate, gla_b_gate[li][None, :], jnp.tile(gla_norm[li], N_HEADS)[None, :])
        o_b = _ret(pb, ret_cos, ret_sin, dmask, qfac, kfac, dec, jnp.tile(ret_norm[li], N_HEADS)[None, :])

        qn = _pad_cols(mla_q_norm[li][None, :], 256)
        kvn = jnp.pad(mla_kv_norm[li][None, :], ((0, 0), (MLA_Q_RANK, 0)))
        wuq = jnp.pad(mla_w_uq[li].reshape(MLA_Q_RANK, N_HEADS, MLA_NOPE + MLA_ROPE),
                      ((0, 256 - MLA_Q_RANK), (0, 0), (0, LANES - MLA_NOPE - MLA_ROPE)))
        wuq = wuq.reshape(256, N_HEADS * LANES).astype(BF16)
        wukv = mla_w_ukv[li].reshape(MLA_KV_RANK, N_HEADS, 2, 64)
        wukv = jnp.pad(wukv, ((MLA_Q_RANK, 0), (0, 0), (0, 0), (0, 64)))
        wukv = wukv.transpose(0, 2, 1, 3).reshape(256, 2 * N_HEADS * LANES).astype(BF16)
        o_c = _mla(pc, qn, kvn, wuq, wukv, mla_cos, mla_sa, mla_sb)

        lam_init = 0.8 - 0.6 * math.exp(-0.3 * li)
        o_d = _diff(pd, diff_lambda[li], _pad_cols(diff_norm[li][None, :], LANES), lam_init)

        o_a, o_b, o_c, o_d = (o.reshape(n, 256) for o in (o_a, o_b, o_c, o_d))
        wo = w_out[li].astype(BF16)
        fn = ffn_norm[li][None, :]
        j = li // 2
        if li % 2 == 0:
            h, hn = _outproj(h, o_a, o_b, o_c, o_d, wo, fn)
            h = _swiglu(h, hn, ffn_w_gate[j].astype(BF16), ffn_w_up[j].astype(BF16),
                        ffn_w_down[j].astype(BF16))
        else:
            h, hn, info, cnt = _outproj(h, o_a, o_b, o_c, o_d, wo, fn,
                                        router=_pad_cols(moe_router[j], LANES), seq_rows=tp)
            return _routed_moe_norm(h, hn, info, cnt, moe_w_gate[j].astype(BF16),
                                    moe_w_up[j].astype(BF16), moe_w_down[j].astype(BF16),
                                    final_norm[None, :], b, tp)
```

```python
import functools
import math

import jax
import jax.numpy as jnp
from jax import lax
from jax.experimental import pallas as pl
from jax.experimental.pallas import tpu as pltpu

F32 = jnp.float32
BF16 = jnp.bfloat16

D_MODEL = 1024
DEPTH = 2
N_META = 16
CHUNK = 64
Q_BLOCK = 128
FRONT_PAD = Q_BLOCK - N_META
EPS = 1e-6
NEG = -1e30
ROPE_THETA = 10000.0
N_HEADS = 4
GLA_DK = 32
GLA_DV = 64
GLA_GATE_RANK = 16
GLA_TAU = 16.0
RET_DK = 64
MLA_Q_RANK = 192
MLA_KV_RANK = 64
MLA_NOPE = 64
MLA_ROPE = 32
DIFF_DK = 32
N_EXPERTS = 8

LANES = 128
SUBLANES = 8
ROW_TILE = 512
FF_CHUNK = 256
PREFETCH_TAIL_CHUNKS = 4
VMEM_LIMIT = 56 * 1024 * 1024

PA_W = 768
PB_W = 1024
PC_W = 384
PD_W = 768


def _cparams(n_axes=1):
    return pltpu.CompilerParams(dimension_semantics=("arbitrary",) * n_axes,
                                vmem_limit_bytes=VMEM_LIMIT)


def _resident(shape):
    nd = len(shape)
    return pl.BlockSpec(shape, lambda *_: (0,) * nd, pipeline_mode=pl.Buffered(1))


def _sigmoid(x):
    return 1.0 / (1.0 + jnp.exp(-x))


def _split_bf16(x):
    hi = x.astype(BF16)
    lo = (x - hi.astype(F32)).astype(BF16)
    return hi, lo


def _dot(a, b):
    return jnp.dot(a, b, preferred_element_type=F32)


def _dot_nt(a, b):
    return lax.dot_general(a, b, (((1,), (1,)), ((), ())), preferred_element_type=F32)


def _inproj_kernel(h_ref, g_ref, w_ref, pa_ref, pb_ref, pc_ref, pd_ref):
    x = h_ref[...]
    ms = jnp.mean(x * x, axis=-1, keepdims=True)
    y = (x * lax.rsqrt(ms + EPS) * g_ref[...]).astype(BF16)
    off = 0
    for o_ref, width in ((pa_ref, PA_W), (pb_ref, PB_W), (pc_ref, PC_W), (pd_ref, PD_W)):
        o_ref[...] = _dot(y, w_ref[:, off:off + width]).astype(BF16)
        off += width


def _inproj(h, g, w):
    n = h.shape[0]
    wtot = PA_W + PB_W + PC_W + PD_W
    row = lambda width: pl.BlockSpec((ROW_TILE, width), lambda i: (i, 0))
    return pl.pallas_call(
        _inproj_kernel,
        grid=(n // ROW_TILE,),
        in_specs=[row(D_MODEL), _resident((1, D_MODEL)), _resident((D_MODEL, wtot))],
        out_specs=[row(PA_W), row(PB_W), row(PC_W), row(PD_W)],
        out_shape=[jax.ShapeDtypeStruct((n, w_), BF16) for w_ in (PA_W, PB_W, PC_W, PD_W)],
        compiler_params=_cparams(),
        name="inproj",
    )(h, g, w)


def _outproj_router_kernel(seq_rows, h_ref, oa_ref, ob_ref, oc_ref, od_ref, wo_ref, fn_ref, router_ref,
                           below_ref, hmid_ref, hn_ref, info_ref, cnt_ref, carry_ref):
    o = jnp.concatenate([oa_ref[...], ob_ref[...], oc_ref[...], od_ref[...]], axis=1)
    hm = h_ref[...] + _dot(o, wo_ref[...])
    hmid_ref[...] = hm
    ms = jnp.mean(hm * hm, axis=-1, keepdims=True)
    y = hm * lax.rsqrt(ms + EPS) * fn_ref[...]
    hn_ref[...] = y
    y_hi, y_lo = _split_bf16(y)
    r_hi, r_lo = _split_bf16(router_ref[...])
    logits = _dot(y_hi, r_hi) + _dot(y_hi, r_lo) + _dot(y_lo, r_hi)
    lane = lax.broadcasted_iota(jnp.int32, logits.shape, 1).astype(F32)
    ninf = float("-inf")
    logits = jnp.where(lane < N_EXPERTS, logits, ninf)
    m1 = jnp.max(logits, axis=-1, keepdims=True)
    i1 = jnp.min(jnp.where(logits == m1, lane, float(LANES)), axis=-1, keepdims=True)
    rest_l = jnp.where(lane == i1, ninf, logits)
    m2 = jnp.max(rest_l, axis=-1, keepdims=True)
    i2 = jnp.min(jnp.where(rest_l == m2, lane, float(LANES)), axis=-1, keepdims=True)
    e2 = jnp.exp(m2 - m1)
    den = 1.0 + e2

    @pl.when(pl.program_id(0) == 0)
    def _():
        carry_ref[...] = jnp.zeros_like(carry_ref)

    sel = jnp.where(lane == i1, 1.0, 0.0) + jnp.where(lane == i2, 1.0, 0.0)
    rows = sel.shape[0]
    seq_pos = (lax.rem(pl.program_id(0) * rows, seq_rows)
               + lax.broadcasted_iota(jnp.int32, (rows, 1), 0))
    seq_pos = jnp.where(seq_pos >= seq_rows, seq_pos - seq_rows, seq_pos)
    sel = jnp.where(seq_pos >= Q_BLOCK, sel, 0.0)
    count = _dot(below_ref[...], sel.astype(BF16)) + carry_ref[...]
    r1 = jnp.sum(jnp.where(lane == i1, count, 0.0), axis=-1, keepdims=True)
    r2 = jnp.sum(jnp.where(lane == i2, count, 0.0), axis=-1, keepdims=True)
    total = carry_ref[...] + jnp.sum(sel, axis=0, keepdims=True)
    carry_ref[...] = total
    cnt_ref[...] = jnp.broadcast_to(total, cnt_ref.shape)
    info = jnp.where(lane == 0, i1, 0.0)
    for k, val in enumerate((i2, 1.0 / den, e2 / den, r1, r2), start=1):
        info = jnp.where(lane == k, val, info)
    info_ref[...] = info


def _outproj_router(h, oa, ob, oc, od, wo, fn, router, seq_rows):
    n = h.shape[0]
    row = lambda width: pl.BlockSpec((ROW_TILE, width), lambda i: (i, 0))
    return pl.pallas_call(
        functools.partial(_outproj_router_kernel, seq_rows),
        grid=(n // ROW_TILE,),
        in_specs=[row(D_MODEL), row(256), row(256), row(256), row(256),
                  _resident((D_MODEL, D_MODEL)), _resident((1, D_MODEL)),
                  _resident((D_MODEL, LANES)), _resident((ROW_TILE, ROW_TILE))],
        out_specs=[row(D_MODEL), row(D_MODEL), row(LANES), pl.BlockSpec((8, LANES), lambda i: (0, 0))],
        out_shape=[jax.ShapeDtypeStruct((n, D_MODEL), F32), jax.ShapeDtypeStruct((n, D_MODEL), F32),
                   jax.ShapeDtypeStruct((n, LANES), F32), jax.ShapeDtypeStruct((8, LANES), F32)],
        scratch_shapes=[pltpu.VMEM((1, LANES), F32)],
        compiler_params=_cparams(),
        name="outproj_router",
    )(h, oa, ob, oc, od, wo, fn, router, jnp.tri(ROW_TILE, k=-1, dtype=BF16))


def _inverse_map_kernel(zs_ref, ze_ref, pos_ref, tok_ref):
    i = pl.program_id(0)

    @pl.when(i == 0)
    def _():
        def zero(j, carry):
            tok_ref[j] = 0
            return carry

        for k in range(zs_ref.shape[0]):
            lax.fori_loop(zs_ref[k], ze_ref[k], zero, 0)

    base = i * ROW_TILE

    for r in range(ROW_TILE):
        tok_ref[pos_ref[0, 0, 2 * r]] = base + r
        tok_ref[pos_ref[0, 0, 2 * r + 1]] = base + r


def _inverse_map(zero_start, zero_end, pos, n_sorted):
    return pl.pallas_call(
        _inverse_map_kernel,
        grid_spec=pltpu.PrefetchScalarGridSpec(
            num_scalar_prefetch=2,
            grid=(pos.shape[0],),
            in_specs=[pl.BlockSpec((1, 1, 2 * ROW_TILE), lambda i, zs, ze: (i, 0, 0),
                                   memory_space=pltpu.SMEM)],
            out_specs=pl.BlockSpec(memory_space=pltpu.SMEM)),
        out_shape=jax.ShapeDtypeStruct((n_sorted + ROW_TILE,), jnp.int32),
        compiler_params=_cparams(),
        name="moe_inverse_map",
    )(zero_start, zero_end, pos)


def _swiglu_tile(x, wg, wu, wd, acc_ref, per_chunk=None):
    n_chunks = wd.shape[0] // FF_CHUNK
    for c in range(n_chunks):
        cols = slice(c * FF_CHUNK, (c + 1) * FF_CHUNK)
        g = _dot(x, wg[:, cols])
        u = _dot(x, wu[:, cols])
        part = _dot((g * _sigmoid(g) * u).astype(BF16), wd[cols, :])
        if c == 0:
            acc_ref[...] = part
        elif c < n_chunks - 1:
            acc_ref[...] += part
        if per_chunk is not None:
            per_chunk(c, n_chunks)
    return acc_ref[...] + part


def _experts_kernel(te_ref, tv_ref, tok_ref, tok_next_ref, hn_ref, wg_ref, wu_ref, wd_ref, ys_ref,
                    x_buf, acc_ref, sem):
    i = pl.program_id(0)
    last = pl.num_programs(0) - 1
    slot = i & 1
    n_blocks = x_buf.shape[1]

    def row_copy(t_ref, s, rb, u):
        tok = t_ref[0, 0, rb * SUBLANES + u]
        return pltpu.make_async_copy(hn_ref.at[pl.ds(tok, 1), :], x_buf.at[s, rb, pl.ds(u, 1), :],
                                     sem.at[s])

    def wait_tile(s):
        pltpu.make_async_copy(x_buf.at[s], x_buf.at[s], sem.at[s]).wait()

    @pl.when(i == 0)
    def _():
        def start_block(rb, carry):
            for u in range(SUBLANES):
                row_copy(tok_ref, 0, rb, u).start(priority=u % 2)
            return carry

        lax.fori_loop(0, n_blocks, start_block, 0)

    @pl.when(jnp.logical_or(i == 0, tv_ref[jnp.maximum(i - 1, 0)] > 0))
    def _():
        wait_tile(slot)

    @pl.when(tv_ref[i] > 0)
    def _():
        def prefetch(c, n_chunks):
            per = -(-ROW_TILE // max(n_chunks - PREFETCH_TAIL_CHUNKS, 1))
            for r in range(c * per, min((c + 1) * per, ROW_TILE)):
                row_copy(tok_next_ref, 1 - slot, r // SUBLANES, r % SUBLANES).start(priority=r % 2)

        x = x_buf[slot].reshape(ROW_TILE, -1).astype(BF16)
        ys_ref[...] = _swiglu_tile(x, wg_ref.at[0], wu_ref.at[0], wd_ref.at[0], acc_ref, prefetch)

    @pl.when(tv_ref[i] == 0)
    def _():
        ys_ref[...] = jnp.zeros_like(ys_ref)

    @pl.when(jnp.logical_and(i == last, tv_ref[i] > 0))
    def _():
        wait_tile(1 - slot)


def _experts(tile_expert, tile_valid, tok, hn, wg, wu, wd):
    n_tiles = tok.shape[0]
    d = hn.shape[1]
    wspec = lambda shape: pl.BlockSpec((1,) + shape[1:], lambda i, te, tv: (te[i], 0, 0),
                                       pipeline_mode=pl.Buffered(1))
    tok_spec = lambda index: pl.BlockSpec((1, 1, ROW_TILE), index, memory_space=pltpu.SMEM)
    return pl.pallas_call(
        _experts_kernel,
        grid_spec=pltpu.PrefetchScalarGridSpec(
            num_scalar_prefetch=2,
            grid=(n_tiles,),
            in_specs=[tok_spec(lambda i, te, tv: (i, 0, 0)),
                      tok_spec(lambda i, te, tv: (jnp.minimum(i + 1, n_tiles - 1), 0, 0)),
                      pl.BlockSpec(memory_space=pl.ANY),
                      wspec(wg.shape), wspec(wu.shape), wspec(wd.shape)],
            out_specs=pl.BlockSpec((ROW_TILE, d), lambda i, te, tv: (i, 0)),
            scratch_shapes=[pltpu.VMEM((2, ROW_TILE // SUBLANES, SUBLANES, d), F32),
                            pltpu.VMEM((ROW_TILE, d), F32),
                            pltpu.SemaphoreType.DMA((2,))]),
        out_shape=jax.ShapeDtypeStruct((n_tiles * ROW_TILE, d), F32),
        compiler_params=_cparams(),
        name="moe_experts",
    )(tile_expert, tile_valid, tok, tok, hn, wg, wu, wd)


def _combine_norm_kernel(tp, pos_ref, h_ref, info_ref, ys_ref, g_ref, o_ref,
                         h_buf, info_buf, y_buf, sem, row_sem):
    n_rows = h_buf.shape[0]
    row0 = pl.multiple_of(pl.program_id(0) * tp + Q_BLOCK + pl.program_id(1) * n_rows, Q_BLOCK)
    h_copy = pltpu.make_async_copy(h_ref.at[pl.ds(row0, n_rows), :], h_buf, sem.at[0])
    info_copy = pltpu.make_async_copy(info_ref.at[pl.ds(row0, n_rows), :], info_buf, sem.at[1])
    h_copy.start()
    info_copy.start()

    def row_copy(rb, u, k):
        p = pos_ref[0, 0, rb * (2 * SUBLANES) + 2 * u + k]
        return pltpu.make_async_copy(ys_ref.at[pl.ds(p, 1), :], y_buf.at[k, rb, pl.ds(u, 1), :], row_sem)

    for rb in range(n_rows // SUBLANES):
        for u in range(SUBLANES):
            row_copy(rb, u, 0).start(priority=0)
            row_copy(rb, u, 1).start(priority=1)
    pltpu.make_async_copy(y_buf, y_buf, row_sem).wait()
    h_copy.wait()
    info_copy.wait()
    info = info_buf[...]
    y0 = y_buf[0].reshape(n_rows, -1)
    y1 = y_buf[1].reshape(n_rows, -1)
    x = h_buf[...] + info[:, 2:3] * y0 + info[:, 3:4] * y1
    ms = jnp.mean(x * x, axis=-1, keepdims=True)
    o_ref[0] = x * lax.rsqrt(ms + EPS) * g_ref[...]


def _combine_norm(pos, h, info, ys, g, b, tp):
    n, d = h.shape
    seq = tp - Q_BLOCK
    tiles = seq // ROW_TILE
    any_spec = pl.BlockSpec(memory_space=pl.ANY)
    return pl.pallas_call(
        functools.partial(_combine_norm_kernel, tp),
        grid=(b, tiles),
        in_specs=[pl.BlockSpec((1, 1, 2 * ROW_TILE), lambda i, j: (i * tiles + j, 0, 0),
                               memory_space=pltpu.SMEM),
                  any_spec, any_spec, any_spec, _resident((1, d))],
        out_specs=pl.BlockSpec((1, ROW_TILE, d), lambda i, j: (i, j, 0)),
        out_shape=jax.ShapeDtypeStruct((b, seq, d), F32),
        scratch_shapes=[pltpu.VMEM((ROW_TILE, d), F32), pltpu.VMEM((ROW_TILE, LANES), F32),
                        pltpu.VMEM((2, ROW_TILE // SUBLANES, SUBLANES, d), F32),
                        pltpu.SemaphoreType.DMA((2,)), pltpu.SemaphoreType.DMA(())],
        compiler_params=_cparams(2),
        name="moe_combine_norm",
    )(pos, h, info, ys, g)


def _routed_moe_norm(h, hn, info, cnt, wg, wu, wd, g, b, tp):
    n = h.shape[0]
    n_routed = b * (tp - Q_BLOCK)
    n_sorted = 2 * n_routed + N_EXPERTS * ROW_TILE
    n_tiles = n_sorted // ROW_TILE
    counts = cnt[0, :N_EXPERTS].astype(jnp.int32)
    tiles_e = (counts + ROW_TILE - 1) // ROW_TILE
    tile_end = jnp.cumsum(tiles_e)
    row_base = (tile_end - tiles_e) * ROW_TILE
    e_idx = info[:, 0:2].astype(jnp.int32)
    pos = row_base[e_idx] + info[:, 4:6].astype(jnp.int32)
    routed = (jnp.arange(n, dtype=jnp.int32) % tp >= Q_BLOCK)[:, None]
    zero_start = jnp.concatenate([row_base + counts, tile_end[-1:] * ROW_TILE])
    zero_end = jnp.concatenate([tile_end * ROW_TILE, jnp.full((1,), n_sorted + ROW_TILE, jnp.int32)])
    tok = _inverse_map(zero_start, zero_end,
                       jnp.where(routed, pos, n_sorted).reshape(n // ROW_TILE, 1, 2 * ROW_TILE),
                       n_sorted)[:n_sorted]
    tile_ids = jnp.arange(n_tiles, dtype=jnp.int32)
    tile_expert = jnp.minimum(jnp.sum((tile_ids[:, None] >= tile_end[None, :]).astype(jnp.int32), axis=1),
                              N_EXPERTS - 1)
    tile_valid = (tile_ids < tile_end[-1]).astype(jnp.int32)
    ys = _experts(tile_expert, tile_valid, tok.reshape(n_tiles, 1, ROW_TILE), hn, wg, wu, wd)
    pos_seq = pos.reshape(b, tp, 2)[:, Q_BLOCK:, :].reshape(-1, 1, 2 * ROW_TILE)
    return _combine_norm(pos_seq, h, info, ys, g, b, tp)


def _outproj_swiglu_kernel(h_ref, oa_ref, ob_ref, oc_ref, od_ref, wo_ref, fn_ref, wg_ref, wu_ref, wd_ref,
                           o_ref, acc_ref):
    o = jnp.concatenate([oa_ref[...], ob_ref[...], oc_ref[...], od_ref[...]], axis=1)
    hm = h_ref[...] + _dot(o, wo_ref[...])
    ms = jnp.mean(hm * hm, axis=-1, keepdims=True)
    hn = (hm * lax.rsqrt(ms + EPS) * fn_ref[...]).astype(BF16)
    o_ref[...] = hm + _swiglu_tile(hn, wg_ref, wu_ref, wd_ref, acc_ref)


def _outproj_swiglu(h, oa, ob, oc, od, wo, fn, wg, wu, wd):
    n = h.shape[0]
    row = lambda width: pl.BlockSpec((ROW_TILE, width), lambda i: (i, 0))
    return pl.pallas_call(
        _outproj_swiglu_kernel,
        grid=(n // ROW_TILE,),
        in_specs=[row(D_MODEL), row(256), row(256), row(256), row(256),
                  _resident(wo.shape), _resident(fn.shape),
                  _resident(wg.shape), _resident(wu.shape), _resident(wd.shape)],
        out_specs=row(D_MODEL),
        out_shape=jax.ShapeDtypeStruct((n, D_MODEL), F32),
        scratch_shapes=[pltpu.VMEM((ROW_TILE, D_MODEL), F32)],
        input_output_aliases={0: 0},
        compiler_params=_cparams(),
        name="outproj_swiglu",
    )(h, oa, ob, oc, od, wo, fn, wg, wu, wd)


def _group_ones(n, group_shift):
    r = lax.broadcasted_iota(jnp.int32, (n, n), 0) >> group_shift
    c = lax.broadcasted_iota(jnp.int32, (n, n), 1) >> group_shift
    return jnp.where(r == c, 1.0, 0.0).astype(BF16)


GLA_ROWS = 256


def _gla_kernel(tp, pa_ref, lr_ref, wg_ref, bg_ref, gn_ref, o_ref, s_ref):
    qhead = lax.broadcasted_iota(jnp.int32, (1, N_HEADS * GLA_DK), 1) >> 5
    ehead = lax.broadcasted_iota(jnp.int32, (1, N_HEADS * GLA_DV), 1) >> 6
    shead = lax.broadcasted_iota(jnp.int32, (N_HEADS * GLA_DK, 1), 0) >> 5
    bd = shead == ehead
    gsum = _group_ones(N_HEADS * GLA_DV, 6)
    wg_hi, wg_lo = _split_bf16(wg_ref[...])
    bg = bg_ref[...]
    gn = gn_ref[...]
    scale = GLA_DK ** -0.5
    s_ref[...] = jnp.zeros_like(s_ref)

    def block(r0, n_rows):
        n_c = n_rows // CHUNK
        rows = pl.ds(r0, n_rows)
        ri = lax.broadcasted_iota(jnp.int32, (n_rows, n_rows), 0)
        ci = lax.broadcasted_iota(jnp.int32, (n_rows, n_rows), 1)
        same = (ri >> 6) == (ci >> 6)
        tri_bf = jnp.where(jnp.logical_and(same, ri >= ci), 1.0, 0.0).astype(BF16)
        ones_bf = jnp.where(same, 1.0, 0.0).astype(BF16)
        r4 = lax.broadcasted_iota(jnp.int32, (N_HEADS * n_rows, n_rows), 0) & (n_rows - 1)
        c4 = lax.broadcasted_iota(jnp.int32, (N_HEADS * n_rows, n_rows), 1)
        tri4 = jnp.logical_and((r4 >> 6) == (c4 >> 6), r4 >= c4)
        col_chunk = lax.broadcasted_iota(jnp.int32, (1, n_rows), 1) >> 6

        q = pa_ref[0, rows, 0:128].astype(F32) * scale
        k = pa_ref[0, rows, 128:256].astype(F32)
        v = pa_ref[0, rows, 256:512]
        og = pa_ref[0, rows, 512:768].astype(F32)
        lr = lr_ref[0, rows, :]
        valid = (r0 + lax.broadcasted_iota(jnp.int32, (n_rows, 1), 0)) >= FRONT_PAD

        pre = _dot(lr, wg_hi) + _dot(lr, wg_lo) + bg
        logsig = jnp.minimum(pre, 0.0) - jnp.log1p(jnp.exp(-jnp.abs(pre)))
        g = jnp.where(valid, logsig * (1.0 / GLA_TAU), 0.0)
        g_hi, g_lo = _split_bf16(g)
        cum = _dot(tri_bf, g_hi) + _dot(tri_bf, g_lo)
        cum_end = _dot(ones_bf, g_hi) + _dot(ones_bf, g_lo)
        qt_bf = (q * jnp.exp(cum)).astype(BF16)
        kt = (k * jnp.exp(-cum)).astype(BF16)
        kd_t = (k * jnp.exp(cum_end - cum)).T.astype(BF16)
        dec_t = jnp.exp(cum_end.T)

        qs = jnp.concatenate([jnp.where(qhead == h, qt_bf, jnp.zeros_like(qt_bf))
                              for h in range(N_HEADS)], axis=0)
        a = jnp.where(tri4, _dot_nt(qs, kt), 0.0).astype(BF16)
        r = _dot(a, v)
        o = jnp.where(ehead == 0, r[0:n_rows, :], 0.0)
        for h in range(1, N_HEADS):
            o = o + jnp.where(ehead == h, r[h * n_rows:(h + 1) * n_rows, :], 0.0)

        s = s_ref[...]
        inter = []
        for c in range(n_c):
            inter.append(_dot(qt_bf[c * CHUNK:(c + 1) * CHUNK, :], s.astype(BF16)))
            upd = _dot(jnp.where(col_chunk == c, kd_t, jnp.zeros_like(kd_t)), v)
            s = dec_t[:, c * CHUNK:c * CHUNK + 1] * s + jnp.where(bd, upd, 0.0)
        s_ref[...] = s
        o = o + jnp.concatenate(inter, axis=0)

        ms = _dot((o * o).astype(BF16), gsum) * (1.0 / GLA_DV)
        y = o * lax.rsqrt(ms + EPS) * gn * (og * _sigmoid(og))
        o_ref[0, rows, :] = jnp.where(valid, y, 0.0).astype(BF16)

    block(0, Q_BLOCK)

    def body(i, carry):
        block(pl.multiple_of(Q_BLOCK + i * GLA_ROWS, Q_BLOCK), GLA_ROWS)
        return carry

    lax.fori_loop(0, (tp - Q_BLOCK) // GLA_ROWS, body, 0)


def _gla(pa, pc, wg, bg, gn):
    b, tp, _ = pa.shape
    return pl.pallas_call(
        functools.partial(_gla_kernel, tp),
        grid=(b,),
        in_specs=[pl.BlockSpec((1, tp, PA_W), lambda i: (i, 0, 0)),
                  pl.BlockSpec((1, tp, LANES), lambda i: (i, 0, 2)),
                  _resident((LANES, LANES)), _resident((1, LANES)), _resident((1, 256))],
        out_specs=pl.BlockSpec((1, tp, 256), lambda i: (i, 0, 0)),
        out_shape=jax.ShapeDtypeStruct((b, tp, 256), BF16),
        scratch_shapes=[pltpu.VMEM((N_HEADS * GLA_DK, N_HEADS * GLA_DV), F32)],
        compiler_params=_cparams(),
        name="gla",
    )(pa, pc, wg, bg, gn)


RET_BLOCK = 128


def _ret_kernel(tp, pb_ref, cos_ref, sin_ref, dmask_ref, qfac_ref, kfac_ref, dec_ref, gn_ref,
                o_ref, s_ref):
    blk = RET_BLOCK
    n_blocks = tp // blk
    qhead = (lax.broadcasted_iota(jnp.int32, (1, 256), 1) & 127) >> 5
    ehead = lax.broadcasted_iota(jnp.int32, (1, 256), 1) >> 6
    shead = (lax.broadcasted_iota(jnp.int32, (256, 1), 0) & 127) >> 5
    bd = shead == ehead
    gsum = _group_ones(256, 6)
    gn = gn_ref[...]
    s_ref[...] = jnp.zeros_like(s_ref)

    def rope(x, cos, sin):
        x1, x2 = x[:, :128], x[:, 128:]
        return jnp.concatenate([x1 * cos - x2 * sin, x1 * sin + x2 * cos], axis=1)

    def block(j, carry):
        r0 = pl.multiple_of(j * blk, blk)
        rows = pl.ds(r0, blk)
        cos = cos_ref[rows, :]
        sin = sin_ref[rows, :]
        q = rope(pb_ref[0, rows, 0:256].astype(F32), cos, sin)
        k = rope(pb_ref[0, rows, 256:512].astype(F32), cos, sin) * (RET_DK ** -0.5)
        v = pb_ref[0, rows, 512:768]
        og = pb_ref[0, rows, 768:1024].astype(F32)
        valid = (r0 + lax.broadcasted_iota(jnp.int32, (blk, 1), 0)) >= FRONT_PAD

        q_bf = q.astype(BF16)
        qs = jnp.concatenate([jnp.where(qhead == h, q_bf, jnp.zeros_like(q_bf))
                              for h in range(N_HEADS)], axis=0)
        a = (_dot_nt(qs, k.astype(BF16)) * dmask_ref[...]).astype(BF16)
        r = _dot(a, v)
        o = _dot((q * qfac_ref[...]).astype(BF16), s_ref[...].astype(BF16))
        for h in range(N_HEADS):
            o = o + jnp.where(ehead == h, r[h * blk:(h + 1) * blk, :], 0.0)

        kd_t = (k * kfac_ref[...]).T.astype(BF16)
        upd = _dot(kd_t, v)
        s_ref[...] = dec_ref[...] * s_ref[...] + jnp.where(bd, upd, 0.0)

        mu = _dot(o.astype(BF16), gsum) * (1.0 / 64)
        xc = o - mu
        var = _dot((xc * xc).astype(BF16), gsum) * (1.0 / 64)
        y = xc * lax.rsqrt(var + EPS) * gn * (og * _sigmoid(og))
        o_ref[0, rows, :] = jnp.where(valid, y, 0.0).astype(BF16)
        return carry

    lax.fori_loop(0, n_blocks, block, 0, unroll=4)


def _ret(pb, cos, sin, dmask, qfac, kfac, dec, gn):
    b, tp, _ = pb.shape
    return pl.pallas_call(
        functools.partial(_ret_kernel, tp),
        grid=(b,),
        in_specs=[pl.BlockSpec((1, tp, PB_W), lambda i: (i, 0, 0)),
                  _resident(cos.shape), _resident(sin.shape), _resident(dmask.shape),
                  _resident(qfac.shape), _resident(kfac.shape), _resident(dec.shape),
                  _resident((1, 256))],
        out_specs=pl.BlockSpec((1, tp, 256), lambda i: (i, 0, 0)),
        out_shape=jax.ShapeDtypeStruct((b, tp, 256), BF16),
        scratch_shapes=[pltpu.VMEM((256, 256), F32)],
        compiler_params=_cparams(),
        name="retention",
    )(pb, cos, sin, dmask, qfac, kfac, dec, gn)


ATT_ROWS = 256
LOG2E = 1.4426950408889634
V_ONE = 64


def _aligned(x, m):
    return x if isinstance(x, int) else pl.multiple_of(x, m)


def _for_chunks(n, body):
    if isinstance(n, int):
        for j in range(n):
            body(j)
        return

    def quad(t, carry):
        for u in range(4):
            body(4 * t + u)
        return carry

    lax.fori_loop(0, n >> 2, quad, 0)
    done = (n >> 2) << 2

    @pl.when((n & 2) != 0)
    def _():
        body(done)
        body(done + 1)

    @pl.when((n & 1) != 0)
    def _():
        body(done + (n & 2))


def _mask_groups(mask, s, n_g):
    rows = s.shape[0] // n_g
    return jnp.concatenate([jnp.where(mask, s[g * rows:(g + 1) * rows, :], NEG)
                            for g in range(n_g)], axis=0)


def _attn_block0(qk_fn, v_fn, n_g):
    rows = pl.ds(0, Q_BLOCK)
    qrow = lax.broadcasted_iota(jnp.int32, (Q_BLOCK, 1), 0)
    kcol = lax.broadcasted_iota(jnp.int32, (1, Q_BLOCK), 1)
    mask = jnp.logical_and(kcol <= qrow, kcol >= FRONT_PAD)
    s = _mask_groups(mask, qk_fn(0, Q_BLOCK, rows), n_g)
    p_bf = jnp.exp2(s - jnp.max(s, axis=-1, keepdims=True)).astype(BF16)
    pv = jnp.concatenate([_dot(p_bf[g * Q_BLOCK:(g + 1) * Q_BLOCK, :], v_fn(g, rows))
                          for g in range(n_g)], axis=0)
    return pv / pv[:, V_ONE:V_ONE + 1]


def _attn_row0(i):
    return _aligned(Q_BLOCK + (i - 1) * ATT_ROWS, Q_BLOCK)


def _attn_result(acc_s):
    acc = acc_s[...]
    return acc / acc[:, V_ONE:V_ONE + 1]


def _attn_block(i, qk_fn, v_fn, n_g, s_meta, s_s, mx_s, acc_s, mid_fn=None):
    rr = ATT_ROWS
    q0 = _attn_row0(i)
    meta_rows = pl.ds(0, Q_BLOCK)
    kcol = lax.broadcasted_iota(jnp.int32, (1, Q_BLOCK), 1)

    def key_rows(j):
        return pl.ds(_aligned(Q_BLOCK + j * rr, Q_BLOCK), rr)

    s = jnp.where(kcol >= FRONT_PAD, qk_fn(q0, rr, meta_rows), NEG)
    s_meta[...] = s
    mx_s[...] = s

    def pass1(j):
        sj = qk_fn(q0, rr, key_rows(j))
        s_s[j] = sj
        mx_s[...] = jnp.maximum(mx_s[...], jnp.maximum(sj[:, :LANES], sj[:, LANES:]))

    _for_chunks(i - 1, pass1)
    if mid_fn is not None:
        mid_fn()
    causal = (lax.broadcasted_iota(jnp.int32, (rr, rr), 1)
              <= lax.broadcasted_iota(jnp.int32, (rr, rr), 0))
    sd = _mask_groups(causal, qk_fn(q0, rr, pl.ds(q0, rr)), n_g)
    s_s[i - 1] = sd
    m = jnp.max(jnp.maximum(mx_s[...], jnp.maximum(sd[:, :LANES], sd[:, LANES:])),
                axis=-1, keepdims=True)
    mx_s[...] = jnp.broadcast_to(m, mx_s.shape)

    p_bf = jnp.exp2(s_meta[...] - mx_s[...]).astype(BF16)
    for g in range(n_g):
        acc_s[g * rr:(g + 1) * rr, :] = _dot(p_bf[g * rr:(g + 1) * rr, :], v_fn(g, meta_rows))

    def pass2(j):
        sj = s_s[j]
        mrep = mx_s[...]
        p0 = jnp.exp2(sj[:, :LANES] - mrep)
        p1 = jnp.exp2(sj[:, LANES:] - mrep)
        pj = jnp.concatenate([p0.astype(BF16), p1.astype(BF16)], axis=1)
        for g in range(n_g):
            acc_s[g * rr:(g + 1) * rr, :] += _dot(pj[g * rr:(g + 1) * rr, :], v_fn(g, key_rows(j)))

    _for_chunks(i, pass2)


def _attn_all_blocks(n_q, block_fn, finish_fn):
    block_fn(1, None)

    def body(i, carry):
        block_fn(i, lambda: finish_fn(i - 1))
        return carry

    lax.fori_loop(2, n_q + 1, body, 0)
    finish_fn(n_q)


def _heads_to_lanes(per_head):
    low = lax.broadcasted_iota(jnp.int32, (1, LANES), 1) < 64
    lo = jnp.where(low, per_head[0], pltpu.roll(per_head[1], 64, 1))
    hi = jnp.where(low, per_head[2], pltpu.roll(per_head[3], 64, 1))
    return jnp.concatenate([lo, hi], axis=1)


def _with_ones_lane(v):
    lane = lax.broadcasted_iota(jnp.int32, (1, LANES), 1)
    return jnp.where(lane == V_ONE, jnp.ones_like(v), v)


def _mla_kernel(tp, pc_ref, qn_ref, kvn_ref, wuq_ref, wukv_ref, cos_ref, sa_ref, sb_ref,
                o_ref, q_s, k_s, v_s, s_meta, s_s, mx_s, acc_s):
    scale = (MLA_NOPE + MLA_ROPE) ** -0.5 * LOG2E
    is_q = lax.broadcasted_iota(jnp.int32, (1, 256), 1) < MLA_Q_RANK

    def prep(r0, n_rows):
        rows = pl.ds(r0, n_rows)
        x = pc_ref[0, rows, 0:256].astype(F32)
        x2 = x * x
        ms_q = jnp.sum(jnp.where(is_q, x2, 0.0), axis=-1, keepdims=True) * (1.0 / MLA_Q_RANK)
        ms_kv = jnp.sum(jnp.where(is_q, 0.0, x2), axis=-1, keepdims=True) * (1.0 / MLA_KV_RANK)
        yq = (x * lax.rsqrt(ms_q + EPS) * qn_ref[...]).astype(BF16)
        ykv = (x * lax.rsqrt(ms_kv + EPS) * kvn_ref[...]).astype(BF16)
        cq = _dot(yq, wuq_ref[...])
        kv = _dot(ykv, wukv_ref[...])
        cos = cos_ref[rows, :]
        sa = sa_ref[rows, :]
        sb = sb_ref[rows, :]

        def rope(t):
            return t * cos + pltpu.roll(t, 16, 1) * sa + pltpu.roll(t, LANES - 16, 1) * sb

        kpe_in = pc_ref[0, rows, 256:384].astype(F32)
        kpe = rope(jnp.where(lax.broadcasted_iota(jnp.int32, (1, LANES), 1) >= 64, kpe_in, 0.0))
        for h in range(N_HEADS):
            q_s[h, rows, :] = (rope(cq[:, h * LANES:(h + 1) * LANES]) * scale).astype(BF16)
            k_s[h, rows, :] = (kv[:, h * LANES:(h + 1) * LANES] + kpe).astype(BF16)
            v_s[h, rows, :] = _with_ones_lane(
                kv[:, (N_HEADS + h) * LANES:(N_HEADS + h + 1) * LANES]).astype(BF16)

    prep(0, Q_BLOCK)

    def prep_body(i, carry):
        prep(pl.multiple_of(Q_BLOCK + i * ATT_ROWS, Q_BLOCK), ATT_ROWS)
        return carry

    lax.fori_loop(0, (tp - Q_BLOCK) // ATT_ROWS, prep_body, 0)

    def qk_fn(q0, n_rows, krows):
        return jnp.concatenate([_dot_nt(q_s[h, pl.ds(q0, n_rows), :], k_s[h, krows, :])
                                for h in range(N_HEADS)], axis=0)

    def v_fn(h, krows):
        return v_s[h, krows, :]

    def emit(q0, n_rows, o, first):
        y = _heads_to_lanes([o[h * n_rows:(h + 1) * n_rows, :] for h in range(N_HEADS)])
        if first:
            qrow = lax.broadcasted_iota(jnp.int32, (n_rows, 1), 0)
            y = jnp.where(qrow >= FRONT_PAD, y, 0.0)
        o_ref[0, pl.ds(q0, n_rows), :] = y.astype(BF16)

    emit(0, Q_BLOCK, _attn_block0(qk_fn, v_fn, N_HEADS), True)

    def block(i, mid_fn):
        _attn_block(i, qk_fn, v_fn, N_HEADS, s_meta, s_s, mx_s, acc_s, mid_fn)

    def finish(i):
        emit(_attn_row0(i), ATT_ROWS, _attn_result(acc_s), False)

    _attn_all_blocks((tp - Q_BLOCK) // ATT_ROWS, block, finish)


def _attn_scratch(n_g, tp):
    g_rows = n_g * ATT_ROWS
    n_slots = (tp - Q_BLOCK) // ATT_ROWS
    return [pltpu.VMEM((g_rows, LANES), F32),
            pltpu.VMEM((n_slots, g_rows, ATT_ROWS), F32),
            pltpu.VMEM((g_rows, LANES), F32),
            pltpu.VMEM((g_rows, LANES), F32)]


def _mla(pc, qn, kvn, wuq, wukv, cos, sa, sb):
    b, tp, _ = pc.shape
    return pl.pallas_call(
        functools.partial(_mla_kernel, tp),
        grid=(b,),
        in_specs=[pl.BlockSpec((1, tp, PC_W), lambda i: (i, 0, 0)),
                  _resident((1, 256)), _resident((1, 256)),
                  _resident(wuq.shape), _resident(wukv.shape),
                  _resident(cos.shape), _resident(sa.shape), _resident(sb.shape)],
        out_specs=pl.BlockSpec((1, tp, 256), lambda i: (i, 0, 0)),
        out_shape=jax.ShapeDtypeStruct((b, tp, 256), BF16),
        scratch_shapes=[pltpu.VMEM((N_HEADS, tp, LANES), BF16),
                        pltpu.VMEM((N_HEADS, tp, LANES), BF16),
                        pltpu.VMEM((N_HEADS, tp, LANES), BF16)] + _attn_scratch(N_HEADS, tp),
        compiler_params=_cparams(),
        name="mla",
    )(pc, qn, kvn, wuq, wukv, cos, sa, sb)


def _diff_kernel(tp, lam_init, pd_ref, lam_ref, dn_ref, o_ref, qs_s, v_s, s_meta, s_s, mx_s, acc_s):
    n_maps = 2 * N_HEADS

    def fill_values(i, carry):
        rows = pl.ds(pl.multiple_of(i * Q_BLOCK, Q_BLOCK), Q_BLOCK)
        low = lax.broadcasted_iota(jnp.int32, (1, LANES), 1) < V_ONE
        for pair in range(N_HEADS // 2):
            two = pd_ref[0, rows, 512 + pair * LANES:512 + (pair + 1) * LANES].astype(F32)
            for h, vals in ((2 * pair, two), (2 * pair + 1, pltpu.roll(two, V_ONE, 1))):
                v_s[h, rows, :] = _with_ones_lane(jnp.where(low, vals, 0.0)).astype(BF16)
        return carry

    lax.fori_loop(0, tp // Q_BLOCK, fill_values, 0)
    scale = DIFF_DK ** -0.5 * LOG2E
    group = lax.broadcasted_iota(jnp.int32, (1, 256), 1) >> 5
    lv = lam_ref[...]
    lam = (jnp.exp(jnp.sum(lv[0:1, :] * lv[1:2, :], axis=-1, keepdims=True))
           - jnp.exp(jnp.sum(lv[2:3, :] * lv[3:4, :], axis=-1, keepdims=True)) + lam_init)
    dn = dn_ref[...]

    def stack_queries(q0, n_rows):
        q = (pd_ref[0, pl.ds(q0, n_rows), 0:256].astype(F32) * scale).astype(BF16)
        for g in range(n_maps):
            qs_s[g * n_rows:(g + 1) * n_rows, :] = jnp.where(group == g, q, jnp.zeros_like(q))

    def qk_fn(q0, n_rows, krows):
        return _dot_nt(qs_s[0:n_maps * n_rows, :], pd_ref[0, krows, 256:512])

    def v_fn(g, krows):
        return v_s[g // 2, krows, :]

    def emit(q0, n_rows, o, first):
        od = jnp.concatenate([o[(2 * h) * n_rows:(2 * h + 1) * n_rows, :]
                              - lam * o[(2 * h + 1) * n_rows:(2 * h + 2) * n_rows, :]
                              for h in range(N_HEADS)], axis=0)
        od = jnp.where(lax.broadcasted_iota(jnp.int32, (1, LANES), 1) < V_ONE, od, 0.0)
        ms = jnp.sum(od * od, axis=-1, keepdims=True) * (1.0 / 64)
        yh = od * lax.rsqrt(ms + EPS) * dn * (1.0 - lam_init)
        y = _heads_to_lanes([yh[h * n_rows:(h + 1) * n_rows, :] for h in range(N_HEADS)])
        if first:
            qrow = lax.broadcasted_iota(jnp.int32, (n_rows, 1), 0)
            y = jnp.where(qrow >= FRONT_PAD, y, 0.0)
        o_ref[0, pl.ds(q0, n_rows), :] = y.astype(BF16)

    stack_queries(0, Q_BLOCK)
    emit(0, Q_BLOCK, _attn_block0(qk_fn, v_fn, n_maps), True)

    def block(i, mid_fn):
        stack_queries(_attn_row0(i), ATT_ROWS)
        _attn_block(i, qk_fn, v_fn, n_maps, s_meta, s_s, mx_s, acc_s, mid_fn)

    def finish(i):
        emit(_attn_row0(i), ATT_ROWS, _attn_result(acc_s), False)

    _attn_all_blocks((tp - Q_BLOCK) // ATT_ROWS, block, finish)


def _diff(pd, lam_rows, dn, lam_init):
    b, tp, _ = pd.shape
    n_maps = 2 * N_HEADS
    return pl.pallas_call(
        functools.partial(_diff_kernel, tp, lam_init),
        grid=(b,),
        in_specs=[pl.BlockSpec((1, tp, PD_W), lambda i: (i, 0, 0)),
                  _resident(lam_rows.shape), _resident((1, LANES))],
        out_specs=pl.BlockSpec((1, tp, 256), lambda i: (i, 0, 0)),
        out_shape=jax.ShapeDtypeStruct((b, tp, 256), BF16),
        scratch_shapes=[pltpu.VMEM((n_maps * ATT_ROWS, 256), BF16),
                        pltpu.VMEM((N_HEADS, tp, LANES), BF16)] + _attn_scratch(n_maps, tp),
        compiler_params=_cparams(),
        name="diffattn",
    )(pd, lam_rows, dn)


def _pad_cols(x, width):
    return jnp.pad(x, ((0, 0), (0, width - x.shape[1])))


def _rot_split(w):
    d = w.shape[0]
    return w.reshape(d, N_HEADS, 2, 32).transpose(0, 2, 1, 3).reshape(d, 256)


def _layout_w_in(w):
    sizes = (128, 128, 256, 16, 256, 256, 256, 256, 256, 192, 64, 32, 256, 256, 256)
    offs = [0]
    for s_ in sizes:
        offs.append(offs[-1] + s_)
    seg = [w[:, offs[i]:offs[i + 1]] for i in range(len(sizes))]
    (a_q, a_k, a_v, a_lr, a_og, r_q, r_k, r_v, r_og, c_cq, c_ckv, c_kpe, d_q, d_k, d_v) = seg
    d = w.shape[0]
    z = lambda n: jnp.zeros((d, n), w.dtype)
    cols = [a_q, a_k, a_v, a_og,
            _rot_split(r_q), _rot_split(r_k), r_v, r_og,
            c_cq, c_ckv, a_lr, z(48), c_kpe, z(32),
            d_q, d_k, d_v]
    return jnp.concatenate(cols, axis=1).astype(BF16)


def _tables(tp):
    pos = jnp.arange(tp, dtype=F32) - FRONT_PAD
    inv = ROPE_THETA ** (-jnp.arange(32, dtype=F32) / 32)
    ang = pos[:, None] * inv[None, :]
    ret_cos = jnp.tile(jnp.cos(ang), (1, N_HEADS))
    ret_sin = jnp.tile(jnp.sin(ang), (1, N_HEADS))
    inv16 = ROPE_THETA ** (-jnp.arange(16, dtype=F32) / 16)
    ang16 = pos[:, None] * inv16[None, :]
    c16, s16 = jnp.cos(ang16), jnp.sin(ang16)
    one = lambda n: jnp.ones((tp, n), F32)
    zero = lambda n: jnp.zeros((tp, n), F32)
    mla_cos = jnp.concatenate([one(64), c16, c16, one(32)], axis=1)
    mla_sa = jnp.concatenate([zero(80), s16, zero(32)], axis=1)
    mla_sb = jnp.concatenate([zero(64), -s16, zero(48)], axis=1)
    lg = jnp.log(1.0 - jnp.exp2(-5.0 - jnp.arange(N_HEADS, dtype=F32)))
    idx = jnp.arange(RET_BLOCK, dtype=F32)
    rel = idx[:, None] - idx[None, :]
    dmask = jnp.where(rel[None] >= 0, jnp.exp(rel[None] * lg[:, None, None]), 0.0)
    dmask = dmask.reshape(N_HEADS * RET_BLOCK, RET_BLOCK)
    lane_head = (jnp.arange(256) % 128) // 32
    qfac = jnp.exp((idx[:, None] + 1.0) * lg[lane_head][None, :])
    kfac = jnp.exp((RET_BLOCK - 1.0 - idx[:, None]) * lg[lane_head][None, :])
    dec = jnp.exp(RET_BLOCK * lg[lane_head])[:, None]
    return ret_cos, ret_sin, mla_cos, mla_sa, mla_sb, dmask, qfac, kfac, dec


def kernel(x, meta_tokens, attn_norm, w_in, gla_w_gate, gla_b_gate, gla_norm, ret_norm, mla_q_norm, mla_w_uq, mla_kv_norm, mla_w_ukv, diff_lambda, diff_norm, w_out, ffn_norm, ffn_w_gate, ffn_w_up, ffn_w_down, moe_router, moe_w_gate, moe_w_up, moe_w_down, final_norm):
    b, seq, d = x.shape
    assert DEPTH % 2 == 0, "the expert layer (odd index) must be last: its combine step applies the final norm"
    tp = FRONT_PAD + N_META + seq
    n = b * tp
    meta = jnp.broadcast_to(meta_tokens[None].astype(x.dtype), (b, N_META, d))
    h = jnp.concatenate([jnp.zeros((b, FRONT_PAD, d), x.dtype), meta, x], axis=1).reshape(n, d)
    ret_cos, ret_sin, mla_cos, mla_sa, mla_sb, dmask, qfac, kfac, dec = _tables(tp)

    for li in range(DEPTH):
        pa, pb, pc, pd = _inproj(h, attn_norm[li][None, :], _layout_w_in(w_in[li]))
        pa, pb, pc, pd = (p.reshape(b, tp, -1) for p in (pa, pb, pc, pd))

        wgate = jnp.pad(gla_w_gate[li], ((0, LANES - GLA_GATE_RANK), (0, 0)))
        o_a = _gla(pa, pc, wgate, gla_b_gate[li][None, :], jnp.tile(gla_norm[li], N_HEADS)[None, :])
        o_b = _ret(pb, ret_cos, ret_sin, dmask, qfac, kfac, dec, jnp.tile(ret_norm[li], N_HEADS)[None, :])

        qn = _pad_cols(mla_q_norm[li][None, :], 256)
        kvn = jnp.pad(mla_kv_norm[li][None, :], ((0, 0), (MLA_Q_RANK, 0)))
        wuq = jnp.pad(mla_w_uq[li].reshape(MLA_Q_RANK, N_HEADS, MLA_NOPE + MLA_ROPE),
                      ((0, 256 - MLA_Q_RANK), (0, 0), (0, LANES - MLA_NOPE - MLA_ROPE)))
        wuq = wuq.reshape(256, N_HEADS * LANES).astype(BF16)
        wukv = mla_w_ukv[li].reshape(MLA_KV_RANK, N_HEADS, 2, 64)
        wukv = jnp.pad(wukv, ((MLA_Q_RANK, 0), (0, 0), (0, 0), (0, 64)))
        wukv = wukv.transpose(0, 2, 1, 3).reshape(256, 2 * N_HEADS * LANES).astype(BF16)
        o_c = _mla(pc, qn, kvn, wuq, wukv, mla_cos, mla_sa, mla_sb)

        lam_init = 0.8 - 0.6 * math.exp(-0.3 * li)
        o_d = _diff(pd, diff_lambda[li], _pad_cols(diff_norm[li][None, :], LANES), lam_init)

        o_a, o_b, o_c, o_d = (o.reshape(n, 256) for o in (o_a, o_b, o_c, o_d))
        wo = w_out[li].astype(BF16)
        fn = ffn_norm[li][None, :]
        j = li // 2
        if li % 2 == 0:
            h = _outproj_swiglu(h, o_a, o_b, o_c, o_d, wo, fn, ffn_w_gate[j].astype(BF16),
                                ffn_w_up[j].astype(BF16), ffn_w_down[j].astype(BF16))
        else:
            h, hn, info, cnt = _outproj_router(h, o_a, o_b, o_c, o_d, wo, fn,
                                               _pad_cols(moe_router[j], LANES), tp)
            return _routed_moe_norm(h, hn, info, cnt, moe_w_gate[j].astype(BF16),
                                    moe_w_up[j].astype(BF16), moe_w_down[j].astype(BF16),
                                    final_norm[None, :], b, tp)
```

```python
import functools
import math

import jax
import jax.numpy as jnp
from jax import lax
from jax.experimental import pallas as pl
from jax.experimental.pallas import tpu as pltpu

F32 = jnp.float32
BF16 = jnp.bfloat16

D_MODEL = 1024
DEPTH = 2
N_META = 16
CHUNK = 64
Q_BLOCK = 128
FRONT_PAD = Q_BLOCK - N_META
EPS = 1e-6
NEG = -1e30
ROPE_THETA = 10000.0
N_HEADS = 4
GLA_DK = 32
GLA_DV = 64
GLA_GATE_RANK = 16
GLA_TAU = 16.0
RET_DK = 64
MLA_Q_RANK = 192
MLA_KV_RANK = 64
MLA_NOPE = 64
MLA_ROPE = 32
DIFF_DK = 32
N_EXPERTS = 8

LANES = 128
SUBLANES = 8
ROW_TILE = 512
FF_CHUNK = 256
PREFETCH_TAIL_CHUNKS = 4
VMEM_LIMIT = 56 * 1024 * 1024

PA_W = 768
PB_W = 1024
PC_W = 384
PD_W = 768


def _cparams(n_axes=1):
    return pltpu.CompilerParams(dimension_semantics=("arbitrary",) * n_axes,
                                vmem_limit_bytes=VMEM_LIMIT)


def _resident(shape):
    nd = len(shape)
    return pl.BlockSpec(shape, lambda *_: (0,) * nd, pipeline_mode=pl.Buffered(1))


def _sigmoid(x):
    return 1.0 / (1.0 + jnp.exp(-x))


def _split_bf16(x):
    hi = x.astype(BF16)
    lo = (x - hi.astype(F32)).astype(BF16)
    return hi, lo


def _dot(a, b):
    return jnp.dot(a, b, preferred_element_type=F32)


def _dot_nt(a, b):
    return lax.dot_general(a, b, (((1,), (1,)), ((), ())), preferred_element_type=F32)


def _inproj_kernel(h_ref, g_ref, w_ref, pa_ref, pb_ref, pc_ref, pd_ref):
    x = h_ref[...]
    ms = jnp.mean(x * x, axis=-1, keepdims=True)
    y = (x * lax.rsqrt(ms + EPS) * g_ref[...]).astype(BF16)
    off = 0
    for o_ref, width in ((pa_ref, PA_W), (pb_ref, PB_W), (pc_ref, PC_W), (pd_ref, PD_W)):
        o_ref[...] = _dot(y, w_ref[:, off:off + width]).astype(BF16)
        off += width


def _inproj(h, g, w):
    n = h.shape[0]
    wtot = PA_W + PB_W + PC_W + PD_W
    row = lambda width: pl.BlockSpec((ROW_TILE, width), lambda i: (i, 0))
    return pl.pallas_call(
        _inproj_kernel,
        grid=(n // ROW_TILE,),
        in_specs=[row(D_MODEL), _resident((1, D_MODEL)), _resident((D_MODEL, wtot))],
        out_specs=[row(PA_W), row(PB_W), row(PC_W), row(PD_W)],
        out_shape=[jax.ShapeDtypeStruct((n, w_), BF16) for w_ in (PA_W, PB_W, PC_W, PD_W)],
        compiler_params=_cparams(),
        name="inproj",
    )(h, g, w)


def _outproj_router_kernel(seq_rows, h_ref, oa_ref, ob_ref, oc_ref, od_ref, wo_ref, fn_ref, router_ref,
                           below_ref, hmid_ref, hn_ref, info_ref, cnt_ref, carry_ref):
    o = jnp.concatenate([oa_ref[...], ob_ref[...], oc_ref[...], od_ref[...]], axis=1)
    hm = h_ref[...] + _dot(o, wo_ref[...])
    hmid_ref[...] = hm
    ms = jnp.mean(hm * hm, axis=-1, keepdims=True)
    y = hm * lax.rsqrt(ms + EPS) * fn_ref[...]
    hn_ref[...] = y
    y_hi, y_lo = _split_bf16(y)
    r_hi, r_lo = _split_bf16(router_ref[...])
    logits = _dot(y_hi, r_hi) + _dot(y_hi, r_lo) + _dot(y_lo, r_hi)
    lane = lax.broadcasted_iota(jnp.int32, logits.shape, 1).astype(F32)
    ninf = float("-inf")
    logits = jnp.where(lane < N_EXPERTS, logits, ninf)
    m1 = jnp.max(logits, axis=-1, keepdims=True)
    i1 = jnp.min(jnp.where(logits == m1, lane, float(LANES)), axis=-1, keepdims=True)
    rest_l = jnp.where(lane == i1, ninf, logits)
    m2 = jnp.max(rest_l, axis=-1, keepdims=True)
    i2 = jnp.min(jnp.where(rest_l == m2, lane, float(LANES)), axis=-1, keepdims=True)
    e2 = jnp.exp(m2 - m1)
    den = 1.0 + e2

    @pl.when(pl.program_id(0) == 0)
    def _():
        carry_ref[...] = jnp.zeros_like(carry_ref)

    sel = jnp.where(lane == i1, 1.0, 0.0) + jnp.where(lane == i2, 1.0, 0.0)
    rows = sel.shape[0]
    seq_pos = (lax.rem(pl.program_id(0) * rows, seq_rows)
               + lax.broadcasted_iota(jnp.int32, (rows, 1), 0))
    seq_pos = jnp.where(seq_pos >= seq_rows, seq_pos - seq_rows, seq_pos)
    sel = jnp.where(seq_pos >= Q_BLOCK, sel, 0.0)
    count = _dot(below_ref[...], sel.astype(BF16)) + carry_ref[...]
    r1 = jnp.sum(jnp.where(lane == i1, count, 0.0), axis=-1, keepdims=True)
    r2 = jnp.sum(jnp.where(lane == i2, count, 0.0), axis=-1, keepdims=True)
    total = carry_ref[...] + jnp.sum(sel, axis=0, keepdims=True)
    carry_ref[...] = total
    cnt_ref[...] = jnp.broadcast_to(total, cnt_ref.shape)
    info = jnp.where(lane == 0, i1, 0.0)
    for k, val in enumerate((i2, 1.0 / den, e2 / den, r1, r2), start=1):
        info = jnp.where(lane == k, val, info)
    info_ref[...] = info


def _outproj_router(h, oa, ob, oc, od, wo, fn, router, seq_rows):
    n = h.shape[0]
    row = lambda width: pl.BlockSpec((ROW_TILE, width), lambda i: (i, 0))
    return pl.pallas_call(
        functools.partial(_outproj_router_kernel, seq_rows),
        grid=(n // ROW_TILE,),
        in_specs=[row(D_MODEL), row(256), row(256), row(256), row(256),
                  _resident((D_MODEL, D_MODEL)), _resident((1, D_MODEL)),
                  _resident((D_MODEL, LANES)), _resident((ROW_TILE, ROW_TILE))],
        out_specs=[row(D_MODEL), row(D_MODEL), row(LANES), pl.BlockSpec((8, LANES), lambda i: (0, 0))],
        out_shape=[jax.ShapeDtypeStruct((n, D_MODEL), F32), jax.ShapeDtypeStruct((n, D_MODEL), F32),
                   jax.ShapeDtypeStruct((n, LANES), F32), jax.ShapeDtypeStruct((8, LANES), F32)],
        scratch_shapes=[pltpu.VMEM((1, LANES), F32)],
        compiler_params=_cparams(),
        name="outproj_router",
    )(h, oa, ob, oc, od, wo, fn, router, jnp.tri(ROW_TILE, k=-1, dtype=BF16))


def _inverse_map_kernel(zs_ref, ze_ref, pos_ref, tok_ref):
    i = pl.program_id(0)

    @pl.when(i == 0)
    def _():
        def zero(j, carry):
            tok_ref[j] = 0
            return carry

        for k in range(zs_ref.shape[0]):
            lax.fori_loop(zs_ref[k], ze_ref[k], zero, 0)

    base = i * ROW_TILE

    for r in range(ROW_TILE):
        tok_ref[pos_ref[0, 0, 2 * r]] = base + r
        tok_ref[pos_ref[0, 0, 2 * r + 1]] = base + r


def _inverse_map(zero_start, zero_end, pos, n_sorted):
    return pl.pallas_call(
        _inverse_map_kernel,
        grid_spec=pltpu.PrefetchScalarGridSpec(
            num_scalar_prefetch=2,
            grid=(pos.shape[0],),
            in_specs=[pl.BlockSpec((1, 1, 2 * ROW_TILE), lambda i, zs, ze: (i, 0, 0),
                                   memory_space=pltpu.SMEM)],
            out_specs=pl.BlockSpec(memory_space=pltpu.SMEM)),
        out_shape=jax.ShapeDtypeStruct((n_sorted + ROW_TILE,), jnp.int32),
        compiler_params=_cparams(),
        name="moe_inverse_map",
    )(zero_start, zero_end, pos)


def _swiglu_tile(x, wg, wu, wd, acc_ref, per_chunk=None):
    n_chunks = wd.shape[0] // FF_CHUNK
    for c in range(n_chunks):
        cols = slice(c * FF_CHUNK, (c + 1) * FF_CHUNK)
        g = _dot(x, wg[:, cols])
        u = _dot(x, wu[:, cols])
        part = _dot((g * _sigmoid(g) * u).astype(BF16), wd[cols, :])
        if c == 0:
            acc_ref[...] = part
        elif c < n_chunks - 1:
            acc_ref[...] += part
        if per_chunk is not None:
            per_chunk(c, n_chunks)
    return acc_ref[...] + part


def _experts_kernel(te_ref, tv_ref, tok_ref, tok_next_ref, hn_ref, wg_ref, wu_ref, wd_ref, ys_ref,
                    x_buf, acc_ref, wg_bf, wu_bf, wd_bf, stage_in, stage_out, sem, wsem):
    i = pl.program_id(0)
    last = pl.num_programs(0) - 1
    slot = i & 1
    n_blocks = x_buf.shape[1]

    def row_copy(t_ref, s, rb, u):
        tok = t_ref[0, 0, rb * SUBLANES + u]
        return pltpu.make_async_copy(hn_ref.at[pl.ds(tok, 1), :], x_buf.at[s, rb, pl.ds(u, 1), :],
                                     sem.at[s])

    def wait_tile(s):
        pltpu.make_async_copy(x_buf.at[s], x_buf.at[s], sem.at[s]).wait()

    @pl.when(i == 0)
    def _():
        def start_block(rb, carry):
            for u in range(SUBLANES):
                row_copy(tok_ref, 0, rb, u).start(priority=u % 2)
            return carry

        lax.fori_loop(0, n_blocks, start_block, 0)

    new_expert = jnp.logical_or(i == 0, te_ref[i] != te_ref[jnp.maximum(i - 1, 0)])

    @pl.when(jnp.logical_and(tv_ref[i] > 0, new_expert))
    def _():
        e = te_ref[i]
        n_chunks = wd_bf.shape[0] // FF_CHUNK

        def chunk_copy(k):
            m, c = divmod(k, n_chunks)
            cols = pl.ds(c * FF_CHUNK, FF_CHUNK)
            if m < 2:
                src = (wg_ref, wu_ref)[m].at[e, :, cols]
                return pltpu.make_async_copy(src, stage_in.at[k % 2], wsem.at[k % 2])
            return pltpu.make_async_copy(wd_ref.at[e, cols, :], stage_out.at[k % 2], wsem.at[k % 2])

        n_copies = 3 * n_chunks
        chunk_copy(0).start()
        chunk_copy(1).start()
        for k in range(n_copies):
            chunk_copy(k).wait()
            m, c = divmod(k, n_chunks)
            if m < 2:
                (wg_bf, wu_bf)[m][:, c * FF_CHUNK:(c + 1) * FF_CHUNK] = stage_in[k % 2].astype(BF16)
            else:
                wd_bf[c * FF_CHUNK:(c + 1) * FF_CHUNK, :] = stage_out[k % 2].astype(BF16)
            if k + 2 < n_copies:
                chunk_copy(k + 2).start()

    @pl.when(jnp.logical_or(i == 0, tv_ref[jnp.maximum(i - 1, 0)] > 0))
    def _():
        wait_tile(slot)

    @pl.when(tv_ref[i] > 0)
    def _():
        def prefetch(c, n_chunks):
            per = -(-ROW_TILE // max(n_chunks - PREFETCH_TAIL_CHUNKS, 1))
            for r in range(c * per, min((c + 1) * per, ROW_TILE)):
                row_copy(tok_next_ref, 1 - slot, r // SUBLANES, r % SUBLANES).start(priority=r % 2)

        x = x_buf[slot].reshape(ROW_TILE, -1).astype(BF16)
        ys_ref[...] = _swiglu_tile(x, wg_bf, wu_bf, wd_bf, acc_ref, prefetch)

    @pl.when(tv_ref[i] == 0)
    def _():
        ys_ref[...] = jnp.zeros_like(ys_ref)

    @pl.when(jnp.logical_and(i == last, tv_ref[i] > 0))
    def _():
        wait_tile(1 - slot)


def _experts(tile_expert, tile_valid, tok, hn, wg, wu, wd):
    n_tiles = tok.shape[0]
    d = hn.shape[1]
    f = wd.shape[1]
    any_spec = pl.BlockSpec(memory_space=pl.ANY)
    tok_spec = lambda index: pl.BlockSpec((1, 1, ROW_TILE), index, memory_space=pltpu.SMEM)
    return pl.pallas_call(
        _experts_kernel,
        grid_spec=pltpu.PrefetchScalarGridSpec(
            num_scalar_prefetch=2,
            grid=(n_tiles,),
            in_specs=[tok_spec(lambda i, te, tv: (i, 0, 0)),
                      tok_spec(lambda i, te, tv: (jnp.minimum(i + 1, n_tiles - 1), 0, 0)),
                      any_spec, any_spec, any_spec, any_spec],
            out_specs=pl.BlockSpec((ROW_TILE, d), lambda i, te, tv: (i, 0)),
            scratch_shapes=[pltpu.VMEM((2, ROW_TILE // SUBLANES, SUBLANES, d), F32),
                            pltpu.VMEM((ROW_TILE, d), F32),
                            pltpu.VMEM((d, f), BF16), pltpu.VMEM((d, f), BF16), pltpu.VMEM((f, d), BF16),
                            pltpu.VMEM((2, d, FF_CHUNK), F32), pltpu.VMEM((2, FF_CHUNK, d), F32),
                            pltpu.SemaphoreType.DMA((2,)), pltpu.SemaphoreType.DMA((2,))]),
        out_shape=jax.ShapeDtypeStruct((n_tiles * ROW_TILE, d), F32),
        compiler_params=_cparams(),
        name="moe_experts",
    )(tile_expert, tile_valid, tok, tok, hn, wg, wu, wd)


def _combine_norm_kernel(tp, pos_ref, h_ref, info_ref, ys_ref, g_ref, o_ref,
                         h_buf, info_buf, y_buf, sem, row_sem):
    n_rows = h_buf.shape[0]
    row0 = pl.multiple_of(pl.program_id(0) * tp + Q_BLOCK + pl.program_id(1) * n_rows, Q_BLOCK)
    h_copy = pltpu.make_async_copy(h_ref.at[pl.ds(row0, n_rows), :], h_buf, sem.at[0])
    info_copy = pltpu.make_async_copy(info_ref.at[pl.ds(row0, n_rows), :], info_buf, sem.at[1])
    h_copy.start()
    info_copy.start()

    def row_copy(rb, u, k):
        p = pos_ref[0, 0, rb * (2 * SUBLANES) + 2 * u + k]
        return pltpu.make_async_copy(ys_ref.at[pl.ds(p, 1), :], y_buf.at[k, rb, pl.ds(u, 1), :], row_sem)

    for rb in range(n_rows // SUBLANES):
        for u in range(SUBLANES):
            row_copy(rb, u, 0).start(priority=0)
            row_copy(rb, u, 1).start(priority=1)
    pltpu.make_async_copy(y_buf, y_buf, row_sem).wait()
    h_copy.wait()
    info_copy.wait()
    info = info_buf[...]
    y0 = y_buf[0].reshape(n_rows, -1)
    y1 = y_buf[1].reshape(n_rows, -1)
    x = h_buf[...] + info[:, 2:3] * y0 + info[:, 3:4] * y1
    ms = jnp.mean(x * x, axis=-1, keepdims=True)
    o_ref[0] = x * lax.rsqrt(ms + EPS) * g_ref[...]


def _combine_norm(pos, h, info, ys, g, b, tp):
    n, d = h.shape
    seq = tp - Q_BLOCK
    tiles = seq // ROW_TILE
    any_spec = pl.BlockSpec(memory_space=pl.ANY)
    return pl.pallas_call(
        functools.partial(_combine_norm_kernel, tp),
        grid=(b, tiles),
        in_specs=[pl.BlockSpec((1, 1, 2 * ROW_TILE), lambda i, j: (i * tiles + j, 0, 0),
                               memory_space=pltpu.SMEM),
                  any_spec, any_spec, any_spec, _resident((1, d))],
        out_specs=pl.BlockSpec((1, ROW_TILE, d), lambda i, j: (i, j, 0)),
        out_shape=jax.ShapeDtypeStruct((b, seq, d), F32),
        scratch_shapes=[pltpu.VMEM((ROW_TILE, d), F32), pltpu.VMEM((ROW_TILE, LANES), F32),
                        pltpu.VMEM((2, ROW_TILE // SUBLANES, SUBLANES, d), F32),
                        pltpu.SemaphoreType.DMA((2,)), pltpu.SemaphoreType.DMA(())],
        compiler_params=_cparams(2),
        name="moe_combine_norm",
    )(pos, h, info, ys, g)


def _routed_moe_norm(h, hn, info, cnt, wg, wu, wd, g, b, tp):
    n = h.shape[0]
    n_routed = b * (tp - Q_BLOCK)
    n_sorted = 2 * n_routed + N_EXPERTS * ROW_TILE
    n_tiles = n_sorted // ROW_TILE
    counts = cnt[0, :N_EXPERTS].astype(jnp.int32)
    tiles_e = (counts + ROW_TILE - 1) // ROW_TILE
    tile_end = jnp.cumsum(tiles_e)
    row_base = (tile_end - tiles_e) * ROW_TILE
    e_idx = info[:, 0:2].astype(jnp.int32)
    pos = row_base[e_idx] + info[:, 4:6].astype(jnp.int32)
    routed = (jnp.arange(n, dtype=jnp.int32) % tp >= Q_BLOCK)[:, None]
    zero_start = jnp.concatenate([row_base + counts, tile_end[-1:] * ROW_TILE])
    zero_end = jnp.concatenate([tile_end * ROW_TILE, jnp.full((1,), n_sorted + ROW_TILE, jnp.int32)])
    tok = _inverse_map(zero_start, zero_end,
                       jnp.where(routed, pos, n_sorted).reshape(n // ROW_TILE, 1, 2 * ROW_TILE),
                       n_sorted)[:n_sorted]
    tile_ids = jnp.arange(n_tiles, dtype=jnp.int32)
    tile_expert = jnp.minimum(jnp.sum((tile_ids[:, None] >= tile_end[None, :]).astype(jnp.int32), axis=1),
                              N_EXPERTS - 1)
    tile_valid = (tile_ids < tile_end[-1]).astype(jnp.int32)
    ys = _experts(tile_expert, tile_valid, tok.reshape(n_tiles, 1, ROW_TILE), hn, wg, wu, wd)
    pos_seq = pos.reshape(b, tp, 2)[:, Q_BLOCK:, :].reshape(-1, 1, 2 * ROW_TILE)
    return _combine_norm(pos_seq, h, info, ys, g, b, tp)


def _outproj_swiglu_kernel(h_ref, oa_ref, ob_ref, oc_ref, od_ref, wo_ref, fn_ref, wg_ref, wu_ref, wd_ref,
                           o_ref, acc_ref):
    o = jnp.concatenate([oa_ref[...], ob_ref[...], oc_ref[...], od_ref[...]], axis=1)
    hm = h_ref[...] + _dot(o, wo_ref[...])
    ms = jnp.mean(hm * hm, axis=-1, keepdims=True)
    hn = (hm * lax.rsqrt(ms + EPS) * fn_ref[...]).astype(BF16)
    o_ref[...] = hm + _swiglu_tile(hn, wg_ref, wu_ref, wd_ref, acc_ref)


def _outproj_swiglu(h, oa, ob, oc, od, wo, fn, wg, wu, wd):
    n = h.shape[0]
    row = lambda width: pl.BlockSpec((ROW_TILE, width), lambda i: (i, 0))
    return pl.pallas_call(
        _outproj_swiglu_kernel,
        grid=(n // ROW_TILE,),
        in_specs=[row(D_MODEL), row(256), row(256), row(256), row(256),
                  _resident(wo.shape), _resident(fn.shape),
                  _resident(wg.shape), _resident(wu.shape), _resident(wd.shape)],
        out_specs=row(D_MODEL),
        out_shape=jax.ShapeDtypeStruct((n, D_MODEL), F32),
        scratch_shapes=[pltpu.VMEM((ROW_TILE, D_MODEL), F32)],
        input_output_aliases={0: 0},
        compiler_params=_cparams(),
        name="outproj_swiglu",
    )(h, oa, ob, oc, od, wo, fn, wg, wu, wd)


def _group_ones(n, group_shift):
    r = lax.broadcasted_iota(jnp.int32, (n, n), 0) >> group_shift
    c = lax.broadcasted_iota(jnp.int32, (n, n), 1) >> group_shift
    return jnp.where(r == c, 1.0, 0.0).astype(BF16)


GLA_ROWS = 256


def _gla_kernel(tp, pa_ref, lr_ref, wg_ref, bg_ref, gn_ref, o_ref, s_ref):
    qhead = lax.broadcasted_iota(jnp.int32, (1, N_HEADS * GLA_DK), 1) >> 5
    ehead = lax.broadcasted_iota(jnp.int32, (1, N_HEADS * GLA_DV), 1) >> 6
    shead = lax.broadcasted_iota(jnp.int32, (N_HEADS * GLA_DK, 1), 0) >> 5
    bd = shead == ehead
    gsum = _group_ones(N_HEADS * GLA_DV, 6)
    wg_hi, wg_lo = _split_bf16(wg_ref[...])
    bg = bg_ref[...]
    gn = gn_ref[...]
    scale = GLA_DK ** -0.5
    s_ref[...] = jnp.zeros_like(s_ref)

    def block(r0, n_rows):
        n_c = n_rows // CHUNK
        rows = pl.ds(r0, n_rows)
        ri = lax.broadcasted_iota(jnp.int32, (n_rows, n_rows), 0)
        ci = lax.broadcasted_iota(jnp.int32, (n_rows, n_rows), 1)
        same = (ri >> 6) == (ci >> 6)
        tri_bf = jnp.where(jnp.logical_and(same, ri >= ci), 1.0, 0.0).astype(BF16)
        ones_bf = jnp.where(same, 1.0, 0.0).astype(BF16)
        r4 = lax.broadcasted_iota(jnp.int32, (N_HEADS * n_rows, n_rows), 0) & (n_rows - 1)
        c4 = lax.broadcasted_iota(jnp.int32, (N_HEADS * n_rows, n_rows), 1)
        tri4 = jnp.logical_and((r4 >> 6) == (c4 >> 6), r4 >= c4)
        col_chunk = lax.broadcasted_iota(jnp.int32, (1, n_rows), 1) >> 6

        q = pa_ref[0, rows, 0:128].astype(F32) * scale
        k = pa_ref[0, rows, 128:256].astype(F32)
        v = pa_ref[0, rows, 256:512]
        og = pa_ref[0, rows, 512:768].astype(F32)
        lr = lr_ref[0, rows, :]
        valid = (r0 + lax.broadcasted_iota(jnp.int32, (n_rows, 1), 0)) >= FRONT_PAD

        pre = _dot(lr, wg_hi) + _dot(lr, wg_lo) + bg
        logsig = jnp.minimum(pre, 0.0) - jnp.log1p(jnp.exp(-jnp.abs(pre)))
        g = jnp.where(valid, logsig * (1.0 / GLA_TAU), 0.0)
        g_hi, g_lo = _split_bf16(g)
        cum = _dot(tri_bf, g_hi) + _dot(tri_bf, g_lo)
        cum_end = _dot(ones_bf, g_hi) + _dot(ones_bf, g_lo)
        qt_bf = (q * jnp.exp(cum)).astype(BF16)
        kt = (k * jnp.exp(-cum)).astype(BF16)
        kd_t = (k * jnp.exp(cum_end - cum)).T.astype(BF16)
        dec_t = jnp.exp(cum_end.T)

        qs = jnp.concatenate([jnp.where(qhead == h, qt_bf, jnp.zeros_like(qt_bf))
                              for h in range(N_HEADS)], axis=0)
        a = jnp.where(tri4, _dot_nt(qs, kt), 0.0).astype(BF16)
        r = _dot(a, v)
        o = jnp.where(ehead == 0, r[0:n_rows, :], 0.0)
        for h in range(1, N_HEADS):
            o = o + jnp.where(ehead == h, r[h * n_rows:(h + 1) * n_rows, :], 0.0)

        s = s_ref[...]
        inter = []
        for c in range(n_c):
            inter.append(_dot(qt_bf[c * CHUNK:(c + 1) * CHUNK, :], s.astype(BF16)))
            upd = _dot(jnp.where(col_chunk == c, kd_t, jnp.zeros_like(kd_t)), v)
            s = dec_t[:, c * CHUNK:c * CHUNK + 1] * s + jnp.where(bd, upd, 0.0)
        s_ref[...] = s
        o = o + jnp.concatenate(inter, axis=0)

        ms = _dot((o * o).astype(BF16), gsum) * (1.0 / GLA_DV)
        y = o * lax.rsqrt(ms + EPS) * gn * (og * _sigmoid(og))
        o_ref[0, rows, :] = jnp.where(valid, y, 0.0).astype(BF16)

    block(0, Q_BLOCK)

    def body(i, carry):
        block(pl.multiple_of(Q_BLOCK + i * GLA_ROWS, Q_BLOCK), GLA_ROWS)
        return carry

    lax.fori_loop(0, (tp - Q_BLOCK) // GLA_ROWS, body, 0)


def _gla(pa, pc, wg, bg, gn):
    b, tp, _ = pa.shape
    return pl.pallas_call(
        functools.partial(_gla_kernel, tp),
        grid=(b,),
        in_specs=[pl.BlockSpec((1, tp, PA_W), lambda i: (i, 0, 0)),
                  pl.BlockSpec((1, tp, LANES), lambda i: (i, 0, 2)),
                  _resident((LANES, LANES)), _resident((1, LANES)), _resident((1, 256))],
        out_specs=pl.BlockSpec((1, tp, 256), lambda i: (i, 0, 0)),
        out_shape=jax.ShapeDtypeStruct((b, tp, 256), BF16),
        scratch_shapes=[pltpu.VMEM((N_HEADS * GLA_DK, N_HEADS * GLA_DV), F32)],
        compiler_params=_cparams(),
        name="gla",
    )(pa, pc, wg, bg, gn)


RET_BLOCK = 128


def _ret_kernel(tp, pb_ref, cos_ref, sin_ref, dmask_ref, qfac_ref, kfac_ref, dec_ref, gn_ref,
                o_ref, s_ref):
    blk = RET_BLOCK
    n_blocks = tp // blk
    qhead = (lax.broadcasted_iota(jnp.int32, (1, 256), 1) & 127) >> 5
    ehead = lax.broadcasted_iota(jnp.int32, (1, 256), 1) >> 6
    shead = (lax.broadcasted_iota(jnp.int32, (256, 1), 0) & 127) >> 5
    bd = shead == ehead
    gsum = _group_ones(256, 6)
    gn = gn_ref[...]
    s_ref[...] = jnp.zeros_like(s_ref)

    def rope(x, cos, sin):
        x1, x2 = x[:, :128], x[:, 128:]
        return jnp.concatenate([x1 * cos - x2 * sin, x1 * sin + x2 * cos], axis=1)

    def block(j, carry):
        r0 = pl.multiple_of(j * blk, blk)
        rows = pl.ds(r0, blk)
        cos = cos_ref[rows, :]
        sin = sin_ref[rows, :]
        q = rope(pb_ref[0, rows, 0:256].astype(F32), cos, sin)
        k = rope(pb_ref[0, rows, 256:512].astype(F32), cos, sin) * (RET_DK ** -0.5)
        v = pb_ref[0, rows, 512:768]
        og = pb_ref[0, rows, 768:1024].astype(F32)
        valid = (r0 + lax.broadcasted_iota(jnp.int32, (blk, 1), 0)) >= FRONT_PAD

        q_bf = q.astype(BF16)
        qs = jnp.concatenate([jnp.where(qhead == h, q_bf, jnp.zeros_like(q_bf))
                              for h in range(N_HEADS)], axis=0)
        a = (_dot_nt(qs, k.astype(BF16)) * dmask_ref[...]).astype(BF16)
        r = _dot(a, v)
        o = _dot((q * qfac_ref[...]).astype(BF16), s_ref[...].astype(BF16))
        for h in range(N_HEADS):
            o = o + jnp.where(ehead == h, r[h * blk:(h + 1) * blk, :], 0.0)

        kd_t = (k * kfac_ref[...]).T.astype(BF16)
        upd = _dot(kd_t, v)
        s_ref[...] = dec_ref[...] * s_ref[...] + jnp.where(bd, upd, 0.0)

        mu = _dot(o.astype(BF16), gsum) * (1.0 / 64)
        xc = o - mu
        var = _dot((xc * xc).astype(BF16), gsum) * (1.0 / 64)
        y = xc * lax.rsqrt(var + EPS) * gn * (og * _sigmoid(og))
        o_ref[0, rows, :] = jnp.where(valid, y, 0.0).astype(BF16)
        return carry

    lax.fori_loop(0, n_blocks, block, 0, unroll=4)


def _ret(pb, cos, sin, dmask, qfac, kfac, dec, gn):
    b, tp, _ = pb.shape
    return pl.pallas_call(
        functools.partial(_ret_kernel, tp),
        grid=(b,),
        in_specs=[pl.BlockSpec((1, tp, PB_W), lambda i: (i, 0, 0)),
                  _resident(cos.shape), _resident(sin.shape), _resident(dmask.shape),
                  _resident(qfac.shape), _resident(kfac.shape), _resident(dec.shape),
                  _resident((1, 256))],
        out_specs=pl.BlockSpec((1, tp, 256), lambda i: (i, 0, 0)),
        out_shape=jax.ShapeDtypeStruct((b, tp, 256), BF16),
        scratch_shapes=[pltpu.VMEM((256, 256), F32)],
        compiler_params=_cparams(),
        name="retention",
    )(pb, cos, sin, dmask, qfac, kfac, dec, gn)


ATT_ROWS = 256
LOG2E = 1.4426950408889634
V_ONE = 64


def _aligned(x, m):
    return x if isinstance(x, int) else pl.multiple_of(x, m)


def _for_chunks(n, body):
    if isinstance(n, int):
        for j in range(n):
            body(j)
        return

    def quad(t, carry):
        for u in range(4):
            body(4 * t + u)
        return carry

    lax.fori_loop(0, n >> 2, quad, 0)
    done = (n >> 2) << 2

    @pl.when((n & 2) != 0)
    def _():
        body(done)
        body(done + 1)

    @pl.when((n & 1) != 0)
    def _():
        body(done + (n & 2))


def _mask_groups(mask, s, n_g):
    rows = s.shape[0] // n_g
    return jnp.concatenate([jnp.where(mask, s[g * rows:(g + 1) * rows, :], NEG)
                            for g in range(n_g)], axis=0)


def _attn_block0(qk_fn, v_fn, n_g):
    rows = pl.ds(0, Q_BLOCK)
    qrow = lax.broadcasted_iota(jnp.int32, (Q_BLOCK, 1), 0)
    kcol = lax.broadcasted_iota(jnp.int32, (1, Q_BLOCK), 1)
    mask = jnp.logical_and(kcol <= qrow, kcol >= FRONT_PAD)
    s = _mask_groups(mask, qk_fn(0, Q_BLOCK, rows), n_g)
    p_bf = jnp.exp2(s - jnp.max(s, axis=-1, keepdims=True)).astype(BF16)
    pv = jnp.concatenate([_dot(p_bf[g * Q_BLOCK:(g + 1) * Q_BLOCK, :], v_fn(g, rows))
                          for g in range(n_g)], axis=0)
    return pv / pv[:, V_ONE:V_ONE + 1]


def _attn_row0(i):
    return _aligned(Q_BLOCK + (i - 1) * ATT_ROWS, Q_BLOCK)


def _attn_result(acc_s):
    acc = acc_s[...]
    return acc / acc[:, V_ONE:V_ONE + 1]


def _attn_block(i, qk_fn, v_fn, n_g, s_meta, s_s, mx_s, acc_s, mid_fn=None):
    rr = ATT_ROWS
    q0 = _attn_row0(i)
    meta_rows = pl.ds(0, Q_BLOCK)
    kcol = lax.broadcasted_iota(jnp.int32, (1, Q_BLOCK), 1)

    def key_rows(j):
        return pl.ds(_aligned(Q_BLOCK + j * rr, Q_BLOCK), rr)

    s = jnp.where(kcol >= FRONT_PAD, qk_fn(q0, rr, meta_rows), NEG)
    s_meta[...] = s
    mx_s[...] = s

    def pass1(j):
        sj = qk_fn(q0, rr, key_rows(j))
        s_s[j] = sj
        mx_s[...] = jnp.maximum(mx_s[...], jnp.maximum(sj[:, :LANES], sj[:, LANES:]))

    _for_chunks(i - 1, pass1)
    if mid_fn is not None:
        mid_fn()
    causal = (lax.broadcasted_iota(jnp.int32, (rr, rr), 1)
              <= lax.broadcasted_iota(jnp.int32, (rr, rr), 0))
    sd = _mask_groups(causal, qk_fn(q0, rr, pl.ds(q0, rr)), n_g)
    s_s[i - 1] = sd
    m = jnp.max(jnp.maximum(mx_s[...], jnp.maximum(sd[:, :LANES], sd[:, LANES:])),
                axis=-1, keepdims=True)
    mx_s[...] = jnp.broadcast_to(m, mx_s.shape)

    p_bf = jnp.exp2(s_meta[...] - mx_s[...]).astype(BF16)
    for g in range(n_g):
        acc_s[g * rr:(g + 1) * rr, :] = _dot(p_bf[g * rr:(g + 1) * rr, :], v_fn(g, meta_rows))

    def pass2(j):
        sj = s_s[j]
        mrep = mx_s[...]
        p0 = jnp.exp2(sj[:, :LANES] - mrep)
        p1 = jnp.exp2(sj[:, LANES:] - mrep)
        pj = jnp.concatenate([p0.astype(BF16), p1.astype(BF16)], axis=1)
        for g in range(n_g):
            acc_s[g * rr:(g + 1) * rr, :] += _dot(pj[g * rr:(g + 1) * rr, :], v_fn(g, key_rows(j)))

    _for_chunks(i, pass2)


def _attn_all_blocks(n_q, block_fn, finish_fn):
    block_fn(1, None)

    def body(i, carry):
        block_fn(i, lambda: finish_fn(i - 1))
        return carry

    lax.fori_loop(2, n_q + 1, body, 0)
    finish_fn(n_q)


def _heads_to_lanes(per_head):
    low = lax.broadcasted_iota(jnp.int32, (1, LANES), 1) < 64
    lo = jnp.where(low, per_head[0], pltpu.roll(per_head[1], 64, 1))
    hi = jnp.where(low, per_head[2], pltpu.roll(per_head[3], 64, 1))
    return jnp.concatenate([lo, hi], axis=1)


def _with_ones_lane(v):
    lane = lax.broadcasted_iota(jnp.int32, (1, LANES), 1)
    return jnp.where(lane == V_ONE, jnp.ones_like(v), v)


def _mla_kernel(tp, pc_ref, qn_ref, kvn_ref, wuq_ref, wukv_ref, cos_ref, sa_ref, sb_ref,
                o_ref, q_s, k_s, v_s, s_meta, s_s, mx_s, acc_s):
    scale = (MLA_NOPE + MLA_ROPE) ** -0.5 * LOG2E
    is_q = lax.broadcasted_iota(jnp.int32, (1, 256), 1) < MLA_Q_RANK

    def prep(r0, n_rows):
        rows = pl.ds(r0, n_rows)
        x = pc_ref[0, rows, 0:256].astype(F32)
        x2 = x * x
        ms_q = jnp.sum(jnp.where(is_q, x2, 0.0), axis=-1, keepdims=True) * (1.0 / MLA_Q_RANK)
        ms_kv = jnp.sum(jnp.where(is_q, 0.0, x2), axis=-1, keepdims=True) * (1.0 / MLA_KV_RANK)
        yq = (x * lax.rsqrt(ms_q + EPS) * qn_ref[...]).astype(BF16)
        ykv = (x * lax.rsqrt(ms_kv + EPS) * kvn_ref[...]).astype(BF16)
        cq = _dot(yq, wuq_ref[...])
        kv = _dot(ykv, wukv_ref[...])
        cos = cos_ref[rows, :]
        sa = sa_ref[rows, :]
        sb = sb_ref[rows, :]

        def rope(t):
            return t * cos + pltpu.roll(t, 16, 1) * sa + pltpu.roll(t, LANES - 16, 1) * sb

        kpe_in = pc_ref[0, rows, 256:384].astype(F32)
        kpe = rope(jnp.where(lax.broadcasted_iota(jnp.int32, (1, LANES), 1) >= 64, kpe_in, 0.0))
        for h in range(N_HEADS):
            q_s[h, rows, :] = (rope(cq[:, h * LANES:(h + 1) * LANES]) * scale).astype(BF16)
            k_s[h, rows, :] = (kv[:, h * LANES:(h + 1) * LANES] + kpe).astype(BF16)
            v_s[h, rows, :] = _with_ones_lane(
                kv[:, (N_HEADS + h) * LANES:(N_HEADS + h + 1) * LANES]).astype(BF16)

    prep(0, Q_BLOCK)

    def prep_body(i, carry):
        prep(pl.multiple_of(Q_BLOCK + i * ATT_ROWS, Q_BLOCK), ATT_ROWS)
        return carry

    lax.fori_loop(0, (tp - Q_BLOCK) // ATT_ROWS, prep_body, 0)

    def qk_fn(q0, n_rows, krows):
        return jnp.concatenate([_dot_nt(q_s[h, pl.ds(q0, n_rows), :], k_s[h, krows, :])
                                for h in range(N_HEADS)], axis=0)

    def v_fn(h, krows):
        return v_s[h, krows, :]

    def emit(q0, n_rows, o, first):
        y = _heads_to_lanes([o[h * n_rows:(h + 1) * n_rows, :] for h in range(N_HEADS)])
        if first:
            qrow = lax.broadcasted_iota(jnp.int32, (n_rows, 1), 0)
            y = jnp.where(qrow >= FRONT_PAD, y, 0.0)
        o_ref[0, pl.ds(q0, n_rows), :] = y.astype(BF16)

    emit(0, Q_BLOCK, _attn_block0(qk_fn, v_fn, N_HEADS), True)

    def block(i, mid_fn):
        _attn_block(i, qk_fn, v_fn, N_HEADS, s_meta, s_s, mx_s, acc_s, mid_fn)

    def finish(i):
        emit(_attn_row0(i), ATT_ROWS, _attn_result(acc_s), False)

    _attn_all_blocks((tp - Q_BLOCK) // ATT_ROWS, block, finish)


def _attn_scratch(n_g, tp):
    g_rows = n_g * ATT_ROWS
    n_slots = (tp - Q_BLOCK) // ATT_ROWS
    return [pltpu.VMEM((g_rows, LANES), F32),
            pltpu.VMEM((n_slots, g_rows, ATT_ROWS), F32),
            pltpu.VMEM((g_rows, LANES), F32),
            pltpu.VMEM((g_rows, LANES), F32)]


def _mla(pc, qn, kvn, wuq, wukv, cos, sa, sb):
    b, tp, _ = pc.shape
    return pl.pallas_call(
        functools.partial(_mla_kernel, tp),
        grid=(b,),
        in_specs=[pl.BlockSpec((1, tp, PC_W), lambda i: (i, 0, 0)),
                  _resident((1, 256)), _resident((1, 256)),
                  _resident(wuq.shape), _resident(wukv.shape),
                  _resident(cos.shape), _resident(sa.shape), _resident(sb.shape)],
        out_specs=pl.BlockSpec((1, tp, 256), lambda i: (i, 0, 0)),
        out_shape=jax.ShapeDtypeStruct((b, tp, 256), BF16),
        scratch_shapes=[pltpu.VMEM((N_HEADS, tp, LANES), BF16),
                        pltpu.VMEM((N_HEADS, tp, LANES), BF16),
                        pltpu.VMEM((N_HEADS, tp, LANES), BF16)] + _attn_scratch(N_HEADS, tp),
        compiler_params=_cparams(),
        name="mla",
    )(pc, qn, kvn, wuq, wukv, cos, sa, sb)


def _diff_kernel(tp, lam_init, pd_ref, lam_ref, dn_ref, o_ref, qs_s, v_s, s_meta, s_s, mx_s, acc_s):
    n_maps = 2 * N_HEADS

    def fill_values(i, carry):
        rows = pl.ds(pl.multiple_of(i * Q_BLOCK, Q_BLOCK), Q_BLOCK)
        low = lax.broadcasted_iota(jnp.int32, (1, LANES), 1) < V_ONE
        for pair in range(N_HEADS // 2):
            two = pd_ref[0, rows, 512 + pair * LANES:512 + (pair + 1) * LANES].astype(F32)
            for h, vals in ((2 * pair, two), (2 * pair + 1, pltpu.roll(two, V_ONE, 1))):
                v_s[h, rows, :] = _with_ones_lane(jnp.where(low, vals, 0.0)).astype(BF16)
        return carry

    lax.fori_loop(0, tp // Q_BLOCK, fill_values, 0)
    scale = DIFF_DK ** -0.5 * LOG2E
    group = lax.broadcasted_iota(jnp.int32, (1, 256), 1) >> 5
    lv = lam_ref[...]
    lam = (jnp.exp(jnp.sum(lv[0:1, :] * lv[1:2, :], axis=-1, keepdims=True))
           - jnp.exp(jnp.sum(lv[2:3, :] * lv[3:4, :], axis=-1, keepdims=True)) + lam_init)
    dn = dn_ref[...]

    def stack_queries(q0, n_rows):
        q = (pd_ref[0, pl.ds(q0, n_rows), 0:256].astype(F32) * scale).astype(BF16)
        for g in range(n_maps):
            qs_s[g * n_rows:(g + 1) * n_rows, :] = jnp.where(group == g, q, jnp.zeros_like(q))

    def qk_fn(q0, n_rows, krows):
        return _dot_nt(qs_s[0:n_maps * n_rows, :], pd_ref[0, krows, 256:512])

    def v_fn(g, krows):
        return v_s[g // 2, krows, :]

    def emit(q0, n_rows, o, first):
        od = jnp.concatenate([o[(2 * h) * n_rows:(2 * h + 1) * n_rows, :]
                              - lam * o[(2 * h + 1) * n_rows:(2 * h + 2) * n_rows, :]
                              for h in range(N_HEADS)], axis=0)
        od = jnp.where(lax.broadcasted_iota(jnp.int32, (1, LANES), 1) < V_ONE, od, 0.0)
        ms = jnp.sum(od * od, axis=-1, keepdims=True) * (1.0 / 64)
        yh = od * lax.rsqrt(ms + EPS) * dn * (1.0 - lam_init)
        y = _heads_to_lanes([yh[h * n_rows:(h + 1) * n_rows, :] for h in range(N_HEADS)])
        if first:
            qrow = lax.broadcasted_iota(jnp.int32, (n_rows, 1), 0)
            y = jnp.where(qrow >= FRONT_PAD, y, 0.0)
        o_ref[0, pl.ds(q0, n_rows), :] = y.astype(BF16)

    stack_queries(0, Q_BLOCK)
    emit(0, Q_BLOCK, _attn_block0(qk_fn, v_fn, n_maps), True)

    def block(i, mid_fn):
        stack_queries(_attn_row0(i), ATT_ROWS)
        _attn_block(i, qk_fn, v_fn, n_maps, s_meta, s_s, mx_s, acc_s, mid_fn)

    def finish(i):
        emit(_attn_row0(i), ATT_ROWS, _attn_result(acc_s), False)

    _attn_all_blocks((tp - Q_BLOCK) // ATT_ROWS, block, finish)


def _diff(pd, lam_rows, dn, lam_init):
    b, tp, _ = pd.shape
    n_maps = 2 * N_HEADS
    return pl.pallas_call(
        functools.partial(_diff_kernel, tp, lam_init),
        grid=(b,),
        in_specs=[pl.BlockSpec((1, tp, PD_W), lambda i: (i, 0, 0)),
                  _resident(lam_rows.shape), _resident((1, LANES))],
        out_specs=pl.BlockSpec((1, tp, 256), lambda i: (i, 0, 0)),
        out_shape=jax.ShapeDtypeStruct((b, tp, 256), BF16),
        scratch_shapes=[pltpu.VMEM((n_maps * ATT_ROWS, 256), BF16),
                        pltpu.VMEM((N_HEADS, tp, LANES), BF16)] + _attn_scratch(n_maps, tp),
        compiler_params=_cparams(),
        name="diffattn",
    )(pd, lam_rows, dn)


def _pad_cols(x, width):
    return jnp.pad(x, ((0, 0), (0, width - x.shape[1])))


def _rot_split(w):
    d = w.shape[0]
    return w.reshape(d, N_HEADS, 2, 32).transpose(0, 2, 1, 3).reshape(d, 256)


def _layout_w_in(w):
    sizes = (128, 128, 256, 16, 256, 256, 256, 256, 256, 192, 64, 32, 256, 256, 256)
    offs = [0]
    for s_ in sizes:
        offs.append(offs[-1] + s_)
    seg = [w[:, offs[i]:offs[i + 1]] for i in range(len(sizes))]
    (a_q, a_k, a_v, a_lr, a_og, r_q, r_k, r_v, r_og, c_cq, c_ckv, c_kpe, d_q, d_k, d_v) = seg
    d = w.shape[0]
    z = lambda n: jnp.zeros((d, n), w.dtype)
    cols = [a_q, a_k, a_v, a_og,
            _rot_split(r_q), _rot_split(r_k), r_v, r_og,
            c_cq, c_ckv, a_lr, z(48), c_kpe, z(32),
            d_q, d_k, d_v]
    return jnp.concatenate(cols, axis=1).astype(BF16)


def _tables(tp):
    pos = jnp.arange(tp, dtype=F32) - FRONT_PAD
    inv = ROPE_THETA ** (-jnp.arange(32, dtype=F32) / 32)
    ang = pos[:, None] * inv[None, :]
    ret_cos = jnp.tile(jnp.cos(ang), (1, N_HEADS))
    ret_sin = jnp.tile(jnp.sin(ang), (1, N_HEADS))
    inv16 = ROPE_THETA ** (-jnp.arange(16, dtype=F32) / 16)
    ang16 = pos[:, None] * inv16[None, :]
    c16, s16 = jnp.cos(ang16), jnp.sin(ang16)
    one = lambda n: jnp.ones((tp, n), F32)
    zero = lambda n: jnp.zeros((tp, n), F32)
    mla_cos = jnp.concatenate([one(64), c16, c16, one(32)], axis=1)
    mla_sa = jnp.concatenate([zero(80), s16, zero(32)], axis=1)
    mla_sb = jnp.concatenate([zero(64), -s16, zero(48)], axis=1)
    lg = jnp.log(1.0 - jnp.exp2(-5.0 - jnp.arange(N_HEADS, dtype=F32)))
    idx = jnp.arange(RET_BLOCK, dtype=F32)
    rel = idx[:, None] - idx[None, :]
    dmask = jnp.where(rel[None] >= 0, jnp.exp(rel[None] * lg[:, None, None]), 0.0)
    dmask = dmask.reshape(N_HEADS * RET_BLOCK, RET_BLOCK)
    lane_head = (jnp.arange(256) % 128) // 32
    qfac = jnp.exp((idx[:, None] + 1.0) * lg[lane_head][None, :])
    kfac = jnp.exp((RET_BLOCK - 1.0 - idx[:, None]) * lg[lane_head][None, :])
    dec = jnp.exp(RET_BLOCK * lg[lane_head])[:, None]
    return ret_cos, ret_sin, mla_cos, mla_sa, mla_sb, dmask, qfac, kfac, dec


def kernel(x, meta_tokens, attn_norm, w_in, gla_w_gate, gla_b_gate, gla_norm, ret_norm, mla_q_norm, mla_w_uq, mla_kv_norm, mla_w_ukv, diff_lambda, diff_norm, w_out, ffn_norm, ffn_w_gate, ffn_w_up, ffn_w_down, moe_router, moe_w_gate, moe_w_up, moe_w_down, final_norm):
    b, seq, d = x.shape
    assert DEPTH % 2 == 0, "the expert layer (odd index) must be last: its combine step applies the final norm"
    tp = FRONT_PAD + N_META + seq
    n = b * tp
    meta = jnp.broadcast_to(meta_tokens[None].astype(x.dtype), (b, N_META, d))
    h = jnp.concatenate([jnp.zeros((b, FRONT_PAD, d), x.dtype), meta, x], axis=1).reshape(n, d)
    ret_cos, ret_sin, mla_cos, mla_sa, mla_sb, dmask, qfac, kfac, dec = _tables(tp)

    for li in range(DEPTH):
        pa, pb, pc, pd = _inproj(h, attn_norm[li][None, :], _layout_w_in(w_in[li]))
        pa, pb, pc, pd = (p.reshape(b, tp, -1) for p in (pa, pb, pc, pd))

        wgate = jnp.pad(gla_w_gate[li], ((0, LANES - GLA_GATE_RANK), (0, 0)))
        o_a = _gla(pa, pc, wgate, gla_b_gate[li][None, :], jnp.tile(gla_norm[li], N_HEADS)[None, :])
        o_b = _ret(pb, ret_cos, ret_sin, dmask, qfac, kfac, dec, jnp.tile(ret_norm[li], N_HEADS)[None, :])

        qn = _pad_cols(mla_q_norm[li][None, :], 256)
        kvn = jnp.pad(mla_kv_norm[li][None, :], ((0, 0), (MLA_Q_RANK, 0)))
        wuq = jnp.pad(mla_w_uq[li].reshape(MLA_Q_RANK, N_HEADS, MLA_NOPE + MLA_ROPE),
                      ((0, 256 - MLA_Q_RANK), (0, 0), (0, LANES - MLA_NOPE - MLA_ROPE)))
        wuq = wuq.reshape(256, N_HEADS * LANES).astype(BF16)
        wukv = mla_w_ukv[li].reshape(MLA_KV_RANK, N_HEADS, 2, 64)
        wukv = jnp.pad(wukv, ((MLA_Q_RANK, 0), (0, 0), (0, 0), (0, 64)))
        wukv = wukv.transpose(0, 2, 1, 3).reshape(256, 2 * N_HEADS * LANES).astype(BF16)
        o_c = _mla(pc, qn, kvn, wuq, wukv, mla_cos, mla_sa, mla_sb)

        lam_init = 0.8 - 0.6 * math.exp(-0.3 * li)
        o_d = _diff(pd, diff_lambda[li], _pad_cols(diff_norm[li][None, :], LANES), lam_init)

        o_a, o_b, o_c, o_d = (o.reshape(n, 256) for o in (o_a, o_b, o_c, o_d))
        wo = w_out[li].astype(BF16)
        fn = ffn_norm[li][None, :]
        j = li // 2
        if li % 2 == 0:
            h = _outproj_swiglu(h, o_a, o_b, o_c, o_d, wo, fn, ffn_w_gate[j].astype(BF16),
                                ffn_w_up[j].astype(BF16), ffn_w_down[j].astype(BF16))
        else:
            h, hn, info, cnt = _outproj_router(h, o_a, o_b, o_c, o_d, wo, fn,
                                               _pad_cols(moe_router[j], LANES), tp)
            return _routed_moe_norm(h, hn, info, cnt, moe_w_gate[j], moe_w_up[j], moe_w_down[j],
                                    final_norm[None, :], b, tp)
```

```python
import functools
import math

import jax
import jax.numpy as jnp
from jax import lax
from jax.experimental import pallas as pl
from jax.experimental.pallas import tpu as pltpu

F32 = jnp.float32
BF16 = jnp.bfloat16

D_MODEL = 1024
DEPTH = 2
N_META = 16
CHUNK = 64
Q_BLOCK = 128
FRONT_PAD = Q_BLOCK - N_META
EPS = 1e-6
NEG = -1e30
ROPE_THETA = 10000.0
N_HEADS = 4
GLA_DK = 32
GLA_DV = 64
GLA_GATE_RANK = 16
GLA_TAU = 16.0
RET_DK = 64
MLA_Q_RANK = 192
MLA_KV_RANK = 64
MLA_NOPE = 64
MLA_ROPE = 32
DIFF_DK = 32
N_EXPERTS = 8

LANES = 128
SUBLANES = 8
ROW_TILE = 512
FF_CHUNK = 256
PREFETCH_TAIL_CHUNKS = 4
VMEM_LIMIT = 56 * 1024 * 1024

PA_W = 768
PB_W = 1024
PC_W = 384
PD_W = 768


def _cparams(n_axes=1):
    return pltpu.CompilerParams(dimension_semantics=("arbitrary",) * n_axes,
                                vmem_limit_bytes=VMEM_LIMIT)


def _resident(shape):
    nd = len(shape)
    return pl.BlockSpec(shape, lambda *_: (0,) * nd, pipeline_mode=pl.Buffered(1))


def _sigmoid(x):
    return 1.0 / (1.0 + jnp.exp(-x))


def _split_bf16(x):
    hi = x.astype(BF16)
    lo = (x - hi.astype(F32)).astype(BF16)
    return hi, lo


def _dot(a, b):
    return jnp.dot(a, b, preferred_element_type=F32)


def _dot_nt(a, b):
    return lax.dot_general(a, b, (((1,), (1,)), ((), ())), preferred_element_type=F32)


def _inproj_kernel(h_ref, g_ref, w_ref, pa_ref, pb_ref, pc_ref, pd_ref):
    x = h_ref[...]
    ms = jnp.mean(x * x, axis=-1, keepdims=True)
    y = (x * lax.rsqrt(ms + EPS) * g_ref[...]).astype(BF16)
    off = 0
    for o_ref, width in ((pa_ref, PA_W), (pb_ref, PB_W), (pc_ref, PC_W), (pd_ref, PD_W)):
        o_ref[...] = _dot(y, w_ref[:, off:off + width]).astype(BF16)
        off += width


def _inproj(h, g, w):
    n = h.shape[0]
    wtot = PA_W + PB_W + PC_W + PD_W
    row = lambda width: pl.BlockSpec((ROW_TILE, width), lambda i: (i, 0))
    return pl.pallas_call(
        _inproj_kernel,
        grid=(n // ROW_TILE,),
        in_specs=[row(D_MODEL), _resident((1, D_MODEL)), _resident((D_MODEL, wtot))],
        out_specs=[row(PA_W), row(PB_W), row(PC_W), row(PD_W)],
        out_shape=[jax.ShapeDtypeStruct((n, w_), BF16) for w_ in (PA_W, PB_W, PC_W, PD_W)],
        compiler_params=_cparams(),
        name="inproj",
    )(h, g, w)


def _outproj_router_kernel(seq_rows, h_ref, oa_ref, ob_ref, oc_ref, od_ref, wo_ref, fn_ref, router_ref,
                           below_ref, hmid_ref, hn_ref, info_ref, cnt_ref, carry_ref):
    o = jnp.concatenate([oa_ref[...], ob_ref[...], oc_ref[...], od_ref[...]], axis=1)
    hm = h_ref[...] + _dot(o, wo_ref[...])
    hmid_ref[...] = hm
    ms = jnp.mean(hm * hm, axis=-1, keepdims=True)
    y = hm * lax.rsqrt(ms + EPS) * fn_ref[...]
    hn_ref[...] = y
    y_hi, y_lo = _split_bf16(y)
    r_hi, r_lo = _split_bf16(router_ref[...])
    logits = _dot(y_hi, r_hi) + _dot(y_hi, r_lo) + _dot(y_lo, r_hi)
    lane = lax.broadcasted_iota(jnp.int32, logits.shape, 1).astype(F32)
    ninf = float("-inf")
    logits = jnp.where(lane < N_EXPERTS, logits, ninf)
    m1 = jnp.max(logits, axis=-1, keepdims=True)
    i1 = jnp.min(jnp.where(logits == m1, lane, float(LANES)), axis=-1, keepdims=True)
    rest_l = jnp.where(lane == i1, ninf, logits)
    m2 = jnp.max(rest_l, axis=-1, keepdims=True)
    i2 = jnp.min(jnp.where(rest_l == m2, lane, float(LANES)), axis=-1, keepdims=True)
    e2 = jnp.exp(m2 - m1)
    den = 1.0 + e2

    @pl.when(pl.program_id(0) == 0)
    def _():
        carry_ref[...] = jnp.zeros_like(carry_ref)

    sel = jnp.where(lane == i1, 1.0, 0.0) + jnp.where(lane == i2, 1.0, 0.0)
    rows = sel.shape[0]
    seq_pos = (lax.rem(pl.program_id(0) * rows, seq_rows)
               + lax.broadcasted_iota(jnp.int32, (rows, 1), 0))
    seq_pos = jnp.where(seq_pos >= seq_rows, seq_pos - seq_rows, seq_pos)
    sel = jnp.where(seq_pos >= Q_BLOCK, sel, 0.0)
    count = _dot(below_ref[...], sel.astype(BF16)) + carry_ref[...]
    r1 = jnp.sum(jnp.where(lane == i1, count, 0.0), axis=-1, keepdims=True)
    r2 = jnp.sum(jnp.where(lane == i2, count, 0.0), axis=-1, keepdims=True)
    total = carry_ref[...] + jnp.sum(sel, axis=0, keepdims=True)
    carry_ref[...] = total
    cnt_ref[...] = jnp.broadcast_to(total, cnt_ref.shape)
    info = jnp.where(lane == 0, i1, 0.0)
    for k, val in enumerate((i2, 1.0 / den, e2 / den, r1, r2), start=1):
        info = jnp.where(lane == k, val, info)
    info_ref[...] = info


def _outproj_router(h, oa, ob, oc, od, wo, fn, router, seq_rows):
    n = h.shape[0]
    row = lambda width: pl.BlockSpec((ROW_TILE, width), lambda i: (i, 0))
    return pl.pallas_call(
        functools.partial(_outproj_router_kernel, seq_rows),
        grid=(n // ROW_TILE,),
        in_specs=[row(D_MODEL), row(256), row(256), row(256), row(256),
                  _resident((D_MODEL, D_MODEL)), _resident((1, D_MODEL)),
                  _resident((D_MODEL, LANES)), _resident((ROW_TILE, ROW_TILE))],
        out_specs=[row(D_MODEL), row(D_MODEL), row(LANES), pl.BlockSpec((8, LANES), lambda i: (0, 0))],
        out_shape=[jax.ShapeDtypeStruct((n, D_MODEL), F32), jax.ShapeDtypeStruct((n, D_MODEL), F32),
                   jax.ShapeDtypeStruct((n, LANES), F32), jax.ShapeDtypeStruct((8, LANES), F32)],
        scratch_shapes=[pltpu.VMEM((1, LANES), F32)],
        compiler_params=_cparams(),
        name="outproj_router",
    )(h, oa, ob, oc, od, wo, fn, router, jnp.tri(ROW_TILE, k=-1, dtype=BF16))


def _inverse_map_kernel(zs_ref, ze_ref, pos_ref, tok_ref):
    i = pl.program_id(0)

    @pl.when(i == 0)
    def _():
        def zero(j, carry):
            tok_ref[j] = 0
            return carry

        for k in range(zs_ref.shape[0]):
            lax.fori_loop(zs_ref[k], ze_ref[k], zero, 0)

    base = i * ROW_TILE

    for r in range(ROW_TILE):
        tok_ref[pos_ref[0, 0, 2 * r]] = base + r
        tok_ref[pos_ref[0, 0, 2 * r + 1]] = base + r


def _inverse_map(zero_start, zero_end, pos, n_sorted):
    return pl.pallas_call(
        _inverse_map_kernel,
        grid_spec=pltpu.PrefetchScalarGridSpec(
            num_scalar_prefetch=2,
            grid=(pos.shape[0],),
            in_specs=[pl.BlockSpec((1, 1, 2 * ROW_TILE), lambda i, zs, ze: (i, 0, 0),
                                   memory_space=pltpu.SMEM)],
            out_specs=pl.BlockSpec(memory_space=pltpu.SMEM)),
        out_shape=jax.ShapeDtypeStruct((n_sorted + ROW_TILE,), jnp.int32),
        compiler_params=_cparams(),
        name="moe_inverse_map",
    )(zero_start, zero_end, pos)


def _swiglu_tile(x, wg, wu, wd, acc_ref, per_chunk=None):
    n_chunks = wd.shape[0] // FF_CHUNK
    for c in range(n_chunks):
        cols = slice(c * FF_CHUNK, (c + 1) * FF_CHUNK)
        g = _dot(x, wg[:, cols])
        u = _dot(x, wu[:, cols])
        part = _dot((g * _sigmoid(g) * u).astype(BF16), wd[cols, :])
        if c == 0:
            acc_ref[...] = part
        elif c < n_chunks - 1:
            acc_ref[...] += part
        if per_chunk is not None:
            per_chunk(c, n_chunks)
    return acc_ref[...] + part


def _experts_kernel(te_ref, tv_ref, tok_ref, tok_next_ref, hn_ref, wg_ref, wu_ref, wd_ref, ys_ref,
                    x_buf, acc_ref, sem):
    i = pl.program_id(0)
    last = pl.num_programs(0) - 1
    slot = i & 1
    n_blocks = x_buf.shape[1]

    def row_copy(t_ref, s, rb, u):
        tok = t_ref[0, 0, rb * SUBLANES + u]
        return pltpu.make_async_copy(hn_ref.at[pl.ds(tok, 1), :], x_buf.at[s, rb, pl.ds(u, 1), :],
                                     sem.at[s])

    def wait_tile(s):
        pltpu.make_async_copy(x_buf.at[s], x_buf.at[s], sem.at[s]).wait()

    @pl.when(i == 0)
    def _():
        def start_block(rb, carry):
            for u in range(SUBLANES):
                row_copy(tok_ref, 0, rb, u).start(priority=u % 2)
            return carry

        lax.fori_loop(0, n_blocks, start_block, 0)

    @pl.when(jnp.logical_or(i == 0, tv_ref[jnp.maximum(i - 1, 0)] > 0))
    def _():
        wait_tile(slot)

    @pl.when(tv_ref[i] > 0)
    def _():
        def prefetch(c, n_chunks):
            per = -(-ROW_TILE // max(n_chunks - PREFETCH_TAIL_CHUNKS, 1))
            for r in range(c * per, min((c + 1) * per, ROW_TILE)):
                row_copy(tok_next_ref, 1 - slot, r // SUBLANES, r % SUBLANES).start(priority=r % 2)

        x = x_buf[slot].reshape(ROW_TILE, -1).astype(BF16)
        ys_ref[...] = _swiglu_tile(x, wg_ref.at[0], wu_ref.at[0], wd_ref.at[0], acc_ref, prefetch)

    @pl.when(tv_ref[i] == 0)
    def _():
        ys_ref[...] = jnp.zeros_like(ys_ref)

    @pl.when(jnp.logical_and(i == last, tv_ref[i] > 0))
    def _():
        wait_tile(1 - slot)


def _experts(tile_expert, tile_valid, tok, hn, wg, wu, wd):
    n_tiles = tok.shape[0]
    d = hn.shape[1]
    wspec = lambda shape: pl.BlockSpec((1,) + shape[1:], lambda i, te, tv: (te[i], 0, 0),
                                       pipeline_mode=pl.Buffered(1))
    tok_spec = lambda index: pl.BlockSpec((1, 1, ROW_TILE), index, memory_space=pltpu.SMEM)
    return pl.pallas_call(
        _experts_kernel,
        grid_spec=pltpu.PrefetchScalarGridSpec(
            num_scalar_prefetch=2,
            grid=(n_tiles,),
            in_specs=[tok_spec(lambda i, te, tv: (i, 0, 0)),
                      tok_spec(lambda i, te, tv: (jnp.minimum(i + 1, n_tiles - 1), 0, 0)),
                      pl.BlockSpec(memory_space=pl.ANY),
                      wspec(wg.shape), wspec(wu.shape), wspec(wd.shape)],
            out_specs=pl.BlockSpec((ROW_TILE, d), lambda i, te, tv: (i, 0)),
            scratch_shapes=[pltpu.VMEM((2, ROW_TILE // SUBLANES, SUBLANES, d), F32),
                            pltpu.VMEM((ROW_TILE, d), F32),
                            pltpu.SemaphoreType.DMA((2,))]),
        out_shape=jax.ShapeDtypeStruct((n_tiles * ROW_TILE, d), F32),
        compiler_params=_cparams(),
        name="moe_experts",
    )(tile_expert, tile_valid, tok, tok, hn, wg, wu, wd)


def _combine_norm_kernel(tp, pos_ref, h_ref, info_ref, ys_ref, g_ref, o_ref,
                         h_buf, info_buf, y_buf, sem, row_sem):
    n_rows = h_buf.shape[0]
    row0 = pl.multiple_of(pl.program_id(0) * tp + Q_BLOCK + pl.program_id(1) * n_rows, Q_BLOCK)
    h_copy = pltpu.make_async_copy(h_ref.at[pl.ds(row0, n_rows), :], h_buf, sem.at[0])
    info_copy = pltpu.make_async_copy(info_ref.at[pl.ds(row0, n_rows), :], info_buf, sem.at[1])
    h_copy.start()
    info_copy.start()

    def row_copy(rb, u, k):
        p = pos_ref[0, 0, rb * (2 * SUBLANES) + 2 * u + k]
        return pltpu.make_async_copy(ys_ref.at[pl.ds(p, 1), :], y_buf.at[k, rb, pl.ds(u, 1), :], row_sem)

    for rb in range(n_rows // SUBLANES):
        for u in range(SUBLANES):
            row_copy(rb, u, 0).start(priority=0)
            row_copy(rb, u, 1).start(priority=1)
    pltpu.make_async_copy(y_buf, y_buf, row_sem).wait()
    h_copy.wait()
    info_copy.wait()
    info = info_buf[...]
    y0 = y_buf[0].reshape(n_rows, -1)
    y1 = y_buf[1].reshape(n_rows, -1)
    x = h_buf[...] + info[:, 2:3] * y0 + info[:, 3:4] * y1
    ms = jnp.mean(x * x, axis=-1, keepdims=True)
    o_ref[0] = x * lax.rsqrt(ms + EPS) * g_ref[...]


def _combine_norm(pos, h, info, ys, g, b, tp):
    n, d = h.shape
    seq = tp - Q_BLOCK
    tiles = seq // ROW_TILE
    any_spec = pl.BlockSpec(memory_space=pl.ANY)
    return pl.pallas_call(
        functools.partial(_combine_norm_kernel, tp),
        grid=(b, tiles),
        in_specs=[pl.BlockSpec((1, 1, 2 * ROW_TILE), lambda i, j: (i * tiles + j, 0, 0),
                               memory_space=pltpu.SMEM),
                  any_spec, any_spec, any_spec, _resident((1, d))],
        out_specs=pl.BlockSpec((1, ROW_TILE, d), lambda i, j: (i, j, 0)),
        out_shape=jax.ShapeDtypeStruct((b, seq, d), F32),
        scratch_shapes=[pltpu.VMEM((ROW_TILE, d), F32), pltpu.VMEM((ROW_TILE, LANES), F32),
                        pltpu.VMEM((2, ROW_TILE // SUBLANES, SUBLANES, d), F32),
                        pltpu.SemaphoreType.DMA((2,)), pltpu.SemaphoreType.DMA(())],
        compiler_params=_cparams(2),
        name="moe_combine_norm",
    )(pos, h, info, ys, g)


def _routed_moe_norm(h, hn, info, cnt, wg, wu, wd, g, b, tp):
    n = h.shape[0]
    n_routed = b * (tp - Q_BLOCK)
    n_sorted = 2 * n_routed + N_EXPERTS * ROW_TILE
    n_tiles = n_sorted // ROW_TILE
    counts = cnt[0, :N_EXPERTS].astype(jnp.int32)
    tiles_e = (counts + ROW_TILE - 1) // ROW_TILE
    tile_end = jnp.cumsum(tiles_e)
    row_base = (tile_end - tiles_e) * ROW_TILE
    e_idx = info[:, 0:2].astype(jnp.int32)
    pos = row_base[e_idx] + info[:, 4:6].astype(jnp.int32)
    routed = (jnp.arange(n, dtype=jnp.int32) % tp >= Q_BLOCK)[:, None]
    zero_start = jnp.concatenate([row_base + counts, tile_end[-1:] * ROW_TILE])
    zero_end = jnp.concatenate([tile_end * ROW_TILE, jnp.full((1,), n_sorted + ROW_TILE, jnp.int32)])
    tok = _inverse_map(zero_start, zero_end,
                       jnp.where(routed, pos, n_sorted).reshape(n // ROW_TILE, 1, 2 * ROW_TILE),
                       n_sorted)[:n_sorted]
    tile_ids = jnp.arange(n_tiles, dtype=jnp.int32)
    tile_expert = jnp.minimum(jnp.sum((tile_ids[:, None] >= tile_end[None, :]).astype(jnp.int32), axis=1),
                              N_EXPERTS - 1)
    tile_valid = (tile_ids < tile_end[-1]).astype(jnp.int32)
    ys = _experts(tile_expert, tile_valid, tok.reshape(n_tiles, 1, ROW_TILE), hn, wg, wu, wd)
    pos_seq = pos.reshape(b, tp, 2)[:, Q_BLOCK:, :].reshape(-1, 1, 2 * ROW_TILE)
    return _combine_norm(pos_seq, h, info, ys, g, b, tp)


def _outproj_swiglu_kernel(h_ref, oa_ref, ob_ref, oc_ref, od_ref, wo_ref, fn_ref, wg_ref, wu_ref, wd_ref,
                           o_ref, acc_ref):
    o = jnp.concatenate([oa_ref[...], ob_ref[...], oc_ref[...], od_ref[...]], axis=1)
    hm = h_ref[...] + _dot(o, wo_ref[...])
    ms = jnp.mean(hm * hm, axis=-1, keepdims=True)
    hn = (hm * lax.rsqrt(ms + EPS) * fn_ref[...]).astype(BF16)
    o_ref[...] = hm + _swiglu_tile(hn, wg_ref, wu_ref, wd_ref, acc_ref)


def _outproj_swiglu(h, oa, ob, oc, od, wo, fn, wg, wu, wd):
    n = h.shape[0]
    row = lambda width: pl.BlockSpec((ROW_TILE, width), lambda i: (i, 0))
    return pl.pallas_call(
        _outproj_swiglu_kernel,
        grid=(n // ROW_TILE,),
        in_specs=[row(D_MODEL), row(256), row(256), row(256), row(256),
                  _resident(wo.shape), _resident(fn.shape),
                  _resident(wg.shape), _resident(wu.shape), _resident(wd.shape)],
        out_specs=row(D_MODEL),
        out_shape=jax.ShapeDtypeStruct((n, D_MODEL), F32),
        scratch_shapes=[pltpu.VMEM((ROW_TILE, D_MODEL), F32)],
        input_output_aliases={0: 0},
        compiler_params=_cparams(),
        name="outproj_swiglu",
    )(h, oa, ob, oc, od, wo, fn, wg, wu, wd)


def _group_ones(n, group_shift):
    r = lax.broadcasted_iota(jnp.int32, (n, n), 0) >> group_shift
    c = lax.broadcasted_iota(jnp.int32, (n, n), 1) >> group_shift
    return jnp.where(r == c, 1.0, 0.0).astype(BF16)


GLA_ROWS = 256


def _gla_kernel(tp, pa_ref, lr_ref, wg_ref, bg_ref, gn_ref, o_ref, s_ref):
    qhead = lax.broadcasted_iota(jnp.int32, (1, N_HEADS * GLA_DK), 1) >> 5
    ehead = lax.broadcasted_iota(jnp.int32, (1, N_HEADS * GLA_DV), 1) >> 6
    shead = lax.broadcasted_iota(jnp.int32, (N_HEADS * GLA_DK, 1), 0) >> 5
    bd = shead == ehead
    gsum = _group_ones(N_HEADS * GLA_DV, 6)
    wg_hi, wg_lo = _split_bf16(wg_ref[...])
    bg = bg_ref[...]
    gn = gn_ref[...]
    scale = GLA_DK ** -0.5
    s_ref[...] = jnp.zeros_like(s_ref)

    def block(r0, n_rows):
        n_c = n_rows // CHUNK
        rows = pl.ds(r0, n_rows)
        ri = lax.broadcasted_iota(jnp.int32, (n_rows, n_rows), 0)
        ci = lax.broadcasted_iota(jnp.int32, (n_rows, n_rows), 1)
        same = (ri >> 6) == (ci >> 6)
        tri_bf = jnp.where(jnp.logical_and(same, ri >= ci), 1.0, 0.0).astype(BF16)
        ones_bf = jnp.where(same, 1.0, 0.0).astype(BF16)
        r4 = lax.broadcasted_iota(jnp.int32, (N_HEADS * n_rows, n_rows), 0) & (n_rows - 1)
        c4 = lax.broadcasted_iota(jnp.int32, (N_HEADS * n_rows, n_rows), 1)
        tri4 = jnp.logical_and((r4 >> 6) == (c4 >> 6), r4 >= c4)
        col_chunk = lax.broadcasted_iota(jnp.int32, (1, n_rows), 1) >> 6

        q = pa_ref[0, rows, 0:128].astype(F32) * scale
        k = pa_ref[0, rows, 128:256].astype(F32)
        v = pa_ref[0, rows, 256:512]
        og = pa_ref[0, rows, 512:768].astype(F32)
        lr = lr_ref[0, rows, :]
        valid = (r0 + lax.broadcasted_iota(jnp.int32, (n_rows, 1), 0)) >= FRONT_PAD

        pre = _dot(lr, wg_hi) + _dot(lr, wg_lo) + bg
        logsig = jnp.minimum(pre, 0.0) - jnp.log1p(jnp.exp(-jnp.abs(pre)))
        g = jnp.where(valid, logsig * (1.0 / GLA_TAU), 0.0)
        g_hi, g_lo = _split_bf16(g)
        cum = _dot(tri_bf, g_hi) + _dot(tri_bf, g_lo)
        cum_end = _dot(ones_bf, g_hi) + _dot(ones_bf, g_lo)
        qt_bf = (q * jnp.exp(cum)).astype(BF16)
        kt = (k * jnp.exp(-cum)).astype(BF16)
        kd_t = (k * jnp.exp(cum_end - cum)).T.astype(BF16)
        dec_t = jnp.exp(cum_end.T)

        qs = jnp.concatenate([jnp.where(qhead == h, qt_bf, jnp.zeros_like(qt_bf))
                              for h in range(N_HEADS)], axis=0)
        a = jnp.where(tri4, _dot_nt(qs, kt), 0.0).astype(BF16)
        r = _dot(a, v)
        o = jnp.where(ehead == 0, r[0:n_rows, :], 0.0)
        for h in range(1, N_HEADS):
            o = o + jnp.where(ehead == h, r[h * n_rows:(h + 1) * n_rows, :], 0.0)

        s = s_ref[...]
        inter = []
        for c in range(n_c):
            inter.append(_dot(qt_bf[c * CHUNK:(c + 1) * CHUNK, :], s.astype(BF16)))
            upd = _dot(jnp.where(col_chunk == c, kd_t, jnp.zeros_like(kd_t)), v)
            s = dec_t[:, c * CHUNK:c * CHUNK + 1] * s + jnp.where(bd, upd, 0.0)
        s_ref[...] = s
        o = o + jnp.concatenate(inter, axis=0)

        ms = _dot((o * o).astype(BF16), gsum) * (1.0 / GLA_DV)
        y = o * lax.rsqrt(ms + EPS) * gn * (og * _sigmoid(og))
        o_ref[0, rows, :] = jnp.where(valid, y, 0.0).astype(BF16)

    block(0, Q_BLOCK)

    def body(i, carry):
        block(pl.multiple_of(Q_BLOCK + i * GLA_ROWS, Q_BLOCK), GLA_ROWS)
        return carry

    lax.fori_loop(0, (tp - Q_BLOCK) // GLA_ROWS, body, 0)


def _gla(pa, pc, wg, bg, gn):
    b, tp, _ = pa.shape
    return pl.pallas_call(
        functools.partial(_gla_kernel, tp),
        grid=(b,),
        in_specs=[pl.BlockSpec((1, tp, PA_W), lambda i: (i, 0, 0)),
                  pl.BlockSpec((1, tp, LANES), lambda i: (i, 0, 2)),
                  _resident((LANES, LANES)), _resident((1, LANES)), _resident((1, 256))],
        out_specs=pl.BlockSpec((1, tp, 256), lambda i: (i, 0, 0)),
        out_shape=jax.ShapeDtypeStruct((b, tp, 256), BF16),
        scratch_shapes=[pltpu.VMEM((N_HEADS * GLA_DK, N_HEADS * GLA_DV), F32)],
        compiler_params=_cparams(),
        name="gla",
    )(pa, pc, wg, bg, gn)


RET_BLOCK = 128


def _ret_kernel(tp, pb_ref, cos_ref, sin_ref, dmask_ref, qfac_ref, kfac_ref, dec_ref, gn_ref,
                o_ref, s_ref):
    blk = RET_BLOCK
    n_blocks = tp // blk
    qhead = (lax.broadcasted_iota(jnp.int32, (1, 256), 1) & 127) >> 5
    ehead = lax.broadcasted_iota(jnp.int32, (1, 256), 1) >> 6
    shead = (lax.broadcasted_iota(jnp.int32, (256, 1), 0) & 127) >> 5
    bd = shead == ehead
    gsum = _group_ones(256, 6)
    gn = gn_ref[...]
    s_ref[...] = jnp.zeros_like(s_ref)

    def rope(x, cos, sin):
        x1, x2 = x[:, :128], x[:, 128:]
        return jnp.concatenate([x1 * cos - x2 * sin, x1 * sin + x2 * cos], axis=1)

    def block(j, carry):
        r0 = pl.multiple_of(j * blk, blk)
        rows = pl.ds(r0, blk)
        cos = cos_ref[rows, :]
        sin = sin_ref[rows, :]
        q = rope(pb_ref[0, rows, 0:256].astype(F32), cos, sin)
        k = rope(pb_ref[0, rows, 256:512].astype(F32), cos, sin) * (RET_DK ** -0.5)
        v = pb_ref[0, rows, 512:768]
        og = pb_ref[0, rows, 768:1024].astype(F32)
        valid = (r0 + lax.broadcasted_iota(jnp.int32, (blk, 1), 0)) >= FRONT_PAD

        q_bf = q.astype(BF16)
        qs = jnp.concatenate([jnp.where(qhead == h, q_bf, jnp.zeros_like(q_bf))
                              for h in range(N_HEADS)], axis=0)
        a = (_dot_nt(qs, k.astype(BF16)) * dmask_ref[...]).astype(BF16)
        r = _dot(a, v)
        o = _dot((q * qfac_ref[...]).astype(BF16), s_ref[...].astype(BF16))
        for h in range(N_HEADS):
            o = o + jnp.where(ehead == h, r[h * blk:(h + 1) * blk, :], 0.0)

        kd_t = (k * kfac_ref[...]).T.astype(BF16)
        upd = _dot(kd_t, v)
        s_ref[...] = dec_ref[...] * s_ref[...] + jnp.where(bd, upd, 0.0)

        mu = _dot(o.astype(BF16), gsum) * (1.0 / 64)
        xc = o - mu
        var = _dot((xc * xc).astype(BF16), gsum) * (1.0 / 64)
        y = xc * lax.rsqrt(var + EPS) * gn * (og * _sigmoid(og))
        o_ref[0, rows, :] = jnp.where(valid, y, 0.0).astype(BF16)
        return carry

    lax.fori_loop(0, n_blocks, block, 0, unroll=True)


def _ret(pb, cos, sin, dmask, qfac, kfac, dec, gn):
    b, tp, _ = pb.shape
    return pl.pallas_call(
        functools.partial(_ret_kernel, tp),
        grid=(b,),
        in_specs=[pl.BlockSpec((1, tp, PB_W), lambda i: (i, 0, 0)),
                  _resident(cos.shape), _resident(sin.shape), _resident(dmask.shape),
                  _resident(qfac.shape), _resident(kfac.shape), _resident(dec.shape),
                  _resident((1, 256))],
        out_specs=pl.BlockSpec((1, tp, 256), lambda i: (i, 0, 0)),
        out_shape=jax.ShapeDtypeStruct((b, tp, 256), BF16),
        scratch_shapes=[pltpu.VMEM((256, 256), F32)],
        compiler_params=_cparams(),
        name="retention",
    )(pb, cos, sin, dmask, qfac, kfac, dec, gn)


ATT_ROWS = 256
LOG2E = 1.4426950408889634
V_ONE = 64


def _aligned(x, m):
    return x if isinstance(x, int) else pl.multiple_of(x, m)


def _for_chunks(n, body):
    if isinstance(n, int):
        for j in range(n):
            body(j)
        return

    def quad(t, carry):
        for u in range(4):
            body(4 * t + u)
        return carry

    lax.fori_loop(0, n >> 2, quad, 0)
    done = (n >> 2) << 2

    @pl.when((n & 2) != 0)
    def _():
        body(done)
        body(done + 1)

    @pl.when((n & 1) != 0)
    def _():
        body(done + (n & 2))


def _mask_groups(mask, s, n_g):
    rows = s.shape[0] // n_g
    return jnp.concatenate([jnp.where(mask, s[g * rows:(g + 1) * rows, :], NEG)
                            for g in range(n_g)], axis=0)


def _attn_block0(qk_fn, v_fn, n_g):
    rows = pl.ds(0, Q_BLOCK)
    qrow = lax.broadcasted_iota(jnp.int32, (Q_BLOCK, 1), 0)
    kcol = lax.broadcasted_iota(jnp.int32, (1, Q_BLOCK), 1)
    mask = jnp.logical_and(kcol <= qrow, kcol >= FRONT_PAD)
    s = _mask_groups(mask, qk_fn(0, Q_BLOCK, rows), n_g)
    p_bf = jnp.exp2(s - jnp.max(s, axis=-1, keepdims=True)).astype(BF16)
    pv = jnp.concatenate([_dot(p_bf[g * Q_BLOCK:(g + 1) * Q_BLOCK, :], v_fn(g, rows))
                          for g in range(n_g)], axis=0)
    return pv / pv[:, V_ONE:V_ONE + 1]


def _attn_row0(i):
    return _aligned(Q_BLOCK + (i - 1) * ATT_ROWS, Q_BLOCK)


def _attn_result(acc_s):
    acc = acc_s[...]
    return acc / acc[:, V_ONE:V_ONE + 1]


def _attn_block(i, qk_fn, v_fn, n_g, s_meta, s_s, mx_s, acc_s, mid_fn=None):
    rr = ATT_ROWS
    q0 = _attn_row0(i)
    meta_rows = pl.ds(0, Q_BLOCK)
    kcol = lax.broadcasted_iota(jnp.int32, (1, Q_BLOCK), 1)

    def key_rows(j):
        return pl.ds(_aligned(Q_BLOCK + j * rr, Q_BLOCK), rr)

    s = jnp.where(kcol >= FRONT_PAD, qk_fn(q0, rr, meta_rows), NEG)
    s_meta[...] = s
    mx_s[...] = s

    def pass1(j):
        sj = qk_fn(q0, rr, key_rows(j))
        s_s[j] = sj
        mx_s[...] = jnp.maximum(mx_s[...], jnp.maximum(sj[:, :LANES], sj[:, LANES:]))

    _for_chunks(i - 1, pass1)
    if mid_fn is not None:
        mid_fn()
    causal = (lax.broadcasted_iota(jnp.int32, (rr, rr), 1)
              <= lax.broadcasted_iota(jnp.int32, (rr, rr), 0))
    sd = _mask_groups(causal, qk_fn(q0, rr, pl.ds(q0, rr)), n_g)
    s_s[i - 1] = sd
    m = jnp.max(jnp.maximum(mx_s[...], jnp.maximum(sd[:, :LANES], sd[:, LANES:])),
                axis=-1, keepdims=True)
    mx_s[...] = jnp.broadcast_to(m, mx_s.shape)

    p_bf = jnp.exp2(s_meta[...] - mx_s[...]).astype(BF16)
    for g in range(n_g):
        acc_s[g * rr:(g + 1) * rr, :] = _dot(p_bf[g * rr:(g + 1) * rr, :], v_fn(g, meta_rows))

    def pass2(j):
        sj = s_s[j]
        mrep = mx_s[...]
        p0 = jnp.exp2(sj[:, :LANES] - mrep)
        p1 = jnp.exp2(sj[:, LANES:] - mrep)
        pj = jnp.concatenate([p0.astype(BF16), p1.astype(BF16)], axis=1)
        for g in range(n_g):
            acc_s[g * rr:(g + 1) * rr, :] += _dot(pj[g * rr:(g + 1) * rr, :], v_fn(g, key_rows(j)))

    _for_chunks(i, pass2)


def _attn_all_blocks(n_q, block_fn, finish_fn):
    block_fn(1, None)

    def body(i, carry):
        block_fn(i, lambda: finish_fn(i - 1))
        return carry

    lax.fori_loop(2, n_q + 1, body, 0)
    finish_fn(n_q)


def _heads_to_lanes(per_head):
    low = lax.broadcasted_iota(jnp.int32, (1, LANES), 1) < 64
    lo = jnp.where(low, per_head[0], pltpu.roll(per_head[1], 64, 1))
    hi = jnp.where(low, per_head[2], pltpu.roll(per_head[3], 64, 1))
    return jnp.concatenate([lo, hi], axis=1)


def _with_ones_lane(v):
    lane = lax.broadcasted_iota(jnp.int32, (1, LANES), 1)
    return jnp.where(lane == V_ONE, jnp.ones_like(v), v)


def _mla_kernel(tp, pc_ref, qn_ref, kvn_ref, wuq_ref, wukv_ref, cos_ref, sa_ref, sb_ref,
                o_ref, q_s, k_s, v_s, s_meta, s_s, mx_s, acc_s):
    scale = (MLA_NOPE + MLA_ROPE) ** -0.5 * LOG2E
    is_q = lax.broadcasted_iota(jnp.int32, (1, 256), 1) < MLA_Q_RANK

    def prep(r0, n_rows):
        rows = pl.ds(r0, n_rows)
        x = pc_ref[0, rows, 0:256].astype(F32)
        x2 = x * x
        ms_q = jnp.sum(jnp.where(is_q, x2, 0.0), axis=-1, keepdims=True) * (1.0 / MLA_Q_RANK)
        ms_kv = jnp.sum(jnp.where(is_q, 0.0, x2), axis=-1, keepdims=True) * (1.0 / MLA_KV_RANK)
        yq = (x * lax.rsqrt(ms_q + EPS) * qn_ref[...]).astype(BF16)
        ykv = (x * lax.rsqrt(ms_kv + EPS) * kvn_ref[...]).astype(BF16)
        cq = _dot(yq, wuq_ref[...])
        kv = _dot(ykv, wukv_ref[...])
        cos = cos_ref[rows, :]
        sa = sa_ref[rows, :]
        sb = sb_ref[rows, :]

        def rope(t):
            return t * cos + pltpu.roll(t, 16, 1) * sa + pltpu.roll(t, LANES - 16, 1) * sb

        kpe_in = pc_ref[0, rows, 256:384].astype(F32)
        kpe = rope(jnp.where(lax.broadcasted_iota(jnp.int32, (1, LANES), 1) >= 64, kpe_in, 0.0))
        for h in range(N_HEADS):
            q_s[h, rows, :] = (rope(cq[:, h * LANES:(h + 1) * LANES]) * scale).astype(BF16)
            k_s[h, rows, :] = (kv[:, h * LANES:(h + 1) * LANES] + kpe).astype(BF16)
            v_s[h, rows, :] = _with_ones_lane(
                kv[:, (N_HEADS + h) * LANES:(N_HEADS + h + 1) * LANES]).astype(BF16)

    prep(0, Q_BLOCK)

    def prep_body(i, carry):
        prep(pl.multiple_of(Q_BLOCK + i * ATT_ROWS, Q_BLOCK), ATT_ROWS)
        return carry

    lax.fori_loop(0, (tp - Q_BLOCK) // ATT_ROWS, prep_body, 0)

    def qk_fn(q0, n_rows, krows):
        return jnp.concatenate([_dot_nt(q_s[h, pl.ds(q0, n_rows), :], k_s[h, krows, :])
                                for h in range(N_HEADS)], axis=0)

    def v_fn(h, krows):
        return v_s[h, krows, :]

    def emit(q0, n_rows, o, first):
        y = _heads_to_lanes([o[h * n_rows:(h + 1) * n_rows, :] for h in range(N_HEADS)])
        if first:
            qrow = lax.broadcasted_iota(jnp.int32, (n_rows, 1), 0)
            y = jnp.where(qrow >= FRONT_PAD, y, 0.0)
        o_ref[0, pl.ds(q0, n_rows), :] = y.astype(BF16)

    emit(0, Q_BLOCK, _attn_block0(qk_fn, v_fn, N_HEADS), True)

    def block(i, mid_fn):
        _attn_block(i, qk_fn, v_fn, N_HEADS, s_meta, s_s, mx_s, acc_s, mid_fn)

    def finish(i):
        emit(_attn_row0(i), ATT_ROWS, _attn_result(acc_s), False)

    _attn_all_blocks((tp - Q_BLOCK) // ATT_ROWS, block, finish)


def _attn_scratch(n_g, tp):
    g_rows = n_g * ATT_ROWS
    n_slots = (tp - Q_BLOCK) // ATT_ROWS
    return [pltpu.VMEM((g_rows, LANES), F32),
            pltpu.VMEM((n_slots, g_rows, ATT_ROWS), F32),
            pltpu.VMEM((g_rows, LANES), F32),
            pltpu.VMEM((g_rows, LANES), F32)]


def _mla(pc, qn, kvn, wuq, wukv, cos, sa, sb):
    b, tp, _ = pc.shape
    return pl.pallas_call(
        functools.partial(_mla_kernel, tp),
        grid=(b,),
        in_specs=[pl.BlockSpec((1, tp, PC_W), lambda i: (i, 0, 0)),
                  _resident((1, 256)), _resident((1, 256)),
                  _resident(wuq.shape), _resident(wukv.shape),
                  _resident(cos.shape), _resident(sa.shape), _resident(sb.shape)],
        out_specs=pl.BlockSpec((1, tp, 256), lambda i: (i, 0, 0)),
        out_shape=jax.ShapeDtypeStruct((b, tp, 256), BF16),
        scratch_shapes=[pltpu.VMEM((N_HEADS, tp, LANES), BF16),
                        pltpu.VMEM((N_HEADS, tp, LANES), BF16),
                        pltpu.VMEM((N_HEADS, tp, LANES), BF16)] + _attn_scratch(N_HEADS, tp),
        compiler_params=_cparams(),
        name="mla",
    )(pc, qn, kvn, wuq, wukv, cos, sa, sb)


def _diff_kernel(tp, lam_init, pd_ref, lam_ref, dn_ref, o_ref, qs_s, v_s, s_meta, s_s, mx_s, acc_s):
    n_maps = 2 * N_HEADS

    def fill_values(i, carry):
        rows = pl.ds(pl.multiple_of(i * Q_BLOCK, Q_BLOCK), Q_BLOCK)
        low = lax.broadcasted_iota(jnp.int32, (1, LANES), 1) < V_ONE
        for pair in range(N_HEADS // 2):
            two = pd_ref[0, rows, 512 + pair * LANES:512 + (pair + 1) * LANES].astype(F32)
            for h, vals in ((2 * pair, two), (2 * pair + 1, pltpu.roll(two, V_ONE, 1))):
                v_s[h, rows, :] = _with_ones_lane(jnp.where(low, vals, 0.0)).astype(BF16)
        return carry

    lax.fori_loop(0, tp // Q_BLOCK, fill_values, 0)
    scale = DIFF_DK ** -0.5 * LOG2E
    group = lax.broadcasted_iota(jnp.int32, (1, 256), 1) >> 5
    lv = lam_ref[...]
    lam = (jnp.exp(jnp.sum(lv[0:1, :] * lv[1:2, :], axis=-1, keepdims=True))
           - jnp.exp(jnp.sum(lv[2:3, :] * lv[3:4, :], axis=-1, keepdims=True)) + lam_init)
    dn = dn_ref[...]

    def stack_queries(q0, n_rows):
        q = (pd_ref[0, pl.ds(q0, n_rows), 0:256].astype(F32) * scale).astype(BF16)
        for g in range(n_maps):
            qs_s[g * n_rows:(g + 1) * n_rows, :] = jnp.where(group == g, q, jnp.zeros_like(q))

    def qk_fn(q0, n_rows, krows):
        return _dot_nt(qs_s[0:n_maps * n_rows, :], pd_ref[0, krows, 256:512])

    def v_fn(g, krows):
        return v_s[g // 2, krows, :]

    def emit(q0, n_rows, o, first):
        od = jnp.concatenate([o[(2 * h) * n_rows:(2 * h + 1) * n_rows, :]
                              - lam * o[(2 * h + 1) * n_rows:(2 * h + 2) * n_rows, :]
                              for h in range(N_HEADS)], axis=0)
        od = jnp.where(lax.broadcasted_iota(jnp.int32, (1, LANES), 1) < V_ONE, od, 0.0)
        ms = jnp.sum(od * od, axis=-1, keepdims=True) * (1.0 / 64)
        yh = od * lax.rsqrt(ms + EPS) * dn * (1.0 - lam_init)
        y = _heads_to_lanes([yh[h * n_rows:(h + 1) * n_rows, :] for h in range(N_HEADS)])
        if first:
            qrow = lax.broadcasted_iota(jnp.int32, (n_rows, 1), 0)
            y = jnp.where(qrow >= FRONT_PAD, y, 0.0)
        o_ref[0, pl.ds(q0, n_rows), :] = y.astype(BF16)

    stack_queries(0, Q_BLOCK)
    emit(0, Q_BLOCK, _attn_block0(qk_fn, v_fn, n_maps), True)

    def block(i, mid_fn):
        stack_queries(_attn_row0(i), ATT_ROWS)
        _attn_block(i, qk_fn, v_fn, n_maps, s_meta, s_s, mx_s, acc_s, mid_fn)

    def finish(i):
        emit(_attn_row0(i), ATT_ROWS, _attn_result(acc_s), False)

    _attn_all_blocks((tp - Q_BLOCK) // ATT_ROWS, block, finish)


def _diff(pd, lam_rows, dn, lam_init):
    b, tp, _ = pd.shape
    n_maps = 2 * N_HEADS
    return pl.pallas_call(
        functools.partial(_diff_kernel, tp, lam_init),
        grid=(b,),
        in_specs=[pl.BlockSpec((1, tp, PD_W), lambda i: (i, 0, 0)),
                  _resident(lam_rows.shape), _resident((1, LANES))],
        out_specs=pl.BlockSpec((1, tp, 256), lambda i: (i, 0, 0)),
        out_shape=jax.ShapeDtypeStruct((b, tp, 256), BF16),
        scratch_shapes=[pltpu.VMEM((n_maps * ATT_ROWS, 256), BF16),
                        pltpu.VMEM((N_HEADS, tp, LANES), BF16)] + _attn_scratch(n_maps, tp),
        compiler_params=_cparams(),
        name="diffattn",
    )(pd, lam_rows, dn)


def _pad_cols(x, width):
    return jnp.pad(x, ((0, 0), (0, width - x.shape[1])))


def _rot_split(w):
    d = w.shape[0]
    return w.reshape(d, N_HEADS, 2, 32).transpose(0, 2, 1, 3).reshape(d, 256)


def _layout_w_in(w):
    sizes = (128, 128, 256, 16, 256, 256, 256, 256, 256, 192, 64, 32, 256, 256, 256)
    offs = [0]
    for s_ in sizes:
        offs.append(offs[-1] + s_)
    seg = [w[:, offs[i]:offs[i + 1]] for i in range(len(sizes))]
    (a_q, a_k, a_v, a_lr, a_og, r_q, r_k, r_v, r_og, c_cq, c_ckv, c_kpe, d_q, d_k, d_v) = seg
    d = w.shape[0]
    z = lambda n: jnp.zeros((d, n), w.dtype)
    cols = [a_q, a_k, a_v, a_og,
            _rot_split(r_q), _rot_split(r_k), r_v, r_og,
            c_cq, c_ckv, a_lr, z(48), c_kpe, z(32),
            d_q, d_k, d_v]
    return jnp.concatenate(cols, axis=1).astype(BF16)


def _tables(tp):
    pos = jnp.arange(tp, dtype=F32) - FRONT_PAD
    inv = ROPE_THETA ** (-jnp.arange(32, dtype=F32) / 32)
    ang = pos[:, None] * inv[None, :]
    ret_cos = jnp.tile(jnp.cos(ang), (1, N_HEADS))
    ret_sin = jnp.tile(jnp.sin(ang), (1, N_HEADS))
    inv16 = ROPE_THETA ** (-jnp.arange(16, dtype=F32) / 16)
    ang16 = pos[:, None] * inv16[None, :]
    c16, s16 = jnp.cos(ang16), jnp.sin(ang16)
    one = lambda n: jnp.ones((tp, n), F32)
    zero = lambda n: jnp.zeros((tp, n), F32)
    mla_cos = jnp.concatenate([one(64), c16, c16, one(32)], axis=1)
    mla_sa = jnp.concatenate([zero(80), s16, zero(32)], axis=1)
    mla_sb = jnp.concatenate([zero(64), -s16, zero(48)], axis=1)
    lg = jnp.log(1.0 - jnp.exp2(-5.0 - jnp.arange(N_HEADS, dtype=F32)))
    idx = jnp.arange(RET_BLOCK, dtype=F32)
    rel = idx[:, None] - idx[None, :]
    dmask = jnp.where(rel[None] >= 0, jnp.exp(rel[None] * lg[:, None, None]), 0.0)
    dmask = dmask.reshape(N_HEADS * RET_BLOCK, RET_BLOCK)
    lane_head = (jnp.arange(256) % 128) // 32
    qfac = jnp.exp((idx[:, None] + 1.0) * lg[lane_head][None, :])
    kfac = jnp.exp((RET_BLOCK - 1.0 - idx[:, None]) * lg[lane_head][None, :])
    dec = jnp.exp(RET_BLOCK * lg[lane_head])[:, None]
    return ret_cos, ret_sin, mla_cos, mla_sa, mla_sb, dmask, qfac, kfac, dec


def kernel(x, meta_tokens, attn_norm, w_in, gla_w_gate, gla_b_gate, gla_norm, ret_norm, mla_q_norm, mla_w_uq, mla_kv_norm, mla_w_ukv, diff_lambda, diff_norm, w_out, ffn_norm, ffn_w_gate, ffn_w_up, ffn_w_down, moe_router, moe_w_gate, moe_w_up, moe_w_down, final_norm):
    b, seq, d = x.shape
    assert DEPTH % 2 == 0, "the expert layer (odd index) must be last: its combine step applies the final norm"
    tp = FRONT_PAD + N_META + seq
    n = b * tp
    meta = jnp.broadcast_to(meta_tokens[None].astype(x.dtype), (b, N_META, d))
    h = jnp.concatenate([jnp.zeros((b, FRONT_PAD, d), x.dtype), meta, x], axis=1).reshape(n, d)
    ret_cos, ret_sin, mla_cos, mla_sa, mla_sb, dmask, qfac, kfac, dec = _tables(tp)

    for li in range(DEPTH):
        pa, pb, pc, pd = _inproj(h, attn_norm[li][None, :], _layout_w_in(w_in[li]))
        pa, pb, pc, pd = (p.reshape(b, tp, -1) for p in (pa, pb, pc, pd))

        wgate = jnp.pad(gla_w_gate[li], ((0, LANES - GLA_GATE_RANK), (0, 0)))
        o_a = _gla(pa, pc, wgate, gla_b_gate[li][None, :], jnp.tile(gla_norm[li], N_HEADS)[None, :])
        o_b = _ret(pb, ret_cos, ret_sin, dmask, qfac, kfac, dec, jnp.tile(ret_norm[li], N_HEADS)[None, :])

        qn = _pad_cols(mla_q_norm[li][None, :], 256)
        kvn = jnp.pad(mla_kv_norm[li][None, :], ((0, 0), (MLA_Q_RANK, 0)))
        wuq = jnp.pad(mla_w_uq[li].reshape(MLA_Q_RANK, N_HEADS, MLA_NOPE + MLA_ROPE),
                      ((0, 256 - MLA_Q_RANK), (0, 0), (0, LANES - MLA_NOPE - MLA_ROPE)))
        wuq = wuq.reshape(256, N_HEADS * LANES).astype(BF16)
        wukv = mla_w_ukv[li].reshape(MLA_KV_RANK, N_HEADS, 2, 64)
        wukv = jnp.pad(wukv, ((MLA_Q_RANK, 0), (0, 0), (0, 0), (0, 64)))
        wukv = wukv.transpose(0, 2, 1, 3).reshape(256, 2 * N_HEADS * LANES).astype(BF16)
        o_c = _mla(pc, qn, kvn, wuq, wukv, mla_cos, mla_sa, mla_sb)

        lam_init = 0.8 - 0.6 * math.exp(-0.3 * li)
        o_d = _diff(pd, diff_lambda[li], _pad_cols(diff_norm[li][None, :], LANES), lam_init)

        o_a, o_b, o_c, o_d = (o.reshape(n, 256) for o in (o_a, o_b, o_c, o_d))
        wo = w_out[li].astype(BF16)
        fn = ffn_norm[li][None, :]
        j = li // 2
        if li % 2 == 0:
            h = _outproj_swiglu(h, o_a, o_b, o_c, o_d, wo, fn, ffn_w_gate[j].astype(BF16),
                                ffn_w_up[j].astype(BF16), ffn_w_down[j].astype(BF16))
        else:
            h, hn, info, cnt = _outproj_router(h, o_a, o_b, o_c, o_d, wo, fn,
                                               _pad_cols(moe_router[j], LANES), tp)
            return _routed_moe_norm(h, hn, info, cnt, moe_w_gate[j].astype(BF16),
                                    moe_w_up[j].astype(BF16), moe_w_down[j].astype(BF16),
                                    final_norm[None, :], b, tp)
```
